```python
import jax, jax.numpy as jnp
from jax import lax
import numpy as np

D_MODEL = 1024
BATCH = 8
SEQ = 4096
DEPTH = 1

N_HEADS = 16
HEAD_DIM = 64
ATTN_WIDTH = N_HEADS * HEAD_DIM
CONV_WIDTH = D_MODEL
CONV_K = 31
D_FF = 4 * D_MODEL
Q_BLOCK = 128
N_ADA = 6
NORM_EPS = 1e-6

IN_COLS = (ATTN_WIDTH, ATTN_WIDTH, ATTN_WIDTH, N_HEADS, 2 * CONV_WIDTH, 2 * D_MODEL)
IN_SPLITS = tuple(int(s) for s in np.cumsum(IN_COLS)[:-1])
D_IN = int(sum(IN_COLS))

kernel_name = "hybrid_fox_conformer_gated_block"


def rms_norm(x, g):
    xf = x.astype(jnp.float32)
    y = xf * lax.rsqrt(jnp.mean(xf * xf, axis=-1, keepdims=True) + NORM_EPS)
    return (y * g.astype(jnp.float32)).astype(x.dtype)


def layer_norm(x, g, b):
    xf = x.astype(jnp.float32)
    mu = jnp.mean(xf, axis=-1, keepdims=True)
    var = jnp.mean(jnp.square(xf - mu), axis=-1, keepdims=True)
    y = (xf - mu) * lax.rsqrt(var + NORM_EPS)
    return (y * g.astype(jnp.float32) + b.astype(jnp.float32)).astype(x.dtype)


def forgetting_attention(q, k, v, log_f):
    B, S, H, Dh = q.shape
    nb = S // Q_BLOCK
    scale = Dh ** -0.5
    f_cum = jnp.cumsum(log_f, axis=1)
    f_key = jnp.transpose(f_cum, (0, 2, 1))
    q_blocks = jnp.transpose(q.reshape(B, nb, Q_BLOCK, H, Dh), (1, 0, 2, 3, 4))
    f_blocks = jnp.transpose(f_cum.reshape(B, nb, Q_BLOCK, H), (1, 0, 3, 2))
    k_pos = jnp.arange(S)

    def one_block(args):
        q_i, f_i, i = args
        s = jnp.einsum('bqhd,bkhd->bhqk', q_i, k).astype(jnp.float32) * scale
        s = s + f_i[..., :, None] - f_key[:, :, None, :]
        q_pos = i * Q_BLOCK + jnp.arange(Q_BLOCK)
        causal = k_pos[None, :] <= q_pos[:, None]
        s = jnp.where(causal[None, None], s, -jnp.inf)
        p = jax.nn.softmax(s, axis=-1)
        return jnp.einsum('bhqk,bkhd->bqhd', p.astype(v.dtype), v)

    out = lax.map(one_block, (q_blocks, f_blocks, jnp.arange(nb)))
    return jnp.transpose(out, (1, 0, 2, 3, 4)).reshape(B, S, H * Dh)


def causal_depthwise_conv(u, w, b):
    K, C = w.shape
    u_pad = jnp.pad(u, ((0, 0), (K - 1, 0), (0, 0)))
    y = lax.conv_general_dilated(
        u_pad, w[:, None, :].astype(u.dtype), window_strides=(1,), padding='VALID',
        dimension_numbers=('NWC', 'WIO', 'NWC'), feature_group_count=C)
    return y + b.astype(u.dtype)


def _fwd_setup_inputs(seed: int = 0) -> dict:
    key = jax.random.key(seed)
    ks = jax.random.split(key, 20)
    n = jax.random.normal
    D, L = D_MODEL, DEPTH
    return {
        "x": n(ks[0], (BATCH, SEQ, D), jnp.float32),
        "c": n(ks[1], (BATCH, D), jnp.float32),
        "w_ada": n(ks[2], (L, D, N_ADA * D), jnp.float32) * (0.5 * D ** -0.5),
        "b_ada": n(ks[3], (L, N_ADA * D), jnp.float32) * 0.02,
        "norm1_g": 1.0 + 0.1 * n(ks[4], (L, D), jnp.float32),
        "w_in": n(ks[5], (L, D, D_IN), jnp.float32) * D ** -0.5,
        "b_forget": 3.0 + 0.5 * n(ks[6], (L, N_HEADS), jnp.float32),
        "q_norm_g": 1.0 + 0.1 * n(ks[7], (L, HEAD_DIM), jnp.float32),
        "k_norm_g": 1.0 + 0.1 * n(ks[8], (L, HEAD_DIM), jnp.float32),
        "w_attn_proj": n(ks[9], (L, ATTN_WIDTH, D), jnp.float32) * ATTN_WIDTH ** -0.5,
        "conv_w": n(ks[10], (L, CONV_K, CONV_WIDTH), jnp.float32) * CONV_K ** -0.5,
        "conv_b": 0.02 * n(ks[11], (L, CONV_WIDTH), jnp.float32),
        "conv_ln_g": 1.0 + 0.1 * n(ks[12], (L, CONV_WIDTH), jnp.float32),
        "conv_ln_b": 0.02 * n(ks[13], (L, CONV_WIDTH), jnp.float32),
        "w_conv_proj": n(ks[14], (L, CONV_WIDTH, D), jnp.float32) * CONV_WIDTH ** -0.5,
        "w_out": n(ks[15], (L, D, D), jnp.float32) * D ** -0.5,
        "norm2_g": 1.0 + 0.1 * n(ks[16], (L, D), jnp.float32),
        "w_mlp1": n(ks[17], (L, D, D_FF), jnp.float32) * D ** -0.5,
        "w_mlp2": n(ks[18], (L, D_FF, D), jnp.float32) * D_FF ** -0.5,
    }


def _fwd_reference(x, c, w_ada, b_ada, norm1_g, w_in, b_forget, q_norm_g, k_norm_g,
              w_attn_proj, conv_w, conv_b, conv_ln_g, conv_ln_b, w_conv_proj,
              w_out, norm2_g, w_mlp1, w_mlp2):
    B, S, D = x.shape
    c_act = jax.nn.silu(c)
    for l in range(DEPTH):
        mod = c_act @ w_ada[l] + b_ada[l]
        sh1, sc1, g1, sh2, sc2, g2 = [m[:, None, :] for m in jnp.split(mod, N_ADA, axis=-1)]

        h = rms_norm(x, norm1_g[l]) * (1.0 + sc1) + sh1
        proj = h @ w_in[l]
        q, k, v, f_logit, glu_in, gate_logit = jnp.split(proj, IN_SPLITS, axis=-1)

        q = rms_norm(q.reshape(B, S, N_HEADS, HEAD_DIM), q_norm_g[l])
        k = rms_norm(k.reshape(B, S, N_HEADS, HEAD_DIM), k_norm_g[l])
        v = v.reshape(B, S, N_HEADS, HEAD_DIM)
        log_f = jax.nn.log_sigmoid(f_logit.astype(jnp.float32) + b_forget[l].astype(jnp.float32))
        branch_a = forgetting_attention(q, k, v, log_f) @ w_attn_proj[l]

        u = glu_in[..., :CONV_WIDTH] * jax.nn.sigmoid(glu_in[..., CONV_WIDTH:])
        u = causal_depthwise_conv(u, conv_w[l], conv_b[l])
        u = jax.nn.silu(layer_norm(u, conv_ln_g[l], conv_ln_b[l]))
        branch_b = u @ w_conv_proj[l]

        gate_a, gate_b = jnp.split(gate_logit, 2, axis=-1)
        merged = jax.nn.sigmoid(gate_a) * branch_a + jax.nn.sigmoid(gate_b) * branch_b
        x = x + g1 * (merged @ w_out[l])

        h2 = rms_norm(x, norm2_g[l]) * (1.0 + sc2) + sh2
        x = x + g2 * (jnp.square(jax.nn.relu(h2 @ w_mlp1[l])) @ w_mlp2[l])
    return x


import jax as _jax
import jax.numpy as _jnp

TWIN_FORMAT = 'train_step'
FWD_PARAMS = ['x', 'c', 'w_ada', 'b_ada', 'norm1_g', 'w_in', 'b_forget', 'q_norm_g', 'k_norm_g', 'w_attn_proj', 'conv_w', 'conv_b', 'conv_ln_g', 'conv_ln_b', 'w_conv_proj', 'w_out', 'norm2_g', 'w_mlp1', 'w_mlp2']
TWIN_WEIGHTS = ['w_ada', 'b_ada', 'norm1_g', 'w_in', 'b_forget', 'q_norm_g', 'k_norm_g', 'w_attn_proj', 'conv_w', 'conv_b', 'conv_ln_g', 'conv_ln_b', 'w_conv_proj', 'w_out', 'norm2_g', 'w_mlp1', 'w_mlp2']
TWIN_DIFF_INPUT = 'x'
TWIN_INPUTS = ['x', 'c', 'w_ada', 'b_ada', 'norm1_g', 'w_in', 'b_forget', 'q_norm_g', 'k_norm_g', 'w_attn_proj', 'conv_w', 'conv_b', 'conv_ln_g', 'conv_ln_b', 'w_conv_proj', 'w_out', 'norm2_g', 'w_mlp1', 'w_mlp2', 'loss_target', 'm_w_ada', 'm_b_ada', 'm_norm1_g', 'm_w_in', 'm_b_forget', 'm_q_norm_g', 'm_k_norm_g', 'm_w_attn_proj', 'm_conv_w', 'm_conv_b', 'm_conv_ln_g', 'm_conv_ln_b', 'm_w_conv_proj', 'm_w_out', 'm_norm2_g', 'm_w_mlp1', 'm_w_mlp2', 'v_w_ada', 'v_b_ada', 'v_norm1_g', 'v_w_in', 'v_b_forget', 'v_q_norm_g', 'v_k_norm_g', 'v_w_attn_proj', 'v_conv_w', 'v_conv_b', 'v_conv_ln_g', 'v_conv_ln_b', 'v_w_conv_proj', 'v_w_out', 'v_norm2_g', 'v_w_mlp1', 'v_w_mlp2']
TWIN_OUTPUTS = ['loss', 'grad_x', 'grad_w_ada', 'grad_b_ada', 'grad_norm1_g', 'grad_w_in', 'grad_b_forget', 'grad_q_norm_g', 'grad_k_norm_g', 'grad_w_attn_proj', 'grad_conv_w', 'grad_conv_b', 'grad_conv_ln_g', 'grad_conv_ln_b', 'grad_w_conv_proj', 'grad_w_out', 'grad_norm2_g', 'grad_w_mlp1', 'grad_w_mlp2', 'delta_w_ada', 'delta_b_ada', 'delta_norm1_g', 'delta_w_in', 'delta_b_forget', 'delta_q_norm_g', 'delta_k_norm_g', 'delta_w_attn_proj', 'delta_conv_w', 'delta_conv_b', 'delta_conv_ln_g', 'delta_conv_ln_b', 'delta_w_conv_proj', 'delta_w_out', 'delta_norm2_g', 'delta_w_mlp1', 'delta_w_mlp2', 'new_m_w_ada', 'new_m_b_ada', 'new_m_norm1_g', 'new_m_w_in', 'new_m_b_forget', 'new_m_q_norm_g', 'new_m_k_norm_g', 'new_m_w_attn_proj', 'new_m_conv_w', 'new_m_conv_b', 'new_m_conv_ln_g', 'new_m_conv_ln_b', 'new_m_w_conv_proj', 'new_m_w_out', 'new_m_norm2_g', 'new_m_w_mlp1', 'new_m_w_mlp2', 'new_v_w_ada', 'new_v_b_ada', 'new_v_norm1_g', 'new_v_w_in', 'new_v_b_forget', 'new_v_q_norm_g', 'new_v_k_norm_g', 'new_v_w_attn_proj', 'new_v_conv_w', 'new_v_conv_b', 'new_v_conv_ln_g', 'new_v_conv_ln_b', 'new_v_w_conv_proj', 'new_v_w_out', 'new_v_norm2_g', 'new_v_w_mlp1', 'new_v_w_mlp2']
TWIN_LEAF_KINDS = {'loss': 'loss', 'grad_x': 'grad_x', 'grad_w_ada': 'grad_w', 'grad_b_ada': 'grad_w', 'grad_norm1_g': 'grad_w', 'grad_w_in': 'grad_w', 'grad_b_forget': 'grad_w', 'grad_q_norm_g': 'grad_w', 'grad_k_norm_g': 'grad_w', 'grad_w_attn_proj': 'grad_w', 'grad_conv_w': 'grad_w', 'grad_conv_b': 'grad_w', 'grad_conv_ln_g': 'grad_w', 'grad_conv_ln_b': 'grad_w', 'grad_w_conv_proj': 'grad_w', 'grad_w_out': 'grad_w', 'grad_norm2_g': 'grad_w', 'grad_w_mlp1': 'grad_w', 'grad_w_mlp2': 'grad_w', 'delta_w_ada': 'delta_w', 'delta_b_ada': 'delta_w', 'delta_norm1_g': 'delta_w', 'delta_w_in': 'delta_w', 'delta_b_forget': 'delta_w', 'delta_q_norm_g': 'delta_w', 'delta_k_norm_g': 'delta_w', 'delta_w_attn_proj': 'delta_w', 'delta_conv_w': 'delta_w', 'delta_conv_b': 'delta_w', 'delta_conv_ln_g': 'delta_w', 'delta_conv_ln_b': 'delta_w', 'delta_w_conv_proj': 'delta_w', 'delta_w_out': 'delta_w', 'delta_norm2_g': 'delta_w', 'delta_w_mlp1': 'delta_w', 'delta_w_mlp2': 'delta_w', 'new_m_w_ada': 'new_m', 'new_m_b_ada': 'new_m', 'new_m_norm1_g': 'new_m', 'new_m_w_in': 'new_m', 'new_m_b_forget': 'new_m', 'new_m_q_norm_g': 'new_m', 'new_m_k_norm_g': 'new_m', 'new_m_w_attn_proj': 'new_m', 'new_m_conv_w': 'new_m', 'new_m_conv_b': 'new_m', 'new_m_conv_ln_g': 'new_m', 'new_m_conv_ln_b': 'new_m', 'new_m_w_conv_proj': 'new_m', 'new_m_w_out': 'new_m', 'new_m_norm2_g': 'new_m', 'new_m_w_mlp1': 'new_m', 'new_m_w_mlp2': 'new_m', 'new_v_w_ada': 'new_v', 'new_v_b_ada': 'new_v', 'new_v_norm1_g': 'new_v', 'new_v_w_in': 'new_v', 'new_v_b_forget': 'new_v', 'new_v_q_norm_g': 'new_v', 'new_v_k_norm_g': 'new_v', 'new_v_w_attn_proj': 'new_v', 'new_v_conv_w': 'new_v', 'new_v_conv_b': 'new_v', 'new_v_conv_ln_g': 'new_v', 'new_v_conv_ln_b': 'new_v', 'new_v_w_conv_proj': 'new_v', 'new_v_w_out': 'new_v', 'new_v_norm2_g': 'new_v', 'new_v_w_mlp1': 'new_v', 'new_v_w_mlp2': 'new_v'}


def _forward(args):
    return _fwd_reference(*[args[k] for k in FWD_PARAMS])


def _output_shape():
    out = _jax.eval_shape(lambda: _forward(_fwd_setup_inputs(0)))
    return out.shape, out.dtype

N_MICROBATCH = 1
ADAM_LR = 0.001
ADAM_B1 = 0.9
ADAM_B2 = 0.999
ADAM_EPS = 1e-08
ADAM_WD = 0.01
ADAM_STEP = 10
PER_EXAMPLE_BATCH_AXIS = {'x': 0, 'c': 0, 'loss_target': 0}
SHARED_INPUTS = []
_WEIGHT_DTYPES = {'w_ada': _jnp.float32, 'b_ada': _jnp.float32, 'norm1_g': _jnp.float32, 'w_in': _jnp.float32, 'b_forget': _jnp.float32, 'q_norm_g': _jnp.float32, 'k_norm_g': _jnp.float32, 'w_attn_proj': _jnp.float32, 'conv_w': _jnp.float32, 'conv_b': _jnp.float32, 'conv_ln_g': _jnp.float32, 'conv_ln_b': _jnp.float32, 'w_conv_proj': _jnp.float32, 'w_out': _jnp.float32, 'norm2_g': _jnp.float32, 'w_mlp1': _jnp.float32, 'w_mlp2': _jnp.float32}
MOMENT_SCALE = {'w_ada': 3.423782e+00, 'b_ada': 7.438116e+00, 'norm1_g': 1.147387e-01, 'w_in': 8.684914e-02, 'b_forget': 1.779857e+00, 'q_norm_g': 5.024980e-01, 'k_norm_g': 5.043781e-01, 'w_attn_proj': 2.286594e-01, 'conv_w': 9.622585e-02, 'conv_b': 6.715225e-01, 'conv_ln_g': 5.276674e-01, 'conv_ln_b': 5.087345e-01, 'w_conv_proj': 1.837896e-01, 'w_out': 2.817036e-01, 'norm2_g': 1.262927e+01, 'w_mlp1': 3.749092e-01, 'w_mlp2': 1.413111e+00}


def _to_microbatches(a, axis):
    t = _jnp.moveaxis(a, axis, 0)
    t = t.reshape((N_MICROBATCH, t.shape[0] // N_MICROBATCH) + t.shape[1:])
    return _jnp.moveaxis(t, 1, axis + 1)


def setup_inputs(seed: int = 0) -> dict:
    inp = _fwd_setup_inputs(seed)
    key = _jax.random.fold_in(_jax.random.key(seed), 7919)
    shape, _ = _output_shape()
    out = dict(inp)
    out["loss_target"] = _jax.random.normal(_jax.random.fold_in(key, 0), shape, _jnp.float32)
    for i, name in enumerate(TWIN_WEIGHTS):
        w = inp[name].astype(_jnp.float32)
        if MOMENT_SCALE is None:
            s = _jnp.sqrt(_jnp.mean(_jnp.square(w)) + 1e-30)
        else:
            s = MOMENT_SCALE[name]
        km, kv = _jax.random.split(_jax.random.fold_in(key, i + 1))
        out[name] = w
        out["m_" + name] = s * _jax.random.normal(km, w.shape, _jnp.float32)
        out["v_" + name] = (s * s) * _jax.random.uniform(kv, w.shape, _jnp.float32, 0.5, 1.5)
    if N_MICROBATCH > 1:
        for name, axis in PER_EXAMPLE_BATCH_AXIS.items():
            out[name] = _to_microbatches(out[name], axis)
    return {'x': out['x'], 'c': out['c'], 'w_ada': out['w_ada'], 'b_ada': out['b_ada'], 'norm1_g': out['norm1_g'], 'w_in': out['w_in'], 'b_forget': out['b_forget'], 'q_norm_g': out['q_norm_g'], 'k_norm_g': out['k_norm_g'], 'w_attn_proj': out['w_attn_proj'], 'conv_w': out['conv_w'], 'conv_b': out['conv_b'], 'conv_ln_g': out['conv_ln_g'], 'conv_ln_b': out['conv_ln_b'], 'w_conv_proj': out['w_conv_proj'], 'w_out': out['w_out'], 'norm2_g': out['norm2_g'], 'w_mlp1': out['w_mlp1'], 'w_mlp2': out['w_mlp2'], 'loss_target': out['loss_target'], 'm_w_ada': out['m_w_ada'], 'm_b_ada': out['m_b_ada'], 'm_norm1_g': out['m_norm1_g'], 'm_w_in': out['m_w_in'], 'm_b_forget': out['m_b_forget'], 'm_q_norm_g': out['m_q_norm_g'], 'm_k_norm_g': out['m_k_norm_g'], 'm_w_attn_proj': out['m_w_attn_proj'], 'm_conv_w': out['m_conv_w'], 'm_conv_b': out['m_conv_b'], 'm_conv_ln_g': out['m_conv_ln_g'], 'm_conv_ln_b': out['m_conv_ln_b'], 'm_w_conv_proj': out['m_w_conv_proj'], 'm_w_out': out['m_w_out'], 'm_norm2_g': out['m_norm2_g'], 'm_w_mlp1': out['m_w_mlp1'], 'm_w_mlp2': out['m_w_mlp2'], 'v_w_ada': out['v_w_ada'], 'v_b_ada': out['v_b_ada'], 'v_norm1_g': out['v_norm1_g'], 'v_w_in': out['v_w_in'], 'v_b_forget': out['v_b_forget'], 'v_q_norm_g': out['v_q_norm_g'], 'v_k_norm_g': out['v_k_norm_g'], 'v_w_attn_proj': out['v_w_attn_proj'], 'v_conv_w': out['v_conv_w'], 'v_conv_b': out['v_conv_b'], 'v_conv_ln_g': out['v_conv_ln_g'], 'v_conv_ln_b': out['v_conv_ln_b'], 'v_w_conv_proj': out['v_w_conv_proj'], 'v_w_out': out['v_w_out'], 'v_norm2_g': out['v_norm2_g'], 'v_w_mlp1': out['v_w_mlp1'], 'v_w_mlp2': out['v_w_mlp2']}


def _loss(weights, diff, rest, loss_target):
    with _jax.named_scope("forward"):
        args = {**rest, TWIN_DIFF_INPUT: diff, **{k: w.astype(_WEIGHT_DTYPES[k]) for k, w in weights.items()}}
        y = _forward(args)
    with _jax.named_scope("loss_head"):
        err = _jnp.square(y.astype(_jnp.float32) - loss_target)
        return 0.5 * _jnp.sum(_jnp.mean(err, axis=-1)) if err.ndim else 0.5 * err


def _adamw(w, g, m, v):
    m = ADAM_B1 * m + (1.0 - ADAM_B1) * g
    v = ADAM_B2 * v + (1.0 - ADAM_B2) * _jnp.square(g)
    m_hat = m / (1.0 - ADAM_B1 ** ADAM_STEP)
    v_hat = v / (1.0 - ADAM_B2 ** ADAM_STEP)
    delta = -ADAM_LR * (m_hat / (_jnp.sqrt(v_hat) + ADAM_EPS) + ADAM_WD * w)
    return delta, m, v


def reference(x, c, w_ada, b_ada, norm1_g, w_in, b_forget, q_norm_g, k_norm_g, w_attn_proj, conv_w, conv_b, conv_ln_g, conv_ln_b, w_conv_proj, w_out, norm2_g, w_mlp1, w_mlp2, loss_target, m_w_ada, m_b_ada, m_norm1_g, m_w_in, m_b_forget, m_q_norm_g, m_k_norm_g, m_w_attn_proj, m_conv_w, m_conv_b, m_conv_ln_g, m_conv_ln_b, m_w_conv_proj, m_w_out, m_norm2_g, m_w_mlp1, m_w_mlp2, v_w_ada, v_b_ada, v_norm1_g, v_w_in, v_b_forget, v_q_norm_g, v_k_norm_g, v_w_attn_proj, v_conv_w, v_conv_b, v_conv_ln_g, v_conv_ln_b, v_w_conv_proj, v_w_out, v_norm2_g, v_w_mlp1, v_w_mlp2):
    given = dict(x=x, c=c, w_ada=w_ada, b_ada=b_ada, norm1_g=norm1_g, w_in=w_in, b_forget=b_forget, q_norm_g=q_norm_g, k_norm_g=k_norm_g, w_attn_proj=w_attn_proj, conv_w=conv_w, conv_b=conv_b, conv_ln_g=conv_ln_g, conv_ln_b=conv_ln_b, w_conv_proj=w_conv_proj, w_out=w_out, norm2_g=norm2_g, w_mlp1=w_mlp1, w_mlp2=w_mlp2, loss_target=loss_target, m_w_ada=m_w_ada, m_b_ada=m_b_ada, m_norm1_g=m_norm1_g, m_w_in=m_w_in, m_b_forget=m_b_forget, m_q_norm_g=m_q_norm_g, m_k_norm_g=m_k_norm_g, m_w_attn_proj=m_w_attn_proj, m_conv_w=m_conv_w, m_conv_b=m_conv_b, m_conv_ln_g=m_conv_ln_g, m_conv_ln_b=m_conv_ln_b, m_w_conv_proj=m_w_conv_proj, m_w_out=m_w_out, m_norm2_g=m_norm2_g, m_w_mlp1=m_w_mlp1, m_w_mlp2=m_w_mlp2, v_w_ada=v_w_ada, v_b_ada=v_b_ada, v_norm1_g=v_norm1_g, v_w_in=v_w_in, v_b_forget=v_b_forget, v_q_norm_g=v_q_norm_g, v_k_norm_g=v_k_norm_g, v_w_attn_proj=v_w_attn_proj, v_conv_w=v_conv_w, v_conv_b=v_conv_b, v_conv_ln_g=v_conv_ln_g, v_conv_ln_b=v_conv_ln_b, v_w_conv_proj=v_w_conv_proj, v_w_out=v_w_out, v_norm2_g=v_norm2_g, v_w_mlp1=v_w_mlp1, v_w_mlp2=v_w_mlp2)
    weights = {n: given[n] for n in TWIN_WEIGHTS}
    shared = {n: given[n] for n in SHARED_INPUTS}
    per_example = {n: given[n] for n in ['x', 'c']}
    grad_fn = _jax.value_and_grad(_loss, argnums=(0, 1))

    def one_microbatch(ex, loss_target):
        ex = dict(ex)
        diff = ex.pop(TWIN_DIFF_INPUT)
        return grad_fn(weights, diff, {**shared, **ex}, loss_target)

    if N_MICROBATCH == 1:
        loss, (grad_w, grad_x) = one_microbatch(per_example, given["loss_target"])
    else:
        def body(carry, xs):
            loss_sum, grad_sum = carry
            l_k, (gw_k, gx_k) = one_microbatch(xs[0], xs[1])
            with _jax.named_scope("update"):
                return (loss_sum + l_k, _jax.tree.map(_jnp.add, grad_sum, gw_k)), gx_k

        init = (_jnp.zeros((), _jnp.float32), _jax.tree.map(_jnp.zeros_like, weights))
        (loss, grad_w), grad_x = _jax.lax.scan(body, init, (per_example, given["loss_target"]))
    with _jax.named_scope("update"):
        delta_w, new_m, new_v = {}, {}, {}
        for n in TWIN_WEIGHTS:
            delta_w[n], new_m[n], new_v[n] = _adamw(weights[n], grad_w[n], given["m_" + n], given["v_" + n])
    return (loss, grad_x, *[grad_w[n] for n in TWIN_WEIGHTS], *[delta_w[n] for n in TWIN_WEIGHTS],
            *[new_m[n] for n in TWIN_WEIGHTS], *[new_v[n] for n in TWIN_WEIGHTS])
```

```python
import functools

import jax
import jax.numpy as jnp
from jax import lax
from jax.experimental import pallas as pl
from jax.experimental.pallas import tpu as pltpu

F32 = jnp.float32
BF16 = jnp.bfloat16
MESH = pl.DeviceIdType.MESH
ANY = pl.BlockSpec(memory_space=pl.ANY)

NORM_EPS = 1e-6
ADAM_LR = 0.001
ADAM_B1 = 0.9
ADAM_B2 = 0.999
ADAM_EPS = 1e-08
ADAM_WD = 0.01
ADAM_STEP = 10
LANES = 128
HALO = 32
NEG = -1e30
VMEM_LIMIT = 56 * 1024 * 1024


def _pcall(body, **kw):
    return pl.pallas_call(body, **kw)


def _cparams(sem=None):
    if sem is None:
        return pltpu.CompilerParams(vmem_limit_bytes=VMEM_LIMIT)
    return pltpu.CompilerParams(dimension_semantics=sem, vmem_limit_bytes=VMEM_LIMIT)


def _sig(x):
    return 1.0 / (1.0 + jnp.exp(-x))


def _split3(x):
    x1 = x.astype(BF16)
    r = x - x1.astype(F32)
    x2 = r.astype(BF16)
    x3 = (r - x2.astype(F32)).astype(BF16)
    return x1, x2, x3


def _dot_rs(x, e):
    out = None
    for t in _split3(x):
        d = jnp.dot(t, e, preferred_element_type=F32)
        out = d if out is None else out + d
    return out


def _dot_ls(e, x):
    out = None
    for t in _split3(x):
        d = jnp.dot(e, t, preferred_element_type=F32)
        out = d if out is None else out + d
    return out


def _tile(n, want):
    if n <= want:
        return n
    t = want - want % LANES
    while n % t:
        t -= LANES
    assert t > 0, (n, want)
    return t


_DIMS = {"nn": ((1,), (0,)), "nt": ((1,), (1,)), "tn": ((0,), (0,))}


def _mm(name, a, b, mode, out_dtypes, epi=None, extras=(), tm=512, tn=512, tk=1024):
    if mode == "nn":
        (M, K), (_, N) = a.shape, b.shape
    elif mode == "nt":
        (M, K), (N, _) = a.shape, b.shape
    else:
        (K, M), (_, N) = a.shape, b.shape
    tm, tn, tk = _tile(M, tm), _tile(N, tn), _tile(K, tk)
    nk = K // tk
    ne, no = len(extras), len(out_dtypes)
    dims = (_DIMS[mode], ((), ()))

    def kern(*refs):
        a_ref, b_ref = refs[0], refs[1]
        e_refs = refs[2:2 + ne]
        o_refs = refs[2 + ne:2 + ne + no]
        acc = refs[-1]
        k = pl.program_id(2)
        d = lax.dot_general(a_ref[...], b_ref[...], dims, preferred_element_type=F32)

        @pl.when(k == 0)
        def _():
            acc[...] = d

        @pl.when(k > 0)
        def _():
            acc[...] += d

        @pl.when(k == nk - 1)
        def _():
            r = acc[...]
            outs = (r,) if epi is None else epi(r, *[e[...] for e in e_refs])
            for o_ref, o in zip(o_refs, outs):
                o_ref[...] = o.astype(o_ref.dtype)

    if mode == "tn":
        a_spec = pl.BlockSpec((tk, tm), lambda i, j, k: (k, i))
    else:
        a_spec = pl.BlockSpec((tm, tk), lambda i, j, k: (i, k))
    if mode == "nt":
        b_spec = pl.BlockSpec((tn, tk), lambda i, j, k: (j, k))
    else:
        b_spec = pl.BlockSpec((tk, tn), lambda i, j, k: (k, j))
    mn_spec = pl.BlockSpec((tm, tn), lambda i, j, k: (i, j))
    outs = _pcall(
        kern, name=name, grid=(M // tm, N // tn, nk),
        in_specs=[a_spec, b_spec] + [mn_spec] * ne,
        out_specs=[mn_spec] * no,
        out_shape=[jax.ShapeDtypeStruct((M, N), dt) for dt in out_dtypes],
        scratch_shapes=[pltpu.VMEM((tm, tn), F32)],
        compiler_params=_cparams(("parallel", "parallel", "arbitrary")),
    )(a, b, *extras)
    return outs[0] if no == 1 else outs


def _rowcall(name, body, S, ts, row_ins, vec_ins, row_outs, vec_outs):
    ts = min(ts, S)
    nri, nvi, nro, nvo = len(row_ins), len(vec_ins), len(row_outs), len(vec_outs)

    def kern(*refs):
        ins = refs[:nri + nvi]
        outs = refs[nri + nvi:]
        if nvo:
            @pl.when(pl.program_id(0) == 0)
            def _():
                for r in outs[nro:]:
                    r[...] = jnp.zeros(r.shape, r.dtype)
        body(*ins, *outs)

    in_specs = [pl.BlockSpec((ts, w), functools.partial(lambda i, cb: (i, cb), cb=cb))
                for (_, w, cb) in row_ins]
    in_specs += [pl.BlockSpec(v.shape, lambda i: (0, 0)) for v in vec_ins]
    out_specs = [pl.BlockSpec((ts, w), lambda i: (i, 0)) for (w, _) in row_outs]
    out_specs += [pl.BlockSpec((r, w), lambda i: (0, 0)) for (r, w) in vec_outs]
    out_shape = [jax.ShapeDtypeStruct((S, w), dt) for (w, dt) in row_outs]
    out_shape += [jax.ShapeDtypeStruct((r, w), F32) for (r, w) in vec_outs]
    return _pcall(
        kern, name=name, grid=(S // ts,), in_specs=in_specs, out_specs=out_specs,
        out_shape=out_shape,
        compiler_params=_cparams(("arbitrary",) if nvo else ("parallel",)),
    )(*[a for (a, _, _) in row_ins], *vec_ins)


def _csum(x):
    return jnp.sum(x, axis=0, keepdims=True)


def _norm_mod(name, x, g, sc, sh, S, D):
    def body(x_ref, g_ref, sc_ref, sh_ref, h_ref):
        xv = x_ref[...]
        r = lax.rsqrt(jnp.mean(xv * xv, axis=-1, keepdims=True) + NORM_EPS)
        h_ref[...] = ((xv * r * g_ref[...]) * (1.0 + sc_ref[...]) + sh_ref[...]).astype(BF16)
    return _rowcall(name, body, S, 512, [(x, D, 0)], [g, sc, sh], [(D, BF16)], [])[0]


def _head_rstd(v, grp, grp_t, hd):
    ss = _dot_rs(v * v, grp) * (1.0 / hd)
    r = lax.rsqrt(ss + NORM_EPS)
    return _dot_rs(r, grp_t)


def _qk_prep(proj, gq, gk, grp, grp_t, S, D, hd):
    scale = hd ** -0.5

    def body(q_ref, k_ref, v_ref, gq_ref, gk_ref, g_ref, gt_ref, qs_ref, kn_ref, vb_ref):
        q = q_ref[...]
        k = k_ref[...]
        rq = _head_rstd(q, g_ref[...], gt_ref[...], hd)
        rk = _head_rstd(k, g_ref[...], gt_ref[...], hd)
        qs_ref[...] = ((q * rq * gq_ref[...]).astype(BF16).astype(F32) * scale).astype(BF16)
        kn_ref[...] = (k * rk * gk_ref[...]).astype(BF16)
        vb_ref[...] = v_ref[...].astype(BF16)

    return _rowcall("qk_prep", body, S, 256, [(proj, D, 0), (proj, D, 1), (proj, D, 2)],
                    [gq, gk, grp, grp_t], [(D, BF16)] * 3, [])


def _fgate_fwd(proj, fcol, bf_pad, tri, S):
    ch = tri.shape[0]

    def body(f_ref, b_ref, tri_ref, out_ref):
        carry = jnp.zeros((1, LANES), F32)
        for c in range(S // ch):
            z = f_ref[c * ch:(c + 1) * ch, :] + b_ref[...]
            lf = jnp.minimum(z, 0.0) - jnp.log(1.0 + jnp.exp(-jnp.abs(z)))
            out_ref[c * ch:(c + 1) * ch, :] = _dot_ls(tri_ref[...], lf) + carry
            carry = carry + _csum(lf)

    return _rowcall("fgate_fwd", body, S, S, [(proj, LANES, fcol)], [bf_pad, tri],
                    [(LANES, F32)], [])[0]


def _fgate_bwd(dfk, dfq, proj, fcol, bf_pad, tri_u, nh, S):
    ch = tri_u.shape[0]

    def body(d_ref, dq_ref, f_ref, b_ref, tri_ref, df_ref, db_ref):
        lane = lax.broadcasted_iota(jnp.int32, (ch, LANES), 1)
        carry = jnp.zeros((1, LANES), F32)
        tot = jnp.zeros((1, LANES), F32)
        for c in reversed(range(S // ch)):
            d = d_ref[c * ch:(c + 1) * ch, :] + dq_ref[c * ch:(c + 1) * ch, :]
            rc = _dot_ls(tri_ref[...], d) + carry
            carry = carry + _csum(d)
            z = f_ref[c * ch:(c + 1) * ch, :] + b_ref[...]
            df = jnp.where(lane < nh, rc * _sig(-z), 0.0)
            df_ref[c * ch:(c + 1) * ch, :] = df.astype(BF16)
            tot = tot + _csum(df)
        db_ref[...] += tot

    return _rowcall("fgate_bwd", body, S, S, [(dfk, LANES, 0), (dfq, LANES, 0), (proj, LANES, fcol)],
                    [bf_pad, tri_u], [(LANES, BF16)], [(1, LANES)])


def _keep(v, mask):
    return jnp.where(mask, v.astype(F32), 0.0).astype(BF16)


def _lane_col(blk, lane, at):
    return jnp.sum(jnp.where(lane == at, blk, 0.0), axis=-1, keepdims=True)


def _flash_fwd(qs, kn, vb, fk_r, S, D, hd, tq):
    hp, nq = D // LANES, S // tq

    def kern(q_ref, k_ref, v_ref, fk_ref, o_ref, o32_ref, lse_ref, m_sc, l_sc, acc_sc):
        qi = pl.program_id(1)
        lane = lax.broadcasted_iota(jnp.int32, (tq, LANES), 1)
        row = lax.broadcasted_iota(jnp.int32, (tq, tq), 0)
        col = lax.broadcasted_iota(jnp.int32, (tq, tq), 1)
        q = q_ref[...]
        for j in range(2):
            qm = _keep(q, (lane >= j * hd) & (lane < (j + 1) * hd))
            m_sc[j] = jnp.full((tq, 1), NEG, F32)
            l_sc[j] = jnp.zeros((tq, 1), F32)
            acc_sc[j] = jnp.zeros((tq, LANES), F32)

            def step(ki, masked, j=j, qm=qm):
                off = pl.multiple_of(ki * tq, tq)
                k = k_ref[pl.ds(off, tq), :]
                v = v_ref[pl.ds(off, tq), :]
                s = lax.dot_general(qm, k, (((1,), (1,)), ((), ())), preferred_element_type=F32)
                s = s - fk_ref[j, ki]
                if masked:
                    s = jnp.where(col <= row, s, NEG)
                m_old = m_sc[j]
                m_new = jnp.maximum(m_old, jnp.max(s, axis=-1, keepdims=True))
                alpha = jnp.exp(m_old - m_new)
                p = jnp.exp(s - m_new)
                l_sc[j] = alpha * l_sc[j] + jnp.sum(p, axis=-1, keepdims=True)
                acc_sc[j] = alpha * acc_sc[j] + jnp.dot(p.astype(BF16), v, preferred_element_type=F32)
                m_sc[j] = m_new

            def loop_body(ki, carry, step=step):
                step(ki, False)
                return carry

            lax.fori_loop(0, qi, loop_body, 0)
            step(qi, True)
        first = lane < hd
        ov = jnp.where(first, acc_sc[0] / l_sc[0], acc_sc[1] / l_sc[1])
        o_ref[...] = ov.astype(BF16)
        o32_ref[...] = ov
        lse_ref[...] = jnp.where(first, m_sc[0] + jnp.log(l_sc[0]), m_sc[1] + jnp.log(l_sc[1]))

    qspec = pl.BlockSpec((tq, LANES), lambda h, i: (i, h))
    fullspec = pl.BlockSpec((S, LANES), lambda h, i: (0, h))
    return _pcall(
        kern, name="flash_fwd", grid=(hp, nq),
        in_specs=[qspec, fullspec, fullspec,
                  pl.BlockSpec((2, nq, 1, tq), lambda h, i: (h, 0, 0, 0))],
        out_specs=[qspec, qspec, qspec],
        out_shape=[jax.ShapeDtypeStruct((S, D), BF16), jax.ShapeDtypeStruct((S, D), F32),
                   jax.ShapeDtypeStruct((S, D), F32)],
        scratch_shapes=[pltpu.VMEM((2, tq, 1), F32), pltpu.VMEM((2, tq, 1), F32),
                        pltpu.VMEM((2, tq, LANES), F32)],
        compiler_params=_cparams(("parallel", "arbitrary")),
    )(qs, kn, vb, fk_r)


def _flash_bwd_kv(qs, kn, vb, do, fk_b, lse_r, delta_r, S, D, hd, tq):
    hp, nq = D // LANES, S // tq

    def kern(q_ref, do_ref, k_ref, v_ref, fk_ref, lse_ref, dl_ref, dk_ref, dv_ref, dfk_ref,
             dk_sc, dv_sc, df_sc):
        ki = pl.program_id(1)
        lane = lax.broadcasted_iota(jnp.int32, (tq, LANES), 1)
        row = lax.broadcasted_iota(jnp.int32, (tq, tq), 0)
        col = lax.broadcasted_iota(jnp.int32, (tq, tq), 1)
        k = k_ref[...]
        v = v_ref[...]
        fkb = fk_ref[...]
        dk_sc[...] = jnp.zeros((tq, LANES), F32)
        dv_sc[...] = jnp.zeros((tq, LANES), F32)
        for j in range(2):
            hm = (lane >= j * hd) & (lane < (j + 1) * hd)
            km = _keep(k, hm)
            vm = _keep(v, hm)
            fk = _lane_col(fkb, lane, j * hd)
            df_sc[j] = jnp.zeros((tq, 1), F32)

            def step(qi, masked, j=j, hm=hm, km=km, vm=vm, fk=fk):
                off = pl.multiple_of(qi * tq, tq)
                q = q_ref[pl.ds(off, tq), :]
                g = do_ref[pl.ds(off, tq), :]
                qm = _keep(q, hm)
                gm = _keep(g, hm)
                st = lax.dot_general(km, q, (((1,), (1,)), ((), ())), preferred_element_type=F32)
                st = st - fk
                if masked:
                    st = jnp.where(row <= col, st, NEG)
                pt = jnp.exp(st - lse_ref[j, qi])
                dv_sc[...] += jnp.dot(pt.astype(BF16), gm, preferred_element_type=F32)
                dpt = lax.dot_general(vm, g, (((1,), (1,)), ((), ())), preferred_element_type=F32)
                dst = pt * (dpt - dl_ref[j, qi])
                dk_sc[...] += jnp.dot(dst.astype(BF16), qm, preferred_element_type=F32)
                df_sc[j] = df_sc[j] - jnp.sum(dst, axis=-1, keepdims=True)

            def loop_body(qi, carry, step=step):
                step(qi, False)
                return carry

            step(ki, True)
            lax.fori_loop(ki + 1, nq, loop_body, 0)
        dk_ref[...] = dk_sc[...].astype(BF16)
        dv_ref[...] = dv_sc[...].astype(BF16)
        dfk_ref[...] = jnp.where(lane < hd, df_sc[0], df_sc[1])

    kspec = pl.BlockSpec((tq, LANES), lambda h, i: (i, h))
    fullspec = pl.BlockSpec((S, LANES), lambda h, i: (0, h))
    rowspec = pl.BlockSpec((2, nq, 1, tq), lambda h, i: (h, 0, 0, 0))
    return _pcall(
        kern, name="flash_bwd_kv", grid=(hp, nq),
        in_specs=[fullspec, fullspec, kspec, kspec, kspec, rowspec, rowspec],
        out_specs=[kspec, kspec, kspec],
        out_shape=[jax.ShapeDtypeStruct((S, D), BF16), jax.ShapeDtypeStruct((S, D), BF16),
                   jax.ShapeDtypeStruct((S, D), F32)],
        scratch_shapes=[pltpu.VMEM((tq, LANES), F32), pltpu.VMEM((tq, LANES), F32),
                        pltpu.VMEM((2, tq, 1), F32)],
        compiler_params=_cparams(("parallel", "arbitrary")),
    )(qs, do, kn, vb, fk_b, lse_r, delta_r)


def _flash_bwd_q(qs, kn, vb, do, fk_r, lse_b, delta_b, S, D, hd, tq):
    hp, nq = D // LANES, S // tq

    def kern(q_ref, do_ref, k_ref, v_ref, fk_ref, lse_ref, dl_ref, dq_ref, dfq_ref, dq_sc, df_sc):
        qi = pl.program_id(1)
        lane = lax.broadcasted_iota(jnp.int32, (tq, LANES), 1)
        row = lax.broadcasted_iota(jnp.int32, (tq, tq), 0)
        col = lax.broadcasted_iota(jnp.int32, (tq, tq), 1)
        q = q_ref[...]
        g = do_ref[...]
        lse_b_ = lse_ref[...]
        dl_b_ = dl_ref[...]
        dq_sc[...] = jnp.zeros((tq, LANES), F32)
        for j in range(2):
            hm = (lane >= j * hd) & (lane < (j + 1) * hd)
            qm = _keep(q, hm)
            gm = _keep(g, hm)
            lse = _lane_col(lse_b_, lane, j * hd)
            dl = _lane_col(dl_b_, lane, j * hd)
            df_sc[j] = jnp.zeros((tq, 1), F32)

            def step(ki, masked, j=j, hm=hm, qm=qm, gm=gm, lse=lse, dl=dl):
                off = pl.multiple_of(ki * tq, tq)
                k = k_ref[pl.ds(off, tq), :]
                v = v_ref[pl.ds(off, tq), :]
                s = lax.dot_general(qm, k, (((1,), (1,)), ((), ())), preferred_element_type=F32)
                s = s - fk_ref[j, ki]
                if masked:
                    s = jnp.where(col <= row, s, NEG)
                p = jnp.exp(s - lse)
                dp = lax.dot_general(gm, v, (((1,), (1,)), ((), ())), preferred_element_type=F32)
                ds = p * (dp - dl)
                km = _keep(k, hm)
                dq_sc[...] += jnp.dot(ds.astype(BF16), km, preferred_element_type=F32)
                df_sc[j] = df_sc[j] + jnp.sum(ds, axis=-1, keepdims=True)

            def loop_body(ki, carry, step=step):
                step(ki, False)
                return carry

            lax.fori_loop(0, qi, loop_body, 0)
            step(qi, True)
        dq_ref[...] = dq_sc[...].astype(BF16)
        dfq_ref[...] = jnp.where(lane < hd, df_sc[0], df_sc[1])

    qspec = pl.BlockSpec((tq, LANES), lambda h, i: (i, h))
    fullspec = pl.BlockSpec((S, LANES), lambda h, i: (0, h))
    rowspec = pl.BlockSpec((2, nq, 1, tq), lambda h, i: (h, 0, 0, 0))
    return _pcall(
        kern, name="flash_bwd_q", grid=(hp, nq),
        in_specs=[qspec, qspec, fullspec, fullspec, rowspec, qspec, qspec],
        out_specs=[qspec, qspec],
        out_shape=[jax.ShapeDtypeStruct((S, D), BF16), jax.ShapeDtypeStruct((S, D), F32)],
        scratch_shapes=[pltpu.VMEM((tq, LANES), F32), pltpu.VMEM((2, tq, 1), F32)],
        compiler_params=_cparams(("parallel", "arbitrary")),
    )(qs, do, kn, vb, fk_r, lse_b, delta_b)


def _delta_prep(do, o, grp, grp_t, S, D):
    def body(g_ref, o_ref, e_ref, et_ref, out_ref):
        prod = g_ref[...].astype(F32) * o_ref[...]
        out_ref[...] = _dot_rs(_dot_rs(prod, e_ref[...]), et_ref[...])
    return _rowcall("delta_prep", body, S, 256, [(do, D, 0), (o, D, 0)], [grp, grp_t],
                    [(D, F32)], [])[0]


def _qk_bwd(proj, dqs, dkn, gq, gk, grp, grp_t, S, D, hd):
    scale = hd ** -0.5

    def one(x, dn, gain, e, et):
        r = _head_rstd(x, e, et, hd)
        xh = x * r
        t = dn * gain
        mean = _dot_rs(_dot_rs(t * xh, e), et) * (1.0 / hd)
        return r * (t - xh * mean), _csum(dn * xh)

    def body(q_ref, k_ref, dq_ref, dk_ref, gq_ref, gk_ref, e_ref, et_ref,
             oq_ref, ok_ref, sq_ref, sk_ref):
        e, et = e_ref[...], et_ref[...]
        dq, sq = one(q_ref[...], dq_ref[...].astype(F32) * scale, gq_ref[...], e, et)
        dk, sk = one(k_ref[...], dk_ref[...].astype(F32), gk_ref[...], e, et)
        oq_ref[...] = dq.astype(BF16)
        ok_ref[...] = dk.astype(BF16)
        sq_ref[...] += sq
        sk_ref[...] += sk

    return _rowcall("qk_bwd", body, S, 256,
                    [(proj, D, 0), (proj, D, 1), (dqs, D, 0), (dkn, D, 0)],
                    [gq, gk, grp, grp_t], [(D, BF16)] * 2, [(1, D)] * 2)


def _conv_fwd(proj, acol, bcol, w_pad, cb, lg, lb, S, C, taps, ts):
    ts = min(ts, S)

    def kern(a_ref, b_ref, w_ref, cb_ref, lg_ref, lb_ref, u1_ref, u3_ref, ubuf):
        @pl.when(pl.program_id(0) == 0)
        def _():
            ubuf[0:HALO, :] = jnp.zeros((HALO, C), F32)

        ubuf[HALO:HALO + ts, :] = a_ref[...] * _sig(b_ref[...])
        acc = jnp.zeros((ts, C), F32) + cb_ref[...]
        for k in range(taps):
            o = HALO - (taps - 1) + k
            acc = acc + w_ref[k:k + 1, :] * ubuf[o:o + ts, :]
        u1_ref[...] = acc
        mu = jnp.mean(acc, axis=-1, keepdims=True)
        xc = acc - mu
        rstd = lax.rsqrt(jnp.mean(xc * xc, axis=-1, keepdims=True) + NORM_EPS)
        u2 = xc * rstd * lg_ref[...] + lb_ref[...]
        u3_ref[...] = (u2 * _sig(u2)).astype(BF16)
        ubuf[0:HALO, :] = ubuf[ts:ts + HALO, :]

    vec = lambda a: pl.BlockSpec(a.shape, lambda i: (0, 0))
    return _pcall(
        kern, name="conv_fwd", grid=(S // ts,),
        in_specs=[pl.BlockSpec((ts, C), lambda i: (i, acol)), pl.BlockSpec((ts, C), lambda i: (i, bcol)),
                  vec(w_pad), vec(cb), vec(lg), vec(lb)],
        out_specs=[pl.BlockSpec((ts, C), lambda i: (i, 0))] * 2,
        out_shape=[jax.ShapeDtypeStruct((S, C), F32), jax.ShapeDtypeStruct((S, C), BF16)],
        scratch_shapes=[pltpu.VMEM((HALO + ts, C), F32)],
        compiler_params=_cparams(("arbitrary",)),
    )(proj, proj, w_pad, cb, lg, lb)


def _conv_bwd(du3, u1, proj, acol, bcol, w_pad, lg, lb, S, C, taps, ts):
    ts = min(ts, S)
    nt = S // ts
    hb = ts // HALO

    def ln_bwd(g, u, lgv, lbv):
        mu = jnp.mean(u, axis=-1, keepdims=True)
        xc = u - mu
        rstd = lax.rsqrt(jnp.mean(xc * xc, axis=-1, keepdims=True) + NORM_EPS)
        xh = xc * rstd
        u2 = xh * lgv + lbv
        s = _sig(u2)
        du2 = g * (s + u2 * s * (1.0 - s))
        dxh = du2 * lgv
        du1 = rstd * (dxh - jnp.mean(dxh, axis=-1, keepdims=True)
                      - xh * jnp.mean(dxh * xh, axis=-1, keepdims=True))
        return du1, du2, xh

    def kern(g_ref, u_ref, a_ref, b_ref, gn_ref, un_ref, ap_ref, bp_ref, w_ref, lg_ref, lb_ref,
             da_ref, db_ref, dw_ref, dcb_ref, dlg_ref, dlb_ref, dbuf, ubuf):
        i = pl.program_id(0)

        @pl.when(i == 0)
        def _():
            dw_ref[...] = jnp.zeros(dw_ref.shape, F32)
            dcb_ref[...] = jnp.zeros(dcb_ref.shape, F32)
            dlg_ref[...] = jnp.zeros(dlg_ref.shape, F32)
            dlb_ref[...] = jnp.zeros(dlb_ref.shape, F32)

        lgv, lbv = lg_ref[...], lb_ref[...]
        du1, du2, xh = ln_bwd(g_ref[...], u_ref[...], lgv, lbv)
        dbuf[0:ts, :] = du1
        du1n, _, _ = ln_bwd(gn_ref[...], un_ref[...], lgv, lbv)
        dbuf[ts:ts + HALO, :] = jnp.where(i < nt - 1, du1n, 0.0)
        a = a_ref[...]
        sb = _sig(b_ref[...])
        ubuf[HALO:HALO + ts, :] = a * sb
        ubuf[0:HALO, :] = jnp.where(i > 0, ap_ref[...] * _sig(bp_ref[...]), 0.0)
        dcb_ref[...] += _csum(du1)
        dlg_ref[...] += _csum(du2 * xh)
        dlb_ref[...] += _csum(du2)
        du0 = jnp.zeros((ts, C), F32)
        for k in range(taps):
            o = taps - 1 - k
            du0 = du0 + w_ref[k:k + 1, :] * dbuf[o:o + ts, :]
            ou = HALO - (taps - 1) + k
            dw_ref[k:k + 1, :] += _csum(du1 * ubuf[ou:ou + ts, :])
        da_ref[...] = (du0 * sb).astype(BF16)
        db_ref[...] = (du0 * a * sb * (1.0 - sb)).astype(BF16)

    vec = lambda a: pl.BlockSpec(a.shape, lambda i: (0, 0))
    tile = lambda cb: pl.BlockSpec((ts, C), functools.partial(lambda i, cb: (i, cb), cb=cb))
    nxt = lambda cb: pl.BlockSpec(
        (HALO, C), functools.partial(lambda i, cb: (jnp.minimum((i + 1) * hb, nt * hb - 1), cb), cb=cb))
    prv = lambda cb: pl.BlockSpec(
        (HALO, C), functools.partial(lambda i, cb: (jnp.maximum(i * hb - 1, 0), cb), cb=cb))
    return _pcall(
        kern, name="conv_bwd", grid=(nt,),
        in_specs=[tile(0), tile(0), tile(acol), tile(bcol), nxt(0), nxt(0), prv(acol), prv(bcol),
                  vec(w_pad), vec(lg), vec(lb)],
        out_specs=[pl.BlockSpec((ts, C), lambda i: (i, 0))] * 2
        + [pl.BlockSpec(w_pad.shape, lambda i: (0, 0))] + [pl.BlockSpec((1, C), lambda i: (0, 0))] * 3,
        out_shape=[jax.ShapeDtypeStruct((S, C), BF16)] * 2
        + [jax.ShapeDtypeStruct(w_pad.shape, F32)] + [jax.ShapeDtypeStruct((1, C), F32)] * 3,
        scratch_shapes=[pltpu.VMEM((ts + HALO, C), F32), pltpu.VMEM((HALO + ts, C), F32)],
        compiler_params=_cparams(("arbitrary",)),
    )(du3, u1, proj, proj, du3, u1, proj, proj, w_pad, lg, lb)


def _gate_merge(proj, gacol, gbcol, ba, bb, S, D):
    def body(ga_ref, gb_ref, a_ref, b_ref, out_ref):
        out_ref[...] = (_sig(ga_ref[...]) * a_ref[...] + _sig(gb_ref[...]) * b_ref[...]).astype(BF16)
    return _rowcall("gate_merge", body, S, 512,
                    [(proj, D, gacol), (proj, D, gbcol), (ba, D, 0), (bb, D, 0)], [], [(D, BF16)], [])[0]


def _gate_bwd(dm, proj, gacol, gbcol, ba, bb, S, D):
    def body(dm_ref, ga_ref, gb_ref, a_ref, b_ref, da_ref, db_ref, dga_ref, dgb_ref):
        dmv = dm_ref[...]
        sa, sb = _sig(ga_ref[...]), _sig(gb_ref[...])
        da_ref[...] = (dmv * sa).astype(BF16)
        db_ref[...] = (dmv * sb).astype(BF16)
        dga_ref[...] = (dmv * a_ref[...] * sa * (1.0 - sa)).astype(BF16)
        dgb_ref[...] = (dmv * b_ref[...] * sb * (1.0 - sb)).astype(BF16)
    return _rowcall("gate_bwd", body, S, 512,
                    [(dm, D, 0), (proj, D, gacol), (proj, D, gbcol), (ba, D, 0), (bb, D, 0)], [],
                    [(D, BF16)] * 4, [])


def _resid_norm2(x, mo, g1, g, sc, sh, S, D):
    def body(x_ref, mo_ref, g1_ref, g_ref, sc_ref, sh_ref, x1_ref, h_ref):
        x1 = x_ref[...] + g1_ref[...] * mo_ref[...]
        x1_ref[...] = x1
        r = lax.rsqrt(jnp.mean(x1 * x1, axis=-1, keepdims=True) + NORM_EPS)
        h_ref[...] = ((x1 * r * g_ref[...]) * (1.0 + sc_ref[...]) + sh_ref[...]).astype(BF16)
    return _rowcall("resid_norm2", body, S, 512, [(x, D, 0), (mo, D, 0)], [g1, g, sc, sh],
                    [(D, F32), (D, BF16)], [])


def _loss_dy(x1, ml, tgt, g2, S, D):
    def body(x1_ref, ml_ref, t_ref, g2_ref, dy_ref, dml_ref, sq_ref, dg2_ref):
        mlv = ml_ref[...]
        diff = x1_ref[...] + g2_ref[...] * mlv - t_ref[...]
        dy = diff * (1.0 / D)
        dy_ref[...] = dy
        dml_ref[...] = (dy * g2_ref[...]).astype(BF16)
        sq_ref[...] += _csum(diff * diff)
        dg2_ref[...] += _csum(dy * mlv)
    return _rowcall("loss_dy", body, S, 512, [(x1, D, 0), (ml, D, 0), (tgt, D, 0)], [g2],
                    [(D, F32), (D, BF16)], [(1, D), (1, D)])


def _norm_bwd(name, xin, dh, dres, g, sc, S, D, extra=None):
    def body(*refs):
        if extra is None:
            x_ref, dh_ref, dr_ref, g_ref, sc_ref, dx_ref, dsh_ref, dsc_ref, dg_ref = refs
        else:
            (x_ref, dh_ref, dr_ref, mo_ref, g_ref, sc_ref, g1_ref,
             dx_ref, dmo_ref, dsh_ref, dsc_ref, dg_ref, dg1_ref) = refs
        xv, dhv, gv = x_ref[...], dh_ref[...], g_ref[...]
        r = lax.rsqrt(jnp.mean(xv * xv, axis=-1, keepdims=True) + NORM_EPS)
        xh = xv * r
        dsh_ref[...] += _csum(dhv)
        dsc_ref[...] += _csum(dhv * xh * gv)
        dxg = dhv * (1.0 + sc_ref[...])
        dg_ref[...] += _csum(dxg * xh)
        dxh = dxg * gv
        dx = dr_ref[...] + r * (dxh - xh * jnp.mean(dxh * xh, axis=-1, keepdims=True))
        dx_ref[...] = dx
        if extra is not None:
            dmo_ref[...] = (dx * g1_ref[...]).astype(BF16)
            dg1_ref[...] += _csum(dx * mo_ref[...])

    rows = [(xin, D, 0), (dh, D, 0), (dres, D, 0)]
    vecs = [g, sc]
    if extra is None:
        return _rowcall(name, body, S, 512, rows, vecs, [(D, F32)], [(1, D)] * 3)
    return _rowcall(name, body, S, 512, rows + [(extra[0], D, 0)], vecs + [extra[1]],
                    [(D, F32), (D, BF16)], [(1, D)] * 4)


def _ada_fwd(c_all, w, b_part):
    B, D = c_all.shape
    N = w.shape[1]
    tn = min(512, N)

    def kern(c_ref, w_ref, b_ref, o_ref):
        cv = c_ref[...]
        ca = cv * _sig(cv)
        o_ref[...] = jnp.dot(ca, w_ref[...], precision=lax.Precision.HIGHEST,
                             preferred_element_type=F32) + b_ref[...]

    return _pcall(
        kern, name="ada_fwd", grid=(N // tn,),
        in_specs=[pl.BlockSpec((B, D), lambda j: (0, 0)), pl.BlockSpec((D, tn), lambda j: (0, j)),
                  pl.BlockSpec((1, tn), lambda j: (0, j))],
        out_specs=pl.BlockSpec((B, tn), lambda j: (0, j)),
        out_shape=jax.ShapeDtypeStruct((B, N), F32),
        compiler_params=_cparams(("parallel",)),
    )(c_all, w, b_part)


def _ada_wgrad(c_t_pad, dmod_pad):
    D = c_t_pad.shape[0]
    N = dmod_pad.shape[1]
    tn = min(512, N)

    def kern(c_ref, d_ref, o_ref):
        cv = c_ref[...]
        ca = cv * _sig(cv)
        o_ref[...] = jnp.dot(ca, d_ref[...], precision=lax.Precision.HIGHEST,
                             preferred_element_type=F32)

    return _pcall(
        kern, name="ada_wgrad", grid=(N // tn,),
        in_specs=[pl.BlockSpec((D, LANES), lambda j: (0, 0)), pl.BlockSpec((LANES, tn), lambda j: (0, j))],
        out_specs=pl.BlockSpec((D, tn), lambda j: (0, j)),
        out_shape=jax.ShapeDtypeStruct((D, N), F32),
        compiler_params=_cparams(("parallel",)),
    )(c_t_pad, dmod_pad)


def _ag_small(name, arrs):
    n = len(arrs)

    def kern(*refs):
        ins, outs = refs[:n], refs[n:2 * n]
        send, recv = refs[2 * n], refs[2 * n + 1]
        x, y, c = lax.axis_index("x"), lax.axis_index("y"), lax.axis_index("c")
        me = 4 * x + 2 * y + c

        def copy(i, m, slot):
            peer = (x ^ ((m >> 2) & 1), y ^ ((m >> 1) & 1), c ^ (m & 1))
            return pltpu.make_async_remote_copy(
                src_ref=ins[i], dst_ref=outs[i].at[slot],
                send_sem=send.at[i * 7 + m - 1], recv_sem=recv.at[i * 7 + m - 1],
                device_id=peer, device_id_type=MESH)

        for i in range(n):
            outs[i][me] = ins[i][...]
            for m in range(1, 8):
                copy(i, m, me).start()
        for i in range(n):
            for m in range(1, 8):
                copy(i, m, me).wait_send()
                copy(i, m, me ^ m).wait_recv()

    vm = pl.BlockSpec(memory_space=pltpu.VMEM)
    return _pcall(
        kern, name=name, in_specs=[vm] * n, out_specs=[vm] * n,
        out_shape=[jax.ShapeDtypeStruct((8,) + a.shape, a.dtype) for a in arrs],
        scratch_shapes=[pltpu.SemaphoreType.DMA((7 * n,)), pltpu.SemaphoreType.DMA((7 * n,))],
        compiler_params=pltpu.CompilerParams(has_side_effects=True),
    )(*arrs)


def _chip_exchange(name, arrs, gather):
    n = len(arrs)

    def kern(*refs):
        ins, outs = refs[:n], refs[n:2 * n]
        send, recv, loc = refs[2 * n], refs[2 * n + 1], refs[2 * n + 2]
        x, y, c = lax.axis_index("x"), lax.axis_index("y"), lax.axis_index("c")
        me = 2 * x + y

        def local(i):
            return pltpu.make_async_copy(ins[i] if gather else ins[i].at[me], outs[i].at[me], loc.at[i])

        def copy(i, m, slot):
            px, py = x ^ ((m >> 1) & 1), y ^ (m & 1)
            return pltpu.make_async_remote_copy(
                src_ref=ins[i] if gather else ins[i].at[2 * px + py], dst_ref=outs[i].at[slot],
                send_sem=send.at[i * 3 + m - 1], recv_sem=recv.at[i * 3 + m - 1],
                device_id=(px, py, c), device_id_type=MESH)

        for i in range(n):
            local(i).start()
            for m in range(1, 4):
                copy(i, m, me).start()
        for i in range(n):
            local(i).wait()
            for m in range(1, 4):
                copy(i, m, me).wait_send()
                copy(i, m, me ^ m).wait_recv()

    shp = (lambda a: (4,) + a.shape) if gather else (lambda a: a.shape)
    return _pcall(
        kern, name=name, in_specs=[ANY] * n, out_specs=[ANY] * n,
        out_shape=[jax.ShapeDtypeStruct(shp(a), a.dtype) for a in arrs],
        scratch_shapes=[pltpu.SemaphoreType.DMA((3 * n,)), pltpu.SemaphoreType.DMA((3 * n,)),
                        pltpu.SemaphoreType.DMA((n,))],
        compiler_params=pltpu.CompilerParams(has_side_effects=True),
    )(*arrs)


def _pair_exchange(name, arrs):
    n = len(arrs)

    def kern(*refs):
        ins, outs = refs[:n], refs[n:2 * n]
        send, recv, loc = refs[2 * n], refs[2 * n + 1], refs[2 * n + 2]
        x, y, c = lax.axis_index("x"), lax.axis_index("y"), lax.axis_index("c")

        def local(i):
            return pltpu.make_async_copy(ins[i], outs[i].at[c], loc.at[i])

        def copy(i, slot):
            return pltpu.make_async_remote_copy(
                src_ref=ins[i], dst_ref=outs[i].at[slot], send_sem=send.at[i], recv_sem=recv.at[i],
                device_id=(x, y, 1 - c), device_id_type=MESH)

        for i in range(n):
            local(i).start()
            copy(i, c).start()
        for i in range(n):
            local(i).wait()
            copy(i, c).wait_send()
            copy(i, 1 - c).wait_recv()

    return _pcall(
        kern, name=name, in_specs=[ANY] * n, out_specs=[ANY] * n,
        out_shape=[jax.ShapeDtypeStruct((2,) + a.shape, a.dtype) for a in arrs],
        scratch_shapes=[pltpu.SemaphoreType.DMA((n,)), pltpu.SemaphoreType.DMA((n,)),
                        pltpu.SemaphoreType.DMA((n,))],
        compiler_params=pltpu.CompilerParams(has_side_effects=True),
    )(*arrs)


def _row_tile(R):
    for t in (256, 128, 64, 32, 16, 8):
        if R % t == 0:
            return t
    return R


def _sum_slots(name, parts):
    K, R, C = parts.shape
    tr = _row_tile(R)

    def kern(p_ref, o_ref):
        acc = p_ref[0].astype(F32)
        for k in range(1, K):
            acc = acc + p_ref[k].astype(F32)
        o_ref[...] = acc

    return _pcall(
        kern, name=name, grid=(R // tr,),
        in_specs=[pl.BlockSpec((K, tr, C), lambda i: (0, i, 0))],
        out_specs=pl.BlockSpec((tr, C), lambda i: (i, 0)),
        out_shape=jax.ShapeDtypeStruct((R, C), F32),
        compiler_params=_cparams(("parallel",)),
    )(parts)


def _adamw(name, w, m, v, gparts):
    R, C = w.shape
    K = gparts.shape[0]
    tr = _row_tile(R)
    c1 = 1.0 - ADAM_B1 ** ADAM_STEP
    c2 = 1.0 - ADAM_B2 ** ADAM_STEP

    def kern(w_ref, m_ref, v_ref, g_ref, go_ref, d_ref, mo_ref, vo_ref):
        g = g_ref[0]
        for k in range(1, K):
            g = g + g_ref[k]
        mn = ADAM_B1 * m_ref[...] + (1.0 - ADAM_B1) * g
        vn = ADAM_B2 * v_ref[...] + (1.0 - ADAM_B2) * (g * g)
        go_ref[...] = g
        mo_ref[...] = mn
        vo_ref[...] = vn
        d_ref[...] = -ADAM_LR * ((mn / c1) / (jnp.sqrt(vn / c2) + ADAM_EPS) + ADAM_WD * w_ref[...])

    spec = pl.BlockSpec((tr, C), lambda i: (i, 0))
    return _pcall(
        kern, name=name, grid=(R // tr,),
        in_specs=[spec, spec, spec, pl.BlockSpec((K, tr, C), lambda i: (0, i, 0))],
        out_specs=[spec] * 4,
        out_shape=[jax.ShapeDtypeStruct((R, C), F32)] * 4,
        compiler_params=_cparams(("parallel",)),
    )(w, m, v, gparts)


def _round_up(a, b):
    return (a + b - 1) // b * b


def kernel(x, c, w_ada, b_ada, norm1_g, w_in, b_forget, q_norm_g, k_norm_g, w_attn_proj, conv_w, conv_b, conv_ln_g, conv_ln_b, w_conv_proj, w_out, norm2_g, w_mlp1, w_mlp2, loss_target, m_w_ada, m_b_ada, m_norm1_g, m_w_in, m_b_forget, m_q_norm_g, m_k_norm_g, m_w_attn_proj, m_conv_w, m_conv_b, m_conv_ln_g, m_conv_ln_b, m_w_conv_proj, m_w_out, m_norm2_g, m_w_mlp1, m_w_mlp2, v_w_ada, v_b_ada, v_norm1_g, v_w_in, v_b_forget, v_q_norm_g, v_k_norm_g, v_w_attn_proj, v_conv_w, v_conv_b, v_conv_ln_g, v_conv_ln_b, v_w_conv_proj, v_w_out, v_norm2_g, v_w_mlp1, v_w_mlp2):
    S, D = x.shape[1], x.shape[2]
    NH, HD = b_forget.shape[-1], q_norm_g.shape[-1]
    TAPS = conv_w.shape[1]
    DIN_S = w_in.shape[-1]
    DIN = 4 * DIN_S
    DFF_S = w_mlp1.shape[-1]
    DFF = 4 * DFF_S
    ADA_S = w_ada.shape[-1]
    DS = w_attn_proj.shape[1]
    CS = conv_w.shape[-1]
    assert NH * HD == D and DIN == 7 * D + NH and TAPS - 1 <= HALO and D % LANES == 0
    NP = _round_up(7 * D + LANES, 512)
    TQ = min(512, S)
    NQ = S // TQ
    FCOL = 7 * D // LANES

    xi, yi, ci = lax.axis_index("x"), lax.axis_index("y"), lax.axis_index("c")
    chip = 2 * xi + yi
    dev = 4 * xi + 2 * yi + ci

    x2 = x.reshape(S, D)
    tgt = loss_target.reshape(S, D)

    lane_head = jnp.arange(D, dtype=jnp.int32) // HD
    grp = (lane_head[:, None] == jnp.arange(LANES, dtype=jnp.int32)[None, :]).astype(BF16)
    grp_t = grp.T
    ch = min(256, S)
    ii = jnp.arange(ch, dtype=jnp.int32)
    tri = (ii[None, :] <= ii[:, None]).astype(BF16)
    tri_u = tri.T
    gq_t = jnp.tile(q_norm_g.reshape(1, HD), (1, NH))
    gk_t = jnp.tile(k_norm_g.reshape(1, HD), (1, NH))
    bf_pad = jnp.pad(b_forget.reshape(1, NH), ((0, 0), (0, LANES - NH)))

    (c_all,) = _ag_small("ag_c", [c.reshape(1, D)])
    c_all = c_all.reshape(8, D)
    b_part = lax.dynamic_slice(b_ada.reshape(1, -1), (0, chip * ADA_S), (1, ADA_S))
    mod_part = _ada_fwd(c_all, w_ada.reshape(D, ADA_S), b_part)
    (mod_all,) = _ag_small("ag_mod", [mod_part])
    mod_full = jnp.concatenate([mod_all[0], mod_all[2], mod_all[4], mod_all[6]], axis=1)
    mod = lax.dynamic_slice(mod_full, (dev, 0), (1, 6 * D))
    sh1, sc1, g1, sh2, sc2, g2 = [mod[:, i * D:(i + 1) * D] for i in range(6)]

    shards = [w_in.reshape(D, DIN_S), w_attn_proj.reshape(DS, D), w_conv_proj.reshape(DS, D),
              w_out.reshape(DS, D), w_mlp1.reshape(D, DFF_S), w_mlp2.reshape(DFF_S, D)]
    gw_in, gw_ap, gw_cp, gw_out, gw_m1, gw_m2 = _chip_exchange(
        "ag_weights", [s.astype(BF16) for s in shards], gather=True)
    (cw_all,) = _ag_small("ag_convw", [jnp.pad(conv_w.reshape(TAPS, CS), ((0, HALO - TAPS), (0, 0)))])
    w_conv = jnp.concatenate([cw_all[0], cw_all[2], cw_all[4], cw_all[6]], axis=1)

    w_in_full = jnp.concatenate([gw_in[k] for k in range(4)], axis=1)
    w_in_p = jnp.concatenate(
        [w_in_full[:, :3 * D], w_in_full[:, 3 * D + NH:], w_in_full[:, 3 * D:3 * D + NH],
         jnp.zeros((D, NP - 7 * D - NH), BF16)], axis=1)
    w_ap = gw_ap.reshape(D, D)
    w_cp = gw_cp.reshape(D, D)
    w_o = gw_out.reshape(D, D)
    w_m1 = jnp.transpose(gw_m1, (1, 0, 2)).reshape(D, DFF)
    w_m2 = gw_m2.reshape(DFF, D)

    n1g = norm1_g.reshape(1, D)
    n2g = norm2_g.reshape(1, D)
    h = _norm_mod("norm_mod1", x2, n1g, sc1, sh1, S, D)
    proj = _mm("mm_in", h, w_in_p, "nn", [F32])
    qs, kn, vb = _qk_prep(proj, gq_t, gk_t, grp, grp_t, S, D, HD)
    f_cum = _fgate_fwd(proj, FCOL, bf_pad, tri, S)
    fk_c = f_cum[:, :NH]
    fk_r = fk_c.T.reshape(NH, NQ, 1, TQ)
    fk_b = jnp.repeat(fk_c, HD, axis=1)
    o, o32, lse_b = _flash_fwd(qs, kn, vb, fk_r, S, D, HD, TQ)
    br_a = _mm("mm_attn_proj", o, w_ap, "nn", [F32])
    cb, clg, clb = conv_b.reshape(1, D), conv_ln_g.reshape(1, D), conv_ln_b.reshape(1, D)
    u1, u3 = _conv_fwd(proj, 3, 4, w_conv, cb, clg, clb, S, D, TAPS, 256)
    br_b = _mm("mm_conv_proj", u3, w_cp, "nn", [F32])
    merged = _gate_merge(proj, 5, 6, br_a, br_b, S, D)
    mo = _mm("mm_out", merged, w_o, "nn", [F32])
    x1, h2 = _resid_norm2(x2, mo, g1, n2g, sc2, sh2, S, D)

    def relu2(r):
        rp = jnp.maximum(r, 0.0)
        return r, rp * rp
    a_pre, z = _mm("mm_mlp1", h2, w_m1, "nn", [F32, BF16], epi=relu2)
    ml = _mm("mm_mlp2", z, w_m2, "nn", [F32])
    dy, dml, sq, dg2 = _loss_dy(x1, ml, tgt, g2, S, D)
    loss = lax.psum(0.5 * jnp.sum(sq) / D, ("x", "y", "c"))

    da = _mm("mm_dz", dml, w_m2, "nt", [BF16], epi=lambda r, a: (r * 2.0 * jnp.maximum(a, 0.0),),
             extras=(a_pre,))
    dw_m2 = _mm("mm_dw_mlp2", z, dml, "tn", [F32])
    dw_m1 = _mm("mm_dw_mlp1", h2, da, "tn", [F32])
    dh2 = _mm("mm_dh2", da, w_m1, "nt", [F32])
    dx1, dmo, dsh2, dsc2, dn2g, dg1 = _norm_bwd("norm2_bwd", x1, dh2, dy, n2g, sc2, S, D, extra=(mo, g1))
    dmerged = _mm("mm_dmerged", dmo, w_o, "nt", [F32])
    dw_o = _mm("mm_dw_out", merged, dmo, "tn", [F32])
    dba, dbb, dga, dgb = _gate_bwd(dmerged, proj, 5, 6, br_a, br_b, S, D)
    do = _mm("mm_do", dba, w_ap, "nt", [BF16])
    dw_ap = _mm("mm_dw_attn_proj", o, dba, "tn", [F32])
    du3 = _mm("mm_du3", dbb, w_cp, "nt", [F32])
    dw_cp = _mm("mm_dw_conv_proj", u3, dbb, "tn", [F32])
    dglu_a, dglu_b, dcw, dcb, dclg, dclb = _conv_bwd(du3, u1, proj, 3, 4, w_conv, clg, clb, S, D, TAPS, 256)

    delta_b = _delta_prep(do, o32, grp, grp_t, S, D)
    to_rows = lambda b: b[:, ::HD].T.reshape(NH, NQ, 1, TQ)
    dkn, dv, dfk_b = _flash_bwd_kv(qs, kn, vb, do, fk_b, to_rows(lse_b), to_rows(delta_b), S, D, HD, TQ)
    dqs, dfq_b = _flash_bwd_q(qs, kn, vb, do, fk_r, lse_b, delta_b, S, D, HD, TQ)
    dq, dk, sq_q, sq_k = _qk_bwd(proj, dqs, dkn, gq_t, gk_t, grp, grp_t, S, D, HD)
    dfk_pad = jnp.pad(dfk_b[:, ::HD], ((0, 0), (0, LANES - NH)))
    dfq_pad = jnp.pad(dfq_b[:, ::HD], ((0, 0), (0, LANES - NH)))
    df, dbf = _fgate_bwd(dfk_pad, dfq_pad, proj, FCOL, bf_pad, tri_u, NH, S)
    dproj = jnp.concatenate(
        [dq, dk, dv, dglu_a, dglu_b, dga, dgb, df, jnp.zeros((S, NP - 7 * D - LANES), BF16)], axis=1)
    dw_in_p = _mm("mm_dw_in", h, dproj, "tn", [F32])
    dh = _mm("mm_dh", dproj, w_in_p, "nt", [F32])
    gx, dsh1, dsc1, dn1g = _norm_bwd("norm1_bwd", x2, dh, dx1, n1g, sc1, S, D)

    packed = jnp.concatenate([dsh1, dsc1, dg1, dsh2, dsc2, dg2, dn1g, dcb, dclg, dclb, dn2g,
                              sq_q, sq_k, dbf], axis=1)
    small_all, dcw_all = _ag_small("ag_small_grads", [packed, dcw])
    small = _sum_slots("sum_small", small_all.reshape(8, 1, -1)).reshape(1, -1)
    dmod_sum = small[:, :6 * D]
    seg = lambda k: small[:, (6 + k) * D:(7 + k) * D]
    g_n1g, g_cb, g_clg, g_clb, g_n2g = seg(0), seg(1), seg(2), seg(3), seg(4)
    g_qn = _sum_slots("sum_qn", seg(5).reshape(NH, 1, HD))
    g_kn = _sum_slots("sum_kn", seg(6).reshape(NH, 1, HD))
    g_bf = small[:, 13 * D:13 * D + NH]
    dcw_mine = lax.dynamic_slice(dcw_all[:, :TAPS, :], (0, 0, chip * CS), (8, TAPS, CS))

    dmod_all = small_all.reshape(8, -1)[:, :6 * D]
    dmod_cols = lax.dynamic_slice(dmod_all, (0, chip * ADA_S), (8, ADA_S))
    c_t_pad = jnp.pad(c_all.T, ((0, 0), (0, LANES - 8)))
    g_wada = _ada_wgrad(c_t_pad, jnp.pad(dmod_cols, ((0, LANES - 8), (0, 0))))

    dw_in_full = jnp.concatenate(
        [dw_in_p[:, :3 * D], dw_in_p[:, 7 * D:7 * D + NH], dw_in_p[:, 3 * D:7 * D]], axis=1)
    parts = [jnp.transpose(dw_in_full.reshape(D, 4, DIN_S), (1, 0, 2)),
             dw_ap.reshape(4, DS, D), dw_cp.reshape(4, DS, D), dw_o.reshape(4, DS, D),
             jnp.transpose(dw_m1.reshape(D, 4, DFF_S), (1, 0, 2)), dw_m2.reshape(4, DFF_S, D)]
    recvd = _chip_exchange("rs_grads", [p.astype(BF16) for p in parts], gather=False)
    names = ["w_in", "w_attn_proj", "w_conv_proj", "w_out", "w_mlp1", "w_mlp2"]
    sums = [_sum_slots("sum_" + nm, r) for nm, r in zip(names, recvd)]
    pairs = _pair_exchange("pair_grads", sums)

    res = {}
    big = dict(zip(names, pairs))
    big_w = {"w_in": (w_in, m_w_in, v_w_in), "w_attn_proj": (w_attn_proj, m_w_attn_proj, v_w_attn_proj),
             "w_conv_proj": (w_conv_proj, m_w_conv_proj, v_w_conv_proj), "w_out": (w_out, m_w_out, v_w_out),
             "w_mlp1": (w_mlp1, m_w_mlp1, v_w_mlp1), "w_mlp2": (w_mlp2, m_w_mlp2, v_w_mlp2)}
    for nm in names:
        w_, m_, v_ = big_w[nm]
        shp = w_.shape
        r2 = lambda t: t.reshape(shp[1], shp[2])
        outs = _adamw("adamw_" + nm, r2(w_), r2(m_), r2(v_), big[nm])
        res[nm] = [t.reshape(shp) for t in outs]
    outs = _adamw("adamw_w_ada", w_ada.reshape(D, ADA_S), m_w_ada.reshape(D, ADA_S),
                  v_w_ada.reshape(D, ADA_S), g_wada.reshape(1, D, ADA_S))
    res["w_ada"] = [t.reshape(w_ada.shape) for t in outs]
    outs = _adamw("adamw_conv_w", conv_w.reshape(TAPS, CS), m_conv_w.reshape(TAPS, CS),
                  v_conv_w.reshape(TAPS, CS), dcw_mine)
    res["conv_w"] = [t.reshape(conv_w.shape) for t in outs]

    small_w = [("b_ada", b_ada, m_b_ada, v_b_ada, dmod_sum), ("norm1_g", norm1_g, m_norm1_g, v_norm1_g, g_n1g),
               ("b_forget", b_forget, m_b_forget, v_b_forget, g_bf),
               ("q_norm_g", q_norm_g, m_q_norm_g, v_q_norm_g, g_qn),
               ("k_norm_g", k_norm_g, m_k_norm_g, v_k_norm_g, g_kn),
               ("conv_b", conv_b, m_conv_b, v_conv_b, g_cb), ("conv_ln_g", conv_ln_g, m_conv_ln_g, v_conv_ln_g, g_clg),
               ("conv_ln_b", conv_ln_b, m_conv_ln_b, v_conv_ln_b, g_clb),
               ("norm2_g", norm2_g, m_norm2_g, v_norm2_g, g_n2g)]
    cat = lambda ts: jnp.concatenate([t.reshape(1, -1) for t in ts], axis=1)
    outs = _adamw("adamw_small", cat([t[1] for t in small_w]), cat([t[2] for t in small_w]),
                  cat([t[3] for t in small_w]), cat([t[4] for t in small_w]).reshape(1, 1, -1))
    off = 0
    for nm, w_, _, _, _ in small_w:
        n = w_.size
        res[nm] = [t[:, off:off + n].reshape(w_.shape) for t in outs]
        off += n

    order = ["w_ada", "b_ada", "norm1_g", "w_in", "b_forget", "q_norm_g", "k_norm_g", "w_attn_proj", "conv_w",
             "conv_b", "conv_ln_g", "conv_ln_b", "w_conv_proj", "w_out", "norm2_g", "w_mlp1", "w_mlp2"]
    return (loss, gx.reshape(x.shape), *[res[n][0] for n in order], *[res[n][1] for n in order],
            *[res[n][2] for n in order], *[res[n][3] for n in order])
```

```python
import functools

import jax
import jax.numpy as jnp
from jax import lax
from jax.experimental import pallas as pl
from jax.experimental.pallas import tpu as pltpu

F32 = jnp.float32
BF16 = jnp.bfloat16
MESH = pl.DeviceIdType.MESH
ANY = pl.BlockSpec(memory_space=pl.ANY)

NORM_EPS = 1e-6
ADAM_LR = 0.001
ADAM_B1 = 0.9
ADAM_B2 = 0.999
ADAM_EPS = 1e-08
ADAM_WD = 0.01
ADAM_STEP = 10
LANES = 128
HALO = 32
NEG = -1e30
VMEM_LIMIT = 56 * 1024 * 1024


def _pcall(body, **kw):
    return pl.pallas_call(body, **kw)


def _cparams(sem=None):
    if sem is None:
        return pltpu.CompilerParams(vmem_limit_bytes=VMEM_LIMIT)
    return pltpu.CompilerParams(dimension_semantics=sem, vmem_limit_bytes=VMEM_LIMIT)


def _sig(x):
    return 1.0 / (1.0 + jnp.exp(-x))


def _split3(x):
    x1 = x.astype(BF16)
    r = x - x1.astype(F32)
    x2 = r.astype(BF16)
    x3 = (r - x2.astype(F32)).astype(BF16)
    return x1, x2, x3


def _dot_rs(x, e):
    out = None
    for t in _split3(x):
        d = jnp.dot(t, e, preferred_element_type=F32)
        out = d if out is None else out + d
    return out


def _dot_ls(e, x):
    out = None
    for t in _split3(x):
        d = jnp.dot(e, t, preferred_element_type=F32)
        out = d if out is None else out + d
    return out


def _tile(n, want):
    if n <= want:
        return n
    t = want - want % LANES
    while n % t:
        t -= LANES
    assert t > 0, (n, want)
    return t


_DIMS = {"nn": ((1,), (0,)), "nt": ((1,), (1,)), "tn": ((0,), (0,))}


def _mm(name, a, b, mode, out_dtypes, epi=None, extras=(), tm=1024, tn=1024, tk=1024):
    if mode == "nn":
        (M, K), (_, N) = a.shape, b.shape
    elif mode == "nt":
        (M, K), (N, _) = a.shape, b.shape
    else:
        (K, M), (_, N) = a.shape, b.shape
    tm, tn, tk = _tile(M, tm), _tile(N, tn), _tile(K, tk)
    nk = K // tk
    ne, no = len(extras), len(out_dtypes)
    dims = (_DIMS[mode], ((), ()))

    def kern(*refs):
        a_ref, b_ref = refs[0], refs[1]
        e_refs = refs[2:2 + ne]
        o_refs = refs[2 + ne:2 + ne + no]
        d = lax.dot_general(a_ref[...], b_ref[...], dims, preferred_element_type=F32)

        def finish(r):
            outs = (r,) if epi is None else epi(r, *[e[...] for e in e_refs])
            for o_ref, o in zip(o_refs, outs):
                o_ref[...] = o.astype(o_ref.dtype)

        if nk == 1:
            finish(d)
            return
        acc = refs[-1]
        k = pl.program_id(2)

        @pl.when(k == 0)
        def _():
            acc[...] = d

        @pl.when((k > 0) & (k < nk - 1))
        def _():
            acc[...] += d

        @pl.when(k == nk - 1)
        def _():
            finish(acc[...] + d)

    if mode == "tn":
        a_spec = pl.BlockSpec((tk, tm), lambda i, j, k: (k, i))
    else:
        a_spec = pl.BlockSpec((tm, tk), lambda i, j, k: (i, k))
    if mode == "nt":
        b_spec = pl.BlockSpec((tn, tk), lambda i, j, k: (j, k))
    else:
        b_spec = pl.BlockSpec((tk, tn), lambda i, j, k: (k, j))
    mn_spec = pl.BlockSpec((tm, tn), lambda i, j, k: (i, j))
    outs = _pcall(
        kern, name=name, grid=(M // tm, N // tn, nk),
        in_specs=[a_spec, b_spec] + [mn_spec] * ne,
        out_specs=[mn_spec] * no,
        out_shape=[jax.ShapeDtypeStruct((M, N), dt) for dt in out_dtypes],
        scratch_shapes=[pltpu.VMEM((tm, tn), F32)] if nk > 1 else [],
        compiler_params=_cparams(("parallel", "parallel", "arbitrary")),
    )(a, b, *extras)
    return outs[0] if no == 1 else outs


def _rowcall(name, body, S, ts, row_ins, vec_ins, row_outs, vec_outs):
    ts = min(ts, S)
    nri, nvi, nro, nvo = len(row_ins), len(vec_ins), len(row_outs), len(vec_outs)

    def kern(*refs):
        ins = refs[:nri + nvi]
        outs = refs[nri + nvi:]
        if nvo:
            @pl.when(pl.program_id(0) == 0)
            def _():
                for r in outs[nro:]:
                    r[...] = jnp.zeros(r.shape, r.dtype)
        body(*ins, *outs)

    in_specs = [pl.BlockSpec((ts, w), functools.partial(lambda i, cb: (i, cb), cb=cb))
                for (_, w, cb) in row_ins]
    in_specs += [pl.BlockSpec(v.shape, lambda i: (0, 0)) for v in vec_ins]
    out_specs = [pl.BlockSpec((ts, w), lambda i: (i, 0)) for (w, _) in row_outs]
    out_specs += [pl.BlockSpec((r, w), lambda i: (0, 0)) for (r, w) in vec_outs]
    out_shape = [jax.ShapeDtypeStruct((S, w), dt) for (w, dt) in row_outs]
    out_shape += [jax.ShapeDtypeStruct((r, w), F32) for (r, w) in vec_outs]
    return _pcall(
        kern, name=name, grid=(S // ts,), in_specs=in_specs, out_specs=out_specs,
        out_shape=out_shape,
        compiler_params=_cparams(("arbitrary",) if nvo else ("parallel",)),
    )(*[a for (a, _, _) in row_ins], *vec_ins)


def _csum(x):
    return jnp.sum(x, axis=0, keepdims=True)


def _norm_mod(name, x, g, sc, sh, S, D):
    def body(x_ref, g_ref, sc_ref, sh_ref, h_ref):
        xv = x_ref[...]
        r = lax.rsqrt(jnp.mean(xv * xv, axis=-1, keepdims=True) + NORM_EPS)
        h_ref[...] = ((xv * r * g_ref[...]) * (1.0 + sc_ref[...]) + sh_ref[...]).astype(BF16)
    return _rowcall(name, body, S, 512, [(x, D, 0)], [g, sc, sh], [(D, BF16)], [])[0]


def _head_rstd(v, grp, grp_t, hd):
    ss = _dot_rs(v * v, grp) * (1.0 / hd)
    r = lax.rsqrt(ss + NORM_EPS)
    return _dot_rs(r, grp_t)


def _qk_prep(proj, gq, gk, grp, grp_t, S, D, hd):
    scale = hd ** -0.5

    def body(q_ref, k_ref, v_ref, gq_ref, gk_ref, g_ref, gt_ref, qs_ref, kn_ref, vb_ref):
        q = q_ref[...]
        k = k_ref[...]
        rq = _head_rstd(q, g_ref[...], gt_ref[...], hd)
        rk = _head_rstd(k, g_ref[...], gt_ref[...], hd)
        qs_ref[...] = ((q * rq * gq_ref[...]).astype(BF16).astype(F32) * scale).astype(BF16)
        kn_ref[...] = (k * rk * gk_ref[...]).astype(BF16)
        vb_ref[...] = v_ref[...].astype(BF16)

    return _rowcall("qk_prep", body, S, 256, [(proj, D, 0), (proj, D, 1), (proj, D, 2)],
                    [gq, gk, grp, grp_t], [(D, BF16)] * 3, [])


def _fgate_fwd(proj, fcol, bf_pad, tri, S):
    ch = tri.shape[0]

    def body(f_ref, b_ref, tri_ref, out_ref):
        carry = jnp.zeros((1, LANES), F32)
        for c in range(S // ch):
            z = f_ref[c * ch:(c + 1) * ch, :] + b_ref[...]
            lf = jnp.minimum(z, 0.0) - jnp.log(1.0 + jnp.exp(-jnp.abs(z)))
            out_ref[c * ch:(c + 1) * ch, :] = _dot_ls(tri_ref[...], lf) + carry
            carry = carry + _csum(lf)

    return _rowcall("fgate_fwd", body, S, S, [(proj, LANES, fcol)], [bf_pad, tri],
                    [(LANES, F32)], [])[0]


def _fgate_bwd(dfk, dfq, proj, fcol, bf_pad, tri_u, nh, S):
    ch = tri_u.shape[0]

    def body(d_ref, dq_ref, f_ref, b_ref, tri_ref, df_ref, db_ref):
        lane = lax.broadcasted_iota(jnp.int32, (ch, LANES), 1)
        carry = jnp.zeros((1, LANES), F32)
        tot = jnp.zeros((1, LANES), F32)
        for c in reversed(range(S // ch)):
            d = d_ref[c * ch:(c + 1) * ch, :] + dq_ref[c * ch:(c + 1) * ch, :]
            rc = _dot_ls(tri_ref[...], d) + carry
            carry = carry + _csum(d)
            z = f_ref[c * ch:(c + 1) * ch, :] + b_ref[...]
            df = jnp.where(lane < nh, rc * _sig(-z), 0.0)
            df_ref[c * ch:(c + 1) * ch, :] = df.astype(BF16)
            tot = tot + _csum(df)
        db_ref[...] += tot

    return _rowcall("fgate_bwd", body, S, S, [(dfk, LANES, 0), (dfq, LANES, 0), (proj, LANES, fcol)],
                    [bf_pad, tri_u], [(LANES, BF16)], [(1, LANES)])


def _keep(v, mask):
    return jnp.where(mask, v.astype(F32), 0.0).astype(BF16)


def _lane_col(blk, lane, at):
    return jnp.sum(jnp.where(lane == at, blk, 0.0), axis=-1, keepdims=True)


def _flash_fwd(qs, kn, vb, fk_r, S, D, hd, tq):
    hp, nq = D // LANES, S // tq

    def kern(q_ref, k_ref, v_ref, fk_ref, o_ref, o32_ref, lse_ref, m_sc, acc_sc):
        qi = pl.program_id(1)
        lane = lax.broadcasted_iota(jnp.int32, (tq, LANES), 1)
        row = lax.broadcasted_iota(jnp.int32, (tq, tq), 0)
        col = lax.broadcasted_iota(jnp.int32, (tq, tq), 1)
        hms = [(lane >= j * hd) & (lane < (j + 1) * hd) for j in range(2)]
        q = q_ref[...]
        qms = [_keep(q, hm) for hm in hms]
        for j in range(2):
            m_sc[j] = jnp.full((tq, 1), NEG, F32)
            acc_sc[j] = jnp.zeros((tq, LANES), F32)

        def step(ki, masked):
            off = pl.multiple_of(ki * tq, tq)
            k = k_ref[pl.ds(off, tq), :]
            v = v_ref[pl.ds(off, tq), :].astype(F32)
            for j in range(2):
                s = lax.dot_general(qms[j], k, (((1,), (1,)), ((), ())), preferred_element_type=F32)
                s = s - fk_ref[j, ki]
                if masked:
                    s = jnp.where(col <= row, s, NEG)
                m_old = m_sc[j]
                m_new = jnp.maximum(m_old, jnp.max(s, axis=-1, keepdims=True))
                alpha = jnp.exp(m_old - m_new)
                p = jnp.exp(s - m_new)
                v1 = jnp.where(hms[j], v, 1.0).astype(BF16)
                acc_sc[j] = alpha * acc_sc[j] + jnp.dot(p.astype(BF16), v1, preferred_element_type=F32)
                m_sc[j] = m_new

        def loop_body(ki, carry):
            step(ki, False)
            return carry

        lax.fori_loop(0, qi, loop_body, 0)
        step(qi, True)
        a0, a1 = acc_sc[0], acc_sc[1]
        l0, l1 = pltpu.roll(a0, hd, 1), pltpu.roll(a1, hd, 1)
        first = lane < hd
        ov = jnp.where(first, a0 / l0, a1 / l1)
        o_ref[...] = ov.astype(BF16)
        o32_ref[...] = ov
        lse_ref[...] = jnp.where(first, m_sc[0] + jnp.log(l0), m_sc[1] + jnp.log(l1))

    qspec = pl.BlockSpec((tq, LANES), lambda h, i: (i, h))
    fullspec = pl.BlockSpec((S, LANES), lambda h, i: (0, h))
    return _pcall(
        kern, name="flash_fwd", grid=(hp, nq),
        in_specs=[qspec, fullspec, fullspec,
                  pl.BlockSpec((2, nq, 1, tq), lambda h, i: (h, 0, 0, 0))],
        out_specs=[qspec, qspec, qspec],
        out_shape=[jax.ShapeDtypeStruct((S, D), BF16), jax.ShapeDtypeStruct((S, D), F32),
                   jax.ShapeDtypeStruct((S, D), F32)],
        scratch_shapes=[pltpu.VMEM((2, tq, 1), F32), pltpu.VMEM((2, tq, LANES), F32)],
        compiler_params=_cparams(("parallel", "arbitrary")),
    )(qs, kn, vb, fk_r)


def _flash_bwd_kv(qs, kn, vb, do, fk_b, lse_r, delta_r, S, D, hd, tq):
    hp, nq = D // LANES, S // tq

    def kern(q_ref, do_ref, k_ref, v_ref, fk_ref, lse_ref, dl_ref, dk_ref, dv_ref, dfq_ref,
             dk_sc, dv_sc):
        ki = pl.program_id(1)
        lane = lax.broadcasted_iota(jnp.int32, (tq, LANES), 1)
        row = lax.broadcasted_iota(jnp.int32, (tq, tq), 0)
        col = lax.broadcasted_iota(jnp.int32, (tq, tq), 1)
        hms = [(lane >= j * hd) & (lane < (j + 1) * hd) for j in range(2)]
        k = k_ref[...]
        v = v_ref[...]
        fkb = fk_ref[...]
        kms = [_keep(k, hm) for hm in hms]
        vms = [_keep(v, hm) for hm in hms]
        fks = [_lane_col(fkb, lane, j * hd) for j in range(2)]
        dk_sc[...] = jnp.zeros((tq, LANES), F32)
        dv_sc[...] = jnp.zeros((tq, LANES), F32)

        @pl.when(ki == 0)
        def _():
            dfq_ref[...] = jnp.zeros(dfq_ref.shape, F32)

        def step(qi, masked):
            off = pl.multiple_of(qi * tq, tq)
            q = q_ref[pl.ds(off, tq), :]
            g = do_ref[pl.ds(off, tq), :]
            for j in range(2):
                qm = _keep(q, hms[j])
                gm = _keep(g, hms[j])
                st = lax.dot_general(kms[j], q, (((1,), (1,)), ((), ())), preferred_element_type=F32)
                st = st - fks[j]
                if masked:
                    st = jnp.where(row <= col, st, NEG)
                pt = jnp.exp(st - lse_ref[j, qi])
                dv_sc[...] += jnp.dot(pt.astype(BF16), gm, preferred_element_type=F32)
                dpt = lax.dot_general(vms[j], g, (((1,), (1,)), ((), ())), preferred_element_type=F32)
                dst = pt * (dpt - dl_ref[j, qi])
                dk_sc[...] += jnp.dot(dst.astype(BF16), qm, preferred_element_type=F32)
                dfq_ref[j, qi] += jnp.sum(dst, axis=0, keepdims=True)

        def loop_body(qi, carry):
            step(qi, False)
            return carry

        step(ki, True)
        lax.fori_loop(ki + 1, nq, loop_body, 0)
        dk_ref[...] = dk_sc[...].astype(BF16)
        dv_ref[...] = dv_sc[...].astype(BF16)

    kspec = pl.BlockSpec((tq, LANES), lambda h, i: (i, h))
    fullspec = pl.BlockSpec((S, LANES), lambda h, i: (0, h))
    rowspec = pl.BlockSpec((2, nq, 1, tq), lambda h, i: (h, 0, 0, 0))
    return _pcall(
        kern, name="flash_bwd_kv", grid=(hp, nq),
        in_specs=[fullspec, fullspec, kspec, kspec, kspec, rowspec, rowspec],
        out_specs=[kspec, kspec, rowspec],
        out_shape=[jax.ShapeDtypeStruct((S, D), BF16), jax.ShapeDtypeStruct((S, D), BF16),
                   jax.ShapeDtypeStruct((2 * hp, nq, 1, tq), F32)],
        scratch_shapes=[pltpu.VMEM((tq, LANES), F32), pltpu.VMEM((tq, LANES), F32)],
        compiler_params=_cparams(("parallel", "arbitrary")),
    )(qs, do, kn, vb, fk_b, lse_r, delta_r)


def _flash_bwd_q(qs, kn, vb, do, fk_r, lse_b, delta_b, S, D, hd, tq):
    hp, nq = D // LANES, S // tq

    def kern(q_ref, do_ref, k_ref, v_ref, fk_ref, lse_ref, dl_ref, dq_ref, dfk_ref, dq_sc):
        qi = pl.program_id(1)
        lane = lax.broadcasted_iota(jnp.int32, (tq, LANES), 1)
        row = lax.broadcasted_iota(jnp.int32, (tq, tq), 0)
        col = lax.broadcasted_iota(jnp.int32, (tq, tq), 1)
        hms = [(lane >= j * hd) & (lane < (j + 1) * hd) for j in range(2)]
        q = q_ref[...]
        g = do_ref[...]
        lse_b_ = lse_ref[...]
        dl_b_ = dl_ref[...]
        qms = [_keep(q, hm) for hm in hms]
        gms = [_keep(g, hm) for hm in hms]
        lses = [_lane_col(lse_b_, lane, j * hd) for j in range(2)]
        dls = [_lane_col(dl_b_, lane, j * hd) for j in range(2)]
        dq_sc[...] = jnp.zeros((tq, LANES), F32)

        @pl.when(qi == 0)
        def _():
            dfk_ref[...] = jnp.zeros(dfk_ref.shape, F32)

        def step(ki, masked):
            off = pl.multiple_of(ki * tq, tq)
            k = k_ref[pl.ds(off, tq), :]
            v = v_ref[pl.ds(off, tq), :]
            for j in range(2):
                s = lax.dot_general(qms[j], k, (((1,), (1,)), ((), ())), preferred_element_type=F32)
                s = s - fk_ref[j, ki]
                if masked:
                    s = jnp.where(col <= row, s, NEG)
                p = jnp.exp(s - lses[j])
                dp = lax.dot_general(gms[j], v, (((1,), (1,)), ((), ())), preferred_element_type=F32)
                ds = p * (dp - dls[j])
                km = _keep(k, hms[j])
                dq_sc[...] += jnp.dot(ds.astype(BF16), km, preferred_element_type=F32)
                dfk_ref[j, ki] -= jnp.sum(ds, axis=0, keepdims=True)

        def loop_body(ki, carry):
            step(ki, False)
            return carry

        lax.fori_loop(0, qi, loop_body, 0)
        step(qi, True)
        dq_ref[...] = dq_sc[...].astype(BF16)

    qspec = pl.BlockSpec((tq, LANES), lambda h, i: (i, h))
    fullspec = pl.BlockSpec((S, LANES), lambda h, i: (0, h))
    rowspec = pl.BlockSpec((2, nq, 1, tq), lambda h, i: (h, 0, 0, 0))
    return _pcall(
        kern, name="flash_bwd_q", grid=(hp, nq),
        in_specs=[qspec, qspec, fullspec, fullspec, rowspec, qspec, qspec],
        out_specs=[qspec, rowspec],
        out_shape=[jax.ShapeDtypeStruct((S, D), BF16), jax.ShapeDtypeStruct((2 * hp, nq, 1, tq), F32)],
        scratch_shapes=[pltpu.VMEM((tq, LANES), F32)],
        compiler_params=_cparams(("parallel", "arbitrary")),
    )(qs, do, kn, vb, fk_r, lse_b, delta_b)


def _delta_prep(do, o, grp, grp_t, S, D):
    def body(g_ref, o_ref, e_ref, et_ref, out_ref):
        prod = g_ref[...].astype(F32) * o_ref[...]
        out_ref[...] = _dot_rs(_dot_rs(prod, e_ref[...]), et_ref[...])
    return _rowcall("delta_prep", body, S, 256, [(do, D, 0), (o, D, 0)], [grp, grp_t],
                    [(D, F32)], [])[0]


def _qk_bwd(proj, dqs, dkn, gq, gk, grp, grp_t, S, D, hd):
    scale = hd ** -0.5

    def one(x, dn, gain, e, et):
        r = _head_rstd(x, e, et, hd)
        xh = x * r
        t = dn * gain
        mean = _dot_rs(_dot_rs(t * xh, e), et) * (1.0 / hd)
        return r * (t - xh * mean), _csum(dn * xh)

    def body(q_ref, k_ref, dq_ref, dk_ref, gq_ref, gk_ref, e_ref, et_ref,
             oq_ref, ok_ref, sq_ref, sk_ref):
        e, et = e_ref[...], et_ref[...]
        dq, sq = one(q_ref[...], dq_ref[...].astype(F32) * scale, gq_ref[...], e, et)
        dk, sk = one(k_ref[...], dk_ref[...].astype(F32), gk_ref[...], e, et)
        oq_ref[...] = dq.astype(BF16)
        ok_ref[...] = dk.astype(BF16)
        sq_ref[...] += sq
        sk_ref[...] += sk

    return _rowcall("qk_bwd", body, S, 256,
                    [(proj, D, 0), (proj, D, 1), (dqs, D, 0), (dkn, D, 0)],
                    [gq, gk, grp, grp_t], [(D, BF16)] * 2, [(1, D)] * 2)


def _conv_fwd(proj, acol, bcol, w_pad, cb, lg, lb, S, C, taps, ts):
    ts = min(ts, S)

    def kern(a_ref, b_ref, w_ref, cb_ref, lg_ref, lb_ref, u1_ref, u3_ref, ubuf):
        @pl.when(pl.program_id(0) == 0)
        def _():
            ubuf[0:HALO, :] = jnp.zeros((HALO, C), F32)

        ubuf[HALO:HALO + ts, :] = a_ref[...] * _sig(b_ref[...])
        acc = jnp.zeros((ts, C), F32) + cb_ref[...]
        for k in range(taps):
            o = HALO - (taps - 1) + k
            acc = acc + w_ref[k:k + 1, :] * ubuf[o:o + ts, :]
        u1_ref[...] = acc
        mu = jnp.mean(acc, axis=-1, keepdims=True)
        xc = acc - mu
        rstd = lax.rsqrt(jnp.mean(xc * xc, axis=-1, keepdims=True) + NORM_EPS)
        u2 = xc * rstd * lg_ref[...] + lb_ref[...]
        u3_ref[...] = (u2 * _sig(u2)).astype(BF16)
        ubuf[0:HALO, :] = ubuf[ts:ts + HALO, :]

    vec = lambda a: pl.BlockSpec(a.shape, lambda i: (0, 0))
    return _pcall(
        kern, name="conv_fwd", grid=(S // ts,),
        in_specs=[pl.BlockSpec((ts, C), lambda i: (i, acol)), pl.BlockSpec((ts, C), lambda i: (i, bcol)),
                  vec(w_pad), vec(cb), vec(lg), vec(lb)],
        out_specs=[pl.BlockSpec((ts, C), lambda i: (i, 0))] * 2,
        out_shape=[jax.ShapeDtypeStruct((S, C), F32), jax.ShapeDtypeStruct((S, C), BF16)],
        scratch_shapes=[pltpu.VMEM((HALO + ts, C), F32)],
        compiler_params=_cparams(("arbitrary",)),
    )(proj, proj, w_pad, cb, lg, lb)


def _conv_bwd(du3, u1, proj, acol, bcol, w_pad, lg, lb, S, C, taps, ts):
    ts = min(ts, S)
    nt = S // ts
    hb = ts // HALO

    def ln_bwd(g, u, lgv, lbv):
        mu = jnp.mean(u, axis=-1, keepdims=True)
        xc = u - mu
        rstd = lax.rsqrt(jnp.mean(xc * xc, axis=-1, keepdims=True) + NORM_EPS)
        xh = xc * rstd
        u2 = xh * lgv + lbv
        s = _sig(u2)
        du2 = g * (s + u2 * s * (1.0 - s))
        dxh = du2 * lgv
        du1 = rstd * (dxh - jnp.mean(dxh, axis=-1, keepdims=True)
                      - xh * jnp.mean(dxh * xh, axis=-1, keepdims=True))
        return du1, du2, xh

    def kern(g_ref, u_ref, a_ref, b_ref, gn_ref, un_ref, ap_ref, bp_ref, w_ref, lg_ref, lb_ref,
             da_ref, db_ref, dw_ref, dcb_ref, dlg_ref, dlb_ref, dbuf, ubuf):
        i = pl.program_id(0)

        @pl.when(i == 0)
        def _():
            dw_ref[...] = jnp.zeros(dw_ref.shape, F32)
            dcb_ref[...] = jnp.zeros(dcb_ref.shape, F32)
            dlg_ref[...] = jnp.zeros(dlg_ref.shape, F32)
            dlb_ref[...] = jnp.zeros(dlb_ref.shape, F32)

        lgv, lbv = lg_ref[...], lb_ref[...]
        du1, du2, xh = ln_bwd(g_ref[...], u_ref[...], lgv, lbv)
        dbuf[0:ts, :] = du1
        du1n, _, _ = ln_bwd(gn_ref[...], un_ref[...], lgv, lbv)
        dbuf[ts:ts + HALO, :] = jnp.where(i < nt - 1, du1n, 0.0)
        a = a_ref[...]
        sb = _sig(b_ref[...])
        ubuf[HALO:HALO + ts, :] = a * sb
        ubuf[0:HALO, :] = jnp.where(i > 0, ap_ref[...] * _sig(bp_ref[...]), 0.0)
        dcb_ref[...] += _csum(du1)
        dlg_ref[...] += _csum(du2 * xh)
        dlb_ref[...] += _csum(du2)
        du0 = jnp.zeros((ts, C), F32)
        for k in range(taps):
            o = taps - 1 - k
            du0 = du0 + w_ref[k:k + 1, :] * dbuf[o:o + ts, :]
            ou = HALO - (taps - 1) + k
            dw_ref[k:k + 1, :] += _csum(du1 * ubuf[ou:ou + ts, :])
        da_ref[...] = (du0 * sb).astype(BF16)
        db_ref[...] = (du0 * a * sb * (1.0 - sb)).astype(BF16)

    vec = lambda a: pl.BlockSpec(a.shape, lambda i: (0, 0))
    tile = lambda cb: pl.BlockSpec((ts, C), functools.partial(lambda i, cb: (i, cb), cb=cb))
    nxt = lambda cb: pl.BlockSpec(
        (HALO, C), functools.partial(lambda i, cb: (jnp.minimum((i + 1) * hb, nt * hb - 1), cb), cb=cb))
    prv = lambda cb: pl.BlockSpec(
        (HALO, C), functools.partial(lambda i, cb: (jnp.maximum(i * hb - 1, 0), cb), cb=cb))
    return _pcall(
        kern, name="conv_bwd", grid=(nt,),
        in_specs=[tile(0), tile(0), tile(acol), tile(bcol), nxt(0), nxt(0), prv(acol), prv(bcol),
                  vec(w_pad), vec(lg), vec(lb)],
        out_specs=[pl.BlockSpec((ts, C), lambda i: (i, 0))] * 2
        + [pl.BlockSpec(w_pad.shape, lambda i: (0, 0))] + [pl.BlockSpec((1, C), lambda i: (0, 0))] * 3,
        out_shape=[jax.ShapeDtypeStruct((S, C), BF16)] * 2
        + [jax.ShapeDtypeStruct(w_pad.shape, F32)] + [jax.ShapeDtypeStruct((1, C), F32)] * 3,
        scratch_shapes=[pltpu.VMEM((ts + HALO, C), F32), pltpu.VMEM((HALO + ts, C), F32)],
        compiler_params=_cparams(("arbitrary",)),
    )(du3, u1, proj, proj, du3, u1, proj, proj, w_pad, lg, lb)


def _gate_merge(proj, gacol, gbcol, ba, bb, S, D):
    def body(ga_ref, gb_ref, a_ref, b_ref, out_ref):
        out_ref[...] = (_sig(ga_ref[...]) * a_ref[...] + _sig(gb_ref[...]) * b_ref[...]).astype(BF16)
    return _rowcall("gate_merge", body, S, 512,
                    [(proj, D, gacol), (proj, D, gbcol), (ba, D, 0), (bb, D, 0)], [], [(D, BF16)], [])[0]


def _gate_bwd(dm, proj, gacol, gbcol, ba, bb, S, D):
    def body(dm_ref, ga_ref, gb_ref, a_ref, b_ref, da_ref, db_ref, dga_ref, dgb_ref):
        dmv = dm_ref[...]
        sa, sb = _sig(ga_ref[...]), _sig(gb_ref[...])
        da_ref[...] = (dmv * sa).astype(BF16)
        db_ref[...] = (dmv * sb).astype(BF16)
        dga_ref[...] = (dmv * a_ref[...] * sa * (1.0 - sa)).astype(BF16)
        dgb_ref[...] = (dmv * b_ref[...] * sb * (1.0 - sb)).astype(BF16)
    return _rowcall("gate_bwd", body, S, 512,
                    [(dm, D, 0), (proj, D, gacol), (proj, D, gbcol), (ba, D, 0), (bb, D, 0)], [],
                    [(D, BF16)] * 4, [])


def _resid_norm2(x, mo, g1, g, sc, sh, S, D):
    def body(x_ref, mo_ref, g1_ref, g_ref, sc_ref, sh_ref, x1_ref, h_ref):
        x1 = x_ref[...] + g1_ref[...] * mo_ref[...]
        x1_ref[...] = x1
        r = lax.rsqrt(jnp.mean(x1 * x1, axis=-1, keepdims=True) + NORM_EPS)
        h_ref[...] = ((x1 * r * g_ref[...]) * (1.0 + sc_ref[...]) + sh_ref[...]).astype(BF16)
    return _rowcall("resid_norm2", body, S, 512, [(x, D, 0), (mo, D, 0)], [g1, g, sc, sh],
                    [(D, F32), (D, BF16)], [])


def _loss_dy(x1, ml, tgt, g2, S, D):
    def body(x1_ref, ml_ref, t_ref, g2_ref, dy_ref, dml_ref, sq_ref, dg2_ref):
        mlv = ml_ref[...]
        diff = x1_ref[...] + g2_ref[...] * mlv - t_ref[...]
        dy = diff * (1.0 / D)
        dy_ref[...] = dy
        dml_ref[...] = (dy * g2_ref[...]).astype(BF16)
        sq_ref[...] += _csum(diff * diff)
        dg2_ref[...] += _csum(dy * mlv)
    return _rowcall("loss_dy", body, S, 512, [(x1, D, 0), (ml, D, 0), (tgt, D, 0)], [g2],
                    [(D, F32), (D, BF16)], [(1, D), (1, D)])


def _norm_bwd(name, xin, dh, dres, g, sc, S, D, extra=None):
    def body(*refs):
        if extra is None:
            x_ref, dh_ref, dr_ref, g_ref, sc_ref, dx_ref, dsh_ref, dsc_ref, dg_ref = refs
        else:
            (x_ref, dh_ref, dr_ref, mo_ref, g_ref, sc_ref, g1_ref,
             dx_ref, dmo_ref, dsh_ref, dsc_ref, dg_ref, dg1_ref) = refs
        xv, dhv, gv = x_ref[...], dh_ref[...], g_ref[...]
        r = lax.rsqrt(jnp.mean(xv * xv, axis=-1, keepdims=True) + NORM_EPS)
        xh = xv * r
        dsh_ref[...] += _csum(dhv)
        dsc_ref[...] += _csum(dhv * xh * gv)
        dxg = dhv * (1.0 + sc_ref[...])
        dg_ref[...] += _csum(dxg * xh)
        dxh = dxg * gv
        dx = dr_ref[...] + r * (dxh - xh * jnp.mean(dxh * xh, axis=-1, keepdims=True))
        dx_ref[...] = dx
        if extra is not None:
            dmo_ref[...] = (dx * g1_ref[...]).astype(BF16)
            dg1_ref[...] += _csum(dx * mo_ref[...])

    rows = [(xin, D, 0), (dh, D, 0), (dres, D, 0)]
    vecs = [g, sc]
    if extra is None:
        return _rowcall(name, body, S, 512, rows, vecs, [(D, F32)], [(1, D)] * 3)
    return _rowcall(name, body, S, 512, rows + [(extra[0], D, 0)], vecs + [extra[1]],
                    [(D, F32), (D, BF16)], [(1, D)] * 4)


def _ada_fwd(c_all, w, b_part):
    B, D = c_all.shape
    N = w.shape[1]
    tn = min(512, N)

    def kern(c_ref, w_ref, b_ref, o_ref):
        cv = c_ref[...]
        ca = cv * _sig(cv)
        o_ref[...] = jnp.dot(ca, w_ref[...], precision=lax.Precision.HIGHEST,
                             preferred_element_type=F32) + b_ref[...]

    return _pcall(
        kern, name="ada_fwd", grid=(N // tn,),
        in_specs=[pl.BlockSpec((B, D), lambda j: (0, 0)), pl.BlockSpec((D, tn), lambda j: (0, j)),
                  pl.BlockSpec((1, tn), lambda j: (0, j))],
        out_specs=pl.BlockSpec((B, tn), lambda j: (0, j)),
        out_shape=jax.ShapeDtypeStruct((B, N), F32),
        compiler_params=_cparams(("parallel",)),
    )(c_all, w, b_part)


def _ada_wgrad(c_t_pad, dmod_pad):
    D = c_t_pad.shape[0]
    N = dmod_pad.shape[1]
    tn = min(512, N)

    def kern(c_ref, d_ref, o_ref):
        cv = c_ref[...]
        ca = cv * _sig(cv)
        o_ref[...] = jnp.dot(ca, d_ref[...], precision=lax.Precision.HIGHEST,
                             preferred_element_type=F32)

    return _pcall(
        kern, name="ada_wgrad", grid=(N // tn,),
        in_specs=[pl.BlockSpec((D, LANES), lambda j: (0, 0)), pl.BlockSpec((LANES, tn), lambda j: (0, j))],
        out_specs=pl.BlockSpec((D, tn), lambda j: (0, j)),
        out_shape=jax.ShapeDtypeStruct((D, N), F32),
        compiler_params=_cparams(("parallel",)),
    )(c_t_pad, dmod_pad)


def _ag_small(name, arrs):
    n = len(arrs)

    def kern(*refs):
        ins, outs = refs[:n], refs[n:2 * n]
        send, recv = refs[2 * n], refs[2 * n + 1]
        x, y, c = lax.axis_index("x"), lax.axis_index("y"), lax.axis_index("c")
        me = 4 * x + 2 * y + c

        def copy(i, m, slot):
            peer = (x ^ ((m >> 2) & 1), y ^ ((m >> 1) & 1), c ^ (m & 1))
            return pltpu.make_async_remote_copy(
                src_ref=ins[i], dst_ref=outs[i].at[slot],
                send_sem=send.at[i * 7 + m - 1], recv_sem=recv.at[i * 7 + m - 1],
                device_id=peer, device_id_type=MESH)

        for i in range(n):
            outs[i][me] = ins[i][...]
            for m in range(1, 8):
                copy(i, m, me).start()
        for i in range(n):
            for m in range(1, 8):
                copy(i, m, me).wait_send()
                copy(i, m, me ^ m).wait_recv()

    vm = pl.BlockSpec(memory_space=pltpu.VMEM)
    return _pcall(
        kern, name=name, in_specs=[vm] * n, out_specs=[vm] * n,
        out_shape=[jax.ShapeDtypeStruct((8,) + a.shape, a.dtype) for a in arrs],
        scratch_shapes=[pltpu.SemaphoreType.DMA((7 * n,)), pltpu.SemaphoreType.DMA((7 * n,))],
        compiler_params=pltpu.CompilerParams(has_side_effects=True),
    )(*arrs)


def _gather_weights(name, arrs):
    n = len(arrs)

    def kern(*refs):
        ins, outs = refs[:n], refs[n:2 * n]
        s1, r1, s2, r2, loc = refs[2 * n:2 * n + 5]
        x, y, c = lax.axis_index("x"), lax.axis_index("y"), lax.axis_index("c")
        me = 2 * x + y

        def half(i, hc):
            hr = ins[i].shape[0] // 2
            return pl.ds(hc * hr, hr)

        def local(i):
            return pltpu.make_async_copy(ins[i], outs[i].at[me], loc.at[i])

        def fetch(i, m, slot):
            px, py = x ^ ((m >> 1) & 1), y ^ (m & 1)
            return pltpu.make_async_remote_copy(
                src_ref=ins[i].at[half(i, c)], dst_ref=outs[i].at[slot, half(i, c)],
                send_sem=s1.at[i * 3 + m - 1], recv_sem=r1.at[i * 3 + m - 1],
                device_id=(px, py, c), device_id_type=MESH)

        def passed(i, m, hc):
            return pltpu.make_async_remote_copy(
                src_ref=outs[i].at[me ^ m, half(i, hc)], dst_ref=outs[i].at[me ^ m, half(i, hc)],
                send_sem=s2.at[i * 3 + m - 1], recv_sem=r2.at[i * 3 + m - 1],
                device_id=(x, y, 1 - c), device_id_type=MESH)

        for i in range(n):
            local(i).start()
            for m in range(1, 4):
                fetch(i, m, me).start()
        for i in range(n):
            for m in range(1, 4):
                fetch(i, m, me ^ m).wait_recv()
                passed(i, m, c).start()
        for i in range(n):
            local(i).wait()
            for m in range(1, 4):
                fetch(i, m, me).wait_send()
                passed(i, m, c).wait_send()
                passed(i, m, 1 - c).wait_recv()

    return _pcall(
        kern, name=name, in_specs=[ANY] * n, out_specs=[ANY] * n,
        out_shape=[jax.ShapeDtypeStruct((4,) + a.shape, a.dtype) for a in arrs],
        scratch_shapes=[pltpu.SemaphoreType.DMA((3 * n,))] * 4 + [pltpu.SemaphoreType.DMA((n,))],
        compiler_params=pltpu.CompilerParams(has_side_effects=True),
    )(*arrs)


def _pair_send_halves(name, arrs):
    n = len(arrs)

    def kern(*refs):
        ins, outs = refs[:n], refs[n:2 * n]
        send, recv = refs[2 * n], refs[2 * n + 1]
        x, y, c = lax.axis_index("x"), lax.axis_index("y"), lax.axis_index("c")

        def copy(i, k, hc):
            return pltpu.make_async_remote_copy(
                src_ref=ins[i].at[k, hc], dst_ref=outs[i].at[k],
                send_sem=send.at[i * 4 + k], recv_sem=recv.at[i * 4 + k],
                device_id=(x, y, 1 - c), device_id_type=MESH)

        for i in range(n):
            for k in range(4):
                copy(i, k, 1 - c).start()
        for i in range(n):
            for k in range(4):
                copy(i, k, 1 - c).wait()

    return _pcall(
        kern, name=name, in_specs=[ANY] * n, out_specs=[ANY] * n,
        out_shape=[jax.ShapeDtypeStruct((4,) + a.shape[2:], a.dtype) for a in arrs],
        scratch_shapes=[pltpu.SemaphoreType.DMA((4 * n,)), pltpu.SemaphoreType.DMA((4 * n,))],
        compiler_params=pltpu.CompilerParams(has_side_effects=True),
    )(*arrs)


def _chip_scatter(name, arrs):
    n = len(arrs)

    def kern(*refs):
        ins, outs = refs[:n], refs[n:2 * n]
        send, recv, loc = refs[2 * n], refs[2 * n + 1], refs[2 * n + 2]
        x, y, c = lax.axis_index("x"), lax.axis_index("y"), lax.axis_index("c")
        me = 2 * x + y

        def local(i):
            return pltpu.make_async_copy(ins[i].at[me], outs[i].at[me], loc.at[i])

        def copy(i, m, slot):
            px, py = x ^ ((m >> 1) & 1), y ^ (m & 1)
            return pltpu.make_async_remote_copy(
                src_ref=ins[i].at[2 * px + py], dst_ref=outs[i].at[slot],
                send_sem=send.at[i * 3 + m - 1], recv_sem=recv.at[i * 3 + m - 1],
                device_id=(px, py, c), device_id_type=MESH)

        for i in range(n):
            local(i).start()
            for m in range(1, 4):
                copy(i, m, me).start()
        for i in range(n):
            local(i).wait()
            for m in range(1, 4):
                copy(i, m, me).wait_send()
                copy(i, m, me ^ m).wait_recv()

    return _pcall(
        kern, name=name, in_specs=[ANY] * n, out_specs=[ANY] * n,
        out_shape=[jax.ShapeDtypeStruct(a.shape, a.dtype) for a in arrs],
        scratch_shapes=[pltpu.SemaphoreType.DMA((3 * n,)), pltpu.SemaphoreType.DMA((3 * n,)),
                        pltpu.SemaphoreType.DMA((n,))],
        compiler_params=pltpu.CompilerParams(has_side_effects=True),
    )(*arrs)


def _pair_exchange(name, arrs):
    n = len(arrs)

    def kern(*refs):
        ins, outs = refs[:n], refs[n:2 * n]
        send, recv, loc = refs[2 * n], refs[2 * n + 1], refs[2 * n + 2]
        x, y, c = lax.axis_index("x"), lax.axis_index("y"), lax.axis_index("c")

        def local(i):
            return pltpu.make_async_copy(ins[i], outs[i].at[c], loc.at[i])

        def copy(i, slot):
            return pltpu.make_async_remote_copy(
                src_ref=ins[i], dst_ref=outs[i].at[slot], send_sem=send.at[i], recv_sem=recv.at[i],
                device_id=(x, y, 1 - c), device_id_type=MESH)

        for i in range(n):
            local(i).start()
            copy(i, c).start()
        for i in range(n):
            local(i).wait()
            copy(i, c).wait_send()
            copy(i, 1 - c).wait_recv()

    return _pcall(
        kern, name=name, in_specs=[ANY] * n, out_specs=[ANY] * n,
        out_shape=[jax.ShapeDtypeStruct((2,) + a.shape, a.dtype) for a in arrs],
        scratch_shapes=[pltpu.SemaphoreType.DMA((n,)), pltpu.SemaphoreType.DMA((n,)),
                        pltpu.SemaphoreType.DMA((n,))],
        compiler_params=pltpu.CompilerParams(has_side_effects=True),
    )(*arrs)


def _row_tile(R):
    for t in (256, 128, 64, 32, 16, 8):
        if R % t == 0:
            return t
    return R


def _sum_slots(name, parts):
    K, R, C = parts.shape
    tr = _row_tile(R)

    def kern(p_ref, o_ref):
        acc = p_ref[0].astype(F32)
        for k in range(1, K):
            acc = acc + p_ref[k].astype(F32)
        o_ref[...] = acc

    return _pcall(
        kern, name=name, grid=(R // tr,),
        in_specs=[pl.BlockSpec((K, tr, C), lambda i: (0, i, 0))],
        out_specs=pl.BlockSpec((tr, C), lambda i: (i, 0)),
        out_shape=jax.ShapeDtypeStruct((R, C), F32),
        compiler_params=_cparams(("parallel",)),
    )(parts)


def _sum_pair(name, core, mine, theirs):
    K, _, hr, C = mine.shape
    tr = _row_tile(hr)

    def kern(c_ref, a_ref, b_ref, o_ref):
        o_ref[0] = (a_ref[0, 0].astype(F32) + b_ref[0].astype(F32)).astype(BF16)

    return _pcall(
        kern, name=name, out_shape=jax.ShapeDtypeStruct((K, hr, C), BF16),
        grid_spec=pltpu.PrefetchScalarGridSpec(
            num_scalar_prefetch=1, grid=(K, hr // tr),
            in_specs=[pl.BlockSpec((1, 1, tr, C), lambda k, r, c_ref: (k, c_ref[0], r, 0)),
                      pl.BlockSpec((1, tr, C), lambda k, r, c_ref: (k, r, 0))],
            out_specs=pl.BlockSpec((1, tr, C), lambda k, r, c_ref: (k, r, 0))),
        compiler_params=_cparams(("parallel", "parallel")),
    )(core, mine, theirs)


def _adamw(name, w, m, v, gparts):
    R, C = w.shape
    K = gparts.shape[0]
    tr = _row_tile(R)
    c1 = 1.0 - ADAM_B1 ** ADAM_STEP
    c2 = 1.0 - ADAM_B2 ** ADAM_STEP

    def kern(w_ref, m_ref, v_ref, g_ref, go_ref, d_ref, mo_ref, vo_ref):
        g = g_ref[0]
        for k in range(1, K):
            g = g + g_ref[k]
        mn = ADAM_B1 * m_ref[...] + (1.0 - ADAM_B1) * g
        vn = ADAM_B2 * v_ref[...] + (1.0 - ADAM_B2) * (g * g)
        go_ref[...] = g
        mo_ref[...] = mn
        vo_ref[...] = vn
        d_ref[...] = -ADAM_LR * ((mn / c1) / (jnp.sqrt(vn / c2) + ADAM_EPS) + ADAM_WD * w_ref[...])

    spec = pl.BlockSpec((tr, C), lambda i: (i, 0))
    return _pcall(
        kern, name=name, grid=(R // tr,),
        in_specs=[spec, spec, spec, pl.BlockSpec((K, tr, C), lambda i: (0, i, 0))],
        out_specs=[spec] * 4,
        out_shape=[jax.ShapeDtypeStruct((R, C), F32)] * 4,
        compiler_params=_cparams(("parallel",)),
    )(w, m, v, gparts)


def _round_up(a, b):
    return (a + b - 1) // b * b


def kernel(x, c, w_ada, b_ada, norm1_g, w_in, b_forget, q_norm_g, k_norm_g, w_attn_proj, conv_w, conv_b, conv_ln_g, conv_ln_b, w_conv_proj, w_out, norm2_g, w_mlp1, w_mlp2, loss_target, m_w_ada, m_b_ada, m_norm1_g, m_w_in, m_b_forget, m_q_norm_g, m_k_norm_g, m_w_attn_proj, m_conv_w, m_conv_b, m_conv_ln_g, m_conv_ln_b, m_w_conv_proj, m_w_out, m_norm2_g, m_w_mlp1, m_w_mlp2, v_w_ada, v_b_ada, v_norm1_g, v_w_in, v_b_forget, v_q_norm_g, v_k_norm_g, v_w_attn_proj, v_conv_w, v_conv_b, v_conv_ln_g, v_conv_ln_b, v_w_conv_proj, v_w_out, v_norm2_g, v_w_mlp1, v_w_mlp2):
    S, D = x.shape[1], x.shape[2]
    NH, HD = b_forget.shape[-1], q_norm_g.shape[-1]
    TAPS = conv_w.shape[1]
    DIN_S = w_in.shape[-1]
    DIN = 4 * DIN_S
    DFF_S = w_mlp1.shape[-1]
    DFF = 4 * DFF_S
    ADA_S = w_ada.shape[-1]
    DS = w_attn_proj.shape[1]
    CS = conv_w.shape[-1]
    assert NH * HD == D and DIN == 7 * D + NH and TAPS - 1 <= HALO and D % LANES == 0 and 2 * HD == LANES
    NP = _round_up(7 * D + LANES, 512)
    TQ = min(512, S)
    NQ = S // TQ
    FCOL = 7 * D // LANES

    xi, yi, ci = lax.axis_index("x"), lax.axis_index("y"), lax.axis_index("c")
    chip = 2 * xi + yi
    dev = 4 * xi + 2 * yi + ci

    x2 = x.reshape(S, D)
    tgt = loss_target.reshape(S, D)

    lane_head = jnp.arange(D, dtype=jnp.int32) // HD
    grp = (lane_head[:, None] == jnp.arange(LANES, dtype=jnp.int32)[None, :]).astype(BF16)
    grp_t = grp.T
    ch = min(256, S)
    ii = jnp.arange(ch, dtype=jnp.int32)
    tri = (ii[None, :] <= ii[:, None]).astype(BF16)
    tri_u = tri.T
    gq_t = jnp.tile(q_norm_g.reshape(1, HD), (1, NH))
    gk_t = jnp.tile(k_norm_g.reshape(1, HD), (1, NH))
    bf_pad = jnp.pad(b_forget.reshape(1, NH), ((0, 0), (0, LANES - NH)))

    (c_all,) = _ag_small("ag_c", [c.reshape(1, D)])
    c_all = c_all.reshape(8, D)
    b_part = lax.dynamic_slice(b_ada.reshape(1, -1), (0, chip * ADA_S), (1, ADA_S))
    mod_part = _ada_fwd(c_all, w_ada.reshape(D, ADA_S), b_part)
    (mod_all,) = _ag_small("ag_mod", [mod_part])
    mod_full = jnp.concatenate([mod_all[0], mod_all[2], mod_all[4], mod_all[6]], axis=1)
    mod = lax.dynamic_slice(mod_full, (dev, 0), (1, 6 * D))
    sh1, sc1, g1, sh2, sc2, g2 = [mod[:, i * D:(i + 1) * D] for i in range(6)]

    shards = [w_in.reshape(D, DIN_S), w_attn_proj.reshape(DS, D), w_conv_proj.reshape(DS, D),
              w_out.reshape(DS, D), w_mlp1.reshape(D, DFF_S), w_mlp2.reshape(DFF_S, D)]
    gw_in, gw_ap, gw_cp, gw_out, gw_m1, gw_m2 = _gather_weights(
        "ag_weights", [s.astype(BF16) for s in shards])
    (cw_all,) = _ag_small("ag_convw", [jnp.pad(conv_w.reshape(TAPS, CS), ((0, HALO - TAPS), (0, 0)))])
    w_conv = jnp.concatenate([cw_all[0], cw_all[2], cw_all[4], cw_all[6]], axis=1)

    w_in_full = jnp.concatenate([gw_in[k] for k in range(4)], axis=1)
    w_in_p = jnp.concatenate(
        [w_in_full[:, :3 * D], w_in_full[:, 3 * D + NH:], w_in_full[:, 3 * D:3 * D + NH],
         jnp.zeros((D, NP - 7 * D - NH), BF16)], axis=1)
    w_ap = gw_ap.reshape(D, D)
    w_cp = gw_cp.reshape(D, D)
    w_o = gw_out.reshape(D, D)
    w_m1 = jnp.transpose(gw_m1, (1, 0, 2)).reshape(D, DFF)
    w_m2 = gw_m2.reshape(DFF, D)

    n1g = norm1_g.reshape(1, D)
    n2g = norm2_g.reshape(1, D)
    h = _norm_mod("norm_mod1", x2, n1g, sc1, sh1, S, D)
    proj = _mm("mm_in", h, w_in_p, "nn", [F32])
    qs, kn, vb = _qk_prep(proj, gq_t, gk_t, grp, grp_t, S, D, HD)
    f_cum = _fgate_fwd(proj, FCOL, bf_pad, tri, S)
    fk_c = f_cum[:, :NH]
    fk_r = fk_c.T.reshape(NH, NQ, 1, TQ)
    fk_b = jnp.repeat(fk_c, HD, axis=1)
    o, o32, lse_b = _flash_fwd(qs, kn, vb, fk_r, S, D, HD, TQ)
    br_a = _mm("mm_attn_proj", o, w_ap, "nn", [F32])
    cb, clg, clb = conv_b.reshape(1, D), conv_ln_g.reshape(1, D), conv_ln_b.reshape(1, D)
    u1, u3 = _conv_fwd(proj, 3, 4, w_conv, cb, clg, clb, S, D, TAPS, 256)
    br_b = _mm("mm_conv_proj", u3, w_cp, "nn", [F32])
    merged = _gate_merge(proj, 5, 6, br_a, br_b, S, D)
    mo = _mm("mm_out", merged, w_o, "nn", [F32])
    x1, h2 = _resid_norm2(x2, mo, g1, n2g, sc2, sh2, S, D)

    def relu2(r):
        rp = jnp.maximum(r, 0.0)
        return r, rp * rp
    a_pre, z = _mm("mm_mlp1", h2, w_m1, "nn", [F32, BF16], epi=relu2)
    ml = _mm("mm_mlp2", z, w_m2, "nn", [F32])
    dy, dml, sq, dg2 = _loss_dy(x1, ml, tgt, g2, S, D)
    loss = lax.psum(0.5 * jnp.sum(sq) / D, ("x", "y", "c"))

    da = _mm("mm_dz", dml, w_m2, "nt", [BF16], epi=lambda r, a: (r * 2.0 * jnp.maximum(a, 0.0),),
             extras=(a_pre,))
    dw_m2 = _mm("mm_dw_mlp2", z, dml, "tn", [F32])
    dw_m1 = _mm("mm_dw_mlp1", h2, da, "tn", [F32])
    dh2 = _mm("mm_dh2", da, w_m1, "nt", [F32])
    dx1, dmo, dsh2, dsc2, dn2g, dg1 = _norm_bwd("norm2_bwd", x1, dh2, dy, n2g, sc2, S, D, extra=(mo, g1))
    dmerged = _mm("mm_dmerged", dmo, w_o, "nt", [F32])
    dw_o = _mm("mm_dw_out", merged, dmo, "tn", [F32])
    dba, dbb, dga, dgb = _gate_bwd(dmerged, proj, 5, 6, br_a, br_b, S, D)
    do = _mm("mm_do", dba, w_ap, "nt", [BF16])
    dw_ap = _mm("mm_dw_attn_proj", o, dba, "tn", [F32])
    du3 = _mm("mm_du3", dbb, w_cp, "nt", [F32])
    dw_cp = _mm("mm_dw_conv_proj", u3, dbb, "tn", [F32])
    dglu_a, dglu_b, dcw, dcb, dclg, dclb = _conv_bwd(du3, u1, proj, 3, 4, w_conv, clg, clb, S, D, TAPS, 256)

    delta_b = _delta_prep(do, o32, grp, grp_t, S, D)
    to_rows = lambda b: b[:, ::HD].T.reshape(NH, NQ, 1, TQ)
    dkn, dv, dfq_r = _flash_bwd_kv(qs, kn, vb, do, fk_b, to_rows(lse_b), to_rows(delta_b), S, D, HD, TQ)
    dqs, dfk_r = _flash_bwd_q(qs, kn, vb, do, fk_r, lse_b, delta_b, S, D, HD, TQ)
    dq, dk, sq_q, sq_k = _qk_bwd(proj, dqs, dkn, gq_t, gk_t, grp, grp_t, S, D, HD)
    to_cols = lambda r: jnp.pad(r.reshape(NH, S).T, ((0, 0), (0, LANES - NH)))
    dfk_pad, dfq_pad = to_cols(dfk_r), to_cols(dfq_r)
    df, dbf = _fgate_bwd(dfk_pad, dfq_pad, proj, FCOL, bf_pad, tri_u, NH, S)
    dproj = jnp.concatenate(
        [dq, dk, dv, dglu_a, dglu_b, dga, dgb, df, jnp.zeros((S, NP - 7 * D - LANES), BF16)], axis=1)
    dw_in_p = _mm("mm_dw_in", h, dproj, "tn", [F32])
    dh = _mm("mm_dh", dproj, w_in_p, "nt", [F32])
    gx, dsh1, dsc1, dn1g = _norm_bwd("norm1_bwd", x2, dh, dx1, n1g, sc1, S, D)

    packed = jnp.concatenate([dsh1, dsc1, dg1, dsh2, dsc2, dg2, dn1g, dcb, dclg, dclb, dn2g,
                              sq_q, sq_k, dbf], axis=1)
    small_all, dcw_all = _ag_small("ag_small_grads", [packed, dcw])
    small = _sum_slots("sum_small", small_all.reshape(8, 1, -1)).reshape(1, -1)
    dmod_sum = small[:, :6 * D]
    seg = lambda k: small[:, (6 + k) * D:(7 + k) * D]
    g_n1g, g_cb, g_clg, g_clb, g_n2g = seg(0), seg(1), seg(2), seg(3), seg(4)
    g_qn = _sum_slots("sum_qn", seg(5).reshape(NH, 1, HD))
    g_kn = _sum_slots("sum_kn", seg(6).reshape(NH, 1, HD))
    g_bf = small[:, 13 * D:13 * D + NH]
    dcw_mine = lax.dynamic_slice(dcw_all[:, :TAPS, :], (0, 0, chip * CS), (8, TAPS, CS))

    dmod_all = small_all.reshape(8, -1)[:, :6 * D]
    dmod_cols = lax.dynamic_slice(dmod_all, (0, chip * ADA_S), (8, ADA_S))
    c_t_pad = jnp.pad(c_all.T, ((0, 0), (0, LANES - 8)))
    g_wada = _ada_wgrad(c_t_pad, jnp.pad(dmod_cols, ((0, LANES - 8), (0, 0))))

    dw_in_full = jnp.concatenate(
        [dw_in_p[:, :3 * D], dw_in_p[:, 7 * D:7 * D + NH], dw_in_p[:, 3 * D:7 * D]], axis=1)
    parts = [jnp.transpose(dw_in_full.reshape(D, 4, DIN_S), (1, 0, 2)),
             dw_ap.reshape(4, DS, D), dw_cp.reshape(4, DS, D), dw_o.reshape(4, DS, D),
             jnp.transpose(dw_m1.reshape(D, 4, DFF_S), (1, 0, 2)), dw_m2.reshape(4, DFF_S, D)]
    parts = [p.astype(BF16).reshape(4, 2, p.shape[1] // 2, p.shape[2]) for p in parts]
    names = ["w_in", "w_attn_proj", "w_conv_proj", "w_out", "w_mlp1", "w_mlp2"]
    theirs = _pair_send_halves("rs_pair", parts)
    core = ci.astype(jnp.int32).reshape(1)
    chip_parts = [_sum_pair("sum_pair_" + nm, core, p, t) for nm, p, t in zip(names, parts, theirs)]
    recvd = _chip_scatter("rs_chips", chip_parts)
    sums = [_sum_slots("sum_" + nm, r) for nm, r in zip(names, recvd)]
    pairs = _pair_exchange("pair_grads", sums)

    res = {}
    big = {nm: p.reshape(1, 2 * p.shape[1], p.shape[2]) for nm, p in zip(names, pairs)}
    big_w = {"w_in": (w_in, m_w_in, v_w_in), "w_attn_proj": (w_attn_proj, m_w_attn_proj, v_w_attn_proj),
             "w_conv_proj": (w_conv_proj, m_w_conv_proj, v_w_conv_proj), "w_out": (w_out, m_w_out, v_w_out),
             "w_mlp1": (w_mlp1, m_w_mlp1, v_w_mlp1), "w_mlp2": (w_mlp2, m_w_mlp2, v_w_mlp2)}
    for nm in names:
        w_, m_, v_ = big_w[nm]
        shp = w_.shape
        r2 = lambda t: t.reshape(shp[1], shp[2])
        outs = _adamw("adamw_" + nm, r2(w_), r2(m_), r2(v_), big[nm])
        res[nm] = [t.reshape(shp) for t in outs]
    outs = _adamw("adamw_w_ada", w_ada.reshape(D, ADA_S), m_w_ada.reshape(D, ADA_S),
                  v_w_ada.reshape(D, ADA_S), g_wada.reshape(1, D, ADA_S))
    res["w_ada"] = [t.reshape(w_ada.shape) for t in outs]
    outs = _adamw("adamw_conv_w", conv_w.reshape(TAPS, CS), m_conv_w.reshape(TAPS, CS),
                  v_conv_w.reshape(TAPS, CS), dcw_mine)
    res["conv_w"] = [t.reshape(conv_w.shape) for t in outs]

    small_w = [("b_ada", b_ada, m_b_ada, v_b_ada, dmod_sum), ("norm1_g", norm1_g, m_norm1_g, v_norm1_g, g_n1g),
               ("b_forget", b_forget, m_b_forget, v_b_forget, g_bf),
               ("q_norm_g", q_norm_g, m_q_norm_g, v_q_norm_g, g_qn),
               ("k_norm_g", k_norm_g, m_k_norm_g, v_k_norm_g, g_kn),
               ("conv_b", conv_b, m_conv_b, v_conv_b, g_cb), ("conv_ln_g", conv_ln_g, m_conv_ln_g, v_conv_ln_g, g_clg),
               ("conv_ln_b", conv_ln_b, m_conv_ln_b, v_conv_ln_b, g_clb),
               ("norm2_g", norm2_g, m_norm2_g, v_norm2_g, g_n2g)]
    cat = lambda ts: jnp.concatenate([t.reshape(1, -1) for t in ts], axis=1)
    outs = _adamw("adamw_small", cat([t[1] for t in small_w]), cat([t[2] for t in small_w]),
                  cat([t[3] for t in small_w]), cat([t[4] for t in small_w]).reshape(1, 1, -1))
    off = 0
    for nm, w_, _, _, _ in small_w:
        n = w_.size
        res[nm] = [t[:, off:off + n].reshape(w_.shape) for t in outs]
        off += n

    order = ["w_ada", "b_ada", "norm1_g", "w_in", "b_forget", "q_norm_g", "k_norm_g", "w_attn_proj", "conv_w",
             "conv_b", "conv_ln_g", "conv_ln_b", "w_conv_proj", "w_out", "norm2_g", "w_mlp1", "w_mlp2"]
    return (loss, gx.reshape(x.shape), *[res[n][0] for n in order], *[res[n][1] for n in order],
            *[res[n][2] for n in order], *[res[n][3] for n in order])
```

```python
import functools

import jax
import jax.numpy as jnp
from jax import lax
from jax.experimental import pallas as pl
from jax.experimental.pallas import tpu as pltpu

F32 = jnp.float32
BF16 = jnp.bfloat16
MESH = pl.DeviceIdType.MESH
ANY = pl.BlockSpec(memory_space=pl.ANY)

NORM_EPS = 1e-6
ADAM_LR = 0.001
ADAM_B1 = 0.9
ADAM_B2 = 0.999
ADAM_EPS = 1e-08
ADAM_WD = 0.01
ADAM_STEP = 10
LANES = 128
SUBLANES = 8
HALO = 32
NEG = -1e30
VMEM_LIMIT = 56 * 1024 * 1024


def _pcall(body, **kw):
    return pl.pallas_call(body, **kw)


def _cparams(sem=None):
    if sem is None:
        return pltpu.CompilerParams(vmem_limit_bytes=VMEM_LIMIT)
    return pltpu.CompilerParams(dimension_semantics=sem, vmem_limit_bytes=VMEM_LIMIT)


def _sig(x):
    return 1.0 / (1.0 + jnp.exp(-x))


def _split3(x):
    x1 = x.astype(BF16)
    r = x - x1.astype(F32)
    x2 = r.astype(BF16)
    x3 = (r - x2.astype(F32)).astype(BF16)
    return x1, x2, x3


def _dot_rs(x, e):
    out = None
    for t in _split3(x):
        d = jnp.dot(t, e, preferred_element_type=F32)
        out = d if out is None else out + d
    return out


def _dot_ls(e, x):
    out = None
    for t in _split3(x):
        d = jnp.dot(e, t, preferred_element_type=F32)
        out = d if out is None else out + d
    return out


def _tile(n, want):
    if n <= want:
        return n
    t = want - want % LANES
    while n % t:
        t -= LANES
    assert t > 0, (n, want)
    return t


_DIMS = {"nn": ((1,), (0,)), "nt": ((1,), (1,)), "tn": ((0,), (0,))}


def _mm(name, a, b, mode, out_dtypes, epi=None, extras=(), tm=1024, tn=1024, tk=1024):
    if mode == "nn":
        (M, K), (_, N) = a.shape, b.shape
    elif mode == "nt":
        (M, K), (N, _) = a.shape, b.shape
    else:
        (K, M), (_, N) = a.shape, b.shape
    tm, tn, tk = _tile(M, tm), _tile(N, tn), _tile(K, tk)
    nk = K // tk
    ne, no = len(extras), len(out_dtypes)
    dims = (_DIMS[mode], ((), ()))

    def kern(*refs):
        a_ref, b_ref = refs[0], refs[1]
        e_refs = refs[2:2 + ne]
        o_refs = refs[2 + ne:2 + ne + no]
        d = lax.dot_general(a_ref[...], b_ref[...], dims, preferred_element_type=F32)

        def finish(r):
            outs = (r,) if epi is None else epi(r, *[e[...] for e in e_refs])
            for o_ref, o in zip(o_refs, outs):
                o_ref[...] = o.astype(o_ref.dtype)

        if nk == 1:
            finish(d)
            return
        acc = refs[-1]
        k = pl.program_id(2)

        @pl.when(k == 0)
        def _():
            acc[...] = d

        @pl.when((k > 0) & (k < nk - 1))
        def _():
            acc[...] += d

        @pl.when(k == nk - 1)
        def _():
            finish(acc[...] + d)

    if mode == "tn":
        a_spec = pl.BlockSpec((tk, tm), lambda i, j, k: (k, i))
    else:
        a_spec = pl.BlockSpec((tm, tk), lambda i, j, k: (i, k))
    if mode == "nt":
        b_spec = pl.BlockSpec((tn, tk), lambda i, j, k: (j, k))
    else:
        b_spec = pl.BlockSpec((tk, tn), lambda i, j, k: (k, j))
    mn_spec = pl.BlockSpec((tm, tn), lambda i, j, k: (i, j))
    outs = _pcall(
        kern, name=name, grid=(M // tm, N // tn, nk),
        in_specs=[a_spec, b_spec] + [mn_spec] * ne,
        out_specs=[mn_spec] * no,
        out_shape=[jax.ShapeDtypeStruct((M, N), dt) for dt in out_dtypes],
        scratch_shapes=[pltpu.VMEM((tm, tn), F32)] if nk > 1 else [],
        compiler_params=_cparams(("parallel", "parallel", "arbitrary")),
    )(a, b, *extras)
    return outs[0] if no == 1 else outs


def _rowcall(name, body, S, ts, row_ins, vec_ins, row_outs, vec_outs):
    ts = min(ts, S)
    nri, nvi, nro, nvo = len(row_ins), len(vec_ins), len(row_outs), len(vec_outs)

    def kern(*refs):
        ins = refs[:nri + nvi]
        outs = refs[nri + nvi:]
        if nvo:
            @pl.when(pl.program_id(0) == 0)
            def _():
                for r in outs[nro:]:
                    r[...] = jnp.zeros(r.shape, r.dtype)
        body(*ins, *outs)

    in_specs = [pl.BlockSpec((ts, w), functools.partial(lambda i, cb: (i, cb), cb=cb))
                for (_, w, cb) in row_ins]
    in_specs += [pl.BlockSpec(v.shape, lambda i: (0, 0)) for v in vec_ins]
    out_specs = [pl.BlockSpec((ts, w), lambda i: (i, 0)) for (w, _) in row_outs]
    out_specs += [pl.BlockSpec((r, w), lambda i: (0, 0)) for (r, w) in vec_outs]
    out_shape = [jax.ShapeDtypeStruct((S, w), dt) for (w, dt) in row_outs]
    out_shape += [jax.ShapeDtypeStruct((r, w), F32) for (r, w) in vec_outs]
    return _pcall(
        kern, name=name, grid=(S // ts,), in_specs=in_specs, out_specs=out_specs,
        out_shape=out_shape,
        compiler_params=_cparams(("arbitrary",) if nvo else ("parallel",)),
    )(*[a for (a, _, _) in row_ins], *vec_ins)


def _csum(x):
    return jnp.sum(x, axis=0, keepdims=True)


def _norm_mod(name, x, g, sc, sh, S, D):
    def body(x_ref, g_ref, sc_ref, sh_ref, h_ref):
        xv = x_ref[...]
        r = lax.rsqrt(jnp.mean(xv * xv, axis=-1, keepdims=True) + NORM_EPS)
        h_ref[...] = ((xv * r * g_ref[...]) * (1.0 + sc_ref[...]) + sh_ref[...]).astype(BF16)
    return _rowcall(name, body, S, 512, [(x, D, 0)], [g, sc, sh], [(D, BF16)], [])[0]


def _head_rstd(v, grp, grp_t, hd):
    ss = _dot_rs(v * v, grp) * (1.0 / hd)
    r = lax.rsqrt(ss + NORM_EPS)
    return _dot_rs(r, grp_t)


def _qk_prep(proj, gq, gk, grp, grp_t, S, D, hd):
    scale = hd ** -0.5

    def body(q_ref, k_ref, v_ref, gq_ref, gk_ref, g_ref, gt_ref, qs_ref, kn_ref, vb_ref):
        q = q_ref[...]
        k = k_ref[...]
        rq = _head_rstd(q, g_ref[...], gt_ref[...], hd)
        rk = _head_rstd(k, g_ref[...], gt_ref[...], hd)
        qs_ref[...] = ((q * rq * gq_ref[...]).astype(BF16).astype(F32) * scale).astype(BF16)
        kn_ref[...] = (k * rk * gk_ref[...]).astype(BF16)
        vb_ref[...] = v_ref[...].astype(BF16)

    return _rowcall("qk_prep", body, S, 256, [(proj, D, 0), (proj, D, 1), (proj, D, 2)],
                    [gq, gk, grp, grp_t], [(D, BF16)] * 3, [])


def _fgate_fwd(proj, fcol, bf_pad, tri, S):
    ch = tri.shape[0]

    def body(f_ref, b_ref, tri_ref, out_ref):
        carry = jnp.zeros((1, LANES), F32)
        for c in range(S // ch):
            z = f_ref[c * ch:(c + 1) * ch, :] + b_ref[...]
            lf = jnp.minimum(z, 0.0) - jnp.log(1.0 + jnp.exp(-jnp.abs(z)))
            out_ref[c * ch:(c + 1) * ch, :] = _dot_ls(tri_ref[...], lf) + carry
            carry = carry + _csum(lf)

    return _rowcall("fgate_fwd", body, S, S, [(proj, LANES, fcol)], [bf_pad, tri],
                    [(LANES, F32)], [])[0]


def _fgate_bwd(dfk, dfq, proj, fcol, bf_pad, tri_u, nh, S):
    ch = tri_u.shape[0]

    def body(d_ref, dq_ref, f_ref, b_ref, tri_ref, df_ref, db_ref):
        lane = lax.broadcasted_iota(jnp.int32, (ch, LANES), 1)
        carry = jnp.zeros((1, LANES), F32)
        tot = jnp.zeros((1, LANES), F32)
        for c in reversed(range(S // ch)):
            d = d_ref[c * ch:(c + 1) * ch, :] + dq_ref[c * ch:(c + 1) * ch, :]
            rc = _dot_ls(tri_ref[...], d) + carry
            carry = carry + _csum(d)
            z = f_ref[c * ch:(c + 1) * ch, :] + b_ref[...]
            df = jnp.where(lane < nh, rc * _sig(-z), 0.0)
            df_ref[c * ch:(c + 1) * ch, :] = df.astype(BF16)
            tot = tot + _csum(df)
        db_ref[...] += tot

    return _rowcall("fgate_bwd", body, S, S, [(dfk, LANES, 0), (dfq, LANES, 0), (proj, LANES, fcol)],
                    [bf_pad, tri_u], [(LANES, BF16)], [(1, LANES)])


def _keep(v, mask):
    return jnp.where(mask, v.astype(F32), 0.0).astype(BF16)


def _lane_col(blk, lane, at):
    return jnp.sum(jnp.where(lane == at, blk, 0.0), axis=-1, keepdims=True)


def _flash_fwd(qs, kn, vb, fk_r, S, D, hd, tq):
    hp, nq = D // LANES, S // tq

    def kern(q_ref, k_ref, v_ref, fk_ref, o_ref, o32_ref, lse_ref):
        qi = pl.program_id(1)
        lane = lax.broadcasted_iota(jnp.int32, (tq, LANES), 1)
        row = lax.broadcasted_iota(jnp.int32, (tq, tq), 0)
        col = lax.broadcasted_iota(jnp.int32, (tq, tq), 1)
        hms = [(lane >= j * hd) & (lane < (j + 1) * hd) for j in range(2)]
        q = q_ref[...]
        qms = [_keep(q, hm) for hm in hms]

        def step(ki, state, masked):
            off = pl.multiple_of(ki * tq, tq)
            k = k_ref[pl.ds(off, tq), :]
            v = v_ref[pl.ds(off, tq), :].astype(F32)
            new = []
            for j in range(2):
                m_old, acc = state[j]
                s = lax.dot_general(qms[j], k, (((1,), (1,)), ((), ())), preferred_element_type=F32)
                s = s - fk_ref[j, ki]
                if masked:
                    s = jnp.where(col <= row, s, NEG)
                m_new = jnp.maximum(m_old, jnp.max(s, axis=-1, keepdims=True))
                alpha = jnp.exp(m_old - m_new)
                p = jnp.exp(s - m_new)
                v1 = jnp.where(hms[j], v, 1.0).astype(BF16)
                acc = alpha * acc + jnp.dot(p.astype(BF16), v1, preferred_element_type=F32)
                new.append((m_new, acc))
            return tuple(new)

        init = tuple((jnp.full((tq, 1), NEG, F32), jnp.zeros((tq, LANES), F32)) for _ in range(2))
        state = lax.fori_loop(0, qi, lambda ki, st: step(ki, st, False), init)
        (m0, a0), (m1, a1) = step(qi, state, True)
        l0, l1 = pltpu.roll(a0, hd, 1), pltpu.roll(a1, hd, 1)
        first = lane < hd
        ov = jnp.where(first, a0 / l0, a1 / l1)
        o_ref[...] = ov.astype(BF16)
        o32_ref[...] = ov
        lse_ref[...] = jnp.where(first, m0 + jnp.log(l0), m1 + jnp.log(l1))

    qspec = pl.BlockSpec((tq, LANES), lambda h, i: (i, h))
    fullspec = pl.BlockSpec((S, LANES), lambda h, i: (0, h))
    return _pcall(
        kern, name="flash_fwd", grid=(hp, nq),
        in_specs=[qspec, fullspec, fullspec,
                  pl.BlockSpec((2, nq, 1, tq), lambda h, i: (h, 0, 0, 0))],
        out_specs=[qspec, qspec, qspec],
        out_shape=[jax.ShapeDtypeStruct((S, D), BF16), jax.ShapeDtypeStruct((S, D), F32),
                   jax.ShapeDtypeStruct((S, D), F32)],
        compiler_params=_cparams(("parallel", "arbitrary")),
    )(qs, kn, vb, fk_r)


def _flash_bwd_kv(qs, kn, vb, do, fk_b, lse_r, delta_r, S, D, hd, tq):
    hp, nq = D // LANES, S // tq

    def kern(q_ref, do_ref, k_ref, v_ref, fk_ref, lse_ref, dl_ref, dk_ref, dv_ref, dfq_ref):
        ki = pl.program_id(1)
        lane = lax.broadcasted_iota(jnp.int32, (tq, LANES), 1)
        row = lax.broadcasted_iota(jnp.int32, (tq, tq), 0)
        col = lax.broadcasted_iota(jnp.int32, (tq, tq), 1)
        hms = [(lane >= j * hd) & (lane < (j + 1) * hd) for j in range(2)]
        k = k_ref[...]
        v = v_ref[...]
        fkb = fk_ref[...]
        kms = [_keep(k, hm) for hm in hms]
        vms = [_keep(v, hm) for hm in hms]
        fks = [_lane_col(fkb, lane, j * hd) for j in range(2)]

        @pl.when(ki == 0)
        def _():
            dfq_ref[...] = jnp.zeros(dfq_ref.shape, F32)

        def step(qi, acc, masked):
            dk, dv = acc
            off = pl.multiple_of(qi * tq, tq)
            q = q_ref[pl.ds(off, tq), :]
            g = do_ref[pl.ds(off, tq), :]
            for j in range(2):
                qm = _keep(q, hms[j])
                gm = _keep(g, hms[j])
                st = lax.dot_general(kms[j], q, (((1,), (1,)), ((), ())), preferred_element_type=F32)
                st = st - fks[j]
                if masked:
                    st = jnp.where(row <= col, st, NEG)
                pt = jnp.exp(st - lse_ref[j, qi])
                dv = dv + jnp.dot(pt.astype(BF16), gm, preferred_element_type=F32)
                dpt = lax.dot_general(vms[j], g, (((1,), (1,)), ((), ())), preferred_element_type=F32)
                dst = pt * (dpt - dl_ref[j, qi])
                dk = dk + jnp.dot(dst.astype(BF16), qm, preferred_element_type=F32)
                dfq_ref[j, qi] += jnp.sum(dst, axis=0, keepdims=True)
            return dk, dv

        zero = jnp.zeros((tq, LANES), F32)
        acc = step(ki, (zero, zero), True)
        dk, dv = lax.fori_loop(ki + 1, nq, lambda qi, a: step(qi, a, False), acc)
        dk_ref[...] = dk.astype(BF16)
        dv_ref[...] = dv.astype(BF16)

    kspec = pl.BlockSpec((tq, LANES), lambda h, i: (i, h))
    fullspec = pl.BlockSpec((S, LANES), lambda h, i: (0, h))
    rowspec = pl.BlockSpec((2, nq, 1, tq), lambda h, i: (h, 0, 0, 0))
    return _pcall(
        kern, name="flash_bwd_kv", grid=(hp, nq),
        in_specs=[fullspec, fullspec, kspec, kspec, kspec, rowspec, rowspec],
        out_specs=[kspec, kspec, rowspec],
        out_shape=[jax.ShapeDtypeStruct((S, D), BF16), jax.ShapeDtypeStruct((S, D), BF16),
                   jax.ShapeDtypeStruct((2 * hp, nq, 1, tq), F32)],
        compiler_params=_cparams(("parallel", "arbitrary")),
    )(qs, do, kn, vb, fk_b, lse_r, delta_r)


def _flash_bwd_q(qs, kn, vb, do, fk_r, lse_b, delta_b, S, D, hd, tq):
    hp, nq = D // LANES, S // tq

    def kern(q_ref, do_ref, k_ref, v_ref, fk_ref, lse_ref, dl_ref, dq_ref, dfk_ref):
        qi = pl.program_id(1)
        lane = lax.broadcasted_iota(jnp.int32, (tq, LANES), 1)
        row = lax.broadcasted_iota(jnp.int32, (tq, tq), 0)
        col = lax.broadcasted_iota(jnp.int32, (tq, tq), 1)
        hms = [(lane >= j * hd) & (lane < (j + 1) * hd) for j in range(2)]
        q = q_ref[...]
        g = do_ref[...]
        lse_b_ = lse_ref[...]
        dl_b_ = dl_ref[...]
        qms = [_keep(q, hm) for hm in hms]
        gms = [_keep(g, hm) for hm in hms]
        lses = [_lane_col(lse_b_, lane, j * hd) for j in range(2)]
        dls = [_lane_col(dl_b_, lane, j * hd) for j in range(2)]

        @pl.when(qi == 0)
        def _():
            dfk_ref[...] = jnp.zeros(dfk_ref.shape, F32)

        def step(ki, dq, masked):
            off = pl.multiple_of(ki * tq, tq)
            k = k_ref[pl.ds(off, tq), :]
            v = v_ref[pl.ds(off, tq), :]
            for j in range(2):
                s = lax.dot_general(qms[j], k, (((1,), (1,)), ((), ())), preferred_element_type=F32)
                s = s - fk_ref[j, ki]
                if masked:
                    s = jnp.where(col <= row, s, NEG)
                p = jnp.exp(s - lses[j])
                dp = lax.dot_general(gms[j], v, (((1,), (1,)), ((), ())), preferred_element_type=F32)
                ds = p * (dp - dls[j])
                km = _keep(k, hms[j])
                dq = dq + jnp.dot(ds.astype(BF16), km, preferred_element_type=F32)
                dfk_ref[j, ki] -= jnp.sum(ds, axis=0, keepdims=True)
            return dq

        dq = lax.fori_loop(0, qi, lambda ki, a: step(ki, a, False), jnp.zeros((tq, LANES), F32))
        dq_ref[...] = step(qi, dq, True).astype(BF16)

    qspec = pl.BlockSpec((tq, LANES), lambda h, i: (i, h))
    fullspec = pl.BlockSpec((S, LANES), lambda h, i: (0, h))
    rowspec = pl.BlockSpec((2, nq, 1, tq), lambda h, i: (h, 0, 0, 0))
    return _pcall(
        kern, name="flash_bwd_q", grid=(hp, nq),
        in_specs=[qspec, qspec, fullspec, fullspec, rowspec, qspec, qspec],
        out_specs=[qspec, rowspec],
        out_shape=[jax.ShapeDtypeStruct((S, D), BF16), jax.ShapeDtypeStruct((2 * hp, nq, 1, tq), F32)],
        compiler_params=_cparams(("parallel", "arbitrary")),
    )(qs, do, kn, vb, fk_r, lse_b, delta_b)


def _delta_prep(do, o, grp, grp_t, S, D):
    def body(g_ref, o_ref, e_ref, et_ref, out_ref):
        prod = g_ref[...].astype(F32) * o_ref[...]
        out_ref[...] = _dot_rs(_dot_rs(prod, e_ref[...]), et_ref[...])
    return _rowcall("delta_prep", body, S, 256, [(do, D, 0), (o, D, 0)], [grp, grp_t],
                    [(D, F32)], [])[0]


def _qk_bwd(proj, dqs, dkn, gq, gk, grp, grp_t, S, D, hd):
    scale = hd ** -0.5

    def one(x, dn, gain, e, et):
        r = _head_rstd(x, e, et, hd)
        xh = x * r
        t = dn * gain
        mean = _dot_rs(_dot_rs(t * xh, e), et) * (1.0 / hd)
        return r * (t - xh * mean), _csum(dn * xh)

    def body(q_ref, k_ref, dq_ref, dk_ref, gq_ref, gk_ref, e_ref, et_ref,
             oq_ref, ok_ref, sq_ref, sk_ref):
        e, et = e_ref[...], et_ref[...]
        dq, sq = one(q_ref[...], dq_ref[...].astype(F32) * scale, gq_ref[...], e, et)
        dk, sk = one(k_ref[...], dk_ref[...].astype(F32), gk_ref[...], e, et)
        oq_ref[...] = dq.astype(BF16)
        ok_ref[...] = dk.astype(BF16)
        sq_ref[...] += sq
        sk_ref[...] += sk

    return _rowcall("qk_bwd", body, S, 256,
                    [(proj, D, 0), (proj, D, 1), (dqs, D, 0), (dkn, D, 0)],
                    [gq, gk, grp, grp_t], [(D, BF16)] * 2, [(1, D)] * 2)


def _shift_copies(buf, sh, ts):
    for b in range(1, SUBLANES):
        sh[b - 1] = buf[b:b + ts + HALO - SUBLANES, :]


def _rows_from(buf, sh, o, ts):
    a, b = divmod(o, SUBLANES)
    if b == 0:
        return buf[o:o + ts, :]
    return sh[b - 1, SUBLANES * a:SUBLANES * a + ts, :]


def _conv_fwd(proj, acol, bcol, w_pad, cb, lg, lb, S, C, taps, ts):
    ts = min(ts, S)

    def kern(a_ref, b_ref, w_ref, cb_ref, lg_ref, lb_ref, u1_ref, u3_ref, ubuf, ush):
        @pl.when(pl.program_id(0) == 0)
        def _():
            ubuf[0:HALO, :] = jnp.zeros((HALO, C), F32)

        ubuf[HALO:HALO + ts, :] = a_ref[...] * _sig(b_ref[...])
        _shift_copies(ubuf, ush, ts)
        acc = jnp.zeros((ts, C), F32) + cb_ref[...]
        for k in range(taps):
            o = HALO - (taps - 1) + k
            acc = acc + w_ref[k:k + 1, :] * _rows_from(ubuf, ush, o, ts)
        u1_ref[...] = acc
        mu = jnp.mean(acc, axis=-1, keepdims=True)
        xc = acc - mu
        rstd = lax.rsqrt(jnp.mean(xc * xc, axis=-1, keepdims=True) + NORM_EPS)
        u2 = xc * rstd * lg_ref[...] + lb_ref[...]
        u3_ref[...] = (u2 * _sig(u2)).astype(BF16)
        ubuf[0:HALO, :] = ubuf[ts:ts + HALO, :]

    vec = lambda a: pl.BlockSpec(a.shape, lambda i: (0, 0))
    return _pcall(
        kern, name="conv_fwd", grid=(S // ts,),
        in_specs=[pl.BlockSpec((ts, C), lambda i: (i, acol)), pl.BlockSpec((ts, C), lambda i: (i, bcol)),
                  vec(w_pad), vec(cb), vec(lg), vec(lb)],
        out_specs=[pl.BlockSpec((ts, C), lambda i: (i, 0))] * 2,
        out_shape=[jax.ShapeDtypeStruct((S, C), F32), jax.ShapeDtypeStruct((S, C), BF16)],
        scratch_shapes=[pltpu.VMEM((HALO + ts, C), F32),
                        pltpu.VMEM((SUBLANES - 1, HALO + ts - SUBLANES, C), F32)],
        compiler_params=_cparams(("arbitrary",)),
    )(proj, proj, w_pad, cb, lg, lb)


def _conv_bwd(du3, u1, proj, acol, bcol, w_pad, lg, lb, S, C, taps, ts):
    ts = min(ts, S)
    nt = S // ts
    hb = ts // HALO

    def ln_bwd(g, u, lgv, lbv):
        mu = jnp.mean(u, axis=-1, keepdims=True)
        xc = u - mu
        rstd = lax.rsqrt(jnp.mean(xc * xc, axis=-1, keepdims=True) + NORM_EPS)
        xh = xc * rstd
        u2 = xh * lgv + lbv
        s = _sig(u2)
        du2 = g * (s + u2 * s * (1.0 - s))
        dxh = du2 * lgv
        du1 = rstd * (dxh - jnp.mean(dxh, axis=-1, keepdims=True)
                      - xh * jnp.mean(dxh * xh, axis=-1, keepdims=True))
        return du1, du2, xh

    def kern(g_ref, u_ref, a_ref, b_ref, gn_ref, un_ref, ap_ref, bp_ref, w_ref, lg_ref, lb_ref,
             da_ref, db_ref, dw_ref, dcb_ref, dlg_ref, dlb_ref, dbuf, ubuf, dsh, ush):
        i = pl.program_id(0)

        @pl.when(i == 0)
        def _():
            dw_ref[...] = jnp.zeros(dw_ref.shape, F32)
            dcb_ref[...] = jnp.zeros(dcb_ref.shape, F32)
            dlg_ref[...] = jnp.zeros(dlg_ref.shape, F32)
            dlb_ref[...] = jnp.zeros(dlb_ref.shape, F32)

        lgv, lbv = lg_ref[...], lb_ref[...]
        du1, du2, xh = ln_bwd(g_ref[...], u_ref[...], lgv, lbv)
        dbuf[0:ts, :] = du1
        du1n, _, _ = ln_bwd(gn_ref[...], un_ref[...], lgv, lbv)
        dbuf[ts:ts + HALO, :] = jnp.where(i < nt - 1, du1n, 0.0)
        a = a_ref[...]
        sb = _sig(b_ref[...])
        ubuf[HALO:HALO + ts, :] = a * sb
        ubuf[0:HALO, :] = jnp.where(i > 0, ap_ref[...] * _sig(bp_ref[...]), 0.0)
        dcb_ref[...] += _csum(du1)
        dlg_ref[...] += _csum(du2 * xh)
        dlb_ref[...] += _csum(du2)
        _shift_copies(dbuf, dsh, ts)
        _shift_copies(ubuf, ush, ts)
        du0 = jnp.zeros((ts, C), F32)
        for k in range(taps):
            du0 = du0 + w_ref[k:k + 1, :] * _rows_from(dbuf, dsh, taps - 1 - k, ts)
            dw_ref[k:k + 1, :] += _csum(du1 * _rows_from(ubuf, ush, HALO - (taps - 1) + k, ts))
        da_ref[...] = (du0 * sb).astype(BF16)
        db_ref[...] = (du0 * a * sb * (1.0 - sb)).astype(BF16)

    vec = lambda a: pl.BlockSpec(a.shape, lambda i: (0, 0))
    tile = lambda cb: pl.BlockSpec((ts, C), functools.partial(lambda i, cb: (i, cb), cb=cb))
    nxt = lambda cb: pl.BlockSpec(
        (HALO, C), functools.partial(lambda i, cb: (jnp.minimum((i + 1) * hb, nt * hb - 1), cb), cb=cb))
    prv = lambda cb: pl.BlockSpec(
        (HALO, C), functools.partial(lambda i, cb: (jnp.maximum(i * hb - 1, 0), cb), cb=cb))
    return _pcall(
        kern, name="conv_bwd", grid=(nt,),
        in_specs=[tile(0), tile(0), tile(acol), tile(bcol), nxt(0), nxt(0), prv(acol), prv(bcol),
                  vec(w_pad), vec(lg), vec(lb)],
        out_specs=[pl.BlockSpec((ts, C), lambda i: (i, 0))] * 2
        + [pl.BlockSpec(w_pad.shape, lambda i: (0, 0))] + [pl.BlockSpec((1, C), lambda i: (0, 0))] * 3,
        out_shape=[jax.ShapeDtypeStruct((S, C), BF16)] * 2
        + [jax.ShapeDtypeStruct(w_pad.shape, F32)] + [jax.ShapeDtypeStruct((1, C), F32)] * 3,
        scratch_shapes=[pltpu.VMEM((ts + HALO, C), F32), pltpu.VMEM((HALO + ts, C), F32)]
        + [pltpu.VMEM((SUBLANES - 1, HALO + ts - SUBLANES, C), F32)] * 2,
        compiler_params=_cparams(("arbitrary",)),
    )(du3, u1, proj, proj, du3, u1, proj, proj, w_pad, lg, lb)


def _gate_merge(proj, gacol, gbcol, ba, bb, S, D):
    def body(ga_ref, gb_ref, a_ref, b_ref, out_ref):
        out_ref[...] = (_sig(ga_ref[...]) * a_ref[...] + _sig(gb_ref[...]) * b_ref[...]).astype(BF16)
    return _rowcall("gate_merge", body, S, 512,
                    [(proj, D, gacol), (proj, D, gbcol), (ba, D, 0), (bb, D, 0)], [], [(D, BF16)], [])[0]


def _gate_bwd(dm, proj, gacol, gbcol, ba, bb, S, D):
    def body(dm_ref, ga_ref, gb_ref, a_ref, b_ref, da_ref, db_ref, dga_ref, dgb_ref):
        dmv = dm_ref[...]
        sa, sb = _sig(ga_ref[...]), _sig(gb_ref[...])
        da_ref[...] = (dmv * sa).astype(BF16)
        db_ref[...] = (dmv * sb).astype(BF16)
        dga_ref[...] = (dmv * a_ref[...] * sa * (1.0 - sa)).astype(BF16)
        dgb_ref[...] = (dmv * b_ref[...] * sb * (1.0 - sb)).astype(BF16)
    return _rowcall("gate_bwd", body, S, 512,
                    [(dm, D, 0), (proj, D, gacol), (proj, D, gbcol), (ba, D, 0), (bb, D, 0)], [],
                    [(D, BF16)] * 4, [])


def _resid_norm2(x, mo, g1, g, sc, sh, S, D):
    def body(x_ref, mo_ref, g1_ref, g_ref, sc_ref, sh_ref, x1_ref, h_ref):
        x1 = x_ref[...] + g1_ref[...] * mo_ref[...]
        x1_ref[...] = x1
        r = lax.rsqrt(jnp.mean(x1 * x1, axis=-1, keepdims=True) + NORM_EPS)
        h_ref[...] = ((x1 * r * g_ref[...]) * (1.0 + sc_ref[...]) + sh_ref[...]).astype(BF16)
    return _rowcall("resid_norm2", body, S, 512, [(x, D, 0), (mo, D, 0)], [g1, g, sc, sh],
                    [(D, F32), (D, BF16)], [])


def _loss_dy(x1, ml, tgt, g2, S, D):
    def body(x1_ref, ml_ref, t_ref, g2_ref, dy_ref, dml_ref, sq_ref, dg2_ref):
        mlv = ml_ref[...]
        diff = x1_ref[...] + g2_ref[...] * mlv - t_ref[...]
        dy = diff * (1.0 / D)
        dy_ref[...] = dy
        dml_ref[...] = (dy * g2_ref[...]).astype(BF16)
        sq_ref[...] += _csum(diff * diff)
        dg2_ref[...] += _csum(dy * mlv)
    return _rowcall("loss_dy", body, S, 512, [(x1, D, 0), (ml, D, 0), (tgt, D, 0)], [g2],
                    [(D, F32), (D, BF16)], [(1, D), (1, D)])


def _norm_bwd(name, xin, dh, dres, g, sc, S, D, extra=None):
    def body(*refs):
        if extra is None:
            x_ref, dh_ref, dr_ref, g_ref, sc_ref, dx_ref, dsh_ref, dsc_ref, dg_ref = refs
        else:
            (x_ref, dh_ref, dr_ref, mo_ref, g_ref, sc_ref, g1_ref,
             dx_ref, dmo_ref, dsh_ref, dsc_ref, dg_ref, dg1_ref) = refs
        xv, dhv, gv = x_ref[...], dh_ref[...], g_ref[...]
        r = lax.rsqrt(jnp.mean(xv * xv, axis=-1, keepdims=True) + NORM_EPS)
        xh = xv * r
        dsh_ref[...] += _csum(dhv)
        dsc_ref[...] += _csum(dhv * xh * gv)
        dxg = dhv * (1.0 + sc_ref[...])
        dg_ref[...] += _csum(dxg * xh)
        dxh = dxg * gv
        dx = dr_ref[...] + r * (dxh - xh * jnp.mean(dxh * xh, axis=-1, keepdims=True))
        dx_ref[...] = dx
        if extra is not None:
            dmo_ref[...] = (dx * g1_ref[...]).astype(BF16)
            dg1_ref[...] += _csum(dx * mo_ref[...])

    rows = [(xin, D, 0), (dh, D, 0), (dres, D, 0)]
    vecs = [g, sc]
    if extra is None:
        return _rowcall(name, body, S, 512, rows, vecs, [(D, F32)], [(1, D)] * 3)
    return _rowcall(name, body, S, 512, rows + [(extra[0], D, 0)], vecs + [extra[1]],
                    [(D, F32), (D, BF16)], [(1, D)] * 4)


def _ada_fwd(c_all, w, b_part):
    B, D = c_all.shape
    N = w.shape[1]
    tn = min(512, N)

    def kern(c_ref, w_ref, b_ref, o_ref):
        cv = c_ref[...]
        ca = cv * _sig(cv)
        o_ref[...] = jnp.dot(ca, w_ref[...], precision=lax.Precision.HIGHEST,
                             preferred_element_type=F32) + b_ref[...]

    return _pcall(
        kern, name="ada_fwd", grid=(N // tn,),
        in_specs=[pl.BlockSpec((B, D), lambda j: (0, 0)), pl.BlockSpec((D, tn), lambda j: (0, j)),
                  pl.BlockSpec((1, tn), lambda j: (0, j))],
        out_specs=pl.BlockSpec((B, tn), lambda j: (0, j)),
        out_shape=jax.ShapeDtypeStruct((B, N), F32),
        compiler_params=_cparams(("parallel",)),
    )(c_all, w, b_part)


def _ada_wgrad(c_t_pad, dmod_pad):
    D = c_t_pad.shape[0]
    N = dmod_pad.shape[1]
    tn = min(512, N)

    def kern(c_ref, d_ref, o_ref):
        cv = c_ref[...]
        ca = cv * _sig(cv)
        o_ref[...] = jnp.dot(ca, d_ref[...], precision=lax.Precision.HIGHEST,
                             preferred_element_type=F32)

    return _pcall(
        kern, name="ada_wgrad", grid=(N // tn,),
        in_specs=[pl.BlockSpec((D, LANES), lambda j: (0, 0)), pl.BlockSpec((LANES, tn), lambda j: (0, j))],
        out_specs=pl.BlockSpec((D, tn), lambda j: (0, j)),
        out_shape=jax.ShapeDtypeStruct((D, N), F32),
        compiler_params=_cparams(("parallel",)),
    )(c_t_pad, dmod_pad)


def _ag_small(name, arrs):
    n = len(arrs)

    def kern(*refs):
        ins, outs = refs[:n], refs[n:2 * n]
        send, recv = refs[2 * n], refs[2 * n + 1]
        x, y, c = lax.axis_index("x"), lax.axis_index("y"), lax.axis_index("c")
        me = 4 * x + 2 * y + c

        def copy(i, m, slot):
            peer = (x ^ ((m >> 2) & 1), y ^ ((m >> 1) & 1), c ^ (m & 1))
            return pltpu.make_async_remote_copy(
                src_ref=ins[i], dst_ref=outs[i].at[slot],
                send_sem=send.at[i * 7 + m - 1], recv_sem=recv.at[i * 7 + m - 1],
                device_id=peer, device_id_type=MESH)

        for i in range(n):
            outs[i][me] = ins[i][...]
            for m in range(1, 8):
                copy(i, m, me).start()
        for i in range(n):
            for m in range(1, 8):
                copy(i, m, me).wait_send()
                copy(i, m, me ^ m).wait_recv()

    vm = pl.BlockSpec(memory_space=pltpu.VMEM)
    return _pcall(
        kern, name=name, in_specs=[vm] * n, out_specs=[vm] * n,
        out_shape=[jax.ShapeDtypeStruct((8,) + a.shape, a.dtype) for a in arrs],
        scratch_shapes=[pltpu.SemaphoreType.DMA((7 * n,)), pltpu.SemaphoreType.DMA((7 * n,))],
        compiler_params=pltpu.CompilerParams(has_side_effects=True),
    )(*arrs)


def _gather_weights(name, arrs):
    n = len(arrs)

    def kern(*refs):
        ins, outs = refs[:n], refs[n:2 * n]
        s1, r1, s2, r2, loc = refs[2 * n:2 * n + 5]
        x, y, c = lax.axis_index("x"), lax.axis_index("y"), lax.axis_index("c")
        me = 2 * x + y

        def half(i, hc):
            hr = ins[i].shape[0] // 2
            return pl.ds(hc * hr, hr)

        def own(i):
            return pltpu.make_async_remote_copy(
                src_ref=ins[i], dst_ref=outs[i].at[me], send_sem=loc.at[i], recv_sem=loc.at[n + i],
                device_id=(x, y, 1 - c), device_id_type=MESH)

        def fetch(i, m, slot):
            px, py = x ^ ((m >> 1) & 1), y ^ (m & 1)
            return pltpu.make_async_remote_copy(
                src_ref=ins[i].at[half(i, c)], dst_ref=outs[i].at[slot, half(i, c)],
                send_sem=s1.at[i * 3 + m - 1], recv_sem=r1.at[i * 3 + m - 1],
                device_id=(px, py, c), device_id_type=MESH)

        def passed(i, m, hc):
            return pltpu.make_async_remote_copy(
                src_ref=outs[i].at[me ^ m, half(i, hc)], dst_ref=outs[i].at[me ^ m, half(i, hc)],
                send_sem=s2.at[i * 3 + m - 1], recv_sem=r2.at[i * 3 + m - 1],
                device_id=(x, y, 1 - c), device_id_type=MESH)

        for i in range(n):
            for m in range(1, 4):
                fetch(i, m, me).start()
        for i in range(n):
            own(i).start()
        for i in range(n):
            for m in range(1, 4):
                fetch(i, m, me ^ m).wait_recv()
                passed(i, m, c).start()
        for i in range(n):
            own(i).wait()
            for m in range(1, 4):
                fetch(i, m, me).wait_send()
                passed(i, m, c).wait_send()
                passed(i, m, 1 - c).wait_recv()

    return _pcall(
        kern, name=name, in_specs=[ANY] * n, out_specs=[ANY] * n,
        out_shape=[jax.ShapeDtypeStruct((4,) + a.shape, a.dtype) for a in arrs],
        scratch_shapes=[pltpu.SemaphoreType.DMA((3 * n,))] * 4 + [pltpu.SemaphoreType.DMA((2 * n,))],
        compiler_params=pltpu.CompilerParams(has_side_effects=True),
    )(*arrs)


def _pair_send_halves(name, arrs):
    n = len(arrs)

    def kern(*refs):
        ins, outs = refs[:n], refs[n:2 * n]
        send, recv = refs[2 * n], refs[2 * n + 1]
        x, y, c = lax.axis_index("x"), lax.axis_index("y"), lax.axis_index("c")

        def copy(i, k, hc):
            return pltpu.make_async_remote_copy(
                src_ref=ins[i].at[k, hc], dst_ref=outs[i].at[k],
                send_sem=send.at[i * 4 + k], recv_sem=recv.at[i * 4 + k],
                device_id=(x, y, 1 - c), device_id_type=MESH)

        for i in range(n):
            for k in range(4):
                copy(i, k, 1 - c).start()
        for i in range(n):
            for k in range(4):
                copy(i, k, 1 - c).wait()

    return _pcall(
        kern, name=name, in_specs=[ANY] * n, out_specs=[ANY] * n,
        out_shape=[jax.ShapeDtypeStruct((4,) + a.shape[2:], a.dtype) for a in arrs],
        scratch_shapes=[pltpu.SemaphoreType.DMA((4 * n,)), pltpu.SemaphoreType.DMA((4 * n,))],
        compiler_params=pltpu.CompilerParams(has_side_effects=True),
    )(*arrs)


def _chip_scatter(name, arrs):
    n = len(arrs)

    def kern(*refs):
        ins, outs = refs[:n], refs[n:2 * n]
        send, recv = refs[2 * n], refs[2 * n + 1]
        x, y, c = lax.axis_index("x"), lax.axis_index("y"), lax.axis_index("c")
        me = 2 * x + y

        def copy(i, m, slot):
            px, py = x ^ ((m >> 1) & 1), y ^ (m & 1)
            return pltpu.make_async_remote_copy(
                src_ref=ins[i].at[2 * px + py], dst_ref=outs[i].at[slot],
                send_sem=send.at[i * 3 + m - 1], recv_sem=recv.at[i * 3 + m - 1],
                device_id=(px, py, c), device_id_type=MESH)

        for i in range(n):
            for m in range(1, 4):
                copy(i, m, me).start()
        for i in range(n):
            for m in range(1, 4):
                copy(i, m, me).wait_send()
                copy(i, m, me ^ m).wait_recv()

    return _pcall(
        kern, name=name, in_specs=[ANY] * n, out_specs=[ANY] * n,
        out_shape=[jax.ShapeDtypeStruct(a.shape, a.dtype) for a in arrs],
        scratch_shapes=[pltpu.SemaphoreType.DMA((3 * n,)), pltpu.SemaphoreType.DMA((3 * n,))],
        compiler_params=pltpu.CompilerParams(has_side_effects=True),
    )(*arrs)


def _pair_swap(name, arrs):
    n = len(arrs)

    def kern(*refs):
        ins, outs = refs[:n], refs[n:2 * n]
        send, recv = refs[2 * n], refs[2 * n + 1]
        x, y, c = lax.axis_index("x"), lax.axis_index("y"), lax.axis_index("c")

        def copy(i):
            return pltpu.make_async_remote_copy(
                src_ref=ins[i], dst_ref=outs[i], send_sem=send.at[i], recv_sem=recv.at[i],
                device_id=(x, y, 1 - c), device_id_type=MESH)

        for i in range(n):
            copy(i).start()
        for i in range(n):
            copy(i).wait()

    return _pcall(
        kern, name=name, in_specs=[ANY] * n, out_specs=[ANY] * n,
        out_shape=[jax.ShapeDtypeStruct(a.shape, a.dtype) for a in arrs],
        scratch_shapes=[pltpu.SemaphoreType.DMA((n,)), pltpu.SemaphoreType.DMA((n,))],
        compiler_params=pltpu.CompilerParams(has_side_effects=True),
    )(*arrs)


def _row_tile(R):
    for t in (256, 128, 64, 32, 16, 8):
        if R % t == 0:
            return t
    return R


def _sum_slots(name, parts):
    K, R, C = parts.shape
    tr = _row_tile(R)

    def kern(p_ref, o_ref):
        acc = p_ref[0].astype(F32)
        for k in range(1, K):
            acc = acc + p_ref[k].astype(F32)
        o_ref[...] = acc

    return _pcall(
        kern, name=name, grid=(R // tr,),
        in_specs=[pl.BlockSpec((K, tr, C), lambda i: (0, i, 0))],
        out_specs=pl.BlockSpec((tr, C), lambda i: (i, 0)),
        out_shape=jax.ShapeDtypeStruct((R, C), F32),
        compiler_params=_cparams(("parallel",)),
    )(parts)


def _sum_pair(name, core, mine, theirs):
    K, _, hr, C = mine.shape
    tr = _row_tile(hr)

    def kern(c_ref, a_ref, b_ref, o_ref):
        o_ref[0] = (a_ref[0, 0].astype(F32) + b_ref[0].astype(F32)).astype(BF16)

    return _pcall(
        kern, name=name, out_shape=jax.ShapeDtypeStruct((K, hr, C), BF16),
        grid_spec=pltpu.PrefetchScalarGridSpec(
            num_scalar_prefetch=1, grid=(K, hr // tr),
            in_specs=[pl.BlockSpec((1, 1, tr, C), lambda k, r, c_ref: (k, c_ref[0], r, 0)),
                      pl.BlockSpec((1, tr, C), lambda k, r, c_ref: (k, r, 0))],
            out_specs=pl.BlockSpec((1, tr, C), lambda k, r, c_ref: (k, r, 0))),
        compiler_params=_cparams(("parallel", "parallel")),
    )(core, mine, theirs)


def _sum_chips(name, chip, own, recv):
    K, hr, C = own.shape
    tr = _row_tile(hr)

    def kern(chip_ref, own_ref, *rest):
        r_refs, o_ref = rest[:K], rest[K]
        me = chip_ref[0]
        mine = own_ref[0].astype(F32)
        acc = None
        for k in range(K):
            t = jnp.where(me == k, mine, r_refs[k][0].astype(F32))
            acc = t if acc is None else acc + t
        o_ref[...] = acc

    def other(k):
        return pl.BlockSpec((1, tr, C), lambda r, s: (jnp.where(s[0] == k, (k + 1) % K, k), r, 0))

    return _pcall(
        kern, name=name, out_shape=jax.ShapeDtypeStruct((hr, C), F32),
        grid_spec=pltpu.PrefetchScalarGridSpec(
            num_scalar_prefetch=1, grid=(hr // tr,),
            in_specs=[pl.BlockSpec((1, tr, C), lambda r, s: (s[0], r, 0))] + [other(k) for k in range(K)],
            out_specs=pl.BlockSpec((tr, C), lambda r, s: (r, 0))),
        compiler_params=_cparams(("parallel",)),
    )(chip, own, *([recv] * K))


def _adam_update(w, m, v, g):
    c1 = 1.0 - ADAM_B1 ** ADAM_STEP
    c2 = 1.0 - ADAM_B2 ** ADAM_STEP
    mn = ADAM_B1 * m + (1.0 - ADAM_B1) * g
    vn = ADAM_B2 * v + (1.0 - ADAM_B2) * (g * g)
    return -ADAM_LR * ((mn / c1) / (jnp.sqrt(vn / c2) + ADAM_EPS) + ADAM_WD * w), mn, vn


def _adamw_halves(name, core, w, m, v, mine, theirs):
    R, C = w.shape
    hr = mine.shape[0]
    tr = _row_tile(hr)
    nbh = hr // tr

    def kern(c_ref, w_ref, m_ref, v_ref, a_ref, b_ref, go_ref, d_ref, mo_ref, vo_ref):
        g = jnp.where(pl.program_id(0) // nbh == c_ref[0], a_ref[...], b_ref[...])
        d, mn, vn = _adam_update(w_ref[...], m_ref[...], v_ref[...], g)
        go_ref[...] = g
        d_ref[...] = d
        mo_ref[...] = mn
        vo_ref[...] = vn

    spec = pl.BlockSpec((tr, C), lambda i, s: (i, 0))
    hspec = pl.BlockSpec((tr, C), lambda i, s: (i % nbh, 0))
    return _pcall(
        kern, name=name, out_shape=[jax.ShapeDtypeStruct((R, C), F32)] * 4,
        grid_spec=pltpu.PrefetchScalarGridSpec(
            num_scalar_prefetch=1, grid=(R // tr,),
            in_specs=[spec, spec, spec, hspec, hspec], out_specs=[spec] * 4),
        compiler_params=_cparams(("parallel",)),
    )(core, w, m, v, mine, theirs)


def _adamw(name, w, m, v, gparts):
    R, C = w.shape
    K = gparts.shape[0]
    tr = _row_tile(R)

    def kern(w_ref, m_ref, v_ref, g_ref, go_ref, d_ref, mo_ref, vo_ref):
        g = g_ref[0]
        for k in range(1, K):
            g = g + g_ref[k]
        d, mn, vn = _adam_update(w_ref[...], m_ref[...], v_ref[...], g)
        go_ref[...] = g
        d_ref[...] = d
        mo_ref[...] = mn
        vo_ref[...] = vn

    spec = pl.BlockSpec((tr, C), lambda i: (i, 0))
    return _pcall(
        kern, name=name, grid=(R // tr,),
        in_specs=[spec, spec, spec, pl.BlockSpec((K, tr, C), lambda i: (0, i, 0))],
        out_specs=[spec] * 4,
        out_shape=[jax.ShapeDtypeStruct((R, C), F32)] * 4,
        compiler_params=_cparams(("parallel",)),
    )(w, m, v, gparts)


def _round_up(a, b):
    return (a + b - 1) // b * b


def kernel(x, c, w_ada, b_ada, norm1_g, w_in, b_forget, q_norm_g, k_norm_g, w_attn_proj, conv_w, conv_b, conv_ln_g, conv_ln_b, w_conv_proj, w_out, norm2_g, w_mlp1, w_mlp2, loss_target, m_w_ada, m_b_ada, m_norm1_g, m_w_in, m_b_forget, m_q_norm_g, m_k_norm_g, m_w_attn_proj, m_conv_w, m_conv_b, m_conv_ln_g, m_conv_ln_b, m_w_conv_proj, m_w_out, m_norm2_g, m_w_mlp1, m_w_mlp2, v_w_ada, v_b_ada, v_norm1_g, v_w_in, v_b_forget, v_q_norm_g, v_k_norm_g, v_w_attn_proj, v_conv_w, v_conv_b, v_conv_ln_g, v_conv_ln_b, v_w_conv_proj, v_w_out, v_norm2_g, v_w_mlp1, v_w_mlp2):
    S, D = x.shape[1], x.shape[2]
    NH, HD = b_forget.shape[-1], q_norm_g.shape[-1]
    TAPS = conv_w.shape[1]
    DIN_S = w_in.shape[-1]
    DIN = 4 * DIN_S
    DFF_S = w_mlp1.shape[-1]
    DFF = 4 * DFF_S
    ADA_S = w_ada.shape[-1]
    DS = w_attn_proj.shape[1]
    CS = conv_w.shape[-1]
    assert NH * HD == D and DIN == 7 * D + NH and TAPS - 1 <= HALO and D % LANES == 0 and 2 * HD == LANES
    NP = _round_up(7 * D + LANES, 512)
    TQ = min(512, S)
    NQ = S // TQ
    FCOL = 7 * D // LANES

    xi, yi, ci = lax.axis_index("x"), lax.axis_index("y"), lax.axis_index("c")
    chip = 2 * xi + yi
    dev = 4 * xi + 2 * yi + ci

    x2 = x.reshape(S, D)
    tgt = loss_target.reshape(S, D)

    lane_head = jnp.arange(D, dtype=jnp.int32) // HD
    grp = (lane_head[:, None] == jnp.arange(LANES, dtype=jnp.int32)[None, :]).astype(BF16)
    grp_t = grp.T
    ch = min(256, S)
    ii = jnp.arange(ch, dtype=jnp.int32)
    tri = (ii[None, :] <= ii[:, None]).astype(BF16)
    tri_u = tri.T
    gq_t = jnp.tile(q_norm_g.reshape(1, HD), (1, NH))
    gk_t = jnp.tile(k_norm_g.reshape(1, HD), (1, NH))
    bf_pad = jnp.pad(b_forget.reshape(1, NH), ((0, 0), (0, LANES - NH)))

    (c_all,) = _ag_small("ag_c", [c.reshape(1, D)])
    c_all = c_all.reshape(8, D)
    b_part = lax.dynamic_slice(b_ada.reshape(1, -1), (0, chip * ADA_S), (1, ADA_S))
    mod_part = _ada_fwd(c_all, w_ada.reshape(D, ADA_S), b_part)
    (mod_all,) = _ag_small("ag_mod", [mod_part])
    mod_full = jnp.concatenate([mod_all[0], mod_all[2], mod_all[4], mod_all[6]], axis=1)
    mod = lax.dynamic_slice(mod_full, (dev, 0), (1, 6 * D))
    sh1, sc1, g1, sh2, sc2, g2 = [mod[:, i * D:(i + 1) * D] for i in range(6)]

    shards = [w_in.reshape(D, DIN_S), w_attn_proj.reshape(DS, D), w_conv_proj.reshape(DS, D),
              w_out.reshape(DS, D), w_mlp1.reshape(D, DFF_S), w_mlp2.reshape(DFF_S, D)]
    gw_in, gw_ap, gw_cp, gw_out, gw_m1, gw_m2 = _gather_weights(
        "ag_weights", [s.astype(BF16) for s in shards])
    (cw_all,) = _ag_small("ag_convw", [jnp.pad(conv_w.reshape(TAPS, CS), ((0, HALO - TAPS), (0, 0)))])
    w_conv = jnp.concatenate([cw_all[0], cw_all[2], cw_all[4], cw_all[6]], axis=1)

    w_in_full = jnp.concatenate([gw_in[k] for k in range(4)], axis=1)
    w_in_p = jnp.concatenate(
        [w_in_full[:, :3 * D], w_in_full[:, 3 * D + NH:], w_in_full[:, 3 * D:3 * D + NH],
         jnp.zeros((D, NP - 7 * D - NH), BF16)], axis=1)
    w_ap = gw_ap.reshape(D, D)
    w_cp = gw_cp.reshape(D, D)
    w_o = gw_out.reshape(D, D)
    w_m1 = jnp.transpose(gw_m1, (1, 0, 2)).reshape(D, DFF)
    w_m2 = gw_m2.reshape(DFF, D)

    n1g = norm1_g.reshape(1, D)
    n2g = norm2_g.reshape(1, D)
    h = _norm_mod("norm_mod1", x2, n1g, sc1, sh1, S, D)
    proj = _mm("mm_in", h, w_in_p, "nn", [F32])
    qs, kn, vb = _qk_prep(proj, gq_t, gk_t, grp, grp_t, S, D, HD)
    f_cum = _fgate_fwd(proj, FCOL, bf_pad, tri, S)
    fk_c = f_cum[:, :NH]
    fk_r = fk_c.T.reshape(NH, NQ, 1, TQ)
    fk_b = jnp.repeat(fk_c, HD, axis=1)
    o, o32, lse_b = _flash_fwd(qs, kn, vb, fk_r, S, D, HD, TQ)
    br_a = _mm("mm_attn_proj", o, w_ap, "nn", [F32])
    cb, clg, clb = conv_b.reshape(1, D), conv_ln_g.reshape(1, D), conv_ln_b.reshape(1, D)
    u1, u3 = _conv_fwd(proj, 3, 4, w_conv, cb, clg, clb, S, D, TAPS, 256)
    br_b = _mm("mm_conv_proj", u3, w_cp, "nn", [F32])
    merged = _gate_merge(proj, 5, 6, br_a, br_b, S, D)
    mo = _mm("mm_out", merged, w_o, "nn", [F32])
    x1, h2 = _resid_norm2(x2, mo, g1, n2g, sc2, sh2, S, D)

    def relu2(r):
        rp = jnp.maximum(r, 0.0)
        return r, rp * rp
    a_pre, z = _mm("mm_mlp1", h2, w_m1, "nn", [F32, BF16], epi=relu2)
    ml = _mm("mm_mlp2", z, w_m2, "nn", [F32])
    dy, dml, sq, dg2 = _loss_dy(x1, ml, tgt, g2, S, D)
    loss = lax.psum(0.5 * jnp.sum(sq) / D, ("x", "y", "c"))

    da = _mm("mm_dz", dml, w_m2, "nt", [BF16], epi=lambda r, a: (r * 2.0 * jnp.maximum(a, 0.0),),
             extras=(a_pre,))
    dw_m2 = _mm("mm_dw_mlp2", z, dml, "tn", [F32])
    dw_m1 = _mm("mm_dw_mlp1", h2, da, "tn", [F32])
    dh2 = _mm("mm_dh2", da, w_m1, "nt", [F32])
    dx1, dmo, dsh2, dsc2, dn2g, dg1 = _norm_bwd("norm2_bwd", x1, dh2, dy, n2g, sc2, S, D, extra=(mo, g1))
    dmerged = _mm("mm_dmerged", dmo, w_o, "nt", [F32])
    dw_o = _mm("mm_dw_out", merged, dmo, "tn", [F32])
    dba, dbb, dga, dgb = _gate_bwd(dmerged, proj, 5, 6, br_a, br_b, S, D)
    do = _mm("mm_do", dba, w_ap, "nt", [BF16])
    dw_ap = _mm("mm_dw_attn_proj", o, dba, "tn", [F32])
    du3 = _mm("mm_du3", dbb, w_cp, "nt", [F32])
    dw_cp = _mm("mm_dw_conv_proj", u3, dbb, "tn", [F32])
    dglu_a, dglu_b, dcw, dcb, dclg, dclb = _conv_bwd(du3, u1, proj, 3, 4, w_conv, clg, clb, S, D, TAPS, 256)

    delta_b = _delta_prep(do, o32, grp, grp_t, S, D)
    to_rows = lambda b: b[:, ::HD].T.reshape(NH, NQ, 1, TQ)
    dkn, dv, dfq_r = _flash_bwd_kv(qs, kn, vb, do, fk_b, to_rows(lse_b), to_rows(delta_b), S, D, HD, TQ)
    dqs, dfk_r = _flash_bwd_q(qs, kn, vb, do, fk_r, lse_b, delta_b, S, D, HD, TQ)
    dq, dk, sq_q, sq_k = _qk_bwd(proj, dqs, dkn, gq_t, gk_t, grp, grp_t, S, D, HD)
    to_cols = lambda r: jnp.pad(r.reshape(NH, S).T, ((0, 0), (0, LANES - NH)))
    dfk_pad, dfq_pad = to_cols(dfk_r), to_cols(dfq_r)
    df, dbf = _fgate_bwd(dfk_pad, dfq_pad, proj, FCOL, bf_pad, tri_u, NH, S)
    dproj = jnp.concatenate(
        [dq, dk, dv, dglu_a, dglu_b, dga, dgb, df, jnp.zeros((S, NP - 7 * D - LANES), BF16)], axis=1)
    dw_in_p = _mm("mm_dw_in", h, dproj, "tn", [F32])
    dh = _mm("mm_dh", dproj, w_in_p, "nt", [F32])
    gx, dsh1, dsc1, dn1g = _norm_bwd("norm1_bwd", x2, dh, dx1, n1g, sc1, S, D)

    packed = jnp.concatenate([dsh1, dsc1, dg1, dsh2, dsc2, dg2, dn1g, dcb, dclg, dclb, dn2g,
                              sq_q, sq_k, dbf], axis=1)
    small_all, dcw_all = _ag_small("ag_small_grads", [packed, dcw])
    small = _sum_slots("sum_small", small_all.reshape(8, 1, -1)).reshape(1, -1)
    dmod_sum = small[:, :6 * D]
    seg = lambda k: small[:, (6 + k) * D:(7 + k) * D]
    g_n1g, g_cb, g_clg, g_clb, g_n2g = seg(0), seg(1), seg(2), seg(3), seg(4)
    g_qn = _sum_slots("sum_qn", seg(5).reshape(NH, 1, HD))
    g_kn = _sum_slots("sum_kn", seg(6).reshape(NH, 1, HD))
    g_bf = small[:, 13 * D:13 * D + NH]
    dcw_mine = lax.dynamic_slice(dcw_all[:, :TAPS, :], (0, 0, chip * CS), (8, TAPS, CS))

    dmod_all = small_all.reshape(8, -1)[:, :6 * D]
    dmod_cols = lax.dynamic_slice(dmod_all, (0, chip * ADA_S), (8, ADA_S))
    c_t_pad = jnp.pad(c_all.T, ((0, 0), (0, LANES - 8)))
    g_wada = _ada_wgrad(c_t_pad, jnp.pad(dmod_cols, ((0, LANES - 8), (0, 0))))

    dw_in_full = jnp.concatenate(
        [dw_in_p[:, :3 * D], dw_in_p[:, 7 * D:7 * D + NH], dw_in_p[:, 3 * D:7 * D]], axis=1)
    parts = [jnp.transpose(dw_in_full.reshape(D, 4, DIN_S), (1, 0, 2)),
             dw_ap.reshape(4, DS, D), dw_cp.reshape(4, DS, D), dw_o.reshape(4, DS, D),
             jnp.transpose(dw_m1.reshape(D, 4, DFF_S), (1, 0, 2)), dw_m2.reshape(4, DFF_S, D)]
    parts = [p.astype(BF16).reshape(4, 2, p.shape[1] // 2, p.shape[2]) for p in parts]
    names = ["w_in", "w_attn_proj", "w_conv_proj", "w_out", "w_mlp1", "w_mlp2"]
    theirs = _pair_send_halves("rs_pair", parts)
    core = ci.astype(jnp.int32).reshape(1)
    chip_parts = [_sum_pair("sum_pair_" + nm, core, p, t) for nm, p, t in zip(names, parts, theirs)]
    recvd = _chip_scatter("rs_chips", chip_parts)
    chip1 = chip.astype(jnp.int32).reshape(1)
    sums = [_sum_chips("sum_" + nm, chip1, p, r) for nm, p, r in zip(names, chip_parts, recvd)]
    others = _pair_swap("pair_grads", sums)

    res = {}
    big = {nm: (a, b) for nm, a, b in zip(names, sums, others)}
    big_w = {"w_in": (w_in, m_w_in, v_w_in), "w_attn_proj": (w_attn_proj, m_w_attn_proj, v_w_attn_proj),
             "w_conv_proj": (w_conv_proj, m_w_conv_proj, v_w_conv_proj), "w_out": (w_out, m_w_out, v_w_out),
             "w_mlp1": (w_mlp1, m_w_mlp1, v_w_mlp1), "w_mlp2": (w_mlp2, m_w_mlp2, v_w_mlp2)}
    for nm in names:
        w_, m_, v_ = big_w[nm]
        shp = w_.shape
        r2 = lambda t: t.reshape(shp[1], shp[2])
        outs = _adamw_halves("adamw_" + nm, core, r2(w_), r2(m_), r2(v_), *big[nm])
        res[nm] = [t.reshape(shp) for t in outs]
    outs = _adamw("adamw_w_ada", w_ada.reshape(D, ADA_S), m_w_ada.reshape(D, ADA_S),
                  v_w_ada.reshape(D, ADA_S), g_wada.reshape(1, D, ADA_S))
    res["w_ada"] = [t.reshape(w_ada.shape) for t in outs]
    outs = _adamw("adamw_conv_w", conv_w.reshape(TAPS, CS), m_conv_w.reshape(TAPS, CS),
                  v_conv_w.reshape(TAPS, CS), dcw_mine)
    res["conv_w"] = [t.reshape(conv_w.shape) for t in outs]

    small_w = [("b_ada", b_ada, m_b_ada, v_b_ada, dmod_sum), ("norm1_g", norm1_g, m_norm1_g, v_norm1_g, g_n1g),
               ("b_forget", b_forget, m_b_forget, v_b_forget, g_bf),
               ("q_norm_g", q_norm_g, m_q_norm_g, v_q_norm_g, g_qn),
               ("k_norm_g", k_norm_g, m_k_norm_g, v_k_norm_g, g_kn),
               ("conv_b", conv_b, m_conv_b, v_conv_b, g_cb), ("conv_ln_g", conv_ln_g, m_conv_ln_g, v_conv_ln_g, g_clg),
               ("conv_ln_b", conv_ln_b, m_conv_ln_b, v_conv_ln_b, g_clb),
               ("norm2_g", norm2_g, m_norm2_g, v_norm2_g, g_n2g)]
    cat = lambda ts: jnp.concatenate([t.reshape(1, -1) for t in ts], axis=1)
    outs = _adamw("adamw_small", cat([t[1] for t in small_w]), cat([t[2] for t in small_w]),
                  cat([t[3] for t in small_w]), cat([t[4] for t in small_w]).reshape(1, 1, -1))
    off = 0
    for nm, w_, _, _, _ in small_w:
        n = w_.size
        res[nm] = [t[:, off:off + n].reshape(w_.shape) for t in outs]
        off += n

    order = ["w_ada", "b_ada", "norm1_g", "w_in", "b_forget", "q_norm_g", "k_norm_g", "w_attn_proj", "conv_w",
             "conv_b", "conv_ln_g", "conv_ln_b", "w_conv_proj", "w_out", "norm2_g", "w_mlp1", "w_mlp2"]
    return (loss, gx.reshape(x.shape), *[res[n][0] for n in order], *[res[n][1] for n in order],
            *[res[n][2] for n in order], *[res[n][3] for n in order])
```

```python
import functools

import jax
import jax.numpy as jnp
from jax import lax
from jax.experimental import pallas as pl
from jax.experimental.pallas import tpu as pltpu

F32 = jnp.float32
BF16 = jnp.bfloat16
MESH = pl.DeviceIdType.MESH
ANY = pl.BlockSpec(memory_space=pl.ANY)

NORM_EPS = 1e-6
ADAM_LR = 0.001
ADAM_B1 = 0.9
ADAM_B2 = 0.999
ADAM_EPS = 1e-08
ADAM_WD = 0.01
ADAM_STEP = 10
LANES = 128
SUBLANES = 8
HALO = 32
CONV_ROWS = 32
CONV_TAPS = 4
NEG = -1e30
VMEM_LIMIT = 56 * 1024 * 1024


def _pcall(body, **kw):
    return pl.pallas_call(body, **kw)


def _cparams(sem=None):
    if sem is None:
        return pltpu.CompilerParams(vmem_limit_bytes=VMEM_LIMIT)
    return pltpu.CompilerParams(dimension_semantics=sem, vmem_limit_bytes=VMEM_LIMIT)


def _sig(x):
    return 1.0 / (1.0 + jnp.exp(-x))


def _split3(x):
    x1 = x.astype(BF16)
    r = x - x1.astype(F32)
    x2 = r.astype(BF16)
    x3 = (r - x2.astype(F32)).astype(BF16)
    return x1, x2, x3


def _dot_rs(x, e):
    out = None
    for t in _split3(x):
        d = jnp.dot(t, e, preferred_element_type=F32)
        out = d if out is None else out + d
    return out


def _dot_ls(e, x):
    out = None
    for t in _split3(x):
        d = jnp.dot(e, t, preferred_element_type=F32)
        out = d if out is None else out + d
    return out


def _tile(n, want):
    if n <= want:
        return n
    t = want - want % LANES
    while n % t:
        t -= LANES
    assert t > 0, (n, want)
    return t


_DIMS = {"nn": ((1,), (0,)), "nt": ((1,), (1,)), "tn": ((0,), (0,))}


def _mm(name, a, b, mode, out_dtypes, epi=None, extras=(), tm=1024, tn=1024, tk=1024):
    if mode == "nn":
        (M, K), (_, N) = a.shape, b.shape
    elif mode == "nt":
        (M, K), (N, _) = a.shape, b.shape
    else:
        (K, M), (_, N) = a.shape, b.shape
    tm, tn, tk = _tile(M, tm), _tile(N, tn), _tile(K, tk)
    nk = K // tk
    ne, no = len(extras), len(out_dtypes)
    dims = (_DIMS[mode], ((), ()))

    def kern(*refs):
        a_ref, b_ref = refs[0], refs[1]
        e_refs = refs[2:2 + ne]
        o_refs = refs[2 + ne:2 + ne + no]
        d = lax.dot_general(a_ref[...], b_ref[...], dims, preferred_element_type=F32)

        def finish(r):
            outs = (r,) if epi is None else epi(r, *[e[...] for e in e_refs])
            for o_ref, o in zip(o_refs, outs):
                o_ref[...] = o.astype(o_ref.dtype)

        if nk == 1:
            finish(d)
            return
        acc = refs[-1]
        k = pl.program_id(2)

        @pl.when(k == 0)
        def _():
            acc[...] = d

        @pl.when((k > 0) & (k < nk - 1))
        def _():
            acc[...] += d

        @pl.when(k == nk - 1)
        def _():
            finish(acc[...] + d)

    if mode == "tn":
        a_spec = pl.BlockSpec((tk, tm), lambda i, j, k: (k, i))
    else:
        a_spec = pl.BlockSpec((tm, tk), lambda i, j, k: (i, k))
    if mode == "nt":
        b_spec = pl.BlockSpec((tn, tk), lambda i, j, k: (j, k))
    else:
        b_spec = pl.BlockSpec((tk, tn), lambda i, j, k: (k, j))
    mn_spec = pl.BlockSpec((tm, tn), lambda i, j, k: (i, j))
    outs = _pcall(
        kern, name=name, grid=(M // tm, N // tn, nk),
        in_specs=[a_spec, b_spec] + [mn_spec] * ne,
        out_specs=[mn_spec] * no,
        out_shape=[jax.ShapeDtypeStruct((M, N), dt) for dt in out_dtypes],
        scratch_shapes=[pltpu.VMEM((tm, tn), F32)] if nk > 1 else [],
        compiler_params=_cparams(("parallel", "parallel", "arbitrary")),
    )(a, b, *extras)
    return outs[0] if no == 1 else outs


def _rowcall(name, body, S, ts, row_ins, vec_ins, row_outs, vec_outs):
    ts = min(ts, S)
    nri, nvi, nro, nvo = len(row_ins), len(vec_ins), len(row_outs), len(vec_outs)

    def kern(*refs):
        ins = refs[:nri + nvi]
        outs = refs[nri + nvi:]
        if nvo:
            @pl.when(pl.program_id(0) == 0)
            def _():
                for r in outs[nro:]:
                    r[...] = jnp.zeros(r.shape, r.dtype)
        body(*ins, *outs)

    in_specs = [pl.BlockSpec((ts, w), functools.partial(lambda i, cb: (i, cb), cb=cb))
                for (_, w, cb) in row_ins]
    in_specs += [pl.BlockSpec(v.shape, lambda i: (0, 0)) for v in vec_ins]
    out_specs = [pl.BlockSpec((ts, w), lambda i: (i, 0)) for (w, _) in row_outs]
    out_specs += [pl.BlockSpec((r, w), lambda i: (0, 0)) for (r, w) in vec_outs]
    out_shape = [jax.ShapeDtypeStruct((S, w), dt) for (w, dt) in row_outs]
    out_shape += [jax.ShapeDtypeStruct((r, w), F32) for (r, w) in vec_outs]
    return _pcall(
        kern, name=name, grid=(S // ts,), in_specs=in_specs, out_specs=out_specs,
        out_shape=out_shape,
        compiler_params=_cparams(("arbitrary",) if nvo else ("parallel",)),
    )(*[a for (a, _, _) in row_ins], *vec_ins)


def _csum(x):
    return jnp.sum(x, axis=0, keepdims=True)


def _norm_mod(name, x, g, sc, sh, S, D):
    def body(x_ref, g_ref, sc_ref, sh_ref, h_ref):
        xv = x_ref[...]
        r = lax.rsqrt(jnp.mean(xv * xv, axis=-1, keepdims=True) + NORM_EPS)
        h_ref[...] = ((xv * r * g_ref[...]) * (1.0 + sc_ref[...]) + sh_ref[...]).astype(BF16)
    return _rowcall(name, body, S, 512, [(x, D, 0)], [g, sc, sh], [(D, BF16)], [])[0]


def _head_rstd(v, grp, grp_t, hd):
    ss = _dot_rs(v * v, grp) * (1.0 / hd)
    r = lax.rsqrt(ss + NORM_EPS)
    return _dot_rs(r, grp_t)


def _qk_prep(proj, gq, gk, grp, grp_t, S, D, hd):
    scale = hd ** -0.5

    def body(q_ref, k_ref, v_ref, gq_ref, gk_ref, g_ref, gt_ref, qs_ref, kn_ref, vb_ref):
        q = q_ref[...]
        k = k_ref[...]
        rq = _head_rstd(q, g_ref[...], gt_ref[...], hd)
        rk = _head_rstd(k, g_ref[...], gt_ref[...], hd)
        qs_ref[...] = ((q * rq * gq_ref[...]).astype(BF16).astype(F32) * scale).astype(BF16)
        kn_ref[...] = (k * rk * gk_ref[...]).astype(BF16)
        vb_ref[...] = v_ref[...].astype(BF16)

    return _rowcall("qk_prep", body, S, 256, [(proj, D, 0), (proj, D, 1), (proj, D, 2)],
                    [gq, gk, grp, grp_t], [(D, BF16)] * 3, [])


def _fgate_fwd(proj, fcol, bf_pad, tri, S):
    ch = tri.shape[0]

    def body(f_ref, b_ref, tri_ref, out_ref):
        carry = jnp.zeros((1, LANES), F32)
        for c in range(S // ch):
            z = f_ref[c * ch:(c + 1) * ch, :] + b_ref[...]
            lf = jnp.minimum(z, 0.0) - jnp.log(1.0 + jnp.exp(-jnp.abs(z)))
            out_ref[c * ch:(c + 1) * ch, :] = _dot_ls(tri_ref[...], lf) + carry
            carry = carry + _csum(lf)

    return _rowcall("fgate_fwd", body, S, S, [(proj, LANES, fcol)], [bf_pad, tri],
                    [(LANES, F32)], [])[0]


def _fgate_bwd(dfk, dfq, proj, fcol, bf_pad, tri_u, nh, S):
    ch = tri_u.shape[0]

    def body(d_ref, dq_ref, f_ref, b_ref, tri_ref, df_ref, db_ref):
        lane = lax.broadcasted_iota(jnp.int32, (ch, LANES), 1)
        carry = jnp.zeros((1, LANES), F32)
        tot = jnp.zeros((1, LANES), F32)
        for c in reversed(range(S // ch)):
            d = d_ref[c * ch:(c + 1) * ch, :] + dq_ref[c * ch:(c + 1) * ch, :]
            rc = _dot_ls(tri_ref[...], d) + carry
            carry = carry + _csum(d)
            z = f_ref[c * ch:(c + 1) * ch, :] + b_ref[...]
            df = jnp.where(lane < nh, rc * _sig(-z), 0.0)
            df_ref[c * ch:(c + 1) * ch, :] = df.astype(BF16)
            tot = tot + _csum(df)
        db_ref[...] += tot

    return _rowcall("fgate_bwd", body, S, S, [(dfk, LANES, 0), (dfq, LANES, 0), (proj, LANES, fcol)],
                    [bf_pad, tri_u], [(LANES, BF16)], [(1, LANES)])


def _keep(v, mask):
    return jnp.where(mask, v.astype(F32), 0.0).astype(BF16)


def _lane_col(blk, lane, at):
    return jnp.sum(jnp.where(lane == at, blk, 0.0), axis=-1, keepdims=True)


def _flash_fwd(qs, kn, vb, fk_r, S, D, hd, tq, ride=None):
    hp, nq = D // LANES, S // tq
    r_in, r_ispec, r_ospec, r_oshape, r_scratch, r_hook = _ride(ride, 4, 3)

    def kern(*refs):
        q_ref, k_ref, v_ref, fk_ref = refs[:4]
        o_ref, o32_ref, lse_ref = refs[4 + len(r_in):7 + len(r_in)]
        hi, qi = pl.program_id(0), pl.program_id(1)
        before, after = r_hook(refs, (hi == 0) & (qi == 0), (hi == hp // 2) & (qi == 0),
                               (hi == hp - 1) & (qi == nq - 1))
        before()
        lane = lax.broadcasted_iota(jnp.int32, (tq, LANES), 1)
        row = lax.broadcasted_iota(jnp.int32, (tq, tq), 0)
        col = lax.broadcasted_iota(jnp.int32, (tq, tq), 1)
        hms = [(lane >= j * hd) & (lane < (j + 1) * hd) for j in range(2)]
        q = q_ref[...]
        qms = [_keep(q, hm) for hm in hms]

        def step(ki, state, masked):
            off = pl.multiple_of(ki * tq, tq)
            k = k_ref[pl.ds(off, tq), :]
            v = v_ref[pl.ds(off, tq), :].astype(F32)
            new = []
            for j in range(2):
                m_old, acc = state[j]
                s = lax.dot_general(qms[j], k, (((1,), (1,)), ((), ())), preferred_element_type=F32)
                s = s - fk_ref[j, ki]
                if masked:
                    s = jnp.where(col <= row, s, NEG)
                m_new = jnp.maximum(m_old, jnp.max(s, axis=-1, keepdims=True))
                alpha = jnp.exp(m_old - m_new)
                p = jnp.exp(s - m_new)
                v1 = jnp.where(hms[j], v, 1.0).astype(BF16)
                acc = alpha * acc + jnp.dot(p.astype(BF16), v1, preferred_element_type=F32)
                new.append((m_new, acc))
            return tuple(new)

        init = tuple((jnp.full((tq, 1), NEG, F32), jnp.zeros((tq, LANES), F32)) for _ in range(2))
        state = lax.fori_loop(0, qi, lambda ki, st: step(ki, st, False), init)
        (m0, a0), (m1, a1) = step(qi, state, True)
        l0, l1 = pltpu.roll(a0, hd, 1), pltpu.roll(a1, hd, 1)
        first = lane < hd
        ov = jnp.where(first, a0 / l0, a1 / l1)
        o_ref[...] = ov.astype(BF16)
        o32_ref[...] = ov
        lse_ref[...] = jnp.where(first, m0 + jnp.log(l0), m1 + jnp.log(l1))
        after()

    qspec = pl.BlockSpec((tq, LANES), lambda h, i: (i, h))
    fullspec = pl.BlockSpec((S, LANES), lambda h, i: (0, h))
    return _pcall(
        kern, name="flash_fwd", grid=(hp, nq),
        in_specs=[qspec, fullspec, fullspec,
                  pl.BlockSpec((2, nq, 1, tq), lambda h, i: (h, 0, 0, 0))] + r_ispec,
        out_specs=[qspec, qspec, qspec] + r_ospec,
        out_shape=[jax.ShapeDtypeStruct((S, D), BF16), jax.ShapeDtypeStruct((S, D), F32),
                   jax.ShapeDtypeStruct((S, D), F32)] + r_oshape,
        scratch_shapes=r_scratch,
        compiler_params=_cparams(("arbitrary", "arbitrary")),
    )(qs, kn, vb, fk_r, *r_in)


def _flash_bwd(qs, kn, vb, do, fk_b, lse_r, delta_r, S, D, hd, tq, ride=None):
    hp, nq = D // LANES, S // tq
    r_in, r_ispec, r_ospec, r_oshape, r_scratch, r_hook = _ride(ride, 7, 5)

    def kern(*refs):
        q_ref, do_ref, k_ref, v_ref, fk_ref, lse_ref, dl_ref = refs[:7]
        dk_ref, dv_ref, dq_ref, dfq_ref, dfk_ref = refs[7 + len(r_in):12 + len(r_in)]
        hi, ki = pl.program_id(0), pl.program_id(1)
        before, after = r_hook(refs, (hi == 0) & (ki == 0), (hi == hp // 2) & (ki == 0),
                               (hi == hp - 1) & (ki == nq - 1))
        before()
        lane = lax.broadcasted_iota(jnp.int32, (tq, LANES), 1)
        row = lax.broadcasted_iota(jnp.int32, (tq, tq), 0)
        col = lax.broadcasted_iota(jnp.int32, (tq, tq), 1)
        hms = [(lane >= j * hd) & (lane < (j + 1) * hd) for j in range(2)]
        k = k_ref[...]
        v = v_ref[...]
        fkb = fk_ref[...]
        kms = [_keep(k, hm) for hm in hms]
        vms = [_keep(v, hm) for hm in hms]
        fks = [_lane_col(fkb, lane, j * hd) for j in range(2)]

        @pl.when(ki == 0)
        def _():
            dfq_ref[...] = jnp.zeros(dfq_ref.shape, F32)
            dq_ref[...] = jnp.zeros(dq_ref.shape, F32)

        def step(qi, acc, masked):
            dk, dv, dfs = acc
            off = pl.multiple_of(qi * tq, tq)
            q = q_ref[pl.ds(off, tq), :]
            g = do_ref[pl.ds(off, tq), :]
            dq = None
            new_dfs = []
            for j in range(2):
                qm = _keep(q, hms[j])
                gm = _keep(g, hms[j])
                st = lax.dot_general(kms[j], q, (((1,), (1,)), ((), ())), preferred_element_type=F32)
                st = st - fks[j]
                if masked:
                    st = jnp.where(row <= col, st, NEG)
                pt = jnp.exp(st - lse_ref[j, qi])
                dv = dv + jnp.dot(pt.astype(BF16), gm, preferred_element_type=F32)
                dpt = lax.dot_general(vms[j], g, (((1,), (1,)), ((), ())), preferred_element_type=F32)
                dst = pt * (dpt - dl_ref[j, qi])
                dsb = dst.astype(BF16)
                dk = dk + jnp.dot(dsb, qm, preferred_element_type=F32)
                t = lax.dot_general(dsb, kms[j], (((0,), (0,)), ((), ())), preferred_element_type=F32)
                dq = t if dq is None else dq + t
                dfq_ref[j, qi] += jnp.sum(dst, axis=0, keepdims=True)
                new_dfs.append(dfs[j] - jnp.sum(dst, axis=1, keepdims=True))
            dq_ref[pl.ds(off, tq), :] += dq
            return dk, dv, tuple(new_dfs)

        zero = jnp.zeros((tq, LANES), F32)
        zcol = jnp.zeros((tq, 1), F32)
        acc = step(ki, (zero, zero, (zcol, zcol)), True)
        dk, dv, dfs = lax.fori_loop(ki + 1, nq, lambda qi, a: step(qi, a, False), acc)
        dk_ref[...] = dk.astype(BF16)
        dv_ref[...] = dv.astype(BF16)
        dfk_ref[...] = jnp.where(lane < hd, dfs[0], dfs[1])
        after()

    kspec = pl.BlockSpec((tq, LANES), lambda h, i: (i, h))
    fullspec = pl.BlockSpec((S, LANES), lambda h, i: (0, h))
    rowspec = pl.BlockSpec((2, nq, 1, tq), lambda h, i: (h, 0, 0, 0))
    return _pcall(
        kern, name="flash_bwd", grid=(hp, nq),
        in_specs=[fullspec, fullspec, kspec, kspec, kspec, rowspec, rowspec] + r_ispec,
        out_specs=[kspec, kspec, fullspec, rowspec, kspec] + r_ospec,
        out_shape=[jax.ShapeDtypeStruct((S, D), BF16), jax.ShapeDtypeStruct((S, D), BF16),
                   jax.ShapeDtypeStruct((S, D), F32), jax.ShapeDtypeStruct((2 * hp, nq, 1, tq), F32),
                   jax.ShapeDtypeStruct((S, D), F32)] + r_oshape,
        scratch_shapes=r_scratch,
        compiler_params=_cparams(("arbitrary", "arbitrary")),
    )(qs, do, kn, vb, fk_b, lse_r, delta_r, *r_in)


def _delta_prep(do, o, grp, grp_t, S, D):
    def body(g_ref, o_ref, e_ref, et_ref, out_ref):
        prod = g_ref[...].astype(F32) * o_ref[...]
        out_ref[...] = _dot_rs(_dot_rs(prod, e_ref[...]), et_ref[...])
    return _rowcall("delta_prep", body, S, 256, [(do, D, 0), (o, D, 0)], [grp, grp_t],
                    [(D, F32)], [])[0]


def _qk_bwd(proj, dqs, dkn, gq, gk, grp, grp_t, S, D, hd):
    scale = hd ** -0.5

    def one(x, dn, gain, e, et):
        r = _head_rstd(x, e, et, hd)
        xh = x * r
        t = dn * gain
        mean = _dot_rs(_dot_rs(t * xh, e), et) * (1.0 / hd)
        return r * (t - xh * mean), _csum(dn * xh)

    def body(q_ref, k_ref, dq_ref, dk_ref, gq_ref, gk_ref, e_ref, et_ref,
             oq_ref, ok_ref, sq_ref, sk_ref):
        e, et = e_ref[...], et_ref[...]
        dq, sq = one(q_ref[...], dq_ref[...].astype(F32) * scale, gq_ref[...], e, et)
        dk, sk = one(k_ref[...], dk_ref[...].astype(F32), gk_ref[...], e, et)
        oq_ref[...] = dq.astype(BF16)
        ok_ref[...] = dk.astype(BF16)
        sq_ref[...] += sq
        sk_ref[...] += sk

    return _rowcall("qk_bwd", body, S, 256,
                    [(proj, D, 0), (proj, D, 1), (dqs, D, 0), (dkn, D, 0)],
                    [gq, gk, grp, grp_t], [(D, BF16)] * 2, [(1, D)] * 2)


def _shift_copies(buf, sh, ts):
    for b in range(1, SUBLANES):
        sh[b - 1] = buf[b:b + ts + HALO - SUBLANES, :]


def _rows_from(buf, sh, o, ts):
    a, b = divmod(o, SUBLANES)
    if b == 0:
        return buf[o:o + ts, :]
    return sh[b - 1, SUBLANES * a:SUBLANES * a + ts, :]


def _conv_fwd(proj, acol, bcol, w_pad, cb, lg, lb, S, C, taps, ts):
    ts = min(ts, S)

    def kern(a_ref, b_ref, w_ref, cb_ref, lg_ref, lb_ref, u1_ref, u3_ref, ubuf, ush):
        @pl.when(pl.program_id(0) == 0)
        def _():
            ubuf[0:HALO, :] = jnp.zeros((HALO, C), F32)

        ubuf[HALO:HALO + ts, :] = a_ref[...] * _sig(b_ref[...])
        _shift_copies(ubuf, ush, ts)
        acc = jnp.zeros((ts, C), F32) + cb_ref[...]
        for k in range(taps):
            acc = acc + w_ref[k:k + 1, :] * _rows_from(ubuf, ush, HALO - (taps - 1) + k, ts)
        u1_ref[...] = acc
        mu = jnp.mean(acc, axis=-1, keepdims=True)
        xc = acc - mu
        rstd = lax.rsqrt(jnp.mean(xc * xc, axis=-1, keepdims=True) + NORM_EPS)
        u2 = xc * rstd * lg_ref[...] + lb_ref[...]
        u3_ref[...] = (u2 * _sig(u2)).astype(BF16)
        ubuf[0:HALO, :] = ubuf[ts:ts + HALO, :]

    vec = lambda a: pl.BlockSpec(a.shape, lambda i: (0, 0))
    return _pcall(
        kern, name="conv_fwd", grid=(S // ts,),
        in_specs=[pl.BlockSpec((ts, C), lambda i: (i, acol)), pl.BlockSpec((ts, C), lambda i: (i, bcol)),
                  vec(w_pad), vec(cb), vec(lg), vec(lb)],
        out_specs=[pl.BlockSpec((ts, C), lambda i: (i, 0))] * 2,
        out_shape=[jax.ShapeDtypeStruct((S, C), F32), jax.ShapeDtypeStruct((S, C), BF16)],
        scratch_shapes=[pltpu.VMEM((HALO + ts, C), F32),
                        pltpu.VMEM((SUBLANES - 1, HALO + ts - SUBLANES, C), F32)],
        compiler_params=_cparams(("arbitrary",)),
    )(proj, proj, w_pad, cb, lg, lb)


def _conv_bwd(du3, u1, proj, acol, bcol, w_pad, lg, lb, S, C, taps, ts):
    ts = min(ts, S)
    nt = S // ts
    hb = ts // HALO

    def ln_bwd(g, u, lgv, lbv):
        mu = jnp.mean(u, axis=-1, keepdims=True)
        xc = u - mu
        rstd = lax.rsqrt(jnp.mean(xc * xc, axis=-1, keepdims=True) + NORM_EPS)
        xh = xc * rstd
        u2 = xh * lgv + lbv
        s = _sig(u2)
        du2 = g * (s + u2 * s * (1.0 - s))
        dxh = du2 * lgv
        du1 = rstd * (dxh - jnp.mean(dxh, axis=-1, keepdims=True)
                      - xh * jnp.mean(dxh * xh, axis=-1, keepdims=True))
        return du1, du2, xh

    def kern(g_ref, u_ref, a_ref, b_ref, gn_ref, un_ref, ap_ref, bp_ref, w_ref, lg_ref, lb_ref,
             da_ref, db_ref, dw_ref, dcb_ref, dlg_ref, dlb_ref, dbuf, ubuf, dsh, ush):
        i = pl.program_id(0)

        @pl.when(i == 0)
        def _():
            dw_ref[...] = jnp.zeros(dw_ref.shape, F32)
            dcb_ref[...] = jnp.zeros(dcb_ref.shape, F32)
            dlg_ref[...] = jnp.zeros(dlg_ref.shape, F32)
            dlb_ref[...] = jnp.zeros(dlb_ref.shape, F32)

        lgv, lbv = lg_ref[...], lb_ref[...]
        du1, du2, xh = ln_bwd(g_ref[...], u_ref[...], lgv, lbv)
        dbuf[0:ts, :] = du1
        du1n, _, _ = ln_bwd(gn_ref[...], un_ref[...], lgv, lbv)
        dbuf[ts:ts + HALO, :] = jnp.where(i < nt - 1, du1n, 0.0)
        a = a_ref[...]
        sb = _sig(b_ref[...])
        ubuf[HALO:HALO + ts, :] = a * sb
        ubuf[0:HALO, :] = jnp.where(i > 0, ap_ref[...] * _sig(bp_ref[...]), 0.0)
        dcb_ref[...] += _csum(du1)
        dlg_ref[...] += _csum(du2 * xh)
        dlb_ref[...] += _csum(du2)
        _shift_copies(dbuf, dsh, ts)
        _shift_copies(ubuf, ush, ts)
        for r0 in range(0, ts, CONV_ROWS):
            du0 = jnp.zeros((CONV_ROWS, C), F32)
            for k in range(taps):
                du0 = du0 + w_ref[k:k + 1, :] * _rows_from(dbuf, dsh, r0 + taps - 1 - k, CONV_ROWS)
            ac = a_ref[r0:r0 + CONV_ROWS, :]
            sc = _sig(b_ref[r0:r0 + CONV_ROWS, :])
            da_ref[r0:r0 + CONV_ROWS, :] = (du0 * sc).astype(BF16)
            db_ref[r0:r0 + CONV_ROWS, :] = (du0 * ac * sc * (1.0 - sc)).astype(BF16)
        for k0 in range(0, taps, CONV_TAPS):
            ks = range(k0, min(k0 + CONV_TAPS, taps))
            accs = [jnp.zeros((SUBLANES, C), F32) for _ in ks]
            for r0 in range(0, ts, CONV_ROWS):
                d = dbuf[r0:r0 + CONV_ROWS, :]
                for t, k in enumerate(ks):
                    prod = d * _rows_from(ubuf, ush, r0 + HALO - (taps - 1) + k, CONV_ROWS)
                    accs[t] = accs[t] + jnp.sum(prod.reshape(CONV_ROWS // SUBLANES, SUBLANES, C), axis=0)
            for t, k in enumerate(ks):
                dw_ref[k:k + 1, :] += _csum(accs[t])

    vec = lambda a: pl.BlockSpec(a.shape, lambda i: (0, 0))
    tile = lambda cb: pl.BlockSpec((ts, C), functools.partial(lambda i, cb: (i, cb), cb=cb))
    nxt = lambda cb: pl.BlockSpec(
        (HALO, C), functools.partial(lambda i, cb: (jnp.minimum((i + 1) * hb, nt * hb - 1), cb), cb=cb))
    prv = lambda cb: pl.BlockSpec(
        (HALO, C), functools.partial(lambda i, cb: (jnp.maximum(i * hb - 1, 0), cb), cb=cb))
    return _pcall(
        kern, name="conv_bwd", grid=(nt,),
        in_specs=[tile(0), tile(0), tile(acol), tile(bcol), nxt(0), nxt(0), prv(acol), prv(bcol),
                  vec(w_pad), vec(lg), vec(lb)],
        out_specs=[pl.BlockSpec((ts, C), lambda i: (i, 0))] * 2
        + [pl.BlockSpec(w_pad.shape, lambda i: (0, 0))] + [pl.BlockSpec((1, C), lambda i: (0, 0))] * 3,
        out_shape=[jax.ShapeDtypeStruct((S, C), BF16)] * 2
        + [jax.ShapeDtypeStruct(w_pad.shape, F32)] + [jax.ShapeDtypeStruct((1, C), F32)] * 3,
        scratch_shapes=[pltpu.VMEM((ts + HALO, C), F32), pltpu.VMEM((HALO + ts, C), F32)]
        + [pltpu.VMEM((SUBLANES - 1, HALO + ts - SUBLANES, C), F32)] * 2,
        compiler_params=_cparams(("arbitrary",)),
    )(du3, u1, proj, proj, du3, u1, proj, proj, w_pad, lg, lb)


def _gate_merge(proj, gacol, gbcol, ba, bb, S, D):
    def body(ga_ref, gb_ref, a_ref, b_ref, out_ref):
        out_ref[...] = (_sig(ga_ref[...]) * a_ref[...] + _sig(gb_ref[...]) * b_ref[...]).astype(BF16)
    return _rowcall("gate_merge", body, S, 512,
                    [(proj, D, gacol), (proj, D, gbcol), (ba, D, 0), (bb, D, 0)], [], [(D, BF16)], [])[0]


def _gate_bwd(dm, proj, gacol, gbcol, ba, bb, S, D):
    def body(dm_ref, ga_ref, gb_ref, a_ref, b_ref, da_ref, db_ref, dga_ref, dgb_ref):
        dmv = dm_ref[...]
        sa, sb = _sig(ga_ref[...]), _sig(gb_ref[...])
        da_ref[...] = (dmv * sa).astype(BF16)
        db_ref[...] = (dmv * sb).astype(BF16)
        dga_ref[...] = (dmv * a_ref[...] * sa * (1.0 - sa)).astype(BF16)
        dgb_ref[...] = (dmv * b_ref[...] * sb * (1.0 - sb)).astype(BF16)
    return _rowcall("gate_bwd", body, S, 512,
                    [(dm, D, 0), (proj, D, gacol), (proj, D, gbcol), (ba, D, 0), (bb, D, 0)], [],
                    [(D, BF16)] * 4, [])


def _resid_norm2(x, mo, g1, g, sc, sh, S, D):
    def body(x_ref, mo_ref, g1_ref, g_ref, sc_ref, sh_ref, x1_ref, h_ref):
        x1 = x_ref[...] + g1_ref[...] * mo_ref[...]
        x1_ref[...] = x1
        r = lax.rsqrt(jnp.mean(x1 * x1, axis=-1, keepdims=True) + NORM_EPS)
        h_ref[...] = ((x1 * r * g_ref[...]) * (1.0 + sc_ref[...]) + sh_ref[...]).astype(BF16)
    return _rowcall("resid_norm2", body, S, 512, [(x, D, 0), (mo, D, 0)], [g1, g, sc, sh],
                    [(D, F32), (D, BF16)], [])


def _loss_dy(x1, ml, tgt, g2, S, D):
    def body(x1_ref, ml_ref, t_ref, g2_ref, dy_ref, dml_ref, sq_ref, dg2_ref):
        mlv = ml_ref[...]
        diff = x1_ref[...] + g2_ref[...] * mlv - t_ref[...]
        dy = diff * (1.0 / D)
        dy_ref[...] = dy
        dml_ref[...] = (dy * g2_ref[...]).astype(BF16)
        sq_ref[...] += _csum(diff * diff)
        dg2_ref[...] += _csum(dy * mlv)
    return _rowcall("loss_dy", body, S, 512, [(x1, D, 0), (ml, D, 0), (tgt, D, 0)], [g2],
                    [(D, F32), (D, BF16)], [(1, D), (1, D)])


def _norm_bwd(name, xin, dh, dres, g, sc, S, D, extra=None):
    def body(*refs):
        if extra is None:
            x_ref, dh_ref, dr_ref, g_ref, sc_ref, dx_ref, dsh_ref, dsc_ref, dg_ref = refs
        else:
            (x_ref, dh_ref, dr_ref, mo_ref, g_ref, sc_ref, g1_ref,
             dx_ref, dmo_ref, dsh_ref, dsc_ref, dg_ref, dg1_ref) = refs
        xv, dhv, gv = x_ref[...], dh_ref[...], g_ref[...]
        r = lax.rsqrt(jnp.mean(xv * xv, axis=-1, keepdims=True) + NORM_EPS)
        xh = xv * r
        dsh_ref[...] += _csum(dhv)
        dsc_ref[...] += _csum(dhv * xh * gv)
        dxg = dhv * (1.0 + sc_ref[...])
        dg_ref[...] += _csum(dxg * xh)
        dxh = dxg * gv
        dx = dr_ref[...] + r * (dxh - xh * jnp.mean(dxh * xh, axis=-1, keepdims=True))
        dx_ref[...] = dx
        if extra is not None:
            dmo_ref[...] = (dx * g1_ref[...]).astype(BF16)
            dg1_ref[...] += _csum(dx * mo_ref[...])

    rows = [(xin, D, 0), (dh, D, 0), (dres, D, 0)]
    vecs = [g, sc]
    if extra is None:
        return _rowcall(name, body, S, 512, rows, vecs, [(D, F32)], [(1, D)] * 3)
    return _rowcall(name, body, S, 512, rows + [(extra[0], D, 0)], vecs + [extra[1]],
                    [(D, F32), (D, BF16)], [(1, D)] * 4)


def _ada_fwd(c_all, w, b_part):
    B, D = c_all.shape
    N = w.shape[1]
    tn = min(512, N)

    def kern(c_ref, w_ref, b_ref, o_ref):
        cv = c_ref[...]
        ca = cv * _sig(cv)
        o_ref[...] = jnp.dot(ca, w_ref[...], precision=lax.Precision.HIGHEST,
                             preferred_element_type=F32) + b_ref[...]

    return _pcall(
        kern, name="ada_fwd", grid=(N // tn,),
        in_specs=[pl.BlockSpec((B, D), lambda j: (0, 0)), pl.BlockSpec((D, tn), lambda j: (0, j)),
                  pl.BlockSpec((1, tn), lambda j: (0, j))],
        out_specs=pl.BlockSpec((B, tn), lambda j: (0, j)),
        out_shape=jax.ShapeDtypeStruct((B, N), F32),
        compiler_params=_cparams(("parallel",)),
    )(c_all, w, b_part)


def _ada_wgrad(c_t_pad, dmod_pad):
    D = c_t_pad.shape[0]
    N = dmod_pad.shape[1]
    tn = min(512, N)

    def kern(c_ref, d_ref, o_ref):
        cv = c_ref[...]
        ca = cv * _sig(cv)
        o_ref[...] = jnp.dot(ca, d_ref[...], precision=lax.Precision.HIGHEST,
                             preferred_element_type=F32)

    return _pcall(
        kern, name="ada_wgrad", grid=(N // tn,),
        in_specs=[pl.BlockSpec((D, LANES), lambda j: (0, 0)), pl.BlockSpec((LANES, tn), lambda j: (0, j))],
        out_specs=pl.BlockSpec((D, tn), lambda j: (0, j)),
        out_shape=jax.ShapeDtypeStruct((D, N), F32),
        compiler_params=_cparams(("parallel",)),
    )(c_t_pad, dmod_pad)


def _ag_small(name, arrs):
    n = len(arrs)

    def kern(*refs):
        ins, outs = refs[:n], refs[n:2 * n]
        send, recv = refs[2 * n], refs[2 * n + 1]
        x, y, c = lax.axis_index("x"), lax.axis_index("y"), lax.axis_index("c")
        me = 4 * x + 2 * y + c

        def copy(i, m, slot):
            peer = (x ^ ((m >> 2) & 1), y ^ ((m >> 1) & 1), c ^ (m & 1))
            return pltpu.make_async_remote_copy(
                src_ref=ins[i], dst_ref=outs[i].at[slot],
                send_sem=send.at[i * 7 + m - 1], recv_sem=recv.at[i * 7 + m - 1],
                device_id=peer, device_id_type=MESH)

        for i in range(n):
            outs[i][me] = ins[i][...]
            for m in range(1, 8):
                copy(i, m, me).start()
        for i in range(n):
            for m in range(1, 8):
                copy(i, m, me).wait_send()
                copy(i, m, me ^ m).wait_recv()

    vm = pl.BlockSpec(memory_space=pltpu.VMEM)
    return _pcall(
        kern, name=name, in_specs=[vm] * n, out_specs=[vm] * n,
        out_shape=[jax.ShapeDtypeStruct((8,) + a.shape, a.dtype) for a in arrs],
        scratch_shapes=[pltpu.SemaphoreType.DMA((7 * n,)), pltpu.SemaphoreType.DMA((7 * n,))],
        compiler_params=pltpu.CompilerParams(has_side_effects=True),
    )(*arrs)


def _exchange(name, arrs, plan):
    out_shape, scratch, phases = plan(arrs)

    def kern(*refs):
        for phase in phases(refs):
            phase()

    return _pcall(
        kern, name=name, in_specs=[ANY] * len(arrs), out_specs=[ANY] * len(out_shape),
        out_shape=out_shape, scratch_shapes=scratch,
        compiler_params=pltpu.CompilerParams(has_side_effects=True),
    )(*arrs)


def _ride(plan_and_arrs, n_in, n_out):
    if plan_and_arrs is None:
        return [], [], [], [], [], lambda refs, first, middle, last: ((lambda: None), (lambda: None))
    plan, arrs = plan_and_arrs
    out_shape, scratch, phases = plan(arrs)
    na, no = len(arrs), len(out_shape)

    def hook(refs, first, middle, last):
        mine = refs[n_in:n_in + na] + refs[n_in + na + n_out:]
        start, mid, finish = phases(mine)

        def before():
            pl.when(first)(start)
            pl.when(middle)(mid)

        def after():
            pl.when(last)(finish)

        return before, after

    return list(arrs), [ANY] * na, [ANY] * no, out_shape, scratch, hook


def _gather_plan(arrs):
    n = len(arrs)

    def phases(refs):
        ins, outs = refs[:n], refs[n:2 * n]
        s1, r1, s2, r2, loc = refs[2 * n:2 * n + 5]
        x, y, c = lax.axis_index("x"), lax.axis_index("y"), lax.axis_index("c")
        me = 2 * x + y

        def half(i, hc):
            hr = ins[i].shape[0] // 2
            return pl.ds(hc * hr, hr)

        def own(i):
            return pltpu.make_async_remote_copy(
                src_ref=ins[i], dst_ref=outs[i].at[me], send_sem=loc.at[i], recv_sem=loc.at[n + i],
                device_id=(x, y, 1 - c), device_id_type=MESH)

        def fetch(i, m, slot):
            px, py = x ^ ((m >> 1) & 1), y ^ (m & 1)
            return pltpu.make_async_remote_copy(
                src_ref=ins[i].at[half(i, c)], dst_ref=outs[i].at[slot, half(i, c)],
                send_sem=s1.at[i * 3 + m - 1], recv_sem=r1.at[i * 3 + m - 1],
                device_id=(px, py, c), device_id_type=MESH)

        def passed(i, m, hc):
            return pltpu.make_async_remote_copy(
                src_ref=outs[i].at[me ^ m, half(i, hc)], dst_ref=outs[i].at[me ^ m, half(i, hc)],
                send_sem=s2.at[i * 3 + m - 1], recv_sem=r2.at[i * 3 + m - 1],
                device_id=(x, y, 1 - c), device_id_type=MESH)

        def start():
            for i in range(n):
                for m in range(1, 4):
                    fetch(i, m, me).start()
            for i in range(n):
                own(i).start()

        def mid():
            for i in range(n):
                for m in range(1, 4):
                    fetch(i, m, me ^ m).wait_recv()
                    passed(i, m, c).start()

        def finish():
            for i in range(n):
                own(i).wait()
                for m in range(1, 4):
                    fetch(i, m, me).wait_send()
                    passed(i, m, c).wait_send()
                    passed(i, m, 1 - c).wait_recv()

        return start, mid, finish

    out_shape = [jax.ShapeDtypeStruct((4,) + a.shape, a.dtype) for a in arrs]
    scratch = [pltpu.SemaphoreType.DMA((3 * n,))] * 4 + [pltpu.SemaphoreType.DMA((2 * n,))]
    return out_shape, scratch, phases


def _pair_send_halves(name, arrs):
    n = len(arrs)

    def kern(*refs):
        ins, outs = refs[:n], refs[n:2 * n]
        send, recv = refs[2 * n], refs[2 * n + 1]
        x, y, c = lax.axis_index("x"), lax.axis_index("y"), lax.axis_index("c")

        def copy(i, k, hc):
            return pltpu.make_async_remote_copy(
                src_ref=ins[i].at[k, hc], dst_ref=outs[i].at[k],
                send_sem=send.at[i * 4 + k], recv_sem=recv.at[i * 4 + k],
                device_id=(x, y, 1 - c), device_id_type=MESH)

        for i in range(n):
            for k in range(4):
                copy(i, k, 1 - c).start()
        for i in range(n):
            for k in range(4):
                copy(i, k, 1 - c).wait()

    return _pcall(
        kern, name=name, in_specs=[ANY] * n, out_specs=[ANY] * n,
        out_shape=[jax.ShapeDtypeStruct((4,) + a.shape[2:], a.dtype) for a in arrs],
        scratch_shapes=[pltpu.SemaphoreType.DMA((4 * n,)), pltpu.SemaphoreType.DMA((4 * n,))],
        compiler_params=pltpu.CompilerParams(has_side_effects=True),
    )(*arrs)


def _scatter_plan(arrs):
    n = len(arrs)

    def phases(refs):
        ins, outs = refs[:n], refs[n:2 * n]
        send, recv = refs[2 * n], refs[2 * n + 1]
        x, y, c = lax.axis_index("x"), lax.axis_index("y"), lax.axis_index("c")
        me = 2 * x + y

        def copy(i, m, slot):
            px, py = x ^ ((m >> 1) & 1), y ^ (m & 1)
            return pltpu.make_async_remote_copy(
                src_ref=ins[i].at[2 * px + py], dst_ref=outs[i].at[slot],
                send_sem=send.at[i * 3 + m - 1], recv_sem=recv.at[i * 3 + m - 1],
                device_id=(px, py, c), device_id_type=MESH)

        def start():
            for i in range(n):
                for m in range(1, 4):
                    copy(i, m, me).start()

        def finish():
            for i in range(n):
                for m in range(1, 4):
                    copy(i, m, me).wait_send()
                    copy(i, m, me ^ m).wait_recv()

        return start, (lambda: None), finish

    out_shape = [jax.ShapeDtypeStruct(a.shape, a.dtype) for a in arrs]
    scratch = [pltpu.SemaphoreType.DMA((3 * n,)), pltpu.SemaphoreType.DMA((3 * n,))]
    return out_shape, scratch, phases


def _pair_swap(name, arrs):
    n = len(arrs)

    def kern(*refs):
        ins, outs = refs[:n], refs[n:2 * n]
        send, recv = refs[2 * n], refs[2 * n + 1]
        x, y, c = lax.axis_index("x"), lax.axis_index("y"), lax.axis_index("c")

        def copy(i):
            return pltpu.make_async_remote_copy(
                src_ref=ins[i], dst_ref=outs[i], send_sem=send.at[i], recv_sem=recv.at[i],
                device_id=(x, y, 1 - c), device_id_type=MESH)

        for i in range(n):
            copy(i).start()
        for i in range(n):
            copy(i).wait()

    return _pcall(
        kern, name=name, in_specs=[ANY] * n, out_specs=[ANY] * n,
        out_shape=[jax.ShapeDtypeStruct(a.shape, a.dtype) for a in arrs],
        scratch_shapes=[pltpu.SemaphoreType.DMA((n,)), pltpu.SemaphoreType.DMA((n,))],
        compiler_params=pltpu.CompilerParams(has_side_effects=True),
    )(*arrs)


def _row_tile(R):
    for t in (256, 128, 64, 32, 16, 8):
        if R % t == 0:
            return t
    return R


def _sum_slots(name, parts):
    K, R, C = parts.shape
    tr = _row_tile(R)

    def kern(p_ref, o_ref):
        acc = p_ref[0].astype(F32)
        for k in range(1, K):
            acc = acc + p_ref[k].astype(F32)
        o_ref[...] = acc

    return _pcall(
        kern, name=name, grid=(R // tr,),
        in_specs=[pl.BlockSpec((K, tr, C), lambda i: (0, i, 0))],
        out_specs=pl.BlockSpec((tr, C), lambda i: (i, 0)),
        out_shape=jax.ShapeDtypeStruct((R, C), F32),
        compiler_params=_cparams(("parallel",)),
    )(parts)


def _sum_pair(name, core, mine, theirs):
    K, _, hr, C = mine.shape
    tr = _row_tile(hr)

    def kern(c_ref, a_ref, b_ref, o_ref):
        o_ref[0] = (a_ref[0, 0].astype(F32) + b_ref[0].astype(F32)).astype(BF16)

    return _pcall(
        kern, name=name, out_shape=jax.ShapeDtypeStruct((K, hr, C), BF16),
        grid_spec=pltpu.PrefetchScalarGridSpec(
            num_scalar_prefetch=1, grid=(K, hr // tr),
            in_specs=[pl.BlockSpec((1, 1, tr, C), lambda k, r, c_ref: (k, c_ref[0], r, 0)),
                      pl.BlockSpec((1, tr, C), lambda k, r, c_ref: (k, r, 0))],
            out_specs=pl.BlockSpec((1, tr, C), lambda k, r, c_ref: (k, r, 0))),
        compiler_params=_cparams(("parallel", "parallel")),
    )(core, mine, theirs)


def _sum_chips(name, chip, own, recv):
    K, hr, C = own.shape
    tr = _row_tile(hr)

    def kern(chip_ref, own_ref, *rest):
        r_refs, o_ref = rest[:K], rest[K]
        me = chip_ref[0]
        mine = own_ref[0].astype(F32)
        acc = None
        for k in range(K):
            t = jnp.where(me == k, mine, r_refs[k][0].astype(F32))
            acc = t if acc is None else acc + t
        o_ref[...] = acc

    def other(k):
        return pl.BlockSpec((1, tr, C), lambda r, s: (jnp.where(s[0] == k, (k + 1) % K, k), r, 0))

    return _pcall(
        kern, name=name, out_shape=jax.ShapeDtypeStruct((hr, C), F32),
        grid_spec=pltpu.PrefetchScalarGridSpec(
            num_scalar_prefetch=1, grid=(hr // tr,),
            in_specs=[pl.BlockSpec((1, tr, C), lambda r, s: (s[0], r, 0))] + [other(k) for k in range(K)],
            out_specs=pl.BlockSpec((tr, C), lambda r, s: (r, 0))),
        compiler_params=_cparams(("parallel",)),
    )(chip, own, *([recv] * K))


def _adam_update(w, m, v, g):
    c1 = 1.0 - ADAM_B1 ** ADAM_STEP
    c2 = 1.0 - ADAM_B2 ** ADAM_STEP
    mn = ADAM_B1 * m + (1.0 - ADAM_B1) * g
    vn = ADAM_B2 * v + (1.0 - ADAM_B2) * (g * g)
    return -ADAM_LR * ((mn / c1) / (jnp.sqrt(vn / c2) + ADAM_EPS) + ADAM_WD * w), mn, vn


def _adamw_halves(name, core, w, m, v, mine, theirs):
    R, C = w.shape
    hr = mine.shape[0]
    tr = _row_tile(hr)
    nbh = hr // tr

    def kern(c_ref, w_ref, m_ref, v_ref, a_ref, b_ref, go_ref, d_ref, mo_ref, vo_ref):
        g = jnp.where(pl.program_id(0) // nbh == c_ref[0], a_ref[...], b_ref[...])
        d, mn, vn = _adam_update(w_ref[...], m_ref[...], v_ref[...], g)
        go_ref[...] = g
        d_ref[...] = d
        mo_ref[...] = mn
        vo_ref[...] = vn

    spec = pl.BlockSpec((tr, C), lambda i, s: (i, 0))
    hspec = pl.BlockSpec((tr, C), lambda i, s: (i % nbh, 0))
    return _pcall(
        kern, name=name, out_shape=[jax.ShapeDtypeStruct((R, C), F32)] * 4,
        grid_spec=pltpu.PrefetchScalarGridSpec(
            num_scalar_prefetch=1, grid=(R // tr,),
            in_specs=[spec, spec, spec, hspec, hspec], out_specs=[spec] * 4),
        compiler_params=_cparams(("parallel",)),
    )(core, w, m, v, mine, theirs)


def _adamw(name, w, m, v, gparts):
    R, C = w.shape
    K = gparts.shape[0]
    tr = _row_tile(R)

    def kern(w_ref, m_ref, v_ref, g_ref, go_ref, d_ref, mo_ref, vo_ref):
        g = g_ref[0]
        for k in range(1, K):
            g = g + g_ref[k]
        d, mn, vn = _adam_update(w_ref[...], m_ref[...], v_ref[...], g)
        go_ref[...] = g
        d_ref[...] = d
        mo_ref[...] = mn
        vo_ref[...] = vn

    spec = pl.BlockSpec((tr, C), lambda i: (i, 0))
    return _pcall(
        kern, name=name, grid=(R // tr,),
        in_specs=[spec, spec, spec, pl.BlockSpec((K, tr, C), lambda i: (0, i, 0))],
        out_specs=[spec] * 4,
        out_shape=[jax.ShapeDtypeStruct((R, C), F32)] * 4,
        compiler_params=_cparams(("parallel",)),
    )(w, m, v, gparts)


def _round_up(a, b):
    return (a + b - 1) // b * b


def kernel(x, c, w_ada, b_ada, norm1_g, w_in, b_forget, q_norm_g, k_norm_g, w_attn_proj, conv_w, conv_b, conv_ln_g, conv_ln_b, w_conv_proj, w_out, norm2_g, w_mlp1, w_mlp2, loss_target, m_w_ada, m_b_ada, m_norm1_g, m_w_in, m_b_forget, m_q_norm_g, m_k_norm_g, m_w_attn_proj, m_conv_w, m_conv_b, m_conv_ln_g, m_conv_ln_b, m_w_conv_proj, m_w_out, m_norm2_g, m_w_mlp1, m_w_mlp2, v_w_ada, v_b_ada, v_norm1_g, v_w_in, v_b_forget, v_q_norm_g, v_k_norm_g, v_w_attn_proj, v_conv_w, v_conv_b, v_conv_ln_g, v_conv_ln_b, v_w_conv_proj, v_w_out, v_norm2_g, v_w_mlp1, v_w_mlp2):
    S, D = x.shape[1], x.shape[2]
    NH, HD = b_forget.shape[-1], q_norm_g.shape[-1]
    TAPS = conv_w.shape[1]
    DIN_S = w_in.shape[-1]
    DIN = 4 * DIN_S
    DFF_S = w_mlp1.shape[-1]
    DFF = 4 * DFF_S
    ADA_S = w_ada.shape[-1]
    DS = w_attn_proj.shape[1]
    CS = conv_w.shape[-1]
    assert NH * HD == D and DIN == 7 * D + NH and TAPS - 1 <= HALO and D % LANES == 0 and 2 * HD == LANES
    NP = _round_up(7 * D + LANES, 512)
    TQ = min(512, S)
    NQ = S // TQ
    FCOL = 7 * D // LANES

    xi, yi, ci = lax.axis_index("x"), lax.axis_index("y"), lax.axis_index("c")
    chip = 2 * xi + yi
    dev = 4 * xi + 2 * yi + ci

    x2 = x.reshape(S, D)
    tgt = loss_target.reshape(S, D)

    lane_head = jnp.arange(D, dtype=jnp.int32) // HD
    grp = (lane_head[:, None] == jnp.arange(LANES, dtype=jnp.int32)[None, :]).astype(BF16)
    grp_t = grp.T
    ch = min(256, S)
    ii = jnp.arange(ch, dtype=jnp.int32)
    tri = (ii[None, :] <= ii[:, None]).astype(BF16)
    tri_u = tri.T
    gq_t = jnp.tile(q_norm_g.reshape(1, HD), (1, NH))
    gk_t = jnp.tile(k_norm_g.reshape(1, HD), (1, NH))
    bf_pad = jnp.pad(b_forget.reshape(1, NH), ((0, 0), (0, LANES - NH)))

    c_all, cw_all = _ag_small(
        "ag_c_convw", [c.reshape(1, D), jnp.pad(conv_w.reshape(TAPS, CS), ((0, HALO - TAPS), (0, 0)))])
    c_all = c_all.reshape(8, D)
    b_part = lax.dynamic_slice(b_ada.reshape(1, -1), (0, chip * ADA_S), (1, ADA_S))
    mod_part = _ada_fwd(c_all, w_ada.reshape(D, ADA_S), b_part)
    (mod_all,) = _ag_small("ag_mod", [mod_part])
    mod_full = jnp.concatenate([mod_all[0], mod_all[2], mod_all[4], mod_all[6]], axis=1)
    mod = lax.dynamic_slice(mod_full, (dev, 0), (1, 6 * D))
    sh1, sc1, g1, sh2, sc2, g2 = [mod[:, i * D:(i + 1) * D] for i in range(6)]

    shards = [w_in.reshape(D, DIN_S), w_attn_proj.reshape(DS, D), w_conv_proj.reshape(DS, D),
              w_out.reshape(DS, D), w_mlp1.reshape(D, DFF_S), w_mlp2.reshape(DFF_S, D)]
    shards = [s.astype(BF16) for s in shards]
    (gw_in,) = _exchange("ag_w_in", shards[:1], _gather_plan)
    w_conv = jnp.concatenate([cw_all[0], cw_all[2], cw_all[4], cw_all[6]], axis=1)

    w_in_full = jnp.concatenate([gw_in[k] for k in range(4)], axis=1)
    w_in_p = jnp.concatenate(
        [w_in_full[:, :3 * D], w_in_full[:, 3 * D + NH:], w_in_full[:, 3 * D:3 * D + NH],
         jnp.zeros((D, NP - 7 * D - NH), BF16)], axis=1)

    n1g = norm1_g.reshape(1, D)
    n2g = norm2_g.reshape(1, D)
    h = _norm_mod("norm_mod1", x2, n1g, sc1, sh1, S, D)
    proj = _mm("mm_in", h, w_in_p, "nn", [F32])
    qs, kn, vb = _qk_prep(proj, gq_t, gk_t, grp, grp_t, S, D, HD)
    f_cum = _fgate_fwd(proj, FCOL, bf_pad, tri, S)
    fk_c = f_cum[:, :NH]
    fk_r = fk_c.T.reshape(NH, NQ, 1, TQ)
    fk_b = jnp.repeat(fk_c, HD, axis=1)
    o, o32, lse_b, gw_ap, gw_cp, gw_out, gw_m1, gw_m2 = _flash_fwd(
        qs, kn, vb, fk_r, S, D, HD, TQ, ride=(_gather_plan, shards[1:]))
    w_ap = gw_ap.reshape(D, D)
    w_cp = gw_cp.reshape(D, D)
    w_o = gw_out.reshape(D, D)
    w_m1 = jnp.transpose(gw_m1, (1, 0, 2)).reshape(D, DFF)
    w_m2 = gw_m2.reshape(DFF, D)
    br_a = _mm("mm_attn_proj", o, w_ap, "nn", [F32])
    cb, clg, clb = conv_b.reshape(1, D), conv_ln_g.reshape(1, D), conv_ln_b.reshape(1, D)
    u1, u3 = _conv_fwd(proj, 3, 4, w_conv, cb, clg, clb, S, D, TAPS, 256)
    br_b = _mm("mm_conv_proj", u3, w_cp, "nn", [F32])
    merged = _gate_merge(proj, 5, 6, br_a, br_b, S, D)
    mo = _mm("mm_out", merged, w_o, "nn", [F32])
    x1, h2 = _resid_norm2(x2, mo, g1, n2g, sc2, sh2, S, D)

    def relu2(r):
        rp = jnp.maximum(r, 0.0)
        return r, rp * rp
    a_pre, z = _mm("mm_mlp1", h2, w_m1, "nn", [F32, BF16], epi=relu2)
    ml = _mm("mm_mlp2", z, w_m2, "nn", [F32])
    dy, dml, sq, dg2 = _loss_dy(x1, ml, tgt, g2, S, D)
    loss = lax.psum(0.5 * jnp.sum(sq) / D, ("x", "y", "c"))

    da = _mm("mm_dz", dml, w_m2, "nt", [BF16], epi=lambda r, a: (r * 2.0 * jnp.maximum(a, 0.0),),
             extras=(a_pre,))
    dw_m2 = _mm("mm_dw_mlp2", z, dml, "tn", [F32])
    dw_m1 = _mm("mm_dw_mlp1", h2, da, "tn", [F32])
    dh2 = _mm("mm_dh2", da, w_m1, "nt", [F32])
    dx1, dmo, dsh2, dsc2, dn2g, dg1 = _norm_bwd("norm2_bwd", x1, dh2, dy, n2g, sc2, S, D, extra=(mo, g1))
    dmerged = _mm("mm_dmerged", dmo, w_o, "nt", [F32])
    dw_o = _mm("mm_dw_out", merged, dmo, "tn", [F32])
    dba, dbb, dga, dgb = _gate_bwd(dmerged, proj, 5, 6, br_a, br_b, S, D)
    do = _mm("mm_do", dba, w_ap, "nt", [BF16])
    dw_ap = _mm("mm_dw_attn_proj", o, dba, "tn", [F32])
    du3 = _mm("mm_du3", dbb, w_cp, "nt", [F32])
    dw_cp = _mm("mm_dw_conv_proj", u3, dbb, "tn", [F32])
    dglu_a, dglu_b, dcw, dcb, dclg, dclb = _conv_bwd(du3, u1, proj, 3, 4, w_conv, clg, clb, S, D, TAPS, 256)

    core = ci.astype(jnp.int32).reshape(1)
    chip1 = chip.astype(jnp.int32).reshape(1)
    halves = lambda p: p.astype(BF16).reshape(4, 2, p.shape[1] // 2, p.shape[2])
    names = ["w_in", "w_attn_proj", "w_conv_proj", "w_out", "w_mlp1", "w_mlp2"]
    parts = [halves(p) for p in (dw_ap.reshape(4, DS, D), dw_cp.reshape(4, DS, D), dw_o.reshape(4, DS, D),
                                 jnp.transpose(dw_m1.reshape(D, 4, DFF_S), (1, 0, 2)), dw_m2.reshape(4, DFF_S, D))]
    theirs = _pair_send_halves("rs_pair", parts)
    chip_parts = [_sum_pair("sum_pair_" + nm, core, p, t) for nm, p, t in zip(names[1:], parts, theirs)]

    delta_b = _delta_prep(do, o32, grp, grp_t, S, D)
    to_rows = lambda b: b[:, ::HD].T.reshape(NH, NQ, 1, TQ)
    dkn, dv, dqs, dfq_r, dfk_b, *recvd = _flash_bwd(
        qs, kn, vb, do, fk_b, to_rows(lse_b), to_rows(delta_b), S, D, HD, TQ, ride=(_scatter_plan, chip_parts))
    dq, dk, sq_q, sq_k = _qk_bwd(proj, dqs, dkn, gq_t, gk_t, grp, grp_t, S, D, HD)
    to_cols = lambda r: jnp.pad(r.reshape(NH, S).T, ((0, 0), (0, LANES - NH)))
    dfq_pad = to_cols(dfq_r)
    dfk_pad = jnp.pad(dfk_b[:, ::HD], ((0, 0), (0, LANES - NH)))
    df, dbf = _fgate_bwd(dfk_pad, dfq_pad, proj, FCOL, bf_pad, tri_u, NH, S)
    dproj = jnp.concatenate(
        [dq, dk, dv, dglu_a, dglu_b, dga, dgb, df, jnp.zeros((S, NP - 7 * D - LANES), BF16)], axis=1)
    dw_in_p = _mm("mm_dw_in", h, dproj, "tn", [F32])
    dh = _mm("mm_dh", dproj, w_in_p, "nt", [F32])
    gx, dsh1, dsc1, dn1g = _norm_bwd("norm1_bwd", x2, dh, dx1, n1g, sc1, S, D)

    packed = jnp.concatenate([dsh1, dsc1, dg1, dsh2, dsc2, dg2, dn1g, dcb, dclg, dclb, dn2g,
                              sq_q, sq_k, dbf], axis=1)
    small_all, dcw_all = _ag_small("ag_small_grads", [packed, dcw])
    small = _sum_slots("sum_small", small_all.reshape(8, 1, -1)).reshape(1, -1)
    dmod_sum = small[:, :6 * D]
    seg = lambda k: small[:, (6 + k) * D:(7 + k) * D]
    g_n1g, g_cb, g_clg, g_clb, g_n2g = seg(0), seg(1), seg(2), seg(3), seg(4)
    g_qn = _sum_slots("sum_qn", seg(5).reshape(NH, 1, HD))
    g_kn = _sum_slots("sum_kn", seg(6).reshape(NH, 1, HD))
    g_bf = small[:, 13 * D:13 * D + NH]
    dcw_mine = lax.dynamic_slice(dcw_all[:, :TAPS, :], (0, 0, chip * CS), (8, TAPS, CS))

    dmod_all = small_all.reshape(8, -1)[:, :6 * D]
    dmod_cols = lax.dynamic_slice(dmod_all, (0, chip * ADA_S), (8, ADA_S))
    c_t_pad = jnp.pad(c_all.T, ((0, 0), (0, LANES - 8)))
    g_wada = _ada_wgrad(c_t_pad, jnp.pad(dmod_cols, ((0, LANES - 8), (0, 0))))

    dw_in_full = jnp.concatenate(
        [dw_in_p[:, :3 * D], dw_in_p[:, 7 * D:7 * D + NH], dw_in_p[:, 3 * D:7 * D]], axis=1)
    part_in = halves(jnp.transpose(dw_in_full.reshape(D, 4, DIN_S), (1, 0, 2)))
    (their_in,) = _pair_send_halves("rs_pair_w_in", [part_in])
    chip_in = _sum_pair("sum_pair_w_in", core, part_in, their_in)
    (recv_in,) = _exchange("rs_chips_w_in", [chip_in], _scatter_plan)
    sums = [_sum_chips("sum_" + nm, chip1, p, r)
            for nm, p, r in zip(names, [chip_in] + chip_parts, [recv_in] + list(recvd))]
    others = _pair_swap("pair_grads", sums)

    res = {}
    big = {nm: (a, b) for nm, a, b in zip(names, sums, others)}
    big_w = {"w_in": (w_in, m_w_in, v_w_in), "w_attn_proj": (w_attn_proj, m_w_attn_proj, v_w_attn_proj),
             "w_conv_proj": (w_conv_proj, m_w_conv_proj, v_w_conv_proj), "w_out": (w_out, m_w_out, v_w_out),
             "w_mlp1": (w_mlp1, m_w_mlp1, v_w_mlp1), "w_mlp2": (w_mlp2, m_w_mlp2, v_w_mlp2)}
    for nm in names:
        w_, m_, v_ = big_w[nm]
        shp = w_.shape
        r2 = lambda t: t.reshape(shp[1], shp[2])
        outs = _adamw_halves("adamw_" + nm, core, r2(w_), r2(m_), r2(v_), *big[nm])
        res[nm] = [t.reshape(shp) for t in outs]
    outs = _adamw("adamw_w_ada", w_ada.reshape(D, ADA_S), m_w_ada.reshape(D, ADA_S),
                  v_w_ada.reshape(D, ADA_S), g_wada.reshape(1, D, ADA_S))
    res["w_ada"] = [t.reshape(w_ada.shape) for t in outs]
    outs = _adamw("adamw_conv_w", conv_w.reshape(TAPS, CS), m_conv_w.reshape(TAPS, CS),
                  v_conv_w.reshape(TAPS, CS), dcw_mine)
    res["conv_w"] = [t.reshape(conv_w.shape) for t in outs]

    small_w = [("b_ada", b_ada, m_b_ada, v_b_ada, dmod_sum), ("norm1_g", norm1_g, m_norm1_g, v_norm1_g, g_n1g),
               ("b_forget", b_forget, m_b_forget, v_b_forget, g_bf),
               ("q_norm_g", q_norm_g, m_q_norm_g, v_q_norm_g, g_qn),
               ("k_norm_g", k_norm_g, m_k_norm_g, v_k_norm_g, g_kn),
               ("conv_b", conv_b, m_conv_b, v_conv_b, g_cb), ("conv_ln_g", conv_ln_g, m_conv_ln_g, v_conv_ln_g, g_clg),
               ("conv_ln_b", conv_ln_b, m_conv_ln_b, v_conv_ln_b, g_clb),
               ("norm2_g", norm2_g, m_norm2_g, v_norm2_g, g_n2g)]
    cat = lambda ts: jnp.concatenate([t.reshape(1, -1) for t in ts], axis=1)
    outs = _adamw("adamw_small", cat([t[1] for t in small_w]), cat([t[2] for t in small_w]),
                  cat([t[3] for t in small_w]), cat([t[4] for t in small_w]).reshape(1, 1, -1))
    off = 0
    for nm, w_, _, _, _ in small_w:
        n = w_.size
        res[nm] = [t[:, off:off + n].reshape(w_.shape) for t in outs]
        off += n

    order = ["w_ada", "b_ada", "norm1_g", "w_in", "b_forget", "q_norm_g", "k_norm_g", "w_attn_proj", "conv_w",
             "conv_b", "conv_ln_g", "conv_ln_b", "w_conv_proj", "w_out", "norm2_g", "w_mlp1", "w_mlp2"]
    return (loss, gx.reshape(x.shape), *[res[n][0] for n in order], *[res[n][1] for n in order],
            *[res[n][2] for n in order], *[res[n][3] for n in order])
```

```python
import functools

import jax
import jax.numpy as jnp
from jax import lax
from jax.experimental import pallas as pl
from jax.experimental.pallas import tpu as pltpu

F32 = jnp.float32
BF16 = jnp.bfloat16
MESH = pl.DeviceIdType.MESH
ANY = pl.BlockSpec(memory_space=pl.ANY)

NORM_EPS = 1e-6
ADAM_LR = 0.001
ADAM_B1 = 0.9
ADAM_B2 = 0.999
ADAM_EPS = 1e-08
ADAM_WD = 0.01
ADAM_STEP = 10
LANES = 128
SUBLANES = 8
HALO = 32
CONV_ROWS = 32
CONV_TAPS = 4
NEG = -1e30
VMEM_LIMIT = 56 * 1024 * 1024


def _pcall(body, **kw):
    return pl.pallas_call(body, **kw)


def _cparams(sem=None):
    if sem is None:
        return pltpu.CompilerParams(vmem_limit_bytes=VMEM_LIMIT)
    return pltpu.CompilerParams(dimension_semantics=sem, vmem_limit_bytes=VMEM_LIMIT)


def _sig(x):
    return 1.0 / (1.0 + jnp.exp(-x))


def _split3(x):
    x1 = x.astype(BF16)
    r = x - x1.astype(F32)
    x2 = r.astype(BF16)
    x3 = (r - x2.astype(F32)).astype(BF16)
    return x1, x2, x3


def _dot_rs(x, e):
    out = None
    for t in _split3(x):
        d = jnp.dot(t, e, preferred_element_type=F32)
        out = d if out is None else out + d
    return out


def _dot_ls(e, x):
    out = None
    for t in _split3(x):
        d = jnp.dot(e, t, preferred_element_type=F32)
        out = d if out is None else out + d
    return out


def _tile(n, want):
    if n <= want:
        return n
    t = want - want % LANES
    while n % t:
        t -= LANES
    assert t > 0, (n, want)
    return t


_DIMS = {"nn": ((1,), (0,)), "nt": ((1,), (1,)), "tn": ((0,), (0,))}


def _mm(name, a, b, mode, out_dtypes, epi=None, extras=(), tm=1024, tn=1024, tk=4096, ride=None):
    if mode == "nn":
        (M, K), (_, N) = a.shape, b.shape
    elif mode == "nt":
        (M, K), (N, _) = a.shape, b.shape
    else:
        (K, M), (_, N) = a.shape, b.shape
    tm, tn, tk = _tile(M, tm), _tile(N, tn), _tile(K, tk)
    nm, nn, nk = M // tm, N // tn, K // tk
    ne, no = len(extras), len(out_dtypes)
    dims = (_DIMS[mode], ((), ()))
    r_in, r_ispec, r_ospec, r_oshape, r_scratch, r_hook = _ride(ride, 2 + ne, no)

    def kern(*refs):
        a_ref, b_ref = refs[0], refs[1]
        e_refs = refs[2:2 + ne]
        o_refs = refs[2 + ne + len(r_in):2 + ne + len(r_in) + no]
        i, j, k = pl.program_id(0), pl.program_id(1), pl.program_id(2)
        before, after = r_hook(refs, (i == 0) & (j == 0) & (k == 0), (i == nm // 2) & (j == 0) & (k == 0),
                               (i == nm - 1) & (j == nn - 1) & (k == nk - 1))
        before()
        d = lax.dot_general(a_ref[...], b_ref[...], dims, preferred_element_type=F32)

        def finish(r):
            outs = (r,) if epi is None else epi(r, *[e[...] for e in e_refs])
            for o_ref, o in zip(o_refs, outs):
                o_ref[...] = o.astype(o_ref.dtype)

        if nk == 1:
            finish(d)
        else:
            acc = refs[-1]

            @pl.when(k == 0)
            def _():
                acc[...] = d

            @pl.when((k > 0) & (k < nk - 1))
            def _():
                acc[...] += d

            @pl.when(k == nk - 1)
            def _():
                finish(acc[...] + d)
        after()

    if mode == "tn":
        a_spec = pl.BlockSpec((tk, tm), lambda i, j, k: (k, i))
    else:
        a_spec = pl.BlockSpec((tm, tk), lambda i, j, k: (i, k))
    if mode == "nt":
        b_spec = pl.BlockSpec((tn, tk), lambda i, j, k: (j, k))
    else:
        b_spec = pl.BlockSpec((tk, tn), lambda i, j, k: (k, j))
    mn_spec = pl.BlockSpec((tm, tn), lambda i, j, k: (i, j))
    outs = _pcall(
        kern, name=name, grid=(nm, nn, nk),
        in_specs=[a_spec, b_spec] + [mn_spec] * ne + r_ispec,
        out_specs=[mn_spec] * no + r_ospec,
        out_shape=[jax.ShapeDtypeStruct((M, N), dt) for dt in out_dtypes] + r_oshape,
        scratch_shapes=r_scratch + ([pltpu.VMEM((tm, tn), F32)] if nk > 1 else []),
        compiler_params=_cparams(("arbitrary",) * 3 if ride else ("parallel", "parallel", "arbitrary")),
    )(a, b, *extras, *r_in)
    return outs[0] if len(outs) == 1 else outs


def _rowcall(name, body, S, ts, row_ins, vec_ins, row_outs, vec_outs):
    ts = min(ts, S)
    nri, nvi, nro, nvo = len(row_ins), len(vec_ins), len(row_outs), len(vec_outs)

    def kern(*refs):
        ins = refs[:nri + nvi]
        outs = refs[nri + nvi:]
        if nvo:
            @pl.when(pl.program_id(0) == 0)
            def _():
                for r in outs[nro:]:
                    r[...] = jnp.zeros(r.shape, r.dtype)
        body(*ins, *outs)

    in_specs = [pl.BlockSpec((ts, w), functools.partial(lambda i, cb: (i, cb), cb=cb))
                for (_, w, cb) in row_ins]
    in_specs += [pl.BlockSpec(v.shape, lambda i: (0, 0)) for v in vec_ins]
    out_specs = [pl.BlockSpec((ts, w), lambda i: (i, 0)) for (w, _) in row_outs]
    out_specs += [pl.BlockSpec((r, w), lambda i: (0, 0)) for (r, w) in vec_outs]
    out_shape = [jax.ShapeDtypeStruct((S, w), dt) for (w, dt) in row_outs]
    out_shape += [jax.ShapeDtypeStruct((r, w), F32) for (r, w) in vec_outs]
    return _pcall(
        kern, name=name, grid=(S // ts,), in_specs=in_specs, out_specs=out_specs,
        out_shape=out_shape,
        compiler_params=_cparams(("arbitrary",) if nvo else ("parallel",)),
    )(*[a for (a, _, _) in row_ins], *vec_ins)


def _csum(x):
    return jnp.sum(x, axis=0, keepdims=True)


def _norm_mod(name, x, g, sc, sh, S, D):
    def body(x_ref, g_ref, sc_ref, sh_ref, h_ref):
        xv = x_ref[...]
        r = lax.rsqrt(jnp.mean(xv * xv, axis=-1, keepdims=True) + NORM_EPS)
        h_ref[...] = ((xv * r * g_ref[...]) * (1.0 + sc_ref[...]) + sh_ref[...]).astype(BF16)
    return _rowcall(name, body, S, 512, [(x, D, 0)], [g, sc, sh], [(D, BF16)], [])[0]


def _head_rstd(v, grp, grp_t, hd):
    ss = _dot_rs(v * v, grp) * (1.0 / hd)
    r = lax.rsqrt(ss + NORM_EPS)
    return _dot_rs(r, grp_t)


def _qk_prep(proj, gq, gk, grp, grp_t, S, D, hd):
    scale = hd ** -0.5

    def body(q_ref, k_ref, v_ref, gq_ref, gk_ref, g_ref, gt_ref, qs_ref, kn_ref, vb_ref):
        q = q_ref[...]
        k = k_ref[...]
        rq = _head_rstd(q, g_ref[...], gt_ref[...], hd)
        rk = _head_rstd(k, g_ref[...], gt_ref[...], hd)
        qs_ref[...] = ((q * rq * gq_ref[...]).astype(BF16).astype(F32) * scale).astype(BF16)
        kn_ref[...] = (k * rk * gk_ref[...]).astype(BF16)
        vb_ref[...] = v_ref[...].astype(BF16)

    return _rowcall("qk_prep", body, S, 256, [(proj, D, 0), (proj, D, 1), (proj, D, 2)],
                    [gq, gk, grp, grp_t], [(D, BF16)] * 3, [])


def _fgate_fwd(proj, fcol, bf_pad, tri, S):
    ch = tri.shape[0]

    def body(f_ref, b_ref, tri_ref, out_ref):
        carry = jnp.zeros((1, LANES), F32)
        for c in range(S // ch):
            z = f_ref[c * ch:(c + 1) * ch, :] + b_ref[...]
            lf = jnp.minimum(z, 0.0) - jnp.log(1.0 + jnp.exp(-jnp.abs(z)))
            out_ref[c * ch:(c + 1) * ch, :] = _dot_ls(tri_ref[...], lf) + carry
            carry = carry + _csum(lf)

    return _rowcall("fgate_fwd", body, S, S, [(proj, LANES, fcol)], [bf_pad, tri],
                    [(LANES, F32)], [])[0]


def _fgate_bwd(dfk, dfq, proj, fcol, bf_pad, tri_u, nh, S):
    ch = tri_u.shape[0]

    def body(d_ref, dq_ref, f_ref, b_ref, tri_ref, df_ref, db_ref):
        lane = lax.broadcasted_iota(jnp.int32, (ch, LANES), 1)
        carry = jnp.zeros((1, LANES), F32)
        tot = jnp.zeros((1, LANES), F32)
        for c in reversed(range(S // ch)):
            d = d_ref[c * ch:(c + 1) * ch, :] + dq_ref[c * ch:(c + 1) * ch, :]
            rc = _dot_ls(tri_ref[...], d) + carry
            carry = carry + _csum(d)
            z = f_ref[c * ch:(c + 1) * ch, :] + b_ref[...]
            df = jnp.where(lane < nh, rc * _sig(-z), 0.0)
            df_ref[c * ch:(c + 1) * ch, :] = df.astype(BF16)
            tot = tot + _csum(df)
        db_ref[...] += tot

    return _rowcall("fgate_bwd", body, S, S, [(dfk, LANES, 0), (dfq, LANES, 0), (proj, LANES, fcol)],
                    [bf_pad, tri_u], [(LANES, BF16)], [(1, LANES)])


def _keep(v, mask):
    return jnp.where(mask, v.astype(F32), 0.0).astype(BF16)


def _lane_col(blk, lane, at):
    return jnp.sum(jnp.where(lane == at, blk, 0.0), axis=-1, keepdims=True)


def _flash_fwd(qs, kn, vb, fk_r, S, D, hd, tq, ride=None):
    hp, nq = D // LANES, S // tq
    r_in, r_ispec, r_ospec, r_oshape, r_scratch, r_hook = _ride(ride, 4, 3)

    def kern(*refs):
        q_ref, k_ref, v_ref, fk_ref = refs[:4]
        o_ref, o32_ref, lse_ref = refs[4 + len(r_in):7 + len(r_in)]
        hi, qi = pl.program_id(0), pl.program_id(1)
        before, after = r_hook(refs, (hi == 0) & (qi == 0), (hi == hp // 2) & (qi == 0),
                               (hi == hp - 1) & (qi == nq - 1))
        before()
        lane = lax.broadcasted_iota(jnp.int32, (tq, LANES), 1)
        row = lax.broadcasted_iota(jnp.int32, (tq, tq), 0)
        col = lax.broadcasted_iota(jnp.int32, (tq, tq), 1)
        hms = [(lane >= j * hd) & (lane < (j + 1) * hd) for j in range(2)]
        q = q_ref[...]
        qms = [_keep(q, hm) for hm in hms]

        def step(ki, state, masked):
            off = pl.multiple_of(ki * tq, tq)
            k = k_ref[pl.ds(off, tq), :]
            v = v_ref[pl.ds(off, tq), :].astype(F32)
            new = []
            for j in range(2):
                m_old, acc = state[j]
                s = lax.dot_general(qms[j], k, (((1,), (1,)), ((), ())), preferred_element_type=F32)
                s = s - fk_ref[j, ki]
                if masked:
                    s = jnp.where(col <= row, s, NEG)
                m_new = jnp.maximum(m_old, jnp.max(s, axis=-1, keepdims=True))
                alpha = jnp.exp(m_old - m_new)
                p = jnp.exp(s - m_new)
                v1 = jnp.where(hms[j], v, 1.0).astype(BF16)
                acc = alpha * acc + jnp.dot(p.astype(BF16), v1, preferred_element_type=F32)
                new.append((m_new, acc))
            return tuple(new)

        init = tuple((jnp.full((tq, 1), NEG, F32), jnp.zeros((tq, LANES), F32)) for _ in range(2))
        state = lax.fori_loop(0, qi, lambda ki, st: step(ki, st, False), init)
        (m0, a0), (m1, a1) = step(qi, state, True)
        l0, l1 = pltpu.roll(a0, hd, 1), pltpu.roll(a1, hd, 1)
        first = lane < hd
        ov = jnp.where(first, a0 / l0, a1 / l1)
        o_ref[...] = ov.astype(BF16)
        o32_ref[...] = ov
        lse_ref[...] = jnp.where(first, m0 + jnp.log(l0), m1 + jnp.log(l1))
        after()

    qspec = pl.BlockSpec((tq, LANES), lambda h, i: (i, h))
    fullspec = pl.BlockSpec((S, LANES), lambda h, i: (0, h))
    return _pcall(
        kern, name="flash_fwd", grid=(hp, nq),
        in_specs=[qspec, fullspec, fullspec,
                  pl.BlockSpec((2, nq, 1, tq), lambda h, i: (h, 0, 0, 0))] + r_ispec,
        out_specs=[qspec, qspec, qspec] + r_ospec,
        out_shape=[jax.ShapeDtypeStruct((S, D), BF16), jax.ShapeDtypeStruct((S, D), F32),
                   jax.ShapeDtypeStruct((S, D), F32)] + r_oshape,
        scratch_shapes=r_scratch,
        compiler_params=_cparams(("arbitrary", "arbitrary")),
    )(qs, kn, vb, fk_r, *r_in)


def _flash_bwd(qs, kn, vb, do, fk_b, lse_r, delta_r, S, D, hd, tq, ride=None):
    hp, nq = D // LANES, S // tq
    r_in, r_ispec, r_ospec, r_oshape, r_scratch, r_hook = _ride(ride, 7, 5)

    def kern(*refs):
        q_ref, do_ref, k_ref, v_ref, fk_ref, lse_ref, dl_ref = refs[:7]
        dk_ref, dv_ref, dq_ref, dfq_ref, dfk_ref = refs[7 + len(r_in):12 + len(r_in)]
        hi, ki = pl.program_id(0), pl.program_id(1)
        before, after = r_hook(refs, (hi == 0) & (ki == 0), (hi == hp // 2) & (ki == 0),
                               (hi == hp - 1) & (ki == nq - 1))
        before()
        lane = lax.broadcasted_iota(jnp.int32, (tq, LANES), 1)
        row = lax.broadcasted_iota(jnp.int32, (tq, tq), 0)
        col = lax.broadcasted_iota(jnp.int32, (tq, tq), 1)
        hms = [(lane >= j * hd) & (lane < (j + 1) * hd) for j in range(2)]
        k = k_ref[...]
        v = v_ref[...]
        fkb = fk_ref[...]
        kms = [_keep(k, hm) for hm in hms]
        vms = [_keep(v, hm) for hm in hms]
        fks = [_lane_col(fkb, lane, 2 * hi + j) for j in range(2)]

        @pl.when(ki == 0)
        def _():
            dfq_ref[...] = jnp.zeros(dfq_ref.shape, F32)
            dq_ref[...] = jnp.zeros(dq_ref.shape, F32)

        def step(qi, acc, masked):
            dk, dv, dfs = acc
            off = pl.multiple_of(qi * tq, tq)
            q = q_ref[pl.ds(off, tq), :]
            g = do_ref[pl.ds(off, tq), :]
            dq = None
            new_dfs = []
            for j in range(2):
                qm = _keep(q, hms[j])
                gm = _keep(g, hms[j])
                st = lax.dot_general(kms[j], q, (((1,), (1,)), ((), ())), preferred_element_type=F32)
                st = st - fks[j]
                if masked:
                    st = jnp.where(row <= col, st, NEG)
                pt = jnp.exp(st - lse_ref[j, qi])
                dv = dv + jnp.dot(pt.astype(BF16), gm, preferred_element_type=F32)
                dpt = lax.dot_general(vms[j], g, (((1,), (1,)), ((), ())), preferred_element_type=F32)
                dst = pt * (dpt - dl_ref[j, qi])
                dsb = dst.astype(BF16)
                dk = dk + jnp.dot(dsb, qm, preferred_element_type=F32)
                t = lax.dot_general(dsb, kms[j], (((0,), (0,)), ((), ())), preferred_element_type=F32)
                dq = t if dq is None else dq + t
                dfq_ref[j, qi] += jnp.sum(dst, axis=0, keepdims=True)
                new_dfs.append(dfs[j] - jnp.sum(dst, axis=1, keepdims=True))
            dq_ref[pl.ds(off, tq), :] += dq
            return dk, dv, tuple(new_dfs)

        zero = jnp.zeros((tq, LANES), F32)
        zcol = jnp.zeros((tq, 1), F32)
        acc = step(ki, (zero, zero, (zcol, zcol)), True)
        dk, dv, dfs = lax.fori_loop(ki + 1, nq, lambda qi, a: step(qi, a, False), acc)
        dk_ref[...] = dk.astype(BF16)
        dv_ref[...] = dv.astype(BF16)
        dfk_ref[...] = jnp.where(lane < hd, dfs[0], dfs[1])
        after()

    kspec = pl.BlockSpec((tq, LANES), lambda h, i: (i, h))
    fullspec = pl.BlockSpec((S, LANES), lambda h, i: (0, h))
    rowspec = pl.BlockSpec((2, nq, 1, tq), lambda h, i: (h, 0, 0, 0))
    return _pcall(
        kern, name="flash_bwd", grid=(hp, nq),
        in_specs=[fullspec, fullspec, kspec, kspec, pl.BlockSpec((tq, LANES), lambda h, i: (i, 0)),
                  rowspec, rowspec] + r_ispec,
        out_specs=[kspec, kspec, fullspec, rowspec, kspec] + r_ospec,
        out_shape=[jax.ShapeDtypeStruct((S, D), BF16), jax.ShapeDtypeStruct((S, D), BF16),
                   jax.ShapeDtypeStruct((S, D), F32), jax.ShapeDtypeStruct((2 * hp, nq, 1, tq), F32),
                   jax.ShapeDtypeStruct((S, D), F32)] + r_oshape,
        scratch_shapes=r_scratch,
        compiler_params=_cparams(("arbitrary", "arbitrary")),
    )(qs, do, kn, vb, fk_b, lse_r, delta_r, *r_in)


def _delta_prep(do, o, lse_b, grp, sel, S, D):
    def body(g_ref, o_ref, l_ref, e_ref, s_ref, dl_ref, lse_ref):
        prod = g_ref[...].astype(F32) * o_ref[...]
        dl_ref[...] = _dot_rs(prod, e_ref[...])
        lse_ref[...] = _dot_rs(l_ref[...], s_ref[...])
    return _rowcall("delta_prep", body, S, 256, [(do, D, 0), (o, D, 0), (lse_b, D, 0)], [grp, sel],
                    [(LANES, F32), (LANES, F32)], [])


def _qk_bwd(proj, dqs, dkn, gq, gk, grp, grp_t, S, D, hd):
    scale = hd ** -0.5

    def one(x, dn, gain, e, et):
        r = _head_rstd(x, e, et, hd)
        xh = x * r
        t = dn * gain
        mean = _dot_rs(_dot_rs(t * xh, e), et) * (1.0 / hd)
        return r * (t - xh * mean), _csum(dn * xh)

    def body(q_ref, k_ref, dq_ref, dk_ref, gq_ref, gk_ref, e_ref, et_ref,
             oq_ref, ok_ref, sq_ref, sk_ref):
        e, et = e_ref[...], et_ref[...]
        dq, sq = one(q_ref[...], dq_ref[...].astype(F32) * scale, gq_ref[...], e, et)
        dk, sk = one(k_ref[...], dk_ref[...].astype(F32), gk_ref[...], e, et)
        oq_ref[...] = dq.astype(BF16)
        ok_ref[...] = dk.astype(BF16)
        sq_ref[...] += sq
        sk_ref[...] += sk

    return _rowcall("qk_bwd", body, S, 256,
                    [(proj, D, 0), (proj, D, 1), (dqs, D, 0), (dkn, D, 0)],
                    [gq, gk, grp, grp_t], [(D, BF16)] * 2, [(1, D)] * 2)


def _shift_copies(buf, sh, ts):
    for b in range(1, SUBLANES):
        sh[b - 1] = buf[b:b + ts + HALO - SUBLANES, :]


def _rows_from(buf, sh, o, ts):
    a, b = divmod(o, SUBLANES)
    if b == 0:
        return buf[o:o + ts, :]
    return sh[b - 1, SUBLANES * a:SUBLANES * a + ts, :]


def _conv_fwd(proj, acol, bcol, w_pad, cb, lg, lb, S, C, taps, ts):
    ts = min(ts, S)

    def kern(a_ref, b_ref, w_ref, cb_ref, lg_ref, lb_ref, u1_ref, u3_ref, ubuf, ush):
        @pl.when(pl.program_id(0) == 0)
        def _():
            ubuf[0:HALO, :] = jnp.zeros((HALO, C), F32)

        ubuf[HALO:HALO + ts, :] = a_ref[...] * _sig(b_ref[...])
        _shift_copies(ubuf, ush, ts)
        acc = jnp.zeros((ts, C), F32) + cb_ref[...]
        for k in range(taps):
            acc = acc + w_ref[k:k + 1, :] * _rows_from(ubuf, ush, HALO - (taps - 1) + k, ts)
        u1_ref[...] = acc
        mu = jnp.mean(acc, axis=-1, keepdims=True)
        xc = acc - mu
        rstd = lax.rsqrt(jnp.mean(xc * xc, axis=-1, keepdims=True) + NORM_EPS)
        u2 = xc * rstd * lg_ref[...] + lb_ref[...]
        u3_ref[...] = (u2 * _sig(u2)).astype(BF16)
        ubuf[0:HALO, :] = ubuf[ts:ts + HALO, :]

    vec = lambda a: pl.BlockSpec(a.shape, lambda i: (0, 0))
    return _pcall(
        kern, name="conv_fwd", grid=(S // ts,),
        in_specs=[pl.BlockSpec((ts, C), lambda i: (i, acol)), pl.BlockSpec((ts, C), lambda i: (i, bcol)),
                  vec(w_pad), vec(cb), vec(lg), vec(lb)],
        out_specs=[pl.BlockSpec((ts, C), lambda i: (i, 0))] * 2,
        out_shape=[jax.ShapeDtypeStruct((S, C), F32), jax.ShapeDtypeStruct((S, C), BF16)],
        scratch_shapes=[pltpu.VMEM((HALO + ts, C), F32),
                        pltpu.VMEM((SUBLANES - 1, HALO + ts - SUBLANES, C), F32)],
        compiler_params=_cparams(("arbitrary",)),
    )(proj, proj, w_pad, cb, lg, lb)


def _conv_bwd(du3, u1, proj, acol, bcol, w_pad, lg, lb, S, C, taps, ts):
    ts = min(ts, S)
    nt = S // ts
    hb = ts // HALO

    def ln_bwd(g, u, lgv, lbv):
        mu = jnp.mean(u, axis=-1, keepdims=True)
        xc = u - mu
        rstd = lax.rsqrt(jnp.mean(xc * xc, axis=-1, keepdims=True) + NORM_EPS)
        xh = xc * rstd
        u2 = xh * lgv + lbv
        s = _sig(u2)
        du2 = g * (s + u2 * s * (1.0 - s))
        dxh = du2 * lgv
        du1 = rstd * (dxh - jnp.mean(dxh, axis=-1, keepdims=True)
                      - xh * jnp.mean(dxh * xh, axis=-1, keepdims=True))
        return du1, du2, xh

    def kern(g_ref, u_ref, a_ref, b_ref, gn_ref, un_ref, ap_ref, bp_ref, w_ref, lg_ref, lb_ref,
             da_ref, db_ref, dw_ref, dcb_ref, dlg_ref, dlb_ref, dbuf, ubuf, dsh, ush):
        i = pl.program_id(0)

        @pl.when(i == 0)
        def _():
            dw_ref[...] = jnp.zeros(dw_ref.shape, F32)
            dcb_ref[...] = jnp.zeros(dcb_ref.shape, F32)
            dlg_ref[...] = jnp.zeros(dlg_ref.shape, F32)
            dlb_ref[...] = jnp.zeros(dlb_ref.shape, F32)

        lgv, lbv = lg_ref[...], lb_ref[...]
        du1, du2, xh = ln_bwd(g_ref[...], u_ref[...], lgv, lbv)
        dbuf[0:ts, :] = du1
        du1n, _, _ = ln_bwd(gn_ref[...], un_ref[...], lgv, lbv)
        dbuf[ts:ts + HALO, :] = jnp.where(i < nt - 1, du1n, 0.0)
        a = a_ref[...]
        sb = _sig(b_ref[...])
        ubuf[HALO:HALO + ts, :] = a * sb
        ubuf[0:HALO, :] = jnp.where(i > 0, ap_ref[...] * _sig(bp_ref[...]), 0.0)
        dcb_ref[...] += _csum(du1)
        dlg_ref[...] += _csum(du2 * xh)
        dlb_ref[...] += _csum(du2)
        _shift_copies(dbuf, dsh, ts)
        _shift_copies(ubuf, ush, ts)
        for r0 in range(0, ts, CONV_ROWS):
            du0 = jnp.zeros((CONV_ROWS, C), F32)
            for k in range(taps):
                du0 = du0 + w_ref[k:k + 1, :] * _rows_from(dbuf, dsh, r0 + taps - 1 - k, CONV_ROWS)
            ac = a_ref[r0:r0 + CONV_ROWS, :]
            sc = _sig(b_ref[r0:r0 + CONV_ROWS, :])
            da_ref[r0:r0 + CONV_ROWS, :] = (du0 * sc).astype(BF16)
            db_ref[r0:r0 + CONV_ROWS, :] = (du0 * ac * sc * (1.0 - sc)).astype(BF16)
        for k0 in range(0, taps, CONV_TAPS):
            ks = range(k0, min(k0 + CONV_TAPS, taps))
            accs = [jnp.zeros((SUBLANES, C), F32) for _ in ks]
            for r0 in range(0, ts, CONV_ROWS):
                d = dbuf[r0:r0 + CONV_ROWS, :]
                for t, k in enumerate(ks):
                    prod = d * _rows_from(ubuf, ush, r0 + HALO - (taps - 1) + k, CONV_ROWS)
                    accs[t] = accs[t] + jnp.sum(prod.reshape(CONV_ROWS // SUBLANES, SUBLANES, C), axis=0)
            for t, k in enumerate(ks):
                dw_ref[k:k + 1, :] += _csum(accs[t])

    vec = lambda a: pl.BlockSpec(a.shape, lambda i: (0, 0))
    tile = lambda cb: pl.BlockSpec((ts, C), functools.partial(lambda i, cb: (i, cb), cb=cb))
    nxt = lambda cb: pl.BlockSpec(
        (HALO, C), functools.partial(lambda i, cb: (jnp.minimum((i + 1) * hb, nt * hb - 1), cb), cb=cb))
    prv = lambda cb: pl.BlockSpec(
        (HALO, C), functools.partial(lambda i, cb: (jnp.maximum(i * hb - 1, 0), cb), cb=cb))
    return _pcall(
        kern, name="conv_bwd", grid=(nt,),
        in_specs=[tile(0), tile(0), tile(acol), tile(bcol), nxt(0), nxt(0), prv(acol), prv(bcol),
                  vec(w_pad), vec(lg), vec(lb)],
        out_specs=[pl.BlockSpec((ts, C), lambda i: (i, 0))] * 2
        + [pl.BlockSpec(w_pad.shape, lambda i: (0, 0))] + [pl.BlockSpec((1, C), lambda i: (0, 0))] * 3,
        out_shape=[jax.ShapeDtypeStruct((S, C), BF16)] * 2
        + [jax.ShapeDtypeStruct(w_pad.shape, F32)] + [jax.ShapeDtypeStruct((1, C), F32)] * 3,
        scratch_shapes=[pltpu.VMEM((ts + HALO, C), F32), pltpu.VMEM((HALO + ts, C), F32)]
        + [pltpu.VMEM((SUBLANES - 1, HALO + ts - SUBLANES, C), F32)] * 2,
        compiler_params=_cparams(("arbitrary",)),
    )(du3, u1, proj, proj, du3, u1, proj, proj, w_pad, lg, lb)


def _gate_merge(proj, gacol, gbcol, ba, bb, S, D):
    def body(ga_ref, gb_ref, a_ref, b_ref, out_ref):
        out_ref[...] = (_sig(ga_ref[...]) * a_ref[...] + _sig(gb_ref[...]) * b_ref[...]).astype(BF16)
    return _rowcall("gate_merge", body, S, 512,
                    [(proj, D, gacol), (proj, D, gbcol), (ba, D, 0), (bb, D, 0)], [], [(D, BF16)], [])[0]


def _gate_bwd(dm, proj, gacol, gbcol, ba, bb, S, D):
    def body(dm_ref, ga_ref, gb_ref, a_ref, b_ref, da_ref, db_ref, dga_ref, dgb_ref):
        dmv = dm_ref[...]
        sa, sb = _sig(ga_ref[...]), _sig(gb_ref[...])
        da_ref[...] = (dmv * sa).astype(BF16)
        db_ref[...] = (dmv * sb).astype(BF16)
        dga_ref[...] = (dmv * a_ref[...] * sa * (1.0 - sa)).astype(BF16)
        dgb_ref[...] = (dmv * b_ref[...] * sb * (1.0 - sb)).astype(BF16)
    return _rowcall("gate_bwd", body, S, 512,
                    [(dm, D, 0), (proj, D, gacol), (proj, D, gbcol), (ba, D, 0), (bb, D, 0)], [],
                    [(D, BF16)] * 4, [])


def _resid_norm2(x, mo, g1, g, sc, sh, S, D):
    def body(x_ref, mo_ref, g1_ref, g_ref, sc_ref, sh_ref, x1_ref, h_ref):
        x1 = x_ref[...] + g1_ref[...] * mo_ref[...]
        x1_ref[...] = x1
        r = lax.rsqrt(jnp.mean(x1 * x1, axis=-1, keepdims=True) + NORM_EPS)
        h_ref[...] = ((x1 * r * g_ref[...]) * (1.0 + sc_ref[...]) + sh_ref[...]).astype(BF16)
    return _rowcall("resid_norm2", body, S, 512, [(x, D, 0), (mo, D, 0)], [g1, g, sc, sh],
                    [(D, F32), (D, BF16)], [])


def _loss_dy(x1, ml, tgt, g2, S, D):
    def body(x1_ref, ml_ref, t_ref, g2_ref, dy_ref, dml_ref, sq_ref, dg2_ref):
        mlv = ml_ref[...]
        diff = x1_ref[...] + g2_ref[...] * mlv - t_ref[...]
        dy = diff * (1.0 / D)
        dy_ref[...] = dy
        dml_ref[...] = (dy * g2_ref[...]).astype(BF16)
        sq_ref[...] += _csum(diff * diff)
        dg2_ref[...] += _csum(dy * mlv)
    return _rowcall("loss_dy", body, S, 512, [(x1, D, 0), (ml, D, 0), (tgt, D, 0)], [g2],
                    [(D, F32), (D, BF16)], [(1, D), (1, D)])


def _norm_bwd(name, xin, dh, dres, g, sc, S, D, extra=None):
    def body(*refs):
        if extra is None:
            x_ref, dh_ref, dr_ref, g_ref, sc_ref, dx_ref, dsh_ref, dsc_ref, dg_ref = refs
        else:
            (x_ref, dh_ref, dr_ref, mo_ref, g_ref, sc_ref, g1_ref,
             dx_ref, dmo_ref, dsh_ref, dsc_ref, dg_ref, dg1_ref) = refs
        xv, dhv, gv = x_ref[...], dh_ref[...], g_ref[...]
        r = lax.rsqrt(jnp.mean(xv * xv, axis=-1, keepdims=True) + NORM_EPS)
        xh = xv * r
        dsh_ref[...] += _csum(dhv)
        dsc_ref[...] += _csum(dhv * xh * gv)
        dxg = dhv * (1.0 + sc_ref[...])
        dg_ref[...] += _csum(dxg * xh)
        dxh = dxg * gv
        dx = dr_ref[...] + r * (dxh - xh * jnp.mean(dxh * xh, axis=-1, keepdims=True))
        dx_ref[...] = dx
        if extra is not None:
            dmo_ref[...] = (dx * g1_ref[...]).astype(BF16)
            dg1_ref[...] += _csum(dx * mo_ref[...])

    rows = [(xin, D, 0), (dh, D, 0), (dres, D, 0)]
    vecs = [g, sc]
    if extra is None:
        return _rowcall(name, body, S, 512, rows, vecs, [(D, F32)], [(1, D)] * 3)
    return _rowcall(name, body, S, 512, rows + [(extra[0], D, 0)], vecs + [extra[1]],
                    [(D, F32), (D, BF16)], [(1, D)] * 4)


def _ada_fwd(c_all, w, b_part):
    B, D = c_all.shape
    N = w.shape[1]
    tn = min(512, N)

    def kern(c_ref, w_ref, b_ref, o_ref):
        cv = c_ref[...]
        ca = cv * _sig(cv)
        o_ref[...] = jnp.dot(ca, w_ref[...], precision=lax.Precision.HIGHEST,
                             preferred_element_type=F32) + b_ref[...]

    return _pcall(
        kern, name="ada_fwd", grid=(N // tn,),
        in_specs=[pl.BlockSpec((B, D), lambda j: (0, 0)), pl.BlockSpec((D, tn), lambda j: (0, j)),
                  pl.BlockSpec((1, tn), lambda j: (0, j))],
        out_specs=pl.BlockSpec((B, tn), lambda j: (0, j)),
        out_shape=jax.ShapeDtypeStruct((B, N), F32),
        compiler_params=_cparams(("parallel",)),
    )(c_all, w, b_part)


def _ada_wgrad(c_t_pad, dmod_pad):
    D = c_t_pad.shape[0]
    N = dmod_pad.shape[1]
    tn = min(512, N)

    def kern(c_ref, d_ref, o_ref):
        cv = c_ref[...]
        ca = cv * _sig(cv)
        o_ref[...] = jnp.dot(ca, d_ref[...], precision=lax.Precision.HIGHEST,
                             preferred_element_type=F32)

    return _pcall(
        kern, name="ada_wgrad", grid=(N // tn,),
        in_specs=[pl.BlockSpec((D, LANES), lambda j: (0, 0)), pl.BlockSpec((LANES, tn), lambda j: (0, j))],
        out_specs=pl.BlockSpec((D, tn), lambda j: (0, j)),
        out_shape=jax.ShapeDtypeStruct((D, N), F32),
        compiler_params=_cparams(("parallel",)),
    )(c_t_pad, dmod_pad)


def _ag_small(name, arrs):
    n = len(arrs)

    def kern(*refs):
        ins, outs = refs[:n], refs[n:2 * n]
        send, recv = refs[2 * n], refs[2 * n + 1]
        x, y, c = lax.axis_index("x"), lax.axis_index("y"), lax.axis_index("c")
        me = 4 * x + 2 * y + c

        def copy(i, m, slot):
            peer = (x ^ ((m >> 2) & 1), y ^ ((m >> 1) & 1), c ^ (m & 1))
            return pltpu.make_async_remote_copy(
                src_ref=ins[i], dst_ref=outs[i].at[slot],
                send_sem=send.at[i * 7 + m - 1], recv_sem=recv.at[i * 7 + m - 1],
                device_id=peer, device_id_type=MESH)

        for i in range(n):
            outs[i][me] = ins[i][...]
            for m in range(1, 8):
                copy(i, m, me).start()
        for i in range(n):
            for m in range(1, 8):
                copy(i, m, me).wait_send()
                copy(i, m, me ^ m).wait_recv()

    vm = pl.BlockSpec(memory_space=pltpu.VMEM)
    return _pcall(
        kern, name=name, in_specs=[vm] * n, out_specs=[vm] * n,
        out_shape=[jax.ShapeDtypeStruct((8,) + a.shape, a.dtype) for a in arrs],
        scratch_shapes=[pltpu.SemaphoreType.DMA((7 * n,)), pltpu.SemaphoreType.DMA((7 * n,))],
        compiler_params=pltpu.CompilerParams(has_side_effects=True),
    )(*arrs)


def _exchange(name, arrs, plan):
    out_shape, scratch, phases = plan(arrs)

    def kern(*refs):
        for phase in phases(refs):
            phase()

    return _pcall(
        kern, name=name, in_specs=[ANY] * len(arrs), out_specs=[ANY] * len(out_shape),
        out_shape=out_shape, scratch_shapes=scratch,
        compiler_params=pltpu.CompilerParams(has_side_effects=True),
    )(*arrs)


def _ride(plan_and_arrs, n_in, n_out):
    if plan_and_arrs is None:
        return [], [], [], [], [], lambda refs, first, middle, last: ((lambda: None), (lambda: None))
    plan, arrs = plan_and_arrs
    out_shape, scratch, phases = plan(arrs)
    na, no = len(arrs), len(out_shape)

    def hook(refs, first, middle, last):
        mine = refs[n_in:n_in + na] + refs[n_in + na + n_out:]
        start, mid, finish = phases(mine)

        def before():
            pl.when(first)(start)
            pl.when(middle)(mid)

        def after():
            pl.when(last)(finish)

        return before, after

    return list(arrs), [ANY] * na, [ANY] * no, out_shape, scratch, hook


def _gather_plan(arrs):
    n = len(arrs)

    def phases(refs):
        ins, outs = refs[:n], refs[n:2 * n]
        s1, r1, s2, r2, loc = refs[2 * n:2 * n + 5]
        x, y, c = lax.axis_index("x"), lax.axis_index("y"), lax.axis_index("c")
        me = 2 * x + y

        def half(i, hc):
            hr = ins[i].shape[0] // 2
            return pl.ds(hc * hr, hr)

        def own(i):
            return pltpu.make_async_remote_copy(
                src_ref=ins[i], dst_ref=outs[i].at[me], send_sem=loc.at[i], recv_sem=loc.at[n + i],
                device_id=(x, y, 1 - c), device_id_type=MESH)

        def fetch(i, m, slot):
            px, py = x ^ ((m >> 1) & 1), y ^ (m & 1)
            return pltpu.make_async_remote_copy(
                src_ref=ins[i].at[half(i, c)], dst_ref=outs[i].at[slot, half(i, c)],
                send_sem=s1.at[i * 3 + m - 1], recv_sem=r1.at[i * 3 + m - 1],
                device_id=(px, py, c), device_id_type=MESH)

        def passed(i, m, hc):
            return pltpu.make_async_remote_copy(
                src_ref=outs[i].at[me ^ m, half(i, hc)], dst_ref=outs[i].at[me ^ m, half(i, hc)],
                send_sem=s2.at[i * 3 + m - 1], recv_sem=r2.at[i * 3 + m - 1],
                device_id=(x, y, 1 - c), device_id_type=MESH)

        def start():
            for i in range(n):
                for m in range(1, 4):
                    fetch(i, m, me).start()
            for i in range(n):
                own(i).start()

        def mid():
            for i in range(n):
                for m in range(1, 4):
                    fetch(i, m, me ^ m).wait_recv()
                    passed(i, m, c).start()

        def finish():
            for i in range(n):
                own(i).wait()
                for m in range(1, 4):
                    fetch(i, m, me).wait_send()
                    passed(i, m, c).wait_send()
                    passed(i, m, 1 - c).wait_recv()

        return start, mid, finish

    out_shape = [jax.ShapeDtypeStruct((4,) + a.shape, a.dtype) for a in arrs]
    scratch = [pltpu.SemaphoreType.DMA((3 * n,))] * 4 + [pltpu.SemaphoreType.DMA((2 * n,))]
    return out_shape, scratch, phases


def _pair_send_halves(name, arrs):
    n = len(arrs)

    def kern(*refs):
        ins, outs = refs[:n], refs[n:2 * n]
        send, recv = refs[2 * n], refs[2 * n + 1]
        x, y, c = lax.axis_index("x"), lax.axis_index("y"), lax.axis_index("c")

        def copy(i, k, hc):
            return pltpu.make_async_remote_copy(
                src_ref=ins[i].at[k, hc], dst_ref=outs[i].at[k],
                send_sem=send.at[i * 4 + k], recv_sem=recv.at[i * 4 + k],
                device_id=(x, y, 1 - c), device_id_type=MESH)

        for i in range(n):
            for k in range(4):
                copy(i, k, 1 - c).start()
        for i in range(n):
            for k in range(4):
                copy(i, k, 1 - c).wait()

    return _pcall(
        kern, name=name, in_specs=[ANY] * n, out_specs=[ANY] * n,
        out_shape=[jax.ShapeDtypeStruct((4,) + a.shape[2:], a.dtype) for a in arrs],
        scratch_shapes=[pltpu.SemaphoreType.DMA((4 * n,)), pltpu.SemaphoreType.DMA((4 * n,))],
        compiler_params=pltpu.CompilerParams(has_side_effects=True),
    )(*arrs)


def _scatter_plan(arrs):
    n = len(arrs)

    def phases(refs):
        ins, outs = refs[:n], refs[n:2 * n]
        send, recv = refs[2 * n], refs[2 * n + 1]
        x, y, c = lax.axis_index("x"), lax.axis_index("y"), lax.axis_index("c")
        me = 2 * x + y

        def copy(i, m, slot):
            px, py = x ^ ((m >> 1) & 1), y ^ (m & 1)
            return pltpu.make_async_remote_copy(
                src_ref=ins[i].at[2 * px + py], dst_ref=outs[i].at[slot],
                send_sem=send.at[i * 3 + m - 1], recv_sem=recv.at[i * 3 + m - 1],
                device_id=(px, py, c), device_id_type=MESH)

        def start():
            for i in range(n):
                for m in range(1, 4):
                    copy(i, m, me).start()

        def finish():
            for i in range(n):
                for m in range(1, 4):
                    copy(i, m, me).wait_send()
                    copy(i, m, me ^ m).wait_recv()

        return start, (lambda: None), finish

    out_shape = [jax.ShapeDtypeStruct(a.shape, a.dtype) for a in arrs]
    scratch = [pltpu.SemaphoreType.DMA((3 * n,)), pltpu.SemaphoreType.DMA((3 * n,))]
    return out_shape, scratch, phases


def _pair_swap(name, arrs):
    n = len(arrs)

    def kern(*refs):
        ins, outs = refs[:n], refs[n:2 * n]
        send, recv = refs[2 * n], refs[2 * n + 1]
        x, y, c = lax.axis_index("x"), lax.axis_index("y"), lax.axis_index("c")

        def copy(i):
            return pltpu.make_async_remote_copy(
                src_ref=ins[i], dst_ref=outs[i], send_sem=send.at[i], recv_sem=recv.at[i],
                device_id=(x, y, 1 - c), device_id_type=MESH)

        for i in range(n):
            copy(i).start()
        for i in range(n):
            copy(i).wait()

    return _pcall(
        kern, name=name, in_specs=[ANY] * n, out_specs=[ANY] * n,
        out_shape=[jax.ShapeDtypeStruct(a.shape, a.dtype) for a in arrs],
        scratch_shapes=[pltpu.SemaphoreType.DMA((n,)), pltpu.SemaphoreType.DMA((n,))],
        compiler_params=pltpu.CompilerParams(has_side_effects=True),
    )(*arrs)


def _row_tile(R):
    for t in (256, 128, 64, 32, 16, 8):
        if R % t == 0:
            return t
    return R


def _sum_slots(name, parts):
    K, R, C = parts.shape
    tr = _row_tile(R)

    def kern(p_ref, o_ref):
        acc = p_ref[0].astype(F32)
        for k in range(1, K):
            acc = acc + p_ref[k].astype(F32)
        o_ref[...] = acc

    return _pcall(
        kern, name=name, grid=(R // tr,),
        in_specs=[pl.BlockSpec((K, tr, C), lambda i: (0, i, 0))],
        out_specs=pl.BlockSpec((tr, C), lambda i: (i, 0)),
        out_shape=jax.ShapeDtypeStruct((R, C), F32),
        compiler_params=_cparams(("parallel",)),
    )(parts)


def _sum_pair(name, core, mine, theirs):
    K, _, hr, C = mine.shape
    tr = _row_tile(hr)

    def kern(c_ref, a_ref, b_ref, o_ref):
        o_ref[0] = (a_ref[0, 0].astype(F32) + b_ref[0].astype(F32)).astype(BF16)

    return _pcall(
        kern, name=name, out_shape=jax.ShapeDtypeStruct((K, hr, C), BF16),
        grid_spec=pltpu.PrefetchScalarGridSpec(
            num_scalar_prefetch=1, grid=(K, hr // tr),
            in_specs=[pl.BlockSpec((1, 1, tr, C), lambda k, r, c_ref: (k, c_ref[0], r, 0)),
                      pl.BlockSpec((1, tr, C), lambda k, r, c_ref: (k, r, 0))],
            out_specs=pl.BlockSpec((1, tr, C), lambda k, r, c_ref: (k, r, 0))),
        compiler_params=_cparams(("parallel", "parallel")),
    )(core, mine, theirs)


def _sum_chips(name, chip, own, recv):
    K, hr, C = own.shape
    tr = _row_tile(hr)

    def kern(chip_ref, own_ref, *rest):
        r_refs, o_ref = rest[:K], rest[K]
        me = chip_ref[0]
        mine = own_ref[0].astype(F32)
        acc = None
        for k in range(K):
            t = jnp.where(me == k, mine, r_refs[k][0].astype(F32))
            acc = t if acc is None else acc + t
        o_ref[...] = acc

    def other(k):
        return pl.BlockSpec((1, tr, C), lambda r, s: (jnp.where(s[0] == k, (k + 1) % K, k), r, 0))

    return _pcall(
        kern, name=name, out_shape=jax.ShapeDtypeStruct((hr, C), F32),
        grid_spec=pltpu.PrefetchScalarGridSpec(
            num_scalar_prefetch=1, grid=(hr // tr,),
            in_specs=[pl.BlockSpec((1, tr, C), lambda r, s: (s[0], r, 0))] + [other(k) for k in range(K)],
            out_specs=pl.BlockSpec((tr, C), lambda r, s: (r, 0))),
        compiler_params=_cparams(("parallel",)),
    )(chip, own, *([recv] * K))


def _adam_update(w, m, v, g):
    c1 = 1.0 - ADAM_B1 ** ADAM_STEP
    c2 = 1.0 - ADAM_B2 ** ADAM_STEP
    mn = ADAM_B1 * m + (1.0 - ADAM_B1) * g
    vn = ADAM_B2 * v + (1.0 - ADAM_B2) * (g * g)
    return -ADAM_LR * ((mn / c1) / (jnp.sqrt(vn / c2) + ADAM_EPS) + ADAM_WD * w), mn, vn


def _adamw_halves(name, core, w, m, v, mine, theirs):
    _, R, C = w.shape
    hr = mine.shape[0]
    tr = _row_tile(hr)
    nbh = hr // tr

    def kern(c_ref, w_ref, m_ref, v_ref, a_ref, b_ref, go_ref, d_ref, mo_ref, vo_ref):
        g = jnp.where(pl.program_id(0) // nbh == c_ref[0], a_ref[...], b_ref[...])
        d, mn, vn = _adam_update(w_ref[...], m_ref[...], v_ref[...], g)
        go_ref[...] = g
        d_ref[...] = d
        mo_ref[...] = mn
        vo_ref[...] = vn

    spec = pl.BlockSpec((None, tr, C), lambda i, s: (0, i, 0))
    hspec = pl.BlockSpec((tr, C), lambda i, s: (i % nbh, 0))
    return _pcall(
        kern, name=name, out_shape=[jax.ShapeDtypeStruct((1, R, C), F32)] * 4,
        grid_spec=pltpu.PrefetchScalarGridSpec(
            num_scalar_prefetch=1, grid=(R // tr,),
            in_specs=[spec, spec, spec, hspec, hspec], out_specs=[spec] * 4),
        compiler_params=_cparams(("parallel",)),
    )(core, w, m, v, mine, theirs)


def _adamw(name, w, m, v, gparts):
    R, C = w.shape
    K = gparts.shape[0]
    tr = _row_tile(R)

    def kern(w_ref, m_ref, v_ref, g_ref, go_ref, d_ref, mo_ref, vo_ref):
        g = g_ref[0]
        for k in range(1, K):
            g = g + g_ref[k]
        d, mn, vn = _adam_update(w_ref[...], m_ref[...], v_ref[...], g)
        go_ref[...] = g
        d_ref[...] = d
        mo_ref[...] = mn
        vo_ref[...] = vn

    spec = pl.BlockSpec((tr, C), lambda i: (i, 0))
    return _pcall(
        kern, name=name, grid=(R // tr,),
        in_specs=[spec, spec, spec, pl.BlockSpec((K, tr, C), lambda i: (0, i, 0))],
        out_specs=[spec] * 4,
        out_shape=[jax.ShapeDtypeStruct((R, C), F32)] * 4,
        compiler_params=_cparams(("parallel",)),
    )(w, m, v, gparts)


def _round_up(a, b):
    return (a + b - 1) // b * b


def kernel(x, c, w_ada, b_ada, norm1_g, w_in, b_forget, q_norm_g, k_norm_g, w_attn_proj, conv_w, conv_b, conv_ln_g, conv_ln_b, w_conv_proj, w_out, norm2_g, w_mlp1, w_mlp2, loss_target, m_w_ada, m_b_ada, m_norm1_g, m_w_in, m_b_forget, m_q_norm_g, m_k_norm_g, m_w_attn_proj, m_conv_w, m_conv_b, m_conv_ln_g, m_conv_ln_b, m_w_conv_proj, m_w_out, m_norm2_g, m_w_mlp1, m_w_mlp2, v_w_ada, v_b_ada, v_norm1_g, v_w_in, v_b_forget, v_q_norm_g, v_k_norm_g, v_w_attn_proj, v_conv_w, v_conv_b, v_conv_ln_g, v_conv_ln_b, v_w_conv_proj, v_w_out, v_norm2_g, v_w_mlp1, v_w_mlp2):
    S, D = x.shape[1], x.shape[2]
    NH, HD = b_forget.shape[-1], q_norm_g.shape[-1]
    TAPS = conv_w.shape[1]
    DIN_S = w_in.shape[-1]
    DIN = 4 * DIN_S
    DFF_S = w_mlp1.shape[-1]
    DFF = 4 * DFF_S
    ADA_S = w_ada.shape[-1]
    DS = w_attn_proj.shape[1]
    CS = conv_w.shape[-1]
    assert NH * HD == D and DIN == 7 * D + NH and TAPS - 1 <= HALO and D % LANES == 0 and 2 * HD == LANES
    NP = _round_up(7 * D + LANES, 512)
    TQ = min(512, S)
    NQ = S // TQ
    FCOL = 7 * D // LANES

    xi, yi, ci = lax.axis_index("x"), lax.axis_index("y"), lax.axis_index("c")
    chip = 2 * xi + yi
    dev = 4 * xi + 2 * yi + ci

    x2 = x.reshape(S, D)
    tgt = loss_target.reshape(S, D)

    lane_head = jnp.arange(D, dtype=jnp.int32) // HD
    grp = (lane_head[:, None] == jnp.arange(LANES, dtype=jnp.int32)[None, :]).astype(BF16)
    grp_t = grp.T
    sel = ((jnp.arange(D, dtype=jnp.int32)[:, None] == HD * jnp.arange(LANES, dtype=jnp.int32)[None, :])
           .astype(BF16))
    ch = min(256, S)
    ii = jnp.arange(ch, dtype=jnp.int32)
    tri = (ii[None, :] <= ii[:, None]).astype(BF16)
    tri_u = tri.T
    gq_t = jnp.tile(q_norm_g.reshape(1, HD), (1, NH))
    gk_t = jnp.tile(k_norm_g.reshape(1, HD), (1, NH))
    bf_pad = jnp.pad(b_forget.reshape(1, NH), ((0, 0), (0, LANES - NH)))

    c_all, cw_all = _ag_small(
        "ag_c_convw", [c.reshape(1, D), jnp.pad(conv_w.reshape(TAPS, CS), ((0, HALO - TAPS), (0, 0)))])
    c_all = c_all.reshape(8, D)
    b_part = lax.dynamic_slice(b_ada.reshape(1, -1), (0, chip * ADA_S), (1, ADA_S))
    mod_part = _ada_fwd(c_all, w_ada.reshape(D, ADA_S), b_part)
    (mod_all,) = _ag_small("ag_mod", [mod_part])
    mod_full = jnp.concatenate([mod_all[0], mod_all[2], mod_all[4], mod_all[6]], axis=1)
    mod = lax.dynamic_slice(mod_full, (dev, 0), (1, 6 * D))
    sh1, sc1, g1, sh2, sc2, g2 = [mod[:, i * D:(i + 1) * D] for i in range(6)]

    shards = [w_in.reshape(D, DIN_S), w_attn_proj.reshape(DS, D), w_conv_proj.reshape(DS, D),
              w_out.reshape(DS, D), w_mlp1.reshape(D, DFF_S), w_mlp2.reshape(DFF_S, D)]
    shards = [s.astype(BF16) for s in shards]
    (gw_in,) = _exchange("ag_w_in", shards[:1], _gather_plan)
    w_conv = jnp.concatenate([cw_all[0], cw_all[2], cw_all[4], cw_all[6]], axis=1)

    w_in_full = jnp.concatenate([gw_in[k] for k in range(4)], axis=1)
    w_in_p = jnp.concatenate(
        [w_in_full[:, :3 * D], w_in_full[:, 3 * D + NH:], w_in_full[:, 3 * D:3 * D + NH],
         jnp.zeros((D, NP - 7 * D - NH), BF16)], axis=1)

    n1g = norm1_g.reshape(1, D)
    n2g = norm2_g.reshape(1, D)
    h = _norm_mod("norm_mod1", x2, n1g, sc1, sh1, S, D)
    proj = _mm("mm_in", h, w_in_p, "nn", [F32])
    qs, kn, vb = _qk_prep(proj, gq_t, gk_t, grp, grp_t, S, D, HD)
    f_cum = _fgate_fwd(proj, FCOL, bf_pad, tri, S)
    fk_c = f_cum[:, :NH]
    fk_r = fk_c.T.reshape(NH, NQ, 1, TQ)
    o, o32, lse_b, gw_ap, gw_cp, gw_out, gw_m1, gw_m2 = _flash_fwd(
        qs, kn, vb, fk_r, S, D, HD, TQ, ride=(_gather_plan, shards[1:]))
    w_ap = gw_ap.reshape(D, D)
    w_cp = gw_cp.reshape(D, D)
    w_o = gw_out.reshape(D, D)
    w_m1 = jnp.transpose(gw_m1, (1, 0, 2)).reshape(D, DFF)
    w_m2 = gw_m2.reshape(DFF, D)
    br_a = _mm("mm_attn_proj", o, w_ap, "nn", [F32])
    cb, clg, clb = conv_b.reshape(1, D), conv_ln_g.reshape(1, D), conv_ln_b.reshape(1, D)
    u1, u3 = _conv_fwd(proj, 3, 4, w_conv, cb, clg, clb, S, D, TAPS, 256)
    br_b = _mm("mm_conv_proj", u3, w_cp, "nn", [F32])
    merged = _gate_merge(proj, 5, 6, br_a, br_b, S, D)
    mo = _mm("mm_out", merged, w_o, "nn", [F32])
    x1, h2 = _resid_norm2(x2, mo, g1, n2g, sc2, sh2, S, D)

    def relu2(r):
        rp = jnp.maximum(r, 0.0)
        return (rp * rp,)
    z = _mm("mm_mlp1", h2, w_m1, "nn", [BF16], epi=relu2)
    ml = _mm("mm_mlp2", z, w_m2, "nn", [F32])
    dy, dml, sq, dg2 = _loss_dy(x1, ml, tgt, g2, S, D)
    loss = lax.psum(0.5 * jnp.sum(sq) / D, ("x", "y", "c"))

    da = _mm("mm_dz", dml, w_m2, "nt", [BF16], epi=lambda r, zz: (r * 2.0 * jnp.sqrt(zz.astype(F32)),),
             extras=(z,))
    dw_m2 = _mm("mm_dw_mlp2", z, dml, "tn", [BF16])
    dw_m1 = _mm("mm_dw_mlp1", h2, da, "tn", [BF16])
    dh2 = _mm("mm_dh2", da, w_m1, "nt", [F32])
    dx1, dmo, dsh2, dsc2, dn2g, dg1 = _norm_bwd("norm2_bwd", x1, dh2, dy, n2g, sc2, S, D, extra=(mo, g1))
    dmerged = _mm("mm_dmerged", dmo, w_o, "nt", [F32])
    dw_o = _mm("mm_dw_out", merged, dmo, "tn", [BF16])
    dba, dbb, dga, dgb = _gate_bwd(dmerged, proj, 5, 6, br_a, br_b, S, D)
    do = _mm("mm_do", dba, w_ap, "nt", [BF16])
    dw_ap = _mm("mm_dw_attn_proj", o, dba, "tn", [BF16])
    du3 = _mm("mm_du3", dbb, w_cp, "nt", [F32])
    dw_cp = _mm("mm_dw_conv_proj", u3, dbb, "tn", [BF16])
    dglu_a, dglu_b, dcw, dcb, dclg, dclb = _conv_bwd(du3, u1, proj, 3, 4, w_conv, clg, clb, S, D, TAPS, 256)

    core = ci.astype(jnp.int32).reshape(1)
    chip1 = chip.astype(jnp.int32).reshape(1)
    halves = lambda p: p.astype(BF16).reshape(4, 2, p.shape[1] // 2, p.shape[2])
    names = ["w_in", "w_attn_proj", "w_conv_proj", "w_out", "w_mlp1", "w_mlp2"]
    parts = [halves(p) for p in (dw_ap.reshape(4, DS, D), dw_cp.reshape(4, DS, D), dw_o.reshape(4, DS, D),
                                 jnp.transpose(dw_m1.reshape(D, 4, DFF_S), (1, 0, 2)), dw_m2.reshape(4, DFF_S, D))]
    theirs = _pair_send_halves("rs_pair", parts)
    chip_parts = [_sum_pair("sum_pair_" + nm, core, p, t) for nm, p, t in zip(names[1:], parts, theirs)]

    delta_c, lse_c = _delta_prep(do, o32, lse_b, grp, sel, S, D)
    to_rows = lambda t: t[:, :NH].T.reshape(NH, NQ, 1, TQ)
    dkn, dv, dqs, dfq_r, dfk_b, *recvd = _flash_bwd(
        qs, kn, vb, do, f_cum, to_rows(lse_c), to_rows(delta_c), S, D, HD, TQ, ride=(_scatter_plan, chip_parts))
    dq, dk, sq_q, sq_k = _qk_bwd(proj, dqs, dkn, gq_t, gk_t, grp, grp_t, S, D, HD)
    to_cols = lambda r: jnp.pad(r.reshape(NH, S).T, ((0, 0), (0, LANES - NH)))
    dfq_pad = to_cols(dfq_r)
    dfk_pad = jnp.pad(dfk_b[:, ::HD], ((0, 0), (0, LANES - NH)))
    df, dbf = _fgate_bwd(dfk_pad, dfq_pad, proj, FCOL, bf_pad, tri_u, NH, S)
    dproj = jnp.concatenate(
        [dq, dk, dv, dglu_a, dglu_b, dga, dgb, df, jnp.zeros((S, NP - 7 * D - LANES), BF16)], axis=1)
    dw_in_p = _mm("mm_dw_in", h, dproj, "tn", [BF16])
    dw_in_full = jnp.concatenate(
        [dw_in_p[:, :3 * D], dw_in_p[:, 7 * D:7 * D + NH], dw_in_p[:, 3 * D:7 * D]], axis=1)
    part_in = halves(jnp.transpose(dw_in_full.reshape(D, 4, DIN_S), (1, 0, 2)))
    (their_in,) = _pair_send_halves("rs_pair_w_in", [part_in])
    chip_in = _sum_pair("sum_pair_w_in", core, part_in, their_in)
    dh, recv_in = _mm("mm_dh", dproj, w_in_p, "nt", [F32], ride=(_scatter_plan, [chip_in]))
    gx, dsh1, dsc1, dn1g = _norm_bwd("norm1_bwd", x2, dh, dx1, n1g, sc1, S, D)

    packed = jnp.concatenate([dsh1, dsc1, dg1, dsh2, dsc2, dg2, dn1g, dcb, dclg, dclb, dn2g,
                              sq_q, sq_k, dbf], axis=1)
    small_all, dcw_all = _ag_small("ag_small_grads", [packed, dcw])
    small = _sum_slots("sum_small", small_all.reshape(8, 1, -1)).reshape(1, -1)
    dmod_sum = small[:, :6 * D]
    seg = lambda k: small[:, (6 + k) * D:(7 + k) * D]
    g_n1g, g_cb, g_clg, g_clb, g_n2g = seg(0), seg(1), seg(2), seg(3), seg(4)
    g_qn = _sum_slots("sum_qn", seg(5).reshape(NH, 1, HD))
    g_kn = _sum_slots("sum_kn", seg(6).reshape(NH, 1, HD))
    g_bf = small[:, 13 * D:13 * D + NH]
    dcw_mine = lax.dynamic_slice(dcw_all[:, :TAPS, :], (0, 0, chip * CS), (8, TAPS, CS))

    dmod_all = small_all.reshape(8, -1)[:, :6 * D]
    dmod_cols = lax.dynamic_slice(dmod_all, (0, chip * ADA_S), (8, ADA_S))
    c_t_pad = jnp.pad(c_all.T, ((0, 0), (0, LANES - 8)))
    g_wada = _ada_wgrad(c_t_pad, jnp.pad(dmod_cols, ((0, LANES - 8), (0, 0))))

    sums =[_sum_chips("sum_" + nm, chip1, p, r)
            for nm, p, r in zip(names, [chip_in] + chip_parts, [recv_in] + list(recvd))]
    others = _pair_swap("pair_grads", sums)

    res = {}
    big = {nm: (a, b) for nm, a, b in zip(names, sums, others)}
    big_w = {"w_in": (w_in, m_w_in, v_w_in), "w_attn_proj": (w_attn_proj, m_w_attn_proj, v_w_attn_proj),
             "w_conv_proj": (w_conv_proj, m_w_conv_proj, v_w_conv_proj), "w_out": (w_out, m_w_out, v_w_out),
             "w_mlp1": (w_mlp1, m_w_mlp1, v_w_mlp1), "w_mlp2": (w_mlp2, m_w_mlp2, v_w_mlp2)}
    for nm in names:
        res[nm] = _adamw_halves("adamw_" + nm, core, *big_w[nm], *big[nm])
    outs = _adamw("adamw_w_ada", w_ada.reshape(D, ADA_S), m_w_ada.reshape(D, ADA_S),
                  v_w_ada.reshape(D, ADA_S), g_wada.reshape(1, D, ADA_S))
    res["w_ada"] = [t.reshape(w_ada.shape) for t in outs]
    outs = _adamw("adamw_conv_w", conv_w.reshape(TAPS, CS), m_conv_w.reshape(TAPS, CS),
                  v_conv_w.reshape(TAPS, CS), dcw_mine)
    res["conv_w"] = [t.reshape(conv_w.shape) for t in outs]

    small_w = [("b_ada", b_ada, m_b_ada, v_b_ada, dmod_sum), ("norm1_g", norm1_g, m_norm1_g, v_norm1_g, g_n1g),
               ("b_forget", b_forget, m_b_forget, v_b_forget, g_bf),
               ("q_norm_g", q_norm_g, m_q_norm_g, v_q_norm_g, g_qn),
               ("k_norm_g", k_norm_g, m_k_norm_g, v_k_norm_g, g_kn),
               ("conv_b", conv_b, m_conv_b, v_conv_b, g_cb), ("conv_ln_g", conv_ln_g, m_conv_ln_g, v_conv_ln_g, g_clg),
               ("conv_ln_b", conv_ln_b, m_conv_ln_b, v_conv_ln_b, g_clb),
               ("norm2_g", norm2_g, m_norm2_g, v_norm2_g, g_n2g)]
    cat = lambda ts: jnp.concatenate([t.reshape(1, -1) for t in ts], axis=1)
    outs = _adamw("adamw_small", cat([t[1] for t in small_w]), cat([t[2] for t in small_w]),
                  cat([t[3] for t in small_w]), cat([t[4] for t in small_w]).reshape(1, 1, -1))
    off = 0
    for nm, w_, _, _, _ in small_w:
        n = w_.size
        res[nm] = [t[:, off:off + n].reshape(w_.shape) for t in outs]
        off += n

    order = ["w_ada", "b_ada", "norm1_g", "w_in", "b_forget", "q_norm_g", "k_norm_g", "w_attn_proj", "conv_w",
             "conv_b", "conv_ln_g", "conv_ln_b", "w_conv_proj", "w_out", "norm2_g", "w_mlp1", "w_mlp2"]
    return (loss, gx.reshape(x.shape), *[res[n][0] for n in order], *[res[n][1] for n in order],
            *[res[n][2] for n in order], *[res[n][3] for n in order])
```

```python
import functools

import jax
import jax.numpy as jnp
from jax import lax
from jax.experimental import pallas as pl
from jax.experimental.pallas import tpu as pltpu

F32 = jnp.float32
BF16 = jnp.bfloat16
MESH = pl.DeviceIdType.MESH
ANY = pl.BlockSpec(memory_space=pl.ANY)

NORM_EPS = 1e-6
ADAM_LR = 0.001
ADAM_B1 = 0.9
ADAM_B2 = 0.999
ADAM_EPS = 1e-08
ADAM_WD = 0.01
ADAM_STEP = 10
LANES = 128
SUBLANES = 8
HALO = 32
CONV_ROWS = 32
CONV_TAPS = 4
NEG = -1e30
VMEM_LIMIT = 56 * 1024 * 1024


def _pcall(body, **kw):
    return pl.pallas_call(body, **kw)


def _cparams(sem=None):
    if sem is None:
        return pltpu.CompilerParams(vmem_limit_bytes=VMEM_LIMIT)
    return pltpu.CompilerParams(dimension_semantics=sem, vmem_limit_bytes=VMEM_LIMIT)


def _sig(x):
    return 1.0 / (1.0 + jnp.exp(-x))


def _split3(x):
    x1 = x.astype(BF16)
    r = x - x1.astype(F32)
    x2 = r.astype(BF16)
    x3 = (r - x2.astype(F32)).astype(BF16)
    return x1, x2, x3


def _dot_rs(x, e):
    out = None
    for t in _split3(x):
        d = jnp.dot(t, e, preferred_element_type=F32)
        out = d if out is None else out + d
    return out


def _dot_ls(e, x):
    out = None
    for t in _split3(x):
        d = jnp.dot(e, t, preferred_element_type=F32)
        out = d if out is None else out + d
    return out


def _tile(n, want):
    if n <= want:
        return n
    t = want - want % LANES
    while n % t:
        t -= LANES
    assert t > 0, (n, want)
    return t


_DIMS = {"nn": ((1,), (0,)), "nt": ((1,), (1,)), "tn": ((0,), (0,))}


def _mm(name, a, b, mode, out_dtypes, epi=None, extras=(), tm=1024, tn=1024, tk=4096, ride=None):
    if mode == "nn":
        (M, K), (_, N) = a.shape, b.shape
    elif mode == "nt":
        (M, K), (N, _) = a.shape, b.shape
    else:
        (K, M), (_, N) = a.shape, b.shape
    tm, tn, tk = _tile(M, tm), _tile(N, tn), _tile(K, tk)
    nm, nn, nk = M // tm, N // tn, K // tk
    ne, no = len(extras), len(out_dtypes)
    dims = (_DIMS[mode], ((), ()))
    r_in, r_ispec, r_ospec, r_oshape, r_scratch, r_hook = _ride(ride, 2 + ne, no)

    def kern(*refs):
        a_ref, b_ref = refs[0], refs[1]
        e_refs = refs[2:2 + ne]
        o_refs = refs[2 + ne + len(r_in):2 + ne + len(r_in) + no]
        i, j, k = pl.program_id(0), pl.program_id(1), pl.program_id(2)
        before, after = r_hook(refs, (i == 0) & (j == 0) & (k == 0), (i == nm // 2) & (j == 0) & (k == 0),
                               (i == nm - 1) & (j == nn - 1) & (k == nk - 1))
        before()
        d = lax.dot_general(a_ref[...], b_ref[...], dims, preferred_element_type=F32)

        def finish(r):
            outs = (r,) if epi is None else epi(r, *[e[...] for e in e_refs])
            for o_ref, o in zip(o_refs, outs):
                o_ref[...] = o.astype(o_ref.dtype)

        if nk == 1:
            finish(d)
        else:
            acc = refs[-1]

            @pl.when(k == 0)
            def _():
                acc[...] = d

            @pl.when((k > 0) & (k < nk - 1))
            def _():
                acc[...] += d

            @pl.when(k == nk - 1)
            def _():
                finish(acc[...] + d)
        after()

    if mode == "tn":
        a_spec = pl.BlockSpec((tk, tm), lambda i, j, k: (k, i))
    else:
        a_spec = pl.BlockSpec((tm, tk), lambda i, j, k: (i, k))
    if mode == "nt":
        b_spec = pl.BlockSpec((tn, tk), lambda i, j, k: (j, k))
    else:
        b_spec = pl.BlockSpec((tk, tn), lambda i, j, k: (k, j))
    mn_spec = pl.BlockSpec((tm, tn), lambda i, j, k: (i, j))
    outs = _pcall(
        kern, name=name, grid=(nm, nn, nk),
        in_specs=[a_spec, b_spec] + [mn_spec] * ne + r_ispec,
        out_specs=[mn_spec] * no + r_ospec,
        out_shape=[jax.ShapeDtypeStruct((M, N), dt) for dt in out_dtypes] + r_oshape,
        scratch_shapes=r_scratch + ([pltpu.VMEM((tm, tn), F32)] if nk > 1 else []),
        compiler_params=_cparams(("arbitrary",) * 3 if ride else ("parallel", "parallel", "arbitrary")),
    )(a, b, *extras, *r_in)
    return outs[0] if len(outs) == 1 else outs


def _rowcall(name, body, S, ts, row_ins, vec_ins, row_outs, vec_outs, into=None):
    ts = min(ts, S)
    nri, nvi, nro, nvo = len(row_ins), len(vec_ins), len(row_outs), len(vec_outs)
    na = 0 if into is None else 1

    def kern(*refs):
        ins = refs[:nri + nvi]
        outs = refs[nri + nvi + na:]
        if nvo:
            @pl.when(pl.program_id(0) == 0)
            def _():
                for r in outs[nro:]:
                    r[...] = jnp.zeros(r.shape, r.dtype)
        body(*ins, *outs)

    in_specs = [pl.BlockSpec((ts, w), functools.partial(lambda i, cb: (i, cb), cb=cb))
                for (_, w, cb) in row_ins]
    in_specs += [pl.BlockSpec(v.shape, lambda i: (0, 0)) for v in vec_ins]
    out_specs = [pl.BlockSpec((ts, w), lambda i: (i, 0)) for (w, _) in row_outs]
    out_specs += [pl.BlockSpec((r, w), lambda i: (0, 0)) for (r, w) in vec_outs]
    out_shape = [jax.ShapeDtypeStruct((S, w), dt) for (w, dt) in row_outs]
    out_shape += [jax.ShapeDtypeStruct((r, w), F32) for (r, w) in vec_outs]
    extra, aliases = [], {}
    if into is not None:
        buf, cb = into
        assert buf.dtype == row_outs[0][1] and buf.shape[0] == S
        in_specs.append(ANY)
        out_specs[0] = pl.BlockSpec((ts, row_outs[0][0]), lambda i: (i, cb))
        out_shape[0] = jax.ShapeDtypeStruct(buf.shape, buf.dtype)
        extra, aliases = [buf], {nri + nvi: 0}
    return _pcall(
        kern, name=name, grid=(S // ts,), in_specs=in_specs, out_specs=out_specs,
        out_shape=out_shape, input_output_aliases=aliases,
        compiler_params=_cparams(("arbitrary",) if nvo else ("parallel",)),
    )(*[a for (a, _, _) in row_ins], *vec_ins, *extra)


def _csum(x):
    return jnp.sum(x, axis=0, keepdims=True)


def _norm_mod(name, x, g, sc, sh, S, D):
    def body(x_ref, g_ref, sc_ref, sh_ref, h_ref):
        xv = x_ref[...]
        r = lax.rsqrt(jnp.mean(xv * xv, axis=-1, keepdims=True) + NORM_EPS)
        h_ref[...] = ((xv * r * g_ref[...]) * (1.0 + sc_ref[...]) + sh_ref[...]).astype(BF16)
    return _rowcall(name, body, S, 512, [(x, D, 0)], [g, sc, sh], [(D, BF16)], [])[0]


def _head_rstd(v, grp, grp_t, hd):
    ss = _dot_rs(v * v, grp) * (1.0 / hd)
    r = lax.rsqrt(ss + NORM_EPS)
    return _dot_rs(r, grp_t)


def _qk_prep(proj, vcol, gq, gk, grp, grp_t, S, D, hd):
    scale = hd ** -0.5

    def body(q_ref, k_ref, v_ref, gq_ref, gk_ref, g_ref, gt_ref, qs_ref, kn_ref, vb_ref):
        q = q_ref[...]
        k = k_ref[...]
        rq = _head_rstd(q, g_ref[...], gt_ref[...], hd)
        rk = _head_rstd(k, g_ref[...], gt_ref[...], hd)
        qs_ref[...] = ((q * rq * gq_ref[...]).astype(BF16).astype(F32) * scale).astype(BF16)
        kn_ref[...] = (k * rk * gk_ref[...]).astype(BF16)
        vb_ref[...] = v_ref[...].astype(BF16)

    return _rowcall("qk_prep", body, S, 256, [(proj, D, 0), (proj, D, 1), (proj, D, vcol)],
                    [gq, gk, grp, grp_t], [(D, BF16)] * 3, [])


def _fgate_fwd(proj, fcol, bf_pad, tri, S):
    ch = tri.shape[0]

    def body(f_ref, b_ref, tri_ref, out_ref):
        carry = jnp.zeros((1, LANES), F32)
        for c in range(S // ch):
            z = f_ref[c * ch:(c + 1) * ch, :] + b_ref[...]
            lf = jnp.minimum(z, 0.0) - jnp.log(1.0 + jnp.exp(-jnp.abs(z)))
            out_ref[c * ch:(c + 1) * ch, :] = _dot_ls(tri_ref[...], lf) + carry
            carry = carry + _csum(lf)

    return _rowcall("fgate_fwd", body, S, S, [(proj, LANES, fcol)], [bf_pad, tri],
                    [(LANES, F32)], [])[0]


def _fgate_bwd(dfk, dfq, proj, fcol, bf_pad, tri_u, nh, S, fw, into):
    ch = tri_u.shape[0]

    def body(d_ref, dq_ref, f_ref, b_ref, tri_ref, df_ref, db_ref):
        if fw > LANES:
            df_ref[:, LANES:fw] = jnp.zeros((S, fw - LANES), BF16)
        lane = lax.broadcasted_iota(jnp.int32, (ch, LANES), 1)
        carry = jnp.zeros((1, LANES), F32)
        tot = jnp.zeros((1, LANES), F32)
        for c in reversed(range(S // ch)):
            d = d_ref[c * ch:(c + 1) * ch, :] + dq_ref[c * ch:(c + 1) * ch, :]
            rc = _dot_ls(tri_ref[...], d) + carry
            carry = carry + _csum(d)
            z = f_ref[c * ch:(c + 1) * ch, :] + b_ref[...]
            df = jnp.where(lane < nh, rc * _sig(-z), 0.0)
            df_ref[c * ch:(c + 1) * ch, 0:LANES] = df.astype(BF16)
            tot = tot + _csum(df)
        db_ref[...] += tot

    return _rowcall("fgate_bwd", body, S, S, [(dfk, LANES, 0), (dfq, LANES, 0), (proj, LANES, fcol)],
                    [bf_pad, tri_u], [(fw, BF16)], [(1, LANES)], into=into)


def _keep(v, mask):
    return jnp.where(mask, v.astype(F32), 0.0).astype(BF16)


def _lane_col(blk, lane, at):
    return jnp.sum(jnp.where(lane == at, blk, 0.0), axis=-1, keepdims=True)


def _flash_fwd(qs, kn, vb, fk_r, S, D, hd, tq, ride=None):
    hp, nq = D // LANES, S // tq
    r_in, r_ispec, r_ospec, r_oshape, r_scratch, r_hook = _ride(ride, 4, 3)

    def kern(*refs):
        q_ref, k_ref, v_ref, fk_ref = refs[:4]
        o_ref, o32_ref, lse_ref = refs[4 + len(r_in):7 + len(r_in)]
        hi, qi = pl.program_id(0), pl.program_id(1)
        before, after = r_hook(refs, (hi == 0) & (qi == 0), (hi == hp // 2) & (qi == 0),
                               (hi == hp - 1) & (qi == nq - 1))
        before()
        lane = lax.broadcasted_iota(jnp.int32, (tq, LANES), 1)
        row = lax.broadcasted_iota(jnp.int32, (tq, tq), 0)
        col = lax.broadcasted_iota(jnp.int32, (tq, tq), 1)
        hms = [(lane >= j * hd) & (lane < (j + 1) * hd) for j in range(2)]
        q = q_ref[...]
        qms = [_keep(q, hm) for hm in hms]

        def step(ki, state, masked):
            off = pl.multiple_of(ki * tq, tq)
            k = k_ref[pl.ds(off, tq), :]
            v = v_ref[pl.ds(off, tq), :].astype(F32)
            new = []
            for j in range(2):
                m_old, acc = state[j]
                s = lax.dot_general(qms[j], k, (((1,), (1,)), ((), ())), preferred_element_type=F32)
                s = s - fk_ref[j, ki]
                if masked:
                    s = jnp.where(col <= row, s, NEG)
                m_new = jnp.maximum(m_old, jnp.max(s, axis=-1, keepdims=True))
                alpha = jnp.exp(m_old - m_new)
                p = jnp.exp(s - m_new)
                v1 = jnp.where(hms[j], v, 1.0).astype(BF16)
                acc = alpha * acc + jnp.dot(p.astype(BF16), v1, preferred_element_type=F32)
                new.append((m_new, acc))
            return tuple(new)

        init = tuple((jnp.full((tq, 1), NEG, F32), jnp.zeros((tq, LANES), F32)) for _ in range(2))
        state = lax.fori_loop(0, qi, lambda ki, st: step(ki, st, False), init)
        (m0, a0), (m1, a1) = step(qi, state, True)
        l0, l1 = pltpu.roll(a0, hd, 1), pltpu.roll(a1, hd, 1)
        first = lane < hd
        ov = jnp.where(first, a0 / l0, a1 / l1)
        o_ref[...] = ov.astype(BF16)
        o32_ref[...] = ov
        lse_ref[...] = jnp.where(first, m0 + jnp.log(l0), m1 + jnp.log(l1))
        after()

    qspec = pl.BlockSpec((tq, LANES), lambda h, i: (i, h))
    fullspec = pl.BlockSpec((S, LANES), lambda h, i: (0, h))
    return _pcall(
        kern, name="flash_fwd", grid=(hp, nq),
        in_specs=[qspec, fullspec, fullspec,
                  pl.BlockSpec((2, nq, 1, tq), lambda h, i: (h, 0, 0, 0))] + r_ispec,
        out_specs=[qspec, qspec, qspec] + r_ospec,
        out_shape=[jax.ShapeDtypeStruct((S, D), BF16), jax.ShapeDtypeStruct((S, D), F32),
                   jax.ShapeDtypeStruct((S, D), F32)] + r_oshape,
        scratch_shapes=r_scratch,
        compiler_params=_cparams(("arbitrary", "arbitrary")),
    )(qs, kn, vb, fk_r, *r_in)


def _flash_bwd(qs, kn, vb, do, fk_b, lse_r, delta_r, S, D, hd, tq, dv_into, ride=None):
    hp, nq = D // LANES, S // tq
    dbuf_hbm, dv_col = dv_into
    r_in, r_ispec, r_ospec, r_oshape, r_scratch, r_hook = _ride(ride, 8, 5)

    def kern(*refs):
        q_ref, do_ref, k_ref, v_ref, fk_ref, lse_ref, dl_ref = refs[:7]
        dk_ref, dv_ref, dq_ref, dfq_ref, dfk_ref = refs[8 + len(r_in):13 + len(r_in)]
        hi, ki = pl.program_id(0), pl.program_id(1)
        before, after = r_hook(refs, (hi == 0) & (ki == 0), (hi == hp // 2) & (ki == 0),
                               (hi == hp - 1) & (ki == nq - 1))
        before()
        lane = lax.broadcasted_iota(jnp.int32, (tq, LANES), 1)
        row = lax.broadcasted_iota(jnp.int32, (tq, tq), 0)
        col = lax.broadcasted_iota(jnp.int32, (tq, tq), 1)
        hms = [(lane >= j * hd) & (lane < (j + 1) * hd) for j in range(2)]
        k = k_ref[...]
        v = v_ref[...]
        fkb = fk_ref[...]
        kms = [_keep(k, hm) for hm in hms]
        vms = [_keep(v, hm) for hm in hms]
        fks = [_lane_col(fkb, lane, 2 * hi + j) for j in range(2)]

        @pl.when(ki == 0)
        def _():
            dfq_ref[...] = jnp.zeros(dfq_ref.shape, F32)
            dq_ref[...] = jnp.zeros(dq_ref.shape, F32)

        def step(qi, acc, masked):
            dk, dv, dfs = acc
            off = pl.multiple_of(qi * tq, tq)
            q = q_ref[pl.ds(off, tq), :]
            g = do_ref[pl.ds(off, tq), :]
            dq = None
            new_dfs = []
            for j in range(2):
                qm = _keep(q, hms[j])
                gm = _keep(g, hms[j])
                st = lax.dot_general(kms[j], q, (((1,), (1,)), ((), ())), preferred_element_type=F32)
                st = st - fks[j]
                if masked:
                    st = jnp.where(row <= col, st, NEG)
                pt = jnp.exp(st - lse_ref[j, qi])
                dv = dv + jnp.dot(pt.astype(BF16), gm, preferred_element_type=F32)
                dpt = lax.dot_general(vms[j], g, (((1,), (1,)), ((), ())), preferred_element_type=F32)
                dst = pt * (dpt - dl_ref[j, qi])
                dsb = dst.astype(BF16)
                dk = dk + jnp.dot(dsb, qm, preferred_element_type=F32)
                t = lax.dot_general(dsb, kms[j], (((0,), (0,)), ((), ())), preferred_element_type=F32)
                dq = t if dq is None else dq + t
                dfq_ref[j, qi] += jnp.sum(dst, axis=0, keepdims=True)
                new_dfs.append(dfs[j] - jnp.sum(dst, axis=1, keepdims=True))
            dq_ref[pl.ds(off, tq), :] += dq
            return dk, dv, tuple(new_dfs)

        zero = jnp.zeros((tq, LANES), F32)
        zcol = jnp.zeros((tq, 1), F32)
        acc = step(ki, (zero, zero, (zcol, zcol)), True)
        dk, dv, dfs = lax.fori_loop(ki + 1, nq, lambda qi, a: step(qi, a, False), acc)
        dk_ref[...] = dk.astype(BF16)
        dv_ref[...] = dv.astype(BF16)
        dfk_ref[...] = jnp.where(lane < hd, dfs[0], dfs[1])
        after()

    kspec = pl.BlockSpec((tq, LANES), lambda h, i: (i, h))
    fullspec = pl.BlockSpec((S, LANES), lambda h, i: (0, h))
    rowspec = pl.BlockSpec((2, nq, 1, tq), lambda h, i: (h, 0, 0, 0))
    return _pcall(
        kern, name="flash_bwd", grid=(hp, nq),
        in_specs=[fullspec, fullspec, kspec, kspec, pl.BlockSpec((tq, LANES), lambda h, i: (i, 0)),
                  rowspec, rowspec, ANY] + r_ispec,
        out_specs=[kspec, pl.BlockSpec((tq, LANES), lambda h, i: (i, h + dv_col)), fullspec, rowspec, kspec]
        + r_ospec,
        out_shape=[jax.ShapeDtypeStruct((S, D), BF16), jax.ShapeDtypeStruct(dbuf_hbm.shape, BF16),
                   jax.ShapeDtypeStruct((S, D), F32), jax.ShapeDtypeStruct((2 * hp, nq, 1, tq), F32),
                   jax.ShapeDtypeStruct((S, D), F32)] + r_oshape,
        scratch_shapes=r_scratch, input_output_aliases={7: 1},
        compiler_params=_cparams(("arbitrary", "arbitrary")),
    )(qs, do, kn, vb, fk_b, lse_r, delta_r, dbuf_hbm, *r_in)


def _delta_prep(do, o, lse_b, grp, sel, S, D):
    def body(g_ref, o_ref, l_ref, e_ref, s_ref, dl_ref, lse_ref):
        prod = g_ref[...].astype(F32) * o_ref[...]
        dl_ref[...] = _dot_rs(prod, e_ref[...])
        lse_ref[...] = _dot_rs(l_ref[...], s_ref[...])
    return _rowcall("delta_prep", body, S, 256, [(do, D, 0), (o, D, 0), (lse_b, D, 0)], [grp, sel],
                    [(LANES, F32), (LANES, F32)], [])


def _qk_bwd(proj, dqs, dkn, gq, gk, grp, grp_t, S, D, hd, into):
    scale = hd ** -0.5

    def one(x, dn, gain, e, et):
        r = _head_rstd(x, e, et, hd)
        xh = x * r
        t = dn * gain
        mean = _dot_rs(_dot_rs(t * xh, e), et) * (1.0 / hd)
        return r * (t - xh * mean), _csum(dn * xh)

    def body(q_ref, k_ref, dq_ref, dk_ref, gq_ref, gk_ref, e_ref, et_ref, o_ref, sq_ref, sk_ref):
        e, et = e_ref[...], et_ref[...]
        dq, sq = one(q_ref[...], dq_ref[...].astype(F32) * scale, gq_ref[...], e, et)
        dk, sk = one(k_ref[...], dk_ref[...].astype(F32), gk_ref[...], e, et)
        o_ref[:, 0:D] = dq.astype(BF16)
        o_ref[:, D:2 * D] = dk.astype(BF16)
        sq_ref[...] += sq
        sk_ref[...] += sk

    return _rowcall("qk_bwd", body, S, 256,
                    [(proj, D, 0), (proj, D, 1), (dqs, D, 0), (dkn, D, 0)],
                    [gq, gk, grp, grp_t], [(2 * D, BF16)], [(1, D)] * 2, into=into)


def _shift_copies(buf, sh, ts):
    for b in range(1, SUBLANES):
        sh[b - 1] = buf[b:b + ts + HALO - SUBLANES, :]


def _rows_from(buf, sh, o, ts):
    a, b = divmod(o, SUBLANES)
    if b == 0:
        return buf[o:o + ts, :]
    return sh[b - 1, SUBLANES * a:SUBLANES * a + ts, :]


def _conv_fwd(proj, acol, bcol, w_pad, cb, lg, lb, S, C, taps, ts):
    ts = min(ts, S)

    def kern(a_ref, b_ref, w_ref, cb_ref, lg_ref, lb_ref, u1_ref, u3_ref, ubuf, ush):
        @pl.when(pl.program_id(0) == 0)
        def _():
            ubuf[0:HALO, :] = jnp.zeros((HALO, C), F32)

        ubuf[HALO:HALO + ts, :] = a_ref[...] * _sig(b_ref[...])
        _shift_copies(ubuf, ush, ts)
        acc = jnp.zeros((ts, C), F32) + cb_ref[...]
        for k in range(taps):
            acc = acc + w_ref[k:k + 1, :] * _rows_from(ubuf, ush, HALO - (taps - 1) + k, ts)
        u1_ref[...] = acc
        mu = jnp.mean(acc, axis=-1, keepdims=True)
        xc = acc - mu
        rstd = lax.rsqrt(jnp.mean(xc * xc, axis=-1, keepdims=True) + NORM_EPS)
        u2 = xc * rstd * lg_ref[...] + lb_ref[...]
        u3_ref[...] = (u2 * _sig(u2)).astype(BF16)
        ubuf[0:HALO, :] = ubuf[ts:ts + HALO, :]

    vec = lambda a: pl.BlockSpec(a.shape, lambda i: (0, 0))
    return _pcall(
        kern, name="conv_fwd", grid=(S // ts,),
        in_specs=[pl.BlockSpec((ts, C), lambda i: (i, acol)), pl.BlockSpec((ts, C), lambda i: (i, bcol)),
                  vec(w_pad), vec(cb), vec(lg), vec(lb)],
        out_specs=[pl.BlockSpec((ts, C), lambda i: (i, 0))] * 2,
        out_shape=[jax.ShapeDtypeStruct((S, C), F32), jax.ShapeDtypeStruct((S, C), BF16)],
        scratch_shapes=[pltpu.VMEM((HALO + ts, C), F32),
                        pltpu.VMEM((SUBLANES - 1, HALO + ts - SUBLANES, C), F32)],
        compiler_params=_cparams(("arbitrary",)),
    )(proj, proj, w_pad, cb, lg, lb)


def _conv_bwd(du3, u1, proj, acol, bcol, w_pad, lg, lb, S, C, taps, ts, into):
    ts = min(ts, S)
    dbuf_hbm, dcol = into
    nt = S // ts
    hb = ts // HALO

    def ln_bwd(g, u, lgv, lbv):
        mu = jnp.mean(u, axis=-1, keepdims=True)
        xc = u - mu
        rstd = lax.rsqrt(jnp.mean(xc * xc, axis=-1, keepdims=True) + NORM_EPS)
        xh = xc * rstd
        u2 = xh * lgv + lbv
        s = _sig(u2)
        du2 = g * (s + u2 * s * (1.0 - s))
        dxh = du2 * lgv
        du1 = rstd * (dxh - jnp.mean(dxh, axis=-1, keepdims=True)
                      - xh * jnp.mean(dxh * xh, axis=-1, keepdims=True))
        return du1, du2, xh

    def kern(g_ref, u_ref, a_ref, b_ref, gn_ref, un_ref, ap_ref, bp_ref, w_ref, lg_ref, lb_ref, _,
             dg_ref, dw_ref, dcb_ref, dlg_ref, dlb_ref, dbuf, ubuf, dsh, ush):
        i = pl.program_id(0)

        @pl.when(i == 0)
        def _():
            dw_ref[...] = jnp.zeros(dw_ref.shape, F32)
            dcb_ref[...] = jnp.zeros(dcb_ref.shape, F32)
            dlg_ref[...] = jnp.zeros(dlg_ref.shape, F32)
            dlb_ref[...] = jnp.zeros(dlb_ref.shape, F32)

        lgv, lbv = lg_ref[...], lb_ref[...]
        du1, du2, xh = ln_bwd(g_ref[...], u_ref[...], lgv, lbv)
        dbuf[0:ts, :] = du1
        du1n, _, _ = ln_bwd(gn_ref[...], un_ref[...], lgv, lbv)
        dbuf[ts:ts + HALO, :] = jnp.where(i < nt - 1, du1n, 0.0)
        a = a_ref[...]
        sb = _sig(b_ref[...])
        ubuf[HALO:HALO + ts, :] = a * sb
        ubuf[0:HALO, :] = jnp.where(i > 0, ap_ref[...] * _sig(bp_ref[...]), 0.0)
        dcb_ref[...] += _csum(du1)
        dlg_ref[...] += _csum(du2 * xh)
        dlb_ref[...] += _csum(du2)
        _shift_copies(dbuf, dsh, ts)
        _shift_copies(ubuf, ush, ts)
        for r0 in range(0, ts, CONV_ROWS):
            du0 = jnp.zeros((CONV_ROWS, C), F32)
            for k in range(taps):
                du0 = du0 + w_ref[k:k + 1, :] * _rows_from(dbuf, dsh, r0 + taps - 1 - k, CONV_ROWS)
            ac = a_ref[r0:r0 + CONV_ROWS, :]
            sc = _sig(b_ref[r0:r0 + CONV_ROWS, :])
            dg_ref[r0:r0 + CONV_ROWS, 0:C] = (du0 * sc).astype(BF16)
            dg_ref[r0:r0 + CONV_ROWS, C:2 * C] = (du0 * ac * sc * (1.0 - sc)).astype(BF16)
        for k0 in range(0, taps, CONV_TAPS):
            ks = range(k0, min(k0 + CONV_TAPS, taps))
            accs = [jnp.zeros((SUBLANES, C), F32) for _ in ks]
            for r0 in range(0, ts, CONV_ROWS):
                d = dbuf[r0:r0 + CONV_ROWS, :]
                for t, k in enumerate(ks):
                    prod = d * _rows_from(ubuf, ush, r0 + HALO - (taps - 1) + k, CONV_ROWS)
                    accs[t] = accs[t] + jnp.sum(prod.reshape(CONV_ROWS // SUBLANES, SUBLANES, C), axis=0)
            for t, k in enumerate(ks):
                dw_ref[k:k + 1, :] += _csum(accs[t])

    vec = lambda a: pl.BlockSpec(a.shape, lambda i: (0, 0))
    tile = lambda cb: pl.BlockSpec((ts, C), functools.partial(lambda i, cb: (i, cb), cb=cb))
    nxt = lambda cb: pl.BlockSpec(
        (HALO, C), functools.partial(lambda i, cb: (jnp.minimum((i + 1) * hb, nt * hb - 1), cb), cb=cb))
    prv = lambda cb: pl.BlockSpec(
        (HALO, C), functools.partial(lambda i, cb: (jnp.maximum(i * hb - 1, 0), cb), cb=cb))
    return _pcall(
        kern, name="conv_bwd", grid=(nt,),
        in_specs=[tile(0), tile(0), tile(acol), tile(bcol), nxt(0), nxt(0), prv(acol), prv(bcol),
                  vec(w_pad), vec(lg), vec(lb), ANY],
        out_specs=[pl.BlockSpec((ts, 2 * C), lambda i: (i, dcol))]
        + [pl.BlockSpec(w_pad.shape, lambda i: (0, 0))] + [pl.BlockSpec((1, C), lambda i: (0, 0))] * 3,
        out_shape=[jax.ShapeDtypeStruct(dbuf_hbm.shape, BF16)]
        + [jax.ShapeDtypeStruct(w_pad.shape, F32)] + [jax.ShapeDtypeStruct((1, C), F32)] * 3,
        scratch_shapes=[pltpu.VMEM((ts + HALO, C), F32), pltpu.VMEM((HALO + ts, C), F32)]
        + [pltpu.VMEM((SUBLANES - 1, HALO + ts - SUBLANES, C), F32)] * 2,
        input_output_aliases={11: 0},
        compiler_params=_cparams(("arbitrary",)),
    )(du3, u1, proj, proj, du3, u1, proj, proj, w_pad, lg, lb, dbuf_hbm)


def _gate_merge(proj, gacol, gbcol, ba, bb, S, D):
    def body(ga_ref, gb_ref, a_ref, b_ref, out_ref):
        out_ref[...] = (_sig(ga_ref[...]) * a_ref[...] + _sig(gb_ref[...]) * b_ref[...]).astype(BF16)
    return _rowcall("gate_merge", body, S, 512,
                    [(proj, D, gacol), (proj, D, gbcol), (ba, D, 0), (bb, D, 0)], [], [(D, BF16)], [])[0]


def _gate_bwd(dm, proj, gacol, gbcol, ba, bb, S, D, into):
    def body(dm_ref, ga_ref, gb_ref, a_ref, b_ref, dg_ref, da_ref, db_ref):
        dmv = dm_ref[...]
        sa, sb = _sig(ga_ref[...]), _sig(gb_ref[...])
        da_ref[...] = (dmv * sa).astype(BF16)
        db_ref[...] = (dmv * sb).astype(BF16)
        dg_ref[:, 0:D] = (dmv * a_ref[...] * sa * (1.0 - sa)).astype(BF16)
        dg_ref[:, D:2 * D] = (dmv * b_ref[...] * sb * (1.0 - sb)).astype(BF16)
    return _rowcall("gate_bwd", body, S, 512,
                    [(dm, D, 0), (proj, D, gacol), (proj, D, gbcol), (ba, D, 0), (bb, D, 0)], [],
                    [(2 * D, BF16), (D, BF16), (D, BF16)], [], into=into)


def _resid_norm2(x, mo, g1, g, sc, sh, S, D):
    def body(x_ref, mo_ref, g1_ref, g_ref, sc_ref, sh_ref, x1_ref, h_ref):
        x1 = x_ref[...] + g1_ref[...] * mo_ref[...]
        x1_ref[...] = x1
        r = lax.rsqrt(jnp.mean(x1 * x1, axis=-1, keepdims=True) + NORM_EPS)
        h_ref[...] = ((x1 * r * g_ref[...]) * (1.0 + sc_ref[...]) + sh_ref[...]).astype(BF16)
    return _rowcall("resid_norm2", body, S, 512, [(x, D, 0), (mo, D, 0)], [g1, g, sc, sh],
                    [(D, F32), (D, BF16)], [])


def _loss_dy(x1, ml, tgt, g2, S, D):
    def body(x1_ref, ml_ref, t_ref, g2_ref, dy_ref, dml_ref, sq_ref, dg2_ref):
        mlv = ml_ref[...]
        diff = x1_ref[...] + g2_ref[...] * mlv - t_ref[...]
        dy = diff * (1.0 / D)
        dy_ref[...] = dy
        dml_ref[...] = (dy * g2_ref[...]).astype(BF16)
        sq_ref[...] += _csum(diff * diff)
        dg2_ref[...] += _csum(dy * mlv)
    return _rowcall("loss_dy", body, S, 512, [(x1, D, 0), (ml, D, 0), (tgt, D, 0)], [g2],
                    [(D, F32), (D, BF16)], [(1, D), (1, D)])


def _norm_bwd(name, xin, dh, dres, g, sc, S, D, extra=None):
    def body(*refs):
        if extra is None:
            x_ref, dh_ref, dr_ref, g_ref, sc_ref, dx_ref, dsh_ref, dsc_ref, dg_ref = refs
        else:
            (x_ref, dh_ref, dr_ref, mo_ref, g_ref, sc_ref, g1_ref,
             dx_ref, dmo_ref, dsh_ref, dsc_ref, dg_ref, dg1_ref) = refs
        xv, dhv, gv = x_ref[...], dh_ref[...], g_ref[...]
        r = lax.rsqrt(jnp.mean(xv * xv, axis=-1, keepdims=True) + NORM_EPS)
        xh = xv * r
        dsh_ref[...] += _csum(dhv)
        dsc_ref[...] += _csum(dhv * xh * gv)
        dxg = dhv * (1.0 + sc_ref[...])
        dg_ref[...] += _csum(dxg * xh)
        dxh = dxg * gv
        dx = dr_ref[...] + r * (dxh - xh * jnp.mean(dxh * xh, axis=-1, keepdims=True))
        dx_ref[...] = dx
        if extra is not None:
            dmo_ref[...] = (dx * g1_ref[...]).astype(BF16)
            dg1_ref[...] += _csum(dx * mo_ref[...])

    rows = [(xin, D, 0), (dh, D, 0), (dres, D, 0)]
    vecs = [g, sc]
    if extra is None:
        return _rowcall(name, body, S, 512, rows, vecs, [(D, F32)], [(1, D)] * 3)
    return _rowcall(name, body, S, 512, rows + [(extra[0], D, 0)], vecs + [extra[1]],
                    [(D, F32), (D, BF16)], [(1, D)] * 4)


def _ada_fwd(c_all, w, b_part):
    B, D = c_all.shape
    N = w.shape[1]
    tn = min(512, N)

    def kern(c_ref, w_ref, b_ref, o_ref):
        cv = c_ref[...]
        ca = cv * _sig(cv)
        o_ref[...] = jnp.dot(ca, w_ref[...], precision=lax.Precision.HIGHEST,
                             preferred_element_type=F32) + b_ref[...]

    return _pcall(
        kern, name="ada_fwd", grid=(N // tn,),
        in_specs=[pl.BlockSpec((B, D), lambda j: (0, 0)), pl.BlockSpec((D, tn), lambda j: (0, j)),
                  pl.BlockSpec((1, tn), lambda j: (0, j))],
        out_specs=pl.BlockSpec((B, tn), lambda j: (0, j)),
        out_shape=jax.ShapeDtypeStruct((B, N), F32),
        compiler_params=_cparams(("parallel",)),
    )(c_all, w, b_part)


def _ada_wgrad(c_t_pad, dmod_pad):
    D = c_t_pad.shape[0]
    N = dmod_pad.shape[1]
    tn = min(512, N)

    def kern(c_ref, d_ref, o_ref):
        cv = c_ref[...]
        ca = cv * _sig(cv)
        o_ref[...] = jnp.dot(ca, d_ref[...], precision=lax.Precision.HIGHEST,
                             preferred_element_type=F32)

    return _pcall(
        kern, name="ada_wgrad", grid=(N // tn,),
        in_specs=[pl.BlockSpec((D, LANES), lambda j: (0, 0)), pl.BlockSpec((LANES, tn), lambda j: (0, j))],
        out_specs=pl.BlockSpec((D, tn), lambda j: (0, j)),
        out_shape=jax.ShapeDtypeStruct((D, N), F32),
        compiler_params=_cparams(("parallel",)),
    )(c_t_pad, dmod_pad)


def _ag_small(name, arrs):
    n = len(arrs)

    def kern(*refs):
        ins, outs = refs[:n], refs[n:2 * n]
        send, recv = refs[2 * n], refs[2 * n + 1]
        x, y, c = lax.axis_index("x"), lax.axis_index("y"), lax.axis_index("c")
        me = 4 * x + 2 * y + c

        def copy(i, m, slot):
            peer = (x ^ ((m >> 2) & 1), y ^ ((m >> 1) & 1), c ^ (m & 1))
            return pltpu.make_async_remote_copy(
                src_ref=ins[i], dst_ref=outs[i].at[slot],
                send_sem=send.at[i * 7 + m - 1], recv_sem=recv.at[i * 7 + m - 1],
                device_id=peer, device_id_type=MESH)

        for i in range(n):
            outs[i][me] = ins[i][...]
            for m in range(1, 8):
                copy(i, m, me).start()
        for i in range(n):
            for m in range(1, 8):
                copy(i, m, me).wait_send()
                copy(i, m, me ^ m).wait_recv()

    vm = pl.BlockSpec(memory_space=pltpu.VMEM)
    return _pcall(
        kern, name=name, in_specs=[vm] * n, out_specs=[vm] * n,
        out_shape=[jax.ShapeDtypeStruct((8,) + a.shape, a.dtype) for a in arrs],
        scratch_shapes=[pltpu.SemaphoreType.DMA((7 * n,)), pltpu.SemaphoreType.DMA((7 * n,))],
        compiler_params=pltpu.CompilerParams(has_side_effects=True),
    )(*arrs)


def _exchange(name, arrs, plan):
    out_shape, scratch, phases = plan(arrs)

    def kern(*refs):
        for phase in phases(refs):
            phase()

    return _pcall(
        kern, name=name, in_specs=[ANY] * len(arrs), out_specs=[ANY] * len(out_shape),
        out_shape=out_shape, scratch_shapes=scratch,
        compiler_params=pltpu.CompilerParams(has_side_effects=True),
    )(*arrs)


def _ride(plan_and_arrs, n_in, n_out):
    if plan_and_arrs is None:
        return [], [], [], [], [], lambda refs, first, middle, last: ((lambda: None), (lambda: None))
    plan, arrs = plan_and_arrs
    out_shape, scratch, phases = plan(arrs)
    na, no = len(arrs), len(out_shape)

    def hook(refs, first, middle, last):
        mine = refs[n_in:n_in + na] + refs[n_in + na + n_out:]
        start, mid, finish = phases(mine)

        def before():
            pl.when(first)(start)
            pl.when(middle)(mid)

        def after():
            pl.when(last)(finish)

        return before, after

    return list(arrs), [ANY] * na, [ANY] * no, out_shape, scratch, hook


def _gather_plan(arrs):
    n = len(arrs)

    def phases(refs):
        ins, outs = refs[:n], refs[n:2 * n]
        s1, r1, s2, r2, loc = refs[2 * n:2 * n + 5]
        x, y, c = lax.axis_index("x"), lax.axis_index("y"), lax.axis_index("c")
        me = 2 * x + y

        def half(i, hc):
            hr = ins[i].shape[0] // 2
            return pl.ds(hc * hr, hr)

        def own(i):
            return pltpu.make_async_remote_copy(
                src_ref=ins[i], dst_ref=outs[i].at[me], send_sem=loc.at[i], recv_sem=loc.at[n + i],
                device_id=(x, y, 1 - c), device_id_type=MESH)

        def fetch(i, m, slot):
            px, py = x ^ ((m >> 1) & 1), y ^ (m & 1)
            return pltpu.make_async_remote_copy(
                src_ref=ins[i].at[half(i, c)], dst_ref=outs[i].at[slot, half(i, c)],
                send_sem=s1.at[i * 3 + m - 1], recv_sem=r1.at[i * 3 + m - 1],
                device_id=(px, py, c), device_id_type=MESH)

        def passed(i, m, hc):
            return pltpu.make_async_remote_copy(
                src_ref=outs[i].at[me ^ m, half(i, hc)], dst_ref=outs[i].at[me ^ m, half(i, hc)],
                send_sem=s2.at[i * 3 + m - 1], recv_sem=r2.at[i * 3 + m - 1],
                device_id=(x, y, 1 - c), device_id_type=MESH)

        def start():
            for i in range(n):
                for m in range(1, 4):
                    fetch(i, m, me).start()
            for i in range(n):
                own(i).start()

        def mid():
            for i in range(n):
                for m in range(1, 4):
                    fetch(i, m, me ^ m).wait_recv()
                    passed(i, m, c).start()

        def finish():
            for i in range(n):
                own(i).wait()
                for m in range(1, 4):
                    fetch(i, m, me).wait_send()
                    passed(i, m, c).wait_send()
                    passed(i, m, 1 - c).wait_recv()

        return start, mid, finish

    out_shape = [jax.ShapeDtypeStruct((4,) + a.shape, a.dtype) for a in arrs]
    scratch = [pltpu.SemaphoreType.DMA((3 * n,))] * 4 + [pltpu.SemaphoreType.DMA((2 * n,))]
    return out_shape, scratch, phases


def _pair_send_halves(name, arrs):
    n = len(arrs)

    def kern(*refs):
        ins, outs = refs[:n], refs[n:2 * n]
        send, recv = refs[2 * n], refs[2 * n + 1]
        x, y, c = lax.axis_index("x"), lax.axis_index("y"), lax.axis_index("c")

        def copy(i, k, hc):
            return pltpu.make_async_remote_copy(
                src_ref=ins[i].at[k, hc], dst_ref=outs[i].at[k],
                send_sem=send.at[i * 4 + k], recv_sem=recv.at[i * 4 + k],
                device_id=(x, y, 1 - c), device_id_type=MESH)

        for i in range(n):
            for k in range(4):
                copy(i, k, 1 - c).start()
        for i in range(n):
            for k in range(4):
                copy(i, k, 1 - c).wait()

    return _pcall(
        kern, name=name, in_specs=[ANY] * n, out_specs=[ANY] * n,
        out_shape=[jax.ShapeDtypeStruct((4,) + a.shape[2:], a.dtype) for a in arrs],
        scratch_shapes=[pltpu.SemaphoreType.DMA((4 * n,)), pltpu.SemaphoreType.DMA((4 * n,))],
        compiler_params=pltpu.CompilerParams(has_side_effects=True),
    )(*arrs)


def _scatter_plan(arrs):
    n = len(arrs)

    def phases(refs):
        ins, outs = refs[:n], refs[n:2 * n]
        send, recv = refs[2 * n], refs[2 * n + 1]
        x, y, c = lax.axis_index("x"), lax.axis_index("y"), lax.axis_index("c")
        me = 2 * x + y

        def copy(i, m, slot):
            px, py = x ^ ((m >> 1) & 1), y ^ (m & 1)
            return pltpu.make_async_remote_copy(
                src_ref=ins[i].at[2 * px + py], dst_ref=outs[i].at[slot],
                send_sem=send.at[i * 3 + m - 1], recv_sem=recv.at[i * 3 + m - 1],
                device_id=(px, py, c), device_id_type=MESH)

        def start():
            for i in range(n):
                for m in range(1, 4):
                    copy(i, m, me).start()

        def finish():
            for i in range(n):
                for m in range(1, 4):
                    copy(i, m, me).wait_send()
                    copy(i, m, me ^ m).wait_recv()

        return start, (lambda: None), finish

    out_shape = [jax.ShapeDtypeStruct(a.shape, a.dtype) for a in arrs]
    scratch = [pltpu.SemaphoreType.DMA((3 * n,)), pltpu.SemaphoreType.DMA((3 * n,))]
    return out_shape, scratch, phases


def _pair_swap(name, arrs):
    n = len(arrs)

    def kern(*refs):
        ins, outs = refs[:n], refs[n:2 * n]
        send, recv = refs[2 * n], refs[2 * n + 1]
        x, y, c = lax.axis_index("x"), lax.axis_index("y"), lax.axis_index("c")

        def copy(i):
            return pltpu.make_async_remote_copy(
                src_ref=ins[i], dst_ref=outs[i], send_sem=send.at[i], recv_sem=recv.at[i],
                device_id=(x, y, 1 - c), device_id_type=MESH)

        for i in range(n):
            copy(i).start()
        for i in range(n):
            copy(i).wait()

    return _pcall(
        kern, name=name, in_specs=[ANY] * n, out_specs=[ANY] * n,
        out_shape=[jax.ShapeDtypeStruct(a.shape, a.dtype) for a in arrs],
        scratch_shapes=[pltpu.SemaphoreType.DMA((n,)), pltpu.SemaphoreType.DMA((n,))],
        compiler_params=pltpu.CompilerParams(has_side_effects=True),
    )(*arrs)


def _row_tile(R):
    for t in (256, 128, 64, 32, 16, 8):
        if R % t == 0:
            return t
    return R


def _sum_slots(name, parts):
    K, R, C = parts.shape
    tr = _row_tile(R)

    def kern(p_ref, o_ref):
        acc = p_ref[0].astype(F32)
        for k in range(1, K):
            acc = acc + p_ref[k].astype(F32)
        o_ref[...] = acc

    return _pcall(
        kern, name=name, grid=(R // tr,),
        in_specs=[pl.BlockSpec((K, tr, C), lambda i: (0, i, 0))],
        out_specs=pl.BlockSpec((tr, C), lambda i: (i, 0)),
        out_shape=jax.ShapeDtypeStruct((R, C), F32),
        compiler_params=_cparams(("parallel",)),
    )(parts)


def _sum_pair(name, core, mine, theirs):
    K, _, hr, C = mine.shape
    tr = _row_tile(hr)

    def kern(c_ref, a_ref, b_ref, o_ref):
        o_ref[0] = (a_ref[0, 0].astype(F32) + b_ref[0].astype(F32)).astype(BF16)

    return _pcall(
        kern, name=name, out_shape=jax.ShapeDtypeStruct((K, hr, C), BF16),
        grid_spec=pltpu.PrefetchScalarGridSpec(
            num_scalar_prefetch=1, grid=(K, hr // tr),
            in_specs=[pl.BlockSpec((1, 1, tr, C), lambda k, r, c_ref: (k, c_ref[0], r, 0)),
                      pl.BlockSpec((1, tr, C), lambda k, r, c_ref: (k, r, 0))],
            out_specs=pl.BlockSpec((1, tr, C), lambda k, r, c_ref: (k, r, 0))),
        compiler_params=_cparams(("parallel", "parallel")),
    )(core, mine, theirs)


def _sum_chips(name, chip, own, recv):
    K, hr, C = own.shape
    tr = _row_tile(hr)

    def kern(chip_ref, own_ref, *rest):
        r_refs, o_ref = rest[:K], rest[K]
        me = chip_ref[0]
        mine = own_ref[0].astype(F32)
        acc = None
        for k in range(K):
            t = jnp.where(me == k, mine, r_refs[k][0].astype(F32))
            acc = t if acc is None else acc + t
        o_ref[...] = acc

    def other(k):
        return pl.BlockSpec((1, tr, C), lambda r, s: (jnp.where(s[0] == k, (k + 1) % K, k), r, 0))

    return _pcall(
        kern, name=name, out_shape=jax.ShapeDtypeStruct((hr, C), F32),
        grid_spec=pltpu.PrefetchScalarGridSpec(
            num_scalar_prefetch=1, grid=(hr // tr,),
            in_specs=[pl.BlockSpec((1, tr, C), lambda r, s: (s[0], r, 0))] + [other(k) for k in range(K)],
            out_specs=pl.BlockSpec((tr, C), lambda r, s: (r, 0))),
        compiler_params=_cparams(("parallel",)),
    )(chip, own, *([recv] * K))


def _adam_update(w, m, v, g):
    c1 = 1.0 - ADAM_B1 ** ADAM_STEP
    c2 = 1.0 - ADAM_B2 ** ADAM_STEP
    mn = ADAM_B1 * m + (1.0 - ADAM_B1) * g
    vn = ADAM_B2 * v + (1.0 - ADAM_B2) * (g * g)
    return -ADAM_LR * ((mn / c1) / (jnp.sqrt(vn / c2) + ADAM_EPS) + ADAM_WD * w), mn, vn


def _adamw_halves(name, core, w, m, v, mine, theirs):
    R, C = w.shape
    hr = mine.shape[0]
    tr = _row_tile(hr)
    nbh = hr // tr

    def kern(c_ref, w_ref, m_ref, v_ref, a_ref, b_ref, go_ref, d_ref, mo_ref, vo_ref):
        g = jnp.where(pl.program_id(0) // nbh == c_ref[0], a_ref[...], b_ref[...])
        d, mn, vn = _adam_update(w_ref[...], m_ref[...], v_ref[...], g)
        go_ref[...] = g
        d_ref[...] = d
        mo_ref[...] = mn
        vo_ref[...] = vn

    spec = pl.BlockSpec((tr, C), lambda i, s: (i, 0))
    hspec = pl.BlockSpec((tr, C), lambda i, s: (i % nbh, 0))
    return _pcall(
        kern, name=name, out_shape=[jax.ShapeDtypeStruct((R, C), F32)] * 4,
        grid_spec=pltpu.PrefetchScalarGridSpec(
            num_scalar_prefetch=1, grid=(R // tr,),
            in_specs=[spec, spec, spec, hspec, hspec], out_specs=[spec] * 4),
        compiler_params=_cparams(("parallel",)),
    )(core, w, m, v, mine, theirs)


def _adamw(name, w, m, v, gparts):
    R, C = w.shape
    K = gparts.shape[0]
    tr = _row_tile(R)

    def kern(w_ref, m_ref, v_ref, g_ref, go_ref, d_ref, mo_ref, vo_ref):
        g = g_ref[0]
        for k in range(1, K):
            g = g + g_ref[k]
        d, mn, vn = _adam_update(w_ref[...], m_ref[...], v_ref[...], g)
        go_ref[...] = g
        d_ref[...] = d
        mo_ref[...] = mn
        vo_ref[...] = vn

    spec = pl.BlockSpec((tr, C), lambda i: (i, 0))
    return _pcall(
        kern, name=name, grid=(R // tr,),
        in_specs=[spec, spec, spec, pl.BlockSpec((K, tr, C), lambda i: (0, i, 0))],
        out_specs=[spec] * 4,
        out_shape=[jax.ShapeDtypeStruct((R, C), F32)] * 4,
        compiler_params=_cparams(("parallel",)),
    )(w, m, v, gparts)


def _round_up(a, b):
    return (a + b - 1) // b * b


def kernel(x, c, w_ada, b_ada, norm1_g, w_in, b_forget, q_norm_g, k_norm_g, w_attn_proj, conv_w, conv_b, conv_ln_g, conv_ln_b, w_conv_proj, w_out, norm2_g, w_mlp1, w_mlp2, loss_target, m_w_ada, m_b_ada, m_norm1_g, m_w_in, m_b_forget, m_q_norm_g, m_k_norm_g, m_w_attn_proj, m_conv_w, m_conv_b, m_conv_ln_g, m_conv_ln_b, m_w_conv_proj, m_w_out, m_norm2_g, m_w_mlp1, m_w_mlp2, v_w_ada, v_b_ada, v_norm1_g, v_w_in, v_b_forget, v_q_norm_g, v_k_norm_g, v_w_attn_proj, v_conv_w, v_conv_b, v_conv_ln_g, v_conv_ln_b, v_w_conv_proj, v_w_out, v_norm2_g, v_w_mlp1, v_w_mlp2):
    S, D = x.shape[1], x.shape[2]
    NH, HD = b_forget.shape[-1], q_norm_g.shape[-1]
    TAPS = conv_w.shape[1]
    DIN_S = w_in.shape[-1]
    DIN = 4 * DIN_S
    DFF_S = w_mlp1.shape[-1]
    DFF = 4 * DFF_S
    ADA_S = w_ada.shape[-1]
    DS = w_attn_proj.shape[1]
    CS = conv_w.shape[-1]
    assert NH * HD == D and DIN == 7 * D + NH and TAPS - 1 <= HALO and D % LANES == 0 and 2 * HD == LANES
    NP = _round_up(7 * D + LANES, 512)
    FW = NP - 7 * D
    assert (7 * D) % FW == 0
    TQ = min(512, S)
    NQ = S // TQ
    FCOL = 7 * D // LANES

    xi, yi, ci = lax.axis_index("x"), lax.axis_index("y"), lax.axis_index("c")
    chip = 2 * xi + yi
    dev = 4 * xi + 2 * yi + ci

    x2 = x.reshape(S, D)
    tgt = loss_target.reshape(S, D)

    lane_head = jnp.arange(D, dtype=jnp.int32) // HD
    grp = (lane_head[:, None] == jnp.arange(LANES, dtype=jnp.int32)[None, :]).astype(BF16)
    grp_t = grp.T
    sel = ((jnp.arange(D, dtype=jnp.int32)[:, None] == HD * jnp.arange(LANES, dtype=jnp.int32)[None, :])
           .astype(BF16))
    ch = min(256, S)
    ii = jnp.arange(ch, dtype=jnp.int32)
    tri = (ii[None, :] <= ii[:, None]).astype(BF16)
    tri_u = tri.T
    gq_t = jnp.tile(q_norm_g.reshape(1, HD), (1, NH))
    gk_t = jnp.tile(k_norm_g.reshape(1, HD), (1, NH))
    bf_pad = jnp.pad(b_forget.reshape(1, NH), ((0, 0), (0, LANES - NH)))

    c_all, cw_all = _ag_small(
        "ag_c_convw", [c.reshape(1, D), jnp.pad(conv_w.reshape(TAPS, CS), ((0, HALO - TAPS), (0, 0)))])
    c_all = c_all.reshape(8, D)
    b_part = lax.dynamic_slice(b_ada.reshape(1, -1), (0, chip * ADA_S), (1, ADA_S))
    mod_part = _ada_fwd(c_all, w_ada.reshape(D, ADA_S), b_part)
    (mod_all,) = _ag_small("ag_mod", [mod_part])
    mod_full = jnp.concatenate([mod_all[0], mod_all[2], mod_all[4], mod_all[6]], axis=1)
    mod = lax.dynamic_slice(mod_full, (dev, 0), (1, 6 * D))
    sh1, sc1, g1, sh2, sc2, g2 = [mod[:, i * D:(i + 1) * D] for i in range(6)]

    shards = [w_in.reshape(D, DIN_S), w_attn_proj.reshape(DS, D), w_conv_proj.reshape(DS, D),
              w_out.reshape(DS, D), w_mlp1.reshape(D, DFF_S), w_mlp2.reshape(DFF_S, D)]
    shards = [s.astype(BF16) for s in shards]
    (gw_in,) = _exchange("ag_w_in", shards[:1], _gather_plan)
    w_conv = jnp.concatenate([cw_all[0], cw_all[2], cw_all[4], cw_all[6]], axis=1)

    w_in_full = jnp.concatenate([gw_in[k] for k in range(4)], axis=1)
    w_in_p = jnp.concatenate(
        [w_in_full[:, :2 * D], w_in_full[:, 3 * D + NH:], w_in_full[:, 2 * D:3 * D + NH],
         jnp.zeros((D, NP - 7 * D - NH), BF16)], axis=1)

    n1g = norm1_g.reshape(1, D)
    n2g = norm2_g.reshape(1, D)
    h = _norm_mod("norm_mod1", x2, n1g, sc1, sh1, S, D)
    proj = _mm("mm_in", h, w_in_p, "nn", [F32])
    qs, kn, vb = _qk_prep(proj, 6, gq_t, gk_t, grp, grp_t, S, D, HD)
    f_cum = _fgate_fwd(proj, FCOL, bf_pad, tri, S)
    fk_c = f_cum[:, :NH]
    fk_r = fk_c.T.reshape(NH, NQ, 1, TQ)
    o, o32, lse_b, gw_ap, gw_cp, gw_out, gw_m1, gw_m2 = _flash_fwd(
        qs, kn, vb, fk_r, S, D, HD, TQ, ride=(_gather_plan, shards[1:]))
    w_ap = gw_ap.reshape(D, D)
    w_cp = gw_cp.reshape(D, D)
    w_o = gw_out.reshape(D, D)
    w_m1 = jnp.transpose(gw_m1, (1, 0, 2)).reshape(D, DFF)
    w_m2 = gw_m2.reshape(DFF, D)
    br_a = _mm("mm_attn_proj", o, w_ap, "nn", [F32])
    cb, clg, clb = conv_b.reshape(1, D), conv_ln_g.reshape(1, D), conv_ln_b.reshape(1, D)
    u1, u3 = _conv_fwd(proj, 2, 3, w_conv, cb, clg, clb, S, D, TAPS, 256)
    br_b = _mm("mm_conv_proj", u3, w_cp, "nn", [F32])
    merged = _gate_merge(proj, 4, 5, br_a, br_b, S, D)
    mo = _mm("mm_out", merged, w_o, "nn", [F32])
    x1, h2 = _resid_norm2(x2, mo, g1, n2g, sc2, sh2, S, D)

    def relu2(r):
        rp = jnp.maximum(r, 0.0)
        return (rp * rp,)
    z = _mm("mm_mlp1", h2, w_m1, "nn", [BF16], epi=relu2)
    ml = _mm("mm_mlp2", z, w_m2, "nn", [F32])
    dy, dml, sq, dg2 = _loss_dy(x1, ml, tgt, g2, S, D)
    loss_part = jnp.full((1, LANES), 0.5 * jnp.sum(sq) / D, F32)

    da = _mm("mm_dz", dml, w_m2, "nt", [BF16], epi=lambda r, zz: (r * 2.0 * jnp.sqrt(zz.astype(F32)),),
             extras=(z,))
    dw_m2 = _mm("mm_dw_mlp2", z, dml, "tn", [BF16])
    dw_m1 = _mm("mm_dw_mlp1", h2, da, "tn", [BF16])
    dh2 = _mm("mm_dh2", da, w_m1, "nt", [F32])
    dx1, dmo, dsh2, dsc2, dn2g, dg1 = _norm_bwd("norm2_bwd", x1, dh2, dy, n2g, sc2, S, D, extra=(mo, g1))
    dmerged = _mm("mm_dmerged", dmo, w_o, "nt", [F32])
    dw_o = _mm("mm_dw_out", merged, dmo, "tn", [BF16])
    dproj, dba, dbb = _gate_bwd(dmerged, proj, 4, 5, br_a, br_b, S, D, into=(lax.empty((S, NP), BF16), 2))
    do = _mm("mm_do", dba, w_ap, "nt", [BF16])
    dw_ap = _mm("mm_dw_attn_proj", o, dba, "tn", [BF16])
    du3 = _mm("mm_du3", dbb, w_cp, "nt", [F32])
    dw_cp = _mm("mm_dw_conv_proj", u3, dbb, "tn", [BF16])
    dproj, dcw, dcb, dclg, dclb = _conv_bwd(du3, u1, proj, 2, 3, w_conv, clg, clb, S, D, TAPS, 256,
                                            into=(dproj, 1))

    core = ci.astype(jnp.int32).reshape(1)
    chip1 = chip.astype(jnp.int32).reshape(1)
    halves = lambda p: p.astype(BF16).reshape(4, 2, p.shape[1] // 2, p.shape[2])
    names = ["w_in", "w_attn_proj", "w_conv_proj", "w_out", "w_mlp1", "w_mlp2"]
    parts = [halves(p) for p in (dw_ap.reshape(4, DS, D), dw_cp.reshape(4, DS, D), dw_o.reshape(4, DS, D),
                                 jnp.transpose(dw_m1.reshape(D, 4, DFF_S), (1, 0, 2)), dw_m2.reshape(4, DFF_S, D))]
    theirs = _pair_send_halves("rs_pair", parts)
    chip_parts = [_sum_pair("sum_pair_" + nm, core, p, t) for nm, p, t in zip(names[1:], parts, theirs)]

    delta_c, lse_c = _delta_prep(do, o32, lse_b, grp, sel, S, D)
    to_rows = lambda t: t[:, :NH].T.reshape(NH, NQ, 1, TQ)
    dkn, dproj, dqs, dfq_r, dfk_b, *recvd = _flash_bwd(
        qs, kn, vb, do, f_cum, to_rows(lse_c), to_rows(delta_c), S, D, HD, TQ,
        dv_into=(dproj, 6 * D // LANES), ride=(_scatter_plan, chip_parts))
    dproj, sq_q, sq_k = _qk_bwd(proj, dqs, dkn, gq_t, gk_t, grp, grp_t, S, D, HD, into=(dproj, 0))
    to_cols = lambda r: jnp.pad(r.reshape(NH, S).T, ((0, 0), (0, LANES - NH)))
    dfq_pad = to_cols(dfq_r)
    dfk_pad = jnp.pad(dfk_b[:, ::HD], ((0, 0), (0, LANES - NH)))
    dproj, dbf = _fgate_bwd(dfk_pad, dfq_pad, proj, FCOL, bf_pad, tri_u, NH, S, FW, into=(dproj, 7 * D // FW))
    dw_in_p = _mm("mm_dw_in", h, dproj, "tn", [BF16])
    dw_in_full = jnp.concatenate(
        [dw_in_p[:, :2 * D], dw_in_p[:, 6 * D:7 * D + NH], dw_in_p[:, 2 * D:6 * D]], axis=1)
    part_in = halves(jnp.transpose(dw_in_full.reshape(D, 4, DIN_S), (1, 0, 2)))
    (their_in,) = _pair_send_halves("rs_pair_w_in", [part_in])
    chip_in = _sum_pair("sum_pair_w_in", core, part_in, their_in)
    dh, recv_in = _mm("mm_dh", dproj, w_in_p, "nt", [F32], ride=(_scatter_plan, [chip_in]))
    gx, dsh1, dsc1, dn1g = _norm_bwd("norm1_bwd", x2, dh, dx1, n1g, sc1, S, D)

    packed = jnp.concatenate([dsh1, dsc1, dg1, dsh2, dsc2, dg2, dn1g, dcb, dclg, dclb, dn2g,
                              sq_q, sq_k, dbf, loss_part], axis=1)
    small_all, dcw_all = _ag_small("ag_small_grads", [packed, dcw])
    small = _sum_slots("sum_small", small_all.reshape(8, 1, -1)).reshape(1, -1)
    dmod_sum = small[:, :6 * D]
    seg = lambda k: small[:, (6 + k) * D:(7 + k) * D]
    g_n1g, g_cb, g_clg, g_clb, g_n2g = seg(0), seg(1), seg(2), seg(3), seg(4)
    g_qn = _sum_slots("sum_qn", seg(5).reshape(NH, 1, HD))
    g_kn = _sum_slots("sum_kn", seg(6).reshape(NH, 1, HD))
    g_bf = small[:, 13 * D:13 * D + NH]
    loss = small[0, 13 * D + LANES]
    dcw_mine = lax.dynamic_slice(dcw_all[:, :TAPS, :], (0, 0, chip * CS), (8, TAPS, CS))

    dmod_all = small_all.reshape(8, -1)[:, :6 * D]
    dmod_cols = lax.dynamic_slice(dmod_all, (0, chip * ADA_S), (8, ADA_S))
    c_t_pad = jnp.pad(c_all.T, ((0, 0), (0, LANES - 8)))
    g_wada = _ada_wgrad(c_t_pad, jnp.pad(dmod_cols, ((0, LANES - 8), (0, 0))))

    sums =[_sum_chips("sum_" + nm, chip1, p, r)
            for nm, p, r in zip(names, [chip_in] + chip_parts, [recv_in] + list(recvd))]
    others = _pair_swap("pair_grads", sums)

    res = {}
    big = {nm: (a, b) for nm, a, b in zip(names, sums, others)}
    big_w = {"w_in": (w_in, m_w_in, v_w_in), "w_attn_proj": (w_attn_proj, m_w_attn_proj, v_w_attn_proj),
             "w_conv_proj": (w_conv_proj, m_w_conv_proj, v_w_conv_proj), "w_out": (w_out, m_w_out, v_w_out),
             "w_mlp1": (w_mlp1, m_w_mlp1, v_w_mlp1), "w_mlp2": (w_mlp2, m_w_mlp2, v_w_mlp2)}
    for nm in names:
        shp = big_w[nm][0].shape
        outs = _adamw_halves("adamw_" + nm, core, *[t.reshape(shp[1], shp[2]) for t in big_w[nm]], *big[nm])
        res[nm] = [t.reshape(shp) for t in outs]
    outs = _adamw("adamw_w_ada", w_ada.reshape(D, ADA_S), m_w_ada.reshape(D, ADA_S),
                  v_w_ada.reshape(D, ADA_S), g_wada.reshape(1, D, ADA_S))
    res["w_ada"] = [t.reshape(w_ada.shape) for t in outs]
    outs = _adamw("adamw_conv_w", conv_w.reshape(TAPS, CS), m_conv_w.reshape(TAPS, CS),
                  v_conv_w.reshape(TAPS, CS), dcw_mine)
    res["conv_w"] = [t.reshape(conv_w.shape) for t in outs]

    small_w = [("b_ada", b_ada, m_b_ada, v_b_ada, dmod_sum), ("norm1_g", norm1_g, m_norm1_g, v_norm1_g, g_n1g),
               ("b_forget", b_forget, m_b_forget, v_b_forget, g_bf),
               ("q_norm_g", q_norm_g, m_q_norm_g, v_q_norm_g, g_qn),
               ("k_norm_g", k_norm_g, m_k_norm_g, v_k_norm_g, g_kn),
               ("conv_b", conv_b, m_conv_b, v_conv_b, g_cb), ("conv_ln_g", conv_ln_g, m_conv_ln_g, v_conv_ln_g, g_clg),
               ("conv_ln_b", conv_ln_b, m_conv_ln_b, v_conv_ln_b, g_clb),
               ("norm2_g", norm2_g, m_norm2_g, v_norm2_g, g_n2g)]
    cat = lambda ts: jnp.concatenate([t.reshape(1, -1) for t in ts], axis=1)
    outs = _adamw("adamw_small", cat([t[1] for t in small_w]), cat([t[2] for t in small_w]),
                  cat([t[3] for t in small_w]), cat([t[4] for t in small_w]).reshape(1, 1, -1))
    off = 0
    for nm, w_, _, _, _ in small_w:
        n = w_.size
        res[nm] = [t[:, off:off + n].reshape(w_.shape) for t in outs]
        off += n

    order = ["w_ada", "b_ada", "norm1_g", "w_in", "b_forget", "q_norm_g", "k_norm_g", "w_attn_proj", "conv_w",
             "conv_b", "conv_ln_g", "conv_ln_b", "w_conv_proj", "w_out", "norm2_g", "w_mlp1", "w_mlp2"]
    return (loss, gx.reshape(x.shape), *[res[n][0] for n in order], *[res[n][1] for n in order],
            *[res[n][2] for n in order], *[res[n][3] for n in order])
```

```python
import functools

import jax
import jax.numpy as jnp
from jax import lax
from jax.experimental import pallas as pl
from jax.experimental.pallas import tpu as pltpu

F32 = jnp.float32
BF16 = jnp.bfloat16
MESH = pl.DeviceIdType.MESH
ANY = pl.BlockSpec(memory_space=pl.ANY)

NORM_EPS = 1e-6
ADAM_LR = 0.001
ADAM_B1 = 0.9
ADAM_B2 = 0.999
ADAM_EPS = 1e-08
ADAM_WD = 0.01
ADAM_STEP = 10
LANES = 128
SUBLANES = 8
HALO = 32
CONV_ROWS = 32
CONV_TAPS = 4
NEG = -1e30
VMEM_LIMIT = 56 * 1024 * 1024


def _pcall(body, **kw):
    return pl.pallas_call(body, **kw)


def _cparams(sem=None):
    if sem is None:
        return pltpu.CompilerParams(vmem_limit_bytes=VMEM_LIMIT)
    return pltpu.CompilerParams(dimension_semantics=sem, vmem_limit_bytes=VMEM_LIMIT)


def _sig(x):
    return 1.0 / (1.0 + jnp.exp(-x))


def _split3(x):
    x1 = x.astype(BF16)
    r = x - x1.astype(F32)
    x2 = r.astype(BF16)
    x3 = (r - x2.astype(F32)).astype(BF16)
    return x1, x2, x3


def _dot_rs(x, e, terms=3):
    out = None
    for t in _split3(x)[:terms]:
        d = jnp.dot(t, e, preferred_element_type=F32)
        out = d if out is None else out + d
    return out


def _dot_ls(e, x):
    out = None
    for t in _split3(x):
        d = jnp.dot(e, t, preferred_element_type=F32)
        out = d if out is None else out + d
    return out


def _tile(n, want):
    if n <= want:
        return n
    t = want - want % LANES
    while n % t:
        t -= LANES
    assert t > 0, (n, want)
    return t


_DIMS = {"nn": ((1,), (0,)), "nt": ((1,), (1,)), "tn": ((0,), (0,))}


def _mm(name, a, b, mode, out_dtypes, epi=None, extras=(), tm=1024, tn=1024, tk=4096, ride=None):
    if mode == "nn":
        (M, K), (_, N) = a.shape, b.shape
    elif mode == "nt":
        (M, K), (N, _) = a.shape, b.shape
    else:
        (K, M), (_, N) = a.shape, b.shape
    tm, tn, tk = _tile(M, tm), _tile(N, tn), _tile(K, tk)
    nm, nn, nk = M // tm, N // tn, K // tk
    ne, no = len(extras), len(out_dtypes)
    dims = (_DIMS[mode], ((), ()))
    r_in, r_ispec, r_ospec, r_oshape, r_scratch, r_hook = _ride(ride, 2 + ne, no)

    def kern(*refs):
        a_ref, b_ref = refs[0], refs[1]
        e_refs = refs[2:2 + ne]
        o_refs = refs[2 + ne + len(r_in):2 + ne + len(r_in) + no]
        i, j, k = pl.program_id(0), pl.program_id(1), pl.program_id(2)
        before, after = r_hook(refs, (i == 0) & (j == 0) & (k == 0), (i == nm // 2) & (j == 0) & (k == 0),
                               (i == nm - 1) & (j == nn - 1) & (k == nk - 1))
        before()
        d = lax.dot_general(a_ref[...], b_ref[...], dims, preferred_element_type=F32)

        def finish(r):
            outs = (r,) if epi is None else epi(r, *[e[...] for e in e_refs])
            for o_ref, o in zip(o_refs, outs):
                o_ref[...] = o.astype(o_ref.dtype)

        if nk == 1:
            finish(d)
        else:
            acc = refs[-1]

            @pl.when(k == 0)
            def _():
                acc[...] = d

            @pl.when((k > 0) & (k < nk - 1))
            def _():
                acc[...] += d

            @pl.when(k == nk - 1)
            def _():
                finish(acc[...] + d)
        after()

    if mode == "tn":
        a_spec = pl.BlockSpec((tk, tm), lambda i, j, k: (k, i))
    else:
        a_spec = pl.BlockSpec((tm, tk), lambda i, j, k: (i, k))
    if mode == "nt":
        b_spec = pl.BlockSpec((tn, tk), lambda i, j, k: (j, k))
    else:
        b_spec = pl.BlockSpec((tk, tn), lambda i, j, k: (k, j))
    mn_spec = pl.BlockSpec((tm, tn), lambda i, j, k: (i, j))
    outs = _pcall(
        kern, name=name, grid=(nm, nn, nk),
        in_specs=[a_spec, b_spec] + [mn_spec] * ne + r_ispec,
        out_specs=[mn_spec] * no + r_ospec,
        out_shape=[jax.ShapeDtypeStruct((M, N), dt) for dt in out_dtypes] + r_oshape,
        scratch_shapes=r_scratch + ([pltpu.VMEM((tm, tn), F32)] if nk > 1 else []),
        compiler_params=_cparams(("arbitrary",) * 3 if ride else ("parallel", "parallel", "arbitrary")),
    )(a, b, *extras, *r_in)
    return outs[0] if len(outs) == 1 else outs


def _rowcall(name, body, S, ts, row_ins, vec_ins, row_outs, vec_outs, into=None):
    ts = min(ts, S)
    nri, nvi, nro, nvo = len(row_ins), len(vec_ins), len(row_outs), len(vec_outs)
    na = 0 if into is None else 1

    def kern(*refs):
        ins = refs[:nri + nvi]
        outs = refs[nri + nvi + na:]
        if nvo:
            @pl.when(pl.program_id(0) == 0)
            def _():
                for r in outs[nro:]:
                    r[...] = jnp.zeros(r.shape, r.dtype)
        body(*ins, *outs)

    in_specs = [pl.BlockSpec((ts, w), functools.partial(lambda i, cb: (i, cb), cb=cb))
                for (_, w, cb) in row_ins]
    in_specs += [pl.BlockSpec(v.shape, lambda i: (0, 0)) for v in vec_ins]
    out_specs = [pl.BlockSpec((ts, w), lambda i: (i, 0)) for (w, _) in row_outs]
    out_specs += [pl.BlockSpec((r, w), lambda i: (0, 0)) for (r, w) in vec_outs]
    out_shape = [jax.ShapeDtypeStruct((S, w), dt) for (w, dt) in row_outs]
    out_shape += [jax.ShapeDtypeStruct((r, w), F32) for (r, w) in vec_outs]
    extra, aliases = [], {}
    if into is not None:
        buf, cb = into
        assert buf.dtype == row_outs[0][1] and buf.shape[0] == S
        in_specs.append(ANY)
        out_specs[0] = pl.BlockSpec((ts, row_outs[0][0]), lambda i: (i, cb))
        out_shape[0] = jax.ShapeDtypeStruct(buf.shape, buf.dtype)
        extra, aliases = [buf], {nri + nvi: 0}
    return _pcall(
        kern, name=name, grid=(S // ts,), in_specs=in_specs, out_specs=out_specs,
        out_shape=out_shape, input_output_aliases=aliases,
        compiler_params=_cparams(("arbitrary",) if nvo else ("parallel",)),
    )(*[a for (a, _, _) in row_ins], *vec_ins, *extra)


def _csum(x):
    return jnp.sum(x, axis=0, keepdims=True)


def _norm_mod(name, x, g, sc, sh, S, D):
    def body(x_ref, g_ref, sc_ref, sh_ref, h_ref):
        xv = x_ref[...]
        r = lax.rsqrt(jnp.mean(xv * xv, axis=-1, keepdims=True) + NORM_EPS)
        h_ref[...] = ((xv * r * g_ref[...]) * (1.0 + sc_ref[...]) + sh_ref[...]).astype(BF16)
    return _rowcall(name, body, S, 512, [(x, D, 0)], [g, sc, sh], [(D, BF16)], [])[0]


def _head_rstd(v, grp, grp_t, hd):
    ss = _dot_rs(v * v, grp, 2) * (1.0 / hd)
    r = lax.rsqrt(ss + NORM_EPS)
    return _dot_rs(r, grp_t, 2)


def _qk_prep(proj, vcol, gq, gk, grp, grp_t, S, D, hd):
    scale = hd ** -0.5

    def body(q_ref, k_ref, v_ref, gq_ref, gk_ref, g_ref, gt_ref, qs_ref, kn_ref, vb_ref):
        q = q_ref[...]
        k = k_ref[...]
        rq = _head_rstd(q, g_ref[...], gt_ref[...], hd)
        rk = _head_rstd(k, g_ref[...], gt_ref[...], hd)
        qs_ref[...] = ((q * rq * gq_ref[...]).astype(BF16).astype(F32) * scale).astype(BF16)
        kn_ref[...] = (k * rk * gk_ref[...]).astype(BF16)
        vb_ref[...] = v_ref[...].astype(BF16)

    return _rowcall("qk_prep", body, S, 256, [(proj, D, 0), (proj, D, 1), (proj, D, vcol)],
                    [gq, gk, grp, grp_t], [(D, BF16)] * 3, [])


def _fgate_fwd(proj, fcol, bf_pad, tri, S):
    ch = tri.shape[0]

    def body(f_ref, b_ref, tri_ref, out_ref):
        carry = jnp.zeros((1, LANES), F32)
        for c in range(S // ch):
            z = f_ref[c * ch:(c + 1) * ch, :] + b_ref[...]
            lf = jnp.minimum(z, 0.0) - jnp.log(1.0 + jnp.exp(-jnp.abs(z)))
            out_ref[c * ch:(c + 1) * ch, :] = _dot_ls(tri_ref[...], lf) + carry
            carry = carry + _csum(lf)

    return _rowcall("fgate_fwd", body, S, S, [(proj, LANES, fcol)], [bf_pad, tri],
                    [(LANES, F32)], [])[0]


def _fgate_bwd(dfk, dfq, proj, fcol, bf_pad, tri_u, nh, S, fw, into):
    ch = tri_u.shape[0]

    def body(d_ref, dq_ref, f_ref, b_ref, tri_ref, df_ref, db_ref):
        if fw > LANES:
            df_ref[:, LANES:fw] = jnp.zeros((S, fw - LANES), BF16)
        lane = lax.broadcasted_iota(jnp.int32, (ch, LANES), 1)
        carry = jnp.zeros((1, LANES), F32)
        tot = jnp.zeros((1, LANES), F32)
        for c in reversed(range(S // ch)):
            d = d_ref[c * ch:(c + 1) * ch, :] + dq_ref[c * ch:(c + 1) * ch, :]
            rc = _dot_ls(tri_ref[...], d) + carry
            carry = carry + _csum(d)
            z = f_ref[c * ch:(c + 1) * ch, :] + b_ref[...]
            df = jnp.where(lane < nh, rc * _sig(-z), 0.0)
            df_ref[c * ch:(c + 1) * ch, 0:LANES] = df.astype(BF16)
            tot = tot + _csum(df)
        db_ref[...] += tot

    return _rowcall("fgate_bwd", body, S, S, [(dfk, LANES, 0), (dfq, LANES, 0), (proj, LANES, fcol)],
                    [bf_pad, tri_u], [(fw, BF16)], [(1, LANES)], into=into)


def _keep(v, mask):
    return jnp.where(mask, v.astype(F32), 0.0).astype(BF16)


def _lane_col(blk, lane, at):
    return jnp.sum(jnp.where(lane == at, blk, 0.0), axis=-1, keepdims=True)


def _flash_fwd(qs, kn, vb, fk_r, S, D, hd, tq, ride=None):
    hp, nq = D // LANES, S // tq
    r_in, r_ispec, r_ospec, r_oshape, r_scratch, r_hook = _ride(ride, 4, 3)

    def kern(*refs):
        q_ref, k_ref, v_ref, fk_ref = refs[:4]
        o_ref, o32_ref, lse_ref = refs[4 + len(r_in):7 + len(r_in)]
        hi, qi = pl.program_id(0), pl.program_id(1)
        before, after = r_hook(refs, (hi == 0) & (qi == 0), (hi == hp // 2) & (qi == 0),
                               (hi == hp - 1) & (qi == nq - 1))
        before()
        lane = lax.broadcasted_iota(jnp.int32, (tq, LANES), 1)
        row = lax.broadcasted_iota(jnp.int32, (tq, tq), 0)
        col = lax.broadcasted_iota(jnp.int32, (tq, tq), 1)
        hms = [(lane >= j * hd) & (lane < (j + 1) * hd) for j in range(2)]
        q = q_ref[...]
        qms = [_keep(q, hm) for hm in hms]

        def step(ki, state, masked):
            off = pl.multiple_of(ki * tq, tq)
            k = k_ref[pl.ds(off, tq), :]
            v = v_ref[pl.ds(off, tq), :].astype(F32)
            new = []
            for j in range(2):
                m_old, acc = state[j]
                s = lax.dot_general(qms[j], k, (((1,), (1,)), ((), ())), preferred_element_type=F32)
                s = s - fk_ref[j, ki]
                if masked:
                    s = jnp.where(col <= row, s, NEG)
                m_new = jnp.maximum(m_old, jnp.max(s, axis=-1, keepdims=True))
                alpha = jnp.exp(m_old - m_new)
                p = jnp.exp(s - m_new)
                v1 = jnp.where(hms[j], v, 1.0).astype(BF16)
                acc = alpha * acc + jnp.dot(p.astype(BF16), v1, preferred_element_type=F32)
                new.append((m_new, acc))
            return tuple(new)

        init = tuple((jnp.full((tq, 1), NEG, F32), jnp.zeros((tq, LANES), F32)) for _ in range(2))
        state = lax.fori_loop(0, qi, lambda ki, st: step(ki, st, False), init)
        (m0, a0), (m1, a1) = step(qi, state, True)
        l0, l1 = pltpu.roll(a0, hd, 1), pltpu.roll(a1, hd, 1)
        first = lane < hd
        ov = jnp.where(first, a0 / l0, a1 / l1)
        o_ref[...] = ov.astype(BF16)
        o32_ref[...] = ov
        lse_ref[...] = jnp.where(first, m0 + jnp.log(l0), m1 + jnp.log(l1))
        after()

    qspec = pl.BlockSpec((tq, LANES), lambda h, i: (i, h))
    fullspec = pl.BlockSpec((S, LANES), lambda h, i: (0, h))
    return _pcall(
        kern, name="flash_fwd", grid=(hp, nq),
        in_specs=[qspec, fullspec, fullspec,
                  pl.BlockSpec((2, nq, 1, tq), lambda h, i: (h, 0, 0, 0))] + r_ispec,
        out_specs=[qspec, qspec, qspec] + r_ospec,
        out_shape=[jax.ShapeDtypeStruct((S, D), BF16), jax.ShapeDtypeStruct((S, D), F32),
                   jax.ShapeDtypeStruct((S, D), F32)] + r_oshape,
        scratch_shapes=r_scratch,
        compiler_params=_cparams(("arbitrary", "arbitrary")),
    )(qs, kn, vb, fk_r, *r_in)


def _flash_bwd(qs, kn, vb, do, fk_b, lse_r, delta_r, S, D, hd, tq, dv_into, ride=None):
    hp, nq = D // LANES, S // tq
    dbuf_hbm, dv_col = dv_into
    r_in, r_ispec, r_ospec, r_oshape, r_scratch, r_hook = _ride(ride, 8, 5)

    def kern(*refs):
        q_ref, do_ref, k_ref, v_ref, fk_ref, lse_ref, dl_ref = refs[:7]
        dk_ref, dv_ref, dq_ref, dfq_ref, dfk_ref = refs[8 + len(r_in):13 + len(r_in)]
        hi, ki = pl.program_id(0), pl.program_id(1)
        before, after = r_hook(refs, (hi == 0) & (ki == 0), (hi == hp // 2) & (ki == 0),
                               (hi == hp - 1) & (ki == nq - 1))
        before()
        lane = lax.broadcasted_iota(jnp.int32, (tq, LANES), 1)
        row = lax.broadcasted_iota(jnp.int32, (tq, tq), 0)
        col = lax.broadcasted_iota(jnp.int32, (tq, tq), 1)
        hms = [(lane >= j * hd) & (lane < (j + 1) * hd) for j in range(2)]
        k = k_ref[...]
        v = v_ref[...]
        fkb = fk_ref[...]
        kms = [_keep(k, hm) for hm in hms]
        vms = [_keep(v, hm) for hm in hms]
        fks = [_lane_col(fkb, lane, 2 * hi + j) for j in range(2)]

        @pl.when(ki == 0)
        def _():
            dfq_ref[...] = jnp.zeros(dfq_ref.shape, F32)
            dq_ref[...] = jnp.zeros(dq_ref.shape, F32)

        def step(qi, acc, masked):
            dk, dv, dfs = acc
            off = pl.multiple_of(qi * tq, tq)
            q = q_ref[pl.ds(off, tq), :]
            g = do_ref[pl.ds(off, tq), :]
            dq = None
            new_dfs = []
            for j in range(2):
                qm = _keep(q, hms[j])
                gm = _keep(g, hms[j])
                st = lax.dot_general(kms[j], q, (((1,), (1,)), ((), ())), preferred_element_type=F32)
                st = st - fks[j]
                if masked:
                    st = jnp.where(row <= col, st, NEG)
                pt = jnp.exp(st - lse_ref[j, qi])
                dv = dv + jnp.dot(pt.astype(BF16), gm, preferred_element_type=F32)
                dpt = lax.dot_general(vms[j], g, (((1,), (1,)), ((), ())), preferred_element_type=F32)
                dst = pt * (dpt - dl_ref[j, qi])
                dsb = dst.astype(BF16)
                dk = dk + jnp.dot(dsb, qm, preferred_element_type=F32)
                t = lax.dot_general(dsb, kms[j], (((0,), (0,)), ((), ())), preferred_element_type=F32)
                dq = t if dq is None else dq + t
                dfq_ref[j, qi] += jnp.sum(dst, axis=0, keepdims=True)
                new_dfs.append(dfs[j] - jnp.sum(dst, axis=1, keepdims=True))
            dq_ref[pl.ds(off, tq), :] += dq
            return dk, dv, tuple(new_dfs)

        zero = jnp.zeros((tq, LANES), F32)
        zcol = jnp.zeros((tq, 1), F32)
        acc = step(ki, (zero, zero, (zcol, zcol)), True)
        dk, dv, dfs = lax.fori_loop(ki + 1, nq, lambda qi, a: step(qi, a, False), acc)
        dk_ref[...] = dk.astype(BF16)
        dv_ref[...] = dv.astype(BF16)
        dfk_ref[...] = jnp.where(lane < hd, dfs[0], dfs[1])
        after()

    kspec = pl.BlockSpec((tq, LANES), lambda h, i: (i, h))
    fullspec = pl.BlockSpec((S, LANES), lambda h, i: (0, h))
    rowspec = pl.BlockSpec((2, nq, 1, tq), lambda h, i: (h, 0, 0, 0))
    return _pcall(
        kern, name="flash_bwd", grid=(hp, nq),
        in_specs=[fullspec, fullspec, kspec, kspec, pl.BlockSpec((tq, LANES), lambda h, i: (i, 0)),
                  rowspec, rowspec, ANY] + r_ispec,
        out_specs=[kspec, pl.BlockSpec((tq, LANES), lambda h, i: (i, h + dv_col)), fullspec, rowspec, kspec]
        + r_ospec,
        out_shape=[jax.ShapeDtypeStruct((S, D), BF16), jax.ShapeDtypeStruct(dbuf_hbm.shape, BF16),
                   jax.ShapeDtypeStruct((S, D), F32), jax.ShapeDtypeStruct((2 * hp, nq, 1, tq), F32),
                   jax.ShapeDtypeStruct((S, D), F32)] + r_oshape,
        scratch_shapes=r_scratch, input_output_aliases={7: 1},
        compiler_params=_cparams(("arbitrary", "arbitrary")),
    )(qs, do, kn, vb, fk_b, lse_r, delta_r, dbuf_hbm, *r_in)


def _delta_prep(do, o, lse_b, grp, sel, S, D):
    def body(g_ref, o_ref, l_ref, e_ref, s_ref, dl_ref, lse_ref):
        prod = g_ref[...].astype(F32) * o_ref[...]
        dl_ref[...] = _dot_rs(prod, e_ref[...], 2)
        lse_ref[...] = _dot_rs(l_ref[...], s_ref[...])
    return _rowcall("delta_prep", body, S, 256, [(do, D, 0), (o, D, 0), (lse_b, D, 0)], [grp, sel],
                    [(LANES, F32), (LANES, F32)], [])


def _qk_bwd(proj, dqs, dkn, gq, gk, grp, grp_t, S, D, hd, into):
    scale = hd ** -0.5

    def one(x, dn, gain, e, et):
        r = _head_rstd(x, e, et, hd)
        xh = x * r
        t = dn * gain
        mean = _dot_rs(_dot_rs(t * xh, e, 2), et, 2) * (1.0 / hd)
        return r * (t - xh * mean), _csum(dn * xh)

    def body(q_ref, k_ref, dq_ref, dk_ref, gq_ref, gk_ref, e_ref, et_ref, o_ref, sq_ref, sk_ref):
        e, et = e_ref[...], et_ref[...]
        dq, sq = one(q_ref[...], dq_ref[...].astype(F32) * scale, gq_ref[...], e, et)
        dk, sk = one(k_ref[...], dk_ref[...].astype(F32), gk_ref[...], e, et)
        o_ref[:, 0:D] = dq.astype(BF16)
        o_ref[:, D:2 * D] = dk.astype(BF16)
        sq_ref[...] += sq
        sk_ref[...] += sk

    return _rowcall("qk_bwd", body, S, 256,
                    [(proj, D, 0), (proj, D, 1), (dqs, D, 0), (dkn, D, 0)],
                    [gq, gk, grp, grp_t], [(2 * D, BF16)], [(1, D)] * 2, into=into)


def _shift_copies(buf, sh, ts):
    for b in range(1, SUBLANES):
        sh[b - 1] = buf[b:b + ts + HALO - SUBLANES, :]


def _rows_from(buf, sh, o, ts):
    a, b = divmod(o, SUBLANES)
    if b == 0:
        return buf[o:o + ts, :]
    return sh[b - 1, SUBLANES * a:SUBLANES * a + ts, :]


def _conv_fwd(proj, acol, bcol, w_pad, cb, lg, lb, S, C, taps, ts):
    ts = min(ts, S)

    def kern(a_ref, b_ref, w_ref, cb_ref, lg_ref, lb_ref, u1_ref, u3_ref, ubuf, ush):
        @pl.when(pl.program_id(0) == 0)
        def _():
            ubuf[0:HALO, :] = jnp.zeros((HALO, C), F32)

        ubuf[HALO:HALO + ts, :] = a_ref[...] * _sig(b_ref[...])
        _shift_copies(ubuf, ush, ts)
        acc = jnp.zeros((ts, C), F32) + cb_ref[...]
        for k in range(taps):
            acc = acc + w_ref[k:k + 1, :] * _rows_from(ubuf, ush, HALO - (taps - 1) + k, ts)
        u1_ref[...] = acc
        mu = jnp.mean(acc, axis=-1, keepdims=True)
        xc = acc - mu
        rstd = lax.rsqrt(jnp.mean(xc * xc, axis=-1, keepdims=True) + NORM_EPS)
        u2 = xc * rstd * lg_ref[...] + lb_ref[...]
        u3_ref[...] = (u2 * _sig(u2)).astype(BF16)
        ubuf[0:HALO, :] = ubuf[ts:ts + HALO, :]

    vec = lambda a: pl.BlockSpec(a.shape, lambda i: (0, 0))
    return _pcall(
        kern, name="conv_fwd", grid=(S // ts,),
        in_specs=[pl.BlockSpec((ts, C), lambda i: (i, acol)), pl.BlockSpec((ts, C), lambda i: (i, bcol)),
                  vec(w_pad), vec(cb), vec(lg), vec(lb)],
        out_specs=[pl.BlockSpec((ts, C), lambda i: (i, 0))] * 2,
        out_shape=[jax.ShapeDtypeStruct((S, C), F32), jax.ShapeDtypeStruct((S, C), BF16)],
        scratch_shapes=[pltpu.VMEM((HALO + ts, C), F32),
                        pltpu.VMEM((SUBLANES - 1, HALO + ts - SUBLANES, C), F32)],
        compiler_params=_cparams(("arbitrary",)),
    )(proj, proj, w_pad, cb, lg, lb)


def _conv_bwd(du3, u1, proj, acol, bcol, w_pad, lg, lb, S, C, taps, ts, into):
    ts = min(ts, S)
    dbuf_hbm, dcol = into
    nt = S // ts
    hb = ts // HALO

    def ln_bwd(g, u, lgv, lbv):
        mu = jnp.mean(u, axis=-1, keepdims=True)
        xc = u - mu
        rstd = lax.rsqrt(jnp.mean(xc * xc, axis=-1, keepdims=True) + NORM_EPS)
        xh = xc * rstd
        u2 = xh * lgv + lbv
        s = _sig(u2)
        du2 = g * (s + u2 * s * (1.0 - s))
        dxh = du2 * lgv
        du1 = rstd * (dxh - jnp.mean(dxh, axis=-1, keepdims=True)
                      - xh * jnp.mean(dxh * xh, axis=-1, keepdims=True))
        return du1, du2, xh

    def kern(g_ref, u_ref, a_ref, b_ref, gn_ref, un_ref, ap_ref, bp_ref, w_ref, lg_ref, lb_ref, _,
             dg_ref, dw_ref, dcb_ref, dlg_ref, dlb_ref, dbuf, ubuf, dsh, ush):
        i = pl.program_id(0)

        @pl.when(i == 0)
        def _():
            dw_ref[...] = jnp.zeros(dw_ref.shape, F32)
            dcb_ref[...] = jnp.zeros(dcb_ref.shape, F32)
            dlg_ref[...] = jnp.zeros(dlg_ref.shape, F32)
            dlb_ref[...] = jnp.zeros(dlb_ref.shape, F32)

        lgv, lbv = lg_ref[...], lb_ref[...]
        du1, du2, xh = ln_bwd(g_ref[...], u_ref[...], lgv, lbv)
        dbuf[0:ts, :] = du1
        du1n, _, _ = ln_bwd(gn_ref[...], un_ref[...], lgv, lbv)
        dbuf[ts:ts + HALO, :] = jnp.where(i < nt - 1, du1n, 0.0)
        a = a_ref[...]
        sb = _sig(b_ref[...])
        ubuf[HALO:HALO + ts, :] = a * sb
        ubuf[0:HALO, :] = jnp.where(i > 0, ap_ref[...] * _sig(bp_ref[...]), 0.0)
        dcb_ref[...] += _csum(du1)
        dlg_ref[...] += _csum(du2 * xh)
        dlb_ref[...] += _csum(du2)
        _shift_copies(dbuf, dsh, ts)
        _shift_copies(ubuf, ush, ts)
        for r0 in range(0, ts, CONV_ROWS):
            du0 = jnp.zeros((CONV_ROWS, C), F32)
            for k in range(taps):
                du0 = du0 + w_ref[k:k + 1, :] * _rows_from(dbuf, dsh, r0 + taps - 1 - k, CONV_ROWS)
            ac = a_ref[r0:r0 + CONV_ROWS, :]
            sc = _sig(b_ref[r0:r0 + CONV_ROWS, :])
            dg_ref[r0:r0 + CONV_ROWS, 0:C] = (du0 * sc).astype(BF16)
            dg_ref[r0:r0 + CONV_ROWS, C:2 * C] = (du0 * ac * sc * (1.0 - sc)).astype(BF16)
        for k0 in range(0, taps, CONV_TAPS):
            ks = range(k0, min(k0 + CONV_TAPS, taps))
            accs = [jnp.zeros((SUBLANES, C), F32) for _ in ks]
            for r0 in range(0, ts, CONV_ROWS):
                d = dbuf[r0:r0 + CONV_ROWS, :]
                for t, k in enumerate(ks):
                    prod = d * _rows_from(ubuf, ush, r0 + HALO - (taps - 1) + k, CONV_ROWS)
                    accs[t] = accs[t] + jnp.sum(prod.reshape(CONV_ROWS // SUBLANES, SUBLANES, C), axis=0)
            for t, k in enumerate(ks):
                dw_ref[k:k + 1, :] += _csum(accs[t])

    vec = lambda a: pl.BlockSpec(a.shape, lambda i: (0, 0))
    tile = lambda cb: pl.BlockSpec((ts, C), functools.partial(lambda i, cb: (i, cb), cb=cb))
    nxt = lambda cb: pl.BlockSpec(
        (HALO, C), functools.partial(lambda i, cb: (jnp.minimum((i + 1) * hb, nt * hb - 1), cb), cb=cb))
    prv = lambda cb: pl.BlockSpec(
        (HALO, C), functools.partial(lambda i, cb: (jnp.maximum(i * hb - 1, 0), cb), cb=cb))
    return _pcall(
        kern, name="conv_bwd", grid=(nt,),
        in_specs=[tile(0), tile(0), tile(acol), tile(bcol), nxt(0), nxt(0), prv(acol), prv(bcol),
                  vec(w_pad), vec(lg), vec(lb), ANY],
        out_specs=[pl.BlockSpec((ts, 2 * C), lambda i: (i, dcol))]
        + [pl.BlockSpec(w_pad.shape, lambda i: (0, 0))] + [pl.BlockSpec((1, C), lambda i: (0, 0))] * 3,
        out_shape=[jax.ShapeDtypeStruct(dbuf_hbm.shape, BF16)]
        + [jax.ShapeDtypeStruct(w_pad.shape, F32)] + [jax.ShapeDtypeStruct((1, C), F32)] * 3,
        scratch_shapes=[pltpu.VMEM((ts + HALO, C), F32), pltpu.VMEM((HALO + ts, C), F32)]
        + [pltpu.VMEM((SUBLANES - 1, HALO + ts - SUBLANES, C), F32)] * 2,
        input_output_aliases={11: 0},
        compiler_params=_cparams(("arbitrary",)),
    )(du3, u1, proj, proj, du3, u1, proj, proj, w_pad, lg, lb, dbuf_hbm)


def _gate_merge(proj, gacol, gbcol, ba, bb, S, D):
    def body(ga_ref, gb_ref, a_ref, b_ref, out_ref):
        out_ref[...] = (_sig(ga_ref[...]) * a_ref[...] + _sig(gb_ref[...]) * b_ref[...]).astype(BF16)
    return _rowcall("gate_merge", body, S, 512,
                    [(proj, D, gacol), (proj, D, gbcol), (ba, D, 0), (bb, D, 0)], [], [(D, BF16)], [])[0]


def _gate_bwd(dm, proj, gacol, gbcol, ba, bb, S, D, into):
    def body(dm_ref, ga_ref, gb_ref, a_ref, b_ref, dg_ref, da_ref, db_ref):
        dmv = dm_ref[...]
        sa, sb = _sig(ga_ref[...]), _sig(gb_ref[...])
        da_ref[...] = (dmv * sa).astype(BF16)
        db_ref[...] = (dmv * sb).astype(BF16)
        dg_ref[:, 0:D] = (dmv * a_ref[...] * sa * (1.0 - sa)).astype(BF16)
        dg_ref[:, D:2 * D] = (dmv * b_ref[...] * sb * (1.0 - sb)).astype(BF16)
    return _rowcall("gate_bwd", body, S, 512,
                    [(dm, D, 0), (proj, D, gacol), (proj, D, gbcol), (ba, D, 0), (bb, D, 0)], [],
                    [(2 * D, BF16), (D, BF16), (D, BF16)], [], into=into)


def _resid_norm2(x, mo, g1, g, sc, sh, S, D):
    def body(x_ref, mo_ref, g1_ref, g_ref, sc_ref, sh_ref, x1_ref, h_ref):
        x1 = x_ref[...] + g1_ref[...] * mo_ref[...]
        x1_ref[...] = x1
        r = lax.rsqrt(jnp.mean(x1 * x1, axis=-1, keepdims=True) + NORM_EPS)
        h_ref[...] = ((x1 * r * g_ref[...]) * (1.0 + sc_ref[...]) + sh_ref[...]).astype(BF16)
    return _rowcall("resid_norm2", body, S, 512, [(x, D, 0), (mo, D, 0)], [g1, g, sc, sh],
                    [(D, F32), (D, BF16)], [])


def _loss_dy(x1, ml, tgt, g2, S, D):
    def body(x1_ref, ml_ref, t_ref, g2_ref, dy_ref, dml_ref, sq_ref, dg2_ref):
        mlv = ml_ref[...]
        diff = x1_ref[...] + g2_ref[...] * mlv - t_ref[...]
        dy = diff * (1.0 / D)
        dy_ref[...] = dy
        dml_ref[...] = (dy * g2_ref[...]).astype(BF16)
        sq_ref[...] += _csum(diff * diff)
        dg2_ref[...] += _csum(dy * mlv)
    return _rowcall("loss_dy", body, S, 512, [(x1, D, 0), (ml, D, 0), (tgt, D, 0)], [g2],
                    [(D, F32), (D, BF16)], [(1, D), (1, D)])


def _norm_bwd(name, xin, dh, dres, g, sc, S, D, extra=None):
    def body(*refs):
        if extra is None:
            x_ref, dh_ref, dr_ref, g_ref, sc_ref, dx_ref, dsh_ref, dsc_ref, dg_ref = refs
        else:
            (x_ref, dh_ref, dr_ref, mo_ref, g_ref, sc_ref, g1_ref,
             dx_ref, dmo_ref, dsh_ref, dsc_ref, dg_ref, dg1_ref) = refs
        xv, dhv, gv = x_ref[...], dh_ref[...], g_ref[...]
        r = lax.rsqrt(jnp.mean(xv * xv, axis=-1, keepdims=True) + NORM_EPS)
        xh = xv * r
        dsh_ref[...] += _csum(dhv)
        dsc_ref[...] += _csum(dhv * xh * gv)
        dxg = dhv * (1.0 + sc_ref[...])
        dg_ref[...] += _csum(dxg * xh)
        dxh = dxg * gv
        dx = dr_ref[...] + r * (dxh - xh * jnp.mean(dxh * xh, axis=-1, keepdims=True))
        dx_ref[...] = dx
        if extra is not None:
            dmo_ref[...] = (dx * g1_ref[...]).astype(BF16)
            dg1_ref[...] += _csum(dx * mo_ref[...])

    rows = [(xin, D, 0), (dh, D, 0), (dres, D, 0)]
    vecs = [g, sc]
    if extra is None:
        return _rowcall(name, body, S, 512, rows, vecs, [(D, F32)], [(1, D)] * 3)
    return _rowcall(name, body, S, 512, rows + [(extra[0], D, 0)], vecs + [extra[1]],
                    [(D, F32), (D, BF16)], [(1, D)] * 4)


def _ada_fwd(c_all, w, b_part):
    B, D = c_all.shape
    N = w.shape[1]
    tn = min(512, N)

    def kern(c_ref, w_ref, b_ref, o_ref):
        cv = c_ref[...]
        ca = cv * _sig(cv)
        o_ref[...] = jnp.dot(ca, w_ref[...], precision=lax.Precision.HIGHEST,
                             preferred_element_type=F32) + b_ref[...]

    return _pcall(
        kern, name="ada_fwd", grid=(N // tn,),
        in_specs=[pl.BlockSpec((B, D), lambda j: (0, 0)), pl.BlockSpec((D, tn), lambda j: (0, j)),
                  pl.BlockSpec((1, tn), lambda j: (0, j))],
        out_specs=pl.BlockSpec((B, tn), lambda j: (0, j)),
        out_shape=jax.ShapeDtypeStruct((B, N), F32),
        compiler_params=_cparams(("parallel",)),
    )(c_all, w, b_part)


def _ada_wgrad(c_t_pad, dmod_pad):
    D = c_t_pad.shape[0]
    N = dmod_pad.shape[1]
    tn = min(512, N)

    def kern(c_ref, d_ref, o_ref):
        cv = c_ref[...]
        ca = cv * _sig(cv)
        o_ref[...] = jnp.dot(ca, d_ref[...], precision=lax.Precision.HIGHEST,
                             preferred_element_type=F32)

    return _pcall(
        kern, name="ada_wgrad", grid=(N // tn,),
        in_specs=[pl.BlockSpec((D, LANES), lambda j: (0, 0)), pl.BlockSpec((LANES, tn), lambda j: (0, j))],
        out_specs=pl.BlockSpec((D, tn), lambda j: (0, j)),
        out_shape=jax.ShapeDtypeStruct((D, N), F32),
        compiler_params=_cparams(("parallel",)),
    )(c_t_pad, dmod_pad)


def _ag_small(name, arrs):
    n = len(arrs)

    def kern(*refs):
        ins, outs = refs[:n], refs[n:2 * n]
        send, recv = refs[2 * n], refs[2 * n + 1]
        x, y, c = lax.axis_index("x"), lax.axis_index("y"), lax.axis_index("c")
        me = 4 * x + 2 * y + c

        def copy(i, m, slot):
            peer = (x ^ ((m >> 2) & 1), y ^ ((m >> 1) & 1), c ^ (m & 1))
            return pltpu.make_async_remote_copy(
                src_ref=ins[i], dst_ref=outs[i].at[slot],
                send_sem=send.at[i * 7 + m - 1], recv_sem=recv.at[i * 7 + m - 1],
                device_id=peer, device_id_type=MESH)

        for i in range(n):
            outs[i][me] = ins[i][...]
            for m in range(1, 8):
                copy(i, m, me).start()
        for i in range(n):
            for m in range(1, 8):
                copy(i, m, me).wait_send()
                copy(i, m, me ^ m).wait_recv()

    vm = pl.BlockSpec(memory_space=pltpu.VMEM)
    return _pcall(
        kern, name=name, in_specs=[vm] * n, out_specs=[vm] * n,
        out_shape=[jax.ShapeDtypeStruct((8,) + a.shape, a.dtype) for a in arrs],
        scratch_shapes=[pltpu.SemaphoreType.DMA((7 * n,)), pltpu.SemaphoreType.DMA((7 * n,))],
        compiler_params=pltpu.CompilerParams(has_side_effects=True),
    )(*arrs)


def _exchange(name, arrs, plan):
    out_shape, scratch, phases = plan(arrs)

    def kern(*refs):
        for phase in phases(refs):
            phase()

    return _pcall(
        kern, name=name, in_specs=[ANY] * len(arrs), out_specs=[ANY] * len(out_shape),
        out_shape=out_shape, scratch_shapes=scratch,
        compiler_params=pltpu.CompilerParams(has_side_effects=True),
    )(*arrs)


def _ride(plan_and_arrs, n_in, n_out):
    if plan_and_arrs is None:
        return [], [], [], [], [], lambda refs, first, middle, last: ((lambda: None), (lambda: None))
    plan, arrs = plan_and_arrs
    out_shape, scratch, phases = plan(arrs)
    na, no = len(arrs), len(out_shape)

    def hook(refs, first, middle, last):
        mine = refs[n_in:n_in + na] + refs[n_in + na + n_out:]
        start, mid, finish = phases(mine)

        def before():
            pl.when(first)(start)
            pl.when(middle)(mid)

        def after():
            pl.when(last)(finish)

        return before, after

    return list(arrs), [ANY] * na, [ANY] * no, out_shape, scratch, hook


def _gather_plan(arrs):
    n = len(arrs)

    def phases(refs):
        ins, outs = refs[:n], refs[n:2 * n]
        s1, r1, s2, r2, loc = refs[2 * n:2 * n + 5]
        x, y, c = lax.axis_index("x"), lax.axis_index("y"), lax.axis_index("c")
        me = 2 * x + y

        def half(i, hc):
            hr = ins[i].shape[0] // 2
            return pl.ds(hc * hr, hr)

        def own(i):
            return pltpu.make_async_remote_copy(
                src_ref=ins[i], dst_ref=outs[i].at[me], send_sem=loc.at[i], recv_sem=loc.at[n + i],
                device_id=(x, y, 1 - c), device_id_type=MESH)

        def fetch(i, m, slot):
            px, py = x ^ ((m >> 1) & 1), y ^ (m & 1)
            return pltpu.make_async_remote_copy(
                src_ref=ins[i].at[half(i, c)], dst_ref=outs[i].at[slot, half(i, c)],
                send_sem=s1.at[i * 3 + m - 1], recv_sem=r1.at[i * 3 + m - 1],
                device_id=(px, py, c), device_id_type=MESH)

        def passed(i, m, hc):
            return pltpu.make_async_remote_copy(
                src_ref=outs[i].at[me ^ m, half(i, hc)], dst_ref=outs[i].at[me ^ m, half(i, hc)],
                send_sem=s2.at[i * 3 + m - 1], recv_sem=r2.at[i * 3 + m - 1],
                device_id=(x, y, 1 - c), device_id_type=MESH)

        def start():
            for i in range(n):
                for m in range(1, 4):
                    fetch(i, m, me).start()
            for i in range(n):
                own(i).start()

        def mid():
            for i in range(n):
                for m in range(1, 4):
                    fetch(i, m, me ^ m).wait_recv()
                    passed(i, m, c).start()

        def finish():
            for i in range(n):
                own(i).wait()
                for m in range(1, 4):
                    fetch(i, m, me).wait_send()
                    passed(i, m, c).wait_send()
                    passed(i, m, 1 - c).wait_recv()

        return start, mid, finish

    out_shape = [jax.ShapeDtypeStruct((4,) + a.shape, a.dtype) for a in arrs]
    scratch = [pltpu.SemaphoreType.DMA((3 * n,))] * 4 + [pltpu.SemaphoreType.DMA((2 * n,))]
    return out_shape, scratch, phases


def _pair_send_halves(name, arrs):
    n = len(arrs)

    def kern(*refs):
        ins, outs = refs[:n], refs[n:2 * n]
        send, recv = refs[2 * n], refs[2 * n + 1]
        x, y, c = lax.axis_index("x"), lax.axis_index("y"), lax.axis_index("c")

        def copy(i, k, hc):
            return pltpu.make_async_remote_copy(
                src_ref=ins[i].at[k, hc], dst_ref=outs[i].at[k],
                send_sem=send.at[i * 4 + k], recv_sem=recv.at[i * 4 + k],
                device_id=(x, y, 1 - c), device_id_type=MESH)

        for i in range(n):
            for k in range(4):
                copy(i, k, 1 - c).start()
        for i in range(n):
            for k in range(4):
                copy(i, k, 1 - c).wait()

    return _pcall(
        kern, name=name, in_specs=[ANY] * n, out_specs=[ANY] * n,
        out_shape=[jax.ShapeDtypeStruct((4,) + a.shape[2:], a.dtype) for a in arrs],
        scratch_shapes=[pltpu.SemaphoreType.DMA((4 * n,)), pltpu.SemaphoreType.DMA((4 * n,))],
        compiler_params=pltpu.CompilerParams(has_side_effects=True),
    )(*arrs)


def _scatter_plan(arrs):
    n = len(arrs)

    def phases(refs):
        ins, outs = refs[:n], refs[n:2 * n]
        send, recv = refs[2 * n], refs[2 * n + 1]
        x, y, c = lax.axis_index("x"), lax.axis_index("y"), lax.axis_index("c")
        me = 2 * x + y

        def copy(i, m, slot):
            px, py = x ^ ((m >> 1) & 1), y ^ (m & 1)
            return pltpu.make_async_remote_copy(
                src_ref=ins[i].at[2 * px + py], dst_ref=outs[i].at[slot],
                send_sem=send.at[i * 3 + m - 1], recv_sem=recv.at[i * 3 + m - 1],
                device_id=(px, py, c), device_id_type=MESH)

        def start():
            for i in range(n):
                for m in range(1, 4):
                    copy(i, m, me).start()

        def finish():
            for i in range(n):
                for m in range(1, 4):
                    copy(i, m, me).wait_send()
                    copy(i, m, me ^ m).wait_recv()

        return start, (lambda: None), finish

    out_shape = [jax.ShapeDtypeStruct(a.shape, a.dtype) for a in arrs]
    scratch = [pltpu.SemaphoreType.DMA((3 * n,)), pltpu.SemaphoreType.DMA((3 * n,))]
    return out_shape, scratch, phases


def _pair_swap(name, arrs):
    n = len(arrs)

    def kern(*refs):
        ins, outs = refs[:n], refs[n:2 * n]
        send, recv = refs[2 * n], refs[2 * n + 1]
        x, y, c = lax.axis_index("x"), lax.axis_index("y"), lax.axis_index("c")

        def copy(i):
            return pltpu.make_async_remote_copy(
                src_ref=ins[i], dst_ref=outs[i], send_sem=send.at[i], recv_sem=recv.at[i],
                device_id=(x, y, 1 - c), device_id_type=MESH)

        for i in range(n):
            copy(i).start()
        for i in range(n):
            copy(i).wait()

    return _pcall(
        kern, name=name, in_specs=[ANY] * n, out_specs=[ANY] * n,
        out_shape=[jax.ShapeDtypeStruct(a.shape, a.dtype) for a in arrs],
        scratch_shapes=[pltpu.SemaphoreType.DMA((n,)), pltpu.SemaphoreType.DMA((n,))],
        compiler_params=pltpu.CompilerParams(has_side_effects=True),
    )(*arrs)


def _row_tile(R):
    for t in (256, 128, 64, 32, 16, 8):
        if R % t == 0:
            return t
    return R


def _sum_slots(name, parts):
    K, R, C = parts.shape
    tr = _row_tile(R)

    def kern(p_ref, o_ref):
        acc = p_ref[0].astype(F32)
        for k in range(1, K):
            acc = acc + p_ref[k].astype(F32)
        o_ref[...] = acc

    return _pcall(
        kern, name=name, grid=(R // tr,),
        in_specs=[pl.BlockSpec((K, tr, C), lambda i: (0, i, 0))],
        out_specs=pl.BlockSpec((tr, C), lambda i: (i, 0)),
        out_shape=jax.ShapeDtypeStruct((R, C), F32),
        compiler_params=_cparams(("parallel",)),
    )(parts)


def _sum_pair(name, core, mine, theirs):
    K, _, hr, C = mine.shape
    tr = _row_tile(hr)

    def kern(c_ref, a_ref, b_ref, o_ref):
        o_ref[0] = (a_ref[0, 0].astype(F32) + b_ref[0].astype(F32)).astype(BF16)

    return _pcall(
        kern, name=name, out_shape=jax.ShapeDtypeStruct((K, hr, C), BF16),
        grid_spec=pltpu.PrefetchScalarGridSpec(
            num_scalar_prefetch=1, grid=(K, hr // tr),
            in_specs=[pl.BlockSpec((1, 1, tr, C), lambda k, r, c_ref: (k, c_ref[0], r, 0)),
                      pl.BlockSpec((1, tr, C), lambda k, r, c_ref: (k, r, 0))],
            out_specs=pl.BlockSpec((1, tr, C), lambda k, r, c_ref: (k, r, 0))),
        compiler_params=_cparams(("parallel", "parallel")),
    )(core, mine, theirs)


def _sum_chips(name, chip, own, recv):
    K, hr, C = own.shape
    tr = _row_tile(hr)

    def kern(chip_ref, own_ref, *rest):
        r_refs, o_ref = rest[:K], rest[K]
        me = chip_ref[0]
        mine = own_ref[0].astype(F32)
        acc = None
        for k in range(K):
            t = jnp.where(me == k, mine, r_refs[k][0].astype(F32))
            acc = t if acc is None else acc + t
        o_ref[...] = acc

    def other(k):
        return pl.BlockSpec((1, tr, C), lambda r, s: (jnp.where(s[0] == k, (k + 1) % K, k), r, 0))

    return _pcall(
        kern, name=name, out_shape=jax.ShapeDtypeStruct((hr, C), F32),
        grid_spec=pltpu.PrefetchScalarGridSpec(
            num_scalar_prefetch=1, grid=(hr // tr,),
            in_specs=[pl.BlockSpec((1, tr, C), lambda r, s: (s[0], r, 0))] + [other(k) for k in range(K)],
            out_specs=pl.BlockSpec((tr, C), lambda r, s: (r, 0))),
        compiler_params=_cparams(("parallel",)),
    )(chip, own, *([recv] * K))


def _adam_update(w, m, v, g):
    c1 = 1.0 - ADAM_B1 ** ADAM_STEP
    c2 = 1.0 - ADAM_B2 ** ADAM_STEP
    mn = ADAM_B1 * m + (1.0 - ADAM_B1) * g
    vn = ADAM_B2 * v + (1.0 - ADAM_B2) * (g * g)
    return -ADAM_LR * ((mn / c1) / (jnp.sqrt(vn / c2) + ADAM_EPS) + ADAM_WD * w), mn, vn


def _adamw_halves(name, core, w, m, v, mine, theirs):
    R, C = w.shape
    hr = mine.shape[0]
    tr = _row_tile(hr)
    nbh = hr // tr

    def kern(c_ref, w_ref, m_ref, v_ref, a_ref, b_ref, go_ref, d_ref, mo_ref, vo_ref):
        g = jnp.where(pl.program_id(0) // nbh == c_ref[0], a_ref[...], b_ref[...])
        d, mn, vn = _adam_update(w_ref[...], m_ref[...], v_ref[...], g)
        go_ref[...] = g
        d_ref[...] = d
        mo_ref[...] = mn
        vo_ref[...] = vn

    spec = pl.BlockSpec((tr, C), lambda i, s: (i, 0))
    hspec = pl.BlockSpec((tr, C), lambda i, s: (i % nbh, 0))
    return _pcall(
        kern, name=name, out_shape=[jax.ShapeDtypeStruct((R, C), F32)] * 4,
        grid_spec=pltpu.PrefetchScalarGridSpec(
            num_scalar_prefetch=1, grid=(R // tr,),
            in_specs=[spec, spec, spec, hspec, hspec], out_specs=[spec] * 4),
        compiler_params=_cparams(("parallel",)),
    )(core, w, m, v, mine, theirs)


def _adamw_many(name, wmvg):
    n = len(wmvg[0])

    def kern(*refs):
        ins, outs = refs[:4 * n], refs[4 * n:]
        for j in range(n):
            g = ins[3 * n + j][...]
            d, mn, vn = _adam_update(ins[j][...], ins[n + j][...], ins[2 * n + j][...], g)
            for i, val in enumerate((g, d, mn, vn)):
                outs[i * n + j][...] = val

    vm = pl.BlockSpec(memory_space=pltpu.VMEM)
    flat = [a for group in wmvg for a in group]
    outs = _pcall(
        kern, name=name, in_specs=[vm] * (4 * n), out_specs=[vm] * (4 * n),
        out_shape=[jax.ShapeDtypeStruct(a.shape, F32) for _ in range(4) for a in wmvg[0]],
    )(*flat)
    return [outs[i * n:(i + 1) * n] for i in range(4)]


def _adamw(name, w, m, v, gparts):
    R, C = w.shape
    K = gparts.shape[0]
    tr = _row_tile(R)

    def kern(w_ref, m_ref, v_ref, g_ref, go_ref, d_ref, mo_ref, vo_ref):
        g = g_ref[0]
        for k in range(1, K):
            g = g + g_ref[k]
        d, mn, vn = _adam_update(w_ref[...], m_ref[...], v_ref[...], g)
        go_ref[...] = g
        d_ref[...] = d
        mo_ref[...] = mn
        vo_ref[...] = vn

    spec = pl.BlockSpec((tr, C), lambda i: (i, 0))
    return _pcall(
        kern, name=name, grid=(R // tr,),
        in_specs=[spec, spec, spec, pl.BlockSpec((K, tr, C), lambda i: (0, i, 0))],
        out_specs=[spec] * 4,
        out_shape=[jax.ShapeDtypeStruct((R, C), F32)] * 4,
        compiler_params=_cparams(("parallel",)),
    )(w, m, v, gparts)


def _round_up(a, b):
    return (a + b - 1) // b * b


def kernel(x, c, w_ada, b_ada, norm1_g, w_in, b_forget, q_norm_g, k_norm_g, w_attn_proj, conv_w, conv_b, conv_ln_g, conv_ln_b, w_conv_proj, w_out, norm2_g, w_mlp1, w_mlp2, loss_target, m_w_ada, m_b_ada, m_norm1_g, m_w_in, m_b_forget, m_q_norm_g, m_k_norm_g, m_w_attn_proj, m_conv_w, m_conv_b, m_conv_ln_g, m_conv_ln_b, m_w_conv_proj, m_w_out, m_norm2_g, m_w_mlp1, m_w_mlp2, v_w_ada, v_b_ada, v_norm1_g, v_w_in, v_b_forget, v_q_norm_g, v_k_norm_g, v_w_attn_proj, v_conv_w, v_conv_b, v_conv_ln_g, v_conv_ln_b, v_w_conv_proj, v_w_out, v_norm2_g, v_w_mlp1, v_w_mlp2):
    S, D = x.shape[1], x.shape[2]
    NH, HD = b_forget.shape[-1], q_norm_g.shape[-1]
    TAPS = conv_w.shape[1]
    DIN_S = w_in.shape[-1]
    DIN = 4 * DIN_S
    DFF_S = w_mlp1.shape[-1]
    DFF = 4 * DFF_S
    ADA_S = w_ada.shape[-1]
    DS = w_attn_proj.shape[1]
    CS = conv_w.shape[-1]
    assert NH * HD == D and DIN == 7 * D + NH and TAPS - 1 <= HALO and D % LANES == 0 and 2 * HD == LANES
    NP = _round_up(7 * D + LANES, 512)
    FW = NP - 7 * D
    assert (7 * D) % FW == 0
    TQ = min(512, S)
    NQ = S // TQ
    FCOL = 7 * D // LANES

    xi, yi, ci = lax.axis_index("x"), lax.axis_index("y"), lax.axis_index("c")
    chip = 2 * xi + yi
    dev = 4 * xi + 2 * yi + ci

    x2 = x.reshape(S, D)
    tgt = loss_target.reshape(S, D)

    lane_head = jnp.arange(D, dtype=jnp.int32) // HD
    grp = (lane_head[:, None] == jnp.arange(LANES, dtype=jnp.int32)[None, :]).astype(BF16)
    grp_t = grp.T
    sel = ((jnp.arange(D, dtype=jnp.int32)[:, None] == HD * jnp.arange(LANES, dtype=jnp.int32)[None, :])
           .astype(BF16))
    ch = min(256, S)
    ii = jnp.arange(ch, dtype=jnp.int32)
    tri = (ii[None, :] <= ii[:, None]).astype(BF16)
    tri_u = tri.T
    gq_t = jnp.tile(q_norm_g.reshape(1, HD), (1, NH))
    gk_t = jnp.tile(k_norm_g.reshape(1, HD), (1, NH))
    bf_pad = jnp.pad(b_forget.reshape(1, NH), ((0, 0), (0, LANES - NH)))

    c_all, cw_all = _ag_small(
        "ag_c_convw", [c.reshape(1, D), jnp.pad(conv_w.reshape(TAPS, CS), ((0, HALO - TAPS), (0, 0)))])
    c_all = c_all.reshape(8, D)
    b_part = lax.dynamic_slice(b_ada.reshape(1, -1), (0, chip * ADA_S), (1, ADA_S))
    mod_part = _ada_fwd(c_all, w_ada.reshape(D, ADA_S), b_part)
    (mod_all,) = _ag_small("ag_mod", [mod_part])
    mod_full = jnp.concatenate([mod_all[0], mod_all[2], mod_all[4], mod_all[6]], axis=1)
    mod = lax.dynamic_slice(mod_full, (dev, 0), (1, 6 * D))
    sh1, sc1, g1, sh2, sc2, g2 = [mod[:, i * D:(i + 1) * D] for i in range(6)]

    shards = [w_in.reshape(D, DIN_S), w_attn_proj.reshape(DS, D), w_conv_proj.reshape(DS, D),
              w_out.reshape(DS, D), w_mlp1.reshape(D, DFF_S), w_mlp2.reshape(DFF_S, D)]
    shards = [s.astype(BF16) for s in shards]
    (gw_in,) = _exchange("ag_w_in", shards[:1], _gather_plan)
    w_conv = jnp.concatenate([cw_all[0], cw_all[2], cw_all[4], cw_all[6]], axis=1)

    SEGS = [(0, 2 * D, 0), (3 * D + NH, DIN, 2 * D), (2 * D, 3 * D + NH, 6 * D)]

    def pieces(a, b):
        out = []
        for k in range(4):
            lo, hi = max(a, k * DIN_S), min(b, (k + 1) * DIN_S)
            if lo < hi:
                out.append(gw_in[k][:, lo - k * DIN_S:hi - k * DIN_S])
        return out

    w_in_p = jnp.concatenate([p for (a, b, _) in SEGS for p in pieces(a, b)]
                             + [jnp.zeros((D, NP - 7 * D - NH), BF16)], axis=1)

    n1g = norm1_g.reshape(1, D)
    n2g = norm2_g.reshape(1, D)
    h = _norm_mod("norm_mod1", x2, n1g, sc1, sh1, S, D)
    proj = _mm("mm_in", h, w_in_p, "nn", [F32])
    qs, kn, vb = _qk_prep(proj, 6, gq_t, gk_t, grp, grp_t, S, D, HD)
    f_cum = _fgate_fwd(proj, FCOL, bf_pad, tri, S)
    fk_c = f_cum[:, :NH]
    fk_r = fk_c.T.reshape(NH, NQ, 1, TQ)
    o, o32, lse_b, gw_ap, gw_cp, gw_out, gw_m1, gw_m2 = _flash_fwd(
        qs, kn, vb, fk_r, S, D, HD, TQ, ride=(_gather_plan, shards[1:]))
    w_ap = gw_ap.reshape(D, D)
    w_cp = gw_cp.reshape(D, D)
    w_o = gw_out.reshape(D, D)
    w_m1 = jnp.transpose(gw_m1, (1, 0, 2)).reshape(D, DFF)
    w_m2 = gw_m2.reshape(DFF, D)
    br_a = _mm("mm_attn_proj", o, w_ap, "nn", [F32])
    cb, clg, clb = conv_b.reshape(1, D), conv_ln_g.reshape(1, D), conv_ln_b.reshape(1, D)
    u1, u3 = _conv_fwd(proj, 2, 3, w_conv, cb, clg, clb, S, D, TAPS, 256)
    br_b = _mm("mm_conv_proj", u3, w_cp, "nn", [F32])
    merged = _gate_merge(proj, 4, 5, br_a, br_b, S, D)
    mo = _mm("mm_out", merged, w_o, "nn", [F32])
    x1, h2 = _resid_norm2(x2, mo, g1, n2g, sc2, sh2, S, D)

    def relu2(r):
        rp = jnp.maximum(r, 0.0)
        return (rp * rp,)
    z = _mm("mm_mlp1", h2, w_m1, "nn", [BF16], epi=relu2)
    ml = _mm("mm_mlp2", z, w_m2, "nn", [F32])
    dy, dml, sq, dg2 = _loss_dy(x1, ml, tgt, g2, S, D)
    loss_part = jnp.full((1, LANES), 0.5 * jnp.sum(sq) / D, F32)

    da = _mm("mm_dz", dml, w_m2, "nt", [BF16], epi=lambda r, zz: (r * 2.0 * jnp.sqrt(zz.astype(F32)),),
             extras=(z,))
    dw_m2 = _mm("mm_dw_mlp2", z, dml, "tn", [BF16])
    dw_m1 = _mm("mm_dw_mlp1", h2, da, "tn", [BF16])
    dh2 = _mm("mm_dh2", da, w_m1, "nt", [F32])
    dx1, dmo, dsh2, dsc2, dn2g, dg1 = _norm_bwd("norm2_bwd", x1, dh2, dy, n2g, sc2, S, D, extra=(mo, g1))
    dmerged = _mm("mm_dmerged", dmo, w_o, "nt", [F32])
    dw_o = _mm("mm_dw_out", merged, dmo, "tn", [BF16])
    dproj, dba, dbb = _gate_bwd(dmerged, proj, 4, 5, br_a, br_b, S, D, into=(lax.empty((S, NP), BF16), 2))
    do = _mm("mm_do", dba, w_ap, "nt", [BF16])
    dw_ap = _mm("mm_dw_attn_proj", o, dba, "tn", [BF16])
    du3 = _mm("mm_du3", dbb, w_cp, "nt", [F32])
    dw_cp = _mm("mm_dw_conv_proj", u3, dbb, "tn", [BF16])
    dproj, dcw, dcb, dclg, dclb = _conv_bwd(du3, u1, proj, 2, 3, w_conv, clg, clb, S, D, TAPS, 256,
                                            into=(dproj, 1))

    core = ci.astype(jnp.int32).reshape(1)
    chip1 = chip.astype(jnp.int32).reshape(1)
    halves = lambda p: p.astype(BF16).reshape(4, 2, p.shape[1] // 2, p.shape[2])
    names = ["w_in", "w_attn_proj", "w_conv_proj", "w_out", "w_mlp1", "w_mlp2"]
    parts = [halves(p) for p in (dw_ap.reshape(4, DS, D), dw_cp.reshape(4, DS, D), dw_o.reshape(4, DS, D),
                                 jnp.transpose(dw_m1.reshape(D, 4, DFF_S), (1, 0, 2)), dw_m2.reshape(4, DFF_S, D))]
    theirs = _pair_send_halves("rs_pair", parts)
    chip_parts = [_sum_pair("sum_pair_" + nm, core, p, t) for nm, p, t in zip(names[1:], parts, theirs)]

    delta_c, lse_c = _delta_prep(do, o32, lse_b, grp, sel, S, D)
    to_rows = lambda t: t[:, :NH].T.reshape(NH, NQ, 1, TQ)
    dkn, dproj, dqs, dfq_r, dfk_b, *recvd = _flash_bwd(
        qs, kn, vb, do, f_cum, to_rows(lse_c), to_rows(delta_c), S, D, HD, TQ,
        dv_into=(dproj, 6 * D // LANES), ride=(_scatter_plan, chip_parts))
    dproj, sq_q, sq_k = _qk_bwd(proj, dqs, dkn, gq_t, gk_t, grp, grp_t, S, D, HD, into=(dproj, 0))
    to_cols = lambda r: jnp.pad(r.reshape(NH, S).T, ((0, 0), (0, LANES - NH)))
    dfq_pad = to_cols(dfq_r)
    dfk_pad = jnp.pad(dfk_b[:, ::HD], ((0, 0), (0, LANES - NH)))
    dproj, dbf = _fgate_bwd(dfk_pad, dfq_pad, proj, FCOL, bf_pad, tri_u, NH, S, FW, into=(dproj, 7 * D // FW))
    dw_in_p = _mm("mm_dw_in", h, dproj, "tn", [BF16])
    def shard_cols(k):
        out = []
        for (a, b, start) in sorted(SEGS):
            lo, hi = max(a, k * DIN_S), min(b, (k + 1) * DIN_S)
            if lo < hi:
                out.append(dw_in_p[:, start + lo - a:start + hi - a])
        return jnp.concatenate(out, axis=1)

    part_in = halves(jnp.stack([shard_cols(k) for k in range(4)]))
    (their_in,) = _pair_send_halves("rs_pair_w_in", [part_in])
    chip_in = _sum_pair("sum_pair_w_in", core, part_in, their_in)
    dh, recv_in = _mm("mm_dh", dproj, w_in_p, "nt", [F32], ride=(_scatter_plan, [chip_in]))
    gx, dsh1, dsc1, dn1g = _norm_bwd("norm1_bwd", x2, dh, dx1, n1g, sc1, S, D)

    packed = jnp.concatenate([dsh1, dsc1, dg1, dsh2, dsc2, dg2, dn1g, dcb, dclg, dclb, dn2g,
                              sq_q, sq_k, dbf, loss_part], axis=1)
    small_all, dcw_all = _ag_small("ag_small_grads", [packed, dcw])
    small = _sum_slots("sum_small", small_all.reshape(8, 1, -1)).reshape(1, -1)
    dmod_sum = small[:, :6 * D]
    seg = lambda k: small[:, (6 + k) * D:(7 + k) * D]
    g_n1g, g_cb, g_clg, g_clb, g_n2g = seg(0), seg(1), seg(2), seg(3), seg(4)
    g_qn = _sum_slots("sum_qn", seg(5).reshape(NH, 1, HD))
    g_kn = _sum_slots("sum_kn", seg(6).reshape(NH, 1, HD))
    g_bf = small[:, 13 * D:13 * D + NH]
    loss = small[0, 13 * D + LANES]
    dcw_mine = lax.dynamic_slice(dcw_all[:, :TAPS, :], (0, 0, chip * CS), (8, TAPS, CS))

    dmod_all = small_all.reshape(8, -1)[:, :6 * D]
    dmod_cols = lax.dynamic_slice(dmod_all, (0, chip * ADA_S), (8, ADA_S))
    c_t_pad = jnp.pad(c_all.T, ((0, 0), (0, LANES - 8)))
    g_wada = _ada_wgrad(c_t_pad, jnp.pad(dmod_cols, ((0, LANES - 8), (0, 0))))

    sums =[_sum_chips("sum_" + nm, chip1, p, r)
            for nm, p, r in zip(names, [chip_in] + chip_parts, [recv_in] + list(recvd))]
    others = _pair_swap("pair_grads", sums)

    res = {}
    big = {nm: (a, b) for nm, a, b in zip(names, sums, others)}
    big_w = {"w_in": (w_in, m_w_in, v_w_in), "w_attn_proj": (w_attn_proj, m_w_attn_proj, v_w_attn_proj),
             "w_conv_proj": (w_conv_proj, m_w_conv_proj, v_w_conv_proj), "w_out": (w_out, m_w_out, v_w_out),
             "w_mlp1": (w_mlp1, m_w_mlp1, v_w_mlp1), "w_mlp2": (w_mlp2, m_w_mlp2, v_w_mlp2)}
    for nm in names:
        shp = big_w[nm][0].shape
        outs = _adamw_halves("adamw_" + nm, core, *[t.reshape(shp[1], shp[2]) for t in big_w[nm]], *big[nm])
        res[nm] = [t.reshape(shp) for t in outs]
    outs = _adamw("adamw_w_ada", w_ada.reshape(D, ADA_S), m_w_ada.reshape(D, ADA_S),
                  v_w_ada.reshape(D, ADA_S), g_wada.reshape(1, D, ADA_S))
    res["w_ada"] = [t.reshape(w_ada.shape) for t in outs]
    outs = _adamw("adamw_conv_w", conv_w.reshape(TAPS, CS), m_conv_w.reshape(TAPS, CS),
                  v_conv_w.reshape(TAPS, CS), dcw_mine)
    res["conv_w"] = [t.reshape(conv_w.shape) for t in outs]

    small_w = [("b_ada", b_ada, m_b_ada, v_b_ada, dmod_sum), ("norm1_g", norm1_g, m_norm1_g, v_norm1_g, g_n1g),
               ("b_forget", b_forget, m_b_forget, v_b_forget, g_bf),
               ("q_norm_g", q_norm_g, m_q_norm_g, v_q_norm_g, g_qn),
               ("k_norm_g", k_norm_g, m_k_norm_g, v_k_norm_g, g_kn),
               ("conv_b", conv_b, m_conv_b, v_conv_b, g_cb), ("conv_ln_g", conv_ln_g, m_conv_ln_g, v_conv_ln_g, g_clg),
               ("conv_ln_b", conv_ln_b, m_conv_ln_b, v_conv_ln_b, g_clb),
               ("norm2_g", norm2_g, m_norm2_g, v_norm2_g, g_n2g)]
    outs = _adamw_many("adamw_small", [[t[i].reshape(1, -1) for t in small_w] for i in (1, 2, 3, 4)])
    for j, (nm, w_, _, _, _) in enumerate(small_w):
        res[nm] = [outs[i][j].reshape(w_.shape) for i in range(4)]

    order = ["w_ada", "b_ada", "norm1_g", "w_in", "b_forget", "q_norm_g", "k_norm_g", "w_attn_proj", "conv_w",
             "conv_b", "conv_ln_g", "conv_ln_b", "w_conv_proj", "w_out", "norm2_g", "w_mlp1", "w_mlp2"]
    return (loss, gx.reshape(x.shape), *[res[n][0] for n in order], *[res[n][1] for n in order],
            *[res[n][2] for n in order], *[res[n][3] for n in order])
```

```python
import functools

import jax
import jax.numpy as jnp
from jax import lax
from jax.experimental import pallas as pl
from jax.experimental.pallas import tpu as pltpu

F32 = jnp.float32
BF16 = jnp.bfloat16
MESH = pl.DeviceIdType.MESH
ANY = pl.BlockSpec(memory_space=pl.ANY)

NORM_EPS = 1e-6
ADAM_LR = 0.001
ADAM_B1 = 0.9
ADAM_B2 = 0.999
ADAM_EPS = 1e-08
ADAM_WD = 0.01
ADAM_STEP = 10
LANES = 128
SUBLANES = 8
HALO = 32
CONV_ROWS = 32
CONV_TAPS = 4
NEG = -1e30
VMEM_LIMIT = 56 * 1024 * 1024


def _pcall(body, **kw):
    return pl.pallas_call(body, **kw)


def _cparams(sem=None):
    if sem is None:
        return pltpu.CompilerParams(vmem_limit_bytes=VMEM_LIMIT)
    return pltpu.CompilerParams(dimension_semantics=sem, vmem_limit_bytes=VMEM_LIMIT)


def _sig(x):
    return 1.0 / (1.0 + jnp.exp(-x))


def _split3(x):
    x1 = x.astype(BF16)
    r = x - x1.astype(F32)
    x2 = r.astype(BF16)
    x3 = (r - x2.astype(F32)).astype(BF16)
    return x1, x2, x3


def _dot_rs(x, e, terms=3):
    out = None
    for t in _split3(x)[:terms]:
        d = jnp.dot(t, e, preferred_element_type=F32)
        out = d if out is None else out + d
    return out


def _dot_ls(e, x):
    out = None
    for t in _split3(x):
        d = jnp.dot(e, t, preferred_element_type=F32)
        out = d if out is None else out + d
    return out


def _tile(n, want):
    if n <= want:
        return n
    t = want - want % LANES
    while n % t:
        t -= LANES
    assert t > 0, (n, want)
    return t


_DIMS = {"nn": ((1,), (0,)), "nt": ((1,), (1,)), "tn": ((0,), (0,))}


def _mm(name, a, b, mode, out_dtypes, epi=None, extras=(), tm=1024, tn=1024, tk=4096, ride=None):
    if mode == "nn":
        (M, K), (_, N) = a.shape, b.shape
    elif mode == "nt":
        (M, K), (N, _) = a.shape, b.shape
    else:
        (K, M), (_, N) = a.shape, b.shape
    tm, tn, tk = _tile(M, tm), _tile(N, tn), _tile(K, tk)
    nm, nn, nk = M // tm, N // tn, K // tk
    ne, no = len(extras), len(out_dtypes)
    dims = (_DIMS[mode], ((), ()))
    r_in, r_ispec, r_ospec, r_oshape, r_scratch, r_hook = _ride(ride, 2 + ne, no)

    def kern(*refs):
        a_ref, b_ref = refs[0], refs[1]
        e_refs = refs[2:2 + ne]
        o_refs = refs[2 + ne + len(r_in):2 + ne + len(r_in) + no]
        i, j, k = pl.program_id(0), pl.program_id(1), pl.program_id(2)
        before, after = r_hook(refs, (i == 0) & (j == 0) & (k == 0), (i == nm // 2) & (j == 0) & (k == 0),
                               (i == nm - 1) & (j == nn - 1) & (k == nk - 1))
        before()
        d = lax.dot_general(a_ref[...], b_ref[...], dims, preferred_element_type=F32)

        def finish(r):
            outs = (r,) if epi is None else epi(r, *[e[...] for e in e_refs])
            for o_ref, o in zip(o_refs, outs):
                o_ref[...] = o.astype(o_ref.dtype)

        if nk == 1:
            finish(d)
        else:
            acc = refs[-1]

            @pl.when(k == 0)
            def _():
                acc[...] = d

            @pl.when((k > 0) & (k < nk - 1))
            def _():
                acc[...] += d

            @pl.when(k == nk - 1)
            def _():
                finish(acc[...] + d)
        after()

    if mode == "tn":
        a_spec = pl.BlockSpec((tk, tm), lambda i, j, k: (k, i))
    else:
        a_spec = pl.BlockSpec((tm, tk), lambda i, j, k: (i, k))
    if mode == "nt":
        b_spec = pl.BlockSpec((tn, tk), lambda i, j, k: (j, k))
    else:
        b_spec = pl.BlockSpec((tk, tn), lambda i, j, k: (k, j))
    mn_spec = pl.BlockSpec((tm, tn), lambda i, j, k: (i, j))
    outs = _pcall(
        kern, name=name, grid=(nm, nn, nk),
        in_specs=[a_spec, b_spec] + [mn_spec] * ne + r_ispec,
        out_specs=[mn_spec] * no + r_ospec,
        out_shape=[jax.ShapeDtypeStruct((M, N), dt) for dt in out_dtypes] + r_oshape,
        scratch_shapes=r_scratch + ([pltpu.VMEM((tm, tn), F32)] if nk > 1 else []),
        compiler_params=_cparams(("arbitrary",) * 3 if ride else ("parallel", "parallel", "arbitrary")),
    )(a, b, *extras, *r_in)
    return outs[0] if len(outs) == 1 else outs


def _rowcall(name, body, S, ts, row_ins, vec_ins, row_outs, vec_outs, into=None):
    ts = min(ts, S)
    nri, nvi, nro, nvo = len(row_ins), len(vec_ins), len(row_outs), len(vec_outs)
    na = 0 if into is None else 1

    def kern(*refs):
        ins = refs[:nri + nvi]
        outs = refs[nri + nvi + na:]
        if nvo:
            @pl.when(pl.program_id(0) == 0)
            def _():
                for r in outs[nro:]:
                    r[...] = jnp.zeros(r.shape, r.dtype)
        body(*ins, *outs)

    in_specs = [pl.BlockSpec((ts, w), functools.partial(lambda i, cb: (i, cb), cb=cb))
                for (_, w, cb) in row_ins]
    in_specs += [pl.BlockSpec(v.shape, lambda i: (0, 0)) for v in vec_ins]
    out_specs = [pl.BlockSpec((ts, w), lambda i: (i, 0)) for (w, _) in row_outs]
    out_specs += [pl.BlockSpec((r, w), lambda i: (0, 0)) for (r, w) in vec_outs]
    out_shape = [jax.ShapeDtypeStruct((S, w), dt) for (w, dt) in row_outs]
    out_shape += [jax.ShapeDtypeStruct((r, w), F32) for (r, w) in vec_outs]
    extra, aliases = [], {}
    if into is not None:
        buf, cb = into
        assert buf.dtype == row_outs[0][1] and buf.shape[0] == S
        in_specs.append(ANY)
        out_specs[0] = pl.BlockSpec((ts, row_outs[0][0]), lambda i: (i, cb))
        out_shape[0] = jax.ShapeDtypeStruct(buf.shape, buf.dtype)
        extra, aliases = [buf], {nri + nvi: 0}
    return _pcall(
        kern, name=name, grid=(S // ts,), in_specs=in_specs, out_specs=out_specs,
        out_shape=out_shape, input_output_aliases=aliases,
        compiler_params=_cparams(("arbitrary",) if nvo else ("parallel",)),
    )(*[a for (a, _, _) in row_ins], *vec_ins, *extra)


def _csum(x):
    return jnp.sum(x, axis=0, keepdims=True)


def _norm_mod(name, x, g, sc, sh, S, D):
    def body(x_ref, g_ref, sc_ref, sh_ref, h_ref):
        xv = x_ref[...]
        r = lax.rsqrt(jnp.mean(xv * xv, axis=-1, keepdims=True) + NORM_EPS)
        h_ref[...] = ((xv * r * g_ref[...]) * (1.0 + sc_ref[...]) + sh_ref[...]).astype(BF16)
    return _rowcall(name, body, S, 512, [(x, D, 0)], [g, sc, sh], [(D, BF16)], [])[0]


def _head_rstd(v, grp, grp_t, hd):
    ss = _dot_rs(v * v, grp, 2) * (1.0 / hd)
    r = lax.rsqrt(ss + NORM_EPS)
    return _dot_rs(r, grp_t, 2)


def _qk_prep(proj, vcol, gq, gk, grp, grp_t, S, D, hd):
    scale = hd ** -0.5

    def body(q_ref, k_ref, v_ref, gq_ref, gk_ref, g_ref, gt_ref, qs_ref, kn_ref, vb_ref):
        q = q_ref[...]
        k = k_ref[...]
        rq = _head_rstd(q, g_ref[...], gt_ref[...], hd)
        rk = _head_rstd(k, g_ref[...], gt_ref[...], hd)
        qs_ref[...] = ((q * rq * gq_ref[...]).astype(BF16).astype(F32) * scale).astype(BF16)
        kn_ref[...] = (k * rk * gk_ref[...]).astype(BF16)
        vb_ref[...] = v_ref[...].astype(BF16)

    return _rowcall("qk_prep", body, S, 256, [(proj, D, 0), (proj, D, 1), (proj, D, vcol)],
                    [gq, gk, grp, grp_t], [(D, BF16)] * 3, [])


def _fgate_fwd(proj, fcol, bf_pad, tri, S):
    ch = tri.shape[0]

    def body(f_ref, b_ref, tri_ref, out_ref):
        carry = jnp.zeros((1, LANES), F32)
        for c in range(S // ch):
            z = f_ref[c * ch:(c + 1) * ch, :] + b_ref[...]
            lf = jnp.minimum(z, 0.0) - jnp.log(1.0 + jnp.exp(-jnp.abs(z)))
            out_ref[c * ch:(c + 1) * ch, :] = _dot_ls(tri_ref[...], lf) + carry
            carry = carry + _csum(lf)

    return _rowcall("fgate_fwd", body, S, S, [(proj, LANES, fcol)], [bf_pad, tri],
                    [(LANES, F32)], [])[0]


def _fgate_bwd(dfk, dfq, proj, fcol, bf_pad, tri_u, nh, S, fw, into):
    ch = tri_u.shape[0]

    def body(d_ref, dq_ref, f_ref, b_ref, tri_ref, df_ref, db_ref):
        if fw > LANES:
            df_ref[:, LANES:fw] = jnp.zeros((S, fw - LANES), BF16)
        lane = lax.broadcasted_iota(jnp.int32, (ch, LANES), 1)
        carry = jnp.zeros((1, LANES), F32)
        tot = jnp.zeros((1, LANES), F32)
        for c in reversed(range(S // ch)):
            d = d_ref[c * ch:(c + 1) * ch, :] + dq_ref[c * ch:(c + 1) * ch, :]
            rc = _dot_ls(tri_ref[...], d) + carry
            carry = carry + _csum(d)
            z = f_ref[c * ch:(c + 1) * ch, :] + b_ref[...]
            df = jnp.where(lane < nh, rc * _sig(-z), 0.0)
            df_ref[c * ch:(c + 1) * ch, 0:LANES] = df.astype(BF16)
            tot = tot + _csum(df)
        db_ref[...] += tot

    return _rowcall("fgate_bwd", body, S, S, [(dfk, LANES, 0), (dfq, LANES, 0), (proj, LANES, fcol)],
                    [bf_pad, tri_u], [(fw, BF16)], [(1, LANES)], into=into)


def _keep(v, mask):
    return jnp.where(mask, v.astype(F32), 0.0).astype(BF16)


def _lane_col(blk, lane, at):
    return jnp.sum(jnp.where(lane == at, blk, 0.0), axis=-1, keepdims=True)


def _flash_fwd(qs, kn, vb, fk_r, S, D, hd, tq, ride=None):
    hp, nq = D // LANES, S // tq
    r_in, r_ispec, r_ospec, r_oshape, r_scratch, r_hook = _ride(ride, 4, 3)

    def kern(*refs):
        q_ref, k_ref, v_ref, fk_ref = refs[:4]
        o_ref, o32_ref, lse_ref = refs[4 + len(r_in):7 + len(r_in)]
        hi, qi = pl.program_id(0), pl.program_id(1)
        before, after = r_hook(refs, (hi == 0) & (qi == 0), (hi == hp // 2) & (qi == 0),
                               (hi == hp - 1) & (qi == nq - 1))
        before()
        lane = lax.broadcasted_iota(jnp.int32, (tq, LANES), 1)
        row = lax.broadcasted_iota(jnp.int32, (tq, tq), 0)
        col = lax.broadcasted_iota(jnp.int32, (tq, tq), 1)
        hms = [(lane >= j * hd) & (lane < (j + 1) * hd) for j in range(2)]
        q = q_ref[...]
        qms = [_keep(q, hm) for hm in hms]

        def step(ki, state, masked):
            off = pl.multiple_of(ki * tq, tq)
            k = k_ref[pl.ds(off, tq), :]
            v = v_ref[pl.ds(off, tq), :].astype(F32)
            new = []
            for j in range(2):
                m_old, acc = state[j]
                s = lax.dot_general(qms[j], k, (((1,), (1,)), ((), ())), preferred_element_type=F32)
                s = s - fk_ref[j, ki]
                if masked:
                    s = jnp.where(col <= row, s, NEG)
                m_new = jnp.maximum(m_old, jnp.max(s, axis=-1, keepdims=True))
                alpha = jnp.exp(m_old - m_new)
                p = jnp.exp(s - m_new)
                v1 = jnp.where(hms[j], v, 1.0).astype(BF16)
                acc = alpha * acc + jnp.dot(p.astype(BF16), v1, preferred_element_type=F32)
                new.append((m_new, acc))
            return tuple(new)

        init = tuple((jnp.full((tq, 1), NEG, F32), jnp.zeros((tq, LANES), F32)) for _ in range(2))
        state = lax.fori_loop(0, qi, lambda ki, st: step(ki, st, False), init)
        (m0, a0), (m1, a1) = step(qi, state, True)
        l0, l1 = pltpu.roll(a0, hd, 1), pltpu.roll(a1, hd, 1)
        first = lane < hd
        ov = jnp.where(first, a0 / l0, a1 / l1)
        o_ref[...] = ov.astype(BF16)
        o32_ref[...] = ov
        lse_ref[...] = jnp.where(first, m0 + jnp.log(l0), m1 + jnp.log(l1))
        after()

    qspec = pl.BlockSpec((tq, LANES), lambda h, i: (i, h))
    fullspec = pl.BlockSpec((S, LANES), lambda h, i: (0, h))
    return _pcall(
        kern, name="flash_fwd", grid=(hp, nq),
        in_specs=[qspec, fullspec, fullspec,
                  pl.BlockSpec((2, nq, 1, tq), lambda h, i: (h, 0, 0, 0))] + r_ispec,
        out_specs=[qspec, qspec, qspec] + r_ospec,
        out_shape=[jax.ShapeDtypeStruct((S, D), BF16), jax.ShapeDtypeStruct((S, D), F32),
                   jax.ShapeDtypeStruct((S, D), F32)] + r_oshape,
        scratch_shapes=r_scratch,
        compiler_params=_cparams(("arbitrary", "arbitrary")),
    )(qs, kn, vb, fk_r, *r_in)


def _flash_bwd(qs, kn, vb, do, fk_b, lse_r, delta_r, S, D, hd, tq, dv_into, ride=None):
    hp, nq = D // LANES, S // tq
    dbuf_hbm, dv_col = dv_into
    r_in, r_ispec, r_ospec, r_oshape, r_scratch, r_hook = _ride(ride, 8, 5)

    def kern(*refs):
        q_ref, do_ref, k_ref, v_ref, fk_ref, lse_ref, dl_ref = refs[:7]
        dk_ref, dv_ref, dq_ref, dfq_ref, dfk_ref = refs[8 + len(r_in):13 + len(r_in)]
        hi, ki = pl.program_id(0), pl.program_id(1)
        before, after = r_hook(refs, (hi == 0) & (ki == 0), (hi == hp // 2) & (ki == 0),
                               (hi == hp - 1) & (ki == nq - 1))
        before()
        lane = lax.broadcasted_iota(jnp.int32, (tq, LANES), 1)
        row = lax.broadcasted_iota(jnp.int32, (tq, tq), 0)
        col = lax.broadcasted_iota(jnp.int32, (tq, tq), 1)
        hms = [(lane >= j * hd) & (lane < (j + 1) * hd) for j in range(2)]
        k = k_ref[...]
        v = v_ref[...]
        fkb = fk_ref[...]
        kms = [_keep(k, hm) for hm in hms]
        vms = [_keep(v, hm) for hm in hms]
        fks = [_lane_col(fkb, lane, 2 * hi + j) for j in range(2)]

        @pl.when(ki == 0)
        def _():
            dfq_ref[...] = jnp.zeros(dfq_ref.shape, F32)
            dq_ref[...] = jnp.zeros(dq_ref.shape, F32)

        def step(qi, acc, masked):
            dk, dv, dfs = acc
            off = pl.multiple_of(qi * tq, tq)
            q = q_ref[pl.ds(off, tq), :]
            g = do_ref[pl.ds(off, tq), :]
            dq = None
            new_dfs = []
            for j in range(2):
                qm = _keep(q, hms[j])
                gm = _keep(g, hms[j])
                st = lax.dot_general(kms[j], q, (((1,), (1,)), ((), ())), preferred_element_type=F32)
                st = st - fks[j]
                if masked:
                    st = jnp.where(row <= col, st, NEG)
                pt = jnp.exp(st - lse_ref[j, qi])
                dv = dv + jnp.dot(pt.astype(BF16), gm, preferred_element_type=F32)
                dpt = lax.dot_general(vms[j], g, (((1,), (1,)), ((), ())), preferred_element_type=F32)
                dst = pt * (dpt - dl_ref[j, qi])
                dsb = dst.astype(BF16)
                dk = dk + jnp.dot(dsb, qm, preferred_element_type=F32)
                t = lax.dot_general(dsb, kms[j], (((0,), (0,)), ((), ())), preferred_element_type=F32)
                dq = t if dq is None else dq + t
                dfq_ref[j, qi] += jnp.sum(dst, axis=0, keepdims=True)
                new_dfs.append(dfs[j] - jnp.sum(dst, axis=1, keepdims=True))
            dq_ref[pl.ds(off, tq), :] += dq
            return dk, dv, tuple(new_dfs)

        zero = jnp.zeros((tq, LANES), F32)
        zcol = jnp.zeros((tq, 1), F32)
        acc = step(ki, (zero, zero, (zcol, zcol)), True)
        dk, dv, dfs = lax.fori_loop(ki + 1, nq, lambda qi, a: step(qi, a, False), acc)
        dk_ref[...] = dk.astype(BF16)
        dv_ref[...] = dv.astype(BF16)
        dfk_ref[...] = jnp.where(lane < hd, dfs[0], dfs[1])
        after()

    kspec = pl.BlockSpec((tq, LANES), lambda h, i: (i, h))
    fullspec = pl.BlockSpec((S, LANES), lambda h, i: (0, h))
    rowspec = pl.BlockSpec((2, nq, 1, tq), lambda h, i: (h, 0, 0, 0))
    return _pcall(
        kern, name="flash_bwd", grid=(hp, nq),
        in_specs=[fullspec, fullspec, kspec, kspec, pl.BlockSpec((tq, LANES), lambda h, i: (i, 0)),
                  rowspec, rowspec, ANY] + r_ispec,
        out_specs=[kspec, pl.BlockSpec((tq, LANES), lambda h, i: (i, h + dv_col)), fullspec, rowspec, kspec]
        + r_ospec,
        out_shape=[jax.ShapeDtypeStruct((S, D), BF16), jax.ShapeDtypeStruct(dbuf_hbm.shape, BF16),
                   jax.ShapeDtypeStruct((S, D), F32), jax.ShapeDtypeStruct((2 * hp, nq, 1, tq), F32),
                   jax.ShapeDtypeStruct((S, D), F32)] + r_oshape,
        scratch_shapes=r_scratch, input_output_aliases={7: 1},
        compiler_params=_cparams(("arbitrary", "arbitrary")),
    )(qs, do, kn, vb, fk_b, lse_r, delta_r, dbuf_hbm, *r_in)


def _delta_prep(do, o, lse_b, grp, sel, S, D):
    def body(g_ref, o_ref, l_ref, e_ref, s_ref, dl_ref, lse_ref):
        prod = g_ref[...].astype(F32) * o_ref[...]
        dl_ref[...] = _dot_rs(prod, e_ref[...], 2)
        lse_ref[...] = _dot_rs(l_ref[...], s_ref[...])
    return _rowcall("delta_prep", body, S, 256, [(do, D, 0), (o, D, 0), (lse_b, D, 0)], [grp, sel],
                    [(LANES, F32), (LANES, F32)], [])


def _qk_bwd(proj, dqs, dkn, gq, gk, grp, grp_t, S, D, hd, into):
    scale = hd ** -0.5

    def one(x, dn, gain, e, et):
        r = _head_rstd(x, e, et, hd)
        xh = x * r
        t = dn * gain
        mean = _dot_rs(_dot_rs(t * xh, e, 2), et, 2) * (1.0 / hd)
        return r * (t - xh * mean), _csum(dn * xh)

    def body(q_ref, k_ref, dq_ref, dk_ref, gq_ref, gk_ref, e_ref, et_ref, o_ref, sq_ref, sk_ref):
        e, et = e_ref[...], et_ref[...]
        dq, sq = one(q_ref[...], dq_ref[...].astype(F32) * scale, gq_ref[...], e, et)
        dk, sk = one(k_ref[...], dk_ref[...].astype(F32), gk_ref[...], e, et)
        o_ref[:, 0:D] = dq.astype(BF16)
        o_ref[:, D:2 * D] = dk.astype(BF16)
        sq_ref[...] += sq
        sk_ref[...] += sk

    return _rowcall("qk_bwd", body, S, 256,
                    [(proj, D, 0), (proj, D, 1), (dqs, D, 0), (dkn, D, 0)],
                    [gq, gk, grp, grp_t], [(2 * D, BF16)], [(1, D)] * 2, into=into)


def _shift_copies(buf, sh, ts):
    for b in range(1, SUBLANES):
        sh[b - 1] = buf[b:b + ts + HALO - SUBLANES, :]


def _rows_from(buf, sh, o, ts):
    a, b = divmod(o, SUBLANES)
    if b == 0:
        return buf[o:o + ts, :]
    return sh[b - 1, SUBLANES * a:SUBLANES * a + ts, :]


def _conv_fwd(proj, acol, bcol, w_pad, cb, lg, lb, S, C, taps, ts):
    ts = min(ts, S)

    def kern(a_ref, b_ref, w_ref, cb_ref, lg_ref, lb_ref, u1_ref, u3_ref, ubuf, ush):
        @pl.when(pl.program_id(0) == 0)
        def _():
            ubuf[0:HALO, :] = jnp.zeros((HALO, C), F32)

        ubuf[HALO:HALO + ts, :] = a_ref[...] * _sig(b_ref[...])
        _shift_copies(ubuf, ush, ts)
        acc = jnp.zeros((ts, C), F32) + cb_ref[...]
        for k in range(taps):
            acc = acc + w_ref[k:k + 1, :] * _rows_from(ubuf, ush, HALO - (taps - 1) + k, ts)
        u1_ref[...] = acc
        mu = jnp.mean(acc, axis=-1, keepdims=True)
        xc = acc - mu
        rstd = lax.rsqrt(jnp.mean(xc * xc, axis=-1, keepdims=True) + NORM_EPS)
        u2 = xc * rstd * lg_ref[...] + lb_ref[...]
        u3_ref[...] = (u2 * _sig(u2)).astype(BF16)
        ubuf[0:HALO, :] = ubuf[ts:ts + HALO, :]

    vec = lambda a: pl.BlockSpec(a.shape, lambda i: (0, 0))
    return _pcall(
        kern, name="conv_fwd", grid=(S // ts,),
        in_specs=[pl.BlockSpec((ts, C), lambda i: (i, acol)), pl.BlockSpec((ts, C), lambda i: (i, bcol)),
                  vec(w_pad), vec(cb), vec(lg), vec(lb)],
        out_specs=[pl.BlockSpec((ts, C), lambda i: (i, 0))] * 2,
        out_shape=[jax.ShapeDtypeStruct((S, C), F32), jax.ShapeDtypeStruct((S, C), BF16)],
        scratch_shapes=[pltpu.VMEM((HALO + ts, C), F32),
                        pltpu.VMEM((SUBLANES - 1, HALO + ts - SUBLANES, C), F32)],
        compiler_params=_cparams(("arbitrary",)),
    )(proj, proj, w_pad, cb, lg, lb)


def _conv_bwd(du3, u1, proj, acol, bcol, w_pad, lg, lb, S, C, taps, ts, into, ride=None):
    ts = min(ts, S)
    dbuf_hbm, dcol = into
    r_in, r_ispec, r_ospec, r_oshape, r_scratch, r_hook = _ride(ride, 12, 5)
    nt = S // ts
    hb = ts // HALO

    def ln_bwd(g, u, lgv, lbv):
        mu = jnp.mean(u, axis=-1, keepdims=True)
        xc = u - mu
        rstd = lax.rsqrt(jnp.mean(xc * xc, axis=-1, keepdims=True) + NORM_EPS)
        xh = xc * rstd
        u2 = xh * lgv + lbv
        s = _sig(u2)
        du2 = g * (s + u2 * s * (1.0 - s))
        dxh = du2 * lgv
        du1 = rstd * (dxh - jnp.mean(dxh, axis=-1, keepdims=True)
                      - xh * jnp.mean(dxh * xh, axis=-1, keepdims=True))
        return du1, du2, xh

    def kern(*refs):
        g_ref, u_ref, a_ref, b_ref, gn_ref, un_ref, ap_ref, bp_ref, w_ref, lg_ref, lb_ref = refs[:11]
        dg_ref, dw_ref, dcb_ref, dlg_ref, dlb_ref = refs[12 + len(r_in):17 + len(r_in)]
        dbuf, ubuf, dsh, ush = refs[-4:]
        i = pl.program_id(0)
        before, after = r_hook(refs, i == 0, i == nt // 2, i == nt - 1)
        before()

        @pl.when(i == 0)
        def _():
            dw_ref[...] = jnp.zeros(dw_ref.shape, F32)
            dcb_ref[...] = jnp.zeros(dcb_ref.shape, F32)
            dlg_ref[...] = jnp.zeros(dlg_ref.shape, F32)
            dlb_ref[...] = jnp.zeros(dlb_ref.shape, F32)

        lgv, lbv = lg_ref[...], lb_ref[...]
        du1, du2, xh = ln_bwd(g_ref[...], u_ref[...], lgv, lbv)
        dbuf[0:ts, :] = du1
        du1n, _, _ = ln_bwd(gn_ref[...], un_ref[...], lgv, lbv)
        dbuf[ts:ts + HALO, :] = jnp.where(i < nt - 1, du1n, 0.0)
        a = a_ref[...]
        sb = _sig(b_ref[...])
        ubuf[HALO:HALO + ts, :] = a * sb
        ubuf[0:HALO, :] = jnp.where(i > 0, ap_ref[...] * _sig(bp_ref[...]), 0.0)
        dcb_ref[...] += _csum(du1)
        dlg_ref[...] += _csum(du2 * xh)
        dlb_ref[...] += _csum(du2)
        _shift_copies(dbuf, dsh, ts)
        _shift_copies(ubuf, ush, ts)
        for r0 in range(0, ts, CONV_ROWS):
            du0 = jnp.zeros((CONV_ROWS, C), F32)
            for k in range(taps):
                du0 = du0 + w_ref[k:k + 1, :] * _rows_from(dbuf, dsh, r0 + taps - 1 - k, CONV_ROWS)
            ac = a_ref[r0:r0 + CONV_ROWS, :]
            sc = _sig(b_ref[r0:r0 + CONV_ROWS, :])
            dg_ref[r0:r0 + CONV_ROWS, 0:C] = (du0 * sc).astype(BF16)
            dg_ref[r0:r0 + CONV_ROWS, C:2 * C] = (du0 * ac * sc * (1.0 - sc)).astype(BF16)
        for k0 in range(0, taps, CONV_TAPS):
            ks = range(k0, min(k0 + CONV_TAPS, taps))
            accs = [jnp.zeros((SUBLANES, C), F32) for _ in ks]
            for r0 in range(0, ts, CONV_ROWS):
                d = dbuf[r0:r0 + CONV_ROWS, :]
                for t, k in enumerate(ks):
                    prod = d * _rows_from(ubuf, ush, r0 + HALO - (taps - 1) + k, CONV_ROWS)
                    accs[t] = accs[t] + jnp.sum(prod.reshape(CONV_ROWS // SUBLANES, SUBLANES, C), axis=0)
            for t, k in enumerate(ks):
                dw_ref[k:k + 1, :] += _csum(accs[t])
        after()

    vec = lambda a: pl.BlockSpec(a.shape, lambda i: (0, 0))
    tile = lambda cb: pl.BlockSpec((ts, C), functools.partial(lambda i, cb: (i, cb), cb=cb))
    nxt = lambda cb: pl.BlockSpec(
        (HALO, C), functools.partial(lambda i, cb: (jnp.minimum((i + 1) * hb, nt * hb - 1), cb), cb=cb))
    prv = lambda cb: pl.BlockSpec(
        (HALO, C), functools.partial(lambda i, cb: (jnp.maximum(i * hb - 1, 0), cb), cb=cb))
    return _pcall(
        kern, name="conv_bwd", grid=(nt,),
        in_specs=[tile(0), tile(0), tile(acol), tile(bcol), nxt(0), nxt(0), prv(acol), prv(bcol),
                  vec(w_pad), vec(lg), vec(lb), ANY] + r_ispec,
        out_specs=[pl.BlockSpec((ts, 2 * C), lambda i: (i, dcol))]
        + [pl.BlockSpec(w_pad.shape, lambda i: (0, 0))] + [pl.BlockSpec((1, C), lambda i: (0, 0))] * 3 + r_ospec,
        out_shape=[jax.ShapeDtypeStruct(dbuf_hbm.shape, BF16)]
        + [jax.ShapeDtypeStruct(w_pad.shape, F32)] + [jax.ShapeDtypeStruct((1, C), F32)] * 3 + r_oshape,
        scratch_shapes=r_scratch + [pltpu.VMEM((ts + HALO, C), F32), pltpu.VMEM((HALO + ts, C), F32)]
        + [pltpu.VMEM((SUBLANES - 1, HALO + ts - SUBLANES, C), F32)] * 2,
        input_output_aliases={11: 0},
        compiler_params=_cparams(("arbitrary",)),
    )(du3, u1, proj, proj, du3, u1, proj, proj, w_pad, lg, lb, dbuf_hbm, *r_in)


def _gate_merge(proj, gacol, gbcol, ba, bb, S, D):
    def body(ga_ref, gb_ref, a_ref, b_ref, out_ref):
        out_ref[...] = (_sig(ga_ref[...]) * a_ref[...] + _sig(gb_ref[...]) * b_ref[...]).astype(BF16)
    return _rowcall("gate_merge", body, S, 512,
                    [(proj, D, gacol), (proj, D, gbcol), (ba, D, 0), (bb, D, 0)], [], [(D, BF16)], [])[0]


def _gate_bwd(dm, proj, gacol, gbcol, ba, bb, S, D, into):
    def body(dm_ref, ga_ref, gb_ref, a_ref, b_ref, dg_ref, da_ref, db_ref):
        dmv = dm_ref[...]
        sa, sb = _sig(ga_ref[...]), _sig(gb_ref[...])
        da_ref[...] = (dmv * sa).astype(BF16)
        db_ref[...] = (dmv * sb).astype(BF16)
        dg_ref[:, 0:D] = (dmv * a_ref[...] * sa * (1.0 - sa)).astype(BF16)
        dg_ref[:, D:2 * D] = (dmv * b_ref[...] * sb * (1.0 - sb)).astype(BF16)
    return _rowcall("gate_bwd", body, S, 512,
                    [(dm, D, 0), (proj, D, gacol), (proj, D, gbcol), (ba, D, 0), (bb, D, 0)], [],
                    [(2 * D, BF16), (D, BF16), (D, BF16)], [], into=into)


def _resid_norm2(x, mo, g1, g, sc, sh, S, D):
    def body(x_ref, mo_ref, g1_ref, g_ref, sc_ref, sh_ref, x1_ref, h_ref):
        x1 = x_ref[...] + g1_ref[...] * mo_ref[...]
        x1_ref[...] = x1
        r = lax.rsqrt(jnp.mean(x1 * x1, axis=-1, keepdims=True) + NORM_EPS)
        h_ref[...] = ((x1 * r * g_ref[...]) * (1.0 + sc_ref[...]) + sh_ref[...]).astype(BF16)
    return _rowcall("resid_norm2", body, S, 512, [(x, D, 0), (mo, D, 0)], [g1, g, sc, sh],
                    [(D, F32), (D, BF16)], [])


def _loss_dy(x1, ml, tgt, g2, S, D):
    def body(x1_ref, ml_ref, t_ref, g2_ref, dy_ref, dml_ref, sq_ref, dg2_ref):
        mlv = ml_ref[...]
        diff = x1_ref[...] + g2_ref[...] * mlv - t_ref[...]
        dy = diff * (1.0 / D)
        dy_ref[...] = dy
        dml_ref[...] = (dy * g2_ref[...]).astype(BF16)
        sq_ref[...] += _csum(diff * diff)
        dg2_ref[...] += _csum(dy * mlv)
    return _rowcall("loss_dy", body, S, 512, [(x1, D, 0), (ml, D, 0), (tgt, D, 0)], [g2],
                    [(D, F32), (D, BF16)], [(1, D), (1, D)])


def _norm_bwd(name, xin, dh, dres, g, sc, S, D, extra=None):
    def body(*refs):
        if extra is None:
            x_ref, dh_ref, dr_ref, g_ref, sc_ref, dx_ref, dsh_ref, dsc_ref, dg_ref = refs
        else:
            (x_ref, dh_ref, dr_ref, mo_ref, g_ref, sc_ref, g1_ref,
             dx_ref, dmo_ref, dsh_ref, dsc_ref, dg_ref, dg1_ref) = refs
        xv, dhv, gv = x_ref[...], dh_ref[...], g_ref[...]
        r = lax.rsqrt(jnp.mean(xv * xv, axis=-1, keepdims=True) + NORM_EPS)
        xh = xv * r
        dsh_ref[...] += _csum(dhv)
        dsc_ref[...] += _csum(dhv * xh * gv)
        dxg = dhv * (1.0 + sc_ref[...])
        dg_ref[...] += _csum(dxg * xh)
        dxh = dxg * gv
        dx = dr_ref[...] + r * (dxh - xh * jnp.mean(dxh * xh, axis=-1, keepdims=True))
        dx_ref[...] = dx
        if extra is not None:
            dmo_ref[...] = (dx * g1_ref[...]).astype(BF16)
            dg1_ref[...] += _csum(dx * mo_ref[...])

    rows = [(xin, D, 0), (dh, D, 0), (dres, D, 0)]
    vecs = [g, sc]
    if extra is None:
        return _rowcall(name, body, S, 512, rows, vecs, [(D, F32)], [(1, D)] * 3)
    return _rowcall(name, body, S, 512, rows + [(extra[0], D, 0)], vecs + [extra[1]],
                    [(D, F32), (D, BF16)], [(1, D)] * 4)


def _ada_fwd(c_all, w, b_part):
    B, D = c_all.shape
    N = w.shape[1]
    tn = min(512, N)

    def kern(c_ref, w_ref, b_ref, o_ref):
        cv = c_ref[...]
        ca = cv * _sig(cv)
        o_ref[...] = jnp.dot(ca, w_ref[...], precision=lax.Precision.HIGHEST,
                             preferred_element_type=F32) + b_ref[...]

    return _pcall(
        kern, name="ada_fwd", grid=(N // tn,),
        in_specs=[pl.BlockSpec((B, D), lambda j: (0, 0)), pl.BlockSpec((D, tn), lambda j: (0, j)),
                  pl.BlockSpec((1, tn), lambda j: (0, j))],
        out_specs=pl.BlockSpec((B, tn), lambda j: (0, j)),
        out_shape=jax.ShapeDtypeStruct((B, N), F32),
        compiler_params=_cparams(("parallel",)),
    )(c_all, w, b_part)


def _ada_wgrad(c_t_pad, dmod_pad):
    D = c_t_pad.shape[0]
    N = dmod_pad.shape[1]
    tn = min(512, N)

    def kern(c_ref, d_ref, o_ref):
        cv = c_ref[...]
        ca = cv * _sig(cv)
        o_ref[...] = jnp.dot(ca, d_ref[...], precision=lax.Precision.HIGHEST,
                             preferred_element_type=F32)

    return _pcall(
        kern, name="ada_wgrad", grid=(N // tn,),
        in_specs=[pl.BlockSpec((D, LANES), lambda j: (0, 0)), pl.BlockSpec((LANES, tn), lambda j: (0, j))],
        out_specs=pl.BlockSpec((D, tn), lambda j: (0, j)),
        out_shape=jax.ShapeDtypeStruct((D, N), F32),
        compiler_params=_cparams(("parallel",)),
    )(c_t_pad, dmod_pad)


def _ag_small(name, arrs):
    n = len(arrs)

    def kern(*refs):
        ins, outs = refs[:n], refs[n:2 * n]
        send, recv = refs[2 * n], refs[2 * n + 1]
        x, y, c = lax.axis_index("x"), lax.axis_index("y"), lax.axis_index("c")
        me = 4 * x + 2 * y + c

        def copy(i, m, slot):
            peer = (x ^ ((m >> 2) & 1), y ^ ((m >> 1) & 1), c ^ (m & 1))
            return pltpu.make_async_remote_copy(
                src_ref=ins[i], dst_ref=outs[i].at[slot],
                send_sem=send.at[i * 7 + m - 1], recv_sem=recv.at[i * 7 + m - 1],
                device_id=peer, device_id_type=MESH)

        for i in range(n):
            outs[i][me] = ins[i][...]
            for m in range(1, 8):
                copy(i, m, me).start()
        for i in range(n):
            for m in range(1, 8):
                copy(i, m, me).wait_send()
                copy(i, m, me ^ m).wait_recv()

    vm = pl.BlockSpec(memory_space=pltpu.VMEM)
    return _pcall(
        kern, name=name, in_specs=[vm] * n, out_specs=[vm] * n,
        out_shape=[jax.ShapeDtypeStruct((8,) + a.shape, a.dtype) for a in arrs],
        scratch_shapes=[pltpu.SemaphoreType.DMA((7 * n,)), pltpu.SemaphoreType.DMA((7 * n,))],
        compiler_params=pltpu.CompilerParams(has_side_effects=True),
    )(*arrs)


def _exchange(name, arrs, plan):
    out_shape, scratch, phases = plan(arrs)

    def kern(*refs):
        for phase in phases(refs):
            phase()

    return _pcall(
        kern, name=name, in_specs=[ANY] * len(arrs), out_specs=[ANY] * len(out_shape),
        out_shape=out_shape, scratch_shapes=scratch,
        compiler_params=pltpu.CompilerParams(has_side_effects=True),
    )(*arrs)


def _ride(plan_and_arrs, n_in, n_out):
    if plan_and_arrs is None:
        return [], [], [], [], [], lambda refs, first, middle, last: ((lambda: None), (lambda: None))
    plan, arrs = plan_and_arrs
    out_shape, scratch, phases = plan(arrs)
    na, no = len(arrs), len(out_shape)

    def hook(refs, first, middle, last):
        mine = refs[n_in:n_in + na] + refs[n_in + na + n_out:]
        start, mid, finish = phases(mine)

        def before():
            pl.when(first)(start)
            pl.when(middle)(mid)

        def after():
            pl.when(last)(finish)

        return before, after

    return list(arrs), [ANY] * na, [ANY] * no, out_shape, scratch, hook


def _gather_plan(arrs):
    n = len(arrs)

    def phases(refs):
        ins, outs = refs[:n], refs[n:2 * n]
        s1, r1, s2, r2, loc = refs[2 * n:2 * n + 5]
        x, y, c = lax.axis_index("x"), lax.axis_index("y"), lax.axis_index("c")
        me = 2 * x + y

        def half(i, hc):
            hr = ins[i].shape[0] // 2
            return pl.ds(hc * hr, hr)

        def own(i):
            return pltpu.make_async_remote_copy(
                src_ref=ins[i], dst_ref=outs[i].at[me], send_sem=loc.at[i], recv_sem=loc.at[n + i],
                device_id=(x, y, 1 - c), device_id_type=MESH)

        def fetch(i, m, slot):
            px, py = x ^ ((m >> 1) & 1), y ^ (m & 1)
            return pltpu.make_async_remote_copy(
                src_ref=ins[i].at[half(i, c)], dst_ref=outs[i].at[slot, half(i, c)],
                send_sem=s1.at[i * 3 + m - 1], recv_sem=r1.at[i * 3 + m - 1],
                device_id=(px, py, c), device_id_type=MESH)

        def passed(i, m, hc):
            return pltpu.make_async_remote_copy(
                src_ref=outs[i].at[me ^ m, half(i, hc)], dst_ref=outs[i].at[me ^ m, half(i, hc)],
                send_sem=s2.at[i * 3 + m - 1], recv_sem=r2.at[i * 3 + m - 1],
                device_id=(x, y, 1 - c), device_id_type=MESH)

        def start():
            for i in range(n):
                for m in range(1, 4):
                    fetch(i, m, me).start()
            for i in range(n):
                own(i).start()

        def mid():
            for i in range(n):
                for m in range(1, 4):
                    fetch(i, m, me ^ m).wait_recv()
                    passed(i, m, c).start()

        def finish():
            for i in range(n):
                own(i).wait()
                for m in range(1, 4):
                    fetch(i, m, me).wait_send()
                    passed(i, m, c).wait_send()
                    passed(i, m, 1 - c).wait_recv()

        return start, mid, finish

    out_shape = [jax.ShapeDtypeStruct((4,) + a.shape, a.dtype) for a in arrs]
    scratch = [pltpu.SemaphoreType.DMA((3 * n,))] * 4 + [pltpu.SemaphoreType.DMA((2 * n,))]
    return out_shape, scratch, phases


def _pair_halves_plan(arrs):
    n = len(arrs)

    def phases(refs):
        ins, outs = refs[:n], refs[n:2 * n]
        send, recv = refs[2 * n], refs[2 * n + 1]
        x, y, c = lax.axis_index("x"), lax.axis_index("y"), lax.axis_index("c")

        def copy(i, k, hc):
            return pltpu.make_async_remote_copy(
                src_ref=ins[i].at[k, hc], dst_ref=outs[i].at[k],
                send_sem=send.at[i * 4 + k], recv_sem=recv.at[i * 4 + k],
                device_id=(x, y, 1 - c), device_id_type=MESH)

        def start():
            for i in range(n):
                for k in range(4):
                    copy(i, k, 1 - c).start()

        def finish():
            for i in range(n):
                for k in range(4):
                    copy(i, k, 1 - c).wait()

        return start, (lambda: None), finish

    out_shape = [jax.ShapeDtypeStruct((4,) + a.shape[2:], a.dtype) for a in arrs]
    scratch = [pltpu.SemaphoreType.DMA((4 * n,)), pltpu.SemaphoreType.DMA((4 * n,))]
    return out_shape, scratch, phases


def _scatter_plan(arrs):
    n = len(arrs)

    def phases(refs):
        ins, outs = refs[:n], refs[n:2 * n]
        send, recv = refs[2 * n], refs[2 * n + 1]
        x, y, c = lax.axis_index("x"), lax.axis_index("y"), lax.axis_index("c")
        me = 2 * x + y

        def copy(i, m, slot):
            px, py = x ^ ((m >> 1) & 1), y ^ (m & 1)
            return pltpu.make_async_remote_copy(
                src_ref=ins[i].at[2 * px + py], dst_ref=outs[i].at[slot],
                send_sem=send.at[i * 3 + m - 1], recv_sem=recv.at[i * 3 + m - 1],
                device_id=(px, py, c), device_id_type=MESH)

        def start():
            for i in range(n):
                for m in range(1, 4):
                    copy(i, m, me).start()

        def finish():
            for i in range(n):
                for m in range(1, 4):
                    copy(i, m, me).wait_send()
                    copy(i, m, me ^ m).wait_recv()

        return start, (lambda: None), finish

    out_shape = [jax.ShapeDtypeStruct(a.shape, a.dtype) for a in arrs]
    scratch = [pltpu.SemaphoreType.DMA((3 * n,)), pltpu.SemaphoreType.DMA((3 * n,))]
    return out_shape, scratch, phases


def _pair_swap_plan(arrs):
    n = len(arrs)

    def phases(refs):
        ins, outs = refs[:n], refs[n:2 * n]
        send, recv = refs[2 * n], refs[2 * n + 1]
        x, y, c = lax.axis_index("x"), lax.axis_index("y"), lax.axis_index("c")

        def copy(i):
            return pltpu.make_async_remote_copy(
                src_ref=ins[i], dst_ref=outs[i], send_sem=send.at[i], recv_sem=recv.at[i],
                device_id=(x, y, 1 - c), device_id_type=MESH)

        def start():
            for i in range(n):
                copy(i).start()

        def finish():
            for i in range(n):
                copy(i).wait()

        return start, (lambda: None), finish

    out_shape = [jax.ShapeDtypeStruct(a.shape, a.dtype) for a in arrs]
    scratch = [pltpu.SemaphoreType.DMA((n,)), pltpu.SemaphoreType.DMA((n,))]
    return out_shape, scratch, phases


def _row_tile(R):
    for t in (256, 128, 64, 32, 16, 8):
        if R % t == 0:
            return t
    return R


def _sum_slots(name, parts):
    K, R, C = parts.shape
    tr = _row_tile(R)

    def kern(p_ref, o_ref):
        acc = p_ref[0].astype(F32)
        for k in range(1, K):
            acc = acc + p_ref[k].astype(F32)
        o_ref[...] = acc

    return _pcall(
        kern, name=name, grid=(R // tr,),
        in_specs=[pl.BlockSpec((K, tr, C), lambda i: (0, i, 0))],
        out_specs=pl.BlockSpec((tr, C), lambda i: (i, 0)),
        out_shape=jax.ShapeDtypeStruct((R, C), F32),
        compiler_params=_cparams(("parallel",)),
    )(parts)


def _sum_pair(name, core, mine, theirs):
    K, _, hr, C = mine.shape
    tr = _row_tile(hr)

    def kern(c_ref, a_ref, b_ref, o_ref):
        o_ref[0] = (a_ref[0, 0].astype(F32) + b_ref[0].astype(F32)).astype(BF16)

    return _pcall(
        kern, name=name, out_shape=jax.ShapeDtypeStruct((K, hr, C), BF16),
        grid_spec=pltpu.PrefetchScalarGridSpec(
            num_scalar_prefetch=1, grid=(K, hr // tr),
            in_specs=[pl.BlockSpec((1, 1, tr, C), lambda k, r, c_ref: (k, c_ref[0], r, 0)),
                      pl.BlockSpec((1, tr, C), lambda k, r, c_ref: (k, r, 0))],
            out_specs=pl.BlockSpec((1, tr, C), lambda k, r, c_ref: (k, r, 0))),
        compiler_params=_cparams(("parallel", "parallel")),
    )(core, mine, theirs)


def _sum_chips(name, chip, own, recv):
    K, hr, C = own.shape
    tr = _row_tile(hr)

    def kern(chip_ref, own_ref, *rest):
        r_refs, o_ref = rest[:K], rest[K]
        me = chip_ref[0]
        mine = own_ref[0].astype(F32)
        acc = None
        for k in range(K):
            t = jnp.where(me == k, mine, r_refs[k][0].astype(F32))
            acc = t if acc is None else acc + t
        o_ref[...] = acc

    def other(k):
        return pl.BlockSpec((1, tr, C), lambda r, s: (jnp.where(s[0] == k, (k + 1) % K, k), r, 0))

    return _pcall(
        kern, name=name, out_shape=jax.ShapeDtypeStruct((hr, C), F32),
        grid_spec=pltpu.PrefetchScalarGridSpec(
            num_scalar_prefetch=1, grid=(hr // tr,),
            in_specs=[pl.BlockSpec((1, tr, C), lambda r, s: (s[0], r, 0))] + [other(k) for k in range(K)],
            out_specs=pl.BlockSpec((tr, C), lambda r, s: (r, 0))),
        compiler_params=_cparams(("parallel",)),
    )(chip, own, *([recv] * K))


def _adam_update(w, m, v, g):
    c1 = 1.0 - ADAM_B1 ** ADAM_STEP
    c2 = 1.0 - ADAM_B2 ** ADAM_STEP
    mn = ADAM_B1 * m + (1.0 - ADAM_B1) * g
    vn = ADAM_B2 * v + (1.0 - ADAM_B2) * (g * g)
    return -ADAM_LR * ((mn / c1) / (jnp.sqrt(vn / c2) + ADAM_EPS) + ADAM_WD * w), mn, vn


def _adamw_halves(name, core, w, m, v, mine, theirs):
    R, C = w.shape
    hr = mine.shape[0]
    tr = _row_tile(hr)
    nbh = hr // tr

    def kern(c_ref, w_ref, m_ref, v_ref, a_ref, b_ref, go_ref, d_ref, mo_ref, vo_ref):
        g = jnp.where(pl.program_id(0) // nbh == c_ref[0], a_ref[...], b_ref[...])
        d, mn, vn = _adam_update(w_ref[...], m_ref[...], v_ref[...], g)
        go_ref[...] = g
        d_ref[...] = d
        mo_ref[...] = mn
        vo_ref[...] = vn

    spec = pl.BlockSpec((tr, C), lambda i, s: (i, 0))
    hspec = pl.BlockSpec((tr, C), lambda i, s: (i % nbh, 0))
    return _pcall(
        kern, name=name, out_shape=[jax.ShapeDtypeStruct((R, C), F32)] * 4,
        grid_spec=pltpu.PrefetchScalarGridSpec(
            num_scalar_prefetch=1, grid=(R // tr,),
            in_specs=[spec, spec, spec, hspec, hspec], out_specs=[spec] * 4),
        compiler_params=_cparams(("parallel",)),
    )(core, w, m, v, mine, theirs)


def _adamw_many(name, wmvg):
    n = len(wmvg[0])

    def kern(*refs):
        ins, outs = refs[:4 * n], refs[4 * n:]
        for j in range(n):
            g = ins[3 * n + j][...]
            d, mn, vn = _adam_update(ins[j][...], ins[n + j][...], ins[2 * n + j][...], g)
            for i, val in enumerate((g, d, mn, vn)):
                outs[i * n + j][...] = val

    vm = pl.BlockSpec(memory_space=pltpu.VMEM)
    flat = [a for group in wmvg for a in group]
    outs = _pcall(
        kern, name=name, in_specs=[vm] * (4 * n), out_specs=[vm] * (4 * n),
        out_shape=[jax.ShapeDtypeStruct(a.shape, F32) for _ in range(4) for a in wmvg[0]],
    )(*flat)
    return [outs[i * n:(i + 1) * n] for i in range(4)]


def _adamw(name, w, m, v, gparts, ride=None):
    R, C = w.shape
    K = gparts.shape[0]
    tr = _row_tile(R)
    nr = R // tr
    r_in, r_ispec, r_ospec, r_oshape, r_scratch, r_hook = _ride(ride, 4, 4)

    def kern(*refs):
        w_ref, m_ref, v_ref, g_ref = refs[:4]
        go_ref, d_ref, mo_ref, vo_ref = refs[4 + len(r_in):8 + len(r_in)]
        i = pl.program_id(0)
        before, after = r_hook(refs, i == 0, i == nr // 2, i == nr - 1)
        before()
        g = g_ref[0]
        for k in range(1, K):
            g = g + g_ref[k]
        d, mn, vn = _adam_update(w_ref[...], m_ref[...], v_ref[...], g)
        go_ref[...] = g
        d_ref[...] = d
        mo_ref[...] = mn
        vo_ref[...] = vn
        after()

    spec = pl.BlockSpec((tr, C), lambda i: (i, 0))
    return _pcall(
        kern, name=name, grid=(nr,),
        in_specs=[spec, spec, spec, pl.BlockSpec((K, tr, C), lambda i: (0, i, 0))] + r_ispec,
        out_specs=[spec] * 4 + r_ospec,
        out_shape=[jax.ShapeDtypeStruct((R, C), F32)] * 4 + r_oshape,
        scratch_shapes=r_scratch,
        compiler_params=_cparams(("arbitrary",) if ride else ("parallel",)),
    )(w, m, v, gparts, *r_in)


def _round_up(a, b):
    return (a + b - 1) // b * b


def kernel(x, c, w_ada, b_ada, norm1_g, w_in, b_forget, q_norm_g, k_norm_g, w_attn_proj, conv_w, conv_b, conv_ln_g, conv_ln_b, w_conv_proj, w_out, norm2_g, w_mlp1, w_mlp2, loss_target, m_w_ada, m_b_ada, m_norm1_g, m_w_in, m_b_forget, m_q_norm_g, m_k_norm_g, m_w_attn_proj, m_conv_w, m_conv_b, m_conv_ln_g, m_conv_ln_b, m_w_conv_proj, m_w_out, m_norm2_g, m_w_mlp1, m_w_mlp2, v_w_ada, v_b_ada, v_norm1_g, v_w_in, v_b_forget, v_q_norm_g, v_k_norm_g, v_w_attn_proj, v_conv_w, v_conv_b, v_conv_ln_g, v_conv_ln_b, v_w_conv_proj, v_w_out, v_norm2_g, v_w_mlp1, v_w_mlp2):
    S, D = x.shape[1], x.shape[2]
    NH, HD = b_forget.shape[-1], q_norm_g.shape[-1]
    TAPS = conv_w.shape[1]
    DIN_S = w_in.shape[-1]
    DIN = 4 * DIN_S
    DFF_S = w_mlp1.shape[-1]
    DFF = 4 * DFF_S
    ADA_S = w_ada.shape[-1]
    DS = w_attn_proj.shape[1]
    CS = conv_w.shape[-1]
    assert NH * HD == D and DIN == 7 * D + NH and TAPS - 1 <= HALO and D % LANES == 0 and 2 * HD == LANES
    NP = _round_up(7 * D + LANES, 512)
    FW = NP - 7 * D
    assert (7 * D) % FW == 0
    TQ = min(512, S)
    NQ = S // TQ
    FCOL = 7 * D // LANES

    xi, yi, ci = lax.axis_index("x"), lax.axis_index("y"), lax.axis_index("c")
    chip = 2 * xi + yi
    dev = 4 * xi + 2 * yi + ci

    x2 = x.reshape(S, D)
    tgt = loss_target.reshape(S, D)

    lane_head = jnp.arange(D, dtype=jnp.int32) // HD
    grp = (lane_head[:, None] == jnp.arange(LANES, dtype=jnp.int32)[None, :]).astype(BF16)
    grp_t = grp.T
    sel = ((jnp.arange(D, dtype=jnp.int32)[:, None] == HD * jnp.arange(LANES, dtype=jnp.int32)[None, :])
           .astype(BF16))
    ch = min(256, S)
    ii = jnp.arange(ch, dtype=jnp.int32)
    tri = (ii[None, :] <= ii[:, None]).astype(BF16)
    tri_u = tri.T
    gq_t = jnp.tile(q_norm_g.reshape(1, HD), (1, NH))
    gk_t = jnp.tile(k_norm_g.reshape(1, HD), (1, NH))
    bf_pad = jnp.pad(b_forget.reshape(1, NH), ((0, 0), (0, LANES - NH)))

    c_all, cw_all = _ag_small(
        "ag_c_convw", [c.reshape(1, D), jnp.pad(conv_w.reshape(TAPS, CS), ((0, HALO - TAPS), (0, 0)))])
    c_all = c_all.reshape(8, D)
    b_part = lax.dynamic_slice(b_ada.reshape(1, -1), (0, chip * ADA_S), (1, ADA_S))
    mod_part = _ada_fwd(c_all, w_ada.reshape(D, ADA_S), b_part)
    (mod_all,) = _ag_small("ag_mod", [mod_part])
    mod_full = jnp.concatenate([mod_all[0], mod_all[2], mod_all[4], mod_all[6]], axis=1)
    mod = lax.dynamic_slice(mod_full, (dev, 0), (1, 6 * D))
    sh1, sc1, g1, sh2, sc2, g2 = [mod[:, i * D:(i + 1) * D] for i in range(6)]

    shards = [w_in.reshape(D, DIN_S), w_attn_proj.reshape(DS, D), w_conv_proj.reshape(DS, D),
              w_out.reshape(DS, D), w_mlp1.reshape(D, DFF_S), w_mlp2.reshape(DFF_S, D)]
    shards = [s.astype(BF16) for s in shards]
    (gw_in,) = _exchange("ag_w_in", shards[:1], _gather_plan)
    w_conv = jnp.concatenate([cw_all[0], cw_all[2], cw_all[4], cw_all[6]], axis=1)

    SEGS = [(0, 2 * D, 0), (3 * D + NH, DIN, 2 * D), (2 * D, 3 * D + NH, 6 * D)]

    def pieces(a, b):
        out = []
        for k in range(4):
            lo, hi = max(a, k * DIN_S), min(b, (k + 1) * DIN_S)
            if lo < hi:
                out.append(gw_in[k][:, lo - k * DIN_S:hi - k * DIN_S])
        return out

    w_in_p = jnp.concatenate([p for (a, b, _) in SEGS for p in pieces(a, b)]
                             + [jnp.zeros((D, NP - 7 * D - NH), BF16)], axis=1)

    n1g = norm1_g.reshape(1, D)
    n2g = norm2_g.reshape(1, D)
    h = _norm_mod("norm_mod1", x2, n1g, sc1, sh1, S, D)
    proj = _mm("mm_in", h, w_in_p, "nn", [F32], tn=1536)
    qs, kn, vb = _qk_prep(proj, 6, gq_t, gk_t, grp, grp_t, S, D, HD)
    f_cum = _fgate_fwd(proj, FCOL, bf_pad, tri, S)
    fk_c = f_cum[:, :NH]
    fk_r = fk_c.T.reshape(NH, NQ, 1, TQ)
    o, o32, lse_b, gw_ap, gw_cp, gw_out, gw_m1, gw_m2 = _flash_fwd(
        qs, kn, vb, fk_r, S, D, HD, TQ, ride=(_gather_plan, shards[1:]))
    w_ap = gw_ap.reshape(D, D)
    w_cp = gw_cp.reshape(D, D)
    w_o = gw_out.reshape(D, D)
    w_m1 = jnp.transpose(gw_m1, (1, 0, 2)).reshape(D, DFF)
    w_m2 = gw_m2.reshape(DFF, D)
    br_a = _mm("mm_attn_proj", o, w_ap, "nn", [F32])
    cb, clg, clb = conv_b.reshape(1, D), conv_ln_g.reshape(1, D), conv_ln_b.reshape(1, D)
    u1, u3 = _conv_fwd(proj, 2, 3, w_conv, cb, clg, clb, S, D, TAPS, 256)
    br_b = _mm("mm_conv_proj", u3, w_cp, "nn", [F32])
    merged = _gate_merge(proj, 4, 5, br_a, br_b, S, D)
    mo = _mm("mm_out", merged, w_o, "nn", [F32])
    x1, h2 = _resid_norm2(x2, mo, g1, n2g, sc2, sh2, S, D)

    def relu2(r):
        rp = jnp.maximum(r, 0.0)
        return (rp * rp,)
    z = _mm("mm_mlp1", h2, w_m1, "nn", [BF16], epi=relu2)
    ml = _mm("mm_mlp2", z, w_m2, "nn", [F32])
    dy, dml, sq, dg2 = _loss_dy(x1, ml, tgt, g2, S, D)
    loss_part = jnp.full((1, LANES), 0.5 * jnp.sum(sq) / D, F32)

    da = _mm("mm_dz", dml, w_m2, "nt", [BF16], epi=lambda r, zz: (r * 2.0 * jnp.sqrt(zz.astype(F32)),),
             extras=(z,))
    dw_m2 = _mm("mm_dw_mlp2", z, dml, "tn", [BF16])
    dw_m1 = _mm("mm_dw_mlp1", h2, da, "tn", [BF16])
    dh2 = _mm("mm_dh2", da, w_m1, "nt", [F32])
    dx1, dmo, dsh2, dsc2, dn2g, dg1 = _norm_bwd("norm2_bwd", x1, dh2, dy, n2g, sc2, S, D, extra=(mo, g1))
    dmerged = _mm("mm_dmerged", dmo, w_o, "nt", [F32])
    dw_o = _mm("mm_dw_out", merged, dmo, "tn", [BF16])
    dproj, dba, dbb = _gate_bwd(dmerged, proj, 4, 5, br_a, br_b, S, D, into=(lax.empty((S, NP), BF16), 2))
    do = _mm("mm_do", dba, w_ap, "nt", [BF16])
    dw_ap = _mm("mm_dw_attn_proj", o, dba, "tn", [BF16])
    du3 = _mm("mm_du3", dbb, w_cp, "nt", [F32])
    dw_cp = _mm("mm_dw_conv_proj", u3, dbb, "tn", [BF16])
    core = ci.astype(jnp.int32).reshape(1)
    chip1 = chip.astype(jnp.int32).reshape(1)
    halves = lambda p: p.astype(BF16).reshape(4, 2, p.shape[1] // 2, p.shape[2])
    names = ["w_in", "w_attn_proj", "w_conv_proj", "w_out", "w_mlp1", "w_mlp2"]
    parts = [halves(p) for p in (dw_ap.reshape(4, DS, D), dw_cp.reshape(4, DS, D), dw_o.reshape(4, DS, D),
                                 jnp.transpose(dw_m1.reshape(D, 4, DFF_S), (1, 0, 2)), dw_m2.reshape(4, DFF_S, D))]
    dproj, dcw, dcb, dclg, dclb, *theirs = _conv_bwd(
        du3, u1, proj, 2, 3, w_conv, clg, clb, S, D, TAPS, 256, into=(dproj, 1), ride=(_pair_halves_plan, parts))
    chip_parts =[_sum_pair("sum_pair_" + nm, core, p, t) for nm, p, t in zip(names[1:], parts, theirs)]

    delta_c, lse_c = _delta_prep(do, o32, lse_b, grp, sel, S, D)
    to_rows = lambda t: t[:, :NH].T.reshape(NH, NQ, 1, TQ)
    dkn, dproj, dqs, dfq_r, dfk_b, *recvd = _flash_bwd(
        qs, kn, vb, do, f_cum, to_rows(lse_c), to_rows(delta_c), S, D, HD, TQ,
        dv_into=(dproj, 6 * D // LANES), ride=(_scatter_plan, chip_parts))
    dproj, sq_q, sq_k = _qk_bwd(proj, dqs, dkn, gq_t, gk_t, grp, grp_t, S, D, HD, into=(dproj, 0))
    to_cols = lambda r: jnp.pad(r.reshape(NH, S).T, ((0, 0), (0, LANES - NH)))
    dfq_pad = to_cols(dfq_r)
    dfk_pad = jnp.pad(dfk_b[:, ::HD], ((0, 0), (0, LANES - NH)))
    dproj, dbf = _fgate_bwd(dfk_pad, dfq_pad, proj, FCOL, bf_pad, tri_u, NH, S, FW, into=(dproj, 7 * D // FW))
    dw_in_p = _mm("mm_dw_in", h, dproj, "tn", [BF16])
    def shard_cols(k):
        out = []
        for (a, b, start) in sorted(SEGS):
            lo, hi = max(a, k * DIN_S), min(b, (k + 1) * DIN_S)
            if lo < hi:
                out.append(dw_in_p[:, start + lo - a:start + hi - a])
        return jnp.concatenate(out, axis=1)

    part_in = halves(jnp.stack([shard_cols(k) for k in range(4)]))
    (their_in,) = _exchange("rs_pair_w_in", [part_in], _pair_halves_plan)
    chip_in = _sum_pair("sum_pair_w_in", core, part_in, their_in)
    dh, recv_in = _mm("mm_dh", dproj, w_in_p, "nt", [F32], ride=(_scatter_plan, [chip_in]))
    gx, dsh1, dsc1, dn1g = _norm_bwd("norm1_bwd", x2, dh, dx1, n1g, sc1, S, D)

    packed = jnp.concatenate([dsh1, dsc1, dg1, dsh2, dsc2, dg2, dn1g, dcb, dclg, dclb, dn2g,
                              sq_q, sq_k, dbf, loss_part], axis=1)
    small_all, dcw_all = _ag_small("ag_small_grads", [packed, dcw])
    small = _sum_slots("sum_small", small_all.reshape(8, 1, -1)).reshape(1, -1)
    dmod_sum = small[:, :6 * D]
    seg = lambda k: small[:, (6 + k) * D:(7 + k) * D]
    g_n1g, g_cb, g_clg, g_clb, g_n2g = seg(0), seg(1), seg(2), seg(3), seg(4)
    g_qn = _sum_slots("sum_qn", seg(5).reshape(NH, 1, HD))
    g_kn = _sum_slots("sum_kn", seg(6).reshape(NH, 1, HD))
    g_bf = small[:, 13 * D:13 * D + NH]
    loss = small[0, 13 * D + LANES]
    dcw_mine = lax.dynamic_slice(dcw_all[:, :TAPS, :], (0, 0, chip * CS), (8, TAPS, CS))

    dmod_all = small_all.reshape(8, -1)[:, :6 * D]
    dmod_cols = lax.dynamic_slice(dmod_all, (0, chip * ADA_S), (8, ADA_S))
    c_t_pad = jnp.pad(c_all.T, ((0, 0), (0, LANES - 8)))
    g_wada = _ada_wgrad(c_t_pad, jnp.pad(dmod_cols, ((0, LANES - 8), (0, 0))))

    sums =[_sum_chips("sum_" + nm, chip1, p, r)
            for nm, p, r in zip(names, [chip_in] + chip_parts, [recv_in] + list(recvd))]

    res = {}
    outs = _adamw("adamw_w_ada", w_ada.reshape(D, ADA_S), m_w_ada.reshape(D, ADA_S),
                  v_w_ada.reshape(D, ADA_S), g_wada.reshape(1, D, ADA_S), ride=(_pair_swap_plan, sums))
    res["w_ada"] = [t.reshape(w_ada.shape) for t in outs[:4]]
    big = {nm: (a, b) for nm, a, b in zip(names, sums, outs[4:])}
    big_w = {"w_in": (w_in, m_w_in, v_w_in), "w_attn_proj": (w_attn_proj, m_w_attn_proj, v_w_attn_proj),
             "w_conv_proj": (w_conv_proj, m_w_conv_proj, v_w_conv_proj), "w_out": (w_out, m_w_out, v_w_out),
             "w_mlp1": (w_mlp1, m_w_mlp1, v_w_mlp1), "w_mlp2": (w_mlp2, m_w_mlp2, v_w_mlp2)}
    for nm in names:
        shp = big_w[nm][0].shape
        outs = _adamw_halves("adamw_" + nm, core, *[t.reshape(shp[1], shp[2]) for t in big_w[nm]], *big[nm])
        res[nm] = [t.reshape(shp) for t in outs]
    outs = _adamw("adamw_conv_w", conv_w.reshape(TAPS, CS), m_conv_w.reshape(TAPS, CS),
                  v_conv_w.reshape(TAPS, CS), dcw_mine)
    res["conv_w"] = [t.reshape(conv_w.shape) for t in outs]

    small_w = [("b_ada", b_ada, m_b_ada, v_b_ada, dmod_sum), ("norm1_g", norm1_g, m_norm1_g, v_norm1_g, g_n1g),
               ("b_forget", b_forget, m_b_forget, v_b_forget, g_bf),
               ("q_norm_g", q_norm_g, m_q_norm_g, v_q_norm_g, g_qn),
               ("k_norm_g", k_norm_g, m_k_norm_g, v_k_norm_g, g_kn),
               ("conv_b", conv_b, m_conv_b, v_conv_b, g_cb), ("conv_ln_g", conv_ln_g, m_conv_ln_g, v_conv_ln_g, g_clg),
               ("conv_ln_b", conv_ln_b, m_conv_ln_b, v_conv_ln_b, g_clb),
               ("norm2_g", norm2_g, m_norm2_g, v_norm2_g, g_n2g)]
    outs = _adamw_many("adamw_small", [[t[i].reshape(1, -1) for t in small_w] for i in (1, 2, 3, 4)])
    for j, (nm, w_, _, _, _) in enumerate(small_w):
        res[nm] = [outs[i][j].reshape(w_.shape) for i in range(4)]

    order = ["w_ada", "b_ada", "norm1_g", "w_in", "b_forget", "q_norm_g", "k_norm_g", "w_attn_proj", "conv_w",
             "conv_b", "conv_ln_g", "conv_ln_b", "w_conv_proj", "w_out", "norm2_g", "w_mlp1", "w_mlp2"]
    return (loss, gx.reshape(x.shape), *[res[n][0] for n in order], *[res[n][1] for n in order],
            *[res[n][2] for n in order], *[res[n][3] for n in order])
```

```python
import functools

import jax
import jax.numpy as jnp
from jax import lax
from jax.experimental import pallas as pl
from jax.experimental.pallas import tpu as pltpu

F32 = jnp.float32
BF16 = jnp.bfloat16
MESH = pl.DeviceIdType.MESH
ANY = pl.BlockSpec(memory_space=pl.ANY)

NORM_EPS = 1e-6
ADAM_LR = 0.001
ADAM_B1 = 0.9
ADAM_B2 = 0.999
ADAM_EPS = 1e-08
ADAM_WD = 0.01
ADAM_STEP = 10
LANES = 128
SUBLANES = 8
HALO = 32
CONV_ROWS = 32
CONV_TAPS = 4
NEG = -1e30
VMEM_LIMIT = 56 * 1024 * 1024


def _pcall(body, **kw):
    return pl.pallas_call(body, **kw)


def _cparams(sem=None):
    if sem is None:
        return pltpu.CompilerParams(vmem_limit_bytes=VMEM_LIMIT)
    return pltpu.CompilerParams(dimension_semantics=sem, vmem_limit_bytes=VMEM_LIMIT)


def _sig(x):
    return 1.0 / (1.0 + jnp.exp(-x))


def _split3(x):
    x1 = x.astype(BF16)
    r = x - x1.astype(F32)
    x2 = r.astype(BF16)
    x3 = (r - x2.astype(F32)).astype(BF16)
    return x1, x2, x3


def _dot_rs(x, e, terms=3):
    out = None
    for t in _split3(x)[:terms]:
        d = jnp.dot(t, e, preferred_element_type=F32)
        out = d if out is None else out + d
    return out


def _dot_ls(e, x):
    out = None
    for t in _split3(x):
        d = jnp.dot(e, t, preferred_element_type=F32)
        out = d if out is None else out + d
    return out


def _tile(n, want):
    if n <= want:
        return n
    t = want - want % LANES
    while n % t:
        t -= LANES
    assert t > 0, (n, want)
    return t


_DIMS = {"nn": ((1,), (0,)), "nt": ((1,), (1,)), "tn": ((0,), (0,))}


def _mm(name, a, b, mode, out_dtypes, epi=None, extras=(), tm=1024, tn=1024, tk=4096, ride=None):
    if mode == "nn":
        (M, K), (_, N) = a.shape, b.shape
    elif mode == "nt":
        (M, K), (N, _) = a.shape, b.shape
    else:
        (K, M), (_, N) = a.shape, b.shape
    tm, tn, tk = _tile(M, tm), _tile(N, tn), _tile(K, tk)
    nm, nn, nk = M // tm, N // tn, K // tk
    ne, no = len(extras), len(out_dtypes)
    dims = (_DIMS[mode], ((), ()))
    r_in, r_ispec, r_ospec, r_oshape, r_scratch, r_hook = _ride(ride, 2 + ne, no)

    def kern(*refs):
        a_ref, b_ref = refs[0], refs[1]
        e_refs = refs[2:2 + ne]
        o_refs = refs[2 + ne + len(r_in):2 + ne + len(r_in) + no]
        i, j, k = pl.program_id(0), pl.program_id(1), pl.program_id(2)
        before, after = r_hook(refs, (i == 0) & (j == 0) & (k == 0), (i == nm // 2) & (j == 0) & (k == 0),
                               (i == nm - 1) & (j == nn - 1) & (k == nk - 1))
        before()
        d = lax.dot_general(a_ref[...], b_ref[...], dims, preferred_element_type=F32)

        def finish(r):
            outs = (r,) if epi is None else epi(r, *[e[...] for e in e_refs])
            for o_ref, o in zip(o_refs, outs):
                o_ref[...] = o.astype(o_ref.dtype)

        if nk == 1:
            finish(d)
        else:
            acc = refs[-1]

            @pl.when(k == 0)
            def _():
                acc[...] = d

            @pl.when((k > 0) & (k < nk - 1))
            def _():
                acc[...] += d

            @pl.when(k == nk - 1)
            def _():
                finish(acc[...] + d)
        after()

    if mode == "tn":
        a_spec = pl.BlockSpec((tk, tm), lambda i, j, k: (k, i))
    else:
        a_spec = pl.BlockSpec((tm, tk), lambda i, j, k: (i, k))
    if mode == "nt":
        b_spec = pl.BlockSpec((tn, tk), lambda i, j, k: (j, k))
    else:
        b_spec = pl.BlockSpec((tk, tn), lambda i, j, k: (k, j))
    mn_spec = pl.BlockSpec((tm, tn), lambda i, j, k: (i, j))
    outs = _pcall(
        kern, name=name, grid=(nm, nn, nk),
        in_specs=[a_spec, b_spec] + [mn_spec] * ne + r_ispec,
        out_specs=[mn_spec] * no + r_ospec,
        out_shape=[jax.ShapeDtypeStruct((M, N), dt) for dt in out_dtypes] + r_oshape,
        scratch_shapes=r_scratch + ([pltpu.VMEM((tm, tn), F32)] if nk > 1 else []),
        compiler_params=_cparams(("arbitrary",) * 3 if ride else ("parallel", "parallel", "arbitrary")),
    )(a, b, *extras, *r_in)
    return outs[0] if len(outs) == 1 else outs


def _rowcall(name, body, S, ts, row_ins, vec_ins, row_outs, vec_outs, into=None):
    ts = min(ts, S)
    nri, nvi, nro, nvo = len(row_ins), len(vec_ins), len(row_outs), len(vec_outs)
    na = 0 if into is None else 1

    def kern(*refs):
        ins = refs[:nri + nvi]
        outs = refs[nri + nvi + na:]
        if nvo:
            @pl.when(pl.program_id(0) == 0)
            def _():
                for r in outs[nro:]:
                    r[...] = jnp.zeros(r.shape, r.dtype)
        body(*ins, *outs)

    in_specs = [pl.BlockSpec((ts, w), functools.partial(lambda i, cb: (i, cb), cb=cb))
                for (_, w, cb) in row_ins]
    in_specs += [pl.BlockSpec(v.shape, lambda i: (0, 0)) for v in vec_ins]
    out_specs = [pl.BlockSpec((ts, w), lambda i: (i, 0)) for (w, _) in row_outs]
    out_specs += [pl.BlockSpec((r, w), lambda i: (0, 0)) for (r, w) in vec_outs]
    out_shape = [jax.ShapeDtypeStruct((S, w), dt) for (w, dt) in row_outs]
    out_shape += [jax.ShapeDtypeStruct((r, w), F32) for (r, w) in vec_outs]
    extra, aliases = [], {}
    if into is not None:
        buf, cb = into
        assert buf.dtype == row_outs[0][1] and buf.shape[0] == S
        in_specs.append(ANY)
        out_specs[0] = pl.BlockSpec((ts, row_outs[0][0]), lambda i: (i, cb))
        out_shape[0] = jax.ShapeDtypeStruct(buf.shape, buf.dtype)
        extra, aliases = [buf], {nri + nvi: 0}
    return _pcall(
        kern, name=name, grid=(S // ts,), in_specs=in_specs, out_specs=out_specs,
        out_shape=out_shape, input_output_aliases=aliases,
        compiler_params=_cparams(("arbitrary",) if nvo else ("parallel",)),
    )(*[a for (a, _, _) in row_ins], *vec_ins, *extra)


def _csum(x):
    return jnp.sum(x, axis=0, keepdims=True)


def _norm_mod(name, x, g, sc, sh, S, D):
    def body(x_ref, g_ref, sc_ref, sh_ref, h_ref):
        xv = x_ref[...]
        r = lax.rsqrt(jnp.mean(xv * xv, axis=-1, keepdims=True) + NORM_EPS)
        h_ref[...] = ((xv * r * g_ref[...]) * (1.0 + sc_ref[...]) + sh_ref[...]).astype(BF16)
    return _rowcall(name, body, S, 512, [(x, D, 0)], [g, sc, sh], [(D, BF16)], [])[0]


def _head_rstd(v, grp, grp_t, hd):
    ss = _dot_rs(v * v, grp, 2) * (1.0 / hd)
    r = lax.rsqrt(ss + NORM_EPS)
    return _dot_rs(r, grp_t, 2)


def _qk_prep(proj, vcol, gq, gk, grp, grp_t, S, D, hd):
    scale = hd ** -0.5

    def body(q_ref, k_ref, v_ref, gq_ref, gk_ref, g_ref, gt_ref, qs_ref, kn_ref, vb_ref):
        q = q_ref[...]
        k = k_ref[...]
        rq = _head_rstd(q, g_ref[...], gt_ref[...], hd)
        rk = _head_rstd(k, g_ref[...], gt_ref[...], hd)
        qs_ref[...] = ((q * rq * gq_ref[...]).astype(BF16).astype(F32) * scale).astype(BF16)
        kn_ref[...] = (k * rk * gk_ref[...]).astype(BF16)
        vb_ref[...] = v_ref[...].astype(BF16)

    return _rowcall("qk_prep", body, S, 256, [(proj, D, 0), (proj, D, 1), (proj, D, vcol)],
                    [gq, gk, grp, grp_t], [(D, BF16)] * 3, [])


def _fgate_fwd(proj, fcol, bf_pad, tri, S):
    ch = tri.shape[0]

    def body(f_ref, b_ref, tri_ref, out_ref):
        carry = jnp.zeros((1, LANES), F32)
        for c in range(S // ch):
            z = f_ref[c * ch:(c + 1) * ch, :] + b_ref[...]
            lf = jnp.minimum(z, 0.0) - jnp.log(1.0 + jnp.exp(-jnp.abs(z)))
            out_ref[c * ch:(c + 1) * ch, :] = _dot_ls(tri_ref[...], lf) + carry
            carry = carry + _csum(lf)

    return _rowcall("fgate_fwd", body, S, S, [(proj, LANES, fcol)], [bf_pad, tri],
                    [(LANES, F32)], [])[0]


def _fgate_bwd(dfk, dfq, proj, fcol, bf_pad, tri_u, nh, S, fw, into):
    ch = tri_u.shape[0]

    def body(d_ref, dq_ref, f_ref, b_ref, tri_ref, df_ref, db_ref):
        if fw > LANES:
            df_ref[:, LANES:fw] = jnp.zeros((S, fw - LANES), BF16)
        lane = lax.broadcasted_iota(jnp.int32, (ch, LANES), 1)
        carry = jnp.zeros((1, LANES), F32)
        tot = jnp.zeros((1, LANES), F32)
        for c in reversed(range(S // ch)):
            d = d_ref[c * ch:(c + 1) * ch, :] + dq_ref[c * ch:(c + 1) * ch, :]
            rc = _dot_ls(tri_ref[...], d) + carry
            carry = carry + _csum(d)
            z = f_ref[c * ch:(c + 1) * ch, :] + b_ref[...]
            df = jnp.where(lane < nh, rc * _sig(-z), 0.0)
            df_ref[c * ch:(c + 1) * ch, 0:LANES] = df.astype(BF16)
            tot = tot + _csum(df)
        db_ref[...] += tot

    return _rowcall("fgate_bwd", body, S, S, [(dfk, LANES, 0), (dfq, LANES, 0), (proj, LANES, fcol)],
                    [bf_pad, tri_u], [(fw, BF16)], [(1, LANES)], into=into)


def _keep(v, mask):
    return jnp.where(mask, v.astype(F32), 0.0).astype(BF16)


def _lane_col(blk, lane, at):
    return jnp.sum(jnp.where(lane == at, blk, 0.0), axis=-1, keepdims=True)


def _flash_fwd(qs, kn, vb, fk_r, S, D, hd, tq, ride=None):
    hp, nq = D // LANES, S // tq
    r_in, r_ispec, r_ospec, r_oshape, r_scratch, r_hook = _ride(ride, 4, 3)

    def kern(*refs):
        q_ref, k_ref, v_ref, fk_ref = refs[:4]
        o_ref, o32_ref, lse_ref = refs[4 + len(r_in):7 + len(r_in)]
        hi, qi = pl.program_id(0), pl.program_id(1)
        before, after = r_hook(refs, (hi == 0) & (qi == 0), (hi == hp // 2) & (qi == 0),
                               (hi == hp - 1) & (qi == nq - 1))
        before()
        lane = lax.broadcasted_iota(jnp.int32, (tq, LANES), 1)
        row = lax.broadcasted_iota(jnp.int32, (tq, tq), 0)
        col = lax.broadcasted_iota(jnp.int32, (tq, tq), 1)
        hms = [(lane >= j * hd) & (lane < (j + 1) * hd) for j in range(2)]
        q = q_ref[...]
        qms = [_keep(q, hm) for hm in hms]

        s_a, s_b = refs[-2], refs[-1]

        def put(s_ref, ki):
            off = pl.multiple_of(ki * tq, tq)
            k = k_ref[pl.ds(off, tq), :]
            for j in range(2):
                s_ref[j] = lax.dot_general(qms[j], k, (((1,), (1,)), ((), ())), preferred_element_type=F32)

        def update(ki, s_ref, state, masked):
            off = pl.multiple_of(ki * tq, tq)
            v = v_ref[pl.ds(off, tq), :].astype(F32)
            new = []
            for j in range(2):
                m_old, acc = state[j]
                s = s_ref[j] - fk_ref[j, ki]
                if masked:
                    s = jnp.where(col <= row, s, NEG)
                m_new = jnp.maximum(m_old, jnp.max(s, axis=-1, keepdims=True))
                alpha = jnp.exp(m_old - m_new)
                p = jnp.exp(s - m_new)
                v1 = jnp.where(hms[j], v, 1.0).astype(BF16)
                acc = alpha * acc + jnp.dot(p.astype(BF16), v1, preferred_element_type=F32)
                new.append((m_new, acc))
            return tuple(new)

        def pair(p, state):
            put(s_b, 2 * p + 1)
            state = update(2 * p, s_a, state, False)
            put(s_a, 2 * p + 2)
            return update(2 * p + 1, s_b, state, False)

        def odd_tail(state):
            put(s_b, qi)
            return update(qi, s_b, update(qi - 1, s_a, state, False), True)

        init = tuple((jnp.full((tq, 1), NEG, F32), jnp.zeros((tq, LANES), F32)) for _ in range(2))
        put(s_a, 0)
        state = lax.fori_loop(0, qi // 2, pair, init)
        (m0, a0), (m1, a1) = lax.cond(qi % 2 == 1, odd_tail, lambda st: update(qi, s_a, st, True), state)
        l0, l1 = pltpu.roll(a0, hd, 1), pltpu.roll(a1, hd, 1)
        first = lane < hd
        ov = jnp.where(first, a0 / l0, a1 / l1)
        o_ref[...] = ov.astype(BF16)
        o32_ref[...] = ov
        lse_ref[...] = jnp.where(first, m0 + jnp.log(l0), m1 + jnp.log(l1))
        after()

    qspec = pl.BlockSpec((tq, LANES), lambda h, i: (i, h))
    fullspec = pl.BlockSpec((S, LANES), lambda h, i: (0, h))
    return _pcall(
        kern, name="flash_fwd", grid=(hp, nq),
        in_specs=[qspec, fullspec, fullspec,
                  pl.BlockSpec((2, nq, 1, tq), lambda h, i: (h, 0, 0, 0))] + r_ispec,
        out_specs=[qspec, qspec, qspec] + r_ospec,
        out_shape=[jax.ShapeDtypeStruct((S, D), BF16), jax.ShapeDtypeStruct((S, D), F32),
                   jax.ShapeDtypeStruct((S, D), F32)] + r_oshape,
        scratch_shapes=r_scratch + [pltpu.VMEM((2, tq, tq), F32)] * 2,
        compiler_params=_cparams(("arbitrary", "arbitrary")),
    )(qs, kn, vb, fk_r, *r_in)


def _flash_bwd(qs, kn, vb, do, fk_b, lse_r, delta_r, S, D, hd, tq, dv_into, ride=None):
    hp, nq = D // LANES, S // tq
    dbuf_hbm, dv_col = dv_into
    r_in, r_ispec, r_ospec, r_oshape, r_scratch, r_hook = _ride(ride, 8, 5)

    def kern(*refs):
        q_ref, do_ref, k_ref, v_ref, fk_ref, lse_ref, dl_ref = refs[:7]
        dk_ref, dv_ref, dq_ref, dfq_ref, dfk_ref = refs[8 + len(r_in):13 + len(r_in)]
        hi, ki = pl.program_id(0), pl.program_id(1)
        before, after = r_hook(refs, (hi == 0) & (ki == 0), (hi == hp // 2) & (ki == 0),
                               (hi == hp - 1) & (ki == nq - 1))
        before()
        lane = lax.broadcasted_iota(jnp.int32, (tq, LANES), 1)
        row = lax.broadcasted_iota(jnp.int32, (tq, tq), 0)
        col = lax.broadcasted_iota(jnp.int32, (tq, tq), 1)
        hms = [(lane >= j * hd) & (lane < (j + 1) * hd) for j in range(2)]
        k = k_ref[...]
        v = v_ref[...]
        fkb = fk_ref[...]
        kms = [_keep(k, hm) for hm in hms]
        vms = [_keep(v, hm) for hm in hms]
        fks = [_lane_col(fkb, lane, 2 * hi + j) for j in range(2)]

        @pl.when(ki == 0)
        def _():
            dfq_ref[...] = jnp.zeros(dfq_ref.shape, F32)
            dq_ref[...] = jnp.zeros(dq_ref.shape, F32)

        def step(qi, acc, masked):
            dk, dv, dfs = acc
            off = pl.multiple_of(qi * tq, tq)
            q = q_ref[pl.ds(off, tq), :]
            g = do_ref[pl.ds(off, tq), :]
            dq = None
            new_dfs = []
            for j in range(2):
                qm = _keep(q, hms[j])
                gm = _keep(g, hms[j])
                st = lax.dot_general(kms[j], q, (((1,), (1,)), ((), ())), preferred_element_type=F32)
                st = st - fks[j]
                if masked:
                    st = jnp.where(row <= col, st, NEG)
                pt = jnp.exp(st - lse_ref[j, qi])
                dv = dv + jnp.dot(pt.astype(BF16), gm, preferred_element_type=F32)
                dpt = lax.dot_general(vms[j], g, (((1,), (1,)), ((), ())), preferred_element_type=F32)
                dst = pt * (dpt - dl_ref[j, qi])
                dsb = dst.astype(BF16)
                dk = dk + jnp.dot(dsb, qm, preferred_element_type=F32)
                t = lax.dot_general(dsb, kms[j], (((0,), (0,)), ((), ())), preferred_element_type=F32)
                dq = t if dq is None else dq + t
                dfq_ref[j, qi] += jnp.sum(dst, axis=0, keepdims=True)
                new_dfs.append(dfs[j] - jnp.sum(dst, axis=1, keepdims=True))
            dq_ref[pl.ds(off, tq), :] += dq
            return dk, dv, tuple(new_dfs)

        zero = jnp.zeros((tq, LANES), F32)
        zcol = jnp.zeros((tq, 1), F32)
        acc = step(ki, (zero, zero, (zcol, zcol)), True)
        dk, dv, dfs = lax.fori_loop(ki + 1, nq, lambda qi, a: step(qi, a, False), acc)
        dk_ref[...] = dk.astype(BF16)
        dv_ref[...] = dv.astype(BF16)
        dfk_ref[...] = jnp.where(lane < hd, dfs[0], dfs[1])
        after()

    kspec = pl.BlockSpec((tq, LANES), lambda h, i: (i, h))
    fullspec = pl.BlockSpec((S, LANES), lambda h, i: (0, h))
    rowspec = pl.BlockSpec((2, nq, 1, tq), lambda h, i: (h, 0, 0, 0))
    return _pcall(
        kern, name="flash_bwd", grid=(hp, nq),
        in_specs=[fullspec, fullspec, kspec, kspec, pl.BlockSpec((tq, LANES), lambda h, i: (i, 0)),
                  rowspec, rowspec, ANY] + r_ispec,
        out_specs=[kspec, pl.BlockSpec((tq, LANES), lambda h, i: (i, h + dv_col)), fullspec, rowspec, kspec]
        + r_ospec,
        out_shape=[jax.ShapeDtypeStruct((S, D), BF16), jax.ShapeDtypeStruct(dbuf_hbm.shape, BF16),
                   jax.ShapeDtypeStruct((S, D), F32), jax.ShapeDtypeStruct((2 * hp, nq, 1, tq), F32),
                   jax.ShapeDtypeStruct((S, D), F32)] + r_oshape,
        scratch_shapes=r_scratch, input_output_aliases={7: 1},
        compiler_params=_cparams(("arbitrary", "arbitrary")),
    )(qs, do, kn, vb, fk_b, lse_r, delta_r, dbuf_hbm, *r_in)


def _delta_prep(do, o, lse_b, grp, sel, S, D):
    def body(g_ref, o_ref, l_ref, e_ref, s_ref, dl_ref, lse_ref):
        prod = g_ref[...].astype(F32) * o_ref[...]
        dl_ref[...] = _dot_rs(prod, e_ref[...], 2)
        lse_ref[...] = _dot_rs(l_ref[...], s_ref[...])
    return _rowcall("delta_prep", body, S, 256, [(do, D, 0), (o, D, 0), (lse_b, D, 0)], [grp, sel],
                    [(LANES, F32), (LANES, F32)], [])


def _qk_bwd(proj, dqs, dkn, gq, gk, grp, grp_t, S, D, hd, into):
    scale = hd ** -0.5

    def one(x, dn, gain, e, et):
        r = _head_rstd(x, e, et, hd)
        xh = x * r
        t = dn * gain
        mean = _dot_rs(_dot_rs(t * xh, e, 2), et, 2) * (1.0 / hd)
        return r * (t - xh * mean), _csum(dn * xh)

    def body(q_ref, k_ref, dq_ref, dk_ref, gq_ref, gk_ref, e_ref, et_ref, o_ref, sq_ref, sk_ref):
        e, et = e_ref[...], et_ref[...]
        dq, sq = one(q_ref[...], dq_ref[...].astype(F32) * scale, gq_ref[...], e, et)
        dk, sk = one(k_ref[...], dk_ref[...].astype(F32), gk_ref[...], e, et)
        o_ref[:, 0:D] = dq.astype(BF16)
        o_ref[:, D:2 * D] = dk.astype(BF16)
        sq_ref[...] += sq
        sk_ref[...] += sk

    return _rowcall("qk_bwd", body, S, 256,
                    [(proj, D, 0), (proj, D, 1), (dqs, D, 0), (dkn, D, 0)],
                    [gq, gk, grp, grp_t], [(2 * D, BF16)], [(1, D)] * 2, into=into)


def _shift_copies(buf, sh, ts):
    for b in range(1, SUBLANES):
        sh[b - 1] = buf[b:b + ts + HALO - SUBLANES, :]


def _rows_from(buf, sh, o, ts):
    a, b = divmod(o, SUBLANES)
    if b == 0:
        return buf[o:o + ts, :]
    return sh[b - 1, SUBLANES * a:SUBLANES * a + ts, :]


def _conv_fwd(proj, acol, bcol, w_pad, cb, lg, lb, S, C, taps, ts):
    ts = min(ts, S)

    def kern(a_ref, b_ref, w_ref, cb_ref, lg_ref, lb_ref, u1_ref, u3_ref, ubuf, ush):
        @pl.when(pl.program_id(0) == 0)
        def _():
            ubuf[0:HALO, :] = jnp.zeros((HALO, C), F32)

        ubuf[HALO:HALO + ts, :] = a_ref[...] * _sig(b_ref[...])
        _shift_copies(ubuf, ush, ts)
        acc = jnp.zeros((ts, C), F32) + cb_ref[...]
        for k in range(taps):
            acc = acc + w_ref[k:k + 1, :] * _rows_from(ubuf, ush, HALO - (taps - 1) + k, ts)
        u1_ref[...] = acc
        mu = jnp.mean(acc, axis=-1, keepdims=True)
        xc = acc - mu
        rstd = lax.rsqrt(jnp.mean(xc * xc, axis=-1, keepdims=True) + NORM_EPS)
        u2 = xc * rstd * lg_ref[...] + lb_ref[...]
        u3_ref[...] = (u2 * _sig(u2)).astype(BF16)
        ubuf[0:HALO, :] = ubuf[ts:ts + HALO, :]

    vec = lambda a: pl.BlockSpec(a.shape, lambda i: (0, 0))
    return _pcall(
        kern, name="conv_fwd", grid=(S // ts,),
        in_specs=[pl.BlockSpec((ts, C), lambda i: (i, acol)), pl.BlockSpec((ts, C), lambda i: (i, bcol)),
                  vec(w_pad), vec(cb), vec(lg), vec(lb)],
        out_specs=[pl.BlockSpec((ts, C), lambda i: (i, 0))] * 2,
        out_shape=[jax.ShapeDtypeStruct((S, C), F32), jax.ShapeDtypeStruct((S, C), BF16)],
        scratch_shapes=[pltpu.VMEM((HALO + ts, C), F32),
                        pltpu.VMEM((SUBLANES - 1, HALO + ts - SUBLANES, C), F32)],
        compiler_params=_cparams(("arbitrary",)),
    )(proj, proj, w_pad, cb, lg, lb)


def _conv_bwd(du3, u1, proj, acol, bcol, w_pad, lg, lb, S, C, taps, ts, into, ride=None):
    ts = min(ts, S)
    dbuf_hbm, dcol = into
    r_in, r_ispec, r_ospec, r_oshape, r_scratch, r_hook = _ride(ride, 12, 5)
    nt = S // ts
    hb = ts // HALO

    def ln_bwd(g, u, lgv, lbv):
        mu = jnp.mean(u, axis=-1, keepdims=True)
        xc = u - mu
        rstd = lax.rsqrt(jnp.mean(xc * xc, axis=-1, keepdims=True) + NORM_EPS)
        xh = xc * rstd
        u2 = xh * lgv + lbv
        s = _sig(u2)
        du2 = g * (s + u2 * s * (1.0 - s))
        dxh = du2 * lgv
        du1 = rstd * (dxh - jnp.mean(dxh, axis=-1, keepdims=True)
                      - xh * jnp.mean(dxh * xh, axis=-1, keepdims=True))
        return du1, du2, xh

    def kern(*refs):
        g_ref, u_ref, a_ref, b_ref, gn_ref, un_ref, ap_ref, bp_ref, w_ref, lg_ref, lb_ref = refs[:11]
        dg_ref, dw_ref, dcb_ref, dlg_ref, dlb_ref = refs[12 + len(r_in):17 + len(r_in)]
        dbuf, ubuf, dsh, ush = refs[-4:]
        i = pl.program_id(0)
        before, after = r_hook(refs, i == 0, i == nt // 2, i == nt - 1)
        before()

        @pl.when(i == 0)
        def _():
            dw_ref[...] = jnp.zeros(dw_ref.shape, F32)
            dcb_ref[...] = jnp.zeros(dcb_ref.shape, F32)
            dlg_ref[...] = jnp.zeros(dlg_ref.shape, F32)
            dlb_ref[...] = jnp.zeros(dlb_ref.shape, F32)

        lgv, lbv = lg_ref[...], lb_ref[...]
        du1, du2, xh = ln_bwd(g_ref[...], u_ref[...], lgv, lbv)
        dbuf[0:ts, :] = du1
        du1n, _, _ = ln_bwd(gn_ref[...], un_ref[...], lgv, lbv)
        dbuf[ts:ts + HALO, :] = jnp.where(i < nt - 1, du1n, 0.0)
        a = a_ref[...]
        sb = _sig(b_ref[...])
        ubuf[HALO:HALO + ts, :] = a * sb
        ubuf[0:HALO, :] = jnp.where(i > 0, ap_ref[...] * _sig(bp_ref[...]), 0.0)
        dcb_ref[...] += _csum(du1)
        dlg_ref[...] += _csum(du2 * xh)
        dlb_ref[...] += _csum(du2)
        _shift_copies(dbuf, dsh, ts)
        _shift_copies(ubuf, ush, ts)
        for r0 in range(0, ts, CONV_ROWS):
            du0 = jnp.zeros((CONV_ROWS, C), F32)
            for k in range(taps):
                du0 = du0 + w_ref[k:k + 1, :] * _rows_from(dbuf, dsh, r0 + taps - 1 - k, CONV_ROWS)
            ac = a_ref[r0:r0 + CONV_ROWS, :]
            sc = _sig(b_ref[r0:r0 + CONV_ROWS, :])
            dg_ref[r0:r0 + CONV_ROWS, 0:C] = (du0 * sc).astype(BF16)
            dg_ref[r0:r0 + CONV_ROWS, C:2 * C] = (du0 * ac * sc * (1.0 - sc)).astype(BF16)
        for k0 in range(0, taps, CONV_TAPS):
            ks = range(k0, min(k0 + CONV_TAPS, taps))
            accs = [jnp.zeros((SUBLANES, C), F32) for _ in ks]
            for r0 in range(0, ts, CONV_ROWS):
                d = dbuf[r0:r0 + CONV_ROWS, :]
                for t, k in enumerate(ks):
                    prod = d * _rows_from(ubuf, ush, r0 + HALO - (taps - 1) + k, CONV_ROWS)
                    accs[t] = accs[t] + jnp.sum(prod.reshape(CONV_ROWS // SUBLANES, SUBLANES, C), axis=0)
            for t, k in enumerate(ks):
                dw_ref[k:k + 1, :] += _csum(accs[t])
        after()

    vec = lambda a: pl.BlockSpec(a.shape, lambda i: (0, 0))
    tile = lambda cb: pl.BlockSpec((ts, C), functools.partial(lambda i, cb: (i, cb), cb=cb))
    nxt = lambda cb: pl.BlockSpec(
        (HALO, C), functools.partial(lambda i, cb: (jnp.minimum((i + 1) * hb, nt * hb - 1), cb), cb=cb))
    prv = lambda cb: pl.BlockSpec(
        (HALO, C), functools.partial(lambda i, cb: (jnp.maximum(i * hb - 1, 0), cb), cb=cb))
    return _pcall(
        kern, name="conv_bwd", grid=(nt,),
        in_specs=[tile(0), tile(0), tile(acol), tile(bcol), nxt(0), nxt(0), prv(acol), prv(bcol),
                  vec(w_pad), vec(lg), vec(lb), ANY] + r_ispec,
        out_specs=[pl.BlockSpec((ts, 2 * C), lambda i: (i, dcol))]
        + [pl.BlockSpec(w_pad.shape, lambda i: (0, 0))] + [pl.BlockSpec((1, C), lambda i: (0, 0))] * 3 + r_ospec,
        out_shape=[jax.ShapeDtypeStruct(dbuf_hbm.shape, BF16)]
        + [jax.ShapeDtypeStruct(w_pad.shape, F32)] + [jax.ShapeDtypeStruct((1, C), F32)] * 3 + r_oshape,
        scratch_shapes=r_scratch + [pltpu.VMEM((ts + HALO, C), F32), pltpu.VMEM((HALO + ts, C), F32)]
        + [pltpu.VMEM((SUBLANES - 1, HALO + ts - SUBLANES, C), F32)] * 2,
        input_output_aliases={11: 0},
        compiler_params=_cparams(("arbitrary",)),
    )(du3, u1, proj, proj, du3, u1, proj, proj, w_pad, lg, lb, dbuf_hbm, *r_in)


def _gate_merge(proj, gacol, gbcol, ba, bb, S, D):
    def body(ga_ref, gb_ref, a_ref, b_ref, out_ref):
        out_ref[...] = (_sig(ga_ref[...]) * a_ref[...] + _sig(gb_ref[...]) * b_ref[...]).astype(BF16)
    return _rowcall("gate_merge", body, S, 512,
                    [(proj, D, gacol), (proj, D, gbcol), (ba, D, 0), (bb, D, 0)], [], [(D, BF16)], [])[0]


def _gate_bwd(dm, proj, gacol, gbcol, ba, bb, S, D, into):
    def body(dm_ref, ga_ref, gb_ref, a_ref, b_ref, dg_ref, da_ref, db_ref):
        dmv = dm_ref[...]
        sa, sb = _sig(ga_ref[...]), _sig(gb_ref[...])
        da_ref[...] = (dmv * sa).astype(BF16)
        db_ref[...] = (dmv * sb).astype(BF16)
        dg_ref[:, 0:D] = (dmv * a_ref[...] * sa * (1.0 - sa)).astype(BF16)
        dg_ref[:, D:2 * D] = (dmv * b_ref[...] * sb * (1.0 - sb)).astype(BF16)
    return _rowcall("gate_bwd", body, S, 512,
                    [(dm, D, 0), (proj, D, gacol), (proj, D, gbcol), (ba, D, 0), (bb, D, 0)], [],
                    [(2 * D, BF16), (D, BF16), (D, BF16)], [], into=into)


def _resid_norm2(x, mo, g1, g, sc, sh, S, D):
    def body(x_ref, mo_ref, g1_ref, g_ref, sc_ref, sh_ref, x1_ref, h_ref):
        x1 = x_ref[...] + g1_ref[...] * mo_ref[...]
        x1_ref[...] = x1
        r = lax.rsqrt(jnp.mean(x1 * x1, axis=-1, keepdims=True) + NORM_EPS)
        h_ref[...] = ((x1 * r * g_ref[...]) * (1.0 + sc_ref[...]) + sh_ref[...]).astype(BF16)
    return _rowcall("resid_norm2", body, S, 512, [(x, D, 0), (mo, D, 0)], [g1, g, sc, sh],
                    [(D, F32), (D, BF16)], [])


def _loss_dy(x1, ml, tgt, g2, S, D):
    def body(x1_ref, ml_ref, t_ref, g2_ref, dy_ref, dml_ref, sq_ref, dg2_ref):
        mlv = ml_ref[...]
        diff = x1_ref[...] + g2_ref[...] * mlv - t_ref[...]
        dy = diff * (1.0 / D)
        dy_ref[...] = dy
        dml_ref[...] = (dy * g2_ref[...]).astype(BF16)
        sq_ref[...] += _csum(diff * diff)
        dg2_ref[...] += _csum(dy * mlv)
    return _rowcall("loss_dy", body, S, 512, [(x1, D, 0), (ml, D, 0), (tgt, D, 0)], [g2],
                    [(D, F32), (D, BF16)], [(1, D), (1, D)])


def _norm_bwd(name, xin, dh, dres, g, sc, S, D, extra=None):
    def body(*refs):
        if extra is None:
            x_ref, dh_ref, dr_ref, g_ref, sc_ref, dx_ref, dsh_ref, dsc_ref, dg_ref = refs
        else:
            (x_ref, dh_ref, dr_ref, mo_ref, g_ref, sc_ref, g1_ref,
             dx_ref, dmo_ref, dsh_ref, dsc_ref, dg_ref, dg1_ref) = refs
        xv, dhv, gv = x_ref[...], dh_ref[...], g_ref[...]
        r = lax.rsqrt(jnp.mean(xv * xv, axis=-1, keepdims=True) + NORM_EPS)
        xh = xv * r
        dsh_ref[...] += _csum(dhv)
        dsc_ref[...] += _csum(dhv * xh * gv)
        dxg = dhv * (1.0 + sc_ref[...])
        dg_ref[...] += _csum(dxg * xh)
        dxh = dxg * gv
        dx = dr_ref[...] + r * (dxh - xh * jnp.mean(dxh * xh, axis=-1, keepdims=True))
        dx_ref[...] = dx
        if extra is not None:
            dmo_ref[...] = (dx * g1_ref[...]).astype(BF16)
            dg1_ref[...] += _csum(dx * mo_ref[...])

    rows = [(xin, D, 0), (dh, D, 0), (dres, D, 0)]
    vecs = [g, sc]
    if extra is None:
        return _rowcall(name, body, S, 512, rows, vecs, [(D, F32)], [(1, D)] * 3)
    return _rowcall(name, body, S, 512, rows + [(extra[0], D, 0)], vecs + [extra[1]],
                    [(D, F32), (D, BF16)], [(1, D)] * 4)


def _ada_fwd(c_all, w, b_part):
    B, D = c_all.shape
    N = w.shape[1]
    tn = min(512, N)

    def kern(c_ref, w_ref, b_ref, o_ref):
        cv = c_ref[...]
        ca = cv * _sig(cv)
        o_ref[...] = jnp.dot(ca, w_ref[...], precision=lax.Precision.HIGHEST,
                             preferred_element_type=F32) + b_ref[...]

    return _pcall(
        kern, name="ada_fwd", grid=(N // tn,),
        in_specs=[pl.BlockSpec((B, D), lambda j: (0, 0)), pl.BlockSpec((D, tn), lambda j: (0, j)),
                  pl.BlockSpec((1, tn), lambda j: (0, j))],
        out_specs=pl.BlockSpec((B, tn), lambda j: (0, j)),
        out_shape=jax.ShapeDtypeStruct((B, N), F32),
        compiler_params=_cparams(("parallel",)),
    )(c_all, w, b_part)


def _ada_wgrad(c_t_pad, dmod_pad):
    D = c_t_pad.shape[0]
    N = dmod_pad.shape[1]
    tn = min(512, N)

    def kern(c_ref, d_ref, o_ref):
        cv = c_ref[...]
        ca = cv * _sig(cv)
        o_ref[...] = jnp.dot(ca, d_ref[...], precision=lax.Precision.HIGHEST,
                             preferred_element_type=F32)

    return _pcall(
        kern, name="ada_wgrad", grid=(N // tn,),
        in_specs=[pl.BlockSpec((D, LANES), lambda j: (0, 0)), pl.BlockSpec((LANES, tn), lambda j: (0, j))],
        out_specs=pl.BlockSpec((D, tn), lambda j: (0, j)),
        out_shape=jax.ShapeDtypeStruct((D, N), F32),
        compiler_params=_cparams(("parallel",)),
    )(c_t_pad, dmod_pad)


def _ag_small(name, arrs):
    n = len(arrs)

    def kern(*refs):
        ins, outs = refs[:n], refs[n:2 * n]
        send, recv = refs[2 * n], refs[2 * n + 1]
        x, y, c = lax.axis_index("x"), lax.axis_index("y"), lax.axis_index("c")
        me = 4 * x + 2 * y + c

        def copy(i, m, slot):
            peer = (x ^ ((m >> 2) & 1), y ^ ((m >> 1) & 1), c ^ (m & 1))
            return pltpu.make_async_remote_copy(
                src_ref=ins[i], dst_ref=outs[i].at[slot],
                send_sem=send.at[i * 7 + m - 1], recv_sem=recv.at[i * 7 + m - 1],
                device_id=peer, device_id_type=MESH)

        for i in range(n):
            outs[i][me] = ins[i][...]
            for m in range(1, 8):
                copy(i, m, me).start()
        for i in range(n):
            for m in range(1, 8):
                copy(i, m, me).wait_send()
                copy(i, m, me ^ m).wait_recv()

    vm = pl.BlockSpec(memory_space=pltpu.VMEM)
    return _pcall(
        kern, name=name, in_specs=[vm] * n, out_specs=[vm] * n,
        out_shape=[jax.ShapeDtypeStruct((8,) + a.shape, a.dtype) for a in arrs],
        scratch_shapes=[pltpu.SemaphoreType.DMA((7 * n,)), pltpu.SemaphoreType.DMA((7 * n,))],
        compiler_params=pltpu.CompilerParams(has_side_effects=True),
    )(*arrs)


def _exchange(name, arrs, plan):
    out_shape, scratch, phases = plan(arrs)

    def kern(*refs):
        for phase in phases(refs):
            phase()

    return _pcall(
        kern, name=name, in_specs=[ANY] * len(arrs), out_specs=[ANY] * len(out_shape),
        out_shape=out_shape, scratch_shapes=scratch,
        compiler_params=pltpu.CompilerParams(has_side_effects=True),
    )(*arrs)


def _ride(plan_and_arrs, n_in, n_out):
    if plan_and_arrs is None:
        return [], [], [], [], [], lambda refs, first, middle, last: ((lambda: None), (lambda: None))
    plan, arrs = plan_and_arrs
    out_shape, scratch, phases = plan(arrs)
    na, no = len(arrs), len(out_shape)

    def hook(refs, first, middle, last):
        mine = refs[n_in:n_in + na] + refs[n_in + na + n_out:]
        start, mid, finish = phases(mine)

        def before():
            pl.when(first)(start)
            pl.when(middle)(mid)

        def after():
            pl.when(last)(finish)

        return before, after

    return list(arrs), [ANY] * na, [ANY] * no, out_shape, scratch, hook


def _gather_plan(arrs):
    n = len(arrs)

    def phases(refs):
        ins, outs = refs[:n], refs[n:2 * n]
        s1, r1, s2, r2, loc = refs[2 * n:2 * n + 5]
        x, y, c = lax.axis_index("x"), lax.axis_index("y"), lax.axis_index("c")
        me = 2 * x + y

        def half(i, hc):
            hr = ins[i].shape[0] // 2
            return pl.ds(hc * hr, hr)

        def own(i):
            return pltpu.make_async_remote_copy(
                src_ref=ins[i], dst_ref=outs[i].at[me], send_sem=loc.at[i], recv_sem=loc.at[n + i],
                device_id=(x, y, 1 - c), device_id_type=MESH)

        def fetch(i, m, slot):
            px, py = x ^ ((m >> 1) & 1), y ^ (m & 1)
            return pltpu.make_async_remote_copy(
                src_ref=ins[i].at[half(i, c)], dst_ref=outs[i].at[slot, half(i, c)],
                send_sem=s1.at[i * 3 + m - 1], recv_sem=r1.at[i * 3 + m - 1],
                device_id=(px, py, c), device_id_type=MESH)

        def passed(i, m, hc):
            return pltpu.make_async_remote_copy(
                src_ref=outs[i].at[me ^ m, half(i, hc)], dst_ref=outs[i].at[me ^ m, half(i, hc)],
                send_sem=s2.at[i * 3 + m - 1], recv_sem=r2.at[i * 3 + m - 1],
                device_id=(x, y, 1 - c), device_id_type=MESH)

        def start():
            for i in range(n):
                for m in range(1, 4):
                    fetch(i, m, me).start()
            for i in range(n):
                own(i).start()

        def mid():
            for i in range(n):
                for m in range(1, 4):
                    fetch(i, m, me ^ m).wait_recv()
                    passed(i, m, c).start()

        def finish():
            for i in range(n):
                own(i).wait()
                for m in range(1, 4):
                    fetch(i, m, me).wait_send()
                    passed(i, m, c).wait_send()
                    passed(i, m, 1 - c).wait_recv()

        return start, mid, finish

    out_shape = [jax.ShapeDtypeStruct((4,) + a.shape, a.dtype) for a in arrs]
    scratch = [pltpu.SemaphoreType.DMA((3 * n,))] * 4 + [pltpu.SemaphoreType.DMA((2 * n,))]
    return out_shape, scratch, phases


def _pair_halves_plan(arrs):
    n = len(arrs)

    def phases(refs):
        ins, outs = refs[:n], refs[n:2 * n]
        send, recv = refs[2 * n], refs[2 * n + 1]
        x, y, c = lax.axis_index("x"), lax.axis_index("y"), lax.axis_index("c")

        def copy(i, k, hc):
            return pltpu.make_async_remote_copy(
                src_ref=ins[i].at[k, hc], dst_ref=outs[i].at[k],
                send_sem=send.at[i * 4 + k], recv_sem=recv.at[i * 4 + k],
                device_id=(x, y, 1 - c), device_id_type=MESH)

        def start():
            for i in range(n):
                for k in range(4):
                    copy(i, k, 1 - c).start()

        def finish():
            for i in range(n):
                for k in range(4):
                    copy(i, k, 1 - c).wait()

        return start, (lambda: None), finish

    out_shape = [jax.ShapeDtypeStruct((4,) + a.shape[2:], a.dtype) for a in arrs]
    scratch = [pltpu.SemaphoreType.DMA((4 * n,)), pltpu.SemaphoreType.DMA((4 * n,))]
    return out_shape, scratch, phases


def _scatter_plan(arrs):
    n = len(arrs)

    def phases(refs):
        ins, outs = refs[:n], refs[n:2 * n]
        send, recv = refs[2 * n], refs[2 * n + 1]
        x, y, c = lax.axis_index("x"), lax.axis_index("y"), lax.axis_index("c")
        me = 2 * x + y

        def copy(i, m, slot):
            px, py = x ^ ((m >> 1) & 1), y ^ (m & 1)
            return pltpu.make_async_remote_copy(
                src_ref=ins[i].at[2 * px + py], dst_ref=outs[i].at[slot],
                send_sem=send.at[i * 3 + m - 1], recv_sem=recv.at[i * 3 + m - 1],
                device_id=(px, py, c), device_id_type=MESH)

        def start():
            for i in range(n):
                for m in range(1, 4):
                    copy(i, m, me).start()

        def finish():
            for i in range(n):
                for m in range(1, 4):
                    copy(i, m, me).wait_send()
                    copy(i, m, me ^ m).wait_recv()

        return start, (lambda: None), finish

    out_shape = [jax.ShapeDtypeStruct(a.shape, a.dtype) for a in arrs]
    scratch = [pltpu.SemaphoreType.DMA((3 * n,)), pltpu.SemaphoreType.DMA((3 * n,))]
    return out_shape, scratch, phases


def _pair_swap_plan(arrs):
    n = len(arrs)

    def phases(refs):
        ins, outs = refs[:n], refs[n:2 * n]
        send, recv = refs[2 * n], refs[2 * n + 1]
        x, y, c = lax.axis_index("x"), lax.axis_index("y"), lax.axis_index("c")

        def copy(i):
            return pltpu.make_async_remote_copy(
                src_ref=ins[i], dst_ref=outs[i], send_sem=send.at[i], recv_sem=recv.at[i],
                device_id=(x, y, 1 - c), device_id_type=MESH)

        def start():
            for i in range(n):
                copy(i).start()

        def finish():
            for i in range(n):
                copy(i).wait()

        return start, (lambda: None), finish

    out_shape = [jax.ShapeDtypeStruct(a.shape, a.dtype) for a in arrs]
    scratch = [pltpu.SemaphoreType.DMA((n,)), pltpu.SemaphoreType.DMA((n,))]
    return out_shape, scratch, phases


def _row_tile(R):
    for t in (256, 128, 64, 32, 16, 8):
        if R % t == 0:
            return t
    return R


def _sum_slots(name, parts):
    K, R, C = parts.shape
    tr = _row_tile(R)

    def kern(p_ref, o_ref):
        acc = p_ref[0].astype(F32)
        for k in range(1, K):
            acc = acc + p_ref[k].astype(F32)
        o_ref[...] = acc

    return _pcall(
        kern, name=name, grid=(R // tr,),
        in_specs=[pl.BlockSpec((K, tr, C), lambda i: (0, i, 0))],
        out_specs=pl.BlockSpec((tr, C), lambda i: (i, 0)),
        out_shape=jax.ShapeDtypeStruct((R, C), F32),
        compiler_params=_cparams(("parallel",)),
    )(parts)


def _sum_pair(name, core, mine, theirs):
    K, _, hr, C = mine.shape
    tr = _row_tile(hr)

    def kern(c_ref, a_ref, b_ref, o_ref):
        o_ref[0] = (a_ref[0, 0].astype(F32) + b_ref[0].astype(F32)).astype(BF16)

    return _pcall(
        kern, name=name, out_shape=jax.ShapeDtypeStruct((K, hr, C), BF16),
        grid_spec=pltpu.PrefetchScalarGridSpec(
            num_scalar_prefetch=1, grid=(K, hr // tr),
            in_specs=[pl.BlockSpec((1, 1, tr, C), lambda k, r, c_ref: (k, c_ref[0], r, 0)),
                      pl.BlockSpec((1, tr, C), lambda k, r, c_ref: (k, r, 0))],
            out_specs=pl.BlockSpec((1, tr, C), lambda k, r, c_ref: (k, r, 0))),
        compiler_params=_cparams(("parallel", "parallel")),
    )(core, mine, theirs)


def _sum_chips(name, chip, own, recv):
    K, hr, C = own.shape
    tr = _row_tile(hr)

    def kern(chip_ref, own_ref, *rest):
        r_refs, o_ref = rest[:K], rest[K]
        me = chip_ref[0]
        mine = own_ref[0].astype(F32)
        acc = None
        for k in range(K):
            t = jnp.where(me == k, mine, r_refs[k][0].astype(F32))
            acc = t if acc is None else acc + t
        o_ref[...] = acc

    def other(k):
        return pl.BlockSpec((1, tr, C), lambda r, s: (jnp.where(s[0] == k, (k + 1) % K, k), r, 0))

    return _pcall(
        kern, name=name, out_shape=jax.ShapeDtypeStruct((hr, C), F32),
        grid_spec=pltpu.PrefetchScalarGridSpec(
            num_scalar_prefetch=1, grid=(hr // tr,),
            in_specs=[pl.BlockSpec((1, tr, C), lambda r, s: (s[0], r, 0))] + [other(k) for k in range(K)],
            out_specs=pl.BlockSpec((tr, C), lambda r, s: (r, 0))),
        compiler_params=_cparams(("parallel",)),
    )(chip, own, *([recv] * K))


def _adam_update(w, m, v, g):
    c1 = 1.0 - ADAM_B1 ** ADAM_STEP
    c2 = 1.0 - ADAM_B2 ** ADAM_STEP
    mn = ADAM_B1 * m + (1.0 - ADAM_B1) * g
    vn = ADAM_B2 * v + (1.0 - ADAM_B2) * (g * g)
    return -ADAM_LR * ((mn / c1) / (jnp.sqrt(vn / c2) + ADAM_EPS) + ADAM_WD * w), mn, vn


def _adamw_halves(name, core, w, m, v, mine, theirs):
    R, C = w.shape
    hr = mine.shape[0]
    tr = _row_tile(hr)
    nbh = hr // tr

    def kern(c_ref, w_ref, m_ref, v_ref, a_ref, b_ref, go_ref, d_ref, mo_ref, vo_ref):
        g = jnp.where(pl.program_id(0) // nbh == c_ref[0], a_ref[...], b_ref[...])
        d, mn, vn = _adam_update(w_ref[...], m_ref[...], v_ref[...], g)
        go_ref[...] = g
        d_ref[...] = d
        mo_ref[...] = mn
        vo_ref[...] = vn

    spec = pl.BlockSpec((tr, C), lambda i, s: (i, 0))
    hspec = pl.BlockSpec((tr, C), lambda i, s: (i % nbh, 0))
    return _pcall(
        kern, name=name, out_shape=[jax.ShapeDtypeStruct((R, C), F32)] * 4,
        grid_spec=pltpu.PrefetchScalarGridSpec(
            num_scalar_prefetch=1, grid=(R // tr,),
            in_specs=[spec, spec, spec, hspec, hspec], out_specs=[spec] * 4),
        compiler_params=_cparams(("parallel",)),
    )(core, w, m, v, mine, theirs)


def _adamw_many(name, wmvg):
    n = len(wmvg[0])

    def kern(*refs):
        ins, outs = refs[:4 * n], refs[4 * n:]
        for j in range(n):
            g = ins[3 * n + j][...]
            d, mn, vn = _adam_update(ins[j][...], ins[n + j][...], ins[2 * n + j][...], g)
            for i, val in enumerate((g, d, mn, vn)):
                outs[i * n + j][...] = val

    vm = pl.BlockSpec(memory_space=pltpu.VMEM)
    flat = [a for group in wmvg for a in group]
    outs = _pcall(
        kern, name=name, in_specs=[vm] * (4 * n), out_specs=[vm] * (4 * n),
        out_shape=[jax.ShapeDtypeStruct(a.shape, F32) for _ in range(4) for a in wmvg[0]],
    )(*flat)
    return [outs[i * n:(i + 1) * n] for i in range(4)]


def _adamw(name, w, m, v, gparts, ride=None):
    R, C = w.shape
    K = gparts.shape[0]
    tr = _row_tile(R)
    nr = R // tr
    r_in, r_ispec, r_ospec, r_oshape, r_scratch, r_hook = _ride(ride, 4, 4)

    def kern(*refs):
        w_ref, m_ref, v_ref, g_ref = refs[:4]
        go_ref, d_ref, mo_ref, vo_ref = refs[4 + len(r_in):8 + len(r_in)]
        i = pl.program_id(0)
        before, after = r_hook(refs, i == 0, i == nr // 2, i == nr - 1)
        before()
        g = g_ref[0]
        for k in range(1, K):
            g = g + g_ref[k]
        d, mn, vn = _adam_update(w_ref[...], m_ref[...], v_ref[...], g)
        go_ref[...] = g
        d_ref[...] = d
        mo_ref[...] = mn
        vo_ref[...] = vn
        after()

    spec = pl.BlockSpec((tr, C), lambda i: (i, 0))
    return _pcall(
        kern, name=name, grid=(nr,),
        in_specs=[spec, spec, spec, pl.BlockSpec((K, tr, C), lambda i: (0, i, 0))] + r_ispec,
        out_specs=[spec] * 4 + r_ospec,
        out_shape=[jax.ShapeDtypeStruct((R, C), F32)] * 4 + r_oshape,
        scratch_shapes=r_scratch,
        compiler_params=_cparams(("arbitrary",) if ride else ("parallel",)),
    )(w, m, v, gparts, *r_in)


def _round_up(a, b):
    return (a + b - 1) // b * b


def kernel(x, c, w_ada, b_ada, norm1_g, w_in, b_forget, q_norm_g, k_norm_g, w_attn_proj, conv_w, conv_b, conv_ln_g, conv_ln_b, w_conv_proj, w_out, norm2_g, w_mlp1, w_mlp2, loss_target, m_w_ada, m_b_ada, m_norm1_g, m_w_in, m_b_forget, m_q_norm_g, m_k_norm_g, m_w_attn_proj, m_conv_w, m_conv_b, m_conv_ln_g, m_conv_ln_b, m_w_conv_proj, m_w_out, m_norm2_g, m_w_mlp1, m_w_mlp2, v_w_ada, v_b_ada, v_norm1_g, v_w_in, v_b_forget, v_q_norm_g, v_k_norm_g, v_w_attn_proj, v_conv_w, v_conv_b, v_conv_ln_g, v_conv_ln_b, v_w_conv_proj, v_w_out, v_norm2_g, v_w_mlp1, v_w_mlp2):
    S, D = x.shape[1], x.shape[2]
    NH, HD = b_forget.shape[-1], q_norm_g.shape[-1]
    TAPS = conv_w.shape[1]
    DIN_S = w_in.shape[-1]
    DIN = 4 * DIN_S
    DFF_S = w_mlp1.shape[-1]
    DFF = 4 * DFF_S
    ADA_S = w_ada.shape[-1]
    DS = w_attn_proj.shape[1]
    CS = conv_w.shape[-1]
    assert NH * HD == D and DIN == 7 * D + NH and TAPS - 1 <= HALO and D % LANES == 0 and 2 * HD == LANES
    NP = _round_up(7 * D + LANES, 512)
    FW = NP - 7 * D
    assert (7 * D) % FW == 0
    TQ = min(512, S)
    NQ = S // TQ
    FCOL = 7 * D // LANES

    xi, yi, ci = lax.axis_index("x"), lax.axis_index("y"), lax.axis_index("c")
    chip = 2 * xi + yi
    dev = 4 * xi + 2 * yi + ci

    x2 = x.reshape(S, D)
    tgt = loss_target.reshape(S, D)

    lane_head = jnp.arange(D, dtype=jnp.int32) // HD
    grp = (lane_head[:, None] == jnp.arange(LANES, dtype=jnp.int32)[None, :]).astype(BF16)
    grp_t = grp.T
    sel = ((jnp.arange(D, dtype=jnp.int32)[:, None] == HD * jnp.arange(LANES, dtype=jnp.int32)[None, :])
           .astype(BF16))
    ch = min(256, S)
    ii = jnp.arange(ch, dtype=jnp.int32)
    tri = (ii[None, :] <= ii[:, None]).astype(BF16)
    tri_u = tri.T
    gq_t = jnp.tile(q_norm_g.reshape(1, HD), (1, NH))
    gk_t = jnp.tile(k_norm_g.reshape(1, HD), (1, NH))
    bf_pad = jnp.pad(b_forget.reshape(1, NH), ((0, 0), (0, LANES - NH)))

    c_all, cw_all = _ag_small(
        "ag_c_convw", [c.reshape(1, D), jnp.pad(conv_w.reshape(TAPS, CS), ((0, HALO - TAPS), (0, 0)))])
    c_all = c_all.reshape(8, D)
    b_part = lax.dynamic_slice(b_ada.reshape(1, -1), (0, chip * ADA_S), (1, ADA_S))
    mod_part = _ada_fwd(c_all, w_ada.reshape(D, ADA_S), b_part)
    (mod_all,) = _ag_small("ag_mod", [mod_part])
    mod_full = jnp.concatenate([mod_all[0], mod_all[2], mod_all[4], mod_all[6]], axis=1)
    mod = lax.dynamic_slice(mod_full, (dev, 0), (1, 6 * D))
    sh1, sc1, g1, sh2, sc2, g2 = [mod[:, i * D:(i + 1) * D] for i in range(6)]

    shards = [w_in.reshape(D, DIN_S), w_attn_proj.reshape(DS, D), w_conv_proj.reshape(DS, D),
              w_out.reshape(DS, D), w_mlp1.reshape(D, DFF_S), w_mlp2.reshape(DFF_S, D)]
    shards = [s.astype(BF16) for s in shards]
    (gw_in,) = _exchange("ag_w_in", shards[:1], _gather_plan)
    w_conv = jnp.concatenate([cw_all[0], cw_all[2], cw_all[4], cw_all[6]], axis=1)

    SEGS = [(0, 2 * D, 0), (3 * D + NH, DIN, 2 * D), (2 * D, 3 * D + NH, 6 * D)]

    def pieces(a, b):
        out = []
        for k in range(4):
            lo, hi = max(a, k * DIN_S), min(b, (k + 1) * DIN_S)
            if lo < hi:
                out.append(gw_in[k][:, lo - k * DIN_S:hi - k * DIN_S])
        return out

    w_in_p = jnp.concatenate([p for (a, b, _) in SEGS for p in pieces(a, b)]
                             + [jnp.zeros((D, NP - 7 * D - NH), BF16)], axis=1)

    n1g = norm1_g.reshape(1, D)
    n2g = norm2_g.reshape(1, D)
    h = _norm_mod("norm_mod1", x2, n1g, sc1, sh1, S, D)
    proj = _mm("mm_in", h, w_in_p, "nn", [F32], tn=1536)
    qs, kn, vb = _qk_prep(proj, 6, gq_t, gk_t, grp, grp_t, S, D, HD)
    f_cum = _fgate_fwd(proj, FCOL, bf_pad, tri, S)
    fk_c = f_cum[:, :NH]
    fk_r = fk_c.T.reshape(NH, NQ, 1, TQ)
    o, o32, lse_b, gw_ap, gw_cp, gw_out, gw_m1, gw_m2 = _flash_fwd(
        qs, kn, vb, fk_r, S, D, HD, TQ, ride=(_gather_plan, shards[1:]))
    w_ap = gw_ap.reshape(D, D)
    w_cp = gw_cp.reshape(D, D)
    w_o = gw_out.reshape(D, D)
    w_m1 = jnp.transpose(gw_m1, (1, 0, 2)).reshape(D, DFF)
    w_m2 = gw_m2.reshape(DFF, D)
    br_a = _mm("mm_attn_proj", o, w_ap, "nn", [F32])
    cb, clg, clb = conv_b.reshape(1, D), conv_ln_g.reshape(1, D), conv_ln_b.reshape(1, D)
    u1, u3 = _conv_fwd(proj, 2, 3, w_conv, cb, clg, clb, S, D, TAPS, 256)
    br_b = _mm("mm_conv_proj", u3, w_cp, "nn", [F32])
    merged = _gate_merge(proj, 4, 5, br_a, br_b, S, D)
    mo = _mm("mm_out", merged, w_o, "nn", [F32])
    x1, h2 = _resid_norm2(x2, mo, g1, n2g, sc2, sh2, S, D)

    def relu2(r):
        rp = jnp.maximum(r, 0.0)
        return (rp * rp,)
    z = _mm("mm_mlp1", h2, w_m1, "nn", [BF16], epi=relu2)
    ml = _mm("mm_mlp2", z, w_m2, "nn", [F32])
    dy, dml, sq, dg2 = _loss_dy(x1, ml, tgt, g2, S, D)
    loss_part = jnp.full((1, LANES), 0.5 * jnp.sum(sq) / D, F32)

    da = _mm("mm_dz", dml, w_m2, "nt", [BF16], epi=lambda r, zz: (r * 2.0 * jnp.sqrt(zz.astype(F32)),),
             extras=(z,))
    dw_m2 = _mm("mm_dw_mlp2", z, dml, "tn", [BF16])
    dw_m1 = _mm("mm_dw_mlp1", h2, da, "tn", [BF16])
    dh2 = _mm("mm_dh2", da, w_m1, "nt", [F32])
    dx1, dmo, dsh2, dsc2, dn2g, dg1 = _norm_bwd("norm2_bwd", x1, dh2, dy, n2g, sc2, S, D, extra=(mo, g1))
    dmerged = _mm("mm_dmerged", dmo, w_o, "nt", [F32])
    dw_o = _mm("mm_dw_out", merged, dmo, "tn", [BF16])
    dproj, dba, dbb = _gate_bwd(dmerged, proj, 4, 5, br_a, br_b, S, D, into=(lax.empty((S, NP), BF16), 2))
    do = _mm("mm_do", dba, w_ap, "nt", [BF16])
    dw_ap = _mm("mm_dw_attn_proj", o, dba, "tn", [BF16])
    du3 = _mm("mm_du3", dbb, w_cp, "nt", [F32])
    dw_cp = _mm("mm_dw_conv_proj", u3, dbb, "tn", [BF16])
    core = ci.astype(jnp.int32).reshape(1)
    chip1 = chip.astype(jnp.int32).reshape(1)
    halves = lambda p: p.astype(BF16).reshape(4, 2, p.shape[1] // 2, p.shape[2])
    names = ["w_in", "w_attn_proj", "w_conv_proj", "w_out", "w_mlp1", "w_mlp2"]
    parts = [halves(p) for p in (dw_ap.reshape(4, DS, D), dw_cp.reshape(4, DS, D), dw_o.reshape(4, DS, D),
                                 jnp.transpose(dw_m1.reshape(D, 4, DFF_S), (1, 0, 2)), dw_m2.reshape(4, DFF_S, D))]
    dproj, dcw, dcb, dclg, dclb, *theirs = _conv_bwd(
        du3, u1, proj, 2, 3, w_conv, clg, clb, S, D, TAPS, 256, into=(dproj, 1), ride=(_pair_halves_plan, parts))
    chip_parts =[_sum_pair("sum_pair_" + nm, core, p, t) for nm, p, t in zip(names[1:], parts, theirs)]

    delta_c, lse_c = _delta_prep(do, o32, lse_b, grp, sel, S, D)
    to_rows = lambda t: t[:, :NH].T.reshape(NH, NQ, 1, TQ)
    dkn, dproj, dqs, dfq_r, dfk_b, *recvd = _flash_bwd(
        qs, kn, vb, do, f_cum, to_rows(lse_c), to_rows(delta_c), S, D, HD, TQ,
        dv_into=(dproj, 6 * D // LANES), ride=(_scatter_plan, chip_parts))
    dproj, sq_q, sq_k = _qk_bwd(proj, dqs, dkn, gq_t, gk_t, grp, grp_t, S, D, HD, into=(dproj, 0))
    to_cols = lambda r: jnp.pad(r.reshape(NH, S).T, ((0, 0), (0, LANES - NH)))
    dfq_pad = to_cols(dfq_r)
    dfk_pad = jnp.pad(dfk_b[:, ::HD], ((0, 0), (0, LANES - NH)))
    dproj, dbf = _fgate_bwd(dfk_pad, dfq_pad, proj, FCOL, bf_pad, tri_u, NH, S, FW, into=(dproj, 7 * D // FW))
    dw_in_p = _mm("mm_dw_in", h, dproj, "tn", [BF16])
    def shard_cols(k):
        out = []
        for (a, b, start) in sorted(SEGS):
            lo, hi = max(a, k * DIN_S), min(b, (k + 1) * DIN_S)
            if lo < hi:
                out.append(dw_in_p[:, start + lo - a:start + hi - a])
        return jnp.concatenate(out, axis=1)

    part_in = halves(jnp.stack([shard_cols(k) for k in range(4)]))
    (their_in,) = _exchange("rs_pair_w_in", [part_in], _pair_halves_plan)
    chip_in = _sum_pair("sum_pair_w_in", core, part_in, their_in)
    dh, recv_in = _mm("mm_dh", dproj, w_in_p, "nt", [F32], ride=(_scatter_plan, [chip_in]))
    gx, dsh1, dsc1, dn1g = _norm_bwd("norm1_bwd", x2, dh, dx1, n1g, sc1, S, D)

    packed = jnp.concatenate([dsh1, dsc1, dg1, dsh2, dsc2, dg2, dn1g, dcb, dclg, dclb, dn2g,
                              sq_q, sq_k, dbf, loss_part], axis=1)
    small_all, dcw_all = _ag_small("ag_small_grads", [packed, dcw])
    small = _sum_slots("sum_small", small_all.reshape(8, 1, -1)).reshape(1, -1)
    dmod_sum = small[:, :6 * D]
    seg = lambda k: small[:, (6 + k) * D:(7 + k) * D]
    g_n1g, g_cb, g_clg, g_clb, g_n2g = seg(0), seg(1), seg(2), seg(3), seg(4)
    g_qn = _sum_slots("sum_qn", seg(5).reshape(NH, 1, HD))
    g_kn = _sum_slots("sum_kn", seg(6).reshape(NH, 1, HD))
    g_bf = small[:, 13 * D:13 * D + NH]
    loss = small[0, 13 * D + LANES]
    dcw_mine = lax.dynamic_slice(dcw_all[:, :TAPS, :], (0, 0, chip * CS), (8, TAPS, CS))

    dmod_all = small_all.reshape(8, -1)[:, :6 * D]
    dmod_cols = lax.dynamic_slice(dmod_all, (0, chip * ADA_S), (8, ADA_S))
    c_t_pad = jnp.pad(c_all.T, ((0, 0), (0, LANES - 8)))
    g_wada = _ada_wgrad(c_t_pad, jnp.pad(dmod_cols, ((0, LANES - 8), (0, 0))))

    sums =[_sum_chips("sum_" + nm, chip1, p, r)
            for nm, p, r in zip(names, [chip_in] + chip_parts, [recv_in] + list(recvd))]

    res = {}
    outs = _adamw("adamw_w_ada", w_ada.reshape(D, ADA_S), m_w_ada.reshape(D, ADA_S),
                  v_w_ada.reshape(D, ADA_S), g_wada.reshape(1, D, ADA_S), ride=(_pair_swap_plan, sums))
    res["w_ada"] = [t.reshape(w_ada.shape) for t in outs[:4]]
    big = {nm: (a, b) for nm, a, b in zip(names, sums, outs[4:])}
    big_w = {"w_in": (w_in, m_w_in, v_w_in), "w_attn_proj": (w_attn_proj, m_w_attn_proj, v_w_attn_proj),
             "w_conv_proj": (w_conv_proj, m_w_conv_proj, v_w_conv_proj), "w_out": (w_out, m_w_out, v_w_out),
             "w_mlp1": (w_mlp1, m_w_mlp1, v_w_mlp1), "w_mlp2": (w_mlp2, m_w_mlp2, v_w_mlp2)}
    for nm in names:
        shp = big_w[nm][0].shape
        outs = _adamw_halves("adamw_" + nm, core, *[t.reshape(shp[1], shp[2]) for t in big_w[nm]], *big[nm])
        res[nm] = [t.reshape(shp) for t in outs]
    outs = _adamw("adamw_conv_w", conv_w.reshape(TAPS, CS), m_conv_w.reshape(TAPS, CS),
                  v_conv_w.reshape(TAPS, CS), dcw_mine)
    res["conv_w"] = [t.reshape(conv_w.shape) for t in outs]

    small_w = [("b_ada", b_ada, m_b_ada, v_b_ada, dmod_sum), ("norm1_g", norm1_g, m_norm1_g, v_norm1_g, g_n1g),
               ("b_forget", b_forget, m_b_forget, v_b_forget, g_bf),
               ("q_norm_g", q_norm_g, m_q_norm_g, v_q_norm_g, g_qn),
               ("k_norm_g", k_norm_g, m_k_norm_g, v_k_norm_g, g_kn),
               ("conv_b", conv_b, m_conv_b, v_conv_b, g_cb), ("conv_ln_g", conv_ln_g, m_conv_ln_g, v_conv_ln_g, g_clg),
               ("conv_ln_b", conv_ln_b, m_conv_ln_b, v_conv_ln_b, g_clb),
               ("norm2_g", norm2_g, m_norm2_g, v_norm2_g, g_n2g)]
    outs = _adamw_many("adamw_small", [[t[i].reshape(1, -1) for t in small_w] for i in (1, 2, 3, 4)])
    for j, (nm, w_, _, _, _) in enumerate(small_w):
        res[nm] = [outs[i][j].reshape(w_.shape) for i in range(4)]

    order = ["w_ada", "b_ada", "norm1_g", "w_in", "b_forget", "q_norm_g", "k_norm_g", "w_attn_proj", "conv_w",
             "conv_b", "conv_ln_g", "conv_ln_b", "w_conv_proj", "w_out", "norm2_g", "w_mlp1", "w_mlp2"]
    return (loss, gx.reshape(x.shape), *[res[n][0] for n in order], *[res[n][1] for n in order],
            *[res[n][2] for n in order], *[res[n][3] for n in order])
```

```python
import functools

import jax
import jax.numpy as jnp
from jax import lax
from jax.experimental import pallas as pl
from jax.experimental.pallas import tpu as pltpu

F32 = jnp.float32
BF16 = jnp.bfloat16
MESH = pl.DeviceIdType.MESH
ANY = pl.BlockSpec(memory_space=pl.ANY)

NORM_EPS = 1e-6
ADAM_LR = 0.001
ADAM_B1 = 0.9
ADAM_B2 = 0.999
ADAM_EPS = 1e-08
ADAM_WD = 0.01
ADAM_STEP = 10
LANES = 128
SUBLANES = 8
HALO = 32
CONV_ROWS = 32
CONV_TAPS = 4
NEG = -1e30
VMEM_LIMIT = 56 * 1024 * 1024


def _pcall(body, **kw):
    return pl.pallas_call(body, **kw)


def _cparams(sem=None):
    if sem is None:
        return pltpu.CompilerParams(vmem_limit_bytes=VMEM_LIMIT)
    return pltpu.CompilerParams(dimension_semantics=sem, vmem_limit_bytes=VMEM_LIMIT)


def _sig(x):
    return 1.0 / (1.0 + jnp.exp(-x))


def _split3(x):
    x1 = x.astype(BF16)
    r = x - x1.astype(F32)
    x2 = r.astype(BF16)
    x3 = (r - x2.astype(F32)).astype(BF16)
    return x1, x2, x3


def _dot_rs(x, e, terms=3):
    out = None
    for t in _split3(x)[:terms]:
        d = jnp.dot(t, e, preferred_element_type=F32)
        out = d if out is None else out + d
    return out


def _dot_ls(e, x):
    out = None
    for t in _split3(x):
        d = jnp.dot(e, t, preferred_element_type=F32)
        out = d if out is None else out + d
    return out


def _tile(n, want):
    if n <= want:
        return n
    t = want - want % LANES
    while n % t:
        t -= LANES
    assert t > 0, (n, want)
    return t


_DIMS = {"nn": ((1,), (0,)), "nt": ((1,), (1,)), "tn": ((0,), (0,))}


def _mm(name, a, b, mode, out_dtypes, epi=None, extras=(), tm=1024, tn=1024, tk=4096, ride=None):
    if mode == "nn":
        (M, K), (_, N) = a.shape, b.shape
    elif mode == "nt":
        (M, K), (N, _) = a.shape, b.shape
    else:
        (K, M), (_, N) = a.shape, b.shape
    tm, tn, tk = _tile(M, tm), _tile(N, tn), _tile(K, tk)
    nm, nn, nk = M // tm, N // tn, K // tk
    ne, no = len(extras), len(out_dtypes)
    dims = (_DIMS[mode], ((), ()))
    r_in, r_ispec, r_ospec, r_oshape, r_scratch, r_hook = _ride(ride, 2 + ne, no)

    def kern(*refs):
        a_ref, b_ref = refs[0], refs[1]
        e_refs = refs[2:2 + ne]
        o_refs = refs[2 + ne + len(r_in):2 + ne + len(r_in) + no]
        i, j, k = pl.program_id(0), pl.program_id(1), pl.program_id(2)
        before, after = r_hook(refs, (i == 0) & (j == 0) & (k == 0), (i == nm // 2) & (j == 0) & (k == 0),
                               (i == nm - 1) & (j == nn - 1) & (k == nk - 1))
        before()
        d = lax.dot_general(a_ref[...], b_ref[...], dims, preferred_element_type=F32)

        def finish(r):
            outs = (r,) if epi is None else epi(r, *[e[...] for e in e_refs])
            for o_ref, o in zip(o_refs, outs):
                o_ref[...] = o.astype(o_ref.dtype)

        if nk == 1:
            finish(d)
        else:
            acc = refs[-1]

            @pl.when(k == 0)
            def _():
                acc[...] = d

            @pl.when((k > 0) & (k < nk - 1))
            def _():
                acc[...] += d

            @pl.when(k == nk - 1)
            def _():
                finish(acc[...] + d)
        after()

    if mode == "tn":
        a_spec = pl.BlockSpec((tk, tm), lambda i, j, k: (k, i))
    else:
        a_spec = pl.BlockSpec((tm, tk), lambda i, j, k: (i, k))
    if mode == "nt":
        b_spec = pl.BlockSpec((tn, tk), lambda i, j, k: (j, k))
    else:
        b_spec = pl.BlockSpec((tk, tn), lambda i, j, k: (k, j))
    mn_spec = pl.BlockSpec((tm, tn), lambda i, j, k: (i, j))
    outs = _pcall(
        kern, name=name, grid=(nm, nn, nk),
        in_specs=[a_spec, b_spec] + [mn_spec] * ne + r_ispec,
        out_specs=[mn_spec] * no + r_ospec,
        out_shape=[jax.ShapeDtypeStruct((M, N), dt) for dt in out_dtypes] + r_oshape,
        scratch_shapes=r_scratch + ([pltpu.VMEM((tm, tn), F32)] if nk > 1 else []),
        compiler_params=_cparams(("arbitrary",) * 3 if ride else ("parallel", "parallel", "arbitrary")),
    )(a, b, *extras, *r_in)
    return outs[0] if len(outs) == 1 else outs


def _rowcall(name, body, S, ts, row_ins, vec_ins, row_outs, vec_outs, into=None):
    ts = min(ts, S)
    nri, nvi, nro, nvo = len(row_ins), len(vec_ins), len(row_outs), len(vec_outs)
    na = 0 if into is None else 1

    def kern(*refs):
        ins = refs[:nri + nvi]
        outs = refs[nri + nvi + na:]
        if nvo:
            @pl.when(pl.program_id(0) == 0)
            def _():
                for r in outs[nro:]:
                    r[...] = jnp.zeros(r.shape, r.dtype)
        body(*ins, *outs)

    in_specs = [pl.BlockSpec((ts, w), functools.partial(lambda i, cb: (i, cb), cb=cb))
                for (_, w, cb) in row_ins]
    in_specs += [pl.BlockSpec(v.shape, lambda i: (0, 0)) for v in vec_ins]
    out_specs = [pl.BlockSpec((ts, w), lambda i: (i, 0)) for (w, _) in row_outs]
    out_specs += [pl.BlockSpec((r, w), lambda i: (0, 0)) for (r, w) in vec_outs]
    out_shape = [jax.ShapeDtypeStruct((S, w), dt) for (w, dt) in row_outs]
    out_shape += [jax.ShapeDtypeStruct((r, w), F32) for (r, w) in vec_outs]
    extra, aliases = [], {}
    if into is not None:
        buf, cb = into
        assert buf.dtype == row_outs[0][1] and buf.shape[0] == S
        in_specs.append(ANY)
        out_specs[0] = pl.BlockSpec((ts, row_outs[0][0]), lambda i: (i, cb))
        out_shape[0] = jax.ShapeDtypeStruct(buf.shape, buf.dtype)
        extra, aliases = [buf], {nri + nvi: 0}
    return _pcall(
        kern, name=name, grid=(S // ts,), in_specs=in_specs, out_specs=out_specs,
        out_shape=out_shape, input_output_aliases=aliases,
        compiler_params=_cparams(("arbitrary",) if nvo else ("parallel",)),
    )(*[a for (a, _, _) in row_ins], *vec_ins, *extra)


def _csum(x):
    return jnp.sum(x, axis=0, keepdims=True)


def _norm_mod(name, x, g, sc, sh, S, D):
    def body(x_ref, g_ref, sc_ref, sh_ref, h_ref):
        xv = x_ref[...]
        r = lax.rsqrt(jnp.mean(xv * xv, axis=-1, keepdims=True) + NORM_EPS)
        h_ref[...] = ((xv * r * g_ref[...]) * (1.0 + sc_ref[...]) + sh_ref[...]).astype(BF16)
    return _rowcall(name, body, S, 512, [(x, D, 0)], [g, sc, sh], [(D, BF16)], [])[0]


def _head_rstd(v, grp, grp_t, hd):
    ss = _dot_rs(v * v, grp, 2) * (1.0 / hd)
    r = lax.rsqrt(ss + NORM_EPS)
    return _dot_rs(r, grp_t, 2)


def _qk_prep(proj, vcol, gq, gk, grp, grp_t, S, D, hd):
    scale = hd ** -0.5

    def body(q_ref, k_ref, v_ref, gq_ref, gk_ref, g_ref, gt_ref, qs_ref, kn_ref, vb_ref):
        q = q_ref[...]
        k = k_ref[...]
        rq = _head_rstd(q, g_ref[...], gt_ref[...], hd)
        rk = _head_rstd(k, g_ref[...], gt_ref[...], hd)
        qs_ref[...] = ((q * rq * gq_ref[...]).astype(BF16).astype(F32) * scale).astype(BF16)
        kn_ref[...] = (k * rk * gk_ref[...]).astype(BF16)
        vb_ref[...] = v_ref[...].astype(BF16)

    return _rowcall("qk_prep", body, S, 256, [(proj, D, 0), (proj, D, 1), (proj, D, vcol)],
                    [gq, gk, grp, grp_t], [(D, BF16)] * 3, [])


def _fgate_fwd(proj, fcol, bf_pad, tri, S):
    ch = tri.shape[0]

    def body(f_ref, b_ref, tri_ref, out_ref):
        carry = jnp.zeros((1, LANES), F32)
        for c in range(S // ch):
            z = f_ref[c * ch:(c + 1) * ch, :] + b_ref[...]
            lf = jnp.minimum(z, 0.0) - jnp.log(1.0 + jnp.exp(-jnp.abs(z)))
            out_ref[c * ch:(c + 1) * ch, :] = _dot_ls(tri_ref[...], lf) + carry
            carry = carry + _csum(lf)

    return _rowcall("fgate_fwd", body, S, S, [(proj, LANES, fcol)], [bf_pad, tri],
                    [(LANES, F32)], [])[0]


def _fgate_bwd(dfk, dfq, proj, fcol, bf_pad, tri_u, nh, S, fw, into):
    ch = tri_u.shape[0]

    def body(d_ref, dq_ref, f_ref, b_ref, tri_ref, df_ref, db_ref):
        if fw > LANES:
            df_ref[:, LANES:fw] = jnp.zeros((S, fw - LANES), BF16)
        lane = lax.broadcasted_iota(jnp.int32, (ch, LANES), 1)
        carry = jnp.zeros((1, LANES), F32)
        tot = jnp.zeros((1, LANES), F32)
        for c in reversed(range(S // ch)):
            d = d_ref[c * ch:(c + 1) * ch, :] + dq_ref[c * ch:(c + 1) * ch, :]
            rc = _dot_ls(tri_ref[...], d) + carry
            carry = carry + _csum(d)
            z = f_ref[c * ch:(c + 1) * ch, :] + b_ref[...]
            df = jnp.where(lane < nh, rc * _sig(-z), 0.0)
            df_ref[c * ch:(c + 1) * ch, 0:LANES] = df.astype(BF16)
            tot = tot + _csum(df)
        db_ref[...] += tot

    return _rowcall("fgate_bwd", body, S, S, [(dfk, LANES, 0), (dfq, LANES, 0), (proj, LANES, fcol)],
                    [bf_pad, tri_u], [(fw, BF16)], [(1, LANES)], into=into)


def _keep(v, mask):
    return jnp.where(mask, v.astype(F32), 0.0).astype(BF16)


def _lane_col(blk, lane, at):
    return jnp.sum(jnp.where(lane == at, blk, 0.0), axis=-1, keepdims=True)


def _flash_fwd(qs, kn, vb, fk_r, S, D, hd, tq, ride=None):
    hp, nq = D // LANES, S // tq
    r_in, r_ispec, r_ospec, r_oshape, r_scratch, r_hook = _ride(ride, 4, 3)

    def kern(*refs):
        q_ref, k_ref, v_ref, fk_ref = refs[:4]
        o_ref, o32_ref, lse_ref = refs[4 + len(r_in):7 + len(r_in)]
        hi, qi = pl.program_id(0), pl.program_id(1)
        before, after = r_hook(refs, (hi == 0) & (qi == 0), (hi == hp // 2) & (qi == 0),
                               (hi == hp - 1) & (qi == nq - 1))
        before()
        lane = lax.broadcasted_iota(jnp.int32, (tq, LANES), 1)
        row = lax.broadcasted_iota(jnp.int32, (tq, tq), 0)
        col = lax.broadcasted_iota(jnp.int32, (tq, tq), 1)
        hms = [(lane >= j * hd) & (lane < (j + 1) * hd) for j in range(2)]
        q = q_ref[...]
        qms = [_keep(q, hm) for hm in hms]

        s_a, s_b = refs[-2], refs[-1]

        def put(s_ref, ki):
            off = pl.multiple_of(ki * tq, tq)
            k = k_ref[pl.ds(off, tq), :]
            for j in range(2):
                s_ref[j] = lax.dot_general(qms[j], k, (((1,), (1,)), ((), ())), preferred_element_type=F32)

        def update(ki, s_ref, state, masked):
            off = pl.multiple_of(ki * tq, tq)
            v = v_ref[pl.ds(off, tq), :].astype(F32)
            new = []
            for j in range(2):
                m_old, acc = state[j]
                s = s_ref[j] - fk_ref[j, ki]
                if masked:
                    s = jnp.where(col <= row, s, NEG)
                m_new = jnp.maximum(m_old, jnp.max(s, axis=-1, keepdims=True))
                alpha = jnp.exp(m_old - m_new)
                p = jnp.exp(s - m_new)
                v1 = jnp.where(hms[j], v, 1.0).astype(BF16)
                acc = alpha * acc + jnp.dot(p.astype(BF16), v1, preferred_element_type=F32)
                new.append((m_new, acc))
            return tuple(new)

        def pair(p, state):
            put(s_b, 2 * p + 1)
            state = update(2 * p, s_a, state, False)
            put(s_a, 2 * p + 2)
            return update(2 * p + 1, s_b, state, False)

        def odd_tail(state):
            put(s_b, qi)
            return update(qi, s_b, update(qi - 1, s_a, state, False), True)

        init = tuple((jnp.full((tq, 1), NEG, F32), jnp.zeros((tq, LANES), F32)) for _ in range(2))
        put(s_a, 0)
        state = lax.fori_loop(0, qi // 2, pair, init)
        (m0, a0), (m1, a1) = lax.cond(qi % 2 == 1, odd_tail, lambda st: update(qi, s_a, st, True), state)
        l0, l1 = pltpu.roll(a0, hd, 1), pltpu.roll(a1, hd, 1)
        first = lane < hd
        ov = jnp.where(first, a0 / l0, a1 / l1)
        o_ref[...] = ov.astype(BF16)
        o32_ref[...] = ov
        lse_ref[...] = jnp.where(first, m0 + jnp.log(l0), m1 + jnp.log(l1))
        after()

    qspec = pl.BlockSpec((tq, LANES), lambda h, i: (i, h))
    fullspec = pl.BlockSpec((S, LANES), lambda h, i: (0, h))
    return _pcall(
        kern, name="flash_fwd", grid=(hp, nq),
        in_specs=[qspec, fullspec, fullspec,
                  pl.BlockSpec((2, nq, 1, tq), lambda h, i: (h, 0, 0, 0))] + r_ispec,
        out_specs=[qspec, qspec, qspec] + r_ospec,
        out_shape=[jax.ShapeDtypeStruct((S, D), BF16), jax.ShapeDtypeStruct((S, D), F32),
                   jax.ShapeDtypeStruct((S, D), F32)] + r_oshape,
        scratch_shapes=r_scratch + [pltpu.VMEM((2, tq, tq), F32)] * 2,
        compiler_params=_cparams(("arbitrary", "arbitrary")),
    )(qs, kn, vb, fk_r, *r_in)


def _flash_bwd(qs, kn, vb, do, fk_b, lse_r, delta_r, S, D, hd, tq, dv_into, ride=None):
    hp, nq = D // LANES, S // tq
    dbuf_hbm, dv_col = dv_into
    r_in, r_ispec, r_ospec, r_oshape, r_scratch, r_hook = _ride(ride, 8, 5)

    def kern(*refs):
        q_ref, do_ref, k_ref, v_ref, fk_ref, lse_ref, dl_ref = refs[:7]
        dk_ref, dv_ref, dq_ref, dfq_ref, dfk_ref = refs[8 + len(r_in):13 + len(r_in)]
        hi, ki = pl.program_id(0), pl.program_id(1)
        before, after = r_hook(refs, (hi == 0) & (ki == 0), (hi == hp // 2) & (ki == 0),
                               (hi == hp - 1) & (ki == nq - 1))
        before()
        lane = lax.broadcasted_iota(jnp.int32, (tq, LANES), 1)
        row = lax.broadcasted_iota(jnp.int32, (tq, tq), 0)
        col = lax.broadcasted_iota(jnp.int32, (tq, tq), 1)
        hms = [(lane >= j * hd) & (lane < (j + 1) * hd) for j in range(2)]
        k = k_ref[...]
        v = v_ref[...]
        fkb = fk_ref[...]
        kms = [_keep(k, hm) for hm in hms]
        vms = [_keep(v, hm) for hm in hms]
        fks = [_lane_col(fkb, lane, 2 * hi + j) for j in range(2)]

        @pl.when(ki == 0)
        def _():
            dfq_ref[...] = jnp.zeros(dfq_ref.shape, F32)
            dq_ref[...] = jnp.zeros(dq_ref.shape, F32)

        def step(qi, acc, masked):
            dk, dv, dfs = acc
            off = pl.multiple_of(qi * tq, tq)
            q = q_ref[pl.ds(off, tq), :]
            g = do_ref[pl.ds(off, tq), :]
            dq = None
            new_dfs = []
            for j in range(2):
                qm = _keep(q, hms[j])
                gm = _keep(g, hms[j])
                st = lax.dot_general(kms[j], q, (((1,), (1,)), ((), ())), preferred_element_type=F32)
                st = st - fks[j]
                if masked:
                    st = jnp.where(row <= col, st, NEG)
                pt = jnp.exp(st - lse_ref[j, qi])
                dv = dv + jnp.dot(pt.astype(BF16), gm, preferred_element_type=F32)
                dpt = lax.dot_general(vms[j], g, (((1,), (1,)), ((), ())), preferred_element_type=F32)
                dst = pt * (dpt - dl_ref[j, qi])
                dsb = dst.astype(BF16)
                dk = dk + jnp.dot(dsb, qm, preferred_element_type=F32)
                t = lax.dot_general(dsb, kms[j], (((0,), (0,)), ((), ())), preferred_element_type=F32)
                dq = t if dq is None else dq + t
                dfq_ref[j, qi] += jnp.sum(dst, axis=0, keepdims=True)
                new_dfs.append(dfs[j] - jnp.sum(dst, axis=1, keepdims=True))
            dq_ref[pl.ds(off, tq), :] += dq
            return dk, dv, tuple(new_dfs)

        zero = jnp.zeros((tq, LANES), F32)
        zcol = jnp.zeros((tq, 1), F32)
        acc = step(ki, (zero, zero, (zcol, zcol)), True)
        dk, dv, dfs = lax.fori_loop(ki + 1, nq, lambda qi, a: step(qi, a, False), acc)
        dk_ref[...] = dk.astype(BF16)
        dv_ref[...] = dv.astype(BF16)
        dfk_ref[...] = jnp.where(lane < hd, dfs[0], dfs[1])
        after()

    kspec = pl.BlockSpec((tq, LANES), lambda h, i: (i, h))
    fullspec = pl.BlockSpec((S, LANES), lambda h, i: (0, h))
    rowspec = pl.BlockSpec((2, nq, 1, tq), lambda h, i: (h, 0, 0, 0))
    return _pcall(
        kern, name="flash_bwd", grid=(hp, nq),
        in_specs=[fullspec, fullspec, kspec, kspec, pl.BlockSpec((tq, LANES), lambda h, i: (i, 0)),
                  rowspec, rowspec, ANY] + r_ispec,
        out_specs=[kspec, pl.BlockSpec((tq, LANES), lambda h, i: (i, h + dv_col)), fullspec, rowspec, kspec]
        + r_ospec,
        out_shape=[jax.ShapeDtypeStruct((S, D), BF16), jax.ShapeDtypeStruct(dbuf_hbm.shape, BF16),
                   jax.ShapeDtypeStruct((S, D), F32), jax.ShapeDtypeStruct((2 * hp, nq, 1, tq), F32),
                   jax.ShapeDtypeStruct((S, D), F32)] + r_oshape,
        scratch_shapes=r_scratch, input_output_aliases={7: 1},
        compiler_params=_cparams(("arbitrary", "arbitrary")),
    )(qs, do, kn, vb, fk_b, lse_r, delta_r, dbuf_hbm, *r_in)


def _delta_prep(do, o, lse_b, grp, sel, S, D):
    def body(g_ref, o_ref, l_ref, e_ref, s_ref, dl_ref, lse_ref):
        prod = g_ref[...].astype(F32) * o_ref[...]
        dl_ref[...] = _dot_rs(prod, e_ref[...], 2)
        lse_ref[...] = _dot_rs(l_ref[...], s_ref[...])
    return _rowcall("delta_prep", body, S, 256, [(do, D, 0), (o, D, 0), (lse_b, D, 0)], [grp, sel],
                    [(LANES, F32), (LANES, F32)], [])


def _qk_bwd(proj, dqs, dkn, gq, gk, grp, grp_t, S, D, hd, into):
    scale = hd ** -0.5

    def one(x, dn, gain, e, et):
        r = _head_rstd(x, e, et, hd)
        xh = x * r
        t = dn * gain
        mean = _dot_rs(_dot_rs(t * xh, e, 2), et, 2) * (1.0 / hd)
        return r * (t - xh * mean), _csum(dn * xh)

    def body(q_ref, k_ref, dq_ref, dk_ref, gq_ref, gk_ref, e_ref, et_ref, o_ref, sq_ref, sk_ref):
        e, et = e_ref[...], et_ref[...]
        dq, sq = one(q_ref[...], dq_ref[...].astype(F32) * scale, gq_ref[...], e, et)
        dk, sk = one(k_ref[...], dk_ref[...].astype(F32), gk_ref[...], e, et)
        o_ref[:, 0:D] = dq.astype(BF16)
        o_ref[:, D:2 * D] = dk.astype(BF16)
        sq_ref[...] += sq
        sk_ref[...] += sk

    return _rowcall("qk_bwd", body, S, 256,
                    [(proj, D, 0), (proj, D, 1), (dqs, D, 0), (dkn, D, 0)],
                    [gq, gk, grp, grp_t], [(2 * D, BF16)], [(1, D)] * 2, into=into)


def _shift_copies(buf, sh, ts):
    for b in range(1, SUBLANES):
        sh[b - 1] = buf[b:b + ts + HALO - SUBLANES, :]


def _rows_from(buf, sh, o, ts):
    a, b = divmod(o, SUBLANES)
    if b == 0:
        return buf[o:o + ts, :]
    return sh[b - 1, SUBLANES * a:SUBLANES * a + ts, :]


def _conv_fwd(proj, acol, bcol, w_pad, cb, lg, lb, S, C, taps, ts):
    ts = min(ts, S)

    def kern(a_ref, b_ref, w_ref, cb_ref, lg_ref, lb_ref, u1_ref, u3_ref, ubuf, ush):
        @pl.when(pl.program_id(0) == 0)
        def _():
            ubuf[0:HALO, :] = jnp.zeros((HALO, C), F32)

        ubuf[HALO:HALO + ts, :] = a_ref[...] * _sig(b_ref[...])
        _shift_copies(ubuf, ush, ts)
        acc = jnp.zeros((ts, C), F32) + cb_ref[...]
        for k in range(taps):
            acc = acc + w_ref[k:k + 1, :] * _rows_from(ubuf, ush, HALO - (taps - 1) + k, ts)
        u1_ref[...] = acc
        mu = jnp.mean(acc, axis=-1, keepdims=True)
        xc = acc - mu
        rstd = lax.rsqrt(jnp.mean(xc * xc, axis=-1, keepdims=True) + NORM_EPS)
        u2 = xc * rstd * lg_ref[...] + lb_ref[...]
        u3_ref[...] = (u2 * _sig(u2)).astype(BF16)
        ubuf[0:HALO, :] = ubuf[ts:ts + HALO, :]

    vec = lambda a: pl.BlockSpec(a.shape, lambda i: (0, 0))
    return _pcall(
        kern, name="conv_fwd", grid=(S // ts,),
        in_specs=[pl.BlockSpec((ts, C), lambda i: (i, acol)), pl.BlockSpec((ts, C), lambda i: (i, bcol)),
                  vec(w_pad), vec(cb), vec(lg), vec(lb)],
        out_specs=[pl.BlockSpec((ts, C), lambda i: (i, 0))] * 2,
        out_shape=[jax.ShapeDtypeStruct((S, C), F32), jax.ShapeDtypeStruct((S, C), BF16)],
        scratch_shapes=[pltpu.VMEM((HALO + ts, C), F32),
                        pltpu.VMEM((SUBLANES - 1, HALO + ts - SUBLANES, C), F32)],
        compiler_params=_cparams(("arbitrary",)),
    )(proj, proj, w_pad, cb, lg, lb)


def _conv_bwd(du3, u1, proj, acol, bcol, w_pad, lg, lb, S, C, taps, ts, into, ride=None):
    ts = min(ts, S)
    dbuf_hbm, dcol = into
    r_in, r_ispec, r_ospec, r_oshape, r_scratch, r_hook = _ride(ride, 12, 5)
    nt = S // ts
    hb = ts // HALO

    def ln_bwd(g, u, lgv, lbv):
        mu = jnp.mean(u, axis=-1, keepdims=True)
        xc = u - mu
        rstd = lax.rsqrt(jnp.mean(xc * xc, axis=-1, keepdims=True) + NORM_EPS)
        xh = xc * rstd
        u2 = xh * lgv + lbv
        s = _sig(u2)
        du2 = g * (s + u2 * s * (1.0 - s))
        dxh = du2 * lgv
        du1 = rstd * (dxh - jnp.mean(dxh, axis=-1, keepdims=True)
                      - xh * jnp.mean(dxh * xh, axis=-1, keepdims=True))
        return du1, du2, xh

    def kern(*refs):
        g_ref, u_ref, a_ref, b_ref, gn_ref, un_ref, ap_ref, bp_ref, w_ref, lg_ref, lb_ref = refs[:11]
        dg_ref, dw_ref, dcb_ref, dlg_ref, dlb_ref = refs[12 + len(r_in):17 + len(r_in)]
        dbuf, ubuf, dsh, ush = refs[-4:]
        i = pl.program_id(0)
        before, after = r_hook(refs, i == 0, i == nt // 2, i == nt - 1)
        before()

        @pl.when(i == 0)
        def _():
            dw_ref[...] = jnp.zeros(dw_ref.shape, F32)
            dcb_ref[...] = jnp.zeros(dcb_ref.shape, F32)
            dlg_ref[...] = jnp.zeros(dlg_ref.shape, F32)
            dlb_ref[...] = jnp.zeros(dlb_ref.shape, F32)

        lgv, lbv = lg_ref[...], lb_ref[...]
        du1, du2, xh = ln_bwd(g_ref[...], u_ref[...], lgv, lbv)
        dbuf[0:ts, :] = du1
        du1n, _, _ = ln_bwd(gn_ref[...], un_ref[...], lgv, lbv)
        dbuf[ts:ts + HALO, :] = jnp.where(i < nt - 1, du1n, 0.0)
        a = a_ref[...]
        sb = _sig(b_ref[...])
        ubuf[HALO:HALO + ts, :] = a * sb
        ubuf[0:HALO, :] = jnp.where(i > 0, ap_ref[...] * _sig(bp_ref[...]), 0.0)
        dcb_ref[...] += _csum(du1)
        dlg_ref[...] += _csum(du2 * xh)
        dlb_ref[...] += _csum(du2)
        _shift_copies(dbuf, dsh, ts)
        _shift_copies(ubuf, ush, ts)
        for r0 in range(0, ts, CONV_ROWS):
            du0 = jnp.zeros((CONV_ROWS, C), F32)
            for k in range(taps):
                du0 = du0 + w_ref[k:k + 1, :] * _rows_from(dbuf, dsh, r0 + taps - 1 - k, CONV_ROWS)
            ac = a_ref[r0:r0 + CONV_ROWS, :]
            sc = _sig(b_ref[r0:r0 + CONV_ROWS, :])
            dg_ref[r0:r0 + CONV_ROWS, 0:C] = (du0 * sc).astype(BF16)
            dg_ref[r0:r0 + CONV_ROWS, C:2 * C] = (du0 * ac * sc * (1.0 - sc)).astype(BF16)
        for k0 in range(0, taps, CONV_TAPS):
            ks = range(k0, min(k0 + CONV_TAPS, taps))
            accs = [jnp.zeros((SUBLANES, C), F32) for _ in ks]
            for r0 in range(0, ts, CONV_ROWS):
                d = dbuf[r0:r0 + CONV_ROWS, :]
                for t, k in enumerate(ks):
                    prod = d * _rows_from(ubuf, ush, r0 + HALO - (taps - 1) + k, CONV_ROWS)
                    accs[t] = accs[t] + jnp.sum(prod.reshape(CONV_ROWS // SUBLANES, SUBLANES, C), axis=0)
            for t, k in enumerate(ks):
                dw_ref[k:k + 1, :] += _csum(accs[t])
        after()

    vec = lambda a: pl.BlockSpec(a.shape, lambda i: (0, 0))
    tile = lambda cb: pl.BlockSpec((ts, C), functools.partial(lambda i, cb: (i, cb), cb=cb))
    nxt = lambda cb: pl.BlockSpec(
        (HALO, C), functools.partial(lambda i, cb: (jnp.minimum((i + 1) * hb, nt * hb - 1), cb), cb=cb))
    prv = lambda cb: pl.BlockSpec(
        (HALO, C), functools.partial(lambda i, cb: (jnp.maximum(i * hb - 1, 0), cb), cb=cb))
    return _pcall(
        kern, name="conv_bwd", grid=(nt,),
        in_specs=[tile(0), tile(0), tile(acol), tile(bcol), nxt(0), nxt(0), prv(acol), prv(bcol),
                  vec(w_pad), vec(lg), vec(lb), ANY] + r_ispec,
        out_specs=[pl.BlockSpec((ts, 2 * C), lambda i: (i, dcol))]
        + [pl.BlockSpec(w_pad.shape, lambda i: (0, 0))] + [pl.BlockSpec((1, C), lambda i: (0, 0))] * 3 + r_ospec,
        out_shape=[jax.ShapeDtypeStruct(dbuf_hbm.shape, BF16)]
        + [jax.ShapeDtypeStruct(w_pad.shape, F32)] + [jax.ShapeDtypeStruct((1, C), F32)] * 3 + r_oshape,
        scratch_shapes=r_scratch + [pltpu.VMEM((ts + HALO, C), F32), pltpu.VMEM((HALO + ts, C), F32)]
        + [pltpu.VMEM((SUBLANES - 1, HALO + ts - SUBLANES, C), F32)] * 2,
        input_output_aliases={11: 0},
        compiler_params=_cparams(("arbitrary",)),
    )(du3, u1, proj, proj, du3, u1, proj, proj, w_pad, lg, lb, dbuf_hbm, *r_in)


def _gate_merge(proj, gacol, gbcol, ba, bb, S, D):
    def body(ga_ref, gb_ref, a_ref, b_ref, out_ref):
        out_ref[...] = (_sig(ga_ref[...]) * a_ref[...] + _sig(gb_ref[...]) * b_ref[...]).astype(BF16)
    return _rowcall("gate_merge", body, S, 512,
                    [(proj, D, gacol), (proj, D, gbcol), (ba, D, 0), (bb, D, 0)], [], [(D, BF16)], [])[0]


def _gate_bwd(dm, proj, gacol, gbcol, ba, bb, S, D, into):
    def body(dm_ref, ga_ref, gb_ref, a_ref, b_ref, dg_ref, da_ref, db_ref):
        dmv = dm_ref[...]
        sa, sb = _sig(ga_ref[...]), _sig(gb_ref[...])
        da_ref[...] = (dmv * sa).astype(BF16)
        db_ref[...] = (dmv * sb).astype(BF16)
        dg_ref[:, 0:D] = (dmv * a_ref[...] * sa * (1.0 - sa)).astype(BF16)
        dg_ref[:, D:2 * D] = (dmv * b_ref[...] * sb * (1.0 - sb)).astype(BF16)
    return _rowcall("gate_bwd", body, S, 512,
                    [(dm, D, 0), (proj, D, gacol), (proj, D, gbcol), (ba, D, 0), (bb, D, 0)], [],
                    [(2 * D, BF16), (D, BF16), (D, BF16)], [], into=into)


def _resid_norm2(x, mo, g1, g, sc, sh, S, D):
    def body(x_ref, mo_ref, g1_ref, g_ref, sc_ref, sh_ref, x1_ref, h_ref):
        x1 = x_ref[...] + g1_ref[...] * mo_ref[...]
        x1_ref[...] = x1
        r = lax.rsqrt(jnp.mean(x1 * x1, axis=-1, keepdims=True) + NORM_EPS)
        h_ref[...] = ((x1 * r * g_ref[...]) * (1.0 + sc_ref[...]) + sh_ref[...]).astype(BF16)
    return _rowcall("resid_norm2", body, S, 512, [(x, D, 0), (mo, D, 0)], [g1, g, sc, sh],
                    [(D, F32), (D, BF16)], [])


def _loss_dy(x1, ml, tgt, g2, S, D):
    def body(x1_ref, ml_ref, t_ref, g2_ref, dy_ref, dml_ref, sq_ref, dg2_ref):
        mlv = ml_ref[...]
        diff = x1_ref[...] + g2_ref[...] * mlv - t_ref[...]
        dy = diff * (1.0 / D)
        dy_ref[...] = dy
        dml_ref[...] = (dy * g2_ref[...]).astype(BF16)
        sq_ref[...] += _csum(diff * diff)
        dg2_ref[...] += _csum(dy * mlv)
    return _rowcall("loss_dy", body, S, 512, [(x1, D, 0), (ml, D, 0), (tgt, D, 0)], [g2],
                    [(D, F32), (D, BF16)], [(1, D), (1, D)])


def _norm_bwd(name, xin, dh, dres, g, sc, S, D, extra=None):
    def body(*refs):
        if extra is None:
            x_ref, dh_ref, dr_ref, g_ref, sc_ref, dx_ref, dsh_ref, dsc_ref, dg_ref = refs
        else:
            (x_ref, dh_ref, dr_ref, mo_ref, g_ref, sc_ref, g1_ref,
             dx_ref, dmo_ref, dsh_ref, dsc_ref, dg_ref, dg1_ref) = refs
        xv, dhv, gv = x_ref[...], dh_ref[...], g_ref[...]
        r = lax.rsqrt(jnp.mean(xv * xv, axis=-1, keepdims=True) + NORM_EPS)
        xh = xv * r
        dsh_ref[...] += _csum(dhv)
        dsc_ref[...] += _csum(dhv * xh * gv)
        dxg = dhv * (1.0 + sc_ref[...])
        dg_ref[...] += _csum(dxg * xh)
        dxh = dxg * gv
        dx = dr_ref[...] + r * (dxh - xh * jnp.mean(dxh * xh, axis=-1, keepdims=True))
        dx_ref[...] = dx
        if extra is not None:
            dmo_ref[...] = (dx * g1_ref[...]).astype(BF16)
            dg1_ref[...] += _csum(dx * mo_ref[...])

    rows = [(xin, D, 0), (dh, D, 0), (dres, D, 0)]
    vecs = [g, sc]
    if extra is None:
        return _rowcall(name, body, S, 512, rows, vecs, [(D, F32)], [(1, D)] * 3)
    return _rowcall(name, body, S, 512, rows + [(extra[0], D, 0)], vecs + [extra[1]],
                    [(D, F32), (D, BF16)], [(1, D)] * 4)


def _ada_fwd(c_all, w, b_part):
    B, D = c_all.shape
    N = w.shape[1]
    tn = min(512, N)

    def kern(c_ref, w_ref, b_ref, o_ref):
        cv = c_ref[...]
        ca = cv * _sig(cv)
        o_ref[...] = jnp.dot(ca, w_ref[...], precision=lax.Precision.HIGHEST,
                             preferred_element_type=F32) + b_ref[...]

    return _pcall(
        kern, name="ada_fwd", grid=(N // tn,),
        in_specs=[pl.BlockSpec((B, D), lambda j: (0, 0)), pl.BlockSpec((D, tn), lambda j: (0, j)),
                  pl.BlockSpec((1, tn), lambda j: (0, j))],
        out_specs=pl.BlockSpec((B, tn), lambda j: (0, j)),
        out_shape=jax.ShapeDtypeStruct((B, N), F32),
        compiler_params=_cparams(("parallel",)),
    )(c_all, w, b_part)


def _ada_wgrad(c_t_pad, dmod_pad):
    D = c_t_pad.shape[0]
    N = dmod_pad.shape[1]
    tn = min(512, N)

    def kern(c_ref, d_ref, o_ref):
        cv = c_ref[...]
        ca = cv * _sig(cv)
        o_ref[...] = jnp.dot(ca, d_ref[...], precision=lax.Precision.HIGHEST,
                             preferred_element_type=F32)

    return _pcall(
        kern, name="ada_wgrad", grid=(N // tn,),
        in_specs=[pl.BlockSpec((D, LANES), lambda j: (0, 0)), pl.BlockSpec((LANES, tn), lambda j: (0, j))],
        out_specs=pl.BlockSpec((D, tn), lambda j: (0, j)),
        out_shape=jax.ShapeDtypeStruct((D, N), F32),
        compiler_params=_cparams(("parallel",)),
    )(c_t_pad, dmod_pad)


def _ag_small(name, arrs):
    n = len(arrs)

    def kern(*refs):
        ins, outs = refs[:n], refs[n:2 * n]
        send, recv = refs[2 * n], refs[2 * n + 1]
        x, y, c = lax.axis_index("x"), lax.axis_index("y"), lax.axis_index("c")
        me = 4 * x + 2 * y + c

        def copy(i, m, slot):
            peer = (x ^ ((m >> 2) & 1), y ^ ((m >> 1) & 1), c ^ (m & 1))
            return pltpu.make_async_remote_copy(
                src_ref=ins[i], dst_ref=outs[i].at[slot],
                send_sem=send.at[i * 7 + m - 1], recv_sem=recv.at[i * 7 + m - 1],
                device_id=peer, device_id_type=MESH)

        for i in range(n):
            outs[i][me] = ins[i][...]
            for m in range(1, 8):
                copy(i, m, me).start()
        for i in range(n):
            for m in range(1, 8):
                copy(i, m, me).wait_send()
                copy(i, m, me ^ m).wait_recv()

    vm = pl.BlockSpec(memory_space=pltpu.VMEM)
    return _pcall(
        kern, name=name, in_specs=[vm] * n, out_specs=[vm] * n,
        out_shape=[jax.ShapeDtypeStruct((8,) + a.shape, a.dtype) for a in arrs],
        scratch_shapes=[pltpu.SemaphoreType.DMA((7 * n,)), pltpu.SemaphoreType.DMA((7 * n,))],
        compiler_params=pltpu.CompilerParams(has_side_effects=True),
    )(*arrs)


def _exchange(name, arrs, plan):
    out_shape, scratch, phases = plan(arrs)

    def kern(*refs):
        for phase in phases(refs):
            phase()

    return _pcall(
        kern, name=name, in_specs=[ANY] * len(arrs), out_specs=[ANY] * len(out_shape),
        out_shape=out_shape, scratch_shapes=scratch,
        compiler_params=pltpu.CompilerParams(has_side_effects=True),
    )(*arrs)


def _ride(plan_and_arrs, n_in, n_out):
    if plan_and_arrs is None:
        return [], [], [], [], [], lambda refs, first, middle, last: ((lambda: None), (lambda: None))
    plan, arrs = plan_and_arrs
    out_shape, scratch, phases = plan(arrs)
    na, no = len(arrs), len(out_shape)

    def hook(refs, first, middle, last):
        mine = refs[n_in:n_in + na] + refs[n_in + na + n_out:]
        start, mid, finish = phases(mine)

        def before():
            pl.when(first)(start)
            pl.when(middle)(mid)

        def after():
            pl.when(last)(finish)

        return before, after

    return list(arrs), [ANY] * na, [ANY] * no, out_shape, scratch, hook


def _gather_plan(arrs):
    n = len(arrs)

    def phases(refs):
        ins, outs = refs[:n], refs[n:2 * n]
        s1, r1, s2, r2, loc = refs[2 * n:2 * n + 5]
        x, y, c = lax.axis_index("x"), lax.axis_index("y"), lax.axis_index("c")
        me = 2 * x + y

        def half(i, hc):
            hr = ins[i].shape[0] // 2
            return pl.ds(hc * hr, hr)

        def own(i):
            return pltpu.make_async_remote_copy(
                src_ref=ins[i], dst_ref=outs[i].at[me], send_sem=loc.at[i], recv_sem=loc.at[n + i],
                device_id=(x, y, 1 - c), device_id_type=MESH)

        def fetch(i, m, slot):
            px, py = x ^ ((m >> 1) & 1), y ^ (m & 1)
            return pltpu.make_async_remote_copy(
                src_ref=ins[i].at[half(i, c)], dst_ref=outs[i].at[slot, half(i, c)],
                send_sem=s1.at[i * 3 + m - 1], recv_sem=r1.at[i * 3 + m - 1],
                device_id=(px, py, c), device_id_type=MESH)

        def passed(i, m, hc):
            return pltpu.make_async_remote_copy(
                src_ref=outs[i].at[me ^ m, half(i, hc)], dst_ref=outs[i].at[me ^ m, half(i, hc)],
                send_sem=s2.at[i * 3 + m - 1], recv_sem=r2.at[i * 3 + m - 1],
                device_id=(x, y, 1 - c), device_id_type=MESH)

        def start():
            for i in range(n):
                for m in range(1, 4):
                    fetch(i, m, me).start()
            for i in range(n):
                own(i).start()

        def mid():
            for i in range(n):
                for m in range(1, 4):
                    fetch(i, m, me ^ m).wait_recv()
                    passed(i, m, c).start()

        def finish():
            for i in range(n):
                own(i).wait()
                for m in range(1, 4):
                    fetch(i, m, me).wait_send()
                    passed(i, m, c).wait_send()
                    passed(i, m, 1 - c).wait_recv()

        return start, mid, finish

    out_shape = [jax.ShapeDtypeStruct((4,) + a.shape, a.dtype) for a in arrs]
    scratch = [pltpu.SemaphoreType.DMA((3 * n,))] * 4 + [pltpu.SemaphoreType.DMA((2 * n,))]
    return out_shape, scratch, phases


def _pair_halves_plan(arrs):
    n = len(arrs)

    def phases(refs):
        ins, outs = refs[:n], refs[n:2 * n]
        send, recv = refs[2 * n], refs[2 * n + 1]
        x, y, c = lax.axis_index("x"), lax.axis_index("y"), lax.axis_index("c")

        def copy(i, k, hc):
            return pltpu.make_async_remote_copy(
                src_ref=ins[i].at[k, hc], dst_ref=outs[i].at[k],
                send_sem=send.at[i * 4 + k], recv_sem=recv.at[i * 4 + k],
                device_id=(x, y, 1 - c), device_id_type=MESH)

        def start():
            for i in range(n):
                for k in range(4):
                    copy(i, k, 1 - c).start()

        def finish():
            for i in range(n):
                for k in range(4):
                    copy(i, k, 1 - c).wait()

        return start, (lambda: None), finish

    out_shape = [jax.ShapeDtypeStruct((4,) + a.shape[2:], a.dtype) for a in arrs]
    scratch = [pltpu.SemaphoreType.DMA((4 * n,)), pltpu.SemaphoreType.DMA((4 * n,))]
    return out_shape, scratch, phases


def _scatter_plan(arrs):
    n = len(arrs)

    def phases(refs):
        ins, outs = refs[:n], refs[n:2 * n]
        send, recv = refs[2 * n], refs[2 * n + 1]
        x, y, c = lax.axis_index("x"), lax.axis_index("y"), lax.axis_index("c")
        me = 2 * x + y

        def copy(i, m, slot):
            px, py = x ^ ((m >> 1) & 1), y ^ (m & 1)
            return pltpu.make_async_remote_copy(
                src_ref=ins[i].at[2 * px + py], dst_ref=outs[i].at[slot],
                send_sem=send.at[i * 3 + m - 1], recv_sem=recv.at[i * 3 + m - 1],
                device_id=(px, py, c), device_id_type=MESH)

        def start():
            for i in range(n):
                for m in range(1, 4):
                    copy(i, m, me).start()

        def finish():
            for i in range(n):
                for m in range(1, 4):
                    copy(i, m, me).wait_send()
                    copy(i, m, me ^ m).wait_recv()

        return start, (lambda: None), finish

    out_shape = [jax.ShapeDtypeStruct(a.shape, a.dtype) for a in arrs]
    scratch = [pltpu.SemaphoreType.DMA((3 * n,)), pltpu.SemaphoreType.DMA((3 * n,))]
    return out_shape, scratch, phases


def _pair_swap_plan(arrs):
    n = len(arrs)

    def phases(refs):
        ins, outs = refs[:n], refs[n:2 * n]
        send, recv = refs[2 * n], refs[2 * n + 1]
        x, y, c = lax.axis_index("x"), lax.axis_index("y"), lax.axis_index("c")

        def copy(i):
            return pltpu.make_async_remote_copy(
                src_ref=ins[i], dst_ref=outs[i], send_sem=send.at[i], recv_sem=recv.at[i],
                device_id=(x, y, 1 - c), device_id_type=MESH)

        def start():
            for i in range(n):
                copy(i).start()

        def finish():
            for i in range(n):
                copy(i).wait()

        return start, (lambda: None), finish

    out_shape = [jax.ShapeDtypeStruct(a.shape, a.dtype) for a in arrs]
    scratch = [pltpu.SemaphoreType.DMA((n,)), pltpu.SemaphoreType.DMA((n,))]
    return out_shape, scratch, phases


def _row_tile(R):
    for t in (256, 128, 64, 32, 16, 8):
        if R % t == 0:
            return t
    return R


def _sum_slots(name, parts):
    K, R, C = parts.shape
    tr = _row_tile(R)

    def kern(p_ref, o_ref):
        acc = p_ref[0].astype(F32)
        for k in range(1, K):
            acc = acc + p_ref[k].astype(F32)
        o_ref[...] = acc

    return _pcall(
        kern, name=name, grid=(R // tr,),
        in_specs=[pl.BlockSpec((K, tr, C), lambda i: (0, i, 0))],
        out_specs=pl.BlockSpec((tr, C), lambda i: (i, 0)),
        out_shape=jax.ShapeDtypeStruct((R, C), F32),
        compiler_params=_cparams(("parallel",)),
    )(parts)


def _sum_pair(name, core, mine, theirs):
    K, _, hr, C = mine.shape
    tr = _row_tile(hr)

    def kern(c_ref, a_ref, b_ref, o_ref):
        o_ref[0] = (a_ref[0, 0].astype(F32) + b_ref[0].astype(F32)).astype(BF16)

    return _pcall(
        kern, name=name, out_shape=jax.ShapeDtypeStruct((K, hr, C), BF16),
        grid_spec=pltpu.PrefetchScalarGridSpec(
            num_scalar_prefetch=1, grid=(K, hr // tr),
            in_specs=[pl.BlockSpec((1, 1, tr, C), lambda k, r, c_ref: (k, c_ref[0], r, 0)),
                      pl.BlockSpec((1, tr, C), lambda k, r, c_ref: (k, r, 0))],
            out_specs=pl.BlockSpec((1, tr, C), lambda k, r, c_ref: (k, r, 0))),
        compiler_params=_cparams(("parallel", "parallel")),
    )(core, mine, theirs)


def _sum_chips(name, chip, own, recv):
    K, hr, C = own.shape
    tr = _row_tile(hr)

    def kern(chip_ref, own_ref, *rest):
        r_refs, o_ref = rest[:K], rest[K]
        me = chip_ref[0]
        mine = own_ref[0].astype(F32)
        acc = None
        for k in range(K):
            t = jnp.where(me == k, mine, r_refs[k][0].astype(F32))
            acc = t if acc is None else acc + t
        o_ref[...] = acc

    def other(k):
        return pl.BlockSpec((1, tr, C), lambda r, s: (jnp.where(s[0] == k, (k + 1) % K, k), r, 0))

    return _pcall(
        kern, name=name, out_shape=jax.ShapeDtypeStruct((hr, C), F32),
        grid_spec=pltpu.PrefetchScalarGridSpec(
            num_scalar_prefetch=1, grid=(hr // tr,),
            in_specs=[pl.BlockSpec((1, tr, C), lambda r, s: (s[0], r, 0))] + [other(k) for k in range(K)],
            out_specs=pl.BlockSpec((tr, C), lambda r, s: (r, 0))),
        compiler_params=_cparams(("parallel",)),
    )(chip, own, *([recv] * K))


def _adam_update(w, m, v, g):
    c1 = 1.0 - ADAM_B1 ** ADAM_STEP
    c2 = 1.0 - ADAM_B2 ** ADAM_STEP
    mn = ADAM_B1 * m + (1.0 - ADAM_B1) * g
    vn = ADAM_B2 * v + (1.0 - ADAM_B2) * (g * g)
    return -ADAM_LR * ((mn / c1) / (jnp.sqrt(vn / c2) + ADAM_EPS) + ADAM_WD * w), mn, vn


def _adamw_halves(name, core, w, m, v, mine, theirs):
    R, C = w.shape
    hr = mine.shape[0]
    tr = _row_tile(hr)
    nbh = hr // tr

    def kern(c_ref, w_ref, m_ref, v_ref, a_ref, b_ref, go_ref, d_ref, mo_ref, vo_ref):
        g = jnp.where(pl.program_id(0) // nbh == c_ref[0], a_ref[...], b_ref[...])
        d, mn, vn = _adam_update(w_ref[...], m_ref[...], v_ref[...], g)
        go_ref[...] = g
        d_ref[...] = d
        mo_ref[...] = mn
        vo_ref[...] = vn

    spec = pl.BlockSpec((tr, C), lambda i, s: (i, 0))
    hspec = pl.BlockSpec((tr, C), lambda i, s: (i % nbh, 0))
    return _pcall(
        kern, name=name, out_shape=[jax.ShapeDtypeStruct((R, C), F32)] * 4,
        grid_spec=pltpu.PrefetchScalarGridSpec(
            num_scalar_prefetch=1, grid=(R // tr,),
            in_specs=[spec, spec, spec, hspec, hspec], out_specs=[spec] * 4),
        compiler_params=_cparams(("parallel",)),
    )(core, w, m, v, mine, theirs)


def _adamw_halves_t(name, core, w_t, m_t, v_t, mine_t, theirs_t):
    C, R = w_t.shape
    hr = mine_t.shape[1]
    tc = min(256, hr)
    nbh = hr // tc

    def kern(c_ref, w_ref, m_ref, v_ref, a_ref, b_ref, go_ref, d_ref, mo_ref, vo_ref):
        g = jnp.where(pl.program_id(0) // nbh == c_ref[0], a_ref[...], b_ref[...])
        d, mn, vn = _adam_update(w_ref[...], m_ref[...], v_ref[...], g)
        go_ref[...] = g
        d_ref[...] = d
        mo_ref[...] = mn
        vo_ref[...] = vn

    spec = pl.BlockSpec((C, tc), lambda j, s: (0, j))
    hspec = pl.BlockSpec((C, tc), lambda j, s: (0, j % nbh))
    return _pcall(
        kern, name=name, out_shape=[jax.ShapeDtypeStruct((C, R), F32)] * 4,
        grid_spec=pltpu.PrefetchScalarGridSpec(
            num_scalar_prefetch=1, grid=(R // tc,),
            in_specs=[spec, spec, spec, hspec, hspec], out_specs=[spec] * 4),
        compiler_params=_cparams(("parallel",)),
    )(core, w_t, m_t, v_t, mine_t, theirs_t)


def _adamw_many(name, wmvg):
    n = len(wmvg[0])

    def kern(*refs):
        ins, outs = refs[:4 * n], refs[4 * n:]
        for j in range(n):
            g = ins[3 * n + j][...]
            d, mn, vn = _adam_update(ins[j][...], ins[n + j][...], ins[2 * n + j][...], g)
            for i, val in enumerate((g, d, mn, vn)):
                outs[i * n + j][...] = val

    vm = pl.BlockSpec(memory_space=pltpu.VMEM)
    flat = [a for group in wmvg for a in group]
    outs = _pcall(
        kern, name=name, in_specs=[vm] * (4 * n), out_specs=[vm] * (4 * n),
        out_shape=[jax.ShapeDtypeStruct(a.shape, F32) for _ in range(4) for a in wmvg[0]],
    )(*flat)
    return [outs[i * n:(i + 1) * n] for i in range(4)]


def _adamw(name, w, m, v, gparts, ride=None):
    R, C = w.shape
    K = gparts.shape[0]
    tr = _row_tile(R)
    nr = R // tr
    r_in, r_ispec, r_ospec, r_oshape, r_scratch, r_hook = _ride(ride, 4, 4)

    def kern(*refs):
        w_ref, m_ref, v_ref, g_ref = refs[:4]
        go_ref, d_ref, mo_ref, vo_ref = refs[4 + len(r_in):8 + len(r_in)]
        i = pl.program_id(0)
        before, after = r_hook(refs, i == 0, i == nr // 2, i == nr - 1)
        before()
        g = g_ref[0]
        for k in range(1, K):
            g = g + g_ref[k]
        d, mn, vn = _adam_update(w_ref[...], m_ref[...], v_ref[...], g)
        go_ref[...] = g
        d_ref[...] = d
        mo_ref[...] = mn
        vo_ref[...] = vn
        after()

    spec = pl.BlockSpec((tr, C), lambda i: (i, 0))
    return _pcall(
        kern, name=name, grid=(nr,),
        in_specs=[spec, spec, spec, pl.BlockSpec((K, tr, C), lambda i: (0, i, 0))] + r_ispec,
        out_specs=[spec] * 4 + r_ospec,
        out_shape=[jax.ShapeDtypeStruct((R, C), F32)] * 4 + r_oshape,
        scratch_shapes=r_scratch,
        compiler_params=_cparams(("arbitrary",) if ride else ("parallel",)),
    )(w, m, v, gparts, *r_in)


def _round_up(a, b):
    return (a + b - 1) // b * b


def kernel(x, c, w_ada, b_ada, norm1_g, w_in, b_forget, q_norm_g, k_norm_g, w_attn_proj, conv_w, conv_b, conv_ln_g, conv_ln_b, w_conv_proj, w_out, norm2_g, w_mlp1, w_mlp2, loss_target, m_w_ada, m_b_ada, m_norm1_g, m_w_in, m_b_forget, m_q_norm_g, m_k_norm_g, m_w_attn_proj, m_conv_w, m_conv_b, m_conv_ln_g, m_conv_ln_b, m_w_conv_proj, m_w_out, m_norm2_g, m_w_mlp1, m_w_mlp2, v_w_ada, v_b_ada, v_norm1_g, v_w_in, v_b_forget, v_q_norm_g, v_k_norm_g, v_w_attn_proj, v_conv_w, v_conv_b, v_conv_ln_g, v_conv_ln_b, v_w_conv_proj, v_w_out, v_norm2_g, v_w_mlp1, v_w_mlp2):
    S, D = x.shape[1], x.shape[2]
    NH, HD = b_forget.shape[-1], q_norm_g.shape[-1]
    TAPS = conv_w.shape[1]
    DIN_S = w_in.shape[-1]
    DIN = 4 * DIN_S
    DFF_S = w_mlp1.shape[-1]
    DFF = 4 * DFF_S
    ADA_S = w_ada.shape[-1]
    DS = w_attn_proj.shape[1]
    CS = conv_w.shape[-1]
    assert NH * HD == D and DIN == 7 * D + NH and TAPS - 1 <= HALO and D % LANES == 0 and 2 * HD == LANES
    NP = _round_up(7 * D + LANES, 512)
    FW = NP - 7 * D
    assert (7 * D) % FW == 0
    TQ = min(512, S)
    NQ = S // TQ
    FCOL = 7 * D // LANES

    xi, yi, ci = lax.axis_index("x"), lax.axis_index("y"), lax.axis_index("c")
    chip = 2 * xi + yi
    dev = 4 * xi + 2 * yi + ci

    x2 = x.reshape(S, D)
    tgt = loss_target.reshape(S, D)

    lane_head = jnp.arange(D, dtype=jnp.int32) // HD
    grp = (lane_head[:, None] == jnp.arange(LANES, dtype=jnp.int32)[None, :]).astype(BF16)
    grp_t = grp.T
    sel = ((jnp.arange(D, dtype=jnp.int32)[:, None] == HD * jnp.arange(LANES, dtype=jnp.int32)[None, :])
           .astype(BF16))
    ch = min(256, S)
    ii = jnp.arange(ch, dtype=jnp.int32)
    tri = (ii[None, :] <= ii[:, None]).astype(BF16)
    tri_u = tri.T
    gq_t = jnp.tile(q_norm_g.reshape(1, HD), (1, NH))
    gk_t = jnp.tile(k_norm_g.reshape(1, HD), (1, NH))
    bf_pad = jnp.pad(b_forget.reshape(1, NH), ((0, 0), (0, LANES - NH)))

    c_all, cw_all = _ag_small(
        "ag_c_convw", [c.reshape(1, D), jnp.pad(conv_w.reshape(TAPS, CS), ((0, HALO - TAPS), (0, 0)))])
    c_all = c_all.reshape(8, D)
    b_part = lax.dynamic_slice(b_ada.reshape(1, -1), (0, chip * ADA_S), (1, ADA_S))
    mod_part = _ada_fwd(c_all, w_ada.reshape(D, ADA_S), b_part)
    (mod_all,) = _ag_small("ag_mod", [mod_part])
    mod_full = jnp.concatenate([mod_all[0], mod_all[2], mod_all[4], mod_all[6]], axis=1)
    mod = lax.dynamic_slice(mod_full, (dev, 0), (1, 6 * D))
    sh1, sc1, g1, sh2, sc2, g2 = [mod[:, i * D:(i + 1) * D] for i in range(6)]

    shards = [w_in.reshape(D, DIN_S), w_attn_proj.reshape(DS, D), w_conv_proj.reshape(DS, D),
              w_out.reshape(DS, D), w_mlp1.reshape(D, DFF_S), w_mlp2.reshape(DFF_S, D)]
    shards = [s.astype(BF16) for s in shards]
    (gw_in,) = _exchange("ag_w_in", shards[:1], _gather_plan)
    w_conv = jnp.concatenate([cw_all[0], cw_all[2], cw_all[4], cw_all[6]], axis=1)

    SEGS = [(0, 2 * D, 0), (3 * D + NH, DIN, 2 * D), (2 * D, 3 * D + NH, 6 * D)]

    def pieces(a, b):
        out = []
        for k in range(4):
            lo, hi = max(a, k * DIN_S), min(b, (k + 1) * DIN_S)
            if lo < hi:
                out.append(gw_in[k][:, lo - k * DIN_S:hi - k * DIN_S])
        return out

    w_in_p = jnp.concatenate([p for (a, b, _) in SEGS for p in pieces(a, b)]
                             + [jnp.zeros((D, NP - 7 * D - NH), BF16)], axis=1)

    n1g = norm1_g.reshape(1, D)
    n2g = norm2_g.reshape(1, D)
    h = _norm_mod("norm_mod1", x2, n1g, sc1, sh1, S, D)
    proj = _mm("mm_in", h, w_in_p, "nn", [F32], tn=1536)
    qs, kn, vb = _qk_prep(proj, 6, gq_t, gk_t, grp, grp_t, S, D, HD)
    f_cum = _fgate_fwd(proj, FCOL, bf_pad, tri, S)
    fk_c = f_cum[:, :NH]
    fk_r = fk_c.T.reshape(NH, NQ, 1, TQ)
    o, o32, lse_b, gw_ap, gw_cp, gw_out, gw_m1, gw_m2 = _flash_fwd(
        qs, kn, vb, fk_r, S, D, HD, TQ, ride=(_gather_plan, shards[1:]))
    w_ap = gw_ap.reshape(D, D)
    w_cp = gw_cp.reshape(D, D)
    w_o = gw_out.reshape(D, D)
    w_m1 = jnp.transpose(gw_m1, (1, 0, 2)).reshape(D, DFF)
    w_m2 = gw_m2.reshape(DFF, D)
    br_a = _mm("mm_attn_proj", o, w_ap, "nn", [F32])
    cb, clg, clb = conv_b.reshape(1, D), conv_ln_g.reshape(1, D), conv_ln_b.reshape(1, D)
    u1, u3 = _conv_fwd(proj, 2, 3, w_conv, cb, clg, clb, S, D, TAPS, 256)
    br_b = _mm("mm_conv_proj", u3, w_cp, "nn", [F32])
    merged = _gate_merge(proj, 4, 5, br_a, br_b, S, D)
    mo = _mm("mm_out", merged, w_o, "nn", [F32])
    x1, h2 = _resid_norm2(x2, mo, g1, n2g, sc2, sh2, S, D)

    def relu2(r):
        rp = jnp.maximum(r, 0.0)
        return (rp * rp,)
    z = _mm("mm_mlp1", h2, w_m1, "nn", [BF16], epi=relu2)
    ml = _mm("mm_mlp2", z, w_m2, "nn", [F32])
    dy, dml, sq, dg2 = _loss_dy(x1, ml, tgt, g2, S, D)
    loss_part = jnp.full((1, LANES), 0.5 * jnp.sum(sq) / D, F32)

    da = _mm("mm_dz", dml, w_m2, "nt", [BF16], epi=lambda r, zz: (r * 2.0 * jnp.sqrt(zz.astype(F32)),),
             extras=(z,))
    dw_m2 = _mm("mm_dw_mlp2", z, dml, "tn", [BF16])
    dw_m1 = _mm("mm_dw_mlp1", h2, da, "tn", [BF16])
    dh2 = _mm("mm_dh2", da, w_m1, "nt", [F32])
    dx1, dmo, dsh2, dsc2, dn2g, dg1 = _norm_bwd("norm2_bwd", x1, dh2, dy, n2g, sc2, S, D, extra=(mo, g1))
    dmerged = _mm("mm_dmerged", dmo, w_o, "nt", [F32])
    dw_o = _mm("mm_dw_out", merged, dmo, "tn", [BF16])
    dproj, dba, dbb = _gate_bwd(dmerged, proj, 4, 5, br_a, br_b, S, D, into=(lax.empty((S, NP), BF16), 2))
    do = _mm("mm_do", dba, w_ap, "nt", [BF16])
    dw_ap = _mm("mm_dw_attn_proj", o, dba, "tn", [BF16])
    du3 = _mm("mm_du3", dbb, w_cp, "nt", [F32])
    dw_cp = _mm("mm_dw_conv_proj", u3, dbb, "tn", [BF16])
    core = ci.astype(jnp.int32).reshape(1)
    chip1 = chip.astype(jnp.int32).reshape(1)
    halves = lambda p: p.astype(BF16).reshape(4, 2, p.shape[1] // 2, p.shape[2])
    names = ["w_in", "w_attn_proj", "w_conv_proj", "w_out", "w_mlp1", "w_mlp2"]
    parts = [halves(p) for p in (dw_ap.reshape(4, DS, D), dw_cp.reshape(4, DS, D), dw_o.reshape(4, DS, D),
                                 jnp.transpose(dw_m1.reshape(D, 4, DFF_S), (1, 0, 2)), dw_m2.reshape(4, DFF_S, D))]
    dproj, dcw, dcb, dclg, dclb, *theirs = _conv_bwd(
        du3, u1, proj, 2, 3, w_conv, clg, clb, S, D, TAPS, 256, into=(dproj, 1), ride=(_pair_halves_plan, parts))
    chip_parts =[_sum_pair("sum_pair_" + nm, core, p, t) for nm, p, t in zip(names[1:], parts, theirs)]

    delta_c, lse_c = _delta_prep(do, o32, lse_b, grp, sel, S, D)
    to_rows = lambda t: t[:, :NH].T.reshape(NH, NQ, 1, TQ)
    dkn, dproj, dqs, dfq_r, dfk_b, *recvd = _flash_bwd(
        qs, kn, vb, do, f_cum, to_rows(lse_c), to_rows(delta_c), S, D, HD, TQ,
        dv_into=(dproj, 6 * D // LANES), ride=(_scatter_plan, chip_parts))
    dproj, sq_q, sq_k = _qk_bwd(proj, dqs, dkn, gq_t, gk_t, grp, grp_t, S, D, HD, into=(dproj, 0))
    to_cols = lambda r: jnp.pad(r.reshape(NH, S).T, ((0, 0), (0, LANES - NH)))
    dfq_pad = to_cols(dfq_r)
    dfk_pad = jnp.pad(dfk_b[:, ::HD], ((0, 0), (0, LANES - NH)))
    dproj, dbf = _fgate_bwd(dfk_pad, dfq_pad, proj, FCOL, bf_pad, tri_u, NH, S, FW, into=(dproj, 7 * D // FW))
    dw_in_p = _mm("mm_dw_in", h, dproj, "tn", [BF16])
    def shard_cols(k):
        out = []
        for (a, b, start) in sorted(SEGS):
            lo, hi = max(a, k * DIN_S), min(b, (k + 1) * DIN_S)
            if lo < hi:
                out.append(dw_in_p[:, start + lo - a:start + hi - a])
        return jnp.concatenate(out, axis=1)

    part_in = halves(jnp.stack([shard_cols(k) for k in range(4)]))
    (their_in,) = _exchange("rs_pair_w_in", [part_in], _pair_halves_plan)
    chip_in = _sum_pair("sum_pair_w_in", core, part_in, their_in)
    dh, recv_in = _mm("mm_dh", dproj, w_in_p, "nt", [F32], ride=(_scatter_plan, [chip_in]))
    gx, dsh1, dsc1, dn1g = _norm_bwd("norm1_bwd", x2, dh, dx1, n1g, sc1, S, D)

    packed = jnp.concatenate([dsh1, dsc1, dg1, dsh2, dsc2, dg2, dn1g, dcb, dclg, dclb, dn2g,
                              sq_q, sq_k, dbf, loss_part], axis=1)
    small_all, dcw_all = _ag_small("ag_small_grads", [packed, dcw])
    small = _sum_slots("sum_small", small_all.reshape(8, 1, -1)).reshape(1, -1)
    dmod_sum = small[:, :6 * D]
    seg = lambda k: small[:, (6 + k) * D:(7 + k) * D]
    g_n1g, g_cb, g_clg, g_clb, g_n2g = seg(0), seg(1), seg(2), seg(3), seg(4)
    g_qn = _sum_slots("sum_qn", seg(5).reshape(NH, 1, HD))
    g_kn = _sum_slots("sum_kn", seg(6).reshape(NH, 1, HD))
    g_bf = small[:, 13 * D:13 * D + NH]
    loss = small[0, 13 * D + LANES]
    dcw_mine = lax.dynamic_slice(dcw_all[:, :TAPS, :], (0, 0, chip * CS), (8, TAPS, CS))

    dmod_all = small_all.reshape(8, -1)[:, :6 * D]
    dmod_cols = lax.dynamic_slice(dmod_all, (0, chip * ADA_S), (8, ADA_S))
    c_t_pad = jnp.pad(c_all.T, ((0, 0), (0, LANES - 8)))
    g_wada = _ada_wgrad(c_t_pad, jnp.pad(dmod_cols, ((0, LANES - 8), (0, 0))))

    sums =[_sum_chips("sum_" + nm, chip1, p, r)
            for nm, p, r in zip(names, [chip_in] + chip_parts, [recv_in] + list(recvd))]

    res = {}
    outs = _adamw("adamw_w_ada", w_ada.reshape(D, ADA_S), m_w_ada.reshape(D, ADA_S),
                  v_w_ada.reshape(D, ADA_S), g_wada.reshape(1, D, ADA_S), ride=(_pair_swap_plan, sums))
    res["w_ada"] = [t.reshape(w_ada.shape) for t in outs[:4]]
    big = {nm: (a, b) for nm, a, b in zip(names, sums, outs[4:])}
    big_w = {"w_in": (w_in, m_w_in, v_w_in), "w_attn_proj": (w_attn_proj, m_w_attn_proj, v_w_attn_proj),
             "w_conv_proj": (w_conv_proj, m_w_conv_proj, v_w_conv_proj), "w_out": (w_out, m_w_out, v_w_out),
             "w_mlp1": (w_mlp1, m_w_mlp1, v_w_mlp1), "w_mlp2": (w_mlp2, m_w_mlp2, v_w_mlp2)}
    for nm in names:
        shp = big_w[nm][0].shape
        if shp[2] % LANES:
            outs = _adamw_halves_t("adamw_" + nm, core, *[t.reshape(shp[1], shp[2]).T for t in big_w[nm]],
                                   *[t.T for t in big[nm]])
            res[nm] = [t.T.reshape(shp) for t in outs]
        else:
            outs = _adamw_halves("adamw_" + nm, core, *[t.reshape(shp[1], shp[2]) for t in big_w[nm]], *big[nm])
            res[nm] = [t.reshape(shp) for t in outs]
    outs = _adamw("adamw_conv_w", conv_w.reshape(TAPS, CS), m_conv_w.reshape(TAPS, CS),
                  v_conv_w.reshape(TAPS, CS), dcw_mine)
    res["conv_w"] = [t.reshape(conv_w.shape) for t in outs]

    small_w = [("b_ada", b_ada, m_b_ada, v_b_ada, dmod_sum), ("norm1_g", norm1_g, m_norm1_g, v_norm1_g, g_n1g),
               ("b_forget", b_forget, m_b_forget, v_b_forget, g_bf),
               ("q_norm_g", q_norm_g, m_q_norm_g, v_q_norm_g, g_qn),
               ("k_norm_g", k_norm_g, m_k_norm_g, v_k_norm_g, g_kn),
               ("conv_b", conv_b, m_conv_b, v_conv_b, g_cb), ("conv_ln_g", conv_ln_g, m_conv_ln_g, v_conv_ln_g, g_clg),
               ("conv_ln_b", conv_ln_b, m_conv_ln_b, v_conv_ln_b, g_clb),
               ("norm2_g", norm2_g, m_norm2_g, v_norm2_g, g_n2g)]
    outs = _adamw_many("adamw_small", [[t[i].reshape(1, -1) for t in small_w] for i in (1, 2, 3, 4)])
    for j, (nm, w_, _, _, _) in enumerate(small_w):
        res[nm] = [outs[i][j].reshape(w_.shape) for i in range(4)]

    order = ["w_ada", "b_ada", "norm1_g", "w_in", "b_forget", "q_norm_g", "k_norm_g", "w_attn_proj", "conv_w",
             "conv_b", "conv_ln_g", "conv_ln_b", "w_conv_proj", "w_out", "norm2_g", "w_mlp1", "w_mlp2"]
    return (loss, gx.reshape(x.shape), *[res[n][0] for n in order], *[res[n][1] for n in order],
            *[res[n][2] for n in order], *[res[n][3] for n in order])
```

```python
import functools

import jax
import jax.numpy as jnp
from jax import lax
from jax.experimental import pallas as pl
from jax.experimental.pallas import tpu as pltpu

F32 = jnp.float32
BF16 = jnp.bfloat16
MESH = pl.DeviceIdType.MESH
ANY = pl.BlockSpec(memory_space=pl.ANY)

NORM_EPS = 1e-6
ADAM_LR = 0.001
ADAM_B1 = 0.9
ADAM_B2 = 0.999
ADAM_EPS = 1e-08
ADAM_WD = 0.01
ADAM_STEP = 10
LANES = 128
SUBLANES = 8
HALO = 32
CONV_ROWS = 32
CONV_TAPS = 4
NEG = -1e30
VMEM_LIMIT = 56 * 1024 * 1024


def _pcall(body, **kw):
    return pl.pallas_call(body, **kw)


def _cparams(sem=None):
    if sem is None:
        return pltpu.CompilerParams(vmem_limit_bytes=VMEM_LIMIT)
    return pltpu.CompilerParams(dimension_semantics=sem, vmem_limit_bytes=VMEM_LIMIT)


def _sig(x):
    return 1.0 / (1.0 + jnp.exp(-x))


def _split3(x):
    x1 = x.astype(BF16)
    r = x - x1.astype(F32)
    x2 = r.astype(BF16)
    x3 = (r - x2.astype(F32)).astype(BF16)
    return x1, x2, x3


def _dot_rs(x, e, terms=3):
    out = None
    for t in _split3(x)[:terms]:
        d = jnp.dot(t, e, preferred_element_type=F32)
        out = d if out is None else out + d
    return out


def _dot_ls(e, x):
    out = None
    for t in _split3(x):
        d = jnp.dot(e, t, preferred_element_type=F32)
        out = d if out is None else out + d
    return out


def _tile(n, want):
    if n <= want:
        return n
    t = want - want % LANES
    while n % t:
        t -= LANES
    assert t > 0, (n, want)
    return t


_DIMS = {"nn": ((1,), (0,)), "nt": ((1,), (1,)), "tn": ((0,), (0,))}


def _mm(name, a, b, mode, out_dtypes, epi=None, extras=(), tm=1024, tn=1024, tk=4096, ride=None):
    if mode == "nn":
        (M, K), (_, N) = a.shape, b.shape
    elif mode == "nt":
        (M, K), (N, _) = a.shape, b.shape
    else:
        (K, M), (_, N) = a.shape, b.shape
    tm, tn, tk = _tile(M, tm), _tile(N, tn), _tile(K, tk)
    nm, nn, nk = M // tm, N // tn, K // tk
    ne, no = len(extras), len(out_dtypes)
    dims = (_DIMS[mode], ((), ()))
    r_in, r_ispec, r_ospec, r_oshape, r_scratch, r_hook = _ride(ride, 2 + ne, no)

    def kern(*refs):
        a_ref, b_ref = refs[0], refs[1]
        e_refs = refs[2:2 + ne]
        o_refs = refs[2 + ne + len(r_in):2 + ne + len(r_in) + no]
        i, j, k = pl.program_id(0), pl.program_id(1), pl.program_id(2)
        before, after = r_hook(refs, (i == 0) & (j == 0) & (k == 0), (i == nm // 2) & (j == 0) & (k == 0),
                               (i == nm - 1) & (j == nn - 1) & (k == nk - 1))
        before()
        d = lax.dot_general(a_ref[...], b_ref[...], dims, preferred_element_type=F32)

        def finish(r):
            outs = (r,) if epi is None else epi(r, *[e[...] for e in e_refs])
            for o_ref, o in zip(o_refs, outs):
                o_ref[...] = o.astype(o_ref.dtype)

        if nk == 1:
            finish(d)
        else:
            acc = refs[-1]

            @pl.when(k == 0)
            def _():
                acc[...] = d

            @pl.when((k > 0) & (k < nk - 1))
            def _():
                acc[...] += d

            @pl.when(k == nk - 1)
            def _():
                finish(acc[...] + d)
        after()

    if mode == "tn":
        a_spec = pl.BlockSpec((tk, tm), lambda i, j, k: (k, i))
    else:
        a_spec = pl.BlockSpec((tm, tk), lambda i, j, k: (i, k))
    if mode == "nt":
        b_spec = pl.BlockSpec((tn, tk), lambda i, j, k: (j, k))
    else:
        b_spec = pl.BlockSpec((tk, tn), lambda i, j, k: (k, j))
    mn_spec = pl.BlockSpec((tm, tn), lambda i, j, k: (i, j))
    outs = _pcall(
        kern, name=name, grid=(nm, nn, nk),
        in_specs=[a_spec, b_spec] + [mn_spec] * ne + r_ispec,
        out_specs=[mn_spec] * no + r_ospec,
        out_shape=[jax.ShapeDtypeStruct((M, N), dt) for dt in out_dtypes] + r_oshape,
        scratch_shapes=r_scratch + ([pltpu.VMEM((tm, tn), F32)] if nk > 1 else []),
        compiler_params=_cparams(("arbitrary",) * 3 if ride else ("parallel", "parallel", "arbitrary")),
    )(a, b, *extras, *r_in)
    return outs[0] if len(outs) == 1 else outs


def _rowcall(name, body, S, ts, row_ins, vec_ins, row_outs, vec_outs, into=None):
    ts = min(ts, S)
    nri, nvi, nro, nvo = len(row_ins), len(vec_ins), len(row_outs), len(vec_outs)
    na = 0 if into is None else 1

    def kern(*refs):
        ins = refs[:nri + nvi]
        outs = refs[nri + nvi + na:]
        if nvo:
            @pl.when(pl.program_id(0) == 0)
            def _():
                for r in outs[nro:]:
                    r[...] = jnp.zeros(r.shape, r.dtype)
        body(*ins, *outs)

    in_specs = [pl.BlockSpec((ts, w), functools.partial(lambda i, cb: (i, cb), cb=cb))
                for (_, w, cb) in row_ins]
    in_specs += [pl.BlockSpec(v.shape, lambda i: (0, 0)) for v in vec_ins]
    out_specs = [pl.BlockSpec((ts, w), lambda i: (i, 0)) for (w, _) in row_outs]
    out_specs += [pl.BlockSpec((r, w), lambda i: (0, 0)) for (r, w) in vec_outs]
    out_shape = [jax.ShapeDtypeStruct((S, w), dt) for (w, dt) in row_outs]
    out_shape += [jax.ShapeDtypeStruct((r, w), F32) for (r, w) in vec_outs]
    extra, aliases = [], {}
    if into is not None:
        buf, cb = into
        assert buf.dtype == row_outs[0][1] and buf.shape[0] == S
        in_specs.append(ANY)
        out_specs[0] = pl.BlockSpec((ts, row_outs[0][0]), lambda i: (i, cb))
        out_shape[0] = jax.ShapeDtypeStruct(buf.shape, buf.dtype)
        extra, aliases = [buf], {nri + nvi: 0}
    return _pcall(
        kern, name=name, grid=(S // ts,), in_specs=in_specs, out_specs=out_specs,
        out_shape=out_shape, input_output_aliases=aliases,
        compiler_params=_cparams(("arbitrary",) if nvo else ("parallel",)),
    )(*[a for (a, _, _) in row_ins], *vec_ins, *extra)


def _csum(x):
    return jnp.sum(x, axis=0, keepdims=True)


def _norm_mod(name, x, g, sc, sh, S, D):
    def body(x_ref, g_ref, sc_ref, sh_ref, h_ref):
        xv = x_ref[...]
        r = lax.rsqrt(jnp.mean(xv * xv, axis=-1, keepdims=True) + NORM_EPS)
        h_ref[...] = ((xv * r * g_ref[...]) * (1.0 + sc_ref[...]) + sh_ref[...]).astype(BF16)
    return _rowcall(name, body, S, 512, [(x, D, 0)], [g, sc, sh], [(D, BF16)], [])[0]


def _head_rstd(v, grp, grp_t, hd):
    ss = _dot_rs(v * v, grp, 1) * (1.0 / hd)
    r = lax.rsqrt(ss + NORM_EPS)
    return _dot_rs(r, grp_t, 2)


def _qk_prep(proj, vcol, gq, gk, grp, grp_t, S, D, hd):
    scale = hd ** -0.5

    def body(q_ref, k_ref, v_ref, gq_ref, gk_ref, g_ref, gt_ref, qs_ref, kn_ref, vb_ref):
        q = q_ref[...]
        k = k_ref[...]
        rq = _head_rstd(q, g_ref[...], gt_ref[...], hd)
        rk = _head_rstd(k, g_ref[...], gt_ref[...], hd)
        qs_ref[...] = ((q * rq * gq_ref[...]).astype(BF16).astype(F32) * scale).astype(BF16)
        kn_ref[...] = (k * rk * gk_ref[...]).astype(BF16)
        vb_ref[...] = v_ref[...].astype(BF16)

    return _rowcall("qk_prep", body, S, 256, [(proj, D, 0), (proj, D, 1), (proj, D, vcol)],
                    [gq, gk, grp, grp_t], [(D, BF16)] * 3, [])


def _fgate_fwd(proj, fcol, bf_pad, tri, S):
    ch = tri.shape[0]

    def body(f_ref, b_ref, tri_ref, out_ref):
        carry = jnp.zeros((1, LANES), F32)
        for c in range(S // ch):
            z = f_ref[c * ch:(c + 1) * ch, :] + b_ref[...]
            lf = jnp.minimum(z, 0.0) - jnp.log(1.0 + jnp.exp(-jnp.abs(z)))
            out_ref[c * ch:(c + 1) * ch, :] = _dot_ls(tri_ref[...], lf) + carry
            carry = carry + _csum(lf)

    return _rowcall("fgate_fwd", body, S, S, [(proj, LANES, fcol)], [bf_pad, tri],
                    [(LANES, F32)], [])[0]


def _fgate_bwd(dfk, dfq, proj, fcol, bf_pad, tri_u, nh, S, fw, into):
    ch = tri_u.shape[0]

    def body(d_ref, dq_ref, f_ref, b_ref, tri_ref, df_ref, db_ref):
        if fw > LANES:
            df_ref[:, LANES:fw] = jnp.zeros((S, fw - LANES), BF16)
        lane = lax.broadcasted_iota(jnp.int32, (ch, LANES), 1)
        carry = jnp.zeros((1, LANES), F32)
        tot = jnp.zeros((1, LANES), F32)
        for c in reversed(range(S // ch)):
            d = d_ref[c * ch:(c + 1) * ch, :] + dq_ref[c * ch:(c + 1) * ch, :]
            rc = _dot_ls(tri_ref[...], d) + carry
            carry = carry + _csum(d)
            z = f_ref[c * ch:(c + 1) * ch, :] + b_ref[...]
            df = jnp.where(lane < nh, rc * _sig(-z), 0.0)
            df_ref[c * ch:(c + 1) * ch, 0:LANES] = df.astype(BF16)
            tot = tot + _csum(df)
        db_ref[...] += tot

    return _rowcall("fgate_bwd", body, S, S, [(dfk, LANES, 0), (dfq, LANES, 0), (proj, LANES, fcol)],
                    [bf_pad, tri_u], [(fw, BF16)], [(1, LANES)], into=into)


def _keep(v, mask):
    return jnp.where(mask, v.astype(F32), 0.0).astype(BF16)


def _lane_col(blk, lane, at):
    return jnp.sum(jnp.where(lane == at, blk, 0.0), axis=-1, keepdims=True)


def _flash_fwd(qs, kn, vb, fk_r, S, D, hd, tq, ride=None):
    hp, nq = D // LANES, S // tq
    r_in, r_ispec, r_ospec, r_oshape, r_scratch, r_hook = _ride(ride, 4, 3)

    def kern(*refs):
        q_ref, k_ref, v_ref, fk_ref = refs[:4]
        o_ref, o32_ref, lse_ref = refs[4 + len(r_in):7 + len(r_in)]
        hi, qi = pl.program_id(0), pl.program_id(1)
        before, after = r_hook(refs, (hi == 0) & (qi == 0), (hi == hp // 2) & (qi == 0),
                               (hi == hp - 1) & (qi == nq - 1))
        before()
        lane = lax.broadcasted_iota(jnp.int32, (tq, LANES), 1)
        row = lax.broadcasted_iota(jnp.int32, (tq, tq), 0)
        col = lax.broadcasted_iota(jnp.int32, (tq, tq), 1)
        hms = [(lane >= j * hd) & (lane < (j + 1) * hd) for j in range(2)]
        q = q_ref[...]
        qms = [_keep(q, hm) for hm in hms]

        s_a, s_b = refs[-2], refs[-1]

        def put(s_ref, ki):
            off = pl.multiple_of(ki * tq, tq)
            k = k_ref[pl.ds(off, tq), :]
            for j in range(2):
                s_ref[j] = lax.dot_general(qms[j], k, (((1,), (1,)), ((), ())), preferred_element_type=F32)

        def update(ki, s_ref, state, masked):
            off = pl.multiple_of(ki * tq, tq)
            v = v_ref[pl.ds(off, tq), :].astype(F32)
            new = []
            for j in range(2):
                m_old, acc = state[j]
                s = s_ref[j] - fk_ref[j, ki]
                if masked:
                    s = jnp.where(col <= row, s, NEG)
                m_new = jnp.maximum(m_old, jnp.max(s, axis=-1, keepdims=True))
                alpha = jnp.exp(m_old - m_new)
                p = jnp.exp(s - m_new)
                v1 = jnp.where(hms[j], v, 1.0).astype(BF16)
                acc = alpha * acc + jnp.dot(p.astype(BF16), v1, preferred_element_type=F32)
                new.append((m_new, acc))
            return tuple(new)

        def pair(p, state):
            put(s_b, 2 * p + 1)
            state = update(2 * p, s_a, state, False)
            put(s_a, 2 * p + 2)
            return update(2 * p + 1, s_b, state, False)

        def odd_tail(state):
            put(s_b, qi)
            return update(qi, s_b, update(qi - 1, s_a, state, False), True)

        init = tuple((jnp.full((tq, 1), NEG, F32), jnp.zeros((tq, LANES), F32)) for _ in range(2))
        put(s_a, 0)
        state = lax.fori_loop(0, qi // 2, pair, init)
        (m0, a0), (m1, a1) = lax.cond(qi % 2 == 1, odd_tail, lambda st: update(qi, s_a, st, True), state)
        l0, l1 = pltpu.roll(a0, hd, 1), pltpu.roll(a1, hd, 1)
        first = lane < hd
        ov = jnp.where(first, a0 / l0, a1 / l1)
        o_ref[...] = ov.astype(BF16)
        o32_ref[...] = ov
        lse_ref[...] = jnp.where(first, m0 + jnp.log(l0), m1 + jnp.log(l1))
        after()

    qspec = pl.BlockSpec((tq, LANES), lambda h, i: (i, h))
    fullspec = pl.BlockSpec((S, LANES), lambda h, i: (0, h))
    return _pcall(
        kern, name="flash_fwd", grid=(hp, nq),
        in_specs=[qspec, fullspec, fullspec,
                  pl.BlockSpec((2, nq, 1, tq), lambda h, i: (h, 0, 0, 0))] + r_ispec,
        out_specs=[qspec, qspec, qspec] + r_ospec,
        out_shape=[jax.ShapeDtypeStruct((S, D), BF16), jax.ShapeDtypeStruct((S, D), F32),
                   jax.ShapeDtypeStruct((S, D), F32)] + r_oshape,
        scratch_shapes=r_scratch + [pltpu.VMEM((2, tq, tq), F32)] * 2,
        compiler_params=_cparams(("arbitrary", "arbitrary")),
    )(qs, kn, vb, fk_r, *r_in)


def _flash_bwd(qs, kn, vb, do, fk_b, lse_r, delta_r, S, D, hd, tq, dv_into, ride=None):
    hp, nq = D // LANES, S // tq
    dbuf_hbm, dv_col = dv_into
    r_in, r_ispec, r_ospec, r_oshape, r_scratch, r_hook = _ride(ride, 8, 5)

    def kern(*refs):
        q_ref, do_ref, k_ref, v_ref, fk_ref, lse_ref, dl_ref = refs[:7]
        dk_ref, dv_ref, dq_ref, dfq_ref, dfk_ref = refs[8 + len(r_in):13 + len(r_in)]
        hi, ki = pl.program_id(0), pl.program_id(1)
        before, after = r_hook(refs, (hi == 0) & (ki == 0), (hi == hp // 2) & (ki == 0),
                               (hi == hp - 1) & (ki == nq - 1))
        before()
        lane = lax.broadcasted_iota(jnp.int32, (tq, LANES), 1)
        row = lax.broadcasted_iota(jnp.int32, (tq, tq), 0)
        col = lax.broadcasted_iota(jnp.int32, (tq, tq), 1)
        hms = [(lane >= j * hd) & (lane < (j + 1) * hd) for j in range(2)]
        k = k_ref[...]
        v = v_ref[...]
        fkb = fk_ref[...]
        kms = [_keep(k, hm) for hm in hms]
        vms = [_keep(v, hm) for hm in hms]
        fks = [_lane_col(fkb, lane, 2 * hi + j) for j in range(2)]

        @pl.when(ki == 0)
        def _():
            dfq_ref[...] = jnp.zeros(dfq_ref.shape, F32)
            dq_ref[...] = jnp.zeros(dq_ref.shape, F32)

        def step(qi, acc, masked):
            dk, dv, dfs = acc
            off = pl.multiple_of(qi * tq, tq)
            q = q_ref[pl.ds(off, tq), :]
            g = do_ref[pl.ds(off, tq), :]
            dq = None
            new_dfs = []
            for j in range(2):
                qm = _keep(q, hms[j])
                gm = _keep(g, hms[j])
                st = lax.dot_general(kms[j], q, (((1,), (1,)), ((), ())), preferred_element_type=F32)
                st = st - fks[j]
                if masked:
                    st = jnp.where(row <= col, st, NEG)
                pt = jnp.exp(st - lse_ref[j, qi])
                dv = dv + jnp.dot(pt.astype(BF16), gm, preferred_element_type=F32)
                dpt = lax.dot_general(vms[j], g, (((1,), (1,)), ((), ())), preferred_element_type=F32)
                dst = pt * (dpt - dl_ref[j, qi])
                dsb = dst.astype(BF16)
                dk = dk + jnp.dot(dsb, qm, preferred_element_type=F32)
                t = lax.dot_general(dsb, kms[j], (((0,), (0,)), ((), ())), preferred_element_type=F32)
                dq = t if dq is None else dq + t
                dfq_ref[j, qi] += jnp.sum(dst, axis=0, keepdims=True)
                new_dfs.append(dfs[j] - jnp.sum(dst, axis=1, keepdims=True))
            dq_ref[pl.ds(off, tq), :] += dq
            return dk, dv, tuple(new_dfs)

        zero = jnp.zeros((tq, LANES), F32)
        zcol = jnp.zeros((tq, 1), F32)
        acc = step(ki, (zero, zero, (zcol, zcol)), True)
        dk, dv, dfs = lax.fori_loop(ki + 1, nq, lambda qi, a: step(qi, a, False), acc)
        dk_ref[...] = dk.astype(BF16)
        dv_ref[...] = dv.astype(BF16)
        dfk_ref[...] = jnp.where(lane < hd, dfs[0], dfs[1])
        after()

    kspec = pl.BlockSpec((tq, LANES), lambda h, i: (i, h))
    fullspec = pl.BlockSpec((S, LANES), lambda h, i: (0, h))
    rowspec = pl.BlockSpec((2, nq, 1, tq), lambda h, i: (h, 0, 0, 0))
    return _pcall(
        kern, name="flash_bwd", grid=(hp, nq),
        in_specs=[fullspec, fullspec, kspec, kspec, pl.BlockSpec((tq, LANES), lambda h, i: (i, 0)),
                  rowspec, rowspec, ANY] + r_ispec,
        out_specs=[kspec, pl.BlockSpec((tq, LANES), lambda h, i: (i, h + dv_col)), fullspec, rowspec, kspec]
        + r_ospec,
        out_shape=[jax.ShapeDtypeStruct((S, D), BF16), jax.ShapeDtypeStruct(dbuf_hbm.shape, BF16),
                   jax.ShapeDtypeStruct((S, D), F32), jax.ShapeDtypeStruct((2 * hp, nq, 1, tq), F32),
                   jax.ShapeDtypeStruct((S, D), F32)] + r_oshape,
        scratch_shapes=r_scratch, input_output_aliases={7: 1},
        compiler_params=_cparams(("arbitrary", "arbitrary")),
    )(qs, do, kn, vb, fk_b, lse_r, delta_r, dbuf_hbm, *r_in)


def _delta_prep(do, o, lse_b, grp, sel, S, D):
    def body(g_ref, o_ref, l_ref, e_ref, s_ref, dl_ref, lse_ref):
        prod = g_ref[...].astype(F32) * o_ref[...]
        dl_ref[...] = _dot_rs(prod, e_ref[...], 2)
        lse_ref[...] = _dot_rs(l_ref[...], s_ref[...])
    return _rowcall("delta_prep", body, S, 256, [(do, D, 0), (o, D, 0), (lse_b, D, 0)], [grp, sel],
                    [(LANES, F32), (LANES, F32)], [])


def _pick_heads(x_b, sel, S, D):
    def body(x_ref, s_ref, o_ref):
        o_ref[...] = _dot_rs(x_ref[...], s_ref[...])
    return _rowcall("pick_heads", body, S, 512, [(x_b, D, 0)], [sel], [(LANES, F32)], [])[0]


def _qk_bwd(proj, dqs, dkn, gq, gk, grp, grp_t, S, D, hd, into):
    scale = hd ** -0.5

    def one(x, dn, gain, e, et):
        r = _head_rstd(x, e, et, hd)
        xh = x * r
        t = dn * gain
        mean = _dot_rs(_dot_rs(t * xh, e, 1), et, 2) * (1.0 / hd)
        return r * (t - xh * mean), _csum(dn * xh)

    def body(q_ref, k_ref, dq_ref, dk_ref, gq_ref, gk_ref, e_ref, et_ref, o_ref, sq_ref, sk_ref):
        e, et = e_ref[...], et_ref[...]
        dq, sq = one(q_ref[...], dq_ref[...].astype(F32) * scale, gq_ref[...], e, et)
        dk, sk = one(k_ref[...], dk_ref[...].astype(F32), gk_ref[...], e, et)
        o_ref[:, 0:D] = dq.astype(BF16)
        o_ref[:, D:2 * D] = dk.astype(BF16)
        sq_ref[...] += sq
        sk_ref[...] += sk

    return _rowcall("qk_bwd", body, S, 256,
                    [(proj, D, 0), (proj, D, 1), (dqs, D, 0), (dkn, D, 0)],
                    [gq, gk, grp, grp_t], [(2 * D, BF16)], [(1, D)] * 2, into=into)


def _shift_copies(buf, sh, ts):
    for b in range(1, SUBLANES):
        sh[b - 1] = buf[b:b + ts + HALO - SUBLANES, :]


def _rows_from(buf, sh, o, ts):
    a, b = divmod(o, SUBLANES)
    if b == 0:
        return buf[o:o + ts, :]
    return sh[b - 1, SUBLANES * a:SUBLANES * a + ts, :]


def _conv_fwd(proj, acol, bcol, w_pad, cb, lg, lb, S, C, taps, ts):
    ts = min(ts, S)

    def kern(a_ref, b_ref, w_ref, cb_ref, lg_ref, lb_ref, u1_ref, u3_ref, ubuf, ush):
        @pl.when(pl.program_id(0) == 0)
        def _():
            ubuf[0:HALO, :] = jnp.zeros((HALO, C), F32)

        ubuf[HALO:HALO + ts, :] = a_ref[...] * _sig(b_ref[...])
        _shift_copies(ubuf, ush, ts)
        acc = jnp.zeros((ts, C), F32) + cb_ref[...]
        for k in range(taps):
            acc = acc + w_ref[k:k + 1, :] * _rows_from(ubuf, ush, HALO - (taps - 1) + k, ts)
        u1_ref[...] = acc
        mu = jnp.mean(acc, axis=-1, keepdims=True)
        xc = acc - mu
        rstd = lax.rsqrt(jnp.mean(xc * xc, axis=-1, keepdims=True) + NORM_EPS)
        u2 = xc * rstd * lg_ref[...] + lb_ref[...]
        u3_ref[...] = (u2 * _sig(u2)).astype(BF16)
        ubuf[0:HALO, :] = ubuf[ts:ts + HALO, :]

    vec = lambda a: pl.BlockSpec(a.shape, lambda i: (0, 0))
    return _pcall(
        kern, name="conv_fwd", grid=(S // ts,),
        in_specs=[pl.BlockSpec((ts, C), lambda i: (i, acol)), pl.BlockSpec((ts, C), lambda i: (i, bcol)),
                  vec(w_pad), vec(cb), vec(lg), vec(lb)],
        out_specs=[pl.BlockSpec((ts, C), lambda i: (i, 0))] * 2,
        out_shape=[jax.ShapeDtypeStruct((S, C), F32), jax.ShapeDtypeStruct((S, C), BF16)],
        scratch_shapes=[pltpu.VMEM((HALO + ts, C), F32),
                        pltpu.VMEM((SUBLANES - 1, HALO + ts - SUBLANES, C), F32)],
        compiler_params=_cparams(("arbitrary",)),
    )(proj, proj, w_pad, cb, lg, lb)


def _conv_bwd(du3, u1, proj, acol, bcol, w_pad, lg, lb, S, C, taps, ts, into, ride=None):
    ts = min(ts, S)
    dbuf_hbm, dcol = into
    r_in, r_ispec, r_ospec, r_oshape, r_scratch, r_hook = _ride(ride, 12, 5)
    nt = S // ts
    hb = ts // HALO

    def ln_bwd(g, u, lgv, lbv):
        mu = jnp.mean(u, axis=-1, keepdims=True)
        xc = u - mu
        rstd = lax.rsqrt(jnp.mean(xc * xc, axis=-1, keepdims=True) + NORM_EPS)
        xh = xc * rstd
        u2 = xh * lgv + lbv
        s = _sig(u2)
        du2 = g * (s + u2 * s * (1.0 - s))
        dxh = du2 * lgv
        du1 = rstd * (dxh - jnp.mean(dxh, axis=-1, keepdims=True)
                      - xh * jnp.mean(dxh * xh, axis=-1, keepdims=True))
        return du1, du2, xh

    def kern(*refs):
        g_ref, u_ref, a_ref, b_ref, gn_ref, un_ref, ap_ref, bp_ref, w_ref, lg_ref, lb_ref = refs[:11]
        dg_ref, dw_ref, dcb_ref, dlg_ref, dlb_ref = refs[12 + len(r_in):17 + len(r_in)]
        dbuf, ubuf, dsh, ush = refs[-4:]
        i = pl.program_id(0)
        before, after = r_hook(refs, i == 0, i == nt // 2, i == nt - 1)
        before()

        @pl.when(i == 0)
        def _():
            dw_ref[...] = jnp.zeros(dw_ref.shape, F32)
            dcb_ref[...] = jnp.zeros(dcb_ref.shape, F32)
            dlg_ref[...] = jnp.zeros(dlg_ref.shape, F32)
            dlb_ref[...] = jnp.zeros(dlb_ref.shape, F32)

        lgv, lbv = lg_ref[...], lb_ref[...]
        du1, du2, xh = ln_bwd(g_ref[...], u_ref[...], lgv, lbv)
        dbuf[0:ts, :] = du1
        du1n, _, _ = ln_bwd(gn_ref[...], un_ref[...], lgv, lbv)
        dbuf[ts:ts + HALO, :] = jnp.where(i < nt - 1, du1n, 0.0)
        a = a_ref[...]
        sb = _sig(b_ref[...])
        ubuf[HALO:HALO + ts, :] = a * sb
        ubuf[0:HALO, :] = jnp.where(i > 0, ap_ref[...] * _sig(bp_ref[...]), 0.0)
        dcb_ref[...] += _csum(du1)
        dlg_ref[...] += _csum(du2 * xh)
        dlb_ref[...] += _csum(du2)
        _shift_copies(dbuf, dsh, ts)
        _shift_copies(ubuf, ush, ts)
        for r0 in range(0, ts, CONV_ROWS):
            du0 = jnp.zeros((CONV_ROWS, C), F32)
            for k in range(taps):
                du0 = du0 + w_ref[k:k + 1, :] * _rows_from(dbuf, dsh, r0 + taps - 1 - k, CONV_ROWS)
            ac = a_ref[r0:r0 + CONV_ROWS, :]
            sc = _sig(b_ref[r0:r0 + CONV_ROWS, :])
            dg_ref[r0:r0 + CONV_ROWS, 0:C] = (du0 * sc).astype(BF16)
            dg_ref[r0:r0 + CONV_ROWS, C:2 * C] = (du0 * ac * sc * (1.0 - sc)).astype(BF16)
        for k0 in range(0, taps, CONV_TAPS):
            ks = range(k0, min(k0 + CONV_TAPS, taps))
            accs = [jnp.zeros((SUBLANES, C), F32) for _ in ks]
            for r0 in range(0, ts, CONV_ROWS):
                d = dbuf[r0:r0 + CONV_ROWS, :]
                for t, k in enumerate(ks):
                    prod = d * _rows_from(ubuf, ush, r0 + HALO - (taps - 1) + k, CONV_ROWS)
                    accs[t] = accs[t] + jnp.sum(prod.reshape(CONV_ROWS // SUBLANES, SUBLANES, C), axis=0)
            for t, k in enumerate(ks):
                dw_ref[k:k + 1, :] += _csum(accs[t])
        after()

    vec = lambda a: pl.BlockSpec(a.shape, lambda i: (0, 0))
    tile = lambda cb: pl.BlockSpec((ts, C), functools.partial(lambda i, cb: (i, cb), cb=cb))
    nxt = lambda cb: pl.BlockSpec(
        (HALO, C), functools.partial(lambda i, cb: (jnp.minimum((i + 1) * hb, nt * hb - 1), cb), cb=cb))
    prv = lambda cb: pl.BlockSpec(
        (HALO, C), functools.partial(lambda i, cb: (jnp.maximum(i * hb - 1, 0), cb), cb=cb))
    return _pcall(
        kern, name="conv_bwd", grid=(nt,),
        in_specs=[tile(0), tile(0), tile(acol), tile(bcol), nxt(0), nxt(0), prv(acol), prv(bcol),
                  vec(w_pad), vec(lg), vec(lb), ANY] + r_ispec,
        out_specs=[pl.BlockSpec((ts, 2 * C), lambda i: (i, dcol))]
        + [pl.BlockSpec(w_pad.shape, lambda i: (0, 0))] + [pl.BlockSpec((1, C), lambda i: (0, 0))] * 3 + r_ospec,
        out_shape=[jax.ShapeDtypeStruct(dbuf_hbm.shape, BF16)]
        + [jax.ShapeDtypeStruct(w_pad.shape, F32)] + [jax.ShapeDtypeStruct((1, C), F32)] * 3 + r_oshape,
        scratch_shapes=r_scratch + [pltpu.VMEM((ts + HALO, C), F32), pltpu.VMEM((HALO + ts, C), F32)]
        + [pltpu.VMEM((SUBLANES - 1, HALO + ts - SUBLANES, C), F32)] * 2,
        input_output_aliases={11: 0},
        compiler_params=_cparams(("arbitrary",)),
    )(du3, u1, proj, proj, du3, u1, proj, proj, w_pad, lg, lb, dbuf_hbm, *r_in)


def _gate_merge(proj, gacol, gbcol, ba, bb, S, D):
    def body(ga_ref, gb_ref, a_ref, b_ref, out_ref):
        out_ref[...] = (_sig(ga_ref[...]) * a_ref[...] + _sig(gb_ref[...]) * b_ref[...]).astype(BF16)
    return _rowcall("gate_merge", body, S, 512,
                    [(proj, D, gacol), (proj, D, gbcol), (ba, D, 0), (bb, D, 0)], [], [(D, BF16)], [])[0]


def _gate_bwd(dm, proj, gacol, gbcol, ba, bb, S, D, into):
    def body(dm_ref, ga_ref, gb_ref, a_ref, b_ref, dg_ref, da_ref, db_ref):
        dmv = dm_ref[...]
        sa, sb = _sig(ga_ref[...]), _sig(gb_ref[...])
        da_ref[...] = (dmv * sa).astype(BF16)
        db_ref[...] = (dmv * sb).astype(BF16)
        dg_ref[:, 0:D] = (dmv * a_ref[...] * sa * (1.0 - sa)).astype(BF16)
        dg_ref[:, D:2 * D] = (dmv * b_ref[...] * sb * (1.0 - sb)).astype(BF16)
    return _rowcall("gate_bwd", body, S, 512,
                    [(dm, D, 0), (proj, D, gacol), (proj, D, gbcol), (ba, D, 0), (bb, D, 0)], [],
                    [(2 * D, BF16), (D, BF16), (D, BF16)], [], into=into)


def _resid_norm2(x, mo, g1, g, sc, sh, S, D):
    def body(x_ref, mo_ref, g1_ref, g_ref, sc_ref, sh_ref, x1_ref, h_ref):
        x1 = x_ref[...] + g1_ref[...] * mo_ref[...]
        x1_ref[...] = x1
        r = lax.rsqrt(jnp.mean(x1 * x1, axis=-1, keepdims=True) + NORM_EPS)
        h_ref[...] = ((x1 * r * g_ref[...]) * (1.0 + sc_ref[...]) + sh_ref[...]).astype(BF16)
    return _rowcall("resid_norm2", body, S, 512, [(x, D, 0), (mo, D, 0)], [g1, g, sc, sh],
                    [(D, F32), (D, BF16)], [])


def _loss_dy(x1, ml, tgt, g2, S, D):
    def body(x1_ref, ml_ref, t_ref, g2_ref, dy_ref, dml_ref, sq_ref, dg2_ref):
        mlv = ml_ref[...]
        diff = x1_ref[...] + g2_ref[...] * mlv - t_ref[...]
        dy = diff * (1.0 / D)
        dy_ref[...] = dy
        dml_ref[...] = (dy * g2_ref[...]).astype(BF16)
        sq_ref[...] += _csum(diff * diff)
        dg2_ref[...] += _csum(dy * mlv)
    return _rowcall("loss_dy", body, S, 512, [(x1, D, 0), (ml, D, 0), (tgt, D, 0)], [g2],
                    [(D, F32), (D, BF16)], [(1, D), (1, D)])


def _norm_bwd(name, xin, dh, dres, g, sc, S, D, extra=None):
    def body(*refs):
        if extra is None:
            x_ref, dh_ref, dr_ref, g_ref, sc_ref, dx_ref, dsh_ref, dsc_ref, dg_ref = refs
        else:
            (x_ref, dh_ref, dr_ref, mo_ref, g_ref, sc_ref, g1_ref,
             dx_ref, dmo_ref, dsh_ref, dsc_ref, dg_ref, dg1_ref) = refs
        xv, dhv, gv = x_ref[...], dh_ref[...], g_ref[...]
        r = lax.rsqrt(jnp.mean(xv * xv, axis=-1, keepdims=True) + NORM_EPS)
        xh = xv * r
        dsh_ref[...] += _csum(dhv)
        dsc_ref[...] += _csum(dhv * xh * gv)
        dxg = dhv * (1.0 + sc_ref[...])
        dg_ref[...] += _csum(dxg * xh)
        dxh = dxg * gv
        dx = dr_ref[...] + r * (dxh - xh * jnp.mean(dxh * xh, axis=-1, keepdims=True))
        dx_ref[...] = dx
        if extra is not None:
            dmo_ref[...] = (dx * g1_ref[...]).astype(BF16)
            dg1_ref[...] += _csum(dx * mo_ref[...])

    rows = [(xin, D, 0), (dh, D, 0), (dres, D, 0)]
    vecs = [g, sc]
    if extra is None:
        return _rowcall(name, body, S, 512, rows, vecs, [(D, F32)], [(1, D)] * 3)
    return _rowcall(name, body, S, 512, rows + [(extra[0], D, 0)], vecs + [extra[1]],
                    [(D, F32), (D, BF16)], [(1, D)] * 4)


def _ada_fwd(c_all, w, b_part):
    B, D = c_all.shape
    N = w.shape[1]
    tn = min(512, N)

    def kern(c_ref, w_ref, b_ref, o_ref):
        cv = c_ref[...]
        ca = cv * _sig(cv)
        o_ref[...] = jnp.dot(ca, w_ref[...], precision=lax.Precision.HIGHEST,
                             preferred_element_type=F32) + b_ref[...]

    return _pcall(
        kern, name="ada_fwd", grid=(N // tn,),
        in_specs=[pl.BlockSpec((B, D), lambda j: (0, 0)), pl.BlockSpec((D, tn), lambda j: (0, j)),
                  pl.BlockSpec((1, tn), lambda j: (0, j))],
        out_specs=pl.BlockSpec((B, tn), lambda j: (0, j)),
        out_shape=jax.ShapeDtypeStruct((B, N), F32),
        compiler_params=_cparams(("parallel",)),
    )(c_all, w, b_part)


def _ada_wgrad(c_t_pad, dmod_pad):
    D = c_t_pad.shape[0]
    N = dmod_pad.shape[1]
    tn = min(512, N)

    def kern(c_ref, d_ref, o_ref):
        cv = c_ref[...]
        ca = cv * _sig(cv)
        o_ref[...] = jnp.dot(ca, d_ref[...], precision=lax.Precision.HIGHEST,
                             preferred_element_type=F32)

    return _pcall(
        kern, name="ada_wgrad", grid=(N // tn,),
        in_specs=[pl.BlockSpec((D, LANES), lambda j: (0, 0)), pl.BlockSpec((LANES, tn), lambda j: (0, j))],
        out_specs=pl.BlockSpec((D, tn), lambda j: (0, j)),
        out_shape=jax.ShapeDtypeStruct((D, N), F32),
        compiler_params=_cparams(("parallel",)),
    )(c_t_pad, dmod_pad)


def _ag_small(name, arrs):
    n = len(arrs)

    def kern(*refs):
        ins, outs = refs[:n], refs[n:2 * n]
        send, recv = refs[2 * n], refs[2 * n + 1]
        x, y, c = lax.axis_index("x"), lax.axis_index("y"), lax.axis_index("c")
        me = 4 * x + 2 * y + c

        def copy(i, m, slot):
            peer = (x ^ ((m >> 2) & 1), y ^ ((m >> 1) & 1), c ^ (m & 1))
            return pltpu.make_async_remote_copy(
                src_ref=ins[i], dst_ref=outs[i].at[slot],
                send_sem=send.at[i * 7 + m - 1], recv_sem=recv.at[i * 7 + m - 1],
                device_id=peer, device_id_type=MESH)

        for i in range(n):
            outs[i][me] = ins[i][...]
            for m in range(1, 8):
                copy(i, m, me).start()
        for i in range(n):
            for m in range(1, 8):
                copy(i, m, me).wait_send()
                copy(i, m, me ^ m).wait_recv()

    vm = pl.BlockSpec(memory_space=pltpu.VMEM)
    return _pcall(
        kern, name=name, in_specs=[vm] * n, out_specs=[vm] * n,
        out_shape=[jax.ShapeDtypeStruct((8,) + a.shape, a.dtype) for a in arrs],
        scratch_shapes=[pltpu.SemaphoreType.DMA((7 * n,)), pltpu.SemaphoreType.DMA((7 * n,))],
        compiler_params=pltpu.CompilerParams(has_side_effects=True),
    )(*arrs)


def _exchange(name, arrs, plan):
    out_shape, scratch, phases = plan(arrs)

    def kern(*refs):
        for phase in phases(refs):
            phase()

    return _pcall(
        kern, name=name, in_specs=[ANY] * len(arrs), out_specs=[ANY] * len(out_shape),
        out_shape=out_shape, scratch_shapes=scratch,
        compiler_params=pltpu.CompilerParams(has_side_effects=True),
    )(*arrs)


def _ride(plan_and_arrs, n_in, n_out):
    if plan_and_arrs is None:
        return [], [], [], [], [], lambda refs, first, middle, last: ((lambda: None), (lambda: None))
    plan, arrs = plan_and_arrs
    out_shape, scratch, phases = plan(arrs)
    na, no = len(arrs), len(out_shape)

    def hook(refs, first, middle, last):
        mine = refs[n_in:n_in + na] + refs[n_in + na + n_out:]
        start, mid, finish = phases(mine)

        def before():
            pl.when(first)(start)
            pl.when(middle)(mid)

        def after():
            pl.when(last)(finish)

        return before, after

    return list(arrs), [ANY] * na, [ANY] * no, out_shape, scratch, hook


def _gather_plan(arrs):
    n = len(arrs)

    def phases(refs):
        ins, outs = refs[:n], refs[n:2 * n]
        s1, r1, s2, r2, loc = refs[2 * n:2 * n + 5]
        x, y, c = lax.axis_index("x"), lax.axis_index("y"), lax.axis_index("c")
        me = 2 * x + y

        def half(i, hc):
            hr = ins[i].shape[0] // 2
            return pl.ds(hc * hr, hr)

        def own(i):
            return pltpu.make_async_remote_copy(
                src_ref=ins[i], dst_ref=outs[i].at[me], send_sem=loc.at[i], recv_sem=loc.at[n + i],
                device_id=(x, y, 1 - c), device_id_type=MESH)

        def fetch(i, m, slot):
            px, py = x ^ ((m >> 1) & 1), y ^ (m & 1)
            return pltpu.make_async_remote_copy(
                src_ref=ins[i].at[half(i, c)], dst_ref=outs[i].at[slot, half(i, c)],
                send_sem=s1.at[i * 3 + m - 1], recv_sem=r1.at[i * 3 + m - 1],
                device_id=(px, py, c), device_id_type=MESH)

        def passed(i, m, hc):
            return pltpu.make_async_remote_copy(
                src_ref=outs[i].at[me ^ m, half(i, hc)], dst_ref=outs[i].at[me ^ m, half(i, hc)],
                send_sem=s2.at[i * 3 + m - 1], recv_sem=r2.at[i * 3 + m - 1],
                device_id=(x, y, 1 - c), device_id_type=MESH)

        def start():
            for i in range(n):
                for m in range(1, 4):
                    fetch(i, m, me).start()
            for i in range(n):
                own(i).start()

        def mid():
            for i in range(n):
                for m in range(1, 4):
                    fetch(i, m, me ^ m).wait_recv()
                    passed(i, m, c).start()

        def finish():
            for i in range(n):
                own(i).wait()
                for m in range(1, 4):
                    fetch(i, m, me).wait_send()
                    passed(i, m, c).wait_send()
                    passed(i, m, 1 - c).wait_recv()

        return start, mid, finish

    out_shape = [jax.ShapeDtypeStruct((4,) + a.shape, a.dtype) for a in arrs]
    scratch = [pltpu.SemaphoreType.DMA((3 * n,))] * 4 + [pltpu.SemaphoreType.DMA((2 * n,))]
    return out_shape, scratch, phases


def _pair_halves_plan(arrs):
    n = len(arrs)

    def phases(refs):
        ins, outs = refs[:n], refs[n:2 * n]
        send, recv = refs[2 * n], refs[2 * n + 1]
        x, y, c = lax.axis_index("x"), lax.axis_index("y"), lax.axis_index("c")

        def copy(i, k, hc):
            return pltpu.make_async_remote_copy(
                src_ref=ins[i].at[k, hc], dst_ref=outs[i].at[k],
                send_sem=send.at[i * 4 + k], recv_sem=recv.at[i * 4 + k],
                device_id=(x, y, 1 - c), device_id_type=MESH)

        def start():
            for i in range(n):
                for k in range(4):
                    copy(i, k, 1 - c).start()

        def finish():
            for i in range(n):
                for k in range(4):
                    copy(i, k, 1 - c).wait()

        return start, (lambda: None), finish

    out_shape = [jax.ShapeDtypeStruct((4,) + a.shape[2:], a.dtype) for a in arrs]
    scratch = [pltpu.SemaphoreType.DMA((4 * n,)), pltpu.SemaphoreType.DMA((4 * n,))]
    return out_shape, scratch, phases


def _scatter_plan(arrs):
    n = len(arrs)

    def phases(refs):
        ins, outs = refs[:n], refs[n:2 * n]
        send, recv = refs[2 * n], refs[2 * n + 1]
        x, y, c = lax.axis_index("x"), lax.axis_index("y"), lax.axis_index("c")
        me = 2 * x + y

        def copy(i, m, slot):
            px, py = x ^ ((m >> 1) & 1), y ^ (m & 1)
            return pltpu.make_async_remote_copy(
                src_ref=ins[i].at[2 * px + py], dst_ref=outs[i].at[slot],
                send_sem=send.at[i * 3 + m - 1], recv_sem=recv.at[i * 3 + m - 1],
                device_id=(px, py, c), device_id_type=MESH)

        def start():
            for i in range(n):
                for m in range(1, 4):
                    copy(i, m, me).start()

        def finish():
            for i in range(n):
                for m in range(1, 4):
                    copy(i, m, me).wait_send()
                    copy(i, m, me ^ m).wait_recv()

        return start, (lambda: None), finish

    out_shape = [jax.ShapeDtypeStruct(a.shape, a.dtype) for a in arrs]
    scratch = [pltpu.SemaphoreType.DMA((3 * n,)), pltpu.SemaphoreType.DMA((3 * n,))]
    return out_shape, scratch, phases


def _pair_swap_plan(arrs):
    n = len(arrs)

    def phases(refs):
        ins, outs = refs[:n], refs[n:2 * n]
        send, recv = refs[2 * n], refs[2 * n + 1]
        x, y, c = lax.axis_index("x"), lax.axis_index("y"), lax.axis_index("c")

        def copy(i):
            return pltpu.make_async_remote_copy(
                src_ref=ins[i], dst_ref=outs[i], send_sem=send.at[i], recv_sem=recv.at[i],
                device_id=(x, y, 1 - c), device_id_type=MESH)

        def start():
            for i in range(n):
                copy(i).start()

        def finish():
            for i in range(n):
                copy(i).wait()

        return start, (lambda: None), finish

    out_shape = [jax.ShapeDtypeStruct(a.shape, a.dtype) for a in arrs]
    scratch = [pltpu.SemaphoreType.DMA((n,)), pltpu.SemaphoreType.DMA((n,))]
    return out_shape, scratch, phases


def _row_tile(R):
    for t in (256, 128, 64, 32, 16, 8):
        if R % t == 0:
            return t
    return R


def _sum_slots(name, parts):
    K, R, C = parts.shape
    tr = _row_tile(R)

    def kern(p_ref, o_ref):
        acc = p_ref[0].astype(F32)
        for k in range(1, K):
            acc = acc + p_ref[k].astype(F32)
        o_ref[...] = acc

    return _pcall(
        kern, name=name, grid=(R // tr,),
        in_specs=[pl.BlockSpec((K, tr, C), lambda i: (0, i, 0))],
        out_specs=pl.BlockSpec((tr, C), lambda i: (i, 0)),
        out_shape=jax.ShapeDtypeStruct((R, C), F32),
        compiler_params=_cparams(("parallel",)),
    )(parts)


def _sum_pair(name, core, mine, theirs):
    K, _, hr, C = mine.shape
    tr = _row_tile(hr)

    def kern(c_ref, a_ref, b_ref, o_ref):
        o_ref[0] = (a_ref[0, 0].astype(F32) + b_ref[0].astype(F32)).astype(BF16)

    return _pcall(
        kern, name=name, out_shape=jax.ShapeDtypeStruct((K, hr, C), BF16),
        grid_spec=pltpu.PrefetchScalarGridSpec(
            num_scalar_prefetch=1, grid=(K, hr // tr),
            in_specs=[pl.BlockSpec((1, 1, tr, C), lambda k, r, c_ref: (k, c_ref[0], r, 0)),
                      pl.BlockSpec((1, tr, C), lambda k, r, c_ref: (k, r, 0))],
            out_specs=pl.BlockSpec((1, tr, C), lambda k, r, c_ref: (k, r, 0))),
        compiler_params=_cparams(("parallel", "parallel")),
    )(core, mine, theirs)


def _sum_chips(name, chip, own, recv):
    K, hr, C = own.shape
    tr = _row_tile(hr)

    def kern(chip_ref, own_ref, *rest):
        r_refs, o_ref = rest[:K], rest[K]
        me = chip_ref[0]
        mine = own_ref[0].astype(F32)
        acc = None
        for k in range(K):
            t = jnp.where(me == k, mine, r_refs[k][0].astype(F32))
            acc = t if acc is None else acc + t
        o_ref[...] = acc

    def other(k):
        return pl.BlockSpec((1, tr, C), lambda r, s: (jnp.where(s[0] == k, (k + 1) % K, k), r, 0))

    return _pcall(
        kern, name=name, out_shape=jax.ShapeDtypeStruct((hr, C), F32),
        grid_spec=pltpu.PrefetchScalarGridSpec(
            num_scalar_prefetch=1, grid=(hr // tr,),
            in_specs=[pl.BlockSpec((1, tr, C), lambda r, s: (s[0], r, 0))] + [other(k) for k in range(K)],
            out_specs=pl.BlockSpec((tr, C), lambda r, s: (r, 0))),
        compiler_params=_cparams(("parallel",)),
    )(chip, own, *([recv] * K))


def _adam_update(w, m, v, g):
    c1 = 1.0 - ADAM_B1 ** ADAM_STEP
    c2 = 1.0 - ADAM_B2 ** ADAM_STEP
    mn = ADAM_B1 * m + (1.0 - ADAM_B1) * g
    vn = ADAM_B2 * v + (1.0 - ADAM_B2) * (g * g)
    return -ADAM_LR * ((mn / c1) / (jnp.sqrt(vn / c2) + ADAM_EPS) + ADAM_WD * w), mn, vn


def _adamw_halves(name, core, w, m, v, mine, theirs):
    R, C = w.shape
    hr = mine.shape[0]
    tr = _row_tile(hr)
    nbh = hr // tr

    def kern(c_ref, w_ref, m_ref, v_ref, a_ref, b_ref, go_ref, d_ref, mo_ref, vo_ref):
        g = jnp.where(pl.program_id(0) // nbh == c_ref[0], a_ref[...], b_ref[...])
        d, mn, vn = _adam_update(w_ref[...], m_ref[...], v_ref[...], g)
        go_ref[...] = g
        d_ref[...] = d
        mo_ref[...] = mn
        vo_ref[...] = vn

    spec = pl.BlockSpec((tr, C), lambda i, s: (i, 0))
    hspec = pl.BlockSpec((tr, C), lambda i, s: (i % nbh, 0))
    return _pcall(
        kern, name=name, out_shape=[jax.ShapeDtypeStruct((R, C), F32)] * 4,
        grid_spec=pltpu.PrefetchScalarGridSpec(
            num_scalar_prefetch=1, grid=(R // tr,),
            in_specs=[spec, spec, spec, hspec, hspec], out_specs=[spec] * 4),
        compiler_params=_cparams(("parallel",)),
    )(core, w, m, v, mine, theirs)


def _adamw_halves_t(name, core, w_t, m_t, v_t, mine_t, theirs_t):
    C, R = w_t.shape
    hr = mine_t.shape[1]
    tc = min(256, hr)
    nbh = hr // tc

    def kern(c_ref, w_ref, m_ref, v_ref, a_ref, b_ref, go_ref, d_ref, mo_ref, vo_ref):
        g = jnp.where(pl.program_id(0) // nbh == c_ref[0], a_ref[...], b_ref[...])
        d, mn, vn = _adam_update(w_ref[...], m_ref[...], v_ref[...], g)
        go_ref[...] = g
        d_ref[...] = d
        mo_ref[...] = mn
        vo_ref[...] = vn

    spec = pl.BlockSpec((C, tc), lambda j, s: (0, j))
    hspec = pl.BlockSpec((C, tc), lambda j, s: (0, j % nbh))
    return _pcall(
        kern, name=name, out_shape=[jax.ShapeDtypeStruct((C, R), F32)] * 4,
        grid_spec=pltpu.PrefetchScalarGridSpec(
            num_scalar_prefetch=1, grid=(R // tc,),
            in_specs=[spec, spec, spec, hspec, hspec], out_specs=[spec] * 4),
        compiler_params=_cparams(("parallel",)),
    )(core, w_t, m_t, v_t, mine_t, theirs_t)


def _adamw_many(name, wmvg):
    n = len(wmvg[0])

    def kern(*refs):
        ins, outs = refs[:4 * n], refs[4 * n:]
        for j in range(n):
            g = ins[3 * n + j][...]
            d, mn, vn = _adam_update(ins[j][...], ins[n + j][...], ins[2 * n + j][...], g)
            for i, val in enumerate((g, d, mn, vn)):
                outs[i * n + j][...] = val

    vm = pl.BlockSpec(memory_space=pltpu.VMEM)
    flat = [a for group in wmvg for a in group]
    outs = _pcall(
        kern, name=name, in_specs=[vm] * (4 * n), out_specs=[vm] * (4 * n),
        out_shape=[jax.ShapeDtypeStruct(a.shape, F32) for _ in range(4) for a in wmvg[0]],
    )(*flat)
    return [outs[i * n:(i + 1) * n] for i in range(4)]


def _adamw(name, w, m, v, gparts, ride=None):
    R, C = w.shape
    K = gparts.shape[0]
    tr = _row_tile(R)
    nr = R // tr
    r_in, r_ispec, r_ospec, r_oshape, r_scratch, r_hook = _ride(ride, 4, 4)

    def kern(*refs):
        w_ref, m_ref, v_ref, g_ref = refs[:4]
        go_ref, d_ref, mo_ref, vo_ref = refs[4 + len(r_in):8 + len(r_in)]
        i = pl.program_id(0)
        before, after = r_hook(refs, i == 0, i == nr // 2, i == nr - 1)
        before()
        g = g_ref[0]
        for k in range(1, K):
            g = g + g_ref[k]
        d, mn, vn = _adam_update(w_ref[...], m_ref[...], v_ref[...], g)
        go_ref[...] = g
        d_ref[...] = d
        mo_ref[...] = mn
        vo_ref[...] = vn
        after()

    spec = pl.BlockSpec((tr, C), lambda i: (i, 0))
    return _pcall(
        kern, name=name, grid=(nr,),
        in_specs=[spec, spec, spec, pl.BlockSpec((K, tr, C), lambda i: (0, i, 0))] + r_ispec,
        out_specs=[spec] * 4 + r_ospec,
        out_shape=[jax.ShapeDtypeStruct((R, C), F32)] * 4 + r_oshape,
        scratch_shapes=r_scratch,
        compiler_params=_cparams(("arbitrary",) if ride else ("parallel",)),
    )(w, m, v, gparts, *r_in)


def _round_up(a, b):
    return (a + b - 1) // b * b


def kernel(x, c, w_ada, b_ada, norm1_g, w_in, b_forget, q_norm_g, k_norm_g, w_attn_proj, conv_w, conv_b, conv_ln_g, conv_ln_b, w_conv_proj, w_out, norm2_g, w_mlp1, w_mlp2, loss_target, m_w_ada, m_b_ada, m_norm1_g, m_w_in, m_b_forget, m_q_norm_g, m_k_norm_g, m_w_attn_proj, m_conv_w, m_conv_b, m_conv_ln_g, m_conv_ln_b, m_w_conv_proj, m_w_out, m_norm2_g, m_w_mlp1, m_w_mlp2, v_w_ada, v_b_ada, v_norm1_g, v_w_in, v_b_forget, v_q_norm_g, v_k_norm_g, v_w_attn_proj, v_conv_w, v_conv_b, v_conv_ln_g, v_conv_ln_b, v_w_conv_proj, v_w_out, v_norm2_g, v_w_mlp1, v_w_mlp2):
    S, D = x.shape[1], x.shape[2]
    NH, HD = b_forget.shape[-1], q_norm_g.shape[-1]
    TAPS = conv_w.shape[1]
    DIN_S = w_in.shape[-1]
    DIN = 4 * DIN_S
    DFF_S = w_mlp1.shape[-1]
    DFF = 4 * DFF_S
    ADA_S = w_ada.shape[-1]
    DS = w_attn_proj.shape[1]
    CS = conv_w.shape[-1]
    assert NH * HD == D and DIN == 7 * D + NH and TAPS - 1 <= HALO and D % LANES == 0 and 2 * HD == LANES
    NP = _round_up(7 * D + LANES, 512)
    FW = NP - 7 * D
    assert (7 * D) % FW == 0
    TQ = min(512, S)
    NQ = S // TQ
    FCOL = 7 * D // LANES

    xi, yi, ci = lax.axis_index("x"), lax.axis_index("y"), lax.axis_index("c")
    chip = 2 * xi + yi
    dev = 4 * xi + 2 * yi + ci

    x2 = x.reshape(S, D)
    tgt = loss_target.reshape(S, D)

    lane_head = jnp.arange(D, dtype=jnp.int32) // HD
    grp = (lane_head[:, None] == jnp.arange(LANES, dtype=jnp.int32)[None, :]).astype(BF16)
    grp_t = grp.T
    sel = ((jnp.arange(D, dtype=jnp.int32)[:, None] == HD * jnp.arange(LANES, dtype=jnp.int32)[None, :])
           .astype(BF16))
    ch = min(256, S)
    ii = jnp.arange(ch, dtype=jnp.int32)
    tri = (ii[None, :] <= ii[:, None]).astype(BF16)
    tri_u = tri.T
    gq_t = jnp.tile(q_norm_g.reshape(1, HD), (1, NH))
    gk_t = jnp.tile(k_norm_g.reshape(1, HD), (1, NH))
    bf_pad = jnp.pad(b_forget.reshape(1, NH), ((0, 0), (0, LANES - NH)))

    c_all, cw_all = _ag_small(
        "ag_c_convw", [c.reshape(1, D), jnp.pad(conv_w.reshape(TAPS, CS), ((0, HALO - TAPS), (0, 0)))])
    c_all = c_all.reshape(8, D)
    b_part = lax.dynamic_slice(b_ada.reshape(1, -1), (0, chip * ADA_S), (1, ADA_S))
    mod_part = _ada_fwd(c_all, w_ada.reshape(D, ADA_S), b_part)
    (mod_all,) = _ag_small("ag_mod", [mod_part])
    mod_full = jnp.concatenate([mod_all[0], mod_all[2], mod_all[4], mod_all[6]], axis=1)
    mod = lax.dynamic_slice(mod_full, (dev, 0), (1, 6 * D))
    sh1, sc1, g1, sh2, sc2, g2 = [mod[:, i * D:(i + 1) * D] for i in range(6)]

    shards = [w_in.reshape(D, DIN_S), w_attn_proj.reshape(DS, D), w_conv_proj.reshape(DS, D),
              w_out.reshape(DS, D), w_mlp1.reshape(D, DFF_S), w_mlp2.reshape(DFF_S, D)]
    shards = [s.astype(BF16) for s in shards]
    (gw_in,) = _exchange("ag_w_in", shards[:1], _gather_plan)
    w_conv = jnp.concatenate([cw_all[0], cw_all[2], cw_all[4], cw_all[6]], axis=1)

    SEGS = [(0, 2 * D, 0), (3 * D + NH, DIN, 2 * D), (2 * D, 3 * D + NH, 6 * D)]

    def pieces(a, b):
        out = []
        for k in range(4):
            lo, hi = max(a, k * DIN_S), min(b, (k + 1) * DIN_S)
            if lo < hi:
                out.append(gw_in[k][:, lo - k * DIN_S:hi - k * DIN_S])
        return out

    w_in_p = jnp.concatenate([p for (a, b, _) in SEGS for p in pieces(a, b)]
                             + [jnp.zeros((D, NP - 7 * D - NH), BF16)], axis=1)

    n1g = norm1_g.reshape(1, D)
    n2g = norm2_g.reshape(1, D)
    h = _norm_mod("norm_mod1", x2, n1g, sc1, sh1, S, D)
    proj = _mm("mm_in", h, w_in_p, "nn", [F32], tn=1536)
    qs, kn, vb = _qk_prep(proj, 6, gq_t, gk_t, grp, grp_t, S, D, HD)
    f_cum = _fgate_fwd(proj, FCOL, bf_pad, tri, S)
    fk_c = f_cum[:, :NH]
    fk_r = fk_c.T.reshape(NH, NQ, 1, TQ)
    o, o32, lse_b, gw_ap, gw_cp, gw_out, gw_m1, gw_m2 = _flash_fwd(
        qs, kn, vb, fk_r, S, D, HD, TQ, ride=(_gather_plan, shards[1:]))
    w_ap = gw_ap.reshape(D, D)
    w_cp = gw_cp.reshape(D, D)
    w_o = gw_out.reshape(D, D)
    w_m1 = jnp.transpose(gw_m1, (1, 0, 2)).reshape(D, DFF)
    w_m2 = gw_m2.reshape(DFF, D)
    br_a = _mm("mm_attn_proj", o, w_ap, "nn", [F32])
    cb, clg, clb = conv_b.reshape(1, D), conv_ln_g.reshape(1, D), conv_ln_b.reshape(1, D)
    u1, u3 = _conv_fwd(proj, 2, 3, w_conv, cb, clg, clb, S, D, TAPS, 256)
    br_b = _mm("mm_conv_proj", u3, w_cp, "nn", [F32])
    merged = _gate_merge(proj, 4, 5, br_a, br_b, S, D)
    mo = _mm("mm_out", merged, w_o, "nn", [F32])
    x1, h2 = _resid_norm2(x2, mo, g1, n2g, sc2, sh2, S, D)

    def relu2(r):
        rp = jnp.maximum(r, 0.0)
        return (rp * rp,)
    z = _mm("mm_mlp1", h2, w_m1, "nn", [BF16], epi=relu2)
    ml = _mm("mm_mlp2", z, w_m2, "nn", [F32])
    dy, dml, sq, dg2 = _loss_dy(x1, ml, tgt, g2, S, D)
    loss_part = jnp.full((1, LANES), 0.5 * jnp.sum(sq) / D, F32)

    da = _mm("mm_dz", dml, w_m2, "nt", [BF16], epi=lambda r, zz: (r * 2.0 * jnp.sqrt(zz.astype(F32)),),
             extras=(z,))
    dw_m2 = _mm("mm_dw_mlp2", z, dml, "tn", [BF16])
    dw_m1 = _mm("mm_dw_mlp1", h2, da, "tn", [BF16])
    dh2 = _mm("mm_dh2", da, w_m1, "nt", [F32])
    dx1, dmo, dsh2, dsc2, dn2g, dg1 = _norm_bwd("norm2_bwd", x1, dh2, dy, n2g, sc2, S, D, extra=(mo, g1))
    dmerged = _mm("mm_dmerged", dmo, w_o, "nt", [F32])
    dw_o = _mm("mm_dw_out", merged, dmo, "tn", [BF16])
    dproj, dba, dbb = _gate_bwd(dmerged, proj, 4, 5, br_a, br_b, S, D, into=(lax.empty((S, NP), BF16), 2))
    do = _mm("mm_do", dba, w_ap, "nt", [BF16])
    dw_ap = _mm("mm_dw_attn_proj", o, dba, "tn", [BF16])
    du3 = _mm("mm_du3", dbb, w_cp, "nt", [F32])
    dw_cp = _mm("mm_dw_conv_proj", u3, dbb, "tn", [BF16])
    core = ci.astype(jnp.int32).reshape(1)
    chip1 = chip.astype(jnp.int32).reshape(1)
    halves = lambda p: p.astype(BF16).reshape(4, 2, p.shape[1] // 2, p.shape[2])
    names = ["w_in", "w_attn_proj", "w_conv_proj", "w_out", "w_mlp1", "w_mlp2"]
    parts = [halves(p) for p in (dw_ap.reshape(4, DS, D), dw_cp.reshape(4, DS, D), dw_o.reshape(4, DS, D),
                                 jnp.transpose(dw_m1.reshape(D, 4, DFF_S), (1, 0, 2)), dw_m2.reshape(4, DFF_S, D))]
    dproj, dcw, dcb, dclg, dclb, *theirs = _conv_bwd(
        du3, u1, proj, 2, 3, w_conv, clg, clb, S, D, TAPS, 256, into=(dproj, 1), ride=(_pair_halves_plan, parts))
    chip_parts =[_sum_pair("sum_pair_" + nm, core, p, t) for nm, p, t in zip(names[1:], parts, theirs)]

    delta_c, lse_c = _delta_prep(do, o32, lse_b, grp, sel, S, D)
    to_rows = lambda t: t[:, :NH].T.reshape(NH, NQ, 1, TQ)
    dkn, dproj, dqs, dfq_r, dfk_b, *recvd = _flash_bwd(
        qs, kn, vb, do, f_cum, to_rows(lse_c), to_rows(delta_c), S, D, HD, TQ,
        dv_into=(dproj, 6 * D // LANES), ride=(_scatter_plan, chip_parts))
    dproj, sq_q, sq_k = _qk_bwd(proj, dqs, dkn, gq_t, gk_t, grp, grp_t, S, D, HD, into=(dproj, 0))
    to_cols = lambda r: jnp.pad(r.reshape(NH, S).T, ((0, 0), (0, LANES - NH)))
    dfq_pad = to_cols(dfq_r)
    dfk_pad = _pick_heads(dfk_b, sel, S, D)
    dproj, dbf = _fgate_bwd(dfk_pad, dfq_pad, proj, FCOL, bf_pad, tri_u, NH, S, FW, into=(dproj, 7 * D // FW))
    dw_in_p = _mm("mm_dw_in", h, dproj, "tn", [BF16])
    def shard_cols(k):
        out = []
        for (a, b, start) in sorted(SEGS):
            lo, hi = max(a, k * DIN_S), min(b, (k + 1) * DIN_S)
            if lo < hi:
                out.append(dw_in_p[:, start + lo - a:start + hi - a])
        return jnp.concatenate(out, axis=1)

    part_in = halves(jnp.stack([shard_cols(k) for k in range(4)]))
    (their_in,) = _exchange("rs_pair_w_in", [part_in], _pair_halves_plan)
    chip_in = _sum_pair("sum_pair_w_in", core, part_in, their_in)
    dh, recv_in = _mm("mm_dh", dproj, w_in_p, "nt", [F32], ride=(_scatter_plan, [chip_in]))
    gx, dsh1, dsc1, dn1g = _norm_bwd("norm1_bwd", x2, dh, dx1, n1g, sc1, S, D)

    packed = jnp.concatenate([dsh1, dsc1, dg1, dsh2, dsc2, dg2, dn1g, dcb, dclg, dclb, dn2g,
                              sq_q, sq_k, dbf, loss_part], axis=1)
    small_all, dcw_all = _ag_small("ag_small_grads", [packed, dcw])
    small = _sum_slots("sum_small", small_all.reshape(8, 1, -1)).reshape(1, -1)
    dmod_sum = small[:, :6 * D]
    seg = lambda k: small[:, (6 + k) * D:(7 + k) * D]
    g_n1g, g_cb, g_clg, g_clb, g_n2g = seg(0), seg(1), seg(2), seg(3), seg(4)
    g_qn = _sum_slots("sum_qn", seg(5).reshape(NH, 1, HD))
    g_kn = _sum_slots("sum_kn", seg(6).reshape(NH, 1, HD))
    g_bf = small[:, 13 * D:13 * D + NH]
    loss = small[0, 13 * D + LANES]
    dcw_mine = lax.dynamic_slice(dcw_all[:, :TAPS, :], (0, 0, chip * CS), (8, TAPS, CS))

    dmod_all = small_all.reshape(8, -1)[:, :6 * D]
    dmod_cols = lax.dynamic_slice(dmod_all, (0, chip * ADA_S), (8, ADA_S))
    c_t_pad = jnp.pad(c_all.T, ((0, 0), (0, LANES - 8)))
    g_wada = _ada_wgrad(c_t_pad, jnp.pad(dmod_cols, ((0, LANES - 8), (0, 0))))

    sums =[_sum_chips("sum_" + nm, chip1, p, r)
            for nm, p, r in zip(names, [chip_in] + chip_parts, [recv_in] + list(recvd))]

    res = {}
    outs = _adamw("adamw_w_ada", w_ada.reshape(D, ADA_S), m_w_ada.reshape(D, ADA_S),
                  v_w_ada.reshape(D, ADA_S), g_wada.reshape(1, D, ADA_S), ride=(_pair_swap_plan, sums))
    res["w_ada"] = [t.reshape(w_ada.shape) for t in outs[:4]]
    big = {nm: (a, b) for nm, a, b in zip(names, sums, outs[4:])}
    big_w = {"w_in": (w_in, m_w_in, v_w_in), "w_attn_proj": (w_attn_proj, m_w_attn_proj, v_w_attn_proj),
             "w_conv_proj": (w_conv_proj, m_w_conv_proj, v_w_conv_proj), "w_out": (w_out, m_w_out, v_w_out),
             "w_mlp1": (w_mlp1, m_w_mlp1, v_w_mlp1), "w_mlp2": (w_mlp2, m_w_mlp2, v_w_mlp2)}
    for nm in names:
        shp = big_w[nm][0].shape
        if shp[2] % LANES:
            outs = _adamw_halves_t("adamw_" + nm, core, *[t.reshape(shp[1], shp[2]).T for t in big_w[nm]],
                                   *[t.T for t in big[nm]])
            res[nm] = [t.T.reshape(shp) for t in outs]
        else:
            outs = _adamw_halves("adamw_" + nm, core, *[t.reshape(shp[1], shp[2]) for t in big_w[nm]], *big[nm])
            res[nm] = [t.reshape(shp) for t in outs]
    outs = _adamw("adamw_conv_w", conv_w.reshape(TAPS, CS), m_conv_w.reshape(TAPS, CS),
                  v_conv_w.reshape(TAPS, CS), dcw_mine)
    res["conv_w"] = [t.reshape(conv_w.shape) for t in outs]

    small_w = [("b_ada", b_ada, m_b_ada, v_b_ada, dmod_sum), ("norm1_g", norm1_g, m_norm1_g, v_norm1_g, g_n1g),
               ("b_forget", b_forget, m_b_forget, v_b_forget, g_bf),
               ("q_norm_g", q_norm_g, m_q_norm_g, v_q_norm_g, g_qn),
               ("k_norm_g", k_norm_g, m_k_norm_g, v_k_norm_g, g_kn),
               ("conv_b", conv_b, m_conv_b, v_conv_b, g_cb), ("conv_ln_g", conv_ln_g, m_conv_ln_g, v_conv_ln_g, g_clg),
               ("conv_ln_b", conv_ln_b, m_conv_ln_b, v_conv_ln_b, g_clb),
               ("norm2_g", norm2_g, m_norm2_g, v_norm2_g, g_n2g)]
    outs = _adamw_many("adamw_small", [[t[i].reshape(1, -1) for t in small_w] for i in (1, 2, 3, 4)])
    for j, (nm, w_, _, _, _) in enumerate(small_w):
        res[nm] = [outs[i][j].reshape(w_.shape) for i in range(4)]

    order = ["w_ada", "b_ada", "norm1_g", "w_in", "b_forget", "q_norm_g", "k_norm_g", "w_attn_proj", "conv_w",
             "conv_b", "conv_ln_g", "conv_ln_b", "w_conv_proj", "w_out", "norm2_g", "w_mlp1", "w_mlp2"]
    return (loss, gx.reshape(x.shape), *[res[n][0] for n in order], *[res[n][1] for n in order],
            *[res[n][2] for n in order], *[res[n][3] for n in order])
```

```python
import functools

import jax
import jax.numpy as jnp
from jax import lax
from jax.experimental import pallas as pl
from jax.experimental.pallas import tpu as pltpu

F32 = jnp.float32
BF16 = jnp.bfloat16
MESH = pl.DeviceIdType.MESH
ANY = pl.BlockSpec(memory_space=pl.ANY)

NORM_EPS = 1e-6
ADAM_LR = 0.001
ADAM_B1 = 0.9
ADAM_B2 = 0.999
ADAM_EPS = 1e-08
ADAM_WD = 0.01
ADAM_STEP = 10
LANES = 128
SUBLANES = 8
HALO = 32
CONV_ROWS = 32
CONV_TAPS = 4
NEG = -1e30
VMEM_LIMIT = 56 * 1024 * 1024


def _pcall(body, **kw):
    return pl.pallas_call(body, **kw)


def _cparams(sem=None):
    if sem is None:
        return pltpu.CompilerParams(vmem_limit_bytes=VMEM_LIMIT)
    return pltpu.CompilerParams(dimension_semantics=sem, vmem_limit_bytes=VMEM_LIMIT)


def _sig(x):
    return 1.0 / (1.0 + jnp.exp(-x))


def _split3(x):
    x1 = x.astype(BF16)
    r = x - x1.astype(F32)
    x2 = r.astype(BF16)
    x3 = (r - x2.astype(F32)).astype(BF16)
    return x1, x2, x3


def _dot_rs(x, e, terms=3):
    out = None
    for t in _split3(x)[:terms]:
        d = jnp.dot(t, e, preferred_element_type=F32)
        out = d if out is None else out + d
    return out


def _dot_ls(e, x):
    out = None
    for t in _split3(x):
        d = jnp.dot(e, t, preferred_element_type=F32)
        out = d if out is None else out + d
    return out


def _tile(n, want):
    if n <= want:
        return n
    t = want - want % LANES
    while n % t:
        t -= LANES
    assert t > 0, (n, want)
    return t


_DIMS = {"nn": ((1,), (0,)), "nt": ((1,), (1,)), "tn": ((0,), (0,))}


def _mm(name, a, b, mode, out_dtypes, epi=None, extras=(), tm=1024, tn=1024, tk=4096, ride=None):
    if mode == "nn":
        (M, K), (_, N) = a.shape, b.shape
    elif mode == "nt":
        (M, K), (N, _) = a.shape, b.shape
    else:
        (K, M), (_, N) = a.shape, b.shape
    tm, tn, tk = _tile(M, tm), _tile(N, tn), _tile(K, tk)
    nm, nn, nk = M // tm, N // tn, K // tk
    ne, no = len(extras), len(out_dtypes)
    dims = (_DIMS[mode], ((), ()))
    r_in, r_ispec, r_ospec, r_oshape, r_scratch, r_hook = _ride(ride, 2 + ne, no)

    def kern(*refs):
        a_ref, b_ref = refs[0], refs[1]
        e_refs = refs[2:2 + ne]
        o_refs = refs[2 + ne + len(r_in):2 + ne + len(r_in) + no]
        i, j, k = pl.program_id(0), pl.program_id(1), pl.program_id(2)
        before, after = r_hook(refs, (i == 0) & (j == 0) & (k == 0), (i == nm // 2) & (j == 0) & (k == 0),
                               (i == nm - 1) & (j == nn - 1) & (k == nk - 1))
        before()
        d = lax.dot_general(a_ref[...], b_ref[...], dims, preferred_element_type=F32)

        def finish(r):
            outs = (r,) if epi is None else epi(r, *[e[...] for e in e_refs])
            for o_ref, o in zip(o_refs, outs):
                o_ref[...] = o.astype(o_ref.dtype)

        if nk == 1:
            finish(d)
        else:
            acc = refs[-1]

            @pl.when(k == 0)
            def _():
                acc[...] = d

            @pl.when((k > 0) & (k < nk - 1))
            def _():
                acc[...] += d

            @pl.when(k == nk - 1)
            def _():
                finish(acc[...] + d)
        after()

    if mode == "tn":
        a_spec = pl.BlockSpec((tk, tm), lambda i, j, k: (k, i))
    else:
        a_spec = pl.BlockSpec((tm, tk), lambda i, j, k: (i, k))
    if mode == "nt":
        b_spec = pl.BlockSpec((tn, tk), lambda i, j, k: (j, k))
    else:
        b_spec = pl.BlockSpec((tk, tn), lambda i, j, k: (k, j))
    mn_spec = pl.BlockSpec((tm, tn), lambda i, j, k: (i, j))
    outs = _pcall(
        kern, name=name, grid=(nm, nn, nk),
        in_specs=[a_spec, b_spec] + [mn_spec] * ne + r_ispec,
        out_specs=[mn_spec] * no + r_ospec,
        out_shape=[jax.ShapeDtypeStruct((M, N), dt) for dt in out_dtypes] + r_oshape,
        scratch_shapes=r_scratch + ([pltpu.VMEM((tm, tn), F32)] if nk > 1 else []),
        compiler_params=_cparams(("arbitrary",) * 3 if ride else ("parallel", "parallel", "arbitrary")),
    )(a, b, *extras, *r_in)
    return outs[0] if len(outs) == 1 else outs


def _rowcall(name, body, S, ts, row_ins, vec_ins, row_outs, vec_outs, into=None):
    ts = min(ts, S)
    nri, nvi, nro, nvo = len(row_ins), len(vec_ins), len(row_outs), len(vec_outs)
    na = 0 if into is None else 1

    def kern(*refs):
        ins = refs[:nri + nvi]
        outs = refs[nri + nvi + na:]
        if nvo:
            @pl.when(pl.program_id(0) == 0)
            def _():
                for r in outs[nro:]:
                    r[...] = jnp.zeros(r.shape, r.dtype)
        body(*ins, *outs)

    in_specs = [pl.BlockSpec((ts, w), functools.partial(lambda i, cb: (i, cb), cb=cb))
                for (_, w, cb) in row_ins]
    in_specs += [pl.BlockSpec(v.shape, lambda i: (0, 0)) for v in vec_ins]
    out_specs = [pl.BlockSpec((ts, w), lambda i: (i, 0)) for (w, _) in row_outs]
    out_specs += [pl.BlockSpec((r, w), lambda i: (0, 0)) for (r, w) in vec_outs]
    out_shape = [jax.ShapeDtypeStruct((S, w), dt) for (w, dt) in row_outs]
    out_shape += [jax.ShapeDtypeStruct((r, w), F32) for (r, w) in vec_outs]
    extra, aliases = [], {}
    if into is not None:
        buf, cb = into
        assert buf.dtype == row_outs[0][1] and buf.shape[0] == S
        in_specs.append(ANY)
        out_specs[0] = pl.BlockSpec((ts, row_outs[0][0]), lambda i: (i, cb))
        out_shape[0] = jax.ShapeDtypeStruct(buf.shape, buf.dtype)
        extra, aliases = [buf], {nri + nvi: 0}
    return _pcall(
        kern, name=name, grid=(S // ts,), in_specs=in_specs, out_specs=out_specs,
        out_shape=out_shape, input_output_aliases=aliases,
        compiler_params=_cparams(("arbitrary",) if nvo else ("parallel",)),
    )(*[a for (a, _, _) in row_ins], *vec_ins, *extra)


def _csum(x):
    return jnp.sum(x, axis=0, keepdims=True)


def _norm_mod(name, x, g, sc, sh, S, D):
    def body(x_ref, g_ref, sc_ref, sh_ref, h_ref):
        xv = x_ref[...]
        r = lax.rsqrt(jnp.mean(xv * xv, axis=-1, keepdims=True) + NORM_EPS)
        h_ref[...] = ((xv * r * g_ref[...]) * (1.0 + sc_ref[...]) + sh_ref[...]).astype(BF16)
    return _rowcall(name, body, S, 512, [(x, D, 0)], [g, sc, sh], [(D, BF16)], [])[0]


def _head_rstd(v, grp, grp_t, hd):
    ss = _dot_rs(v * v, grp, 1) * (1.0 / hd)
    r = lax.rsqrt(ss + NORM_EPS)
    return _dot_rs(r, grp_t, 2)


def _qk_prep(proj, vcol, gq, gk, grp, grp_t, S, D, hd):
    scale = hd ** -0.5

    def body(q_ref, k_ref, v_ref, gq_ref, gk_ref, g_ref, gt_ref, qs_ref, kn_ref, vb_ref):
        q = q_ref[...]
        k = k_ref[...]
        rq = _head_rstd(q, g_ref[...], gt_ref[...], hd)
        rk = _head_rstd(k, g_ref[...], gt_ref[...], hd)
        qs_ref[...] = ((q * rq * gq_ref[...]).astype(BF16).astype(F32) * scale).astype(BF16)
        kn_ref[...] = (k * rk * gk_ref[...]).astype(BF16)
        vb_ref[...] = v_ref[...].astype(BF16)

    return _rowcall("qk_prep", body, S, 512, [(proj, D, 0), (proj, D, 1), (proj, D, vcol)],
                    [gq, gk, grp, grp_t], [(D, BF16)] * 3, [])


def _fgate_fwd(proj, fcol, bf_pad, tri, S):
    ch = tri.shape[0]

    def body(f_ref, b_ref, tri_ref, out_ref):
        carry = jnp.zeros((1, LANES), F32)
        for c in range(S // ch):
            z = f_ref[c * ch:(c + 1) * ch, :] + b_ref[...]
            lf = jnp.minimum(z, 0.0) - jnp.log(1.0 + jnp.exp(-jnp.abs(z)))
            out_ref[c * ch:(c + 1) * ch, :] = _dot_ls(tri_ref[...], lf) + carry
            carry = carry + _csum(lf)

    return _rowcall("fgate_fwd", body, S, S, [(proj, LANES, fcol)], [bf_pad, tri],
                    [(LANES, F32)], [])[0]


def _fgate_bwd(dfk, dfq, proj, fcol, bf_pad, tri_u, nh, S, fw, into):
    ch = tri_u.shape[0]

    def body(d_ref, dq_ref, f_ref, b_ref, tri_ref, df_ref, db_ref):
        if fw > LANES:
            df_ref[:, LANES:fw] = jnp.zeros((S, fw - LANES), BF16)
        lane = lax.broadcasted_iota(jnp.int32, (ch, LANES), 1)
        carry = jnp.zeros((1, LANES), F32)
        tot = jnp.zeros((1, LANES), F32)
        for c in reversed(range(S // ch)):
            d = d_ref[c * ch:(c + 1) * ch, :] + dq_ref[c * ch:(c + 1) * ch, :]
            rc = _dot_ls(tri_ref[...], d) + carry
            carry = carry + _csum(d)
            z = f_ref[c * ch:(c + 1) * ch, :] + b_ref[...]
            df = jnp.where(lane < nh, rc * _sig(-z), 0.0)
            df_ref[c * ch:(c + 1) * ch, 0:LANES] = df.astype(BF16)
            tot = tot + _csum(df)
        db_ref[...] += tot

    return _rowcall("fgate_bwd", body, S, S, [(dfk, LANES, 0), (dfq, LANES, 0), (proj, LANES, fcol)],
                    [bf_pad, tri_u], [(fw, BF16)], [(1, LANES)], into=into)


def _keep(v, mask):
    return jnp.where(mask, v.astype(F32), 0.0).astype(BF16)


def _lane_col(blk, lane, at):
    return jnp.sum(jnp.where(lane == at, blk, 0.0), axis=-1, keepdims=True)


def _flash_fwd(qs, kn, vb, fk_r, S, D, hd, tq, ride=None):
    hp, nq = D // LANES, S // tq
    r_in, r_ispec, r_ospec, r_oshape, r_scratch, r_hook = _ride(ride, 4, 3)

    def kern(*refs):
        q_ref, k_ref, v_ref, fk_ref = refs[:4]
        o_ref, o32_ref, lse_ref = refs[4 + len(r_in):7 + len(r_in)]
        hi, qi = pl.program_id(0), pl.program_id(1)
        before, after = r_hook(refs, (hi == 0) & (qi == 0), (hi == hp // 2) & (qi == 0),
                               (hi == hp - 1) & (qi == nq - 1))
        before()
        lane = lax.broadcasted_iota(jnp.int32, (tq, LANES), 1)
        row = lax.broadcasted_iota(jnp.int32, (tq, tq), 0)
        col = lax.broadcasted_iota(jnp.int32, (tq, tq), 1)
        hms = [(lane >= j * hd) & (lane < (j + 1) * hd) for j in range(2)]
        q = q_ref[...]
        qms = [_keep(q, hm) for hm in hms]

        s_a, s_b, v1_sc = refs[-3], refs[-2], refs[-1]

        @pl.when(qi == 0)
        def _():
            for c0 in range(0, S, tq):
                vf = v_ref[c0:c0 + tq, :].astype(F32)
                for j in range(2):
                    v1_sc[j, c0:c0 + tq, :] = jnp.where(hms[j], vf, 1.0).astype(BF16)

        def put(s_ref, ki):
            off = pl.multiple_of(ki * tq, tq)
            k = k_ref[pl.ds(off, tq), :]
            for j in range(2):
                s_ref[j] = lax.dot_general(qms[j], k, (((1,), (1,)), ((), ())), preferred_element_type=F32)

        def update(ki, s_ref, state, masked):
            off = pl.multiple_of(ki * tq, tq)
            new = []
            for j in range(2):
                m_old, acc = state[j]
                s = s_ref[j] - fk_ref[j, ki]
                if masked:
                    s = jnp.where(col <= row, s, NEG)
                m_new = jnp.maximum(m_old, jnp.max(s, axis=-1, keepdims=True))
                alpha = jnp.exp(m_old - m_new)
                p = jnp.exp(s - m_new)
                acc = alpha * acc + jnp.dot(p.astype(BF16), v1_sc[j, pl.ds(off, tq), :],
                                            preferred_element_type=F32)
                new.append((m_new, acc))
            return tuple(new)

        def pair(p, state):
            put(s_b, 2 * p + 1)
            state = update(2 * p, s_a, state, False)
            put(s_a, 2 * p + 2)
            return update(2 * p + 1, s_b, state, False)

        def odd_tail(state):
            put(s_b, qi)
            return update(qi, s_b, update(qi - 1, s_a, state, False), True)

        init = tuple((jnp.full((tq, 1), NEG, F32), jnp.zeros((tq, LANES), F32)) for _ in range(2))
        put(s_a, 0)
        state = lax.fori_loop(0, qi // 2, pair, init)
        (m0, a0), (m1, a1) = lax.cond(qi % 2 == 1, odd_tail, lambda st: update(qi, s_a, st, True), state)
        l0, l1 = pltpu.roll(a0, hd, 1), pltpu.roll(a1, hd, 1)
        first = lane < hd
        ov = jnp.where(first, a0 / l0, a1 / l1)
        o_ref[...] = ov.astype(BF16)
        o32_ref[...] = ov
        lse_ref[...] = jnp.where(first, m0 + jnp.log(l0), m1 + jnp.log(l1))
        after()

    qspec = pl.BlockSpec((tq, LANES), lambda h, i: (i, h))
    fullspec = pl.BlockSpec((S, LANES), lambda h, i: (0, h))
    return _pcall(
        kern, name="flash_fwd", grid=(hp, nq),
        in_specs=[qspec, fullspec, fullspec,
                  pl.BlockSpec((2, nq, 1, tq), lambda h, i: (h, 0, 0, 0))] + r_ispec,
        out_specs=[qspec, qspec, qspec] + r_ospec,
        out_shape=[jax.ShapeDtypeStruct((S, D), BF16), jax.ShapeDtypeStruct((S, D), F32),
                   jax.ShapeDtypeStruct((S, D), F32)] + r_oshape,
        scratch_shapes=r_scratch + [pltpu.VMEM((2, tq, tq), F32)] * 2 + [pltpu.VMEM((2, S, LANES), BF16)],
        compiler_params=_cparams(("arbitrary", "arbitrary")),
    )(qs, kn, vb, fk_r, *r_in)


def _flash_bwd(qs, kn, vb, do, fk_b, lse_r, delta_r, S, D, hd, tq, dv_into, ride=None):
    hp, nq = D // LANES, S // tq
    dbuf_hbm, dv_col = dv_into
    r_in, r_ispec, r_ospec, r_oshape, r_scratch, r_hook = _ride(ride, 8, 5)

    def kern(*refs):
        q_ref, do_ref, k_ref, v_ref, fk_ref, lse_ref, dl_ref = refs[:7]
        dk_ref, dv_ref, dq_ref, dfq_ref, dfk_ref = refs[8 + len(r_in):13 + len(r_in)]
        hi, ki = pl.program_id(0), pl.program_id(1)
        before, after = r_hook(refs, (hi == 0) & (ki == 0), (hi == hp // 2) & (ki == 0),
                               (hi == hp - 1) & (ki == nq - 1))
        before()
        lane = lax.broadcasted_iota(jnp.int32, (tq, LANES), 1)
        row = lax.broadcasted_iota(jnp.int32, (tq, tq), 0)
        col = lax.broadcasted_iota(jnp.int32, (tq, tq), 1)
        hms = [(lane >= j * hd) & (lane < (j + 1) * hd) for j in range(2)]
        k = k_ref[...]
        v = v_ref[...]
        fkb = fk_ref[...]
        kms = [_keep(k, hm) for hm in hms]
        vms = [_keep(v, hm) for hm in hms]
        fks = [_lane_col(fkb, lane, 2 * hi + j) for j in range(2)]

        @pl.when(ki == 0)
        def _():
            dfq_ref[...] = jnp.zeros(dfq_ref.shape, F32)
            dq_ref[...] = jnp.zeros(dq_ref.shape, F32)

        def step(qi, acc, masked):
            dk, dv, dfs = acc
            off = pl.multiple_of(qi * tq, tq)
            q = q_ref[pl.ds(off, tq), :]
            g = do_ref[pl.ds(off, tq), :]
            dq = None
            new_dfs = []
            for j in range(2):
                qm = _keep(q, hms[j])
                gm = _keep(g, hms[j])
                st = lax.dot_general(kms[j], q, (((1,), (1,)), ((), ())), preferred_element_type=F32)
                st = st - fks[j]
                if masked:
                    st = jnp.where(row <= col, st, NEG)
                pt = jnp.exp(st - lse_ref[j, qi])
                dv = dv + jnp.dot(pt.astype(BF16), gm, preferred_element_type=F32)
                dpt = lax.dot_general(vms[j], g, (((1,), (1,)), ((), ())), preferred_element_type=F32)
                dst = pt * (dpt - dl_ref[j, qi])
                dsb = dst.astype(BF16)
                dk = dk + jnp.dot(dsb, qm, preferred_element_type=F32)
                t = lax.dot_general(dsb, kms[j], (((0,), (0,)), ((), ())), preferred_element_type=F32)
                dq = t if dq is None else dq + t
                dfq_ref[j, qi] += jnp.sum(dst, axis=0, keepdims=True)
                new_dfs.append(dfs[j] - jnp.sum(dst, axis=1, keepdims=True))
            dq_ref[pl.ds(off, tq), :] += dq
            return dk, dv, tuple(new_dfs)

        zero = jnp.zeros((tq, LANES), F32)
        zcol = jnp.zeros((tq, 1), F32)
        acc = step(ki, (zero, zero, (zcol, zcol)), True)
        dk, dv, dfs = lax.fori_loop(ki + 1, nq, lambda qi, a: step(qi, a, False), acc)
        dk_ref[...] = dk.astype(BF16)
        dv_ref[...] = dv.astype(BF16)
        dfk_ref[...] = jnp.where(lane < hd, dfs[0], dfs[1])
        after()

    kspec = pl.BlockSpec((tq, LANES), lambda h, i: (i, h))
    fullspec = pl.BlockSpec((S, LANES), lambda h, i: (0, h))
    rowspec = pl.BlockSpec((2, nq, 1, tq), lambda h, i: (h, 0, 0, 0))
    return _pcall(
        kern, name="flash_bwd", grid=(hp, nq),
        in_specs=[fullspec, fullspec, kspec, kspec, pl.BlockSpec((tq, LANES), lambda h, i: (i, 0)),
                  rowspec, rowspec, ANY] + r_ispec,
        out_specs=[kspec, pl.BlockSpec((tq, LANES), lambda h, i: (i, h + dv_col)), fullspec, rowspec, kspec]
        + r_ospec,
        out_shape=[jax.ShapeDtypeStruct((S, D), BF16), jax.ShapeDtypeStruct(dbuf_hbm.shape, BF16),
                   jax.ShapeDtypeStruct((S, D), F32), jax.ShapeDtypeStruct((2 * hp, nq, 1, tq), F32),
                   jax.ShapeDtypeStruct((S, D), F32)] + r_oshape,
        scratch_shapes=r_scratch, input_output_aliases={7: 1},
        compiler_params=_cparams(("arbitrary", "arbitrary")),
    )(qs, do, kn, vb, fk_b, lse_r, delta_r, dbuf_hbm, *r_in)


def _delta_prep(do, o, lse_b, grp, sel, S, D):
    def body(g_ref, o_ref, l_ref, e_ref, s_ref, dl_ref, lse_ref):
        prod = g_ref[...].astype(F32) * o_ref[...]
        dl_ref[...] = _dot_rs(prod, e_ref[...], 2)
        lse_ref[...] = _dot_rs(l_ref[...], s_ref[...])
    return _rowcall("delta_prep", body, S, 512, [(do, D, 0), (o, D, 0), (lse_b, D, 0)], [grp, sel],
                    [(LANES, F32), (LANES, F32)], [])


def _pick_heads(x_b, sel, S, D):
    def body(x_ref, s_ref, o_ref):
        o_ref[...] = _dot_rs(x_ref[...], s_ref[...])
    return _rowcall("pick_heads", body, S, 512, [(x_b, D, 0)], [sel], [(LANES, F32)], [])[0]


def _qk_bwd(proj, dqs, dkn, gq, gk, grp, grp_t, S, D, hd, into):
    scale = hd ** -0.5

    def one(x, dn, gain, e, et):
        r = _head_rstd(x, e, et, hd)
        xh = x * r
        t = dn * gain
        mean = _dot_rs(_dot_rs(t * xh, e, 1), et, 2) * (1.0 / hd)
        return r * (t - xh * mean), _csum(dn * xh)

    def body(q_ref, k_ref, dq_ref, dk_ref, gq_ref, gk_ref, e_ref, et_ref, o_ref, sq_ref, sk_ref):
        e, et = e_ref[...], et_ref[...]
        dq, sq = one(q_ref[...], dq_ref[...].astype(F32) * scale, gq_ref[...], e, et)
        dk, sk = one(k_ref[...], dk_ref[...].astype(F32), gk_ref[...], e, et)
        o_ref[:, 0:D] = dq.astype(BF16)
        o_ref[:, D:2 * D] = dk.astype(BF16)
        sq_ref[...] += sq
        sk_ref[...] += sk

    return _rowcall("qk_bwd", body, S, 512,
                    [(proj, D, 0), (proj, D, 1), (dqs, D, 0), (dkn, D, 0)],
                    [gq, gk, grp, grp_t], [(2 * D, BF16)], [(1, D)] * 2, into=into)


def _shift_copies(buf, sh, ts):
    for b in range(1, SUBLANES):
        sh[b - 1] = buf[b:b + ts + HALO - SUBLANES, :]


def _rows_from(buf, sh, o, ts):
    a, b = divmod(o, SUBLANES)
    if b == 0:
        return buf[o:o + ts, :]
    return sh[b - 1, SUBLANES * a:SUBLANES * a + ts, :]


def _conv_fwd(proj, acol, bcol, w_pad, cb, lg, lb, S, C, taps, ts):
    ts = min(ts, S)

    def kern(a_ref, b_ref, w_ref, cb_ref, lg_ref, lb_ref, u1_ref, u3_ref, ubuf, ush):
        @pl.when(pl.program_id(0) == 0)
        def _():
            ubuf[0:HALO, :] = jnp.zeros((HALO, C), F32)

        ubuf[HALO:HALO + ts, :] = a_ref[...] * _sig(b_ref[...])
        _shift_copies(ubuf, ush, ts)
        acc = jnp.zeros((ts, C), F32) + cb_ref[...]
        for k in range(taps):
            acc = acc + w_ref[k:k + 1, :] * _rows_from(ubuf, ush, HALO - (taps - 1) + k, ts)
        u1_ref[...] = acc
        mu = jnp.mean(acc, axis=-1, keepdims=True)
        xc = acc - mu
        rstd = lax.rsqrt(jnp.mean(xc * xc, axis=-1, keepdims=True) + NORM_EPS)
        u2 = xc * rstd * lg_ref[...] + lb_ref[...]
        u3_ref[...] = (u2 * _sig(u2)).astype(BF16)
        ubuf[0:HALO, :] = ubuf[ts:ts + HALO, :]

    vec = lambda a: pl.BlockSpec(a.shape, lambda i: (0, 0))
    return _pcall(
        kern, name="conv_fwd", grid=(S // ts,),
        in_specs=[pl.BlockSpec((ts, C), lambda i: (i, acol)), pl.BlockSpec((ts, C), lambda i: (i, bcol)),
                  vec(w_pad), vec(cb), vec(lg), vec(lb)],
        out_specs=[pl.BlockSpec((ts, C), lambda i: (i, 0))] * 2,
        out_shape=[jax.ShapeDtypeStruct((S, C), F32), jax.ShapeDtypeStruct((S, C), BF16)],
        scratch_shapes=[pltpu.VMEM((HALO + ts, C), F32),
                        pltpu.VMEM((SUBLANES - 1, HALO + ts - SUBLANES, C), F32)],
        compiler_params=_cparams(("arbitrary",)),
    )(proj, proj, w_pad, cb, lg, lb)


def _conv_bwd(du3, u1, proj, acol, bcol, w_pad, lg, lb, S, C, taps, ts, into, ride=None):
    ts = min(ts, S)
    dbuf_hbm, dcol = into
    r_in, r_ispec, r_ospec, r_oshape, r_scratch, r_hook = _ride(ride, 12, 5)
    nt = S // ts
    hb = ts // HALO

    def ln_bwd(g, u, lgv, lbv):
        mu = jnp.mean(u, axis=-1, keepdims=True)
        xc = u - mu
        rstd = lax.rsqrt(jnp.mean(xc * xc, axis=-1, keepdims=True) + NORM_EPS)
        xh = xc * rstd
        u2 = xh * lgv + lbv
        s = _sig(u2)
        du2 = g * (s + u2 * s * (1.0 - s))
        dxh = du2 * lgv
        du1 = rstd * (dxh - jnp.mean(dxh, axis=-1, keepdims=True)
                      - xh * jnp.mean(dxh * xh, axis=-1, keepdims=True))
        return du1, du2, xh

    def kern(*refs):
        g_ref, u_ref, a_ref, b_ref, gn_ref, un_ref, ap_ref, bp_ref, w_ref, lg_ref, lb_ref = refs[:11]
        dg_ref, dw_ref, dcb_ref, dlg_ref, dlb_ref = refs[12 + len(r_in):17 + len(r_in)]
        dbuf, ubuf, dsh, ush = refs[-4:]
        i = pl.program_id(0)
        before, after = r_hook(refs, i == 0, i == nt // 2, i == nt - 1)
        before()

        @pl.when(i == 0)
        def _():
            dw_ref[...] = jnp.zeros(dw_ref.shape, F32)
            dcb_ref[...] = jnp.zeros(dcb_ref.shape, F32)
            dlg_ref[...] = jnp.zeros(dlg_ref.shape, F32)
            dlb_ref[...] = jnp.zeros(dlb_ref.shape, F32)

        lgv, lbv = lg_ref[...], lb_ref[...]
        du1, du2, xh = ln_bwd(g_ref[...], u_ref[...], lgv, lbv)
        dbuf[0:ts, :] = du1
        du1n, _, _ = ln_bwd(gn_ref[...], un_ref[...], lgv, lbv)
        dbuf[ts:ts + HALO, :] = jnp.where(i < nt - 1, du1n, 0.0)
        a = a_ref[...]
        sb = _sig(b_ref[...])
        ubuf[HALO:HALO + ts, :] = a * sb
        ubuf[0:HALO, :] = jnp.where(i > 0, ap_ref[...] * _sig(bp_ref[...]), 0.0)
        dcb_ref[...] += _csum(du1)
        dlg_ref[...] += _csum(du2 * xh)
        dlb_ref[...] += _csum(du2)
        _shift_copies(dbuf, dsh, ts)
        _shift_copies(ubuf, ush, ts)
        for r0 in range(0, ts, CONV_ROWS):
            du0 = jnp.zeros((CONV_ROWS, C), F32)
            for k in range(taps):
                du0 = du0 + w_ref[k:k + 1, :] * _rows_from(dbuf, dsh, r0 + taps - 1 - k, CONV_ROWS)
            ac = a_ref[r0:r0 + CONV_ROWS, :]
            sc = _sig(b_ref[r0:r0 + CONV_ROWS, :])
            dg_ref[r0:r0 + CONV_ROWS, 0:C] = (du0 * sc).astype(BF16)
            dg_ref[r0:r0 + CONV_ROWS, C:2 * C] = (du0 * ac * sc * (1.0 - sc)).astype(BF16)
        for k0 in range(0, taps, CONV_TAPS):
            ks = range(k0, min(k0 + CONV_TAPS, taps))
            accs = [jnp.zeros((SUBLANES, C), F32) for _ in ks]
            for r0 in range(0, ts, CONV_ROWS):
                d = dbuf[r0:r0 + CONV_ROWS, :]
                for t, k in enumerate(ks):
                    prod = d * _rows_from(ubuf, ush, r0 + HALO - (taps - 1) + k, CONV_ROWS)
                    accs[t] = accs[t] + jnp.sum(prod.reshape(CONV_ROWS // SUBLANES, SUBLANES, C), axis=0)
            for t, k in enumerate(ks):
                dw_ref[k:k + 1, :] += _csum(accs[t])
        after()

    vec = lambda a: pl.BlockSpec(a.shape, lambda i: (0, 0))
    tile = lambda cb: pl.BlockSpec((ts, C), functools.partial(lambda i, cb: (i, cb), cb=cb))
    nxt = lambda cb: pl.BlockSpec(
        (HALO, C), functools.partial(lambda i, cb: (jnp.minimum((i + 1) * hb, nt * hb - 1), cb), cb=cb))
    prv = lambda cb: pl.BlockSpec(
        (HALO, C), functools.partial(lambda i, cb: (jnp.maximum(i * hb - 1, 0), cb), cb=cb))
    return _pcall(
        kern, name="conv_bwd", grid=(nt,),
        in_specs=[tile(0), tile(0), tile(acol), tile(bcol), nxt(0), nxt(0), prv(acol), prv(bcol),
                  vec(w_pad), vec(lg), vec(lb), ANY] + r_ispec,
        out_specs=[pl.BlockSpec((ts, 2 * C), lambda i: (i, dcol))]
        + [pl.BlockSpec(w_pad.shape, lambda i: (0, 0))] + [pl.BlockSpec((1, C), lambda i: (0, 0))] * 3 + r_ospec,
        out_shape=[jax.ShapeDtypeStruct(dbuf_hbm.shape, BF16)]
        + [jax.ShapeDtypeStruct(w_pad.shape, F32)] + [jax.ShapeDtypeStruct((1, C), F32)] * 3 + r_oshape,
        scratch_shapes=r_scratch + [pltpu.VMEM((ts + HALO, C), F32), pltpu.VMEM((HALO + ts, C), F32)]
        + [pltpu.VMEM((SUBLANES - 1, HALO + ts - SUBLANES, C), F32)] * 2,
        input_output_aliases={11: 0},
        compiler_params=_cparams(("arbitrary",)),
    )(du3, u1, proj, proj, du3, u1, proj, proj, w_pad, lg, lb, dbuf_hbm, *r_in)


def _gate_merge(proj, gacol, gbcol, ba, bb, S, D):
    def body(ga_ref, gb_ref, a_ref, b_ref, out_ref):
        out_ref[...] = (_sig(ga_ref[...]) * a_ref[...] + _sig(gb_ref[...]) * b_ref[...]).astype(BF16)
    return _rowcall("gate_merge", body, S, 512,
                    [(proj, D, gacol), (proj, D, gbcol), (ba, D, 0), (bb, D, 0)], [], [(D, BF16)], [])[0]


def _gate_bwd(dm, proj, gacol, gbcol, ba, bb, S, D, into):
    def body(dm_ref, ga_ref, gb_ref, a_ref, b_ref, dg_ref, da_ref, db_ref):
        dmv = dm_ref[...]
        sa, sb = _sig(ga_ref[...]), _sig(gb_ref[...])
        da_ref[...] = (dmv * sa).astype(BF16)
        db_ref[...] = (dmv * sb).astype(BF16)
        dg_ref[:, 0:D] = (dmv * a_ref[...] * sa * (1.0 - sa)).astype(BF16)
        dg_ref[:, D:2 * D] = (dmv * b_ref[...] * sb * (1.0 - sb)).astype(BF16)
    return _rowcall("gate_bwd", body, S, 512,
                    [(dm, D, 0), (proj, D, gacol), (proj, D, gbcol), (ba, D, 0), (bb, D, 0)], [],
                    [(2 * D, BF16), (D, BF16), (D, BF16)], [], into=into)


def _resid_norm2(x, mo, g1, g, sc, sh, S, D):
    def body(x_ref, mo_ref, g1_ref, g_ref, sc_ref, sh_ref, x1_ref, h_ref):
        x1 = x_ref[...] + g1_ref[...] * mo_ref[...]
        x1_ref[...] = x1
        r = lax.rsqrt(jnp.mean(x1 * x1, axis=-1, keepdims=True) + NORM_EPS)
        h_ref[...] = ((x1 * r * g_ref[...]) * (1.0 + sc_ref[...]) + sh_ref[...]).astype(BF16)
    return _rowcall("resid_norm2", body, S, 512, [(x, D, 0), (mo, D, 0)], [g1, g, sc, sh],
                    [(D, F32), (D, BF16)], [])


def _loss_dy(x1, ml, tgt, g2, S, D):
    def body(x1_ref, ml_ref, t_ref, g2_ref, dy_ref, dml_ref, sq_ref, dg2_ref):
        mlv = ml_ref[...]
        diff = x1_ref[...] + g2_ref[...] * mlv - t_ref[...]
        dy = diff * (1.0 / D)
        dy_ref[...] = dy
        dml_ref[...] = (dy * g2_ref[...]).astype(BF16)
        sq_ref[...] += _csum(diff * diff)
        dg2_ref[...] += _csum(dy * mlv)
    return _rowcall("loss_dy", body, S, 512, [(x1, D, 0), (ml, D, 0), (tgt, D, 0)], [g2],
                    [(D, F32), (D, BF16)], [(1, D), (1, D)])


def _norm_bwd(name, xin, dh, dres, g, sc, S, D, extra=None):
    def body(*refs):
        if extra is None:
            x_ref, dh_ref, dr_ref, g_ref, sc_ref, dx_ref, dsh_ref, dsc_ref, dg_ref = refs
        else:
            (x_ref, dh_ref, dr_ref, mo_ref, g_ref, sc_ref, g1_ref,
             dx_ref, dmo_ref, dsh_ref, dsc_ref, dg_ref, dg1_ref) = refs
        xv, dhv, gv = x_ref[...], dh_ref[...], g_ref[...]
        r = lax.rsqrt(jnp.mean(xv * xv, axis=-1, keepdims=True) + NORM_EPS)
        xh = xv * r
        dsh_ref[...] += _csum(dhv)
        dsc_ref[...] += _csum(dhv * xh * gv)
        dxg = dhv * (1.0 + sc_ref[...])
        dg_ref[...] += _csum(dxg * xh)
        dxh = dxg * gv
        dx = dr_ref[...] + r * (dxh - xh * jnp.mean(dxh * xh, axis=-1, keepdims=True))
        dx_ref[...] = dx
        if extra is not None:
            dmo_ref[...] = (dx * g1_ref[...]).astype(BF16)
            dg1_ref[...] += _csum(dx * mo_ref[...])

    rows = [(xin, D, 0), (dh, D, 0), (dres, D, 0)]
    vecs = [g, sc]
    if extra is None:
        return _rowcall(name, body, S, 512, rows, vecs, [(D, F32)], [(1, D)] * 3)
    return _rowcall(name, body, S, 512, rows + [(extra[0], D, 0)], vecs + [extra[1]],
                    [(D, F32), (D, BF16)], [(1, D)] * 4)


def _ada_fwd(c_all, w, b_part):
    B, D = c_all.shape
    N = w.shape[1]
    tn = min(512, N)

    def kern(c_ref, w_ref, b_ref, o_ref):
        cv = c_ref[...]
        ca = cv * _sig(cv)
        o_ref[...] = jnp.dot(ca, w_ref[...], precision=lax.Precision.HIGHEST,
                             preferred_element_type=F32) + b_ref[...]

    return _pcall(
        kern, name="ada_fwd", grid=(N // tn,),
        in_specs=[pl.BlockSpec((B, D), lambda j: (0, 0)), pl.BlockSpec((D, tn), lambda j: (0, j)),
                  pl.BlockSpec((1, tn), lambda j: (0, j))],
        out_specs=pl.BlockSpec((B, tn), lambda j: (0, j)),
        out_shape=jax.ShapeDtypeStruct((B, N), F32),
        compiler_params=_cparams(("parallel",)),
    )(c_all, w, b_part)


def _ada_wgrad(c_t_pad, dmod_pad):
    D = c_t_pad.shape[0]
    N = dmod_pad.shape[1]
    tn = min(512, N)

    def kern(c_ref, d_ref, o_ref):
        cv = c_ref[...]
        ca = cv * _sig(cv)
        o_ref[...] = jnp.dot(ca, d_ref[...], precision=lax.Precision.HIGHEST,
                             preferred_element_type=F32)

    return _pcall(
        kern, name="ada_wgrad", grid=(N // tn,),
        in_specs=[pl.BlockSpec((D, LANES), lambda j: (0, 0)), pl.BlockSpec((LANES, tn), lambda j: (0, j))],
        out_specs=pl.BlockSpec((D, tn), lambda j: (0, j)),
        out_shape=jax.ShapeDtypeStruct((D, N), F32),
        compiler_params=_cparams(("parallel",)),
    )(c_t_pad, dmod_pad)


def _ag_small(name, arrs):
    n = len(arrs)

    def kern(*refs):
        ins, outs = refs[:n], refs[n:2 * n]
        send, recv = refs[2 * n], refs[2 * n + 1]
        x, y, c = lax.axis_index("x"), lax.axis_index("y"), lax.axis_index("c")
        me = 4 * x + 2 * y + c

        def copy(i, m, slot):
            peer = (x ^ ((m >> 2) & 1), y ^ ((m >> 1) & 1), c ^ (m & 1))
            return pltpu.make_async_remote_copy(
                src_ref=ins[i], dst_ref=outs[i].at[slot],
                send_sem=send.at[i * 7 + m - 1], recv_sem=recv.at[i * 7 + m - 1],
                device_id=peer, device_id_type=MESH)

        for i in range(n):
            outs[i][me] = ins[i][...]
            for m in range(1, 8):
                copy(i, m, me).start()
        for i in range(n):
            for m in range(1, 8):
                copy(i, m, me).wait_send()
                copy(i, m, me ^ m).wait_recv()

    vm = pl.BlockSpec(memory_space=pltpu.VMEM)
    return _pcall(
        kern, name=name, in_specs=[vm] * n, out_specs=[vm] * n,
        out_shape=[jax.ShapeDtypeStruct((8,) + a.shape, a.dtype) for a in arrs],
        scratch_shapes=[pltpu.SemaphoreType.DMA((7 * n,)), pltpu.SemaphoreType.DMA((7 * n,))],
        compiler_params=pltpu.CompilerParams(has_side_effects=True),
    )(*arrs)


def _exchange(name, arrs, plan):
    out_shape, scratch, phases = plan(arrs)

    def kern(*refs):
        for phase in phases(refs):
            phase()

    return _pcall(
        kern, name=name, in_specs=[ANY] * len(arrs), out_specs=[ANY] * len(out_shape),
        out_shape=out_shape, scratch_shapes=scratch,
        compiler_params=pltpu.CompilerParams(has_side_effects=True),
    )(*arrs)


def _ride(plan_and_arrs, n_in, n_out):
    if plan_and_arrs is None:
        return [], [], [], [], [], lambda refs, first, middle, last: ((lambda: None), (lambda: None))
    plan, arrs = plan_and_arrs
    out_shape, scratch, phases = plan(arrs)
    na, no = len(arrs), len(out_shape)

    def hook(refs, first, middle, last):
        mine = refs[n_in:n_in + na] + refs[n_in + na + n_out:]
        start, mid, finish = phases(mine)

        def before():
            pl.when(first)(start)
            pl.when(middle)(mid)

        def after():
            pl.when(last)(finish)

        return before, after

    return list(arrs), [ANY] * na, [ANY] * no, out_shape, scratch, hook


def _gather_plan(arrs):
    n = len(arrs)

    def phases(refs):
        ins, outs = refs[:n], refs[n:2 * n]
        s1, r1, s2, r2, loc = refs[2 * n:2 * n + 5]
        x, y, c = lax.axis_index("x"), lax.axis_index("y"), lax.axis_index("c")
        me = 2 * x + y

        def half(i, hc):
            hr = ins[i].shape[0] // 2
            return pl.ds(hc * hr, hr)

        def own(i):
            return pltpu.make_async_remote_copy(
                src_ref=ins[i], dst_ref=outs[i].at[me], send_sem=loc.at[i], recv_sem=loc.at[n + i],
                device_id=(x, y, 1 - c), device_id_type=MESH)

        def fetch(i, m, slot):
            px, py = x ^ ((m >> 1) & 1), y ^ (m & 1)
            return pltpu.make_async_remote_copy(
                src_ref=ins[i].at[half(i, c)], dst_ref=outs[i].at[slot, half(i, c)],
                send_sem=s1.at[i * 3 + m - 1], recv_sem=r1.at[i * 3 + m - 1],
                device_id=(px, py, c), device_id_type=MESH)

        def passed(i, m, hc):
            return pltpu.make_async_remote_copy(
                src_ref=outs[i].at[me ^ m, half(i, hc)], dst_ref=outs[i].at[me ^ m, half(i, hc)],
                send_sem=s2.at[i * 3 + m - 1], recv_sem=r2.at[i * 3 + m - 1],
                device_id=(x, y, 1 - c), device_id_type=MESH)

        def start():
            for i in range(n):
                for m in range(1, 4):
                    fetch(i, m, me).start()
            for i in range(n):
                own(i).start()

        def mid():
            for i in range(n):
                for m in range(1, 4):
                    fetch(i, m, me ^ m).wait_recv()
                    passed(i, m, c).start()

        def finish():
            for i in range(n):
                own(i).wait()
                for m in range(1, 4):
                    fetch(i, m, me).wait_send()
                    passed(i, m, c).wait_send()
                    passed(i, m, 1 - c).wait_recv()

        return start, mid, finish

    out_shape = [jax.ShapeDtypeStruct((4,) + a.shape, a.dtype) for a in arrs]
    scratch = [pltpu.SemaphoreType.DMA((3 * n,))] * 4 + [pltpu.SemaphoreType.DMA((2 * n,))]
    return out_shape, scratch, phases


def _pair_halves_plan(arrs):
    n = len(arrs)

    def phases(refs):
        ins, outs = refs[:n], refs[n:2 * n]
        send, recv = refs[2 * n], refs[2 * n + 1]
        x, y, c = lax.axis_index("x"), lax.axis_index("y"), lax.axis_index("c")

        def copy(i, k, hc):
            return pltpu.make_async_remote_copy(
                src_ref=ins[i].at[k, hc], dst_ref=outs[i].at[k],
                send_sem=send.at[i * 4 + k], recv_sem=recv.at[i * 4 + k],
                device_id=(x, y, 1 - c), device_id_type=MESH)

        def start():
            for i in range(n):
                for k in range(4):
                    copy(i, k, 1 - c).start()

        def finish():
            for i in range(n):
                for k in range(4):
                    copy(i, k, 1 - c).wait()

        return start, (lambda: None), finish

    out_shape = [jax.ShapeDtypeStruct((4,) + a.shape[2:], a.dtype) for a in arrs]
    scratch = [pltpu.SemaphoreType.DMA((4 * n,)), pltpu.SemaphoreType.DMA((4 * n,))]
    return out_shape, scratch, phases


def _scatter_plan(arrs):
    n = len(arrs)

    def phases(refs):
        ins, outs = refs[:n], refs[n:2 * n]
        send, recv = refs[2 * n], refs[2 * n + 1]
        x, y, c = lax.axis_index("x"), lax.axis_index("y"), lax.axis_index("c")
        me = 2 * x + y

        def copy(i, m, slot):
            px, py = x ^ ((m >> 1) & 1), y ^ (m & 1)
            return pltpu.make_async_remote_copy(
                src_ref=ins[i].at[2 * px + py], dst_ref=outs[i].at[slot],
                send_sem=send.at[i * 3 + m - 1], recv_sem=recv.at[i * 3 + m - 1],
                device_id=(px, py, c), device_id_type=MESH)

        def start():
            for i in range(n):
                for m in range(1, 4):
                    copy(i, m, me).start()

        def finish():
            for i in range(n):
                for m in range(1, 4):
                    copy(i, m, me).wait_send()
                    copy(i, m, me ^ m).wait_recv()

        return start, (lambda: None), finish

    out_shape = [jax.ShapeDtypeStruct(a.shape, a.dtype) for a in arrs]
    scratch = [pltpu.SemaphoreType.DMA((3 * n,)), pltpu.SemaphoreType.DMA((3 * n,))]
    return out_shape, scratch, phases


def _pair_swap_plan(arrs):
    n = len(arrs)

    def phases(refs):
        ins, outs = refs[:n], refs[n:2 * n]
        send, recv = refs[2 * n], refs[2 * n + 1]
        x, y, c = lax.axis_index("x"), lax.axis_index("y"), lax.axis_index("c")

        def copy(i):
            return pltpu.make_async_remote_copy(
                src_ref=ins[i], dst_ref=outs[i], send_sem=send.at[i], recv_sem=recv.at[i],
                device_id=(x, y, 1 - c), device_id_type=MESH)

        def start():
            for i in range(n):
                copy(i).start()

        def finish():
            for i in range(n):
                copy(i).wait()

        return start, (lambda: None), finish

    out_shape = [jax.ShapeDtypeStruct(a.shape, a.dtype) for a in arrs]
    scratch = [pltpu.SemaphoreType.DMA((n,)), pltpu.SemaphoreType.DMA((n,))]
    return out_shape, scratch, phases


def _row_tile(R):
    for t in (256, 128, 64, 32, 16, 8):
        if R % t == 0:
            return t
    return R


def _sum_slots(name, parts):
    K, R, C = parts.shape
    tr = _row_tile(R)

    def kern(p_ref, o_ref):
        acc = p_ref[0].astype(F32)
        for k in range(1, K):
            acc = acc + p_ref[k].astype(F32)
        o_ref[...] = acc

    return _pcall(
        kern, name=name, grid=(R // tr,),
        in_specs=[pl.BlockSpec((K, tr, C), lambda i: (0, i, 0))],
        out_specs=pl.BlockSpec((tr, C), lambda i: (i, 0)),
        out_shape=jax.ShapeDtypeStruct((R, C), F32),
        compiler_params=_cparams(("parallel",)),
    )(parts)


def _sum_pair(name, core, mine, theirs):
    K, _, hr, C = mine.shape
    tr = _row_tile(hr)

    def kern(c_ref, a_ref, b_ref, o_ref):
        o_ref[0] = (a_ref[0, 0].astype(F32) + b_ref[0].astype(F32)).astype(BF16)

    return _pcall(
        kern, name=name, out_shape=jax.ShapeDtypeStruct((K, hr, C), BF16),
        grid_spec=pltpu.PrefetchScalarGridSpec(
            num_scalar_prefetch=1, grid=(K, hr // tr),
            in_specs=[pl.BlockSpec((1, 1, tr, C), lambda k, r, c_ref: (k, c_ref[0], r, 0)),
                      pl.BlockSpec((1, tr, C), lambda k, r, c_ref: (k, r, 0))],
            out_specs=pl.BlockSpec((1, tr, C), lambda k, r, c_ref: (k, r, 0))),
        compiler_params=_cparams(("parallel", "parallel")),
    )(core, mine, theirs)


def _sum_chips(name, chip, own, recv):
    K, hr, C = own.shape
    tr = _row_tile(hr)

    def kern(chip_ref, own_ref, *rest):
        r_refs, o_ref = rest[:K], rest[K]
        me = chip_ref[0]
        mine = own_ref[0].astype(F32)
        acc = None
        for k in range(K):
            t = jnp.where(me == k, mine, r_refs[k][0].astype(F32))
            acc = t if acc is None else acc + t
        o_ref[...] = acc

    def other(k):
        return pl.BlockSpec((1, tr, C), lambda r, s: (jnp.where(s[0] == k, (k + 1) % K, k), r, 0))

    return _pcall(
        kern, name=name, out_shape=jax.ShapeDtypeStruct((hr, C), F32),
        grid_spec=pltpu.PrefetchScalarGridSpec(
            num_scalar_prefetch=1, grid=(hr // tr,),
            in_specs=[pl.BlockSpec((1, tr, C), lambda r, s: (s[0], r, 0))] + [other(k) for k in range(K)],
            out_specs=pl.BlockSpec((tr, C), lambda r, s: (r, 0))),
        compiler_params=_cparams(("parallel",)),
    )(chip, own, *([recv] * K))


def _adam_update(w, m, v, g):
    c1 = 1.0 - ADAM_B1 ** ADAM_STEP
    c2 = 1.0 - ADAM_B2 ** ADAM_STEP
    mn = ADAM_B1 * m + (1.0 - ADAM_B1) * g
    vn = ADAM_B2 * v + (1.0 - ADAM_B2) * (g * g)
    return -ADAM_LR * ((mn / c1) / (jnp.sqrt(vn / c2) + ADAM_EPS) + ADAM_WD * w), mn, vn


def _adamw_halves(name, core, w, m, v, mine, theirs):
    R, C = w.shape
    hr = mine.shape[0]
    tr = _row_tile(hr)
    nbh = hr // tr

    def kern(c_ref, w_ref, m_ref, v_ref, a_ref, b_ref, go_ref, d_ref, mo_ref, vo_ref):
        g = jnp.where(pl.program_id(0) // nbh == c_ref[0], a_ref[...], b_ref[...])
        d, mn, vn = _adam_update(w_ref[...], m_ref[...], v_ref[...], g)
        go_ref[...] = g
        d_ref[...] = d
        mo_ref[...] = mn
        vo_ref[...] = vn

    spec = pl.BlockSpec((tr, C), lambda i, s: (i, 0))
    hspec = pl.BlockSpec((tr, C), lambda i, s: (i % nbh, 0))
    return _pcall(
        kern, name=name, out_shape=[jax.ShapeDtypeStruct((R, C), F32)] * 4,
        grid_spec=pltpu.PrefetchScalarGridSpec(
            num_scalar_prefetch=1, grid=(R // tr,),
            in_specs=[spec, spec, spec, hspec, hspec], out_specs=[spec] * 4),
        compiler_params=_cparams(("parallel",)),
    )(core, w, m, v, mine, theirs)


def _adamw_halves_t(name, core, w_t, m_t, v_t, mine_t, theirs_t):
    C, R = w_t.shape
    hr = mine_t.shape[1]
    tc = min(256, hr)
    nbh = hr // tc

    def kern(c_ref, w_ref, m_ref, v_ref, a_ref, b_ref, go_ref, d_ref, mo_ref, vo_ref):
        g = jnp.where(pl.program_id(0) // nbh == c_ref[0], a_ref[...], b_ref[...])
        d, mn, vn = _adam_update(w_ref[...], m_ref[...], v_ref[...], g)
        go_ref[...] = g
        d_ref[...] = d
        mo_ref[...] = mn
        vo_ref[...] = vn

    spec = pl.BlockSpec((C, tc), lambda j, s: (0, j))
    hspec = pl.BlockSpec((C, tc), lambda j, s: (0, j % nbh))
    return _pcall(
        kern, name=name, out_shape=[jax.ShapeDtypeStruct((C, R), F32)] * 4,
        grid_spec=pltpu.PrefetchScalarGridSpec(
            num_scalar_prefetch=1, grid=(R // tc,),
            in_specs=[spec, spec, spec, hspec, hspec], out_specs=[spec] * 4),
        compiler_params=_cparams(("parallel",)),
    )(core, w_t, m_t, v_t, mine_t, theirs_t)


def _adamw_many(name, wmvg):
    n = len(wmvg[0])

    def kern(*refs):
        ins, outs = refs[:4 * n], refs[4 * n:]
        for j in range(n):
            g = ins[3 * n + j][...]
            d, mn, vn = _adam_update(ins[j][...], ins[n + j][...], ins[2 * n + j][...], g)
            for i, val in enumerate((g, d, mn, vn)):
                outs[i * n + j][...] = val

    vm = pl.BlockSpec(memory_space=pltpu.VMEM)
    flat = [a for group in wmvg for a in group]
    outs = _pcall(
        kern, name=name, in_specs=[vm] * (4 * n), out_specs=[vm] * (4 * n),
        out_shape=[jax.ShapeDtypeStruct(a.shape, F32) for _ in range(4) for a in wmvg[0]],
    )(*flat)
    return [outs[i * n:(i + 1) * n] for i in range(4)]


def _adamw(name, w, m, v, gparts, ride=None):
    R, C = w.shape
    K = gparts.shape[0]
    tr = _row_tile(R)
    nr = R // tr
    r_in, r_ispec, r_ospec, r_oshape, r_scratch, r_hook = _ride(ride, 4, 4)

    def kern(*refs):
        w_ref, m_ref, v_ref, g_ref = refs[:4]
        go_ref, d_ref, mo_ref, vo_ref = refs[4 + len(r_in):8 + len(r_in)]
        i = pl.program_id(0)
        before, after = r_hook(refs, i == 0, i == nr // 2, i == nr - 1)
        before()
        g = g_ref[0]
        for k in range(1, K):
            g = g + g_ref[k]
        d, mn, vn = _adam_update(w_ref[...], m_ref[...], v_ref[...], g)
        go_ref[...] = g
        d_ref[...] = d
        mo_ref[...] = mn
        vo_ref[...] = vn
        after()

    spec = pl.BlockSpec((tr, C), lambda i: (i, 0))
    return _pcall(
        kern, name=name, grid=(nr,),
        in_specs=[spec, spec, spec, pl.BlockSpec((K, tr, C), lambda i: (0, i, 0))] + r_ispec,
        out_specs=[spec] * 4 + r_ospec,
        out_shape=[jax.ShapeDtypeStruct((R, C), F32)] * 4 + r_oshape,
        scratch_shapes=r_scratch,
        compiler_params=_cparams(("arbitrary",) if ride else ("parallel",)),
    )(w, m, v, gparts, *r_in)


def _round_up(a, b):
    return (a + b - 1) // b * b


def kernel(x, c, w_ada, b_ada, norm1_g, w_in, b_forget, q_norm_g, k_norm_g, w_attn_proj, conv_w, conv_b, conv_ln_g, conv_ln_b, w_conv_proj, w_out, norm2_g, w_mlp1, w_mlp2, loss_target, m_w_ada, m_b_ada, m_norm1_g, m_w_in, m_b_forget, m_q_norm_g, m_k_norm_g, m_w_attn_proj, m_conv_w, m_conv_b, m_conv_ln_g, m_conv_ln_b, m_w_conv_proj, m_w_out, m_norm2_g, m_w_mlp1, m_w_mlp2, v_w_ada, v_b_ada, v_norm1_g, v_w_in, v_b_forget, v_q_norm_g, v_k_norm_g, v_w_attn_proj, v_conv_w, v_conv_b, v_conv_ln_g, v_conv_ln_b, v_w_conv_proj, v_w_out, v_norm2_g, v_w_mlp1, v_w_mlp2):
    S, D = x.shape[1], x.shape[2]
    NH, HD = b_forget.shape[-1], q_norm_g.shape[-1]
    TAPS = conv_w.shape[1]
    DIN_S = w_in.shape[-1]
    DIN = 4 * DIN_S
    DFF_S = w_mlp1.shape[-1]
    DFF = 4 * DFF_S
    ADA_S = w_ada.shape[-1]
    DS = w_attn_proj.shape[1]
    CS = conv_w.shape[-1]
    assert NH * HD == D and DIN == 7 * D + NH and TAPS - 1 <= HALO and D % LANES == 0 and 2 * HD == LANES
    NP = _round_up(7 * D + LANES, 512)
    FW = NP - 7 * D
    assert (7 * D) % FW == 0
    TQ = min(512, S)
    NQ = S // TQ
    FCOL = 7 * D // LANES

    xi, yi, ci = lax.axis_index("x"), lax.axis_index("y"), lax.axis_index("c")
    chip = 2 * xi + yi
    dev = 4 * xi + 2 * yi + ci

    x2 = x.reshape(S, D)
    tgt = loss_target.reshape(S, D)

    lane_head = jnp.arange(D, dtype=jnp.int32) // HD
    grp = (lane_head[:, None] == jnp.arange(LANES, dtype=jnp.int32)[None, :]).astype(BF16)
    grp_t = grp.T
    sel = ((jnp.arange(D, dtype=jnp.int32)[:, None] == HD * jnp.arange(LANES, dtype=jnp.int32)[None, :])
           .astype(BF16))
    ch = min(256, S)
    ii = jnp.arange(ch, dtype=jnp.int32)
    tri = (ii[None, :] <= ii[:, None]).astype(BF16)
    tri_u = tri.T
    gq_t = jnp.tile(q_norm_g.reshape(1, HD), (1, NH))
    gk_t = jnp.tile(k_norm_g.reshape(1, HD), (1, NH))
    bf_pad = jnp.pad(b_forget.reshape(1, NH), ((0, 0), (0, LANES - NH)))

    c_all, cw_all = _ag_small(
        "ag_c_convw", [c.reshape(1, D), jnp.pad(conv_w.reshape(TAPS, CS), ((0, HALO - TAPS), (0, 0)))])
    c_all = c_all.reshape(8, D)
    b_part = lax.dynamic_slice(b_ada.reshape(1, -1), (0, chip * ADA_S), (1, ADA_S))
    mod_part = _ada_fwd(c_all, w_ada.reshape(D, ADA_S), b_part)
    (mod_all,) = _ag_small("ag_mod", [mod_part])
    mod_full = jnp.concatenate([mod_all[0], mod_all[2], mod_all[4], mod_all[6]], axis=1)
    mod = lax.dynamic_slice(mod_full, (dev, 0), (1, 6 * D))
    sh1, sc1, g1, sh2, sc2, g2 = [mod[:, i * D:(i + 1) * D] for i in range(6)]

    shards = [w_in.reshape(D, DIN_S), w_attn_proj.reshape(DS, D), w_conv_proj.reshape(DS, D),
              w_out.reshape(DS, D), w_mlp1.reshape(D, DFF_S), w_mlp2.reshape(DFF_S, D)]
    shards = [s.astype(BF16) for s in shards]
    (gw_in,) = _exchange("ag_w_in", shards[:1], _gather_plan)
    w_conv = jnp.concatenate([cw_all[0], cw_all[2], cw_all[4], cw_all[6]], axis=1)

    SEGS = [(0, 2 * D, 0), (3 * D + NH, DIN, 2 * D), (2 * D, 3 * D + NH, 6 * D)]

    def pieces(a, b):
        out = []
        for k in range(4):
            lo, hi = max(a, k * DIN_S), min(b, (k + 1) * DIN_S)
            if lo < hi:
                out.append(gw_in[k][:, lo - k * DIN_S:hi - k * DIN_S])
        return out

    w_in_p = jnp.concatenate([p for (a, b, _) in SEGS for p in pieces(a, b)]
                             + [jnp.zeros((D, NP - 7 * D - NH), BF16)], axis=1)

    n1g = norm1_g.reshape(1, D)
    n2g = norm2_g.reshape(1, D)
    h = _norm_mod("norm_mod1", x2, n1g, sc1, sh1, S, D)
    proj = _mm("mm_in", h, w_in_p, "nn", [F32], tn=1536)
    qs, kn, vb = _qk_prep(proj, 6, gq_t, gk_t, grp, grp_t, S, D, HD)
    f_cum = _fgate_fwd(proj, FCOL, bf_pad, tri, S)
    fk_c = f_cum[:, :NH]
    fk_r = fk_c.T.reshape(NH, NQ, 1, TQ)
    o, o32, lse_b, gw_ap, gw_cp, gw_out, gw_m1, gw_m2 = _flash_fwd(
        qs, kn, vb, fk_r, S, D, HD, TQ, ride=(_gather_plan, shards[1:]))
    w_ap = gw_ap.reshape(D, D)
    w_cp = gw_cp.reshape(D, D)
    w_o = gw_out.reshape(D, D)
    w_m1 = jnp.transpose(gw_m1, (1, 0, 2)).reshape(D, DFF)
    w_m2 = gw_m2.reshape(DFF, D)
    br_a = _mm("mm_attn_proj", o, w_ap, "nn", [F32])
    cb, clg, clb = conv_b.reshape(1, D), conv_ln_g.reshape(1, D), conv_ln_b.reshape(1, D)
    u1, u3 = _conv_fwd(proj, 2, 3, w_conv, cb, clg, clb, S, D, TAPS, 256)
    br_b = _mm("mm_conv_proj", u3, w_cp, "nn", [F32])
    merged = _gate_merge(proj, 4, 5, br_a, br_b, S, D)
    mo = _mm("mm_out", merged, w_o, "nn", [F32])
    x1, h2 = _resid_norm2(x2, mo, g1, n2g, sc2, sh2, S, D)

    def relu2(r):
        rp = jnp.maximum(r, 0.0)
        return (rp * rp,)
    z = _mm("mm_mlp1", h2, w_m1, "nn", [BF16], epi=relu2)
    ml = _mm("mm_mlp2", z, w_m2, "nn", [F32])
    dy, dml, sq, dg2 = _loss_dy(x1, ml, tgt, g2, S, D)
    loss_part = jnp.full((1, LANES), 0.5 * jnp.sum(sq) / D, F32)

    da = _mm("mm_dz", dml, w_m2, "nt", [BF16], epi=lambda r, zz: (r * 2.0 * jnp.sqrt(zz.astype(F32)),),
             extras=(z,))
    dw_m2 = _mm("mm_dw_mlp2", z, dml, "tn", [BF16])
    dw_m1 = _mm("mm_dw_mlp1", h2, da, "tn", [BF16])
    dh2 = _mm("mm_dh2", da, w_m1, "nt", [F32])
    dx1, dmo, dsh2, dsc2, dn2g, dg1 = _norm_bwd("norm2_bwd", x1, dh2, dy, n2g, sc2, S, D, extra=(mo, g1))
    dmerged = _mm("mm_dmerged", dmo, w_o, "nt", [F32])
    dw_o = _mm("mm_dw_out", merged, dmo, "tn", [BF16])
    dproj, dba, dbb = _gate_bwd(dmerged, proj, 4, 5, br_a, br_b, S, D, into=(lax.empty((S, NP), BF16), 2))
    do = _mm("mm_do", dba, w_ap, "nt", [BF16])
    dw_ap = _mm("mm_dw_attn_proj", o, dba, "tn", [BF16])
    du3 = _mm("mm_du3", dbb, w_cp, "nt", [F32])
    dw_cp = _mm("mm_dw_conv_proj", u3, dbb, "tn", [BF16])
    core = ci.astype(jnp.int32).reshape(1)
    chip1 = chip.astype(jnp.int32).reshape(1)
    halves = lambda p: p.astype(BF16).reshape(4, 2, p.shape[1] // 2, p.shape[2])
    names = ["w_in", "w_attn_proj", "w_conv_proj", "w_out", "w_mlp1", "w_mlp2"]
    parts = [halves(p) for p in (dw_ap.reshape(4, DS, D), dw_cp.reshape(4, DS, D), dw_o.reshape(4, DS, D),
                                 jnp.transpose(dw_m1.reshape(D, 4, DFF_S), (1, 0, 2)), dw_m2.reshape(4, DFF_S, D))]
    dproj, dcw, dcb, dclg, dclb, *theirs = _conv_bwd(
        du3, u1, proj, 2, 3, w_conv, clg, clb, S, D, TAPS, 256, into=(dproj, 1), ride=(_pair_halves_plan, parts))
    chip_parts =[_sum_pair("sum_pair_" + nm, core, p, t) for nm, p, t in zip(names[1:], parts, theirs)]

    delta_c, lse_c = _delta_prep(do, o32, lse_b, grp, sel, S, D)
    to_rows = lambda t: t[:, :NH].T.reshape(NH, NQ, 1, TQ)
    dkn, dproj, dqs, dfq_r, dfk_b, *recvd = _flash_bwd(
        qs, kn, vb, do, f_cum, to_rows(lse_c), to_rows(delta_c), S, D, HD, TQ,
        dv_into=(dproj, 6 * D // LANES), ride=(_scatter_plan, chip_parts))
    dproj, sq_q, sq_k = _qk_bwd(proj, dqs, dkn, gq_t, gk_t, grp, grp_t, S, D, HD, into=(dproj, 0))
    to_cols = lambda r: jnp.pad(r.reshape(NH, S).T, ((0, 0), (0, LANES - NH)))
    dfq_pad = to_cols(dfq_r)
    dfk_pad = _pick_heads(dfk_b, sel, S, D)
    dproj, dbf = _fgate_bwd(dfk_pad, dfq_pad, proj, FCOL, bf_pad, tri_u, NH, S, FW, into=(dproj, 7 * D // FW))
    dw_in_p = _mm("mm_dw_in", h, dproj, "tn", [BF16])
    def shard_cols(k):
        out = []
        for (a, b, start) in sorted(SEGS):
            lo, hi = max(a, k * DIN_S), min(b, (k + 1) * DIN_S)
            if lo < hi:
                out.append(dw_in_p[:, start + lo - a:start + hi - a])
        return jnp.concatenate(out, axis=1)

    part_in = halves(jnp.stack([shard_cols(k) for k in range(4)]))
    (their_in,) = _exchange("rs_pair_w_in", [part_in], _pair_halves_plan)
    chip_in = _sum_pair("sum_pair_w_in", core, part_in, their_in)
    dh, recv_in = _mm("mm_dh", dproj, w_in_p, "nt", [F32], ride=(_scatter_plan, [chip_in]))
    gx, dsh1, dsc1, dn1g = _norm_bwd("norm1_bwd", x2, dh, dx1, n1g, sc1, S, D)

    packed = jnp.concatenate([dsh1, dsc1, dg1, dsh2, dsc2, dg2, dn1g, dcb, dclg, dclb, dn2g,
                              sq_q, sq_k, dbf, loss_part], axis=1)
    small_all, dcw_all = _ag_small("ag_small_grads", [packed, dcw])
    small = _sum_slots("sum_small", small_all.reshape(8, 1, -1)).reshape(1, -1)
    dmod_sum = small[:, :6 * D]
    seg = lambda k: small[:, (6 + k) * D:(7 + k) * D]
    g_n1g, g_cb, g_clg, g_clb, g_n2g = seg(0), seg(1), seg(2), seg(3), seg(4)
    g_qn = _sum_slots("sum_qn", seg(5).reshape(NH, 1, HD))
    g_kn = _sum_slots("sum_kn", seg(6).reshape(NH, 1, HD))
    g_bf = small[:, 13 * D:13 * D + NH]
    loss = small[0, 13 * D + LANES]
    dcw_mine = lax.dynamic_slice(dcw_all[:, :TAPS, :], (0, 0, chip * CS), (8, TAPS, CS))

    dmod_all = small_all.reshape(8, -1)[:, :6 * D]
    dmod_cols = lax.dynamic_slice(dmod_all, (0, chip * ADA_S), (8, ADA_S))
    c_t_pad = jnp.pad(c_all.T, ((0, 0), (0, LANES - 8)))
    g_wada = _ada_wgrad(c_t_pad, jnp.pad(dmod_cols, ((0, LANES - 8), (0, 0))))

    sums =[_sum_chips("sum_" + nm, chip1, p, r)
            for nm, p, r in zip(names, [chip_in] + chip_parts, [recv_in] + list(recvd))]

    res = {}
    outs = _adamw("adamw_w_ada", w_ada.reshape(D, ADA_S), m_w_ada.reshape(D, ADA_S),
                  v_w_ada.reshape(D, ADA_S), g_wada.reshape(1, D, ADA_S), ride=(_pair_swap_plan, sums))
    res["w_ada"] = [t.reshape(w_ada.shape) for t in outs[:4]]
    big = {nm: (a, b) for nm, a, b in zip(names, sums, outs[4:])}
    big_w = {"w_in": (w_in, m_w_in, v_w_in), "w_attn_proj": (w_attn_proj, m_w_attn_proj, v_w_attn_proj),
             "w_conv_proj": (w_conv_proj, m_w_conv_proj, v_w_conv_proj), "w_out": (w_out, m_w_out, v_w_out),
             "w_mlp1": (w_mlp1, m_w_mlp1, v_w_mlp1), "w_mlp2": (w_mlp2, m_w_mlp2, v_w_mlp2)}
    for nm in names:
        shp = big_w[nm][0].shape
        if shp[2] % LANES:
            outs = _adamw_halves_t("adamw_" + nm, core, *[t.reshape(shp[1], shp[2]).T for t in big_w[nm]],
                                   *[t.T for t in big[nm]])
            res[nm] = [t.T.reshape(shp) for t in outs]
        else:
            outs = _adamw_halves("adamw_" + nm, core, *[t.reshape(shp[1], shp[2]) for t in big_w[nm]], *big[nm])
            res[nm] = [t.reshape(shp) for t in outs]
    outs = _adamw("adamw_conv_w", conv_w.reshape(TAPS, CS), m_conv_w.reshape(TAPS, CS),
                  v_conv_w.reshape(TAPS, CS), dcw_mine)
    res["conv_w"] = [t.reshape(conv_w.shape) for t in outs]

    small_w = [("b_ada", b_ada, m_b_ada, v_b_ada, dmod_sum), ("norm1_g", norm1_g, m_norm1_g, v_norm1_g, g_n1g),
               ("b_forget", b_forget, m_b_forget, v_b_forget, g_bf),
               ("q_norm_g", q_norm_g, m_q_norm_g, v_q_norm_g, g_qn),
               ("k_norm_g", k_norm_g, m_k_norm_g, v_k_norm_g, g_kn),
               ("conv_b", conv_b, m_conv_b, v_conv_b, g_cb), ("conv_ln_g", conv_ln_g, m_conv_ln_g, v_conv_ln_g, g_clg),
               ("conv_ln_b", conv_ln_b, m_conv_ln_b, v_conv_ln_b, g_clb),
               ("norm2_g", norm2_g, m_norm2_g, v_norm2_g, g_n2g)]
    outs = _adamw_many("adamw_small", [[t[i].reshape(1, -1) for t in small_w] for i in (1, 2, 3, 4)])
    for j, (nm, w_, _, _, _) in enumerate(small_w):
        res[nm] = [outs[i][j].reshape(w_.shape) for i in range(4)]

    order = ["w_ada", "b_ada", "norm1_g", "w_in", "b_forget", "q_norm_g", "k_norm_g", "w_attn_proj", "conv_w",
             "conv_b", "conv_ln_g", "conv_ln_b", "w_conv_proj", "w_out", "norm2_g", "w_mlp1", "w_mlp2"]
    return (loss, gx.reshape(x.shape), *[res[n][0] for n in order], *[res[n][1] for n in order],
            *[res[n][2] for n in order], *[res[n][3] for n in order])
```

```python
import functools

import jax
import jax.numpy as jnp
from jax import lax
from jax.experimental import pallas as pl
from jax.experimental.pallas import tpu as pltpu

F32 = jnp.float32
BF16 = jnp.bfloat16
MESH = pl.DeviceIdType.MESH
ANY = pl.BlockSpec(memory_space=pl.ANY)

NORM_EPS = 1e-6
ADAM_LR = 0.001
ADAM_B1 = 0.9
ADAM_B2 = 0.999
ADAM_EPS = 1e-08
ADAM_WD = 0.01
ADAM_STEP = 10
LANES = 128
SUBLANES = 8
HALO = 32
CONV_ROWS = 32
CONV_TAPS = 4
NEG = -1e30
VMEM_LIMIT = 56 * 1024 * 1024


def _pcall(body, **kw):
    return pl.pallas_call(body, **kw)


def _cparams(sem=None):
    if sem is None:
        return pltpu.CompilerParams(vmem_limit_bytes=VMEM_LIMIT)
    return pltpu.CompilerParams(dimension_semantics=sem, vmem_limit_bytes=VMEM_LIMIT)


def _sig(x):
    return 1.0 / (1.0 + jnp.exp(-x))


def _split3(x):
    x1 = x.astype(BF16)
    r = x - x1.astype(F32)
    x2 = r.astype(BF16)
    x3 = (r - x2.astype(F32)).astype(BF16)
    return x1, x2, x3


def _dot_rs(x, e, terms=3):
    out = None
    for t in _split3(x)[:terms]:
        d = jnp.dot(t, e, preferred_element_type=F32)
        out = d if out is None else out + d
    return out


def _dot_ls(e, x):
    out = None
    for t in _split3(x):
        d = jnp.dot(e, t, preferred_element_type=F32)
        out = d if out is None else out + d
    return out


def _tile(n, want):
    if n <= want:
        return n
    t = want - want % LANES
    while n % t:
        t -= LANES
    assert t > 0, (n, want)
    return t


_DIMS = {"nn": ((1,), (0,)), "nt": ((1,), (1,)), "tn": ((0,), (0,))}


def _mm(name, a, b, mode, out_dtypes, epi=None, extras=(), tm=1024, tn=1024, tk=4096, ride=None):
    if mode == "nn":
        (M, K), (_, N) = a.shape, b.shape
    elif mode == "nt":
        (M, K), (N, _) = a.shape, b.shape
    else:
        (K, M), (_, N) = a.shape, b.shape
    tm, tn, tk = _tile(M, tm), _tile(N, tn), _tile(K, tk)
    nm, nn, nk = M // tm, N // tn, K // tk
    ne, no = len(extras), len(out_dtypes)
    dims = (_DIMS[mode], ((), ()))
    r_in, r_ispec, r_ospec, r_oshape, r_scratch, r_hook = _ride(ride, 2 + ne, no)

    def kern(*refs):
        a_ref, b_ref = refs[0], refs[1]
        e_refs = refs[2:2 + ne]
        o_refs = refs[2 + ne + len(r_in):2 + ne + len(r_in) + no]
        i, j, k = pl.program_id(0), pl.program_id(1), pl.program_id(2)
        before, after = r_hook(refs, (i == 0) & (j == 0) & (k == 0), (i == nm // 2) & (j == 0) & (k == 0),
                               (i == nm - 1) & (j == nn - 1) & (k == nk - 1))
        before()
        d = lax.dot_general(a_ref[...], b_ref[...], dims, preferred_element_type=F32)

        def finish(r):
            outs = (r,) if epi is None else epi(r, *[e[...] for e in e_refs])
            for o_ref, o in zip(o_refs, outs):
                o_ref[...] = o.astype(o_ref.dtype)

        if nk == 1:
            finish(d)
        else:
            acc = refs[-1]

            @pl.when(k == 0)
            def _():
                acc[...] = d

            @pl.when((k > 0) & (k < nk - 1))
            def _():
                acc[...] += d

            @pl.when(k == nk - 1)
            def _():
                finish(acc[...] + d)
        after()

    if mode == "tn":
        a_spec = pl.BlockSpec((tk, tm), lambda i, j, k: (k, i))
    else:
        a_spec = pl.BlockSpec((tm, tk), lambda i, j, k: (i, k))
    if mode == "nt":
        b_spec = pl.BlockSpec((tn, tk), lambda i, j, k: (j, k))
    else:
        b_spec = pl.BlockSpec((tk, tn), lambda i, j, k: (k, j))
    mn_spec = pl.BlockSpec((tm, tn), lambda i, j, k: (i, j))
    outs = _pcall(
        kern, name=name, grid=(nm, nn, nk),
        in_specs=[a_spec, b_spec] + [mn_spec] * ne + r_ispec,
        out_specs=[mn_spec] * no + r_ospec,
        out_shape=[jax.ShapeDtypeStruct((M, N), dt) for dt in out_dtypes] + r_oshape,
        scratch_shapes=r_scratch + ([pltpu.VMEM((tm, tn), F32)] if nk > 1 else []),
        compiler_params=_cparams(("arbitrary",) * 3 if ride else ("parallel", "parallel", "arbitrary")),
    )(a, b, *extras, *r_in)
    return outs[0] if len(outs) == 1 else outs


def _rowcall(name, body, S, ts, row_ins, vec_ins, row_outs, vec_outs, into=None):
    ts = min(ts, S)
    nri, nvi, nro, nvo = len(row_ins), len(vec_ins), len(row_outs), len(vec_outs)
    na = 0 if into is None else 1

    def kern(*refs):
        ins = refs[:nri + nvi]
        outs = refs[nri + nvi + na:]
        if nvo:
            @pl.when(pl.program_id(0) == 0)
            def _():
                for r in outs[nro:]:
                    r[...] = jnp.zeros(r.shape, r.dtype)
        body(*ins, *outs)

    in_specs = [pl.BlockSpec((ts, w), functools.partial(lambda i, cb: (i, cb), cb=cb))
                for (_, w, cb) in row_ins]
    in_specs += [pl.BlockSpec(v.shape, lambda i: (0, 0)) for v in vec_ins]
    out_specs = [pl.BlockSpec((ts, w), lambda i: (i, 0)) for (w, _) in row_outs]
    out_specs += [pl.BlockSpec((r, w), lambda i: (0, 0)) for (r, w) in vec_outs]
    out_shape = [jax.ShapeDtypeStruct((S, w), dt) for (w, dt) in row_outs]
    out_shape += [jax.ShapeDtypeStruct((r, w), F32) for (r, w) in vec_outs]
    extra, aliases = [], {}
    if into is not None:
        buf, cb = into
        assert buf.dtype == row_outs[0][1] and buf.shape[0] == S
        in_specs.append(ANY)
        out_specs[0] = pl.BlockSpec((ts, row_outs[0][0]), lambda i: (i, cb))
        out_shape[0] = jax.ShapeDtypeStruct(buf.shape, buf.dtype)
        extra, aliases = [buf], {nri + nvi: 0}
    return _pcall(
        kern, name=name, grid=(S // ts,), in_specs=in_specs, out_specs=out_specs,
        out_shape=out_shape, input_output_aliases=aliases,
        compiler_params=_cparams(("arbitrary",) if nvo else ("parallel",)),
    )(*[a for (a, _, _) in row_ins], *vec_ins, *extra)


def _csum(x):
    return jnp.sum(x, axis=0, keepdims=True)


def _norm_mod(name, x, g, sc, sh, S, D):
    def body(x_ref, g_ref, sc_ref, sh_ref, h_ref):
        xv = x_ref[...]
        r = lax.rsqrt(jnp.mean(xv * xv, axis=-1, keepdims=True) + NORM_EPS)
        h_ref[...] = ((xv * r * g_ref[...]) * (1.0 + sc_ref[...]) + sh_ref[...]).astype(BF16)
    return _rowcall(name, body, S, 1024, [(x, D, 0)], [g, sc, sh], [(D, BF16)], [])[0]


def _head_rstd(v, grp, grp_t, hd):
    ss = _dot_rs(v * v, grp, 1) * (1.0 / hd)
    r = lax.rsqrt(ss + NORM_EPS)
    return _dot_rs(r, grp_t, 2)


def _qk_prep(proj, vcol, gq, gk, grp, grp_t, S, D, hd):
    scale = hd ** -0.5

    def body(q_ref, k_ref, v_ref, gq_ref, gk_ref, g_ref, gt_ref, qs_ref, kn_ref, vb_ref):
        q = q_ref[...]
        k = k_ref[...]
        rq = _head_rstd(q, g_ref[...], gt_ref[...], hd)
        rk = _head_rstd(k, g_ref[...], gt_ref[...], hd)
        qs_ref[...] = ((q * rq * gq_ref[...]).astype(BF16).astype(F32) * scale).astype(BF16)
        kn_ref[...] = (k * rk * gk_ref[...]).astype(BF16)
        vb_ref[...] = v_ref[...].astype(BF16)

    return _rowcall("qk_prep", body, S, 512, [(proj, D, 0), (proj, D, 1), (proj, D, vcol)],
                    [gq, gk, grp, grp_t], [(D, BF16)] * 3, [])


def _fgate_fwd(proj, fcol, bf_pad, tri, S):
    ch = tri.shape[0]

    def body(f_ref, b_ref, tri_ref, out_ref):
        carry = jnp.zeros((1, LANES), F32)
        for c in range(S // ch):
            z = f_ref[c * ch:(c + 1) * ch, :] + b_ref[...]
            lf = jnp.minimum(z, 0.0) - jnp.log(1.0 + jnp.exp(-jnp.abs(z)))
            out_ref[c * ch:(c + 1) * ch, :] = _dot_ls(tri_ref[...], lf) + carry
            carry = carry + _csum(lf)

    return _rowcall("fgate_fwd", body, S, S, [(proj, LANES, fcol)], [bf_pad, tri],
                    [(LANES, F32)], [])[0]


def _fgate_bwd(dfk, dfq, proj, fcol, bf_pad, tri_u, nh, S, fw, into):
    ch = tri_u.shape[0]

    def body(d_ref, dq_ref, f_ref, b_ref, tri_ref, df_ref, db_ref):
        if fw > LANES:
            df_ref[:, LANES:fw] = jnp.zeros((S, fw - LANES), BF16)
        lane = lax.broadcasted_iota(jnp.int32, (ch, LANES), 1)
        carry = jnp.zeros((1, LANES), F32)
        tot = jnp.zeros((1, LANES), F32)
        for c in reversed(range(S // ch)):
            d = d_ref[c * ch:(c + 1) * ch, :] + dq_ref[c * ch:(c + 1) * ch, :]
            rc = _dot_ls(tri_ref[...], d) + carry
            carry = carry + _csum(d)
            z = f_ref[c * ch:(c + 1) * ch, :] + b_ref[...]
            df = jnp.where(lane < nh, rc * _sig(-z), 0.0)
            df_ref[c * ch:(c + 1) * ch, 0:LANES] = df.astype(BF16)
            tot = tot + _csum(df)
        db_ref[...] += tot

    return _rowcall("fgate_bwd", body, S, S, [(dfk, LANES, 0), (dfq, LANES, 0), (proj, LANES, fcol)],
                    [bf_pad, tri_u], [(fw, BF16)], [(1, LANES)], into=into)


def _keep(v, mask):
    return jnp.where(mask, v.astype(F32), 0.0).astype(BF16)


def _lane_col(blk, lane, at):
    return jnp.sum(jnp.where(lane == at, blk, 0.0), axis=-1, keepdims=True)


def _flash_fwd(qs, kn, vb, fk_r, S, D, hd, tq, ride=None):
    hp, nq = D // LANES, S // tq
    r_in, r_ispec, r_ospec, r_oshape, r_scratch, r_hook = _ride(ride, 4, 3)

    def kern(*refs):
        q_ref, k_ref, v_ref, fk_ref = refs[:4]
        o_ref, o32_ref, lse_ref = refs[4 + len(r_in):7 + len(r_in)]
        hi, qi = pl.program_id(0), pl.program_id(1)
        before, after = r_hook(refs, (hi == 0) & (qi == 0), (hi == hp // 2) & (qi == 0),
                               (hi == hp - 1) & (qi == nq - 1))
        before()
        lane = lax.broadcasted_iota(jnp.int32, (tq, LANES), 1)
        row = lax.broadcasted_iota(jnp.int32, (tq, tq), 0)
        col = lax.broadcasted_iota(jnp.int32, (tq, tq), 1)
        hms = [(lane >= j * hd) & (lane < (j + 1) * hd) for j in range(2)]
        q = q_ref[...]
        qms = [_keep(q, hm) for hm in hms]

        s_a, s_b, v1_sc = refs[-3], refs[-2], refs[-1]

        @pl.when(qi == 0)
        def _():
            for c0 in range(0, S, tq):
                vf = v_ref[c0:c0 + tq, :].astype(F32)
                for j in range(2):
                    v1_sc[j, c0:c0 + tq, :] = jnp.where(hms[j], vf, 1.0).astype(BF16)

        def put(s_ref, ki):
            off = pl.multiple_of(ki * tq, tq)
            k = k_ref[pl.ds(off, tq), :]
            for j in range(2):
                s_ref[j] = lax.dot_general(qms[j], k, (((1,), (1,)), ((), ())), preferred_element_type=F32)

        def update(ki, s_ref, state, masked):
            off = pl.multiple_of(ki * tq, tq)
            new = []
            for j in range(2):
                m_old, acc = state[j]
                s = s_ref[j] - fk_ref[j, ki]
                if masked:
                    s = jnp.where(col <= row, s, NEG)
                m_new = jnp.maximum(m_old, jnp.max(s, axis=-1, keepdims=True))
                alpha = jnp.exp(m_old - m_new)
                p = jnp.exp(s - m_new)
                acc = alpha * acc + jnp.dot(p.astype(BF16), v1_sc[j, pl.ds(off, tq), :],
                                            preferred_element_type=F32)
                new.append((m_new, acc))
            return tuple(new)

        def pair(p, state):
            put(s_b, 2 * p + 1)
            state = update(2 * p, s_a, state, False)
            put(s_a, 2 * p + 2)
            return update(2 * p + 1, s_b, state, False)

        def odd_tail(state):
            put(s_b, qi)
            return update(qi, s_b, update(qi - 1, s_a, state, False), True)

        init = tuple((jnp.full((tq, 1), NEG, F32), jnp.zeros((tq, LANES), F32)) for _ in range(2))
        put(s_a, 0)
        state = lax.fori_loop(0, qi // 2, pair, init)
        (m0, a0), (m1, a1) = lax.cond(qi % 2 == 1, odd_tail, lambda st: update(qi, s_a, st, True), state)
        l0, l1 = pltpu.roll(a0, hd, 1), pltpu.roll(a1, hd, 1)
        first = lane < hd
        ov = jnp.where(first, a0 / l0, a1 / l1)
        o_ref[...] = ov.astype(BF16)
        o32_ref[...] = ov
        lse_ref[...] = jnp.where(first, m0 + jnp.log(l0), m1 + jnp.log(l1))
        after()

    qspec = pl.BlockSpec((tq, LANES), lambda h, i: (i, h))
    fullspec = pl.BlockSpec((S, LANES), lambda h, i: (0, h))
    return _pcall(
        kern, name="flash_fwd", grid=(hp, nq),
        in_specs=[qspec, fullspec, fullspec,
                  pl.BlockSpec((2, nq, 1, tq), lambda h, i: (h, 0, 0, 0))] + r_ispec,
        out_specs=[qspec, qspec, qspec] + r_ospec,
        out_shape=[jax.ShapeDtypeStruct((S, D), BF16), jax.ShapeDtypeStruct((S, D), F32),
                   jax.ShapeDtypeStruct((S, D), F32)] + r_oshape,
        scratch_shapes=r_scratch + [pltpu.VMEM((2, tq, tq), F32)] * 2 + [pltpu.VMEM((2, S, LANES), BF16)],
        compiler_params=_cparams(("arbitrary", "arbitrary")),
    )(qs, kn, vb, fk_r, *r_in)


def _flash_bwd(qs, kn, vb, do, fk_b, lse_r, delta_r, S, D, hd, tq, dv_into, ride=None):
    hp, nq = D // LANES, S // tq
    dbuf_hbm, dv_col = dv_into
    r_in, r_ispec, r_ospec, r_oshape, r_scratch, r_hook = _ride(ride, 8, 5)

    def kern(*refs):
        q_ref, do_ref, k_ref, v_ref, fk_ref, lse_ref, dl_ref = refs[:7]
        dk_ref, dv_ref, dq_ref, dfq_ref, dfk_ref = refs[8 + len(r_in):13 + len(r_in)]
        hi, ki = pl.program_id(0), pl.program_id(1)
        before, after = r_hook(refs, (hi == 0) & (ki == 0), (hi == hp // 2) & (ki == 0),
                               (hi == hp - 1) & (ki == nq - 1))
        before()
        lane = lax.broadcasted_iota(jnp.int32, (tq, LANES), 1)
        row = lax.broadcasted_iota(jnp.int32, (tq, tq), 0)
        col = lax.broadcasted_iota(jnp.int32, (tq, tq), 1)
        hms = [(lane >= j * hd) & (lane < (j + 1) * hd) for j in range(2)]
        k = k_ref[...]
        v = v_ref[...]
        fkb = fk_ref[...]
        kms = [_keep(k, hm) for hm in hms]
        vms = [_keep(v, hm) for hm in hms]
        fks = [_lane_col(fkb, lane, 2 * hi + j) for j in range(2)]

        @pl.when(ki == 0)
        def _():
            dfq_ref[...] = jnp.zeros(dfq_ref.shape, F32)
            dq_ref[...] = jnp.zeros(dq_ref.shape, F32)

        def step(qi, acc, masked):
            dk, dv, dfs = acc
            off = pl.multiple_of(qi * tq, tq)
            q = q_ref[pl.ds(off, tq), :]
            g = do_ref[pl.ds(off, tq), :]
            dq = None
            new_dfs = []
            for j in range(2):
                qm = _keep(q, hms[j])
                gm = _keep(g, hms[j])
                st = lax.dot_general(kms[j], q, (((1,), (1,)), ((), ())), preferred_element_type=F32)
                st = st - fks[j]
                if masked:
                    st = jnp.where(row <= col, st, NEG)
                pt = jnp.exp(st - lse_ref[j, qi])
                dv = dv + jnp.dot(pt.astype(BF16), gm, preferred_element_type=F32)
                dpt = lax.dot_general(vms[j], g, (((1,), (1,)), ((), ())), preferred_element_type=F32)
                dst = pt * (dpt - dl_ref[j, qi])
                dsb = dst.astype(BF16)
                dk = dk + jnp.dot(dsb, qm, preferred_element_type=F32)
                t = lax.dot_general(dsb, kms[j], (((0,), (0,)), ((), ())), preferred_element_type=F32)
                dq = t if dq is None else dq + t
                dfq_ref[j, qi] += jnp.sum(dst, axis=0, keepdims=True)
                new_dfs.append(dfs[j] - jnp.sum(dst, axis=1, keepdims=True))
            dq_ref[pl.ds(off, tq), :] += dq
            return dk, dv, tuple(new_dfs)

        zero = jnp.zeros((tq, LANES), F32)
        zcol = jnp.zeros((tq, 1), F32)
        acc = step(ki, (zero, zero, (zcol, zcol)), True)
        dk, dv, dfs = lax.fori_loop(ki + 1, nq, lambda qi, a: step(qi, a, False), acc)
        dk_ref[...] = dk.astype(BF16)
        dv_ref[...] = dv.astype(BF16)
        dfk_ref[...] = jnp.where(lane < hd, dfs[0], dfs[1])
        after()

    kspec = pl.BlockSpec((tq, LANES), lambda h, i: (i, h))
    fullspec = pl.BlockSpec((S, LANES), lambda h, i: (0, h))
    rowspec = pl.BlockSpec((2, nq, 1, tq), lambda h, i: (h, 0, 0, 0))
    return _pcall(
        kern, name="flash_bwd", grid=(hp, nq),
        in_specs=[fullspec, fullspec, kspec, kspec, pl.BlockSpec((tq, LANES), lambda h, i: (i, 0)),
                  rowspec, rowspec, ANY] + r_ispec,
        out_specs=[kspec, pl.BlockSpec((tq, LANES), lambda h, i: (i, h + dv_col)), fullspec, rowspec, kspec]
        + r_ospec,
        out_shape=[jax.ShapeDtypeStruct((S, D), BF16), jax.ShapeDtypeStruct(dbuf_hbm.shape, BF16),
                   jax.ShapeDtypeStruct((S, D), F32), jax.ShapeDtypeStruct((2 * hp, nq, 1, tq), F32),
                   jax.ShapeDtypeStruct((S, D), F32)] + r_oshape,
        scratch_shapes=r_scratch, input_output_aliases={7: 1},
        compiler_params=_cparams(("arbitrary", "arbitrary")),
    )(qs, do, kn, vb, fk_b, lse_r, delta_r, dbuf_hbm, *r_in)


def _delta_prep(do, o, lse_b, grp, sel, S, D):
    def body(g_ref, o_ref, l_ref, e_ref, s_ref, dl_ref, lse_ref):
        prod = g_ref[...].astype(F32) * o_ref[...]
        dl_ref[...] = _dot_rs(prod, e_ref[...], 2)
        lse_ref[...] = _dot_rs(l_ref[...], s_ref[...])
    return _rowcall("delta_prep", body, S, 512, [(do, D, 0), (o, D, 0), (lse_b, D, 0)], [grp, sel],
                    [(LANES, F32), (LANES, F32)], [])


def _pick_heads(x_b, sel, S, D):
    def body(x_ref, s_ref, o_ref):
        o_ref[...] = _dot_rs(x_ref[...], s_ref[...])
    return _rowcall("pick_heads", body, S, 512, [(x_b, D, 0)], [sel], [(LANES, F32)], [])[0]


def _qk_bwd(proj, dqs, dkn, gq, gk, grp, grp_t, S, D, hd, into):
    scale = hd ** -0.5

    def one(x, dn, gain, e, et):
        r = _head_rstd(x, e, et, hd)
        xh = x * r
        t = dn * gain
        mean = _dot_rs(_dot_rs(t * xh, e, 1), et, 2) * (1.0 / hd)
        return r * (t - xh * mean), _csum(dn * xh)

    def body(q_ref, k_ref, dq_ref, dk_ref, gq_ref, gk_ref, e_ref, et_ref, o_ref, sq_ref, sk_ref):
        e, et = e_ref[...], et_ref[...]
        dq, sq = one(q_ref[...], dq_ref[...].astype(F32) * scale, gq_ref[...], e, et)
        dk, sk = one(k_ref[...], dk_ref[...].astype(F32), gk_ref[...], e, et)
        o_ref[:, 0:D] = dq.astype(BF16)
        o_ref[:, D:2 * D] = dk.astype(BF16)
        sq_ref[...] += sq
        sk_ref[...] += sk

    return _rowcall("qk_bwd", body, S, 512,
                    [(proj, D, 0), (proj, D, 1), (dqs, D, 0), (dkn, D, 0)],
                    [gq, gk, grp, grp_t], [(2 * D, BF16)], [(1, D)] * 2, into=into)


def _shift_copies(buf, sh, ts):
    for b in range(1, SUBLANES):
        sh[b - 1] = buf[b:b + ts + HALO - SUBLANES, :]


def _rows_from(buf, sh, o, ts):
    a, b = divmod(o, SUBLANES)
    if b == 0:
        return buf[o:o + ts, :]
    return sh[b - 1, SUBLANES * a:SUBLANES * a + ts, :]


def _conv_fwd(proj, acol, bcol, w_pad, cb, lg, lb, S, C, taps, ts):
    ts = min(ts, S)

    def kern(a_ref, b_ref, w_ref, cb_ref, lg_ref, lb_ref, u1_ref, u3_ref, ubuf, ush):
        @pl.when(pl.program_id(0) == 0)
        def _():
            ubuf[0:HALO, :] = jnp.zeros((HALO, C), F32)

        ubuf[HALO:HALO + ts, :] = a_ref[...] * _sig(b_ref[...])
        _shift_copies(ubuf, ush, ts)
        acc = jnp.zeros((ts, C), F32) + cb_ref[...]
        for k in range(taps):
            acc = acc + w_ref[k:k + 1, :] * _rows_from(ubuf, ush, HALO - (taps - 1) + k, ts)
        u1_ref[...] = acc
        mu = jnp.mean(acc, axis=-1, keepdims=True)
        xc = acc - mu
        rstd = lax.rsqrt(jnp.mean(xc * xc, axis=-1, keepdims=True) + NORM_EPS)
        u2 = xc * rstd * lg_ref[...] + lb_ref[...]
        u3_ref[...] = (u2 * _sig(u2)).astype(BF16)
        ubuf[0:HALO, :] = ubuf[ts:ts + HALO, :]

    vec = lambda a: pl.BlockSpec(a.shape, lambda i: (0, 0))
    return _pcall(
        kern, name="conv_fwd", grid=(S // ts,),
        in_specs=[pl.BlockSpec((ts, C), lambda i: (i, acol)), pl.BlockSpec((ts, C), lambda i: (i, bcol)),
                  vec(w_pad), vec(cb), vec(lg), vec(lb)],
        out_specs=[pl.BlockSpec((ts, C), lambda i: (i, 0))] * 2,
        out_shape=[jax.ShapeDtypeStruct((S, C), F32), jax.ShapeDtypeStruct((S, C), BF16)],
        scratch_shapes=[pltpu.VMEM((HALO + ts, C), F32),
                        pltpu.VMEM((SUBLANES - 1, HALO + ts - SUBLANES, C), F32)],
        compiler_params=_cparams(("arbitrary",)),
    )(proj, proj, w_pad, cb, lg, lb)


def _conv_bwd(du3, u1, proj, acol, bcol, w_pad, lg, lb, S, C, taps, ts, into, ride=None):
    ts = min(ts, S)
    dbuf_hbm, dcol = into
    r_in, r_ispec, r_ospec, r_oshape, r_scratch, r_hook = _ride(ride, 12, 5)
    nt = S // ts
    hb = ts // HALO

    def ln_bwd(g, u, lgv, lbv):
        mu = jnp.mean(u, axis=-1, keepdims=True)
        xc = u - mu
        rstd = lax.rsqrt(jnp.mean(xc * xc, axis=-1, keepdims=True) + NORM_EPS)
        xh = xc * rstd
        u2 = xh * lgv + lbv
        s = _sig(u2)
        du2 = g * (s + u2 * s * (1.0 - s))
        dxh = du2 * lgv
        du1 = rstd * (dxh - jnp.mean(dxh, axis=-1, keepdims=True)
                      - xh * jnp.mean(dxh * xh, axis=-1, keepdims=True))
        return du1, du2, xh

    def kern(*refs):
        g_ref, u_ref, a_ref, b_ref, gn_ref, un_ref, ap_ref, bp_ref, w_ref, lg_ref, lb_ref = refs[:11]
        dg_ref, dw_ref, dcb_ref, dlg_ref, dlb_ref = refs[12 + len(r_in):17 + len(r_in)]
        dbuf, ubuf, dsh, ush = refs[-4:]
        i = pl.program_id(0)
        before, after = r_hook(refs, i == 0, i == nt // 2, i == nt - 1)
        before()

        @pl.when(i == 0)
        def _():
            dw_ref[...] = jnp.zeros(dw_ref.shape, F32)
            dcb_ref[...] = jnp.zeros(dcb_ref.shape, F32)
            dlg_ref[...] = jnp.zeros(dlg_ref.shape, F32)
            dlb_ref[...] = jnp.zeros(dlb_ref.shape, F32)

        lgv, lbv = lg_ref[...], lb_ref[...]
        du1, du2, xh = ln_bwd(g_ref[...], u_ref[...], lgv, lbv)
        dbuf[0:ts, :] = du1
        du1n, _, _ = ln_bwd(gn_ref[...], un_ref[...], lgv, lbv)
        dbuf[ts:ts + HALO, :] = jnp.where(i < nt - 1, du1n, 0.0)
        a = a_ref[...]
        sb = _sig(b_ref[...])
        ubuf[HALO:HALO + ts, :] = a * sb
        ubuf[0:HALO, :] = jnp.where(i > 0, ap_ref[...] * _sig(bp_ref[...]), 0.0)
        dcb_ref[...] += _csum(du1)
        dlg_ref[...] += _csum(du2 * xh)
        dlb_ref[...] += _csum(du2)
        _shift_copies(dbuf, dsh, ts)
        _shift_copies(ubuf, ush, ts)
        for r0 in range(0, ts, CONV_ROWS):
            du0 = jnp.zeros((CONV_ROWS, C), F32)
            for k in range(taps):
                du0 = du0 + w_ref[k:k + 1, :] * _rows_from(dbuf, dsh, r0 + taps - 1 - k, CONV_ROWS)
            ac = a_ref[r0:r0 + CONV_ROWS, :]
            sc = _sig(b_ref[r0:r0 + CONV_ROWS, :])
            dg_ref[r0:r0 + CONV_ROWS, 0:C] = (du0 * sc).astype(BF16)
            dg_ref[r0:r0 + CONV_ROWS, C:2 * C] = (du0 * ac * sc * (1.0 - sc)).astype(BF16)
        for k0 in range(0, taps, CONV_TAPS):
            ks = range(k0, min(k0 + CONV_TAPS, taps))
            accs = [jnp.zeros((SUBLANES, C), F32) for _ in ks]
            for r0 in range(0, ts, CONV_ROWS):
                d = dbuf[r0:r0 + CONV_ROWS, :]
                for t, k in enumerate(ks):
                    prod = d * _rows_from(ubuf, ush, r0 + HALO - (taps - 1) + k, CONV_ROWS)
                    accs[t] = accs[t] + jnp.sum(prod.reshape(CONV_ROWS // SUBLANES, SUBLANES, C), axis=0)
            for t, k in enumerate(ks):
                dw_ref[k:k + 1, :] += _csum(accs[t])
        after()

    vec = lambda a: pl.BlockSpec(a.shape, lambda i: (0, 0))
    tile = lambda cb: pl.BlockSpec((ts, C), functools.partial(lambda i, cb: (i, cb), cb=cb))
    nxt = lambda cb: pl.BlockSpec(
        (HALO, C), functools.partial(lambda i, cb: (jnp.minimum((i + 1) * hb, nt * hb - 1), cb), cb=cb))
    prv = lambda cb: pl.BlockSpec(
        (HALO, C), functools.partial(lambda i, cb: (jnp.maximum(i * hb - 1, 0), cb), cb=cb))
    return _pcall(
        kern, name="conv_bwd", grid=(nt,),
        in_specs=[tile(0), tile(0), tile(acol), tile(bcol), nxt(0), nxt(0), prv(acol), prv(bcol),
                  vec(w_pad), vec(lg), vec(lb), ANY] + r_ispec,
        out_specs=[pl.BlockSpec((ts, 2 * C), lambda i: (i, dcol))]
        + [pl.BlockSpec(w_pad.shape, lambda i: (0, 0))] + [pl.BlockSpec((1, C), lambda i: (0, 0))] * 3 + r_ospec,
        out_shape=[jax.ShapeDtypeStruct(dbuf_hbm.shape, BF16)]
        + [jax.ShapeDtypeStruct(w_pad.shape, F32)] + [jax.ShapeDtypeStruct((1, C), F32)] * 3 + r_oshape,
        scratch_shapes=r_scratch + [pltpu.VMEM((ts + HALO, C), F32), pltpu.VMEM((HALO + ts, C), F32)]
        + [pltpu.VMEM((SUBLANES - 1, HALO + ts - SUBLANES, C), F32)] * 2,
        input_output_aliases={11: 0},
        compiler_params=_cparams(("arbitrary",)),
    )(du3, u1, proj, proj, du3, u1, proj, proj, w_pad, lg, lb, dbuf_hbm, *r_in)


def _gate_merge(proj, gacol, gbcol, ba, bb, S, D):
    def body(ga_ref, gb_ref, a_ref, b_ref, out_ref):
        out_ref[...] = (_sig(ga_ref[...]) * a_ref[...] + _sig(gb_ref[...]) * b_ref[...]).astype(BF16)
    return _rowcall("gate_merge", body, S, 1024,
                    [(proj, D, gacol), (proj, D, gbcol), (ba, D, 0), (bb, D, 0)], [], [(D, BF16)], [])[0]


def _gate_bwd(dm, proj, gacol, gbcol, ba, bb, S, D, into):
    def body(dm_ref, ga_ref, gb_ref, a_ref, b_ref, dg_ref, da_ref, db_ref):
        dmv = dm_ref[...]
        sa, sb = _sig(ga_ref[...]), _sig(gb_ref[...])
        da_ref[...] = (dmv * sa).astype(BF16)
        db_ref[...] = (dmv * sb).astype(BF16)
        dg_ref[:, 0:D] = (dmv * a_ref[...] * sa * (1.0 - sa)).astype(BF16)
        dg_ref[:, D:2 * D] = (dmv * b_ref[...] * sb * (1.0 - sb)).astype(BF16)
    return _rowcall("gate_bwd", body, S, 512,
                    [(dm, D, 0), (proj, D, gacol), (proj, D, gbcol), (ba, D, 0), (bb, D, 0)], [],
                    [(2 * D, BF16), (D, BF16), (D, BF16)], [], into=into)


def _resid_norm2(x, mo, g1, g, sc, sh, S, D):
    def body(x_ref, mo_ref, g1_ref, g_ref, sc_ref, sh_ref, x1_ref, h_ref):
        x1 = x_ref[...] + g1_ref[...] * mo_ref[...]
        x1_ref[...] = x1
        r = lax.rsqrt(jnp.mean(x1 * x1, axis=-1, keepdims=True) + NORM_EPS)
        h_ref[...] = ((x1 * r * g_ref[...]) * (1.0 + sc_ref[...]) + sh_ref[...]).astype(BF16)
    return _rowcall("resid_norm2", body, S, 1024, [(x, D, 0), (mo, D, 0)], [g1, g, sc, sh],
                    [(D, F32), (D, BF16)], [])


def _loss_dy(x1, ml, tgt, g2, S, D):
    def body(x1_ref, ml_ref, t_ref, g2_ref, dy_ref, dml_ref, sq_ref, dg2_ref):
        mlv = ml_ref[...]
        diff = x1_ref[...] + g2_ref[...] * mlv - t_ref[...]
        dy = diff * (1.0 / D)
        dy_ref[...] = dy
        dml_ref[...] = (dy * g2_ref[...]).astype(BF16)
        sq_ref[...] += _csum(diff * diff)
        dg2_ref[...] += _csum(dy * mlv)
    return _rowcall("loss_dy", body, S, 512, [(x1, D, 0), (ml, D, 0), (tgt, D, 0)], [g2],
                    [(D, F32), (D, BF16)], [(1, D), (1, D)])


def _norm_bwd(name, xin, dh, dres, g, sc, S, D, extra=None):
    def body(*refs):
        if extra is None:
            x_ref, dh_ref, dr_ref, g_ref, sc_ref, dx_ref, dsh_ref, dsc_ref, dg_ref = refs
        else:
            (x_ref, dh_ref, dr_ref, mo_ref, g_ref, sc_ref, g1_ref,
             dx_ref, dmo_ref, dsh_ref, dsc_ref, dg_ref, dg1_ref) = refs
        xv, dhv, gv = x_ref[...], dh_ref[...], g_ref[...]
        r = lax.rsqrt(jnp.mean(xv * xv, axis=-1, keepdims=True) + NORM_EPS)
        xh = xv * r
        dsh_ref[...] += _csum(dhv)
        dsc_ref[...] += _csum(dhv * xh * gv)
        dxg = dhv * (1.0 + sc_ref[...])
        dg_ref[...] += _csum(dxg * xh)
        dxh = dxg * gv
        dx = dr_ref[...] + r * (dxh - xh * jnp.mean(dxh * xh, axis=-1, keepdims=True))
        dx_ref[...] = dx
        if extra is not None:
            dmo_ref[...] = (dx * g1_ref[...]).astype(BF16)
            dg1_ref[...] += _csum(dx * mo_ref[...])

    rows = [(xin, D, 0), (dh, D, 0), (dres, D, 0)]
    vecs = [g, sc]
    if extra is None:
        return _rowcall(name, body, S, 512, rows, vecs, [(D, F32)], [(1, D)] * 3)
    return _rowcall(name, body, S, 512, rows + [(extra[0], D, 0)], vecs + [extra[1]],
                    [(D, F32), (D, BF16)], [(1, D)] * 4)


def _ada_fwd(c_all, w, b_part):
    B, D = c_all.shape
    N = w.shape[1]
    tn = min(512, N)

    def kern(c_ref, w_ref, b_ref, o_ref):
        cv = c_ref[...]
        ca = cv * _sig(cv)
        o_ref[...] = jnp.dot(ca, w_ref[...], precision=lax.Precision.HIGHEST,
                             preferred_element_type=F32) + b_ref[...]

    return _pcall(
        kern, name="ada_fwd", grid=(N // tn,),
        in_specs=[pl.BlockSpec((B, D), lambda j: (0, 0)), pl.BlockSpec((D, tn), lambda j: (0, j)),
                  pl.BlockSpec((1, tn), lambda j: (0, j))],
        out_specs=pl.BlockSpec((B, tn), lambda j: (0, j)),
        out_shape=jax.ShapeDtypeStruct((B, N), F32),
        compiler_params=_cparams(("parallel",)),
    )(c_all, w, b_part)


def _ada_wgrad(c_t_pad, dmod_pad):
    D = c_t_pad.shape[0]
    N = dmod_pad.shape[1]
    tn = min(512, N)

    def kern(c_ref, d_ref, o_ref):
        cv = c_ref[...]
        ca = cv * _sig(cv)
        o_ref[...] = jnp.dot(ca, d_ref[...], precision=lax.Precision.HIGHEST,
                             preferred_element_type=F32)

    return _pcall(
        kern, name="ada_wgrad", grid=(N // tn,),
        in_specs=[pl.BlockSpec((D, LANES), lambda j: (0, 0)), pl.BlockSpec((LANES, tn), lambda j: (0, j))],
        out_specs=pl.BlockSpec((D, tn), lambda j: (0, j)),
        out_shape=jax.ShapeDtypeStruct((D, N), F32),
        compiler_params=_cparams(("parallel",)),
    )(c_t_pad, dmod_pad)


def _ag_small(name, arrs):
    n = len(arrs)

    def kern(*refs):
        ins, outs = refs[:n], refs[n:2 * n]
        send, recv = refs[2 * n], refs[2 * n + 1]
        x, y, c = lax.axis_index("x"), lax.axis_index("y"), lax.axis_index("c")
        me = 4 * x + 2 * y + c

        def copy(i, m, slot):
            peer = (x ^ ((m >> 2) & 1), y ^ ((m >> 1) & 1), c ^ (m & 1))
            return pltpu.make_async_remote_copy(
                src_ref=ins[i], dst_ref=outs[i].at[slot],
                send_sem=send.at[i * 7 + m - 1], recv_sem=recv.at[i * 7 + m - 1],
                device_id=peer, device_id_type=MESH)

        for i in range(n):
            outs[i][me] = ins[i][...]
            for m in range(1, 8):
                copy(i, m, me).start()
        for i in range(n):
            for m in range(1, 8):
                copy(i, m, me).wait_send()
                copy(i, m, me ^ m).wait_recv()

    vm = pl.BlockSpec(memory_space=pltpu.VMEM)
    return _pcall(
        kern, name=name, in_specs=[vm] * n, out_specs=[vm] * n,
        out_shape=[jax.ShapeDtypeStruct((8,) + a.shape, a.dtype) for a in arrs],
        scratch_shapes=[pltpu.SemaphoreType.DMA((7 * n,)), pltpu.SemaphoreType.DMA((7 * n,))],
        compiler_params=pltpu.CompilerParams(has_side_effects=True),
    )(*arrs)


def _exchange(name, arrs, plan):
    out_shape, scratch, phases = plan(arrs)

    def kern(*refs):
        for phase in phases(refs):
            phase()

    return _pcall(
        kern, name=name, in_specs=[ANY] * len(arrs), out_specs=[ANY] * len(out_shape),
        out_shape=out_shape, scratch_shapes=scratch,
        compiler_params=pltpu.CompilerParams(has_side_effects=True),
    )(*arrs)


def _ride(plan_and_arrs, n_in, n_out):
    if plan_and_arrs is None:
        return [], [], [], [], [], lambda refs, first, middle, last: ((lambda: None), (lambda: None))
    plan, arrs = plan_and_arrs
    out_shape, scratch, phases = plan(arrs)
    na, no = len(arrs), len(out_shape)

    def hook(refs, first, middle, last):
        mine = refs[n_in:n_in + na] + refs[n_in + na + n_out:]
        start, mid, finish = phases(mine)

        def before():
            pl.when(first)(start)
            pl.when(middle)(mid)

        def after():
            pl.when(last)(finish)

        return before, after

    return list(arrs), [ANY] * na, [ANY] * no, out_shape, scratch, hook


def _gather_plan(arrs):
    n = len(arrs)

    def phases(refs):
        ins, outs = refs[:n], refs[n:2 * n]
        s1, r1, s2, r2, loc = refs[2 * n:2 * n + 5]
        x, y, c = lax.axis_index("x"), lax.axis_index("y"), lax.axis_index("c")
        me = 2 * x + y

        def half(i, hc):
            hr = ins[i].shape[0] // 2
            return pl.ds(hc * hr, hr)

        def own(i):
            return pltpu.make_async_remote_copy(
                src_ref=ins[i], dst_ref=outs[i].at[me], send_sem=loc.at[i], recv_sem=loc.at[n + i],
                device_id=(x, y, 1 - c), device_id_type=MESH)

        def fetch(i, m, slot):
            px, py = x ^ ((m >> 1) & 1), y ^ (m & 1)
            return pltpu.make_async_remote_copy(
                src_ref=ins[i].at[half(i, c)], dst_ref=outs[i].at[slot, half(i, c)],
                send_sem=s1.at[i * 3 + m - 1], recv_sem=r1.at[i * 3 + m - 1],
                device_id=(px, py, c), device_id_type=MESH)

        def passed(i, m, hc):
            return pltpu.make_async_remote_copy(
                src_ref=outs[i].at[me ^ m, half(i, hc)], dst_ref=outs[i].at[me ^ m, half(i, hc)],
                send_sem=s2.at[i * 3 + m - 1], recv_sem=r2.at[i * 3 + m - 1],
                device_id=(x, y, 1 - c), device_id_type=MESH)

        def start():
            for i in range(n):
                for m in range(1, 4):
                    fetch(i, m, me).start()
            for i in range(n):
                own(i).start()

        def mid():
            for i in range(n):
                for m in range(1, 4):
                    fetch(i, m, me ^ m).wait_recv()
                    passed(i, m, c).start()

        def finish():
            for i in range(n):
                own(i).wait()
                for m in range(1, 4):
                    fetch(i, m, me).wait_send()
                    passed(i, m, c).wait_send()
                    passed(i, m, 1 - c).wait_recv()

        return start, mid, finish

    out_shape = [jax.ShapeDtypeStruct((4,) + a.shape, a.dtype) for a in arrs]
    scratch = [pltpu.SemaphoreType.DMA((3 * n,))] * 4 + [pltpu.SemaphoreType.DMA((2 * n,))]
    return out_shape, scratch, phases


def _pair_halves_plan(arrs):
    n = len(arrs)

    def phases(refs):
        ins, outs = refs[:n], refs[n:2 * n]
        send, recv = refs[2 * n], refs[2 * n + 1]
        x, y, c = lax.axis_index("x"), lax.axis_index("y"), lax.axis_index("c")

        def copy(i, k, hc):
            return pltpu.make_async_remote_copy(
                src_ref=ins[i].at[k, hc], dst_ref=outs[i].at[k],
                send_sem=send.at[i * 4 + k], recv_sem=recv.at[i * 4 + k],
                device_id=(x, y, 1 - c), device_id_type=MESH)

        def start():
            for i in range(n):
                for k in range(4):
                    copy(i, k, 1 - c).start()

        def finish():
            for i in range(n):
                for k in range(4):
                    copy(i, k, 1 - c).wait()

        return start, (lambda: None), finish

    out_shape = [jax.ShapeDtypeStruct((4,) + a.shape[2:], a.dtype) for a in arrs]
    scratch = [pltpu.SemaphoreType.DMA((4 * n,)), pltpu.SemaphoreType.DMA((4 * n,))]
    return out_shape, scratch, phases


def _scatter_plan(arrs):
    n = len(arrs)

    def phases(refs):
        ins, outs = refs[:n], refs[n:2 * n]
        send, recv = refs[2 * n], refs[2 * n + 1]
        x, y, c = lax.axis_index("x"), lax.axis_index("y"), lax.axis_index("c")
        me = 2 * x + y

        def copy(i, m, slot):
            px, py = x ^ ((m >> 1) & 1), y ^ (m & 1)
            return pltpu.make_async_remote_copy(
                src_ref=ins[i].at[2 * px + py], dst_ref=outs[i].at[slot],
                send_sem=send.at[i * 3 + m - 1], recv_sem=recv.at[i * 3 + m - 1],
                device_id=(px, py, c), device_id_type=MESH)

        def start():
            for i in range(n):
                for m in range(1, 4):
                    copy(i, m, me).start()

        def finish():
            for i in range(n):
                for m in range(1, 4):
                    copy(i, m, me).wait_send()
                    copy(i, m, me ^ m).wait_recv()

        return start, (lambda: None), finish

    out_shape = [jax.ShapeDtypeStruct(a.shape, a.dtype) for a in arrs]
    scratch = [pltpu.SemaphoreType.DMA((3 * n,)), pltpu.SemaphoreType.DMA((3 * n,))]
    return out_shape, scratch, phases


def _pair_swap_plan(arrs):
    n = len(arrs)

    def phases(refs):
        ins, outs = refs[:n], refs[n:2 * n]
        send, recv = refs[2 * n], refs[2 * n + 1]
        x, y, c = lax.axis_index("x"), lax.axis_index("y"), lax.axis_index("c")

        def copy(i):
            return pltpu.make_async_remote_copy(
                src_ref=ins[i], dst_ref=outs[i], send_sem=send.at[i], recv_sem=recv.at[i],
                device_id=(x, y, 1 - c), device_id_type=MESH)

        def start():
            for i in range(n):
                copy(i).start()

        def finish():
            for i in range(n):
                copy(i).wait()

        return start, (lambda: None), finish

    out_shape = [jax.ShapeDtypeStruct(a.shape, a.dtype) for a in arrs]
    scratch = [pltpu.SemaphoreType.DMA((n,)), pltpu.SemaphoreType.DMA((n,))]
    return out_shape, scratch, phases


def _row_tile(R):
    for t in (256, 128, 64, 32, 16, 8):
        if R % t == 0:
            return t
    return R


def _sum_slots(name, parts):
    K, R, C = parts.shape
    tr = _row_tile(R)

    def kern(p_ref, o_ref):
        acc = p_ref[0].astype(F32)
        for k in range(1, K):
            acc = acc + p_ref[k].astype(F32)
        o_ref[...] = acc

    return _pcall(
        kern, name=name, grid=(R // tr,),
        in_specs=[pl.BlockSpec((K, tr, C), lambda i: (0, i, 0))],
        out_specs=pl.BlockSpec((tr, C), lambda i: (i, 0)),
        out_shape=jax.ShapeDtypeStruct((R, C), F32),
        compiler_params=_cparams(("parallel",)),
    )(parts)


def _sum_pair(name, core, mine, theirs):
    K, _, hr, C = mine.shape
    tr = _row_tile(hr)

    def kern(c_ref, a_ref, b_ref, o_ref):
        o_ref[0] = (a_ref[0, 0].astype(F32) + b_ref[0].astype(F32)).astype(BF16)

    return _pcall(
        kern, name=name, out_shape=jax.ShapeDtypeStruct((K, hr, C), BF16),
        grid_spec=pltpu.PrefetchScalarGridSpec(
            num_scalar_prefetch=1, grid=(K, hr // tr),
            in_specs=[pl.BlockSpec((1, 1, tr, C), lambda k, r, c_ref: (k, c_ref[0], r, 0)),
                      pl.BlockSpec((1, tr, C), lambda k, r, c_ref: (k, r, 0))],
            out_specs=pl.BlockSpec((1, tr, C), lambda k, r, c_ref: (k, r, 0))),
        compiler_params=_cparams(("parallel", "parallel")),
    )(core, mine, theirs)


def _sum_chips(name, chip, own, recv):
    K, hr, C = own.shape
    tr = _row_tile(hr)

    def kern(chip_ref, own_ref, *rest):
        r_refs, o_ref = rest[:K], rest[K]
        me = chip_ref[0]
        mine = own_ref[0].astype(F32)
        acc = None
        for k in range(K):
            t = jnp.where(me == k, mine, r_refs[k][0].astype(F32))
            acc = t if acc is None else acc + t
        o_ref[...] = acc

    def other(k):
        return pl.BlockSpec((1, tr, C), lambda r, s: (jnp.where(s[0] == k, (k + 1) % K, k), r, 0))

    return _pcall(
        kern, name=name, out_shape=jax.ShapeDtypeStruct((hr, C), F32),
        grid_spec=pltpu.PrefetchScalarGridSpec(
            num_scalar_prefetch=1, grid=(hr // tr,),
            in_specs=[pl.BlockSpec((1, tr, C), lambda r, s: (s[0], r, 0))] + [other(k) for k in range(K)],
            out_specs=pl.BlockSpec((tr, C), lambda r, s: (r, 0))),
        compiler_params=_cparams(("parallel",)),
    )(chip, own, *([recv] * K))


def _adam_update(w, m, v, g):
    c1 = 1.0 - ADAM_B1 ** ADAM_STEP
    c2 = 1.0 - ADAM_B2 ** ADAM_STEP
    mn = ADAM_B1 * m + (1.0 - ADAM_B1) * g
    vn = ADAM_B2 * v + (1.0 - ADAM_B2) * (g * g)
    return -ADAM_LR * ((mn / c1) / (jnp.sqrt(vn / c2) + ADAM_EPS) + ADAM_WD * w), mn, vn


def _adamw_halves(name, core, w, m, v, mine, theirs):
    R, C = w.shape
    hr = mine.shape[0]
    tr = _row_tile(hr)
    nbh = hr // tr

    def kern(c_ref, w_ref, m_ref, v_ref, a_ref, b_ref, go_ref, d_ref, mo_ref, vo_ref):
        g = jnp.where(pl.program_id(0) // nbh == c_ref[0], a_ref[...], b_ref[...])
        d, mn, vn = _adam_update(w_ref[...], m_ref[...], v_ref[...], g)
        go_ref[...] = g
        d_ref[...] = d
        mo_ref[...] = mn
        vo_ref[...] = vn

    spec = pl.BlockSpec((tr, C), lambda i, s: (i, 0))
    hspec = pl.BlockSpec((tr, C), lambda i, s: (i % nbh, 0))
    return _pcall(
        kern, name=name, out_shape=[jax.ShapeDtypeStruct((R, C), F32)] * 4,
        grid_spec=pltpu.PrefetchScalarGridSpec(
            num_scalar_prefetch=1, grid=(R // tr,),
            in_specs=[spec, spec, spec, hspec, hspec], out_specs=[spec] * 4),
        compiler_params=_cparams(("parallel",)),
    )(core, w, m, v, mine, theirs)


def _adamw_halves_t(name, core, w_t, m_t, v_t, mine_t, theirs_t):
    C, R = w_t.shape
    hr = mine_t.shape[1]
    tc = min(256, hr)
    nbh = hr // tc

    def kern(c_ref, w_ref, m_ref, v_ref, a_ref, b_ref, go_ref, d_ref, mo_ref, vo_ref):
        g = jnp.where(pl.program_id(0) // nbh == c_ref[0], a_ref[...], b_ref[...])
        d, mn, vn = _adam_update(w_ref[...], m_ref[...], v_ref[...], g)
        go_ref[...] = g
        d_ref[...] = d
        mo_ref[...] = mn
        vo_ref[...] = vn

    spec = pl.BlockSpec((C, tc), lambda j, s: (0, j))
    hspec = pl.BlockSpec((C, tc), lambda j, s: (0, j % nbh))
    return _pcall(
        kern, name=name, out_shape=[jax.ShapeDtypeStruct((C, R), F32)] * 4,
        grid_spec=pltpu.PrefetchScalarGridSpec(
            num_scalar_prefetch=1, grid=(R // tc,),
            in_specs=[spec, spec, spec, hspec, hspec], out_specs=[spec] * 4),
        compiler_params=_cparams(("parallel",)),
    )(core, w_t, m_t, v_t, mine_t, theirs_t)


def _adamw_many(name, wmvg):
    n = len(wmvg[0])

    def kern(*refs):
        ins, outs = refs[:4 * n], refs[4 * n:]
        for j in range(n):
            g = ins[3 * n + j][...]
            d, mn, vn = _adam_update(ins[j][...], ins[n + j][...], ins[2 * n + j][...], g)
            for i, val in enumerate((g, d, mn, vn)):
                outs[i * n + j][...] = val

    vm = pl.BlockSpec(memory_space=pltpu.VMEM)
    flat = [a for group in wmvg for a in group]
    outs = _pcall(
        kern, name=name, in_specs=[vm] * (4 * n), out_specs=[vm] * (4 * n),
        out_shape=[jax.ShapeDtypeStruct(a.shape, F32) for _ in range(4) for a in wmvg[0]],
    )(*flat)
    return [outs[i * n:(i + 1) * n] for i in range(4)]


def _adamw(name, w, m, v, gparts, ride=None):
    R, C = w.shape
    K = gparts.shape[0]
    tr = _row_tile(R)
    nr = R // tr
    r_in, r_ispec, r_ospec, r_oshape, r_scratch, r_hook = _ride(ride, 4, 4)

    def kern(*refs):
        w_ref, m_ref, v_ref, g_ref = refs[:4]
        go_ref, d_ref, mo_ref, vo_ref = refs[4 + len(r_in):8 + len(r_in)]
        i = pl.program_id(0)
        before, after = r_hook(refs, i == 0, i == nr // 2, i == nr - 1)
        before()
        g = g_ref[0]
        for k in range(1, K):
            g = g + g_ref[k]
        d, mn, vn = _adam_update(w_ref[...], m_ref[...], v_ref[...], g)
        go_ref[...] = g
        d_ref[...] = d
        mo_ref[...] = mn
        vo_ref[...] = vn
        after()

    spec = pl.BlockSpec((tr, C), lambda i: (i, 0))
    return _pcall(
        kern, name=name, grid=(nr,),
        in_specs=[spec, spec, spec, pl.BlockSpec((K, tr, C), lambda i: (0, i, 0))] + r_ispec,
        out_specs=[spec] * 4 + r_ospec,
        out_shape=[jax.ShapeDtypeStruct((R, C), F32)] * 4 + r_oshape,
        scratch_shapes=r_scratch,
        compiler_params=_cparams(("arbitrary",) if ride else ("parallel",)),
    )(w, m, v, gparts, *r_in)


def _round_up(a, b):
    return (a + b - 1) // b * b


def kernel(x, c, w_ada, b_ada, norm1_g, w_in, b_forget, q_norm_g, k_norm_g, w_attn_proj, conv_w, conv_b, conv_ln_g, conv_ln_b, w_conv_proj, w_out, norm2_g, w_mlp1, w_mlp2, loss_target, m_w_ada, m_b_ada, m_norm1_g, m_w_in, m_b_forget, m_q_norm_g, m_k_norm_g, m_w_attn_proj, m_conv_w, m_conv_b, m_conv_ln_g, m_conv_ln_b, m_w_conv_proj, m_w_out, m_norm2_g, m_w_mlp1, m_w_mlp2, v_w_ada, v_b_ada, v_norm1_g, v_w_in, v_b_forget, v_q_norm_g, v_k_norm_g, v_w_attn_proj, v_conv_w, v_conv_b, v_conv_ln_g, v_conv_ln_b, v_w_conv_proj, v_w_out, v_norm2_g, v_w_mlp1, v_w_mlp2):
    S, D = x.shape[1], x.shape[2]
    NH, HD = b_forget.shape[-1], q_norm_g.shape[-1]
    TAPS = conv_w.shape[1]
    DIN_S = w_in.shape[-1]
    DIN = 4 * DIN_S
    DFF_S = w_mlp1.shape[-1]
    DFF = 4 * DFF_S
    ADA_S = w_ada.shape[-1]
    DS = w_attn_proj.shape[1]
    CS = conv_w.shape[-1]
    assert NH * HD == D and DIN == 7 * D + NH and TAPS - 1 <= HALO and D % LANES == 0 and 2 * HD == LANES
    NP = _round_up(7 * D + LANES, 512)
    FW = NP - 7 * D
    assert (7 * D) % FW == 0
    TQ = min(512, S)
    NQ = S // TQ
    FCOL = 7 * D // LANES

    xi, yi, ci = lax.axis_index("x"), lax.axis_index("y"), lax.axis_index("c")
    chip = 2 * xi + yi
    dev = 4 * xi + 2 * yi + ci

    x2 = x.reshape(S, D)
    tgt = loss_target.reshape(S, D)

    lane_head = jnp.arange(D, dtype=jnp.int32) // HD
    grp = (lane_head[:, None] == jnp.arange(LANES, dtype=jnp.int32)[None, :]).astype(BF16)
    grp_t = grp.T
    sel = ((jnp.arange(D, dtype=jnp.int32)[:, None] == HD * jnp.arange(LANES, dtype=jnp.int32)[None, :])
           .astype(BF16))
    ch = min(256, S)
    ii = jnp.arange(ch, dtype=jnp.int32)
    tri = (ii[None, :] <= ii[:, None]).astype(BF16)
    tri_u = tri.T
    gq_t = jnp.tile(q_norm_g.reshape(1, HD), (1, NH))
    gk_t = jnp.tile(k_norm_g.reshape(1, HD), (1, NH))
    bf_pad = jnp.pad(b_forget.reshape(1, NH), ((0, 0), (0, LANES - NH)))

    c_all, cw_all = _ag_small(
        "ag_c_convw", [c.reshape(1, D), jnp.pad(conv_w.reshape(TAPS, CS), ((0, HALO - TAPS), (0, 0)))])
    c_all = c_all.reshape(8, D)
    b_part = lax.dynamic_slice(b_ada.reshape(1, -1), (0, chip * ADA_S), (1, ADA_S))
    mod_part = _ada_fwd(c_all, w_ada.reshape(D, ADA_S), b_part)
    (mod_all,) = _ag_small("ag_mod", [mod_part])
    mod_full = jnp.concatenate([mod_all[0], mod_all[2], mod_all[4], mod_all[6]], axis=1)
    mod = lax.dynamic_slice(mod_full, (dev, 0), (1, 6 * D))
    sh1, sc1, g1, sh2, sc2, g2 = [mod[:, i * D:(i + 1) * D] for i in range(6)]

    shards = [w_in.reshape(D, DIN_S), w_attn_proj.reshape(DS, D), w_conv_proj.reshape(DS, D),
              w_out.reshape(DS, D), w_mlp1.reshape(D, DFF_S), w_mlp2.reshape(DFF_S, D)]
    shards = [s.astype(BF16) for s in shards]
    (gw_in,) = _exchange("ag_w_in", shards[:1], _gather_plan)
    w_conv = jnp.concatenate([cw_all[0], cw_all[2], cw_all[4], cw_all[6]], axis=1)

    SEGS = [(0, 2 * D, 0), (3 * D + NH, DIN, 2 * D), (2 * D, 3 * D + NH, 6 * D)]

    def pieces(a, b):
        out = []
        for k in range(4):
            lo, hi = max(a, k * DIN_S), min(b, (k + 1) * DIN_S)
            if lo < hi:
                out.append(gw_in[k][:, lo - k * DIN_S:hi - k * DIN_S])
        return out

    w_in_p = jnp.concatenate([p for (a, b, _) in SEGS for p in pieces(a, b)]
                             + [jnp.zeros((D, NP - 7 * D - NH), BF16)], axis=1)

    n1g = norm1_g.reshape(1, D)
    n2g = norm2_g.reshape(1, D)
    h = _norm_mod("norm_mod1", x2, n1g, sc1, sh1, S, D)
    proj = _mm("mm_in", h, w_in_p, "nn", [F32], tn=1536)
    qs, kn, vb = _qk_prep(proj, 6, gq_t, gk_t, grp, grp_t, S, D, HD)
    f_cum = _fgate_fwd(proj, FCOL, bf_pad, tri, S)
    fk_c = f_cum[:, :NH]
    fk_r = fk_c.T.reshape(NH, NQ, 1, TQ)
    o, o32, lse_b, gw_ap, gw_cp, gw_out, gw_m1, gw_m2 = _flash_fwd(
        qs, kn, vb, fk_r, S, D, HD, TQ, ride=(_gather_plan, shards[1:]))
    w_ap = gw_ap.reshape(D, D)
    w_cp = gw_cp.reshape(D, D)
    w_o = gw_out.reshape(D, D)
    w_m1 = jnp.transpose(gw_m1, (1, 0, 2)).reshape(D, DFF)
    w_m2 = gw_m2.reshape(DFF, D)
    br_a = _mm("mm_attn_proj", o, w_ap, "nn", [F32])
    cb, clg, clb = conv_b.reshape(1, D), conv_ln_g.reshape(1, D), conv_ln_b.reshape(1, D)
    u1, u3 = _conv_fwd(proj, 2, 3, w_conv, cb, clg, clb, S, D, TAPS, 256)
    br_b = _mm("mm_conv_proj", u3, w_cp, "nn", [F32])
    merged = _gate_merge(proj, 4, 5, br_a, br_b, S, D)
    mo = _mm("mm_out", merged, w_o, "nn", [F32])
    x1, h2 = _resid_norm2(x2, mo, g1, n2g, sc2, sh2, S, D)

    def relu2(r):
        rp = jnp.maximum(r, 0.0)
        return (rp * rp,)
    z = _mm("mm_mlp1", h2, w_m1, "nn", [BF16], epi=relu2)
    ml = _mm("mm_mlp2", z, w_m2, "nn", [F32])
    dy, dml, sq, dg2 = _loss_dy(x1, ml, tgt, g2, S, D)
    loss_part = jnp.full((1, LANES), 0.5 * jnp.sum(sq) / D, F32)

    da = _mm("mm_dz", dml, w_m2, "nt", [BF16], epi=lambda r, zz: (r * 2.0 * jnp.sqrt(zz.astype(F32)),),
             extras=(z,))
    dw_m2 = _mm("mm_dw_mlp2", z, dml, "tn", [BF16])
    dw_m1 = _mm("mm_dw_mlp1", h2, da, "tn", [BF16])
    dh2 = _mm("mm_dh2", da, w_m1, "nt", [F32])
    dx1, dmo, dsh2, dsc2, dn2g, dg1 = _norm_bwd("norm2_bwd", x1, dh2, dy, n2g, sc2, S, D, extra=(mo, g1))
    dmerged = _mm("mm_dmerged", dmo, w_o, "nt", [F32])
    dw_o = _mm("mm_dw_out", merged, dmo, "tn", [BF16])
    dproj, dba, dbb = _gate_bwd(dmerged, proj, 4, 5, br_a, br_b, S, D, into=(lax.empty((S, NP), BF16), 2))
    do = _mm("mm_do", dba, w_ap, "nt", [BF16])
    dw_ap = _mm("mm_dw_attn_proj", o, dba, "tn", [BF16])
    du3 = _mm("mm_du3", dbb, w_cp, "nt", [F32])
    dw_cp = _mm("mm_dw_conv_proj", u3, dbb, "tn", [BF16])
    core = ci.astype(jnp.int32).reshape(1)
    chip1 = chip.astype(jnp.int32).reshape(1)
    halves = lambda p: p.astype(BF16).reshape(4, 2, p.shape[1] // 2, p.shape[2])
    names = ["w_in", "w_attn_proj", "w_conv_proj", "w_out", "w_mlp1", "w_mlp2"]
    parts = [halves(p) for p in (dw_ap.reshape(4, DS, D), dw_cp.reshape(4, DS, D), dw_o.reshape(4, DS, D),
                                 jnp.transpose(dw_m1.reshape(D, 4, DFF_S), (1, 0, 2)), dw_m2.reshape(4, DFF_S, D))]
    dproj, dcw, dcb, dclg, dclb, *theirs = _conv_bwd(
        du3, u1, proj, 2, 3, w_conv, clg, clb, S, D, TAPS, 256, into=(dproj, 1), ride=(_pair_halves_plan, parts))
    chip_parts =[_sum_pair("sum_pair_" + nm, core, p, t) for nm, p, t in zip(names[1:], parts, theirs)]

    delta_c, lse_c = _delta_prep(do, o32, lse_b, grp, sel, S, D)
    to_rows = lambda t: t[:, :NH].T.reshape(NH, NQ, 1, TQ)
    dkn, dproj, dqs, dfq_r, dfk_b, *recvd = _flash_bwd(
        qs, kn, vb, do, f_cum, to_rows(lse_c), to_rows(delta_c), S, D, HD, TQ,
        dv_into=(dproj, 6 * D // LANES), ride=(_scatter_plan, chip_parts))
    dproj, sq_q, sq_k = _qk_bwd(proj, dqs, dkn, gq_t, gk_t, grp, grp_t, S, D, HD, into=(dproj, 0))
    to_cols = lambda r: jnp.pad(r.reshape(NH, S).T, ((0, 0), (0, LANES - NH)))
    dfq_pad = to_cols(dfq_r)
    dfk_pad = _pick_heads(dfk_b, sel, S, D)
    dproj, dbf = _fgate_bwd(dfk_pad, dfq_pad, proj, FCOL, bf_pad, tri_u, NH, S, FW, into=(dproj, 7 * D // FW))
    dw_in_p = _mm("mm_dw_in", h, dproj, "tn", [BF16])
    def shard_cols(k):
        out = []
        for (a, b, start) in sorted(SEGS):
            lo, hi = max(a, k * DIN_S), min(b, (k + 1) * DIN_S)
            if lo < hi:
                out.append(dw_in_p[:, start + lo - a:start + hi - a])
        return jnp.concatenate(out, axis=1)

    part_in = halves(jnp.stack([shard_cols(k) for k in range(4)]))
    (their_in,) = _exchange("rs_pair_w_in", [part_in], _pair_halves_plan)
    chip_in = _sum_pair("sum_pair_w_in", core, part_in, their_in)
    dh, recv_in = _mm("mm_dh", dproj, w_in_p, "nt", [F32], ride=(_scatter_plan, [chip_in]))
    gx, dsh1, dsc1, dn1g = _norm_bwd("norm1_bwd", x2, dh, dx1, n1g, sc1, S, D)

    packed = jnp.concatenate([dsh1, dsc1, dg1, dsh2, dsc2, dg2, dn1g, dcb, dclg, dclb, dn2g,
                              sq_q, sq_k, dbf, loss_part], axis=1)
    small_all, dcw_all = _ag_small("ag_small_grads", [packed, dcw])
    small = _sum_slots("sum_small", small_all.reshape(8, 1, -1)).reshape(1, -1)
    dmod_sum = small[:, :6 * D]
    seg = lambda k: small[:, (6 + k) * D:(7 + k) * D]
    g_n1g, g_cb, g_clg, g_clb, g_n2g = seg(0), seg(1), seg(2), seg(3), seg(4)
    g_qn = _sum_slots("sum_qn", seg(5).reshape(NH, 1, HD))
    g_kn = _sum_slots("sum_kn", seg(6).reshape(NH, 1, HD))
    g_bf = small[:, 13 * D:13 * D + NH]
    loss = small[0, 13 * D + LANES]
    dcw_mine = lax.dynamic_slice(dcw_all[:, :TAPS, :], (0, 0, chip * CS), (8, TAPS, CS))

    dmod_all = small_all.reshape(8, -1)[:, :6 * D]
    dmod_cols = lax.dynamic_slice(dmod_all, (0, chip * ADA_S), (8, ADA_S))
    c_t_pad = jnp.pad(c_all.T, ((0, 0), (0, LANES - 8)))
    g_wada = _ada_wgrad(c_t_pad, jnp.pad(dmod_cols, ((0, LANES - 8), (0, 0))))

    sums =[_sum_chips("sum_" + nm, chip1, p, r)
            for nm, p, r in zip(names, [chip_in] + chip_parts, [recv_in] + list(recvd))]

    res = {}
    outs = _adamw("adamw_w_ada", w_ada.reshape(D, ADA_S), m_w_ada.reshape(D, ADA_S),
                  v_w_ada.reshape(D, ADA_S), g_wada.reshape(1, D, ADA_S), ride=(_pair_swap_plan, sums))
    res["w_ada"] = [t.reshape(w_ada.shape) for t in outs[:4]]
    big = {nm: (a, b) for nm, a, b in zip(names, sums, outs[4:])}
    big_w = {"w_in": (w_in, m_w_in, v_w_in), "w_attn_proj": (w_attn_proj, m_w_attn_proj, v_w_attn_proj),
             "w_conv_proj": (w_conv_proj, m_w_conv_proj, v_w_conv_proj), "w_out": (w_out, m_w_out, v_w_out),
             "w_mlp1": (w_mlp1, m_w_mlp1, v_w_mlp1), "w_mlp2": (w_mlp2, m_w_mlp2, v_w_mlp2)}
    for nm in names:
        shp = big_w[nm][0].shape
        if shp[2] % LANES:
            outs = _adamw_halves_t("adamw_" + nm, core, *[t.reshape(shp[1], shp[2]).T for t in big_w[nm]],
                                   *[t.T for t in big[nm]])
            res[nm] = [t.T.reshape(shp) for t in outs]
        else:
            outs = _adamw_halves("adamw_" + nm, core, *[t.reshape(shp[1], shp[2]) for t in big_w[nm]], *big[nm])
            res[nm] = [t.reshape(shp) for t in outs]
    outs = _adamw("adamw_conv_w", conv_w.reshape(TAPS, CS), m_conv_w.reshape(TAPS, CS),
                  v_conv_w.reshape(TAPS, CS), dcw_mine)
    res["conv_w"] = [t.reshape(conv_w.shape) for t in outs]

    small_w = [("b_ada", b_ada, m_b_ada, v_b_ada, dmod_sum), ("norm1_g", norm1_g, m_norm1_g, v_norm1_g, g_n1g),
               ("b_forget", b_forget, m_b_forget, v_b_forget, g_bf),
               ("q_norm_g", q_norm_g, m_q_norm_g, v_q_norm_g, g_qn),
               ("k_norm_g", k_norm_g, m_k_norm_g, v_k_norm_g, g_kn),
               ("conv_b", conv_b, m_conv_b, v_conv_b, g_cb), ("conv_ln_g", conv_ln_g, m_conv_ln_g, v_conv_ln_g, g_clg),
               ("conv_ln_b", conv_ln_b, m_conv_ln_b, v_conv_ln_b, g_clb),
               ("norm2_g", norm2_g, m_norm2_g, v_norm2_g, g_n2g)]
    outs = _adamw_many("adamw_small", [[t[i].reshape(1, -1) for t in small_w] for i in (1, 2, 3, 4)])
    for j, (nm, w_, _, _, _) in enumerate(small_w):
        res[nm] = [outs[i][j].reshape(w_.shape) for i in range(4)]

    order = ["w_ada", "b_ada", "norm1_g", "w_in", "b_forget", "q_norm_g", "k_norm_g", "w_attn_proj", "conv_w",
             "conv_b", "conv_ln_g", "conv_ln_b", "w_conv_proj", "w_out", "norm2_g", "w_mlp1", "w_mlp2"]
    return (loss, gx.reshape(x.shape), *[res[n][0] for n in order], *[res[n][1] for n in order],
            *[res[n][2] for n in order], *[res[n][3] for n in order])
```
